```python
import jax, jax.numpy as jnp
from jax import lax
import numpy as np

D_MODEL = 2048
BATCH = 8
SEQ = 4096
DEPTH = 1

CHUNK = 64
A_HEADS = 16
A_HEAD_DIM = 64
A_WIDTH = A_HEADS * A_HEAD_DIM
A_PAST_CHUNKS = 8
A_BAND = (A_PAST_CHUNKS + 1) * CHUNK
REL_CLIP = 256
REL_SIZE = REL_CLIP + CHUNK
B_HEADS = 4
B_KEY_DIM = D_MODEL // 4
B_VAL_DIM = D_MODEL // 2
B_HK = B_KEY_DIM // B_HEADS
B_HV = B_VAL_DIM // B_HEADS
GATE_RANK = 16
GATE_TAU = 16.0
D_FF = 256 * ((8 * D_MODEL // 3 + 255) // 256)
N_MOD = 9
ALPHA = (2.0 * DEPTH) ** 0.25
BETA = (8.0 * DEPTH) ** -0.25
LN_EPS = 1e-5
RMS_EPS = 1e-6
SPLITS = (A_WIDTH, A_WIDTH, A_WIDTH,
          B_KEY_DIM, B_KEY_DIM, B_VAL_DIM,
          B_VAL_DIM, GATE_RANK,
          D_MODEL, D_MODEL)
SPLIT_POINTS = tuple(int(v) for v in np.cumsum(SPLITS)[:-1])
W_IN_COLS = sum(SPLITS)

kernel_name = "hybrid_chunk_attn_gla_macaron_deepnorm_adaln"


def layer_norm(x, g, b):
    xf = x.astype(jnp.float32)
    mu = jnp.mean(xf, axis=-1, keepdims=True)
    var = jnp.mean(jnp.square(xf - mu), axis=-1, keepdims=True)
    y = (xf - mu) * lax.rsqrt(var + LN_EPS)
    return (y * g.astype(jnp.float32) + b.astype(jnp.float32)).astype(x.dtype)


def modulate(x, shift, scale):
    return x * (1.0 + scale[:, None, :]) + shift[:, None, :]


def swiglu(u, w_in, w_out):
    a, b = jnp.split(u @ w_in, 2, axis=-1)
    return (jax.nn.silu(a) * b) @ w_out


def chunk_band_attention(q, k, v, rel_bias):
    bn, s, h, dh = q.shape
    nc = s // CHUNK
    pad = A_PAST_CHUNKS * CHUNK
    kp = jnp.pad(k, ((0, 0), (pad, 0), (0, 0), (0, 0)))
    vp = jnp.pad(v, ((0, 0), (pad, 0), (0, 0), (0, 0)))
    qi = jnp.arange(CHUNK)[:, None]
    ks = jnp.arange(A_BAND)[None, :]
    rel = ks - pad - qi
    idx = jnp.clip(rel, -REL_CLIP, CHUNK - 1) + REL_CLIP
    bias = rel_bias[:, idx].astype(jnp.float32)
    qc = q.reshape(bn, nc, CHUNK, h, dh).swapaxes(0, 1)
    scale = dh ** -0.5

    def one_chunk(args):
        n, qn = args
        start = n * CHUNK
        kb = lax.dynamic_slice_in_dim(kp, start, A_BAND, axis=1)
        vb = lax.dynamic_slice_in_dim(vp, start, A_BAND, axis=1)
        sc = jnp.einsum('bqhd,bkhd->bhqk', qn, kb).astype(jnp.float32) * scale + bias
        valid = (start - pad + jnp.arange(A_BAND)) >= 0
        sc = jnp.where(valid, sc, -jnp.inf)
        p = jax.nn.softmax(sc, axis=-1).astype(vb.dtype)
        return jnp.einsum('bhqk,bkhd->bqhd', p, vb)

    out = lax.map(one_chunk, (jnp.arange(nc), qc))
    return out.swapaxes(0, 1).reshape(bn, s, h * dh)


def gla_chunk_readout(q, k, v, log_a):
    bn, s, h, dk = q.shape
    dv = v.shape[-1]
    nc = s // CHUNK
    qc = q.reshape(bn, nc, CHUNK, h, dk).astype(jnp.float32)
    kc = k.reshape(bn, nc, CHUNK, h, dk).astype(jnp.float32)
    vc = v.reshape(bn, nc, CHUNK, h, dv).astype(jnp.float32)
    cum = jnp.cumsum(log_a.reshape(bn, nc, CHUNK, h, dk).astype(jnp.float32), axis=2)
    last = cum[:, :, -1:]
    kdec = kc * jnp.exp(last - cum)
    u = jnp.einsum('bnchk,bnchv->bnhkv', kdec, vc)
    chunk_decay = jnp.exp(last[:, :, 0])

    def step(state, xs):
        dec, un, qn = xs
        state = dec[..., None] * state + un
        return state, jnp.einsum('bchk,bhkv->bchv', qn, state)

    s0 = jnp.zeros((bn, h, dk, dv), jnp.float32)
    _, o = lax.scan(step, s0, (chunk_decay.swapaxes(0, 1), u.swapaxes(0, 1), qc.swapaxes(0, 1)))
    return o.swapaxes(0, 1).reshape(bn, s, h, dv)


def token_mix(u, w_mix_in, rel_bias, w_alpha2, b_alpha, gla_norm_g, w_proj_a, w_proj_b, w_mix_out):
    bn, s, _ = u.shape
    qa, ka, va, qb, kb, vb, rb, lr, ga, gb = jnp.split(u @ w_mix_in, SPLIT_POINTS, axis=-1)
    ya = chunk_band_attention(qa.reshape(bn, s, A_HEADS, A_HEAD_DIM),
                              ka.reshape(bn, s, A_HEADS, A_HEAD_DIM),
                              va.reshape(bn, s, A_HEADS, A_HEAD_DIM), rel_bias)
    log_a = jax.nn.log_sigmoid((lr @ w_alpha2 + b_alpha).astype(jnp.float32)) / GATE_TAU
    ob = gla_chunk_readout(qb.reshape(bn, s, B_HEADS, B_HK) * (B_HK ** -0.5),
                           kb.reshape(bn, s, B_HEADS, B_HK),
                           vb.reshape(bn, s, B_HEADS, B_HV),
                           log_a.reshape(bn, s, B_HEADS, B_HK))
    ob = ob * lax.rsqrt(jnp.mean(jnp.square(ob), axis=-1, keepdims=True) + RMS_EPS)
    ob = (ob * gla_norm_g.astype(jnp.float32)).astype(u.dtype).reshape(bn, s, B_VAL_DIM)
    yb = ob * jax.nn.silu(rb)
    merged = jax.nn.sigmoid(ga) * (ya @ w_proj_a) + jax.nn.sigmoid(gb) * (yb @ w_proj_b)
    return merged @ w_mix_out


def _fwd_setup_inputs(seed: int = 0) -> dict:
    key = jax.random.key(seed)
    ks = jax.random.split(key, 24)
    f32 = jnp.float32
    nrm = lambda k, shape, s: jax.random.normal(k, shape, f32) * s
    L, D = DEPTH, D_MODEL
    return {
        "x": nrm(ks[0], (BATCH, SEQ, D), 1.0),
        "c": nrm(ks[1], (BATCH, D), 1.0),
        "w_ada": nrm(ks[2], (L, D, N_MOD * D), 0.5 * D ** -0.5),
        "b_ada": nrm(ks[3], (L, N_MOD * D), 0.01),
        "ffn1_w_in": nrm(ks[4], (L, D, 2 * D_FF), D ** -0.5),
        "ffn1_w_out": nrm(ks[5], (L, D_FF, D), BETA * D_FF ** -0.5),
        "ln1_g": 1.0 + nrm(ks[6], (L, D), 0.02),
        "ln1_b": nrm(ks[7], (L, D), 0.02),
        "w_mix_in": nrm(ks[8], (L, D, W_IN_COLS), D ** -0.5),
        "rel_bias": nrm(ks[9], (L, A_HEADS, REL_SIZE), 0.5),
        "w_alpha2": nrm(ks[10], (L, GATE_RANK, B_KEY_DIM), GATE_RANK ** -0.5),
        "b_alpha": nrm(ks[11], (L, B_KEY_DIM), 0.1),
        "gla_norm_g": 1.0 + nrm(ks[12], (L, B_HV), 0.02),
        "w_proj_a": nrm(ks[13], (L, A_WIDTH, D), BETA * A_WIDTH ** -0.5),
        "w_proj_b": nrm(ks[14], (L, B_VAL_DIM, D), BETA * B_VAL_DIM ** -0.5),
        "w_mix_out": nrm(ks[15], (L, D, D), BETA * D ** -0.5),
        "ln2_g": 1.0 + nrm(ks[16], (L, D), 0.02),
        "ln2_b": nrm(ks[17], (L, D), 0.02),
        "ffn2_w_in": nrm(ks[18], (L, D, 2 * D_FF), D ** -0.5),
        "ffn2_w_out": nrm(ks[19], (L, D_FF, D), BETA * D_FF ** -0.5),
        "ln3_g": 1.0 + nrm(ks[20], (L, D), 0.02),
        "ln3_b": nrm(ks[21], (L, D), 0.02),
    }


def _fwd_reference(x, c, w_ada, b_ada, ffn1_w_in, ffn1_w_out, ln1_g, ln1_b, w_mix_in, rel_bias,
              w_alpha2, b_alpha, gla_norm_g, w_proj_a, w_proj_b, w_mix_out, ln2_g, ln2_b,
              ffn2_w_in, ffn2_w_out, ln3_g, ln3_b):
    h = x
    for l in range(DEPTH):
        mod = jax.nn.silu(c) @ w_ada[l] + b_ada[l]
        sh1, sc1, g1, sh2, sc2, g2, sh3, sc3, g3 = jnp.split(mod, N_MOD, axis=-1)
        f1 = swiglu(modulate(h, sh1, sc1), ffn1_w_in[l], ffn1_w_out[l])
        h = layer_norm(ALPHA * h + 0.5 * g1[:, None, :] * f1, ln1_g[l], ln1_b[l])
        m = token_mix(modulate(h, sh2, sc2), w_mix_in[l], rel_bias[l], w_alpha2[l], b_alpha[l],
                      gla_norm_g[l], w_proj_a[l], w_proj_b[l], w_mix_out[l])
        h = layer_norm(ALPHA * h + g2[:, None, :] * m, ln2_g[l], ln2_b[l])
        f2 = swiglu(modulate(h, sh3, sc3), ffn2_w_in[l], ffn2_w_out[l])
        h = layer_norm(ALPHA * h + 0.5 * g3[:, None, :] * f2, ln3_g[l], ln3_b[l])
    return h


import jax as _jax
import jax.numpy as _jnp

TWIN_FORMAT = 'train_step'
FWD_PARAMS = ['x', 'c', 'w_ada', 'b_ada', 'ffn1_w_in', 'ffn1_w_out', 'ln1_g', 'ln1_b', 'w_mix_in', 'rel_bias', 'w_alpha2', 'b_alpha', 'gla_norm_g', 'w_proj_a', 'w_proj_b', 'w_mix_out', 'ln2_g', 'ln2_b', 'ffn2_w_in', 'ffn2_w_out', 'ln3_g', 'ln3_b']
TWIN_WEIGHTS = ['w_ada', 'b_ada', 'ffn1_w_in', 'ffn1_w_out', 'ln1_g', 'ln1_b', 'w_mix_in', 'rel_bias', 'w_alpha2', 'b_alpha', 'gla_norm_g', 'w_proj_a', 'w_proj_b', 'w_mix_out', 'ln2_g', 'ln2_b', 'ffn2_w_in', 'ffn2_w_out', 'ln3_g', 'ln3_b']
TWIN_DIFF_INPUT = 'x'
TWIN_INPUTS = ['x', 'c', 'w_ada', 'b_ada', 'ffn1_w_in', 'ffn1_w_out', 'ln1_g', 'ln1_b', 'w_mix_in', 'rel_bias', 'w_alpha2', 'b_alpha', 'gla_norm_g', 'w_proj_a', 'w_proj_b', 'w_mix_out', 'ln2_g', 'ln2_b', 'ffn2_w_in', 'ffn2_w_out', 'ln3_g', 'ln3_b', 'loss_target', 'm_w_ada', 'm_b_ada', 'm_ffn1_w_in', 'm_ffn1_w_out', 'm_ln1_g', 'm_ln1_b', 'm_w_mix_in', 'm_rel_bias', 'm_w_alpha2', 'm_b_alpha', 'm_gla_norm_g', 'm_w_proj_a', 'm_w_proj_b', 'm_w_mix_out', 'm_ln2_g', 'm_ln2_b', 'm_ffn2_w_in', 'm_ffn2_w_out', 'm_ln3_g', 'm_ln3_b', 'v_w_ada', 'v_b_ada', 'v_ffn1_w_in', 'v_ffn1_w_out', 'v_ln1_g', 'v_ln1_b', 'v_w_mix_in', 'v_rel_bias', 'v_w_alpha2', 'v_b_alpha', 'v_gla_norm_g', 'v_w_proj_a', 'v_w_proj_b', 'v_w_mix_out', 'v_ln2_g', 'v_ln2_b', 'v_ffn2_w_in', 'v_ffn2_w_out', 'v_ln3_g', 'v_ln3_b']
TWIN_OUTPUTS = ['loss', 'grad_x', 'grad_w_ada', 'grad_b_ada', 'grad_ffn1_w_in', 'grad_ffn1_w_out', 'grad_ln1_g', 'grad_ln1_b', 'grad_w_mix_in', 'grad_rel_bias', 'grad_w_alpha2', 'grad_b_alpha', 'grad_gla_norm_g', 'grad_w_proj_a', 'grad_w_proj_b', 'grad_w_mix_out', 'grad_ln2_g', 'grad_ln2_b', 'grad_ffn2_w_in', 'grad_ffn2_w_out', 'grad_ln3_g', 'grad_ln3_b', 'delta_w_ada', 'delta_b_ada', 'delta_ffn1_w_in', 'delta_ffn1_w_out', 'delta_ln1_g', 'delta_ln1_b', 'delta_w_mix_in', 'delta_rel_bias', 'delta_w_alpha2', 'delta_b_alpha', 'delta_gla_norm_g', 'delta_w_proj_a', 'delta_w_proj_b', 'delta_w_mix_out', 'delta_ln2_g', 'delta_ln2_b', 'delta_ffn2_w_in', 'delta_ffn2_w_out', 'delta_ln3_g', 'delta_ln3_b', 'new_m_w_ada', 'new_m_b_ada', 'new_m_ffn1_w_in', 'new_m_ffn1_w_out', 'new_m_ln1_g', 'new_m_ln1_b', 'new_m_w_mix_in', 'new_m_rel_bias', 'new_m_w_alpha2', 'new_m_b_alpha', 'new_m_gla_norm_g', 'new_m_w_proj_a', 'new_m_w_proj_b', 'new_m_w_mix_out', 'new_m_ln2_g', 'new_m_ln2_b', 'new_m_ffn2_w_in', 'new_m_ffn2_w_out', 'new_m_ln3_g', 'new_m_ln3_b', 'new_v_w_ada', 'new_v_b_ada', 'new_v_ffn1_w_in', 'new_v_ffn1_w_out', 'new_v_ln1_g', 'new_v_ln1_b', 'new_v_w_mix_in', 'new_v_rel_bias', 'new_v_w_alpha2', 'new_v_b_alpha', 'new_v_gla_norm_g', 'new_v_w_proj_a', 'new_v_w_proj_b', 'new_v_w_mix_out', 'new_v_ln2_g', 'new_v_ln2_b', 'new_v_ffn2_w_in', 'new_v_ffn2_w_out', 'new_v_ln3_g', 'new_v_ln3_b']
TWIN_LEAF_KINDS = {'loss': 'loss', 'grad_x': 'grad_x', 'grad_w_ada': 'grad_w', 'grad_b_ada': 'grad_w', 'grad_ffn1_w_in': 'grad_w', 'grad_ffn1_w_out': 'grad_w', 'grad_ln1_g': 'grad_w', 'grad_ln1_b': 'grad_w', 'grad_w_mix_in': 'grad_w', 'grad_rel_bias': 'grad_w', 'grad_w_alpha2': 'grad_w', 'grad_b_alpha': 'grad_w', 'grad_gla_norm_g': 'grad_w', 'grad_w_proj_a': 'grad_w', 'grad_w_proj_b': 'grad_w', 'grad_w_mix_out': 'grad_w', 'grad_ln2_g': 'grad_w', 'grad_ln2_b': 'grad_w', 'grad_ffn2_w_in': 'grad_w', 'grad_ffn2_w_out': 'grad_w', 'grad_ln3_g': 'grad_w', 'grad_ln3_b': 'grad_w', 'delta_w_ada': 'delta_w', 'delta_b_ada': 'delta_w', 'delta_ffn1_w_in': 'delta_w', 'delta_ffn1_w_out': 'delta_w', 'delta_ln1_g': 'delta_w', 'delta_ln1_b': 'delta_w', 'delta_w_mix_in': 'delta_w', 'delta_rel_bias': 'delta_w', 'delta_w_alpha2': 'delta_w', 'delta_b_alpha': 'delta_w', 'delta_gla_norm_g': 'delta_w', 'delta_w_proj_a': 'delta_w', 'delta_w_proj_b': 'delta_w', 'delta_w_mix_out': 'delta_w', 'delta_ln2_g': 'delta_w', 'delta_ln2_b': 'delta_w', 'delta_ffn2_w_in': 'delta_w', 'delta_ffn2_w_out': 'delta_w', 'delta_ln3_g': 'delta_w', 'delta_ln3_b': 'delta_w', 'new_m_w_ada': 'new_m', 'new_m_b_ada': 'new_m', 'new_m_ffn1_w_in': 'new_m', 'new_m_ffn1_w_out': 'new_m', 'new_m_ln1_g': 'new_m', 'new_m_ln1_b': 'new_m', 'new_m_w_mix_in': 'new_m', 'new_m_rel_bias': 'new_m', 'new_m_w_alpha2': 'new_m', 'new_m_b_alpha': 'new_m', 'new_m_gla_norm_g': 'new_m', 'new_m_w_proj_a': 'new_m', 'new_m_w_proj_b': 'new_m', 'new_m_w_mix_out': 'new_m', 'new_m_ln2_g': 'new_m', 'new_m_ln2_b': 'new_m', 'new_m_ffn2_w_in': 'new_m', 'new_m_ffn2_w_out': 'new_m', 'new_m_ln3_g': 'new_m', 'new_m_ln3_b': 'new_m', 'new_v_w_ada': 'new_v', 'new_v_b_ada': 'new_v', 'new_v_ffn1_w_in': 'new_v', 'new_v_ffn1_w_out': 'new_v', 'new_v_ln1_g': 'new_v', 'new_v_ln1_b': 'new_v', 'new_v_w_mix_in': 'new_v', 'new_v_rel_bias': 'new_v', 'new_v_w_alpha2': 'new_v', 'new_v_b_alpha': 'new_v', 'new_v_gla_norm_g': 'new_v', 'new_v_w_proj_a': 'new_v', 'new_v_w_proj_b': 'new_v', 'new_v_w_mix_out': 'new_v', 'new_v_ln2_g': 'new_v', 'new_v_ln2_b': 'new_v', 'new_v_ffn2_w_in': 'new_v', 'new_v_ffn2_w_out': 'new_v', 'new_v_ln3_g': 'new_v', 'new_v_ln3_b': 'new_v'}


def _forward(args):
    return _fwd_reference(*[args[k] for k in FWD_PARAMS])


def _output_shape():
    def fwd():
        inp = _fwd_setup_inputs(0)
        return _fwd_reference(*[inp[k] for k in FWD_PARAMS])
    out = _jax.eval_shape(fwd)
    return out.shape, out.dtype

N_MICROBATCH = 1
ADAM_LR = 0.001
ADAM_B1 = 0.9
ADAM_B2 = 0.999
ADAM_EPS = 1e-08
ADAM_WD = 0.01
ADAM_STEP = 10
PER_EXAMPLE_BATCH_AXIS = {'x': 0, 'c': 0, 'loss_target': 0}
SHARED_INPUTS = []
_WEIGHT_DTYPES = {'w_ada': _jnp.float32, 'b_ada': _jnp.float32, 'ffn1_w_in': _jnp.float32, 'ffn1_w_out': _jnp.float32, 'ln1_g': _jnp.float32, 'ln1_b': _jnp.float32, 'w_mix_in': _jnp.float32, 'rel_bias': _jnp.float32, 'w_alpha2': _jnp.float32, 'b_alpha': _jnp.float32, 'gla_norm_g': _jnp.float32, 'w_proj_a': _jnp.float32, 'w_proj_b': _jnp.float32, 'w_mix_out': _jnp.float32, 'ln2_g': _jnp.float32, 'ln2_b': _jnp.float32, 'ffn2_w_in': _jnp.float32, 'ffn2_w_out': _jnp.float32, 'ln3_g': _jnp.float32, 'ln3_b': _jnp.float32}
MOMENT_SCALE = {'w_ada': 5.981393e-03, 'b_ada': 1.022265e-02, 'ffn1_w_in': 2.925728e-03, 'ffn1_w_out': 8.041312e-03, 'ln1_g': 5.882180e-01, 'ln1_b': 2.807624e-01, 'w_mix_in': 2.868568e-03, 'rel_bias': 4.658886e-04, 'w_alpha2': 1.096884e-03, 'b_alpha': 3.447814e-03, 'gla_norm_g': 9.612469e-03, 'w_proj_a': 2.801372e-03, 'w_proj_b': 4.813061e-03, 'w_mix_out': 5.568987e-03, 'ln2_g': 5.889148e-01, 'ln2_b': 2.821737e-01, 'ffn2_w_in': 2.925654e-03, 'ffn2_w_out': 8.038342e-03, 'ln3_g': 1.602918e+01, 'ln3_b': 4.852971e-01}


def _to_microbatches(a, axis):
    t = _jnp.moveaxis(a, axis, 0)
    t = t.reshape((N_MICROBATCH, t.shape[0] // N_MICROBATCH) + t.shape[1:])
    return _jnp.moveaxis(t, 1, axis + 1)


def setup_inputs(seed: int = 0) -> dict:
    inp = _fwd_setup_inputs(seed)
    key = _jax.random.fold_in(_jax.random.key(seed), 7919)
    shape, _ = _output_shape()
    out = dict(inp)
    out["loss_target"] = _jax.random.normal(_jax.random.fold_in(key, 0), shape, _jnp.float32)
    for i, name in enumerate(TWIN_WEIGHTS):
        w = inp[name].astype(_jnp.float32)
        if MOMENT_SCALE is None:
            s = _jnp.sqrt(_jnp.mean(_jnp.square(w)) + 1e-30)
        else:
            s = MOMENT_SCALE[name]
        km, kv = _jax.random.split(_jax.random.fold_in(key, i + 1))
        out[name] = w
        out["m_" + name] = s * _jax.random.normal(km, w.shape, _jnp.float32)
        out["v_" + name] = (s * s) * _jax.random.uniform(kv, w.shape, _jnp.float32, 0.5, 1.5)
    if N_MICROBATCH > 1:
        for name, axis in PER_EXAMPLE_BATCH_AXIS.items():
            out[name] = _to_microbatches(out[name], axis)
    return {'x': out['x'], 'c': out['c'], 'w_ada': out['w_ada'], 'b_ada': out['b_ada'], 'ffn1_w_in': out['ffn1_w_in'], 'ffn1_w_out': out['ffn1_w_out'], 'ln1_g': out['ln1_g'], 'ln1_b': out['ln1_b'], 'w_mix_in': out['w_mix_in'], 'rel_bias': out['rel_bias'], 'w_alpha2': out['w_alpha2'], 'b_alpha': out['b_alpha'], 'gla_norm_g': out['gla_norm_g'], 'w_proj_a': out['w_proj_a'], 'w_proj_b': out['w_proj_b'], 'w_mix_out': out['w_mix_out'], 'ln2_g': out['ln2_g'], 'ln2_b': out['ln2_b'], 'ffn2_w_in': out['ffn2_w_in'], 'ffn2_w_out': out['ffn2_w_out'], 'ln3_g': out['ln3_g'], 'ln3_b': out['ln3_b'], 'loss_target': out['loss_target'], 'm_w_ada': out['m_w_ada'], 'm_b_ada': out['m_b_ada'], 'm_ffn1_w_in': out['m_ffn1_w_in'], 'm_ffn1_w_out': out['m_ffn1_w_out'], 'm_ln1_g': out['m_ln1_g'], 'm_ln1_b': out['m_ln1_b'], 'm_w_mix_in': out['m_w_mix_in'], 'm_rel_bias': out['m_rel_bias'], 'm_w_alpha2': out['m_w_alpha2'], 'm_b_alpha': out['m_b_alpha'], 'm_gla_norm_g': out['m_gla_norm_g'], 'm_w_proj_a': out['m_w_proj_a'], 'm_w_proj_b': out['m_w_proj_b'], 'm_w_mix_out': out['m_w_mix_out'], 'm_ln2_g': out['m_ln2_g'], 'm_ln2_b': out['m_ln2_b'], 'm_ffn2_w_in': out['m_ffn2_w_in'], 'm_ffn2_w_out': out['m_ffn2_w_out'], 'm_ln3_g': out['m_ln3_g'], 'm_ln3_b': out['m_ln3_b'], 'v_w_ada': out['v_w_ada'], 'v_b_ada': out['v_b_ada'], 'v_ffn1_w_in': out['v_ffn1_w_in'], 'v_ffn1_w_out': out['v_ffn1_w_out'], 'v_ln1_g': out['v_ln1_g'], 'v_ln1_b': out['v_ln1_b'], 'v_w_mix_in': out['v_w_mix_in'], 'v_rel_bias': out['v_rel_bias'], 'v_w_alpha2': out['v_w_alpha2'], 'v_b_alpha': out['v_b_alpha'], 'v_gla_norm_g': out['v_gla_norm_g'], 'v_w_proj_a': out['v_w_proj_a'], 'v_w_proj_b': out['v_w_proj_b'], 'v_w_mix_out': out['v_w_mix_out'], 'v_ln2_g': out['v_ln2_g'], 'v_ln2_b': out['v_ln2_b'], 'v_ffn2_w_in': out['v_ffn2_w_in'], 'v_ffn2_w_out': out['v_ffn2_w_out'], 'v_ln3_g': out['v_ln3_g'], 'v_ln3_b': out['v_ln3_b']}


def _loss(weights, diff, rest, loss_target):
    with _jax.named_scope("forward"):
        args = {**rest, TWIN_DIFF_INPUT: diff, **{k: w.astype(_WEIGHT_DTYPES[k]) for k, w in weights.items()}}
        y = _forward(args)
    with _jax.named_scope("loss_head"):
        err = _jnp.square(y.astype(_jnp.float32) - loss_target)
        return 0.5 * _jnp.sum(_jnp.mean(err, axis=-1)) if err.ndim else 0.5 * err


def _adamw(w, g, m, v):
    m = ADAM_B1 * m + (1.0 - ADAM_B1) * g
    v = ADAM_B2 * v + (1.0 - ADAM_B2) * _jnp.square(g)
    m_hat = m / (1.0 - ADAM_B1 ** ADAM_STEP)
    v_hat = v / (1.0 - ADAM_B2 ** ADAM_STEP)
    delta = -ADAM_LR * (m_hat / (_jnp.sqrt(v_hat) + ADAM_EPS) + ADAM_WD * w)
    return delta, m, v


def reference(x, c, w_ada, b_ada, ffn1_w_in, ffn1_w_out, ln1_g, ln1_b, w_mix_in, rel_bias, w_alpha2, b_alpha, gla_norm_g, w_proj_a, w_proj_b, w_mix_out, ln2_g, ln2_b, ffn2_w_in, ffn2_w_out, ln3_g, ln3_b, loss_target, m_w_ada, m_b_ada, m_ffn1_w_in, m_ffn1_w_out, m_ln1_g, m_ln1_b, m_w_mix_in, m_rel_bias, m_w_alpha2, m_b_alpha, m_gla_norm_g, m_w_proj_a, m_w_proj_b, m_w_mix_out, m_ln2_g, m_ln2_b, m_ffn2_w_in, m_ffn2_w_out, m_ln3_g, m_ln3_b, v_w_ada, v_b_ada, v_ffn1_w_in, v_ffn1_w_out, v_ln1_g, v_ln1_b, v_w_mix_in, v_rel_bias, v_w_alpha2, v_b_alpha, v_gla_norm_g, v_w_proj_a, v_w_proj_b, v_w_mix_out, v_ln2_g, v_ln2_b, v_ffn2_w_in, v_ffn2_w_out, v_ln3_g, v_ln3_b):
    given = dict(x=x, c=c, w_ada=w_ada, b_ada=b_ada, ffn1_w_in=ffn1_w_in, ffn1_w_out=ffn1_w_out, ln1_g=ln1_g, ln1_b=ln1_b, w_mix_in=w_mix_in, rel_bias=rel_bias, w_alpha2=w_alpha2, b_alpha=b_alpha, gla_norm_g=gla_norm_g, w_proj_a=w_proj_a, w_proj_b=w_proj_b, w_mix_out=w_mix_out, ln2_g=ln2_g, ln2_b=ln2_b, ffn2_w_in=ffn2_w_in, ffn2_w_out=ffn2_w_out, ln3_g=ln3_g, ln3_b=ln3_b, loss_target=loss_target, m_w_ada=m_w_ada, m_b_ada=m_b_ada, m_ffn1_w_in=m_ffn1_w_in, m_ffn1_w_out=m_ffn1_w_out, m_ln1_g=m_ln1_g, m_ln1_b=m_ln1_b, m_w_mix_in=m_w_mix_in, m_rel_bias=m_rel_bias, m_w_alpha2=m_w_alpha2, m_b_alpha=m_b_alpha, m_gla_norm_g=m_gla_norm_g, m_w_proj_a=m_w_proj_a, m_w_proj_b=m_w_proj_b, m_w_mix_out=m_w_mix_out, m_ln2_g=m_ln2_g, m_ln2_b=m_ln2_b, m_ffn2_w_in=m_ffn2_w_in, m_ffn2_w_out=m_ffn2_w_out, m_ln3_g=m_ln3_g, m_ln3_b=m_ln3_b, v_w_ada=v_w_ada, v_b_ada=v_b_ada, v_ffn1_w_in=v_ffn1_w_in, v_ffn1_w_out=v_ffn1_w_out, v_ln1_g=v_ln1_g, v_ln1_b=v_ln1_b, v_w_mix_in=v_w_mix_in, v_rel_bias=v_rel_bias, v_w_alpha2=v_w_alpha2, v_b_alpha=v_b_alpha, v_gla_norm_g=v_gla_norm_g, v_w_proj_a=v_w_proj_a, v_w_proj_b=v_w_proj_b, v_w_mix_out=v_w_mix_out, v_ln2_g=v_ln2_g, v_ln2_b=v_ln2_b, v_ffn2_w_in=v_ffn2_w_in, v_ffn2_w_out=v_ffn2_w_out, v_ln3_g=v_ln3_g, v_ln3_b=v_ln3_b)
    weights = {n: given[n] for n in TWIN_WEIGHTS}
    shared = {n: given[n] for n in SHARED_INPUTS}
    per_example = {n: given[n] for n in ['x', 'c']}
    grad_fn = _jax.value_and_grad(_loss, argnums=(0, 1))

    def one_microbatch(ex, loss_target):
        ex = dict(ex)
        diff = ex.pop(TWIN_DIFF_INPUT)
        return grad_fn(weights, diff, {**shared, **ex}, loss_target)

    if N_MICROBATCH == 1:
        loss, (grad_w, grad_x) = one_microbatch(per_example, given["loss_target"])
    else:
        def body(carry, xs):
            loss_sum, grad_sum = carry
            l_k, (gw_k, gx_k) = one_microbatch(xs[0], xs[1])
            with _jax.named_scope("update"):
                return (loss_sum + l_k, _jax.tree.map(_jnp.add, grad_sum, gw_k)), gx_k

        init = (_jnp.zeros((), _jnp.float32), _jax.tree.map(_jnp.zeros_like, weights))
        (loss, grad_w), grad_x = _jax.lax.scan(body, init, (per_example, given["loss_target"]))
    with _jax.named_scope("update"):
        delta_w, new_m, new_v = {}, {}, {}
        for n in TWIN_WEIGHTS:
            delta_w[n], new_m[n], new_v[n] = _adamw(weights[n], grad_w[n], given["m_" + n], given["v_" + n])
    return (loss, grad_x, *[grad_w[n] for n in TWIN_WEIGHTS], *[delta_w[n] for n in TWIN_WEIGHTS],
            *[new_m[n] for n in TWIN_WEIGHTS], *[new_v[n] for n in TWIN_WEIGHTS])
```

```python
import functools

import numpy as np
import jax
import jax.numpy as jnp
from jax import lax
from jax.experimental import pallas as pl
from jax.experimental.pallas import tpu as pltpu

F32 = jnp.float32
BF16 = jnp.bfloat16
MESH = pl.DeviceIdType.MESH
N_DEV = 8
N_CHIP = 4

CHUNK = 64
A_PAST_CHUNKS = 8
REL_CLIP = 256
GATE_TAU = 16.0
N_MOD = 9
DEPTH = 1
ALPHA = (2.0 * DEPTH) ** 0.25
LN_EPS = 1e-5
RMS_EPS = 1e-6
ADAM_LR = 0.001
ADAM_B1 = 0.9
ADAM_B2 = 0.999
ADAM_EPS = 1e-08
ADAM_WD = 0.01
ADAM_STEP = 10

LANE = 128
VMEM_LIMIT = 56 * 2 ** 20
QB = 4 * CHUNK
KW = 3 * QB
GB = 8 * CHUNK
NEG = -1e30
HI = lax.Precision.HIGHEST

ANY = pl.BlockSpec(memory_space=pl.ANY)
VMEM_SPEC = pl.BlockSpec(memory_space=pltpu.VMEM)


def _params(*sem):
    return pltpu.CompilerParams(dimension_semantics=sem, vmem_limit_bytes=VMEM_LIMIT)


def _sds(shape, dtype):
    return jax.ShapeDtypeStruct(shape, dtype)


def _dot(a, b):
    return jnp.dot(a, b, preferred_element_type=F32)


def _dot_nt(a, b):
    return lax.dot_general(a, b, (((1,), (1,)), ((), ())), preferred_element_type=F32)


def _dot_tn(a, b):
    return lax.dot_general(a, b, (((0,), (0,)), ((), ())), preferred_element_type=F32)


def _sigmoid(x):
    return 1.0 / (1.0 + jnp.exp(-x))


def _colsum(x):
    return jnp.sum(x, axis=0, keepdims=True)


def _row_tile(rows, cap, mult):
    for t in range(min(rows, cap), 0, -1):
        if rows % t == 0 and t % mult == 0:
            return t
    return rows


def _me():
    return lax.axis_index("x"), lax.axis_index("y"), lax.axis_index("c")


def _flip(me, k):
    return tuple((1 - p) if (k >> s) & 1 else p for p, s in zip(me, (2, 1, 0)))


def _lin(p):
    return 4 * p[0] + 2 * p[1] + p[2]


def all_gather_small(x, name):
    r, n = x.shape

    def body(x_ref, out_ref, send_sems, recv_sems, local_sem):
        me = _me()
        mine = pltpu.make_async_copy(x_ref, out_ref.at[_lin(me)], local_sem)
        mine.start()
        sends = []
        for k in range(1, N_DEV):
            cp = pltpu.make_async_remote_copy(
                src_ref=x_ref, dst_ref=out_ref.at[_lin(me)], send_sem=send_sems.at[k - 1],
                recv_sem=recv_sems.at[k - 1], device_id=_flip(me, k), device_id_type=MESH)
            cp.start()
            sends.append(cp)
        for k in range(1, N_DEV):
            peer = _flip(me, k)
            pltpu.make_async_remote_copy(
                src_ref=x_ref, dst_ref=out_ref.at[_lin(peer)], send_sem=send_sems.at[k - 1],
                recv_sem=recv_sems.at[k - 1], device_id=peer, device_id_type=MESH).wait_recv()
        for cp in sends:
            cp.wait_send()
        mine.wait()

    return pl.pallas_call(
        body, name=name, out_shape=_sds((N_DEV, r, n), x.dtype),
        in_specs=[VMEM_SPEC], out_specs=VMEM_SPEC,
        scratch_shapes=[pltpu.SemaphoreType.DMA((N_DEV - 1,)), pltpu.SemaphoreType.DMA((N_DEV - 1,)),
                        pltpu.SemaphoreType.DMA],
    )(x)


def all_gather_big(shards, name):
    n = len(shards)
    per = N_DEV - 1

    def body(*refs):
        ins, outs = refs[:n], refs[n:2 * n]
        send_sems, recv_sems, local_sems = refs[2 * n:]
        me = _me()
        mx, my, mc = me
        sibling = (mx, my, 1 - mc)
        chips = [(1 - mx, my), (mx, 1 - my), (1 - mx, 1 - my)]

        def copy(a, k, block, to, src=None):
            dst = outs[a].at[_lin(block)]
            return pltpu.make_async_remote_copy(
                src_ref=dst if src is None else src, dst_ref=dst, send_sem=send_sems.at[a * per + k],
                recv_sem=recv_sems.at[a * per + k], device_id=to, device_id_type=MESH)

        mines = [pltpu.make_async_copy(ins[a], outs[a].at[_lin(me)], local_sems.at[a]) for a in range(n)]
        for m in mines:
            m.start()
        first = []
        for a in range(n):
            first += [copy(a, 1 + j, me, (*chip, mc), src=ins[a]) for j, chip in enumerate(chips)]
            first.append(copy(a, 0, me, sibling, src=ins[a]))
        for cp in first:
            cp.start()
        passed = []
        for a in range(n):
            for j, chip in enumerate(chips):
                copy(a, 1 + j, (*chip, mc), me).wait_recv()
                p = copy(a, 4 + j, (*chip, mc), sibling)
                p.start()
                passed.append(p)
        for a in range(n):
            copy(a, 0, sibling, me).wait_recv()
            for j, chip in enumerate(chips):
                copy(a, 4 + j, (*chip, 1 - mc), me).wait_recv()
        for cp in first + passed:
            cp.wait_send()
        for m in mines:
            m.wait()

    return pl.pallas_call(
        body, name=name, out_shape=[_sds((N_DEV,) + s.shape, s.dtype) for s in shards],
        in_specs=[ANY] * n, out_specs=[ANY] * n,
        scratch_shapes=[pltpu.SemaphoreType.DMA((n * per,)), pltpu.SemaphoreType.DMA((n * per,)),
                        pltpu.SemaphoreType.DMA((n,))],
    )(*shards)


def pair_exchange(gs, name):
    n = len(gs)

    def body(*refs):
        ins, outs = refs[:n], refs[n:2 * n]
        send_sems, recv_sems = refs[2 * n:]
        mx, my, mc = _me()
        cps = []
        for a in range(n):
            for q in range(N_CHIP):
                cp = pltpu.make_async_remote_copy(
                    src_ref=ins[a].at[2 * q + (1 - mc)], dst_ref=outs[a].at[q],
                    send_sem=send_sems.at[a * N_CHIP + q], recv_sem=recv_sems.at[a * N_CHIP + q],
                    device_id=(mx, my, 1 - mc), device_id_type=MESH)
                cp.start()
                cps.append(cp)
        for cp in cps:
            cp.wait()

    return pl.pallas_call(
        body, name=name, out_shape=[_sds((N_CHIP,) + g.shape[1:], g.dtype) for g in gs],
        in_specs=[ANY] * n, out_specs=[ANY] * n,
        scratch_shapes=[pltpu.SemaphoreType.DMA((n * N_CHIP,)), pltpu.SemaphoreType.DMA((n * N_CHIP,))],
    )(*gs)


def pair_add(core, g, got, name):
    _, rows, cols = g.shape
    tr = _row_tile(rows, 512, 16)

    def body(core_ref, g_ref, got_ref, h_ref):
        h_ref[...] = (g_ref[...].astype(F32) + got_ref[...].astype(F32)).astype(h_ref.dtype)

    return pl.pallas_call(
        body, name=name, out_shape=_sds((N_CHIP, rows, cols), g.dtype),
        grid_spec=pltpu.PrefetchScalarGridSpec(
            num_scalar_prefetch=1, grid=(N_CHIP, rows // tr),
            in_specs=[pl.BlockSpec((None, tr, cols), lambda q, i, c: (2 * q + c[0], i, 0)),
                      pl.BlockSpec((None, tr, cols), lambda q, i, c: (q, i, 0))],
            out_specs=pl.BlockSpec((None, tr, cols), lambda q, i, c: (q, i, 0))),
        compiler_params=_params("parallel", "parallel"),
    )(core, g, got)


def chip_exchange(hs, name):
    n = len(hs)
    rel = N_CHIP - 1

    def body(*refs):
        ins, outs = refs[:n], refs[n:2 * n]
        send_sems, recv_sems = refs[2 * n:]
        mx, my, mc = _me()
        chips = [(1 - mx, my), (mx, 1 - my), (1 - mx, 1 - my)]
        cps = []
        for a in range(n):
            for k, (px, py) in enumerate(chips):
                cp = pltpu.make_async_remote_copy(
                    src_ref=ins[a].at[2 * px + py], dst_ref=outs[a].at[k],
                    send_sem=send_sems.at[a * rel + k], recv_sem=recv_sems.at[a * rel + k],
                    device_id=(px, py, mc), device_id_type=MESH)
                cp.start()
                cps.append(cp)
        for cp in cps:
            cp.wait()

    return pl.pallas_call(
        body, name=name, out_shape=[_sds((rel,) + h.shape[1:], h.dtype) for h in hs],
        in_specs=[ANY] * n, out_specs=[ANY] * n,
        scratch_shapes=[pltpu.SemaphoreType.DMA((n * rel,)), pltpu.SemaphoreType.DMA((n * rel,))],
    )(*hs)


def _adam(w, g, m, v):
    m = ADAM_B1 * m + (1.0 - ADAM_B1) * g
    v = ADAM_B2 * v + (1.0 - ADAM_B2) * (g * g)
    m_hat = m / (1.0 - ADAM_B1 ** ADAM_STEP)
    v_hat = v / (1.0 - ADAM_B2 ** ADAM_STEP)
    delta = -ADAM_LR * (m_hat / (jnp.sqrt(v_hat) + ADAM_EPS) + ADAM_WD * w)
    return delta, m, v


def adamw_owned(chip, h, got, w, m, v, name):
    rows, cols = w.shape
    tr = _row_tile(rows, 256, 16)

    def body(chip_ref, h_ref, got_ref, w_ref, m_ref, v_ref, g_out, d_out, m_out, v_out):
        g = h_ref[...].astype(F32)
        for k in range(N_CHIP - 1):
            g = g + got_ref[k].astype(F32)
        d, mn, vn = _adam(w_ref[...], g, m_ref[...], v_ref[...])
        g_out[...] = g
        d_out[...] = d
        m_out[...] = mn
        v_out[...] = vn

    blk = pl.BlockSpec((tr, cols), lambda i, c: (i, 0))
    return pl.pallas_call(
        body, name=name, out_shape=[_sds((rows, cols), F32)] * 4,
        grid_spec=pltpu.PrefetchScalarGridSpec(
            num_scalar_prefetch=1, grid=(rows // tr,),
            in_specs=[pl.BlockSpec((None, tr, cols), lambda i, c: (c[0], i, 0)),
                      pl.BlockSpec((N_CHIP - 1, tr, cols), lambda i, c: (0, i, 0)), blk, blk, blk],
            out_specs=[blk] * 4),
        compiler_params=_params("parallel"),
    )(chip, h, got, w, m, v)


def adamw_sum(parts, w, m, v, name):
    n_parts, rows, cols = parts.shape
    tr = _row_tile(rows, 256, 8)

    def body(p_ref, w_ref, m_ref, v_ref, g_out, d_out, m_out, v_out):
        g = p_ref[0]
        for k in range(1, n_parts):
            g = g + p_ref[k]
        d, mn, vn = _adam(w_ref[...], g, m_ref[...], v_ref[...])
        g_out[...] = g
        d_out[...] = d
        m_out[...] = mn
        v_out[...] = vn

    blk = pl.BlockSpec((tr, cols), lambda i: (i, 0))
    return pl.pallas_call(
        body, name=name, out_shape=[_sds((rows, cols), F32)] * 4, grid=(rows // tr,),
        in_specs=[pl.BlockSpec((n_parts, tr, cols), lambda i: (0, i, 0)), blk, blk, blk],
        out_specs=[blk] * 4, compiler_params=_params("parallel"),
    )(parts, w, m, v)


def adaln_cols(c_all, w, b, name):
    d, n = w.shape
    tn = _row_tile(n, 768, LANE)

    def body(c_ref, w_ref, b_ref, o_ref):
        c = c_ref[...]
        o_ref[...] = jnp.dot(c * _sigmoid(c), w_ref[...], preferred_element_type=F32, precision=HI) + b_ref[...]

    return pl.pallas_call(
        body, name=name, out_shape=_sds((N_DEV, n), F32), grid=(n // tn,),
        in_specs=[pl.BlockSpec((N_DEV, d), lambda j: (0, 0)), pl.BlockSpec((d, tn), lambda j: (0, j)),
                  pl.BlockSpec((1, tn), lambda j: (0, j))],
        out_specs=pl.BlockSpec((N_DEV, tn), lambda j: (0, j)), compiler_params=_params("parallel"),
    )(c_all, w, b)


def adaln_wgrad(c_all, dmod_cols, name):
    d = c_all.shape[1]
    n = dmod_cols.shape[1]
    tn = _row_tile(n, 768, LANE)

    def body(c_ref, g_ref, o_ref):
        c = c_ref[...]
        o_ref[...] = lax.dot_general(c * _sigmoid(c), g_ref[...], (((0,), (0,)), ((), ())),
                                     preferred_element_type=F32, precision=HI)

    return pl.pallas_call(
        body, name=name, out_shape=_sds((d, n), F32), grid=(n // tn,),
        in_specs=[pl.BlockSpec((N_DEV, d), lambda j: (0, 0)), pl.BlockSpec((N_DEV, tn), lambda j: (0, j))],
        out_specs=pl.BlockSpec((d, tn), lambda j: (0, j)), compiler_params=_params("parallel"),
    )(c_all, dmod_cols)


def _modulate(h, sh, sc):
    return (h * (1.0 + sc) + sh).astype(BF16)


def ffn_in(h, sh, sc, w3, name):
    t, d = h.shape
    nb, _, bw = w3.shape
    half = nb // 2
    tm = _row_tile(t, 256, 16)

    def body(h_ref, sh_ref, sc_ref, wa_ref, wb_ref, a_ref, b_ref, s_ref):
        u = _modulate(h_ref[...], sh_ref[...], sc_ref[...])
        a = _dot(u, wa_ref[...])
        b = _dot(u, wb_ref[...])
        a_ref[...] = a.astype(BF16)
        b_ref[...] = b.astype(BF16)
        s_ref[...] = (a * _sigmoid(a) * b).astype(BF16)

    vec = pl.BlockSpec((1, d), lambda j, i: (0, 0))
    out = pl.BlockSpec((tm, bw), lambda j, i: (i, j))
    return pl.pallas_call(
        body, name=name, out_shape=[_sds((t, half * bw), BF16)] * 3, grid=(half, t // tm),
        in_specs=[pl.BlockSpec((tm, d), lambda j, i: (i, 0)), vec, vec,
                  pl.BlockSpec((None, d, bw), lambda j, i: (j, 0, 0)),
                  pl.BlockSpec((None, d, bw), lambda j, i: (j + half, 0, 0))],
        out_specs=[out] * 3, compiler_params=_params("parallel", "parallel"),
    )(h, sh, sc, w3, w3)


def mod_matmul(h, sh, sc, w3, name):
    t, d = h.shape
    nb, _, bw = w3.shape
    tm = _row_tile(t, 256, 16)

    def body(h_ref, sh_ref, sc_ref, w_ref, o_ref):
        o_ref[...] = _dot(_modulate(h_ref[...], sh_ref[...], sc_ref[...]), w_ref[...]).astype(BF16)

    vec = pl.BlockSpec((1, d), lambda j, i: (0, 0))
    return pl.pallas_call(
        body, name=name, out_shape=_sds((t, nb * bw), BF16), grid=(nb, t // tm),
        in_specs=[pl.BlockSpec((tm, d), lambda j, i: (i, 0)), vec, vec,
                  pl.BlockSpec((None, d, bw), lambda j, i: (j, 0, 0))],
        out_specs=pl.BlockSpec((tm, bw), lambda j, i: (i, j)), compiler_params=_params("parallel", "parallel"),
    )(h, sh, sc, w3)


def out_ln(s, w, hin, gmod, ln_g, ln_b, coef, name):
    t, kdim = s.shape
    d = w.shape[1]
    tm = _row_tile(t, 512, 16)
    tk = _row_tile(kdim, 512, LANE)
    nk = kdim // tk

    def body(s_ref, w_ref, hin_ref, gm_ref, g_ref, b_ref, f_ref, z_ref, h_ref, acc):
        k = pl.program_id(1)

        @pl.when(k == 0)
        def _():
            acc[...] = jnp.zeros_like(acc)

        acc[...] += _dot(s_ref[...], w_ref[...])

        @pl.when(k == nk - 1)
        def _():
            f = acc[...]
            z = ALPHA * hin_ref[...] + (coef * gm_ref[...]) * f
            mu = jnp.mean(z, axis=-1, keepdims=True)
            zc = z - mu
            var = jnp.mean(zc * zc, axis=-1, keepdims=True)
            f_ref[...] = f.astype(BF16)
            z_ref[...] = z
            h_ref[...] = zc * lax.rsqrt(var + LN_EPS) * g_ref[...] + b_ref[...]

    vec = pl.BlockSpec((1, d), lambda i, k: (0, 0))
    row = pl.BlockSpec((tm, d), lambda i, k: (i, 0))
    return pl.pallas_call(
        body, name=name, out_shape=[_sds((t, d), BF16), _sds((t, d), F32), _sds((t, d), F32)],
        grid=(t // tm, nk),
        in_specs=[pl.BlockSpec((tm, tk), lambda i, k: (i, k)), pl.BlockSpec((tk, d), lambda i, k: (k, 0)),
                  row, vec, vec, vec],
        out_specs=[row, row, row], scratch_shapes=[pltpu.VMEM((tm, d), F32)],
        compiler_params=_params("parallel", "arbitrary"),
    )(s, w, hin, gmod, ln_g, ln_b)


def ln_bwd(dh, z, f, ln_g, gmod, coef, name, target=None):
    t, d = z.shape
    tm = _row_tile(t, 256, 16)
    head = target is not None

    def body(*refs):
        if head:
            dh_ref, tg_ref, z_ref, f_ref, g_ref, gm_ref, dz_ref, df_ref, dg_ref, db_ref, dgm_ref, loss_ref = refs
        else:
            dh_ref, z_ref, f_ref, g_ref, gm_ref, dz_ref, df_ref, dg_ref, db_ref, dgm_ref = refs
        i = pl.program_id(0)

        @pl.when(i == 0)
        def _():
            dg_ref[...] = jnp.zeros_like(dg_ref)
            db_ref[...] = jnp.zeros_like(db_ref)
            dgm_ref[...] = jnp.zeros_like(dgm_ref)
            if head:
                loss_ref[...] = jnp.zeros_like(loss_ref)

        dh = dh_ref[...]
        if head:
            err = dh - tg_ref[...]
            loss_ref[...] += 0.5 * jnp.sum(jnp.mean(err * err, axis=-1, keepdims=True))
            dh = err / d
        zv = z_ref[...]
        mu = jnp.mean(zv, axis=-1, keepdims=True)
        zc = zv - mu
        rstd = lax.rsqrt(jnp.mean(zc * zc, axis=-1, keepdims=True) + LN_EPS)
        xhat = zc * rstd
        dxh = dh * g_ref[...]
        dz = rstd * (dxh - jnp.mean(dxh, axis=-1, keepdims=True)
                     - xhat * jnp.mean(dxh * xhat, axis=-1, keepdims=True))
        dz_ref[...] = dz
        df_ref[...] = ((coef * gm_ref[...]) * dz).astype(BF16)
        dg_ref[...] += _colsum(dh * xhat)
        db_ref[...] += _colsum(dh)
        dgm_ref[...] += _colsum(coef * f_ref[...].astype(F32) * dz)

    vec = pl.BlockSpec((1, d), lambda i: (0, 0))
    row = pl.BlockSpec((tm, d), lambda i: (i, 0))
    ins = [dh] + ([target] if head else []) + [z, f, ln_g, gmod]
    in_specs = [row] + ([row] if head else []) + [row, row, vec, vec]
    out_shape = [_sds((t, d), F32), _sds((t, d), BF16)] + [_sds((1, d), F32)] * 3
    out_specs = [row, row, vec, vec, vec]
    if head:
        out_shape.append(_sds((1, LANE), F32))
        out_specs.append(pl.BlockSpec((1, LANE), lambda i: (0, 0)))
    return pl.pallas_call(
        body, name=name, out_shape=out_shape, grid=(t // tm,), in_specs=in_specs, out_specs=out_specs,
        compiler_params=_params("arbitrary"),
    )(*ins)


def ffn_bwd_act(df, w, a, b, name):
    t, d = df.shape
    fdim = w.shape[0]
    bw = fdim // (N_DEV // 2)
    tm = _row_tile(t, 256, 16)

    def body(df_ref, w_ref, a_ref, b_ref, o_ref):
        ds = _dot_nt(df_ref[...], w_ref[...])
        av = a_ref[...].astype(F32)
        sg = _sigmoid(av)
        o_ref[0] = (ds * b_ref[...].astype(F32) * (sg * (1.0 + av * (1.0 - sg)))).astype(BF16)
        o_ref[1] = (ds * (av * sg)).astype(BF16)

    act = pl.BlockSpec((tm, bw), lambda j, i: (i, j))
    return pl.pallas_call(
        body, name=name, out_shape=_sds((2, t, fdim), BF16), grid=(fdim // bw, t // tm),
        in_specs=[pl.BlockSpec((tm, d), lambda j, i: (i, 0)), pl.BlockSpec((bw, d), lambda j, i: (j, 0)), act, act],
        out_specs=pl.BlockSpec((2, tm, bw), lambda j, i: (0, i, j)),
        compiler_params=_params("parallel", "parallel"),
    )(df, w, a, b)


def matmul_tn(name, a, a_block, a_map, b, b_block, b_map, out_shape, o_block, o_map, n_out, mod=None):
    tk = [s for s in a_block if s is not None][0]
    nk = a.shape[-2] // tk
    m, nn = [s for s in o_block if s is not None]

    def body(*refs):
        if mod is None:
            a_ref, b_ref, o_ref, acc = refs
        else:
            a_ref, sh_ref, sc_ref, b_ref, o_ref, acc = refs
        k = pl.program_id(1)

        @pl.when(k == 0)
        def _():
            acc[...] = jnp.zeros_like(acc)

        av = a_ref[...]
        if mod is not None:
            av = _modulate(av, sh_ref[...], sc_ref[...])
        acc[...] += _dot_tn(av, b_ref[...])

        @pl.when(k == nk - 1)
        def _():
            o_ref[...] = acc[...].astype(o_ref.dtype)

    ins = [a] + (list(mod) if mod is not None else []) + [b]
    in_specs = [pl.BlockSpec(a_block, a_map)]
    if mod is not None:
        vec = pl.BlockSpec((1, mod[0].shape[1]), lambda n, k: (0, 0))
        in_specs += [vec, vec]
    in_specs.append(pl.BlockSpec(b_block, b_map))
    return pl.pallas_call(
        body, name=name, out_shape=out_shape, grid=(n_out, nk), in_specs=in_specs,
        out_specs=pl.BlockSpec(o_block, o_map), scratch_shapes=[pltpu.VMEM((m, nn), F32)],
        compiler_params=_params("parallel", "arbitrary"),
    )(*ins)


def matmul_nt_blocks(name, dy, dy_block, dy_map, w3, t, resid=None):
    nk, n, bw = w3.shape
    tm = [s for s in dy_block if s is not None][0]

    def body(*refs):
        if resid is None:
            dy_ref, w_ref, o_ref, acc = refs
        else:
            dy_ref, w_ref, dz_ref, hin_ref, sc_ref, o_ref, dsc_ref, dsh_ref, acc = refs
        i, k = pl.program_id(0), pl.program_id(1)

        @pl.when(k == 0)
        def _():
            acc[...] = jnp.zeros_like(acc)

        if resid is not None:
            @pl.when((k == 0) & (i == 0))
            def _():
                dsc_ref[...] = jnp.zeros_like(dsc_ref)
                dsh_ref[...] = jnp.zeros_like(dsh_ref)

        acc[...] += _dot_nt(dy_ref[...], w_ref[...])

        @pl.when(k == nk - 1)
        def _():
            du = acc[...]
            if resid is None:
                o_ref[...] = du.astype(o_ref.dtype)
            else:
                o_ref[...] = ALPHA * dz_ref[...] + du * (1.0 + sc_ref[...])
                dsc_ref[...] += _colsum(du * hin_ref[...])
                dsh_ref[...] += _colsum(du)

    row = pl.BlockSpec((tm, n), lambda i, k: (i, 0))
    vec = pl.BlockSpec((1, n), lambda i, k: (0, 0))
    in_specs = [pl.BlockSpec(dy_block, dy_map), pl.BlockSpec((None, n, bw), lambda i, k: (k, 0, 0))]
    ins = [dy, w3]
    if resid is None:
        out_shape, out_specs = _sds((t, n), BF16), row
    else:
        ins += list(resid)
        in_specs += [row, row, vec]
        out_shape = [_sds((t, n), F32), _sds((1, n), F32), _sds((1, n), F32)]
        out_specs = [row, vec, vec]
    return pl.pallas_call(
        body, name=name, out_shape=out_shape, grid=(t // tm, nk), in_specs=in_specs, out_specs=out_specs,
        scratch_shapes=[pltpu.VMEM((tm, n), F32)], compiler_params=_params("arbitrary", "arbitrary"),
    )(*ins)


def _attn_tables(n_rel):
    q = np.arange(QB)[:, None]
    k = np.arange(KW)[None, :]
    rel = k - (KW - QB) - q
    idx = np.clip(rel, -REL_CLIP, CHUNK - 1) + REL_CLIP
    kc, qc = k // CHUNK, q // CHUNK
    band = (kc >= qc) & (kc <= qc + A_PAST_CHUNKS)
    assert idx.max() < n_rel
    return idx.astype(np.int32), band


def _kv_specs(dh, order):
    if order == "hi":
        return [pl.BlockSpec((None, QB, dh), functools.partial(lambda r, h, i: (h, jnp.maximum(i - r, 0), 0), r))
                for r in (2, 1, 0)]
    raise ValueError(order)


def _scores(q, ks, bias, i, scale):
    s = jnp.concatenate([_dot_nt(q, kk) for kk in ks], axis=1) * scale + bias
    col = lax.broadcasted_iota(jnp.int32, s.shape, 1)
    return jnp.where(col >= (2 - i) * QB, s, NEG)


def attn_fwd(q, k, v, bias, name):
    nh, t, dh = q.shape
    scale = dh ** -0.5

    def body(q_ref, k0, k1, k2, v0, v1, v2, b_ref, o_ref, lse_ref):
        i = pl.program_id(1)
        s = _scores(q_ref[...], [k0[...], k1[...], k2[...]], b_ref[...], i, scale)
        m = jnp.max(s, axis=-1, keepdims=True)
        p = jnp.exp(s - m)
        l = jnp.sum(p, axis=-1, keepdims=True)
        pb = p.astype(BF16)
        o = sum(_dot(pb[:, r * QB:(r + 1) * QB], vv[...]) for r, vv in enumerate((v0, v1, v2)))
        o_ref[...] = (o / l).astype(BF16)
        lse_ref[...] = m + jnp.log(l)

    qs = pl.BlockSpec((None, QB, dh), lambda h, i: (h, i, 0))
    st = pl.BlockSpec((None, QB, 1), lambda h, i: (h, i, 0))
    kv = _kv_specs(dh, "hi")
    return pl.pallas_call(
        body, name=name, out_shape=[_sds((nh, t, dh), BF16), _sds((nh, t, 1), F32)], grid=(nh, t // QB),
        in_specs=[qs] + kv + kv + [pl.BlockSpec((None, QB, KW), lambda h, i: (h, 0, 0))],
        out_specs=[qs, st], compiler_params=_params("parallel", "parallel"),
    )(q, k, k, k, v, v, v, bias)


def attn_bwd_q(q, k, v, bias, lse, do, name):
    nh, t, dh = q.shape
    scale = dh ** -0.5

    def body(q_ref, k0, k1, k2, v0, v1, v2, b_ref, lse_ref, do_ref, dq_ref, dl_ref, db_ref):
        i = pl.program_id(1)
        ks = [k0[...], k1[...], k2[...]]
        s = _scores(q_ref[...], ks, b_ref[...], i, scale)
        p = jnp.exp(s - lse_ref[...])
        dov = do_ref[...]
        dp = jnp.concatenate([_dot_nt(dov, vv[...]) for vv in (v0, v1, v2)], axis=1)
        delta = jnp.sum(p * dp, axis=-1, keepdims=True)
        ds = p * (dp - delta)
        dsb = ds.astype(BF16)
        dq = sum(_dot(dsb[:, r * QB:(r + 1) * QB], kk) for r, kk in enumerate(ks))
        dq_ref[...] = (dq * scale).astype(BF16)
        dl_ref[...] = delta

        @pl.when(i == 0)
        def _():
            db_ref[...] = jnp.zeros_like(db_ref)

        db_ref[...] += ds

    qs = pl.BlockSpec((None, QB, dh), lambda h, i: (h, i, 0))
    st = pl.BlockSpec((None, QB, 1), lambda h, i: (h, i, 0))
    tab = pl.BlockSpec((None, QB, KW), lambda h, i: (h, 0, 0))
    kv = _kv_specs(dh, "hi")
    return pl.pallas_call(
        body, name=name,
        out_shape=[_sds((nh, t, dh), BF16), _sds((nh, t, 1), F32), _sds((nh, QB, KW), F32)],
        grid=(nh, t // QB), in_specs=[qs] + kv + kv + [tab, st, qs], out_specs=[qs, st, tab],
        compiler_params=_params("parallel", "arbitrary"),
    )(q, k, k, k, v, v, v, bias, lse, do)


def attn_bwd_kv(q, k, v, bias, lse, delta, do, name):
    nh, t, dh = q.shape
    nb = t // QB
    scale = dh ** -0.5

    def body(k_ref, v_ref, b_ref, *refs):
        qr, lr, dr, gr = refs[0:3], refs[3:6], refs[6:9], refs[9:12]
        dk_ref, dv_ref = refs[12:]
        j = pl.program_id(1)
        kk, vv = k_ref[...], v_ref[...]
        dk = jnp.zeros((QB, dh), F32)
        dv = jnp.zeros((QB, dh), F32)
        for r in range(3):
            seg = 2 - r
            qv, dov = qr[r][...], gr[r][...]
            s = _dot_nt(qv, kk) * scale + b_ref[:, seg * QB:(seg + 1) * QB]
            p = jnp.where(j + r < nb, jnp.exp(s - lr[r][...]), 0.0)
            ds = p * (_dot_nt(dov, vv) - dr[r][...])
            dv = dv + _dot_tn(p.astype(BF16), dov)
            dk = dk + _dot_tn(ds.astype(BF16), qv)
        dk_ref[...] = (dk * scale).astype(BF16)
        dv_ref[...] = dv.astype(BF16)

    def later(width):
        return [pl.BlockSpec((None, QB, width),
                             functools.partial(lambda r, h, j: (h, jnp.minimum(j + r, nb - 1), 0), r))
                for r in range(3)]

    own = pl.BlockSpec((None, QB, dh), lambda h, j: (h, j, 0))
    return pl.pallas_call(
        body, name=name, out_shape=[_sds((nh, t, dh), BF16)] * 2, grid=(nh, nb),
        in_specs=[own, own, pl.BlockSpec((None, QB, KW), lambda h, j: (h, 0, 0))]
        + later(dh) + later(1) + later(1) + later(dh),
        out_specs=[own, own], compiler_params=_params("parallel", "parallel"),
    )(k, v, bias, q, q, q, lse, lse, lse, delta, delta, delta, do, do, do)


def _tri(strict):
    r = lax.broadcasted_iota(jnp.int32, (CHUNK, CHUNK), 0)
    c = lax.broadcasted_iota(jnp.int32, (CHUNK, CHUNK), 1)
    return jnp.where((c < r) if strict else (c <= r), 1.0, 0.0).astype(F32)


def _gate(lr, wa, ba):
    y = _dot(lr, wa) + ba
    return (jnp.minimum(y, 0.0) - jnp.log(1.0 + jnp.exp(-jnp.abs(y)))) / GATE_TAU, y


def _decays(la):
    cum = jnp.dot(_tri(False), la, preferred_element_type=F32, precision=HI)
    last = cum[CHUNK - 1:CHUNK, :]
    return jnp.exp(last - cum), jnp.exp(last)


def _gla_specs(cols, hk, hv, order):
    def at(start, width):
        return pl.BlockSpec((GB, width), lambda h, i: (order(i), start // width + h))
    return [at(cols["qb"], hk), at(cols["kb"], hk), at(cols["vb"], hv), at(cols["rb"], hv),
            pl.BlockSpec((GB, LANE), lambda h, i: (order(i), cols["lr"] // LANE))]


def gla_fwd(p, cols, wa, ba, gn, nh, hk, hv, name):
    t = p.shape[0]
    nc = t // CHUNK
    scale = hk ** -0.5
    per = GB // CHUNK

    def body(q_ref, k_ref, v_ref, r_ref, lr_ref, wa_ref, ba_ref, gn_ref, o_ref, y_ref, st_ref, state):
        @pl.when(pl.program_id(1) == 0)
        def _():
            state[...] = jnp.zeros_like(state)

        for c in range(per):
            rows = pl.ds(c * CHUNK, CHUNK)
            la, _ = _gate(lr_ref[rows, :], wa_ref[...], ba_ref[...])
            w, decay = _decays(la)
            kdec = (k_ref[rows, :].astype(F32) * w).astype(BF16)
            st = decay * state[...] + _dot_tn(v_ref[rows, :], kdec)
            state[...] = st
            st_ref[c] = st
            o = _dot_nt(q_ref[rows, :], st.astype(BF16)) * scale
            o_ref[rows, :] = o
            rinv = lax.rsqrt(jnp.mean(o * o, axis=-1, keepdims=True) + RMS_EPS)
            rv = r_ref[rows, :].astype(F32)
            y_ref[rows, :] = (o * rinv * gn_ref[...] * (rv * _sigmoid(rv))).astype(BF16)

    return pl.pallas_call(
        body, name=name,
        out_shape=[_sds((t, nh * hv), F32), _sds((t, nh * hv), BF16), _sds((nh, nc, hv, hk), F32)],
        grid=(nh, t // GB),
        in_specs=_gla_specs(cols, hk, hv, lambda i: i)
        + [pl.BlockSpec((LANE, hk), lambda h, i: (0, h)), pl.BlockSpec((1, hk), lambda h, i: (0, h)),
           pl.BlockSpec((1, hv), lambda h, i: (0, 0))],
        out_specs=[pl.BlockSpec((GB, hv), lambda h, i: (i, h)), pl.BlockSpec((GB, hv), lambda h, i: (i, h)),
                   pl.BlockSpec((None, per, hv, hk), lambda h, i: (h, i, 0, 0))],
        scratch_shapes=[pltpu.VMEM((hv, hk), F32)], compiler_params=_params("parallel", "arbitrary"),
    )(p, p, p, p, p, wa, ba, gn)


def gla_bwd(p, cols, wa, ba, gn, o, states, dy, nh, hk, hv, name):
    t = p.shape[0]
    nblk = t // GB
    scale = hk ** -0.5
    per = GB // CHUNK

    def rev(i):
        return nblk - 1 - i

    def body(q_ref, k_ref, v_ref, r_ref, lr_ref, wa_ref, ba_ref, gn_ref, o_ref, st_ref, sp_ref, dy_ref,
             dq_ref, dk_ref, dv_ref, dr_ref, dg_ref, dgn_ref, carry):
        h, i = pl.program_id(0), pl.program_id(1)

        @pl.when(i == 0)
        def _():
            carry[...] = jnp.zeros_like(carry)

        @pl.when((i == 0) & (h == 0))
        def _():
            dgn_ref[...] = jnp.zeros_like(dgn_ref)

        gnv = gn_ref[...]
        for c in reversed(range(per)):
            rows = pl.ds(c * CHUNK, CHUNK)
            rv = r_ref[rows, :].astype(F32)
            sg = _sigmoid(rv)
            dyv = dy_ref[rows, :].astype(F32)
            ov = o_ref[rows, :]
            rinv = lax.rsqrt(jnp.mean(ov * ov, axis=-1, keepdims=True) + RMS_EPS)
            dn = dyv * (rv * sg)
            dr_ref[rows, :] = (dyv * (ov * rinv * gnv) * (sg * (1.0 + rv * (1.0 - sg)))).astype(BF16)
            dgn_ref[...] += _colsum(dn * ov * rinv)
            dxh = dn * gnv
            do = rinv * dxh - ov * (rinv * rinv * rinv) * jnp.mean(dxh * ov, axis=-1, keepdims=True)
            dob = (do * scale).astype(BF16)
            qv, kv, vv = q_ref[rows, :], k_ref[rows, :], v_ref[rows, :]
            dq_ref[rows, :] = _dot(dob, st_ref[c].astype(BF16)).astype(BF16)
            dst = carry[...] + _dot_tn(dob, qv)
            if c > 0:
                prev = st_ref[c - 1]
            else:
                prev = jnp.where(i == nblk - 1, 0.0, sp_ref[0])
            ddecay = _colsum(dst * prev)
            la, y = _gate(lr_ref[rows, :], wa_ref[...], ba_ref[...])
            w, decay = _decays(la)
            kf = kv.astype(F32)
            kdec = (kf * w).astype(BF16)
            dstb = dst.astype(BF16)
            dkdec = _dot(vv, dstb)
            dv_ref[rows, :] = _dot_nt(kdec, dstb).astype(BF16)
            dk_ref[rows, :] = (dkdec * w).astype(BF16)
            e = dkdec * kf * w
            dla = jnp.dot(_tri(True), e, preferred_element_type=F32, precision=HI) + ddecay * decay
            dg_ref[rows, :] = dla * (1.0 / GATE_TAU) * _sigmoid(-y)
            carry[...] = decay * dst

    per_head = lambda width: pl.BlockSpec((GB, width), lambda h, i: (rev(i), h))
    return pl.pallas_call(
        body, name=name,
        out_shape=[_sds((t, nh * hk), BF16), _sds((t, nh * hk), BF16), _sds((t, nh * hv), BF16),
                   _sds((t, nh * hv), BF16), _sds((t, nh * hk), F32), _sds((1, hv), F32)],
        grid=(nh, nblk),
        in_specs=_gla_specs(cols, hk, hv, rev)
        + [pl.BlockSpec((LANE, hk), lambda h, i: (0, h)), pl.BlockSpec((1, hk), lambda h, i: (0, h)),
           pl.BlockSpec((1, hv), lambda h, i: (0, 0)), per_head(hv),
           pl.BlockSpec((None, per, hv, hk), lambda h, i: (h, rev(i), 0, 0)),
           pl.BlockSpec((None, 1, hv, hk), lambda h, i: (h, jnp.maximum(rev(i) * per - 1, 0), 0, 0)),
           per_head(hv)],
        out_specs=[per_head(hk), per_head(hk), per_head(hv), per_head(hv), per_head(hk),
                   pl.BlockSpec((1, hv), lambda h, i: (0, 0))],
        scratch_shapes=[pltpu.VMEM((hv, hk), F32)], compiler_params=_params("arbitrary", "arbitrary"),
    )(p, p, p, p, p, wa, ba, gn, o, states, states, dy)


def gate_bwd(p, lr_col, dg, wa, name):
    t, kd = dg.shape
    tm = _row_tile(t, 512, 16)

    def body(lr_ref, dg_ref, wa_ref, dlr_ref, dwa_ref, dba_ref):
        @pl.when(pl.program_id(0) == 0)
        def _():
            dwa_ref[...] = jnp.zeros_like(dwa_ref)
            dba_ref[...] = jnp.zeros_like(dba_ref)

        g = dg_ref[...]
        gb = g.astype(BF16)
        dlr_ref[...] = _dot_nt(gb, wa_ref[...]).astype(BF16)
        dwa_ref[...] += _dot_tn(lr_ref[...], gb)
        dba_ref[...] += _colsum(g)

    return pl.pallas_call(
        body, name=name, out_shape=[_sds((t, LANE), BF16), _sds((LANE, kd), F32), _sds((1, kd), F32)],
        grid=(t // tm,),
        in_specs=[pl.BlockSpec((tm, LANE), lambda i: (i, lr_col // LANE)), pl.BlockSpec((tm, kd), lambda i: (i, 0)),
                  pl.BlockSpec((LANE, kd), lambda i: (0, 0))],
        out_specs=[pl.BlockSpec((tm, LANE), lambda i: (i, 0)), pl.BlockSpec((LANE, kd), lambda i: (0, 0)),
                   pl.BlockSpec((1, kd), lambda i: (0, 0))],
        compiler_params=_params("arbitrary"),
    )(p, dg, wa)


def proj_merge(ya, yb, wa3, wb3, p, ga_col, gb_col, name):
    t, kd = ya.shape
    nb, _, bw = wa3.shape
    tm = _row_tile(t, 512, 16)

    def body(ya_ref, yb_ref, wa_ref, wb_ref, ga_ref, gb_ref, pa_ref, pb_ref, mg_ref):
        pa = _dot(ya_ref[...], wa_ref[...])
        pb = _dot(yb_ref[...], wb_ref[...])
        pa_ref[...] = pa.astype(BF16)
        pb_ref[...] = pb.astype(BF16)
        mg_ref[...] = (_sigmoid(ga_ref[...].astype(F32)) * pa + _sigmoid(gb_ref[...].astype(F32)) * pb).astype(BF16)

    act = pl.BlockSpec((tm, kd), lambda j, i: (i, 0))
    wsp = pl.BlockSpec((None, kd, bw), lambda j, i: (j, 0, 0))
    out = pl.BlockSpec((tm, bw), lambda j, i: (i, j))
    return pl.pallas_call(
        body, name=name, out_shape=[_sds((t, nb * bw), BF16)] * 3, grid=(nb, t // tm),
        in_specs=[act, act, wsp, wsp, pl.BlockSpec((tm, bw), lambda j, i: (i, ga_col // bw + j)),
                  pl.BlockSpec((tm, bw), lambda j, i: (i, gb_col // bw + j))],
        out_specs=[out] * 3, compiler_params=_params("parallel", "parallel"),
    )(ya, yb, wa3, wb3, p, p)


def merge_bwd(dm, w, p, ga_col, gb_col, pa, pb, name):
    t, d = dm.shape
    n = w.shape[0]
    tn = _row_tile(n, 512, LANE)
    tm = _row_tile(t, 256, 16)

    def body(dm_ref, w_ref, ga_ref, gb_ref, pa_ref, pb_ref, dpa_ref, dpb_ref, dga_ref, dgb_ref):
        dmg = _dot_nt(dm_ref[...], w_ref[...])
        sa = _sigmoid(ga_ref[...].astype(F32))
        sb = _sigmoid(gb_ref[...].astype(F32))
        dpa_ref[...] = (dmg * sa).astype(BF16)
        dpb_ref[...] = (dmg * sb).astype(BF16)
        dga_ref[...] = (dmg * pa_ref[...].astype(F32) * sa * (1.0 - sa)).astype(BF16)
        dgb_ref[...] = (dmg * pb_ref[...].astype(F32) * sb * (1.0 - sb)).astype(BF16)

    out = pl.BlockSpec((tm, tn), lambda j, i: (i, j))
    return pl.pallas_call(
        body, name=name, out_shape=[_sds((t, n), BF16)] * 4, grid=(n // tn, t // tm),
        in_specs=[pl.BlockSpec((tm, d), lambda j, i: (i, 0)), pl.BlockSpec((tn, d), lambda j, i: (j, 0)),
                  pl.BlockSpec((tm, tn), lambda j, i: (i, ga_col // tn + j)),
                  pl.BlockSpec((tm, tn), lambda j, i: (i, gb_col // tn + j)), out, out],
        out_specs=[out] * 4, compiler_params=_params("parallel", "parallel"),
    )(dm, w, p, p, pa, pb)


def rel_bias_grad(skew, clip_map, name):
    nh, _, jd = skew.shape
    n_rel = clip_map.shape[1]

    def body(s_ref, c_ref, o_ref):
        sums = jnp.concatenate([_colsum(s_ref[h]) for h in range(nh)], axis=0)
        o_ref[...] = jnp.dot(sums, c_ref[...], preferred_element_type=F32, precision=HI)

    return pl.pallas_call(
        body, name=name, out_shape=_sds((nh, n_rel), F32), in_specs=[VMEM_SPEC, VMEM_SPEC], out_specs=VMEM_SPEC,
        compiler_params=pltpu.CompilerParams(vmem_limit_bytes=VMEM_LIMIT),
    )(skew, clip_map)


MIX_BLOCK = 9 * LANE


def mix_layout(d, a_width, bk, bv):
    main = 3 * a_width + 2 * bk + 2 * bv
    cols = {"qa": 0, "ka": a_width, "va": 2 * a_width, "qb": 3 * a_width, "kb": 3 * a_width + bk,
            "vb": 3 * a_width + 2 * bk, "rb": 3 * a_width + 2 * bk + bv, "ga": main, "gb": main + d,
            "lr": main + 2 * d}
    total = main + 2 * d + LANE
    assert total % MIX_BLOCK == 0
    return cols, main, total


def mix_weight_in(gathered, d, main, rank, total):
    full = gathered.transpose(1, 0, 2).reshape(d, -1)
    w = jnp.concatenate([full[:, :main], full[:, main + rank:], full[:, main:main + rank],
                         jnp.zeros((d, LANE - rank), full.dtype)], axis=1)
    return w.reshape(d, total // MIX_BLOCK, MIX_BLOCK).transpose(1, 0, 2)


def mix_weight_grad_out(g3, d, main, rank):
    full = g3.transpose(1, 0, 2).reshape(d, -1)
    w = jnp.concatenate([full[:, :main], full[:, main + 2 * d:main + 2 * d + rank], full[:, main:main + 2 * d]], axis=1)
    return w.reshape(d, N_DEV, -1).transpose(1, 0, 2)


def heads_major(x, nh):
    t = x.shape[0]
    return x.reshape(t, nh, -1).transpose(1, 0, 2)


def heads_minor(x):
    nh, t, dh = x.shape
    return x.transpose(1, 0, 2).reshape(t, nh * dh)


def ffn_forward(h, sh, sc, g, w_in3, w_out, ln_g, ln_b, tag):
    a, b, s = ffn_in(h, sh, sc, w_in3, f"{tag}_in")
    f, z, hout = out_ln(s, w_out, h, g, ln_g, ln_b, 0.5, f"{tag}_out")
    return hout, (h, a, b, s, f, z)


def ffn_backward(dh, saved, sh, sc, g, w_in3, w_out, ln_g, tag, target=None):
    hin, a, b, s, f, z = saved
    t, d = hin.shape
    nb, _, bw = w_in3.shape
    half = nb // 2
    fdim = w_out.shape[0]
    res = ln_bwd(dh, z, f, ln_g, g, 0.5, f"{tag}_ln_bwd", target=target)
    dz, df, dln_g, dln_b, dg = res[:5]
    dab = ffn_bwd_act(df, w_out, a, b, f"{tag}_act_bwd")
    tk = _row_tile(t, 512, 16)
    dw_out = matmul_tn(f"{tag}_dwout", s, (tk, bw), lambda n, k: (k, n), df, (tk, d), lambda n, k: (k, 0),
                       _sds((fdim, d), BF16), (bw, d), lambda n, k: (n, 0), fdim // bw)
    tm = _row_tile(t, 256, 16)
    dhin, dsc, dsh = matmul_nt_blocks(f"{tag}_du", dab, (None, tm, bw), lambda i, k: (k // half, i, k % half),
                                      w_in3, t, resid=(dz, hin, sc))
    dw_in = matmul_tn(f"{tag}_dwin", hin, (tk, d), lambda n, k: (k, 0), dab, (None, tk, bw),
                      lambda n, k: (n // half, k, n % half), _sds((nb, d, bw), BF16), (None, d, bw),
                      lambda n, k: (n, 0, 0), nb, mod=(sh, sc))
    grads = dict(w_in=dw_in, w_out=dw_out.reshape(N_DEV, fdim // N_DEV, d), ln_g=dln_g, ln_b=dln_b,
                 sh=dsh, sc=dsc, g=dg)
    return dhin, grads, (res[5] if target is not None else None)


def local_step(x, target, mod, wts, rel_bias, w_alpha2, b_alpha, gla_norm_g, lns):
    t, d = x.shape
    sh1, sc1, g1, sh2, sc2, g2, sh3, sc3, g3 = [mod[i:i + 1] for i in range(N_MOD)]
    ln1_g, ln1_b, ln2_g, ln2_b, ln3_g, ln3_b = lns
    n_heads_a, n_rel = rel_bias.shape
    rank, bk = w_alpha2.shape
    hv = gla_norm_g.shape[1]
    a_width = wts["proj_a"].shape[1]
    bv = wts["proj_b"].shape[1]
    nh_b = bv // hv
    hk = bk // nh_b
    cols, main, total = mix_layout(d, a_width, bk, bv)

    h1, saved1 = ffn_forward(x, sh1, sc1, g1, wts["ffn1_in"], wts["ffn1_out"], ln1_g, ln1_b, "ffn1")
    w_mix3 = mix_weight_in(wts["mix_in"], d, main, rank, total)
    p = mod_matmul(h1, sh2, sc2, w_mix3, "mix_in")
    idx, band = _attn_tables(n_rel)
    bias = jnp.where(band[None], rel_bias[:, idx], NEG)
    qa, ka, va = [heads_major(p[:, cols[n]:cols[n] + a_width], n_heads_a) for n in ("qa", "ka", "va")]
    ya_h, lse = attn_fwd(qa, ka, va, bias, "attn_fwd")
    ya = heads_minor(ya_h)
    wa_pad = jnp.zeros((LANE, bk), BF16).at[:rank].set(w_alpha2.astype(BF16))
    o_b, yb, states = gla_fwd(p, cols, wa_pad, b_alpha, gla_norm_g, nh_b, hk, hv, "gla_fwd")
    pa, pb, merged = proj_merge(ya, yb, wts["proj_a"], wts["proj_b"], p, cols["ga"], cols["gb"], "proj_merge")
    m, z2, h2 = out_ln(merged, wts["mix_out"], h1, g2, ln2_g, ln2_b, 1.0, "mix_out")
    h3, saved3 = ffn_forward(h2, sh3, sc3, g3, wts["ffn2_in"], wts["ffn2_out"], ln3_g, ln3_b, "ffn2")

    dh2, gr3, loss = ffn_backward(h3, saved3, sh3, sc3, g3, wts["ffn2_in"], wts["ffn2_out"], ln3_g, "ffn2",
                                  target=target)
    dz2, dm, dln2_g, dln2_b, dg2 = ln_bwd(dh2, z2, m, ln2_g, g2, 1.0, "mix_ln_bwd")
    dpa, dpb, dga, dgb = merge_bwd(dm, wts["mix_out"], p, cols["ga"], cols["gb"], pa, pb, "merge_bwd")
    tk = _row_tile(t, 512, 16)
    dw_mix_out = matmul_tn("mix_dwout", merged, (tk, 512), lambda n, k: (k, n), dm, (tk, d), lambda n, k: (k, 0),
                           _sds((d, d), BF16), (512, d), lambda n, k: (n, 0), d // 512)
    tm = _row_tile(t, 512, 16)
    pbw = wts["proj_a"].shape[2]
    dya = matmul_nt_blocks("proj_a_dy", dpa, (tm, pbw), lambda i, k: (i, k), wts["proj_a"], t)
    dyb = matmul_nt_blocks("proj_b_dy", dpb, (tm, pbw), lambda i, k: (i, k), wts["proj_b"], t)
    dw_pa = matmul_tn("proj_a_dw", ya, (tk, a_width), lambda n, k: (k, 0), dpa, (tk, pbw), lambda n, k: (k, n),
                      _sds((N_DEV, a_width, pbw), BF16), (None, a_width, pbw), lambda n, k: (n, 0, 0), N_DEV)
    dw_pb = matmul_tn("proj_b_dw", yb, (tk, bv), lambda n, k: (k, 0), dpb, (tk, pbw), lambda n, k: (k, n),
                      _sds((N_DEV, bv, pbw), BF16), (None, bv, pbw), lambda n, k: (n, 0, 0), N_DEV)
    dqb, dkb, dvb, drb, dgate, dgn = gla_bwd(p, cols, wa_pad, b_alpha, gla_norm_g, o_b, states, dyb,
                                             nh_b, hk, hv, "gla_bwd")
    dlr, dwa_pad, dba = gate_bwd(p, cols["lr"], dgate, wa_pad, "gate_bwd")
    do_h = heads_major(dya, n_heads_a)
    dqa_h, delta, dbias = attn_bwd_q(qa, ka, va, bias, lse, do_h, "attn_bwd_q")
    dka_h, dva_h = attn_bwd_kv(qa, ka, va, bias, lse, delta, do_h, "attn_bwd_kv")
    jd = KW + QB - 1
    skew = jnp.pad(dbias[:, ::-1, :], ((0, 0), (0, 0), (0, QB))).reshape(n_heads_a, -1)[:, :QB * jd]
    skew = jnp.pad(skew.reshape(n_heads_a, QB, jd), ((0, 0), (0, 0), (0, 1)))
    clip_map = (np.clip(np.arange(jd + 1) - (KW - 1), -REL_CLIP, CHUNK - 1)[:, None] + REL_CLIP
                == np.arange(n_rel)[None, :]).astype(np.float32)
    clip_map[jd] = 0.0
    d_rel = rel_bias_grad(skew, jnp.asarray(clip_map), "rel_bias_grad")
    dp = jnp.concatenate([heads_minor(dqa_h), heads_minor(dka_h), heads_minor(dva_h), dqb, dkb, dvb, drb,
                          dga, dgb, dlr], axis=1)
    tm = _row_tile(t, 256, 16)
    dh1, dsc2, dsh2 = matmul_nt_blocks("mix_du", dp, (tm, MIX_BLOCK), lambda i, k: (i, k), w_mix3, t,
                                       resid=(dz2, h1, sc2))
    nmix = total // MIX_BLOCK
    dw_mix3 = matmul_tn("mix_dwin", h1, (tk, d), lambda n, k: (k, 0), dp, (tk, MIX_BLOCK), lambda n, k: (k, n),
                        _sds((nmix, d, MIX_BLOCK), BF16), (None, d, MIX_BLOCK), lambda n, k: (n, 0, 0), nmix,
                        mod=(sh2, sc2))
    dw_mix_in = mix_weight_grad_out(dw_mix3, d, main, rank)
    dx, gr1, _ = ffn_backward(dh1, saved1, sh1, sc1, g1, wts["ffn1_in"], wts["ffn1_out"], ln1_g, "ffn1")

    dmod = [gr1["sh"], gr1["sc"], gr1["g"], dsh2, dsc2, dg2, gr3["sh"], gr3["sc"], gr3["g"]]
    big = dict(ffn1_in=gr1["w_in"], ffn1_out=gr1["w_out"], mix_in=dw_mix_in, proj_a=dw_pa, proj_b=dw_pb,
               mix_out=dw_mix_out.reshape(N_DEV, d // N_DEV, d), ffn2_in=gr3["w_in"], ffn2_out=gr3["w_out"])
    small = dict(ln1_g=gr1["ln_g"], ln1_b=gr1["ln_b"], ln2_g=dln2_g, ln2_b=dln2_b, ln3_g=gr3["ln_g"],
                 ln3_b=gr3["ln_b"], b_alpha=dba, gla_norm_g=dgn, w_alpha2=dwa_pad[:rank], rel_bias=d_rel)
    return loss, dx, dmod, big, small


BIG = ("ffn1_in", "ffn1_out", "mix_in", "proj_a", "proj_b", "mix_out", "ffn2_in", "ffn2_out")
SMALL_REPLICATED = ("b_ada", "ln1_g", "ln1_b", "ln2_g", "ln2_b", "ln3_g", "ln3_b", "b_alpha", "gla_norm_g")
SMALL_SHARDED = ("rel_bias", "w_alpha2")
WEIGHT_ORDER = ("w_ada", "b_ada", "ffn1_w_in", "ffn1_w_out", "ln1_g", "ln1_b", "w_mix_in", "rel_bias", "w_alpha2",
                "b_alpha", "gla_norm_g", "w_proj_a", "w_proj_b", "w_mix_out", "ln2_g", "ln2_b", "ffn2_w_in",
                "ffn2_w_out", "ln3_g", "ln3_b")
BIG_NAME = dict(ffn1_in="ffn1_w_in", ffn1_out="ffn1_w_out", mix_in="w_mix_in", proj_a="w_proj_a",
                proj_b="w_proj_b", mix_out="w_mix_out", ffn2_in="ffn2_w_in", ffn2_out="ffn2_w_out")


def kernel(x, c, w_ada, b_ada, ffn1_w_in, ffn1_w_out, ln1_g, ln1_b, w_mix_in, rel_bias, w_alpha2, b_alpha, gla_norm_g, w_proj_a, w_proj_b, w_mix_out, ln2_g, ln2_b, ffn2_w_in, ffn2_w_out, ln3_g, ln3_b, loss_target, m_w_ada, m_b_ada, m_ffn1_w_in, m_ffn1_w_out, m_ln1_g, m_ln1_b, m_w_mix_in, m_rel_bias, m_w_alpha2, m_b_alpha, m_gla_norm_g, m_w_proj_a, m_w_proj_b, m_w_mix_out, m_ln2_g, m_ln2_b, m_ffn2_w_in, m_ffn2_w_out, m_ln3_g, m_ln3_b, v_w_ada, v_b_ada, v_ffn1_w_in, v_ffn1_w_out, v_ln1_g, v_ln1_b, v_w_mix_in, v_rel_bias, v_w_alpha2, v_b_alpha, v_gla_norm_g, v_w_proj_a, v_w_proj_b, v_w_mix_out, v_ln2_g, v_ln2_b, v_ffn2_w_in, v_ffn2_w_out, v_ln3_g, v_ln3_b):
    env = dict(locals())
    w = {n: env[n] for n in WEIGHT_ORDER}
    mom = {n: env["m_" + n] for n in WEIGHT_ORDER}
    var = {n: env["v_" + n] for n in WEIGHT_ORDER}
    me = _me()
    dev = _lin(me)
    core = jnp.reshape(me[2], (1,)).astype(jnp.int32)
    chip = jnp.reshape(2 * me[0] + me[1], (1,)).astype(jnp.int32)
    d = x.shape[-1]

    ada_cols = w_ada.shape[-1]
    c_all = all_gather_small(c, "gather_c")[:, 0, :]
    b_cols = lax.dynamic_slice_in_dim(b_ada, dev * ada_cols, ada_cols, axis=1)
    mod_cols = adaln_cols(c_all, w_ada[0], b_cols, "adaln_cols")
    mod_all = all_gather_small(mod_cols, "gather_mod")
    mod = lax.dynamic_index_in_dim(mod_all, dev, axis=1, keepdims=False).reshape(N_MOD, d)

    shards = [w[BIG_NAME[n]][0].astype(BF16) for n in BIG]
    wts = dict(zip(BIG, all_gather_big(shards, "gather_weights")))
    for n in ("ffn1_out", "ffn2_out", "mix_out"):
        wts[n] = wts[n].reshape(-1, d)
    small_w = all_gather_small(jnp.concatenate([rel_bias[0], w_alpha2[0]], axis=1), "gather_small_w")
    n_rel_cols = rel_bias.shape[-1]
    rel_full = small_w[:, :, :n_rel_cols].transpose(1, 0, 2).reshape(small_w.shape[1], -1)
    wa2_full = small_w[:, :, n_rel_cols:].transpose(1, 0, 2).reshape(small_w.shape[1], -1)

    lns = [ln1_g, ln1_b, ln2_g, ln2_b, ln3_g, ln3_b]
    loss, dx, dmod, big, small = local_step(x[0], loss_target[0], mod, wts, rel_full, wa2_full, b_alpha,
                                            gla_norm_g, lns)
    loss = lax.psum(loss[0, 0], ("x", "y", "c"))

    names = list(BIG)
    got = pair_exchange([big[n] for n in names], "grad_pair_exchange")
    sums = [pair_add(core, big[n], g, f"grad_pair_add_{n}") for n, g in zip(names, got)]
    recv = chip_exchange(sums, "grad_chip_exchange")
    out = {}
    for n, hsum, r in zip(names, sums, recv):
        full = BIG_NAME[n]
        out[full] = [o[None] for o in adamw_owned(chip, hsum, r, w[full][0], mom[full][0], var[full][0], f"adamw_{n}")]

    packed = jnp.concatenate([g.reshape(1, -1) for g in dmod]
                             + [small[n].reshape(1, -1) for n in SMALL_REPLICATED[1:] + SMALL_SHARDED], axis=1)
    parts = all_gather_small(packed, "gather_small_grads")
    n_mod = N_MOD * d
    dmod_all = parts[:, 0, :n_mod]
    g_w_ada = adaln_wgrad(c_all, lax.dynamic_slice_in_dim(dmod_all, dev * ada_cols, ada_cols, axis=1), "adaln_wgrad")
    out["w_ada"] = [o[None] for o in adamw_sum(g_w_ada[None], w_ada[0], m_w_ada[0], v_w_ada[0], "adamw_w_ada")]

    def pack(src):
        rows = [src[n].reshape(1, -1) for n in SMALL_REPLICATED]
        return jnp.concatenate(rows + [src[n].reshape(1, -1) for n in SMALL_SHARDED], axis=1)

    n_rep = sum(w[n].size for n in SMALL_REPLICATED)
    rep_parts = parts[:, :, :n_rep]
    off = n_rep
    shard_parts = []
    for n in SMALL_SHARDED:
        rows, cols_local = w[n].shape[1], w[n].shape[2]
        full_part = parts[:, 0, off:off + rows * cols_local * N_DEV].reshape(N_DEV, rows, cols_local * N_DEV)
        mine = lax.dynamic_slice_in_dim(full_part, dev * cols_local, cols_local, axis=2)
        shard_parts.append(mine.reshape(N_DEV, 1, rows * cols_local))
        off += rows * cols_local * N_DEV
    small_parts = jnp.concatenate([rep_parts] + shard_parts, axis=2)
    res = adamw_sum(small_parts, pack(w), pack(mom), pack(var), "adamw_small")
    off = 0
    for n in SMALL_REPLICATED + SMALL_SHARDED:
        size = w[n].size
        out[n] = [r[:, off:off + size].reshape(w[n].shape) for r in res]
        off += size

    flat = [loss, dx[None]]
    for k in range(4):
        flat += [out[n][k] for n in WEIGHT_ORDER]
    return tuple(flat)
```

```python
import functools

import numpy as np
import jax
import jax.numpy as jnp
from jax import lax
from jax.experimental import pallas as pl
from jax.experimental.pallas import tpu as pltpu

F32 = jnp.float32
BF16 = jnp.bfloat16
MESH = pl.DeviceIdType.MESH
N_DEV = 8
N_CHIP = 4

CHUNK = 64
A_PAST_CHUNKS = 8
REL_CLIP = 256
GATE_TAU = 16.0
N_MOD = 9
DEPTH = 1
ALPHA = (2.0 * DEPTH) ** 0.25
LN_EPS = 1e-5
RMS_EPS = 1e-6
ADAM_LR = 0.001
ADAM_B1 = 0.9
ADAM_B2 = 0.999
ADAM_EPS = 1e-08
ADAM_WD = 0.01
ADAM_STEP = 10

LANE = 128
VMEM_LIMIT = 56 * 2 ** 20
QB = 4 * CHUNK
KW = 3 * QB
GB = 8 * CHUNK
NEG = -1e30
HI = lax.Precision.HIGHEST

ANY = pl.BlockSpec(memory_space=pl.ANY)
VMEM_SPEC = pl.BlockSpec(memory_space=pltpu.VMEM)


def _params(*sem):
    return pltpu.CompilerParams(dimension_semantics=sem, vmem_limit_bytes=VMEM_LIMIT)


def _sds(shape, dtype):
    return jax.ShapeDtypeStruct(shape, dtype)


def _dot(a, b):
    return jnp.dot(a, b, preferred_element_type=F32)


def _dot_nt(a, b):
    return lax.dot_general(a, b, (((1,), (1,)), ((), ())), preferred_element_type=F32)


def _dot_tn(a, b):
    return lax.dot_general(a, b, (((0,), (0,)), ((), ())), preferred_element_type=F32)


def _sigmoid(x):
    return 1.0 / (1.0 + jnp.exp(-x))


def _colsum(x):
    return jnp.sum(x, axis=0, keepdims=True)


def _row_tile(rows, cap, mult):
    for t in range(min(rows, cap), 0, -1):
        if rows % t == 0 and t % mult == 0:
            return t
    return rows


def _me():
    return lax.axis_index("x"), lax.axis_index("y"), lax.axis_index("c")


def _flip(me, k):
    return tuple((1 - p) if (k >> s) & 1 else p for p, s in zip(me, (2, 1, 0)))


def _lin(p):
    return 4 * p[0] + 2 * p[1] + p[2]


def all_gather_small(x, name):
    r, n = x.shape

    def body(x_ref, out_ref, send_sems, recv_sems, local_sem):
        me = _me()
        mine = pltpu.make_async_copy(x_ref, out_ref.at[_lin(me)], local_sem)
        mine.start()
        sends = []
        for k in range(1, N_DEV):
            cp = pltpu.make_async_remote_copy(
                src_ref=x_ref, dst_ref=out_ref.at[_lin(me)], send_sem=send_sems.at[k - 1],
                recv_sem=recv_sems.at[k - 1], device_id=_flip(me, k), device_id_type=MESH)
            cp.start()
            sends.append(cp)
        for k in range(1, N_DEV):
            peer = _flip(me, k)
            pltpu.make_async_remote_copy(
                src_ref=x_ref, dst_ref=out_ref.at[_lin(peer)], send_sem=send_sems.at[k - 1],
                recv_sem=recv_sems.at[k - 1], device_id=peer, device_id_type=MESH).wait_recv()
        for cp in sends:
            cp.wait_send()
        mine.wait()

    return pl.pallas_call(
        body, name=name, out_shape=_sds((N_DEV, r, n), x.dtype),
        in_specs=[VMEM_SPEC], out_specs=VMEM_SPEC,
        scratch_shapes=[pltpu.SemaphoreType.DMA((N_DEV - 1,)), pltpu.SemaphoreType.DMA((N_DEV - 1,)),
                        pltpu.SemaphoreType.DMA],
    )(x)


HBM_SPEC = pl.BlockSpec(memory_space=pltpu.HBM)
SEM_SPEC = pl.BlockSpec(memory_space=pltpu.SEMAPHORE)
EFFECT = pltpu.SideEffectType.DATAFLOW_SIDE_EFFECTING
FIRST = N_CHIP


def _hbm(v):
    return pltpu.with_memory_space_constraint(v, pltpu.HBM)


def _other_chips(mx, my):
    return [(1 - mx, my), (mx, 1 - my), (1 - mx, 1 - my)]


def gather_start(shards, after, name):
    n = len(shards)
    lands = [lax.empty((N_DEV,) + s.shape, s.dtype) for s in shards]

    def body(*refs):
        ins, zones = refs[:n], refs[n:2 * n]
        send_sems, recv_sems = refs[2 * n + 1], refs[2 * n + 2]
        token = refs[-1]
        me = _me()
        mx, my, mc = me
        for a in range(n):
            dst = zones[a].at[_lin(me)]
            targets = [(mx, my, 1 - mc)] + [(*chip, mc) for chip in _other_chips(mx, my)]
            for k, to in enumerate(targets):
                pltpu.make_async_remote_copy(
                    src_ref=ins[a], dst_ref=dst, send_sem=send_sems.at[a * FIRST + k],
                    recv_sem=recv_sems.at[a * FIRST + k], device_id=to, device_id_type=MESH).start()
        token[...] = jnp.zeros_like(token)

    sems = pltpu.SemaphoreType.DMA((n * FIRST,))
    out = pl.pallas_call(
        body, name=name,
        out_shape=(sems, sems, *[pltpu.HBM(s.shape, s.dtype) for s in shards],
                   *[pltpu.HBM(z.shape, z.dtype) for z in lands], _sds((8, LANE), F32)),
        in_specs=[HBM_SPEC] * (2 * n) + [ANY],
        out_specs=(SEM_SPEC, SEM_SPEC, *[HBM_SPEC] * (2 * n), VMEM_SPEC),
        input_output_aliases={a: 2 + a for a in range(2 * n)},
        compiler_params=pltpu.CompilerParams(has_side_effects=EFFECT),
    )(*[_hbm(s) for s in shards], *[_hbm(z) for z in lands], after)
    return out[0], out[1], out[2:2 + n], out[2 + n:2 + 2 * n], out[-1]


def gather_wait(started, after, name):
    send_sems, recv_sems, shards, lands, _ = started
    n = len(shards)

    def body(*refs):
        ins, zones = refs[:n], refs[n:2 * n]
        send_ref, recv_ref = refs[2 * n], refs[2 * n + 1]
        mx, my, mc = _me()
        for a in range(n):
            for k in range(FIRST):
                cp = pltpu.make_async_remote_copy(
                    src_ref=ins[a], dst_ref=zones[a].at[0], send_sem=send_ref.at[a * FIRST + k],
                    recv_sem=recv_ref.at[a * FIRST + k], device_id=(mx, my, 1 - mc), device_id_type=MESH)
                cp.wait_send()
                cp.wait_recv()

    out = pl.pallas_call(
        body, name=name,
        out_shape=(*[pltpu.HBM(s.shape, s.dtype) for s in shards], *[pltpu.HBM(z.shape, z.dtype) for z in lands]),
        in_specs=[HBM_SPEC] * (2 * n) + [SEM_SPEC, SEM_SPEC, ANY], out_specs=tuple([HBM_SPEC] * (2 * n)),
        input_output_aliases={a: a for a in range(2 * n)},
        compiler_params=pltpu.CompilerParams(has_side_effects=EFFECT),
    )(*shards, *lands, send_sems, recv_sems, after)
    return out[:n], out[n:]


def gather_forward(shards, lands, name):
    n = len(shards)
    rel = N_CHIP - 1

    def body(*refs):
        ins, zones, outs = refs[:n], refs[n:2 * n], refs[2 * n:3 * n]
        send_sems, recv_sems, local_sems = refs[3 * n:]
        me = _me()
        mx, my, mc = me
        chips = _other_chips(mx, my)
        mines = [pltpu.make_async_copy(ins[a], outs[a].at[_lin(me)], local_sems.at[a]) for a in range(n)]
        for m in mines:
            m.start()

        def copy(a, j, core):
            blk = _lin((*chips[j], core))
            return pltpu.make_async_remote_copy(
                src_ref=zones[a].at[blk], dst_ref=outs[a].at[blk], send_sem=send_sems.at[a * rel + j],
                recv_sem=recv_sems.at[a * rel + j], device_id=(mx, my, 1 - mc), device_id_type=MESH)

        sends = [copy(a, j, mc) for a in range(n) for j in range(rel)]
        for cp in sends:
            cp.start()
        for a in range(n):
            for j in range(rel):
                copy(a, j, 1 - mc).wait_recv()
        for cp in sends:
            cp.wait_send()
        for m in mines:
            m.wait()

    return pl.pallas_call(
        body, name=name, out_shape=[_sds(z.shape, z.dtype) for z in lands],
        in_specs=[ANY] * (2 * n), out_specs=[ANY] * n, input_output_aliases={n + a: a for a in range(n)},
        scratch_shapes=[pltpu.SemaphoreType.DMA((n * rel,)), pltpu.SemaphoreType.DMA((n * rel,)),
                        pltpu.SemaphoreType.DMA((n,))],
    )(*shards, *lands)


def pair_exchange(gs, name):
    n = len(gs)

    def body(*refs):
        ins, outs = refs[:n], refs[n:2 * n]
        send_sems, recv_sems = refs[2 * n:]
        mx, my, mc = _me()
        cps = []
        for a in range(n):
            for q in range(N_CHIP):
                cp = pltpu.make_async_remote_copy(
                    src_ref=ins[a].at[2 * q + (1 - mc)], dst_ref=outs[a].at[q],
                    send_sem=send_sems.at[a * N_CHIP + q], recv_sem=recv_sems.at[a * N_CHIP + q],
                    device_id=(mx, my, 1 - mc), device_id_type=MESH)
                cp.start()
                cps.append(cp)
        for cp in cps:
            cp.wait()

    return pl.pallas_call(
        body, name=name, out_shape=[_sds((N_CHIP,) + g.shape[1:], g.dtype) for g in gs],
        in_specs=[ANY] * n, out_specs=[ANY] * n,
        scratch_shapes=[pltpu.SemaphoreType.DMA((n * N_CHIP,)), pltpu.SemaphoreType.DMA((n * N_CHIP,))],
    )(*gs)


def _tile2(rows, cols, row_mult):
    tr = _row_tile(rows, 512, row_mult)
    if tr < rows or rows * cols <= 2 ** 20:
        return tr, cols
    return rows, _row_tile(cols, 512, LANE)


def pair_add(core, g, got, name):
    _, rows, cols = g.shape
    tr, tc = _tile2(rows, cols, 16)

    def body(core_ref, g_ref, got_ref, h_ref):
        h_ref[...] = (g_ref[...].astype(F32) + got_ref[...].astype(F32)).astype(h_ref.dtype)

    blk = pl.BlockSpec((None, tr, tc), lambda q, i, j, c: (q, i, j))
    return pl.pallas_call(
        body, name=name, out_shape=_sds((N_CHIP, rows, cols), g.dtype),
        grid_spec=pltpu.PrefetchScalarGridSpec(
            num_scalar_prefetch=1, grid=(N_CHIP, rows // tr, cols // tc),
            in_specs=[pl.BlockSpec((None, tr, tc), lambda q, i, j, c: (2 * q + c[0], i, j)), blk],
            out_specs=blk),
        compiler_params=_params("parallel", "parallel", "parallel"),
    )(core, g, got)


def chip_exchange_start(hs, name):
    n = len(hs)
    rel = N_CHIP - 1
    lands = [lax.empty((rel,) + h.shape[1:], h.dtype) for h in hs]

    def body(*refs):
        ins, zones = refs[:n], refs[n:2 * n]
        send_sems, recv_sems = refs[2 * n], refs[2 * n + 1]
        token = refs[-1]
        mx, my, mc = _me()
        for a in range(n):
            for k, (px, py) in enumerate(_other_chips(mx, my)):
                pltpu.make_async_remote_copy(
                    src_ref=ins[a].at[2 * px + py], dst_ref=zones[a].at[k], send_sem=send_sems.at[a * rel + k],
                    recv_sem=recv_sems.at[a * rel + k], device_id=(px, py, mc), device_id_type=MESH).start()
        token[...] = jnp.zeros_like(token)

    sems = pltpu.SemaphoreType.DMA((n * rel,))
    out = pl.pallas_call(
        body, name=name,
        out_shape=(sems, sems, *[pltpu.HBM(h.shape, h.dtype) for h in hs],
                   *[pltpu.HBM(z.shape, z.dtype) for z in lands], _sds((8, LANE), F32)),
        in_specs=[HBM_SPEC] * (2 * n), out_specs=(SEM_SPEC, SEM_SPEC, *[HBM_SPEC] * (2 * n), VMEM_SPEC),
        input_output_aliases={a: 2 + a for a in range(2 * n)},
        compiler_params=pltpu.CompilerParams(has_side_effects=EFFECT),
    )(*[_hbm(h) for h in hs], *[_hbm(z) for z in lands])
    return out[0], out[1], out[2:2 + n], out[2 + n:2 + 2 * n], out[-1]


def chip_exchange_wait(started, after, name):
    send_sems, recv_sems, hs, lands, _ = started
    n = len(hs)
    rel = N_CHIP - 1

    def body(*refs):
        ins, zones = refs[:n], refs[n:2 * n]
        send_ref, recv_ref = refs[2 * n], refs[2 * n + 1]
        mx, my, mc = _me()
        for a in range(n):
            for k, (px, py) in enumerate(_other_chips(mx, my)):
                cp = pltpu.make_async_remote_copy(
                    src_ref=ins[a].at[0], dst_ref=zones[a].at[k], send_sem=send_ref.at[a * rel + k],
                    recv_sem=recv_ref.at[a * rel + k], device_id=(px, py, mc), device_id_type=MESH)
                cp.wait_send()
                cp.wait_recv()

    out = pl.pallas_call(
        body, name=name,
        out_shape=(*[pltpu.HBM(h.shape, h.dtype) for h in hs], *[pltpu.HBM(z.shape, z.dtype) for z in lands]),
        in_specs=[HBM_SPEC] * (2 * n) + [SEM_SPEC, SEM_SPEC, ANY], out_specs=tuple([HBM_SPEC] * (2 * n)),
        input_output_aliases={a: a for a in range(2 * n)},
        compiler_params=pltpu.CompilerParams(has_side_effects=EFFECT),
    )(*hs, *lands, send_sems, recv_sems, after)
    return out[:n], out[n:]


def _adam(w, g, m, v):
    m = ADAM_B1 * m + (1.0 - ADAM_B1) * g
    v = ADAM_B2 * v + (1.0 - ADAM_B2) * (g * g)
    m_hat = m / (1.0 - ADAM_B1 ** ADAM_STEP)
    v_hat = v / (1.0 - ADAM_B2 ** ADAM_STEP)
    delta = -ADAM_LR * (m_hat / (jnp.sqrt(v_hat) + ADAM_EPS) + ADAM_WD * w)
    return delta, m, v


def adamw_owned(chip, h, got, w, m, v, name):
    rows, cols = w.shape
    tr = _row_tile(rows, 256, 16)

    def body(chip_ref, h_ref, got_ref, w_ref, m_ref, v_ref, g_out, d_out, m_out, v_out):
        g = h_ref[...].astype(F32)
        for k in range(N_CHIP - 1):
            g = g + got_ref[k].astype(F32)
        d, mn, vn = _adam(w_ref[...], g, m_ref[...], v_ref[...])
        g_out[...] = g
        d_out[...] = d
        m_out[...] = mn
        v_out[...] = vn

    blk = pl.BlockSpec((tr, cols), lambda i, c: (i, 0))
    return pl.pallas_call(
        body, name=name, out_shape=[_sds((rows, cols), F32)] * 4,
        grid_spec=pltpu.PrefetchScalarGridSpec(
            num_scalar_prefetch=1, grid=(rows // tr,),
            in_specs=[pl.BlockSpec((None, tr, cols), lambda i, c: (c[0], i, 0)),
                      pl.BlockSpec((N_CHIP - 1, tr, cols), lambda i, c: (0, i, 0)), blk, blk, blk],
            out_specs=[blk] * 4),
        compiler_params=_params("parallel"),
    )(chip, h, got, w, m, v)


def owned_sum(chip, h, got, name):
    _, rows, cols = h.shape
    tr, tc = _tile2(rows, cols, 16)

    def body(chip_ref, h_ref, got_ref, g_out):
        g = h_ref[...].astype(F32)
        for k in range(N_CHIP - 1):
            g = g + got_ref[k].astype(F32)
        g_out[...] = g

    return pl.pallas_call(
        body, name=name, out_shape=_sds((rows, cols), F32),
        grid_spec=pltpu.PrefetchScalarGridSpec(
            num_scalar_prefetch=1, grid=(rows // tr, cols // tc),
            in_specs=[pl.BlockSpec((None, tr, tc), lambda i, j, c: (c[0], i, j)),
                      pl.BlockSpec((N_CHIP - 1, tr, tc), lambda i, j, c: (0, i, j))],
            out_specs=pl.BlockSpec((tr, tc), lambda i, j, c: (i, j))),
        compiler_params=_params("parallel", "parallel"),
    )(chip, h, got)


def adamw_sum(parts, w, m, v, name):
    n_parts, rows, cols = parts.shape
    tr = _row_tile(rows, 256, 8)

    def body(p_ref, w_ref, m_ref, v_ref, g_out, d_out, m_out, v_out):
        g = p_ref[0]
        for k in range(1, n_parts):
            g = g + p_ref[k]
        d, mn, vn = _adam(w_ref[...], g, m_ref[...], v_ref[...])
        g_out[...] = g
        d_out[...] = d
        m_out[...] = mn
        v_out[...] = vn

    blk = pl.BlockSpec((tr, cols), lambda i: (i, 0))
    return pl.pallas_call(
        body, name=name, out_shape=[_sds((rows, cols), F32)] * 4, grid=(rows // tr,),
        in_specs=[pl.BlockSpec((n_parts, tr, cols), lambda i: (0, i, 0)), blk, blk, blk],
        out_specs=[blk] * 4, compiler_params=_params("parallel"),
    )(parts, w, m, v)


def adaln_cols(c_all, w, b, name):
    d, n = w.shape
    tn = _row_tile(n, 768, LANE)

    def body(c_ref, w_ref, b_ref, o_ref):
        c = c_ref[...]
        o_ref[...] = jnp.dot(c * _sigmoid(c), w_ref[...], preferred_element_type=F32, precision=HI) + b_ref[...]

    return pl.pallas_call(
        body, name=name, out_shape=_sds((N_DEV, n), F32), grid=(n // tn,),
        in_specs=[pl.BlockSpec((N_DEV, d), lambda j: (0, 0)), pl.BlockSpec((d, tn), lambda j: (0, j)),
                  pl.BlockSpec((1, tn), lambda j: (0, j))],
        out_specs=pl.BlockSpec((N_DEV, tn), lambda j: (0, j)), compiler_params=_params("parallel"),
    )(c_all, w, b)


def adaln_wgrad(c_all, dmod_cols, name):
    d = c_all.shape[1]
    n = dmod_cols.shape[1]
    tn = _row_tile(n, 768, LANE)

    def body(c_ref, g_ref, o_ref):
        c = c_ref[...]
        o_ref[...] = lax.dot_general(c * _sigmoid(c), g_ref[...], (((0,), (0,)), ((), ())),
                                     preferred_element_type=F32, precision=HI)

    return pl.pallas_call(
        body, name=name, out_shape=_sds((d, n), F32), grid=(n // tn,),
        in_specs=[pl.BlockSpec((N_DEV, d), lambda j: (0, 0)), pl.BlockSpec((N_DEV, tn), lambda j: (0, j))],
        out_specs=pl.BlockSpec((d, tn), lambda j: (0, j)), compiler_params=_params("parallel"),
    )(c_all, dmod_cols)


def _modulate(h, sh, sc):
    return (h * (1.0 + sc) + sh).astype(BF16)


def ffn_in(h, sh, sc, w3, name):
    t, d = h.shape
    nb, _, bw = w3.shape
    half = nb // 2
    tm = _row_tile(t, 256, 16)

    def body(h_ref, sh_ref, sc_ref, wa_ref, wb_ref, a_ref, b_ref, s_ref):
        u = _modulate(h_ref[...], sh_ref[...], sc_ref[...])
        a = _dot(u, wa_ref[...])
        b = _dot(u, wb_ref[...])
        a_ref[...] = a.astype(BF16)
        b_ref[...] = b.astype(BF16)
        s_ref[...] = (a * _sigmoid(a) * b).astype(BF16)

    vec = pl.BlockSpec((1, d), lambda j, i: (0, 0))
    out = pl.BlockSpec((tm, bw), lambda j, i: (i, j))
    return pl.pallas_call(
        body, name=name, out_shape=[_sds((t, half * bw), BF16)] * 3, grid=(half, t // tm),
        in_specs=[pl.BlockSpec((tm, d), lambda j, i: (i, 0)), vec, vec,
                  pl.BlockSpec((None, d, bw), lambda j, i: (j, 0, 0)),
                  pl.BlockSpec((None, d, bw), lambda j, i: (j + half, 0, 0))],
        out_specs=[out] * 3, compiler_params=_params("parallel", "parallel"),
    )(h, sh, sc, w3, w3)


def mod_matmul(h, sh, sc, wt, bw, name):
    t, d = h.shape
    n = wt.shape[0]
    tm = _row_tile(t, 256, 16)

    def body(h_ref, sh_ref, sc_ref, w_ref, o_ref):
        o_ref[...] = _dot_nt(_modulate(h_ref[...], sh_ref[...], sc_ref[...]), w_ref[...]).astype(BF16)

    vec = pl.BlockSpec((1, d), lambda j, i: (0, 0))
    return pl.pallas_call(
        body, name=name, out_shape=_sds((t, n), BF16), grid=(n // bw, t // tm),
        in_specs=[pl.BlockSpec((tm, d), lambda j, i: (i, 0)), vec, vec, pl.BlockSpec((bw, d), lambda j, i: (j, 0))],
        out_specs=pl.BlockSpec((tm, bw), lambda j, i: (i, j)), compiler_params=_params("parallel", "parallel"),
    )(h, sh, sc, wt)


def out_ln(s, w, hin, gmod, ln_g, ln_b, coef, name):
    t, kdim = s.shape
    d = w.shape[1]
    tm = _row_tile(t, 512, 16)
    tk = _row_tile(kdim, 512, LANE)
    nk = kdim // tk

    def body(s_ref, w_ref, hin_ref, gm_ref, g_ref, b_ref, f_ref, z_ref, h_ref, acc):
        k = pl.program_id(1)

        @pl.when(k == 0)
        def _():
            acc[...] = jnp.zeros_like(acc)

        acc[...] += _dot(s_ref[...], w_ref[...])

        @pl.when(k == nk - 1)
        def _():
            f = acc[...]
            z = ALPHA * hin_ref[...] + (coef * gm_ref[...]) * f
            mu = jnp.mean(z, axis=-1, keepdims=True)
            zc = z - mu
            var = jnp.mean(zc * zc, axis=-1, keepdims=True)
            f_ref[...] = f.astype(BF16)
            z_ref[...] = z
            h_ref[...] = zc * lax.rsqrt(var + LN_EPS) * g_ref[...] + b_ref[...]

    vec = pl.BlockSpec((1, d), lambda i, k: (0, 0))
    row = pl.BlockSpec((tm, d), lambda i, k: (i, 0))
    return pl.pallas_call(
        body, name=name, out_shape=[_sds((t, d), BF16), _sds((t, d), F32), _sds((t, d), F32)],
        grid=(t // tm, nk),
        in_specs=[pl.BlockSpec((tm, tk), lambda i, k: (i, k)), pl.BlockSpec((tk, d), lambda i, k: (k, 0)),
                  row, vec, vec, vec],
        out_specs=[row, row, row], scratch_shapes=[pltpu.VMEM((tm, d), F32)],
        compiler_params=_params("parallel", "arbitrary"),
    )(s, w, hin, gmod, ln_g, ln_b)


def ln_bwd(dh, z, f, ln_g, gmod, coef, name, target=None):
    t, d = z.shape
    tm = _row_tile(t, 256, 16)
    head = target is not None

    def body(*refs):
        if head:
            dh_ref, tg_ref, z_ref, f_ref, g_ref, gm_ref, dz_ref, df_ref, dg_ref, db_ref, dgm_ref, loss_ref = refs
        else:
            dh_ref, z_ref, f_ref, g_ref, gm_ref, dz_ref, df_ref, dg_ref, db_ref, dgm_ref = refs
        i = pl.program_id(0)

        @pl.when(i == 0)
        def _():
            dg_ref[...] = jnp.zeros_like(dg_ref)
            db_ref[...] = jnp.zeros_like(db_ref)
            dgm_ref[...] = jnp.zeros_like(dgm_ref)
            if head:
                loss_ref[...] = jnp.zeros_like(loss_ref)

        dh = dh_ref[...]
        if head:
            err = dh - tg_ref[...]
            loss_ref[...] += 0.5 * jnp.sum(jnp.mean(err * err, axis=-1, keepdims=True))
            dh = err / d
        zv = z_ref[...]
        mu = jnp.mean(zv, axis=-1, keepdims=True)
        zc = zv - mu
        rstd = lax.rsqrt(jnp.mean(zc * zc, axis=-1, keepdims=True) + LN_EPS)
        xhat = zc * rstd
        dxh = dh * g_ref[...]
        dz = rstd * (dxh - jnp.mean(dxh, axis=-1, keepdims=True)
                     - xhat * jnp.mean(dxh * xhat, axis=-1, keepdims=True))
        dz_ref[...] = dz
        df_ref[...] = ((coef * gm_ref[...]) * dz).astype(BF16)
        dg_ref[...] += _colsum(dh * xhat)
        db_ref[...] += _colsum(dh)
        dgm_ref[...] += _colsum(coef * f_ref[...].astype(F32) * dz)

    vec = pl.BlockSpec((1, d), lambda i: (0, 0))
    row = pl.BlockSpec((tm, d), lambda i: (i, 0))
    ins = [dh] + ([target] if head else []) + [z, f, ln_g, gmod]
    in_specs = [row] + ([row] if head else []) + [row, row, vec, vec]
    out_shape = [_sds((t, d), F32), _sds((t, d), BF16)] + [_sds((1, d), F32)] * 3
    out_specs = [row, row, vec, vec, vec]
    if head:
        out_shape.append(_sds((1, LANE), F32))
        out_specs.append(pl.BlockSpec((1, LANE), lambda i: (0, 0)))
    return pl.pallas_call(
        body, name=name, out_shape=out_shape, grid=(t // tm,), in_specs=in_specs, out_specs=out_specs,
        compiler_params=_params("arbitrary"),
    )(*ins)


def ffn_bwd_act(df, w, a, b, name):
    t, d = df.shape
    fdim = w.shape[0]
    bw = fdim // (N_DEV // 2)
    tm = _row_tile(t, 256, 16)

    def body(df_ref, w_ref, a_ref, b_ref, o_ref):
        ds = _dot_nt(df_ref[...], w_ref[...])
        av = a_ref[...].astype(F32)
        sg = _sigmoid(av)
        o_ref[0] = (ds * b_ref[...].astype(F32) * (sg * (1.0 + av * (1.0 - sg)))).astype(BF16)
        o_ref[1] = (ds * (av * sg)).astype(BF16)

    act = pl.BlockSpec((tm, bw), lambda j, i: (i, j))
    return pl.pallas_call(
        body, name=name, out_shape=_sds((2, t, fdim), BF16), grid=(fdim // bw, t // tm),
        in_specs=[pl.BlockSpec((tm, d), lambda j, i: (i, 0)), pl.BlockSpec((bw, d), lambda j, i: (j, 0)), act, act],
        out_specs=pl.BlockSpec((2, tm, bw), lambda j, i: (0, i, j)),
        compiler_params=_params("parallel", "parallel"),
    )(df, w, a, b)


def matmul_tn(name, a, a_block, a_map, b, b_block, b_map, out_shape, o_block, o_map, n_out, mod=None,
              mod_b=False):
    tk = [s for s in a_block if s is not None][0]
    nk = a.shape[-2] // tk
    m, nn = [s for s in o_block if s is not None]

    def body(*refs):
        if mod is None:
            a_ref, b_ref, o_ref, acc = refs
        else:
            a_ref, sh_ref, sc_ref, b_ref, o_ref, acc = refs
        k = pl.program_id(1)

        @pl.when(k == 0)
        def _():
            acc[...] = jnp.zeros_like(acc)

        av, bv = a_ref[...], b_ref[...]
        if mod is not None and mod_b:
            bv = _modulate(bv, sh_ref[...], sc_ref[...])
        elif mod is not None:
            av = _modulate(av, sh_ref[...], sc_ref[...])
        acc[...] += _dot_tn(av, bv)

        @pl.when(k == nk - 1)
        def _():
            o_ref[...] = acc[...].astype(o_ref.dtype)

    ins = [a] + (list(mod) if mod is not None else []) + [b]
    in_specs = [pl.BlockSpec(a_block, a_map)]
    if mod is not None:
        vec = pl.BlockSpec((1, mod[0].shape[1]), lambda n, k: (0, 0))
        in_specs += [vec, vec]
    in_specs.append(pl.BlockSpec(b_block, b_map))
    return pl.pallas_call(
        body, name=name, out_shape=out_shape, grid=(n_out, nk), in_specs=in_specs,
        out_specs=pl.BlockSpec(o_block, o_map), scratch_shapes=[pltpu.VMEM((m, nn), F32)],
        compiler_params=_params("parallel", "arbitrary"),
    )(*ins)


def matmul_nt_blocks(name, dy, dy_block, dy_map, w3, t, resid=None):
    tm, bw = [s for s in dy_block if s is not None]
    rows = w3.ndim == 2
    if rows:
        nk, n = w3.shape[0] // bw, w3.shape[1]
    else:
        nk, n, _ = w3.shape

    def body(*refs):
        if resid is None:
            dy_ref, w_ref, o_ref, acc = refs
        else:
            dy_ref, w_ref, dz_ref, hin_ref, sc_ref, o_ref, dsc_ref, dsh_ref, acc = refs
        i, k = pl.program_id(0), pl.program_id(1)

        @pl.when(k == 0)
        def _():
            acc[...] = jnp.zeros_like(acc)

        if resid is not None:
            @pl.when((k == 0) & (i == 0))
            def _():
                dsc_ref[...] = jnp.zeros_like(dsc_ref)
                dsh_ref[...] = jnp.zeros_like(dsh_ref)

        acc[...] += _dot(dy_ref[...], w_ref[...]) if rows else _dot_nt(dy_ref[...], w_ref[...])

        @pl.when(k == nk - 1)
        def _():
            du = acc[...]
            if resid is None:
                o_ref[...] = du.astype(o_ref.dtype)
            else:
                o_ref[...] = ALPHA * dz_ref[...] + du * (1.0 + sc_ref[...])
                dsc_ref[...] += _colsum(du * hin_ref[...])
                dsh_ref[...] += _colsum(du)

    row = pl.BlockSpec((tm, n), lambda i, k: (i, 0))
    vec = pl.BlockSpec((1, n), lambda i, k: (0, 0))
    w_spec = pl.BlockSpec((bw, n), lambda i, k: (k, 0)) if rows else pl.BlockSpec((None, n, bw), lambda i, k: (k, 0, 0))
    in_specs = [pl.BlockSpec(dy_block, dy_map), w_spec]
    ins = [dy, w3]
    if resid is None:
        out_shape, out_specs = _sds((t, n), BF16), row
    else:
        ins += list(resid)
        in_specs += [row, row, vec]
        out_shape = [_sds((t, n), F32), _sds((1, n), F32), _sds((1, n), F32)]
        out_specs = [row, vec, vec]
    return pl.pallas_call(
        body, name=name, out_shape=out_shape, grid=(t // tm, nk), in_specs=in_specs, out_specs=out_specs,
        scratch_shapes=[pltpu.VMEM((tm, n), F32)], compiler_params=_params("arbitrary", "arbitrary"),
    )(*ins)


REL_W = KW + QB


def bias_table(rel_bias):
    nh, n_rel = rel_bias.shape
    lo = KW - QB - REL_CLIP
    hi = KW - lo - n_rel
    assert n_rel == REL_CLIP + CHUNK and lo >= 0 and hi >= 0
    first, last = rel_bias[:, :1], rel_bias[:, -1:]
    row = jnp.concatenate([jnp.broadcast_to(first, (nh, lo)), rel_bias, jnp.broadcast_to(last, (nh, hi)),
                           jnp.broadcast_to(first, (nh, QB))], axis=1)
    table = jnp.tile(row, (1, QB))[:, :QB * (REL_W - 1)].reshape(nh, QB, REL_W - 1)[:, :, :KW]
    q = np.arange(QB)[:, None] // CHUNK
    k = np.arange(KW)[None, :] // CHUNK
    band = (k >= q) & (k <= q + A_PAST_CHUNKS)
    return jnp.where(band[None], table, NEG)


def bias_grad_skew(dbias):
    nh = dbias.shape[0]
    flat = jnp.pad(dbias, ((0, 0), (0, 0), (0, REL_W - 1 - KW))).reshape(nh, QB * (REL_W - 1))
    return jnp.pad(flat, ((0, 0), (0, QB))).reshape(nh, QB, REL_W)


def bias_clip_map(n_rel):
    m = np.arange(REL_W)
    dist = np.where(m < KW, m, m - REL_W) - (KW - QB)
    idx = np.clip(dist, -REL_CLIP, CHUNK - 1) + REL_CLIP
    return (idx[:, None] == np.arange(n_rel)[None, :]).astype(np.float32)


def _kv_specs(dh, order):
    if order == "hi":
        return [pl.BlockSpec((None, QB, dh), functools.partial(lambda r, h, i: (h, jnp.maximum(i - r, 0), 0), r))
                for r in (2, 1, 0)]
    raise ValueError(order)


def _scores(q, ks, bias, i, scale):
    s = jnp.concatenate([_dot_nt(q, kk) for kk in ks], axis=1) * scale + bias
    col = lax.broadcasted_iota(jnp.int32, s.shape, 1)
    return jnp.where(col >= (2 - i) * QB, s, NEG)


def attn_fwd(q, k, v, bias, name):
    nh, t, dh = q.shape
    scale = dh ** -0.5

    def body(q_ref, k0, k1, k2, v0, v1, v2, b_ref, o_ref, lse_ref):
        i = pl.program_id(1)
        s = _scores(q_ref[...], [k0[...], k1[...], k2[...]], b_ref[...], i, scale)
        m = jnp.max(s, axis=-1, keepdims=True)
        p = jnp.exp(s - m)
        l = jnp.sum(p, axis=-1, keepdims=True)
        pb = p.astype(BF16)
        o = sum(_dot(pb[:, r * QB:(r + 1) * QB], vv[...]) for r, vv in enumerate((v0, v1, v2)))
        o_ref[...] = (o / l).astype(BF16)
        lse_ref[...] = m + jnp.log(l)

    qs = pl.BlockSpec((None, QB, dh), lambda h, i: (h, i, 0))
    st = pl.BlockSpec((None, QB, 1), lambda h, i: (h, i, 0))
    kv = _kv_specs(dh, "hi")
    return pl.pallas_call(
        body, name=name, out_shape=[_sds((nh, t, dh), BF16), _sds((nh, t, 1), F32)], grid=(nh, t // QB),
        in_specs=[qs] + kv + kv + [pl.BlockSpec((None, QB, KW), lambda h, i: (h, 0, 0))],
        out_specs=[qs, st], compiler_params=_params("parallel", "parallel"),
    )(q, k, k, k, v, v, v, bias)


def attn_bwd_q(q, k, v, bias, lse, do, name):
    nh, t, dh = q.shape
    scale = dh ** -0.5

    def body(q_ref, k0, k1, k2, v0, v1, v2, b_ref, lse_ref, do_ref, dq_ref, dl_ref, db_ref):
        i = pl.program_id(1)
        ks = [k0[...], k1[...], k2[...]]
        s = _scores(q_ref[...], ks, b_ref[...], i, scale)
        p = jnp.exp(s - lse_ref[...])
        dov = do_ref[...]
        dp = jnp.concatenate([_dot_nt(dov, vv[...]) for vv in (v0, v1, v2)], axis=1)
        delta = jnp.sum(p * dp, axis=-1, keepdims=True)
        ds = p * (dp - delta)
        dsb = ds.astype(BF16)
        dq = sum(_dot(dsb[:, r * QB:(r + 1) * QB], kk) for r, kk in enumerate(ks))
        dq_ref[...] = (dq * scale).astype(BF16)
        dl_ref[...] = delta

        @pl.when(i == 0)
        def _():
            db_ref[...] = jnp.zeros_like(db_ref)

        db_ref[...] += ds

    qs = pl.BlockSpec((None, QB, dh), lambda h, i: (h, i, 0))
    st = pl.BlockSpec((None, QB, 1), lambda h, i: (h, i, 0))
    tab = pl.BlockSpec((None, QB, KW), lambda h, i: (h, 0, 0))
    kv = _kv_specs(dh, "hi")
    return pl.pallas_call(
        body, name=name,
        out_shape=[_sds((nh, t, dh), BF16), _sds((nh, t, 1), F32), _sds((nh, QB, KW), F32)],
        grid=(nh, t // QB), in_specs=[qs] + kv + kv + [tab, st, qs], out_specs=[qs, st, tab],
        compiler_params=_params("parallel", "arbitrary"),
    )(q, k, k, k, v, v, v, bias, lse, do)


def attn_bwd_kv(q, k, v, bias, lse, delta, do, name):
    nh, t, dh = q.shape
    nb = t // QB
    scale = dh ** -0.5

    def body(k_ref, v_ref, b_ref, *refs):
        qr, lr, dr, gr = refs[0:3], refs[3:6], refs[6:9], refs[9:12]
        dk_ref, dv_ref = refs[12:]
        j = pl.program_id(1)
        kk, vv = k_ref[...], v_ref[...]
        dk = jnp.zeros((QB, dh), F32)
        dv = jnp.zeros((QB, dh), F32)
        for r in range(3):
            seg = 2 - r
            qv, dov = qr[r][...], gr[r][...]
            s = _dot_nt(qv, kk) * scale + b_ref[:, seg * QB:(seg + 1) * QB]
            p = jnp.where(j + r < nb, jnp.exp(s - lr[r][...]), 0.0)
            ds = p * (_dot_nt(dov, vv) - dr[r][...])
            dv = dv + _dot_tn(p.astype(BF16), dov)
            dk = dk + _dot_tn(ds.astype(BF16), qv)
        dk_ref[...] = (dk * scale).astype(BF16)
        dv_ref[...] = dv.astype(BF16)

    def later(width):
        return [pl.BlockSpec((None, QB, width),
                             functools.partial(lambda r, h, j: (h, jnp.minimum(j + r, nb - 1), 0), r))
                for r in range(3)]

    own = pl.BlockSpec((None, QB, dh), lambda h, j: (h, j, 0))
    return pl.pallas_call(
        body, name=name, out_shape=[_sds((nh, t, dh), BF16)] * 2, grid=(nh, nb),
        in_specs=[own, own, pl.BlockSpec((None, QB, KW), lambda h, j: (h, 0, 0))]
        + later(dh) + later(1) + later(1) + later(dh),
        out_specs=[own, own], compiler_params=_params("parallel", "parallel"),
    )(k, v, bias, q, q, q, lse, lse, lse, delta, delta, delta, do, do, do)


def _tri(strict):
    r = lax.broadcasted_iota(jnp.int32, (CHUNK, CHUNK), 0)
    c = lax.broadcasted_iota(jnp.int32, (CHUNK, CHUNK), 1)
    return jnp.where((c < r) if strict else (c <= r), 1.0, 0.0).astype(F32)


def _gate(lr, wa, ba):
    y = _dot(lr, wa) + ba
    return (jnp.minimum(y, 0.0) - jnp.log(1.0 + jnp.exp(-jnp.abs(y)))) / GATE_TAU, y


def _decays(la):
    cum = jnp.dot(_tri(False), la, preferred_element_type=F32, precision=HI)
    last = cum[CHUNK - 1:CHUNK, :]
    return jnp.exp(last - cum), jnp.exp(last)


def _gla_specs(cols, hk, hv, order):
    def at(start, width):
        return pl.BlockSpec((GB, width), lambda h, i: (order(i), start // width + h))
    return [at(cols["qb"], hk), at(cols["kb"], hk), at(cols["vb"], hv), at(cols["rb"], hv),
            pl.BlockSpec((GB, LANE), lambda h, i: (order(i), cols["lr"] // LANE))]


def gla_fwd(p, cols, wa, ba, gn, nh, hk, hv, name):
    t = p.shape[0]
    nc = t // CHUNK
    scale = hk ** -0.5
    per = GB // CHUNK

    def body(q_ref, k_ref, v_ref, r_ref, lr_ref, wa_ref, ba_ref, gn_ref, o_ref, y_ref, st_ref, state):
        @pl.when(pl.program_id(1) == 0)
        def _():
            state[...] = jnp.zeros_like(state)

        for c in range(per):
            rows = pl.ds(c * CHUNK, CHUNK)
            la, _ = _gate(lr_ref[rows, :], wa_ref[...], ba_ref[...])
            w, decay = _decays(la)
            kdec = (k_ref[rows, :].astype(F32) * w).astype(BF16)
            st = decay * state[...] + _dot_tn(v_ref[rows, :], kdec)
            state[...] = st
            st_ref[c] = st
            o = _dot_nt(q_ref[rows, :], st.astype(BF16)) * scale
            o_ref[rows, :] = o
            rinv = lax.rsqrt(jnp.mean(o * o, axis=-1, keepdims=True) + RMS_EPS)
            rv = r_ref[rows, :].astype(F32)
            y_ref[rows, :] = (o * rinv * gn_ref[...] * (rv * _sigmoid(rv))).astype(BF16)

    return pl.pallas_call(
        body, name=name,
        out_shape=[_sds((t, nh * hv), F32), _sds((t, nh * hv), BF16), _sds((nh, nc, hv, hk), F32)],
        grid=(nh, t // GB),
        in_specs=_gla_specs(cols, hk, hv, lambda i: i)
        + [pl.BlockSpec((LANE, hk), lambda h, i: (0, h)), pl.BlockSpec((1, hk), lambda h, i: (0, h)),
           pl.BlockSpec((1, hv), lambda h, i: (0, 0))],
        out_specs=[pl.BlockSpec((GB, hv), lambda h, i: (i, h)), pl.BlockSpec((GB, hv), lambda h, i: (i, h)),
                   pl.BlockSpec((None, per, hv, hk), lambda h, i: (h, i, 0, 0))],
        scratch_shapes=[pltpu.VMEM((hv, hk), F32)], compiler_params=_params("parallel", "arbitrary"),
    )(p, p, p, p, p, wa, ba, gn)


def gla_bwd(p, cols, wa, ba, gn, o, states, dy, nh, hk, hv, name):
    t = p.shape[0]
    nblk = t // GB
    scale = hk ** -0.5
    per = GB // CHUNK

    def rev(i):
        return nblk - 1 - i

    def body(q_ref, k_ref, v_ref, r_ref, lr_ref, wa_ref, ba_ref, gn_ref, o_ref, st_ref, sp_ref, dy_ref,
             dq_ref, dk_ref, dv_ref, dr_ref, dg_ref, dgn_ref, carry):
        h, i = pl.program_id(0), pl.program_id(1)

        @pl.when(i == 0)
        def _():
            carry[...] = jnp.zeros_like(carry)

        @pl.when((i == 0) & (h == 0))
        def _():
            dgn_ref[...] = jnp.zeros_like(dgn_ref)

        gnv = gn_ref[...]
        for c in reversed(range(per)):
            rows = pl.ds(c * CHUNK, CHUNK)
            rv = r_ref[rows, :].astype(F32)
            sg = _sigmoid(rv)
            dyv = dy_ref[rows, :].astype(F32)
            ov = o_ref[rows, :]
            rinv = lax.rsqrt(jnp.mean(ov * ov, axis=-1, keepdims=True) + RMS_EPS)
            dn = dyv * (rv * sg)
            dr_ref[rows, :] = (dyv * (ov * rinv * gnv) * (sg * (1.0 + rv * (1.0 - sg)))).astype(BF16)
            dgn_ref[...] += _colsum(dn * ov * rinv)
            dxh = dn * gnv
            do = rinv * dxh - ov * (rinv * rinv * rinv) * jnp.mean(dxh * ov, axis=-1, keepdims=True)
            dob = (do * scale).astype(BF16)
            qv, kv, vv = q_ref[rows, :], k_ref[rows, :], v_ref[rows, :]
            dq_ref[rows, :] = _dot(dob, st_ref[c].astype(BF16)).astype(BF16)
            dst = carry[...] + _dot_tn(dob, qv)
            if c > 0:
                prev = st_ref[c - 1]
            else:
                prev = jnp.where(i == nblk - 1, 0.0, sp_ref[0])
            ddecay = _colsum(dst * prev)
            la, y = _gate(lr_ref[rows, :], wa_ref[...], ba_ref[...])
            w, decay = _decays(la)
            kf = kv.astype(F32)
            kdec = (kf * w).astype(BF16)
            dstb = dst.astype(BF16)
            dkdec = _dot(vv, dstb)
            dv_ref[rows, :] = _dot_nt(kdec, dstb).astype(BF16)
            dk_ref[rows, :] = (dkdec * w).astype(BF16)
            e = dkdec * kf * w
            dla = jnp.dot(_tri(True), e, preferred_element_type=F32, precision=HI) + ddecay * decay
            dg_ref[rows, :] = dla * (1.0 / GATE_TAU) * _sigmoid(-y)
            carry[...] = decay * dst

    per_head = lambda width: pl.BlockSpec((GB, width), lambda h, i: (rev(i), h))
    return pl.pallas_call(
        body, name=name,
        out_shape=[_sds((t, nh * hk), BF16), _sds((t, nh * hk), BF16), _sds((t, nh * hv), BF16),
                   _sds((t, nh * hv), BF16), _sds((t, nh * hk), F32), _sds((1, hv), F32)],
        grid=(nh, nblk),
        in_specs=_gla_specs(cols, hk, hv, rev)
        + [pl.BlockSpec((LANE, hk), lambda h, i: (0, h)), pl.BlockSpec((1, hk), lambda h, i: (0, h)),
           pl.BlockSpec((1, hv), lambda h, i: (0, 0)), per_head(hv),
           pl.BlockSpec((None, per, hv, hk), lambda h, i: (h, rev(i), 0, 0)),
           pl.BlockSpec((None, 1, hv, hk), lambda h, i: (h, jnp.maximum(rev(i) * per - 1, 0), 0, 0)),
           per_head(hv)],
        out_specs=[per_head(hk), per_head(hk), per_head(hv), per_head(hv), per_head(hk),
                   pl.BlockSpec((1, hv), lambda h, i: (0, 0))],
        scratch_shapes=[pltpu.VMEM((hv, hk), F32)], compiler_params=_params("arbitrary", "arbitrary"),
    )(p, p, p, p, p, wa, ba, gn, o, states, states, dy)


def gate_bwd(p, lr_col, dg, wa, name):
    t, kd = dg.shape
    tm = _row_tile(t, 512, 16)

    def body(lr_ref, dg_ref, wa_ref, dlr_ref, dwa_ref, dba_ref):
        @pl.when(pl.program_id(0) == 0)
        def _():
            dwa_ref[...] = jnp.zeros_like(dwa_ref)
            dba_ref[...] = jnp.zeros_like(dba_ref)

        g = dg_ref[...]
        gb = g.astype(BF16)
        dlr_ref[...] = _dot_nt(gb, wa_ref[...]).astype(BF16)
        dwa_ref[...] += _dot_tn(lr_ref[...], gb)
        dba_ref[...] += _colsum(g)

    return pl.pallas_call(
        body, name=name, out_shape=[_sds((t, LANE), BF16), _sds((LANE, kd), F32), _sds((1, kd), F32)],
        grid=(t // tm,),
        in_specs=[pl.BlockSpec((tm, LANE), lambda i: (i, lr_col // LANE)), pl.BlockSpec((tm, kd), lambda i: (i, 0)),
                  pl.BlockSpec((LANE, kd), lambda i: (0, 0))],
        out_specs=[pl.BlockSpec((tm, LANE), lambda i: (i, 0)), pl.BlockSpec((LANE, kd), lambda i: (0, 0)),
                   pl.BlockSpec((1, kd), lambda i: (0, 0))],
        compiler_params=_params("arbitrary"),
    )(p, dg, wa)


def proj_merge(ya, yb, wa3, wb3, p, ga_col, gb_col, name):
    t, kd = ya.shape
    nb, _, bw = wa3.shape
    tm = _row_tile(t, 512, 16)

    def body(ya_ref, yb_ref, wa_ref, wb_ref, ga_ref, gb_ref, pa_ref, pb_ref, mg_ref):
        pa = _dot(ya_ref[...], wa_ref[...])
        pb = _dot(yb_ref[...], wb_ref[...])
        pa_ref[...] = pa.astype(BF16)
        pb_ref[...] = pb.astype(BF16)
        mg_ref[...] = (_sigmoid(ga_ref[...].astype(F32)) * pa + _sigmoid(gb_ref[...].astype(F32)) * pb).astype(BF16)

    act = pl.BlockSpec((tm, kd), lambda j, i: (i, 0))
    wsp = pl.BlockSpec((None, kd, bw), lambda j, i: (j, 0, 0))
    out = pl.BlockSpec((tm, bw), lambda j, i: (i, j))
    return pl.pallas_call(
        body, name=name, out_shape=[_sds((t, nb * bw), BF16)] * 3, grid=(nb, t // tm),
        in_specs=[act, act, wsp, wsp, pl.BlockSpec((tm, bw), lambda j, i: (i, ga_col // bw + j)),
                  pl.BlockSpec((tm, bw), lambda j, i: (i, gb_col // bw + j))],
        out_specs=[out] * 3, compiler_params=_params("parallel", "parallel"),
    )(ya, yb, wa3, wb3, p, p)


def merge_bwd(dm, w, p, ga_col, gb_col, pa, pb, name):
    t, d = dm.shape
    n = w.shape[0]
    tn = _row_tile(n, 512, LANE)
    tm = _row_tile(t, 256, 16)

    def body(dm_ref, w_ref, ga_ref, gb_ref, pa_ref, pb_ref, dpa_ref, dpb_ref, dga_ref, dgb_ref):
        dmg = _dot_nt(dm_ref[...], w_ref[...])
        sa = _sigmoid(ga_ref[...].astype(F32))
        sb = _sigmoid(gb_ref[...].astype(F32))
        dpa_ref[...] = (dmg * sa).astype(BF16)
        dpb_ref[...] = (dmg * sb).astype(BF16)
        dga_ref[...] = (dmg * pa_ref[...].astype(F32) * sa * (1.0 - sa)).astype(BF16)
        dgb_ref[...] = (dmg * pb_ref[...].astype(F32) * sb * (1.0 - sb)).astype(BF16)

    out = pl.BlockSpec((tm, tn), lambda j, i: (i, j))
    return pl.pallas_call(
        body, name=name, out_shape=[_sds((t, n), BF16)] * 4, grid=(n // tn, t // tm),
        in_specs=[pl.BlockSpec((tm, d), lambda j, i: (i, 0)), pl.BlockSpec((tn, d), lambda j, i: (j, 0)),
                  pl.BlockSpec((tm, tn), lambda j, i: (i, ga_col // tn + j)),
                  pl.BlockSpec((tm, tn), lambda j, i: (i, gb_col // tn + j)), out, out],
        out_specs=[out] * 4, compiler_params=_params("parallel", "parallel"),
    )(dm, w, p, p, pa, pb)


def rel_bias_grad(skew, clip_map, name):
    nh, _, jd = skew.shape
    n_rel = clip_map.shape[1]

    def body(s_ref, c_ref, o_ref):
        sums = jnp.concatenate([_colsum(s_ref[h]) for h in range(nh)], axis=0)
        o_ref[...] = jnp.dot(sums, c_ref[...], preferred_element_type=F32, precision=HI)

    return pl.pallas_call(
        body, name=name, out_shape=_sds((nh, n_rel), F32), in_specs=[VMEM_SPEC, VMEM_SPEC], out_specs=VMEM_SPEC,
        compiler_params=pltpu.CompilerParams(vmem_limit_bytes=VMEM_LIMIT),
    )(skew, clip_map)


MIX_BLOCK = 9 * LANE


def mix_layout(d, a_width, bk, bv):
    main = 3 * a_width + 2 * bk + 2 * bv
    cols = {"qa": 0, "ka": a_width, "va": 2 * a_width, "qb": 3 * a_width, "kb": 3 * a_width + bk,
            "vb": 3 * a_width + 2 * bk, "rb": 3 * a_width + 2 * bk + bv, "ga": main, "gb": main + d,
            "lr": main + 2 * d}
    total = main + 2 * d + LANE
    assert total % MIX_BLOCK == 0
    return cols, main, total


def mix_weight_in(wt, main, rank):
    d = wt.shape[1]
    return jnp.concatenate([wt[:main], wt[main + rank:], wt[main:main + rank],
                            jnp.zeros((LANE - rank, d), wt.dtype)], axis=0)


def mix_weight_grad_out(gt, main, rank):
    d = gt.shape[1]
    w = jnp.concatenate([gt[:main], gt[main + 2 * d:main + 2 * d + rank], gt[main:main + 2 * d]], axis=0)
    return w.reshape(N_DEV, -1, d)


def heads_major(x, nh):
    t = x.shape[0]
    return x.reshape(t, nh, -1).transpose(1, 0, 2)


def heads_minor(x):
    nh, t, dh = x.shape
    return x.transpose(1, 0, 2).reshape(t, nh * dh)


def ffn_forward(h, sh, sc, g, w_in3, w_out, ln_g, ln_b, tag):
    a, b, s = ffn_in(h, sh, sc, w_in3, f"{tag}_in")
    f, z, hout = out_ln(s, w_out, h, g, ln_g, ln_b, 0.5, f"{tag}_out")
    return hout, (h, a, b, s, f, z)


def ffn_backward_weights(dh, saved, sh, sc, g, w_in3, w_out, ln_g, tag, target=None):
    hin, a, b, s, f, z = saved
    t, d = hin.shape
    nb, _, bw = w_in3.shape
    half = nb // 2
    fdim = w_out.shape[0]
    res = ln_bwd(dh, z, f, ln_g, g, 0.5, f"{tag}_ln_bwd", target=target)
    dz, df, dln_g, dln_b, dg = res[:5]
    dab = ffn_bwd_act(df, w_out, a, b, f"{tag}_act_bwd")
    tk = _row_tile(t, 512, 16)
    dw_out = matmul_tn(f"{tag}_dwout", s, (tk, bw), lambda n, k: (k, n), df, (tk, d), lambda n, k: (k, 0),
                       _sds((fdim, d), BF16), (bw, d), lambda n, k: (n, 0), fdim // bw)
    dw_in = matmul_tn(f"{tag}_dwin", hin, (tk, d), lambda n, k: (k, 0), dab, (None, tk, bw),
                      lambda n, k: (n // half, k, n % half), _sds((nb, d, bw), BF16), (None, d, bw),
                      lambda n, k: (n, 0, 0), nb, mod=(sh, sc))
    grads = dict(w_in=dw_in, w_out=dw_out.reshape(N_DEV, fdim // N_DEV, d), ln_g=dln_g, ln_b=dln_b, g=dg)
    return (dab, dz), grads, (res[5] if target is not None else None)


def ffn_backward_input(carry, saved, sc, w_in3, tag):
    dab, dz = carry
    hin = saved[0]
    t = hin.shape[0]
    nb, _, bw = w_in3.shape
    half = nb // 2
    tm = _row_tile(t, 256, 16)
    return matmul_nt_blocks(f"{tag}_du", dab, (None, tm, bw), lambda i, k: (k // half, i, k % half),
                            w_in3, t, resid=(dz, hin, sc))


def _after(v, token):
    return v if token is None else v + token[:1, :1]


def local_step(x, target, mod, weights_of, grads_ready, rel_bias, w_alpha2, b_alpha, gla_norm_g, lns):
    t, d = x.shape
    sh1, sc1, g1, sh2, sc2, g2, sh3, sc3, g3 = [mod[i:i + 1] for i in range(N_MOD)]
    ln1_g, ln1_b, ln2_g, ln2_b, ln3_g, ln3_b = lns
    n_heads_a, n_rel = rel_bias.shape
    rank, bk = w_alpha2.shape
    hv = gla_norm_g.shape[1]

    w1 = weights_of("ffn1", x)
    h1, saved1 = ffn_forward(x, sh1, sc1, g1, w1["in"], w1["out"], ln1_g, ln1_b, "ffn1")
    wm = weights_of("mix", h1)
    a_width = wm["proj_a"].shape[1]
    bv = wm["proj_b"].shape[1]
    nh_b = bv // hv
    hk = bk // nh_b
    cols, main, total = mix_layout(d, a_width, bk, bv)
    w_mix = mix_weight_in(wm["in_t"], main, rank)
    p = mod_matmul(h1, sh2, sc2, w_mix, MIX_BLOCK, "mix_in")
    bias = bias_table(rel_bias)
    qa, ka, va = [heads_major(p[:, cols[n]:cols[n] + a_width], n_heads_a) for n in ("qa", "ka", "va")]
    ya_h, lse = attn_fwd(qa, ka, va, bias, "attn_fwd")
    ya = heads_minor(ya_h)
    wa_pad = jnp.zeros((LANE, bk), BF16).at[:rank].set(w_alpha2.astype(BF16))
    o_b, yb, states = gla_fwd(p, cols, wa_pad, b_alpha, gla_norm_g, nh_b, hk, hv, "gla_fwd")
    pa, pb, merged = proj_merge(ya, yb, wm["proj_a"], wm["proj_b"], p, cols["ga"], cols["gb"], "proj_merge")
    m, z2, h2 = out_ln(merged, wm["out"], h1, g2, ln2_g, ln2_b, 1.0, "mix_out")
    w3 = weights_of("ffn2", h2)
    h3, saved3 = ffn_forward(h2, sh3, sc3, g3, w3["in"], w3["out"], ln3_g, ln3_b, "ffn2")

    carry3, gr3, loss = ffn_backward_weights(h3, saved3, sh3, sc3, g3, w3["in"], w3["out"], ln3_g, "ffn2",
                                             target=target)
    token = grads_ready("ffn2", dict(ffn2_in=gr3["w_in"], ffn2_out=gr3["w_out"]))
    dh2, dsc3, dsh3 = ffn_backward_input(carry3, saved3, _after(sc3, token), w3["in"], "ffn2")
    dz2, dm, dln2_g, dln2_b, dg2 = ln_bwd(dh2, z2, m, ln2_g, g2, 1.0, "mix_ln_bwd")
    dpa, dpb, dga, dgb = merge_bwd(dm, wm["out"], p, cols["ga"], cols["gb"], pa, pb, "merge_bwd")
    tk = _row_tile(t, 512, 16)
    dw_mix_out = matmul_tn("mix_dwout", merged, (tk, 512), lambda n, k: (k, n), dm, (tk, d), lambda n, k: (k, 0),
                           _sds((d, d), BF16), (512, d), lambda n, k: (n, 0), d // 512)
    tm = _row_tile(t, 512, 16)
    pbw = wm["proj_a"].shape[2]
    dya = matmul_nt_blocks("proj_a_dy", dpa, (tm, pbw), lambda i, k: (i, k), wm["proj_a"], t)
    dyb = matmul_nt_blocks("proj_b_dy", dpb, (tm, pbw), lambda i, k: (i, k), wm["proj_b"], t)
    dw_pa = matmul_tn("proj_a_dw", ya, (tk, a_width), lambda n, k: (k, 0), dpa, (tk, pbw), lambda n, k: (k, n),
                      _sds((N_DEV, a_width, pbw), BF16), (None, a_width, pbw), lambda n, k: (n, 0, 0), N_DEV)
    dw_pb = matmul_tn("proj_b_dw", yb, (tk, bv), lambda n, k: (k, 0), dpb, (tk, pbw), lambda n, k: (k, n),
                      _sds((N_DEV, bv, pbw), BF16), (None, bv, pbw), lambda n, k: (n, 0, 0), N_DEV)
    dqb, dkb, dvb, drb, dgate, dgn = gla_bwd(p, cols, wa_pad, b_alpha, gla_norm_g, o_b, states, dyb,
                                             nh_b, hk, hv, "gla_bwd")
    dlr, dwa_pad, dba = gate_bwd(p, cols["lr"], dgate, wa_pad, "gate_bwd")
    do_h = heads_major(dya, n_heads_a)
    dqa_h, delta, dbias = attn_bwd_q(qa, ka, va, bias, lse, do_h, "attn_bwd_q")
    dka_h, dva_h = attn_bwd_kv(qa, ka, va, bias, lse, delta, do_h, "attn_bwd_kv")
    d_rel = rel_bias_grad(bias_grad_skew(dbias), jnp.asarray(bias_clip_map(n_rel)), "rel_bias_grad")
    dp = jnp.concatenate([heads_minor(dqa_h), heads_minor(dka_h), heads_minor(dva_h), dqb, dkb, dvb, drb,
                          dga, dgb, dlr], axis=1)
    dw_mix_t = matmul_tn("mix_dwin", dp, (tk, MIX_BLOCK), lambda n, k: (k, n), h1, (tk, d), lambda n, k: (k, 0),
                         _sds((total, d), BF16), (MIX_BLOCK, d), lambda n, k: (n, 0), total // MIX_BLOCK,
                         mod=(sh2, sc2), mod_b=True)
    token = grads_ready("mix", dict(mix_in=mix_weight_grad_out(dw_mix_t, main, rank), proj_a=dw_pa, proj_b=dw_pb,
                                    mix_out=dw_mix_out.reshape(N_DEV, d // N_DEV, d)))
    tm = _row_tile(t, 256, 16)
    dh1, dsc2, dsh2 = matmul_nt_blocks("mix_du", dp, (tm, MIX_BLOCK), lambda i, k: (i, k), w_mix, t,
                                       resid=(dz2, h1, _after(sc2, token)))
    carry1, gr1, _ = ffn_backward_weights(dh1, saved1, sh1, sc1, g1, w1["in"], w1["out"], ln1_g, "ffn1")
    token = grads_ready("ffn1", dict(ffn1_in=gr1["w_in"], ffn1_out=gr1["w_out"]))
    dx, dsc1, dsh1 = ffn_backward_input(carry1, saved1, _after(sc1, token), w1["in"], "ffn1")

    dmod = [dsh1, dsc1, gr1["g"], dsh2, dsc2, dg2, dsh3, dsc3, gr3["g"]]
    small = dict(ln1_g=gr1["ln_g"], ln1_b=gr1["ln_b"], ln2_g=dln2_g, ln2_b=dln2_b, ln3_g=gr3["ln_g"],
                 ln3_b=gr3["ln_b"], b_alpha=dba, gla_norm_g=dgn, w_alpha2=dwa_pad[:rank], rel_bias=d_rel)
    return loss, dx, dmod, small


GROUPS = dict(ffn1=("ffn1_in", "ffn1_out"), mix=("mix_in", "proj_a", "proj_b", "mix_out"),
              ffn2=("ffn2_in", "ffn2_out"))
SMALL_REPLICATED = ("b_ada", "ln1_g", "ln1_b", "ln2_g", "ln2_b", "ln3_g", "ln3_b", "b_alpha", "gla_norm_g")
SMALL_SHARDED = ("rel_bias", "w_alpha2")
WEIGHT_ORDER = ("w_ada", "b_ada", "ffn1_w_in", "ffn1_w_out", "ln1_g", "ln1_b", "w_mix_in", "rel_bias", "w_alpha2",
                "b_alpha", "gla_norm_g", "w_proj_a", "w_proj_b", "w_mix_out", "ln2_g", "ln2_b", "ffn2_w_in",
                "ffn2_w_out", "ln3_g", "ln3_b")
BIG_NAME = dict(ffn1_in="ffn1_w_in", ffn1_out="ffn1_w_out", mix_in="w_mix_in", proj_a="w_proj_a",
                proj_b="w_proj_b", mix_out="w_mix_out", ffn2_in="ffn2_w_in", ffn2_out="ffn2_w_out")


def kernel(x, c, w_ada, b_ada, ffn1_w_in, ffn1_w_out, ln1_g, ln1_b, w_mix_in, rel_bias, w_alpha2, b_alpha, gla_norm_g, w_proj_a, w_proj_b, w_mix_out, ln2_g, ln2_b, ffn2_w_in, ffn2_w_out, ln3_g, ln3_b, loss_target, m_w_ada, m_b_ada, m_ffn1_w_in, m_ffn1_w_out, m_ln1_g, m_ln1_b, m_w_mix_in, m_rel_bias, m_w_alpha2, m_b_alpha, m_gla_norm_g, m_w_proj_a, m_w_proj_b, m_w_mix_out, m_ln2_g, m_ln2_b, m_ffn2_w_in, m_ffn2_w_out, m_ln3_g, m_ln3_b, v_w_ada, v_b_ada, v_ffn1_w_in, v_ffn1_w_out, v_ln1_g, v_ln1_b, v_w_mix_in, v_rel_bias, v_w_alpha2, v_b_alpha, v_gla_norm_g, v_w_proj_a, v_w_proj_b, v_w_mix_out, v_ln2_g, v_ln2_b, v_ffn2_w_in, v_ffn2_w_out, v_ln3_g, v_ln3_b):
    env = dict(locals())
    w = {n: env[n] for n in WEIGHT_ORDER}
    mom = {n: env["m_" + n] for n in WEIGHT_ORDER}
    var = {n: env["v_" + n] for n in WEIGHT_ORDER}
    me = _me()
    dev = _lin(me)
    core = jnp.reshape(me[2], (1,)).astype(jnp.int32)
    chip = jnp.reshape(2 * me[0] + me[1], (1,)).astype(jnp.int32)
    d = x.shape[-1]

    ada_cols = w_ada.shape[-1]
    c_all = all_gather_small(c, "gather_c")[:, 0, :]
    b_cols = lax.dynamic_slice_in_dim(b_ada, dev * ada_cols, ada_cols, axis=1)
    mod_cols = adaln_cols(c_all, w_ada[0], b_cols, "adaln_cols")
    mod_all = all_gather_small(mod_cols, "gather_mod")
    mod = lax.dynamic_index_in_dim(mod_all, dev, axis=1, keepdims=False).reshape(N_MOD, d)

    small_w = all_gather_small(jnp.concatenate([rel_bias[0], w_alpha2[0]], axis=1), "gather_small_w")
    n_rel_cols = rel_bias.shape[-1]
    rel_full = small_w[:, :, :n_rel_cols].transpose(1, 0, 2).reshape(small_w.shape[1], -1)
    wa2_full = small_w[:, :, n_rel_cols:].transpose(1, 0, 2).reshape(small_w.shape[1], -1)

    def shard(n):
        s = w[BIG_NAME[n]][0].astype(BF16)
        return s.T if n == "mix_in" else s

    started = {}
    order = small_w
    for grp, members in GROUPS.items():
        started[grp] = gather_start([shard(n) for n in members], order, f"gather_start_{grp}")
        order = started[grp][-1]
    mod = _after(mod, order)

    def weights_of(grp, after):
        shards, lands = gather_wait(started[grp], after, f"gather_wait_{grp}")
        full = dict(zip(GROUPS[grp], gather_forward(shards, lands, f"gather_forward_{grp}")))
        if grp == "mix":
            return dict(in_t=full["mix_in"].reshape(-1, d), proj_a=full["proj_a"], proj_b=full["proj_b"],
                        out=full["mix_out"].reshape(-1, d))
        return {"in": full[grp + "_in"], "out": full[grp + "_out"].reshape(-1, d)}

    exchanges = {}

    def grads_ready(grp, grads):
        names = GROUPS[grp]
        got = pair_exchange([grads[n] for n in names], f"grad_pair_exchange_{grp}")
        sums = [pair_add(core, grads[n], g, f"grad_pair_add_{n}") for n, g in zip(names, got)]
        exchanges[grp] = chip_exchange_start(sums, f"grad_chip_start_{grp}")
        return exchanges[grp][-1]

    lns = [ln1_g, ln1_b, ln2_g, ln2_b, ln3_g, ln3_b]
    loss, dx, dmod, small = local_step(x[0], loss_target[0], mod, weights_of, grads_ready, rel_full, wa2_full,
                                       b_alpha, gla_norm_g, lns)
    loss = lax.psum(loss[0, 0], ("x", "y", "c"))

    packed = jnp.concatenate([g.reshape(1, -1) for g in dmod]
                             + [small[n].reshape(1, -1) for n in SMALL_REPLICATED[1:] + SMALL_SHARDED], axis=1)
    parts = all_gather_small(packed, "gather_small_grads")
    n_mod = N_MOD * d
    dmod_all = parts[:, 0, :n_mod]
    g_w_ada = adaln_wgrad(c_all, lax.dynamic_slice_in_dim(dmod_all, dev * ada_cols, ada_cols, axis=1), "adaln_wgrad")
    out = {}
    out["w_ada"] =[o[None] for o in adamw_sum(g_w_ada[None], w_ada[0], m_w_ada[0], v_w_ada[0], "adamw_w_ada")]

    def pack(src):
        rows = [src[n].reshape(1, -1) for n in SMALL_REPLICATED]
        return jnp.concatenate(rows + [src[n].reshape(1, -1) for n in SMALL_SHARDED], axis=1)

    n_rep = sum(w[n].size for n in SMALL_REPLICATED)
    rep_parts = parts[:, :, :n_rep]
    off = n_rep
    shard_parts = []
    for n in SMALL_SHARDED:
        rows, cols_local = w[n].shape[1], w[n].shape[2]
        full_part = parts[:, 0, off:off + rows * cols_local * N_DEV].reshape(N_DEV, rows, cols_local * N_DEV)
        mine = lax.dynamic_slice_in_dim(full_part, dev * cols_local, cols_local, axis=2)
        shard_parts.append(mine.reshape(N_DEV, 1, rows * cols_local))
        off += rows * cols_local * N_DEV
    small_parts = jnp.concatenate([rep_parts] + shard_parts, axis=2)
    res = adamw_sum(small_parts, pack(w), pack(mom), pack(var), "adamw_small")
    off = 0
    for n in SMALL_REPLICATED + SMALL_SHARDED:
        size = w[n].size
        out[n] = [r[:, off:off + size].reshape(w[n].shape) for r in res]
        off += size

    order = res[0]
    for grp in reversed(list(GROUPS)):
        sums, recv = chip_exchange_wait(exchanges[grp], order, f"grad_chip_wait_{grp}")
        for n, hsum, r in zip(GROUPS[grp], sums, recv):
            full = BIG_NAME[n]
            if n == "mix_in":
                g = owned_sum(chip, hsum, r, f"owned_sum_{n}").T
                res_n = adamw_sum(g[None], w[full][0], mom[full][0], var[full][0], f"adamw_{n}")
            else:
                res_n = adamw_owned(chip, hsum, r, w[full][0], mom[full][0], var[full][0], f"adamw_{n}")
            out[full] = [o[None] for o in res_n]
            order = res_n[0]

    flat = [loss, dx[None]]
    for k in range(4):
        flat += [out[n][k] for n in WEIGHT_ORDER]
    return tuple(flat)
```

```python
import functools

import numpy as np
import jax
import jax.numpy as jnp
from jax import lax
from jax.experimental import pallas as pl
from jax.experimental.pallas import tpu as pltpu

F32 = jnp.float32
BF16 = jnp.bfloat16
MESH = pl.DeviceIdType.MESH
N_DEV = 8
N_CHIP = 4

CHUNK = 64
A_PAST_CHUNKS = 8
REL_CLIP = 256
GATE_TAU = 16.0
N_MOD = 9
DEPTH = 1
ALPHA = (2.0 * DEPTH) ** 0.25
LN_EPS = 1e-5
RMS_EPS = 1e-6
ADAM_LR = 0.001
ADAM_B1 = 0.9
ADAM_B2 = 0.999
ADAM_EPS = 1e-08
ADAM_WD = 0.01
ADAM_STEP = 10

LANE = 128
VMEM_LIMIT = 56 * 2 ** 20
QB = 4 * CHUNK
KW = 3 * QB
GB = 8 * CHUNK
NEG = -1e30
K_TILE_CAP = 11 * LANE
HI = lax.Precision.HIGHEST

ANY = pl.BlockSpec(memory_space=pl.ANY)
VMEM_SPEC = pl.BlockSpec(memory_space=pltpu.VMEM)


def _params(*sem):
    return pltpu.CompilerParams(dimension_semantics=sem, vmem_limit_bytes=VMEM_LIMIT)


def _sds(shape, dtype):
    return jax.ShapeDtypeStruct(shape, dtype)


def _dot(a, b):
    return jnp.dot(a, b, preferred_element_type=F32)


def _dot_nt(a, b):
    return lax.dot_general(a, b, (((1,), (1,)), ((), ())), preferred_element_type=F32)


def _dot_tn(a, b):
    return lax.dot_general(a, b, (((0,), (0,)), ((), ())), preferred_element_type=F32)


def _sigmoid(x):
    return 1.0 / (1.0 + jnp.exp(-x))


def _colsum(x):
    return jnp.sum(x, axis=0, keepdims=True)


def _row_tile(rows, cap, mult):
    for t in range(min(rows, cap), 0, -1):
        if rows % t == 0 and t % mult == 0:
            return t
    return rows


def _me():
    return lax.axis_index("x"), lax.axis_index("y"), lax.axis_index("c")


def _flip(me, k):
    return tuple((1 - p) if (k >> s) & 1 else p for p, s in zip(me, (2, 1, 0)))


def _lin(p):
    return 4 * p[0] + 2 * p[1] + p[2]


def all_gather_small(x, name):
    r, n = x.shape

    def body(x_ref, out_ref, send_sems, recv_sems, local_sem):
        me = _me()
        mine = pltpu.make_async_copy(x_ref, out_ref.at[_lin(me)], local_sem)
        mine.start()
        sends = []
        for k in range(1, N_DEV):
            cp = pltpu.make_async_remote_copy(
                src_ref=x_ref, dst_ref=out_ref.at[_lin(me)], send_sem=send_sems.at[k - 1],
                recv_sem=recv_sems.at[k - 1], device_id=_flip(me, k), device_id_type=MESH)
            cp.start()
            sends.append(cp)
        for k in range(1, N_DEV):
            peer = _flip(me, k)
            pltpu.make_async_remote_copy(
                src_ref=x_ref, dst_ref=out_ref.at[_lin(peer)], send_sem=send_sems.at[k - 1],
                recv_sem=recv_sems.at[k - 1], device_id=peer, device_id_type=MESH).wait_recv()
        for cp in sends:
            cp.wait_send()
        mine.wait()

    return pl.pallas_call(
        body, name=name, out_shape=_sds((N_DEV, r, n), x.dtype),
        in_specs=[VMEM_SPEC], out_specs=VMEM_SPEC,
        scratch_shapes=[pltpu.SemaphoreType.DMA((N_DEV - 1,)), pltpu.SemaphoreType.DMA((N_DEV - 1,)),
                        pltpu.SemaphoreType.DMA],
    )(x)


HBM_SPEC = pl.BlockSpec(memory_space=pltpu.HBM)
SEM_SPEC = pl.BlockSpec(memory_space=pltpu.SEMAPHORE)
EFFECT = pltpu.SideEffectType.DATAFLOW_SIDE_EFFECTING
FIRST = N_CHIP


def _hbm(v):
    return pltpu.with_memory_space_constraint(v, pltpu.HBM)


def _other_chips(mx, my):
    return [(1 - mx, my), (mx, 1 - my), (1 - mx, 1 - my)]


def place_own(dev, shard, name):
    rows, cols = shard.shape
    tr, tc = _tile2(rows, cols, 16)

    def body(dev_ref, s_ref, land_ref, o_ref):
        o_ref[...] = s_ref[...]

    land = lax.empty((N_DEV, rows, cols), shard.dtype)
    return pl.pallas_call(
        body, name=name, out_shape=_sds(land.shape, land.dtype),
        grid_spec=pltpu.PrefetchScalarGridSpec(
            num_scalar_prefetch=1, grid=(rows // tr, cols // tc),
            in_specs=[pl.BlockSpec((tr, tc), lambda i, j, d: (i, j)), ANY],
            out_specs=pl.BlockSpec((None, tr, tc), lambda i, j, d: (d[0], i, j))),
        input_output_aliases={2: 0}, compiler_params=_params("parallel", "parallel"),
    )(dev, shard, land)


def gather_start(shards, lands, after, name):
    n = len(shards)

    def body(*refs):
        ins, zones = refs[:n], refs[n:2 * n]
        send_sems, recv_sems = refs[2 * n + 1], refs[2 * n + 2]
        token = refs[-1]
        me = _me()
        mx, my, mc = me
        for a in range(n):
            dst = zones[a].at[_lin(me)]
            targets = [(mx, my, 1 - mc)] + [(*chip, mc) for chip in _other_chips(mx, my)]
            for k, to in enumerate(targets):
                pltpu.make_async_remote_copy(
                    src_ref=ins[a], dst_ref=dst, send_sem=send_sems.at[a * FIRST + k],
                    recv_sem=recv_sems.at[a * FIRST + k], device_id=to, device_id_type=MESH).start()
        token[...] = jnp.zeros_like(token)

    sems = pltpu.SemaphoreType.DMA((n * FIRST,))
    out = pl.pallas_call(
        body, name=name,
        out_shape=(sems, sems, *[pltpu.HBM(s.shape, s.dtype) for s in shards],
                   *[pltpu.HBM(z.shape, z.dtype) for z in lands], _sds((8, LANE), F32)),
        in_specs=[HBM_SPEC] * (2 * n) + [ANY],
        out_specs=(SEM_SPEC, SEM_SPEC, *[HBM_SPEC] * (2 * n), VMEM_SPEC),
        input_output_aliases={a: 2 + a for a in range(2 * n)},
        compiler_params=pltpu.CompilerParams(has_side_effects=EFFECT),
    )(*[_hbm(s) for s in shards], *[_hbm(z) for z in lands], after)
    return out[0], out[1], out[2:2 + n], out[2 + n:2 + 2 * n], out[-1]


def gather_wait(started, after, name):
    send_sems, recv_sems, shards, lands, _ = started
    n = len(shards)

    def body(*refs):
        ins, zones = refs[:n], refs[n:2 * n]
        send_ref, recv_ref = refs[2 * n], refs[2 * n + 1]
        mx, my, mc = _me()
        for a in range(n):
            for k in range(FIRST):
                cp = pltpu.make_async_remote_copy(
                    src_ref=ins[a], dst_ref=zones[a].at[0], send_sem=send_ref.at[a * FIRST + k],
                    recv_sem=recv_ref.at[a * FIRST + k], device_id=(mx, my, 1 - mc), device_id_type=MESH)
                cp.wait_send()
                cp.wait_recv()

    out = pl.pallas_call(
        body, name=name,
        out_shape=(*[pltpu.HBM(s.shape, s.dtype) for s in shards], *[pltpu.HBM(z.shape, z.dtype) for z in lands]),
        in_specs=[HBM_SPEC] * (2 * n) + [SEM_SPEC, SEM_SPEC, ANY], out_specs=tuple([HBM_SPEC] * (2 * n)),
        input_output_aliases={a: a for a in range(2 * n)},
        compiler_params=pltpu.CompilerParams(has_side_effects=EFFECT),
    )(*shards, *lands, send_sems, recv_sems, after)
    return out[:n], out[n:]


def gather_forward(lands, name):
    n = len(lands)
    rel = N_CHIP - 1

    def body(*refs):
        zones, outs = refs[:n], refs[n:2 * n]
        send_sems, recv_sems = refs[2 * n:]
        mx, my, mc = _me()
        chips = _other_chips(mx, my)

        def copy(a, j, core):
            blk = _lin((*chips[j], core))
            return pltpu.make_async_remote_copy(
                src_ref=zones[a].at[blk], dst_ref=outs[a].at[blk], send_sem=send_sems.at[a * rel + j],
                recv_sem=recv_sems.at[a * rel + j], device_id=(mx, my, 1 - mc), device_id_type=MESH)

        sends = [copy(a, j, mc) for a in range(n) for j in range(rel)]
        for cp in sends:
            cp.start()
        for a in range(n):
            for j in range(rel):
                copy(a, j, 1 - mc).wait_recv()
        for cp in sends:
            cp.wait_send()

    return pl.pallas_call(
        body, name=name, out_shape=[_sds(z.shape, z.dtype) for z in lands],
        in_specs=[ANY] * n, out_specs=[ANY] * n, input_output_aliases={a: a for a in range(n)},
        scratch_shapes=[pltpu.SemaphoreType.DMA((n * rel,)), pltpu.SemaphoreType.DMA((n * rel,))],
    )(*lands)


def pair_exchange(gs, name):
    n = len(gs)

    def body(*refs):
        ins, outs = refs[:n], refs[n:2 * n]
        send_sems, recv_sems = refs[2 * n:]
        mx, my, mc = _me()
        cps = []
        for a in range(n):
            for q in range(N_CHIP):
                cp = pltpu.make_async_remote_copy(
                    src_ref=ins[a].at[2 * q + (1 - mc)], dst_ref=outs[a].at[q],
                    send_sem=send_sems.at[a * N_CHIP + q], recv_sem=recv_sems.at[a * N_CHIP + q],
                    device_id=(mx, my, 1 - mc), device_id_type=MESH)
                cp.start()
                cps.append(cp)
        for cp in cps:
            cp.wait()

    return pl.pallas_call(
        body, name=name, out_shape=[_sds((N_CHIP,) + g.shape[1:], g.dtype) for g in gs],
        in_specs=[ANY] * n, out_specs=[ANY] * n,
        scratch_shapes=[pltpu.SemaphoreType.DMA((n * N_CHIP,)), pltpu.SemaphoreType.DMA((n * N_CHIP,))],
    )(*gs)


def _tile2(rows, cols, row_mult):
    tr = _row_tile(rows, 512, row_mult)
    if tr < rows or rows * cols <= 2 ** 20:
        return tr, cols
    return rows, _row_tile(cols, 512, LANE)


def pair_add(core, g, got, name):
    _, rows, cols = g.shape
    tr, tc = _tile2(rows, cols, 16)

    def body(core_ref, g_ref, got_ref, h_ref):
        h_ref[...] = (g_ref[...].astype(F32) + got_ref[...].astype(F32)).astype(h_ref.dtype)

    blk = pl.BlockSpec((None, tr, tc), lambda q, i, j, c: (q, i, j))
    return pl.pallas_call(
        body, name=name, out_shape=_sds((N_CHIP, rows, cols), g.dtype),
        grid_spec=pltpu.PrefetchScalarGridSpec(
            num_scalar_prefetch=1, grid=(N_CHIP, rows // tr, cols // tc),
            in_specs=[pl.BlockSpec((None, tr, tc), lambda q, i, j, c: (2 * q + c[0], i, j)), blk],
            out_specs=blk),
        compiler_params=_params("parallel", "parallel", "parallel"),
    )(core, g, got)


def chip_exchange_start(hs, name):
    n = len(hs)
    rel = N_CHIP - 1
    lands = [lax.empty((rel,) + h.shape[1:], h.dtype) for h in hs]

    def body(*refs):
        ins, zones = refs[:n], refs[n:2 * n]
        send_sems, recv_sems = refs[2 * n], refs[2 * n + 1]
        token = refs[-1]
        mx, my, mc = _me()
        for a in range(n):
            for k, (px, py) in enumerate(_other_chips(mx, my)):
                pltpu.make_async_remote_copy(
                    src_ref=ins[a].at[2 * px + py], dst_ref=zones[a].at[k], send_sem=send_sems.at[a * rel + k],
                    recv_sem=recv_sems.at[a * rel + k], device_id=(px, py, mc), device_id_type=MESH).start()
        token[...] = jnp.zeros_like(token)

    sems = pltpu.SemaphoreType.DMA((n * rel,))
    out = pl.pallas_call(
        body, name=name,
        out_shape=(sems, sems, *[pltpu.HBM(h.shape, h.dtype) for h in hs],
                   *[pltpu.HBM(z.shape, z.dtype) for z in lands], _sds((8, LANE), F32)),
        in_specs=[HBM_SPEC] * (2 * n), out_specs=(SEM_SPEC, SEM_SPEC, *[HBM_SPEC] * (2 * n), VMEM_SPEC),
        input_output_aliases={a: 2 + a for a in range(2 * n)},
        compiler_params=pltpu.CompilerParams(has_side_effects=EFFECT),
    )(*[_hbm(h) for h in hs], *[_hbm(z) for z in lands])
    return out[0], out[1], out[2:2 + n], out[2 + n:2 + 2 * n], out[-1]


def chip_exchange_wait(started, after, name):
    send_sems, recv_sems, hs, lands, _ = started
    n = len(hs)
    rel = N_CHIP - 1

    def body(*refs):
        ins, zones = refs[:n], refs[n:2 * n]
        send_ref, recv_ref = refs[2 * n], refs[2 * n + 1]
        mx, my, mc = _me()
        for a in range(n):
            for k, (px, py) in enumerate(_other_chips(mx, my)):
                cp = pltpu.make_async_remote_copy(
                    src_ref=ins[a].at[0], dst_ref=zones[a].at[k], send_sem=send_ref.at[a * rel + k],
                    recv_sem=recv_ref.at[a * rel + k], device_id=(px, py, mc), device_id_type=MESH)
                cp.wait_send()
                cp.wait_recv()

    out = pl.pallas_call(
        body, name=name,
        out_shape=(*[pltpu.HBM(h.shape, h.dtype) for h in hs], *[pltpu.HBM(z.shape, z.dtype) for z in lands]),
        in_specs=[HBM_SPEC] * (2 * n) + [SEM_SPEC, SEM_SPEC, ANY], out_specs=tuple([HBM_SPEC] * (2 * n)),
        input_output_aliases={a: a for a in range(2 * n)},
        compiler_params=pltpu.CompilerParams(has_side_effects=EFFECT),
    )(*hs, *lands, send_sems, recv_sems, after)
    return out[:n], out[n:]


def _adam(w, g, m, v):
    m = ADAM_B1 * m + (1.0 - ADAM_B1) * g
    v = ADAM_B2 * v + (1.0 - ADAM_B2) * (g * g)
    m_hat = m / (1.0 - ADAM_B1 ** ADAM_STEP)
    v_hat = v / (1.0 - ADAM_B2 ** ADAM_STEP)
    delta = -ADAM_LR * (m_hat / (jnp.sqrt(v_hat) + ADAM_EPS) + ADAM_WD * w)
    return delta, m, v


def adamw_owned(chip, h, got, w, m, v, name):
    rows, cols = w.shape
    tr = _row_tile(rows, 256, 16)

    def body(chip_ref, h_ref, got_ref, w_ref, m_ref, v_ref, g_out, d_out, m_out, v_out):
        g = h_ref[...].astype(F32)
        for k in range(N_CHIP - 1):
            g = g + got_ref[k].astype(F32)
        d, mn, vn = _adam(w_ref[...], g, m_ref[...], v_ref[...])
        g_out[...] = g
        d_out[...] = d
        m_out[...] = mn
        v_out[...] = vn

    blk = pl.BlockSpec((tr, cols), lambda i, c: (i, 0))
    return pl.pallas_call(
        body, name=name, out_shape=[_sds((rows, cols), F32)] * 4,
        grid_spec=pltpu.PrefetchScalarGridSpec(
            num_scalar_prefetch=1, grid=(rows // tr,),
            in_specs=[pl.BlockSpec((None, tr, cols), lambda i, c: (c[0], i, 0)),
                      pl.BlockSpec((N_CHIP - 1, tr, cols), lambda i, c: (0, i, 0)), blk, blk, blk],
            out_specs=[blk] * 4),
        compiler_params=_params("parallel"),
    )(chip, h, got, w, m, v)


def owned_sum(chip, h, got, name):
    _, rows, cols = h.shape
    tr, tc = _tile2(rows, cols, 16)

    def body(chip_ref, h_ref, got_ref, g_out):
        g = h_ref[...].astype(F32)
        for k in range(N_CHIP - 1):
            g = g + got_ref[k].astype(F32)
        g_out[...] = g

    return pl.pallas_call(
        body, name=name, out_shape=_sds((rows, cols), F32),
        grid_spec=pltpu.PrefetchScalarGridSpec(
            num_scalar_prefetch=1, grid=(rows // tr, cols // tc),
            in_specs=[pl.BlockSpec((None, tr, tc), lambda i, j, c: (c[0], i, j)),
                      pl.BlockSpec((N_CHIP - 1, tr, tc), lambda i, j, c: (0, i, j))],
            out_specs=pl.BlockSpec((tr, tc), lambda i, j, c: (i, j))),
        compiler_params=_params("parallel", "parallel"),
    )(chip, h, got)


def adamw_sum(parts, w, m, v, name):
    n_parts, rows, cols = parts.shape
    tr = _row_tile(rows, 256, 8)

    def body(p_ref, w_ref, m_ref, v_ref, g_out, d_out, m_out, v_out):
        g = p_ref[0]
        for k in range(1, n_parts):
            g = g + p_ref[k]
        d, mn, vn = _adam(w_ref[...], g, m_ref[...], v_ref[...])
        g_out[...] = g
        d_out[...] = d
        m_out[...] = mn
        v_out[...] = vn

    blk = pl.BlockSpec((tr, cols), lambda i: (i, 0))
    return pl.pallas_call(
        body, name=name, out_shape=[_sds((rows, cols), F32)] * 4, grid=(rows // tr,),
        in_specs=[pl.BlockSpec((n_parts, tr, cols), lambda i: (0, i, 0)), blk, blk, blk],
        out_specs=[blk] * 4, compiler_params=_params("parallel"),
    )(parts, w, m, v)


def adaln_cols(c_all, w, b, name):
    d, n = w.shape
    tn = _row_tile(n, 768, LANE)

    def body(c_ref, w_ref, b_ref, o_ref):
        c = c_ref[...]
        o_ref[...] = jnp.dot(c * _sigmoid(c), w_ref[...], preferred_element_type=F32, precision=HI) + b_ref[...]

    return pl.pallas_call(
        body, name=name, out_shape=_sds((N_DEV, n), F32), grid=(n // tn,),
        in_specs=[pl.BlockSpec((N_DEV, d), lambda j: (0, 0)), pl.BlockSpec((d, tn), lambda j: (0, j)),
                  pl.BlockSpec((1, tn), lambda j: (0, j))],
        out_specs=pl.BlockSpec((N_DEV, tn), lambda j: (0, j)), compiler_params=_params("parallel"),
    )(c_all, w, b)


def adaln_wgrad(c_all, dmod_cols, name):
    d = c_all.shape[1]
    n = dmod_cols.shape[1]
    tn = _row_tile(n, 768, LANE)

    def body(c_ref, g_ref, o_ref):
        c = c_ref[...]
        o_ref[...] = lax.dot_general(c * _sigmoid(c), g_ref[...], (((0,), (0,)), ((), ())),
                                     preferred_element_type=F32, precision=HI)

    return pl.pallas_call(
        body, name=name, out_shape=_sds((d, n), F32), grid=(n // tn,),
        in_specs=[pl.BlockSpec((N_DEV, d), lambda j: (0, 0)), pl.BlockSpec((N_DEV, tn), lambda j: (0, j))],
        out_specs=pl.BlockSpec((d, tn), lambda j: (0, j)), compiler_params=_params("parallel"),
    )(c_all, dmod_cols)


def _modulate(h, sh, sc):
    return (h * (1.0 + sc) + sh).astype(BF16)


def ffn_in(h, sh, sc, w3, name):
    t, d = h.shape
    nb, _, bw = w3.shape
    half = nb // 2
    tm = _row_tile(t, 256, 16)

    def body(h_ref, sh_ref, sc_ref, wa_ref, wb_ref, a_ref, b_ref, s_ref):
        u = _modulate(h_ref[...], sh_ref[...], sc_ref[...])
        a = _dot(u, wa_ref[...])
        b = _dot(u, wb_ref[...])
        a_ref[...] = a.astype(BF16)
        b_ref[...] = b.astype(BF16)
        s_ref[...] = (a * _sigmoid(a) * b).astype(BF16)

    vec = pl.BlockSpec((1, d), lambda j, i: (0, 0))
    out = pl.BlockSpec((tm, bw), lambda j, i: (i, j))
    return pl.pallas_call(
        body, name=name, out_shape=[_sds((t, half * bw), BF16)] * 3, grid=(half, t // tm),
        in_specs=[pl.BlockSpec((tm, d), lambda j, i: (i, 0)), vec, vec,
                  pl.BlockSpec((None, d, bw), lambda j, i: (j, 0, 0)),
                  pl.BlockSpec((None, d, bw), lambda j, i: (j + half, 0, 0))],
        out_specs=[out] * 3, compiler_params=_params("parallel", "parallel"),
    )(h, sh, sc, w3, w3)


def mod_matmul(h, sh, sc, wt, bw, name):
    t, d = h.shape
    n = wt.shape[0]
    tm = _row_tile(t, 256, 16)

    def body(h_ref, sh_ref, sc_ref, w_ref, o_ref):
        o_ref[...] = _dot_nt(_modulate(h_ref[...], sh_ref[...], sc_ref[...]), w_ref[...]).astype(BF16)

    vec = pl.BlockSpec((1, d), lambda j, i: (0, 0))
    return pl.pallas_call(
        body, name=name, out_shape=_sds((t, n), BF16), grid=(n // bw, t // tm),
        in_specs=[pl.BlockSpec((tm, d), lambda j, i: (i, 0)), vec, vec, pl.BlockSpec((bw, d), lambda j, i: (j, 0))],
        out_specs=pl.BlockSpec((tm, bw), lambda j, i: (i, j)), compiler_params=_params("parallel", "parallel"),
    )(h, sh, sc, wt)


def out_ln(s, w, hin, gmod, ln_g, ln_b, coef, name):
    t, kdim = s.shape
    d = w.shape[1]
    tm = _row_tile(t, 512, 16)
    tk = _row_tile(kdim, K_TILE_CAP, LANE)
    nk = kdim // tk

    def body(s_ref, w_ref, hin_ref, gm_ref, g_ref, b_ref, f_ref, z_ref, h_ref, acc):
        k = pl.program_id(1)

        @pl.when(k == 0)
        def _():
            acc[...] = jnp.zeros_like(acc)

        acc[...] += _dot(s_ref[...], w_ref[...])

        @pl.when(k == nk - 1)
        def _():
            f = acc[...]
            z = ALPHA * hin_ref[...] + (coef * gm_ref[...]) * f
            mu = jnp.mean(z, axis=-1, keepdims=True)
            zc = z - mu
            var = jnp.mean(zc * zc, axis=-1, keepdims=True)
            f_ref[...] = f.astype(BF16)
            z_ref[...] = z
            h_ref[...] = zc * lax.rsqrt(var + LN_EPS) * g_ref[...] + b_ref[...]

    vec = pl.BlockSpec((1, d), lambda i, k: (0, 0))
    row = pl.BlockSpec((tm, d), lambda i, k: (i, 0))
    return pl.pallas_call(
        body, name=name, out_shape=[_sds((t, d), BF16), _sds((t, d), F32), _sds((t, d), F32)],
        grid=(t // tm, nk),
        in_specs=[pl.BlockSpec((tm, tk), lambda i, k: (i, k)), pl.BlockSpec((tk, d), lambda i, k: (k, 0)),
                  pl.BlockSpec((tm, d), lambda i, k: (i, 0), pipeline_mode=pl.Buffered(1)), vec, vec, vec],
        out_specs=[row, row, row], scratch_shapes=[pltpu.VMEM((tm, d), F32)],
        compiler_params=_params("parallel", "arbitrary"),
    )(s, w, hin, gmod, ln_g, ln_b)


def ln_bwd(dh, z, f, ln_g, gmod, coef, name, target=None):
    t, d = z.shape
    tm = _row_tile(t, 256, 16)
    head = target is not None

    def body(*refs):
        if head:
            dh_ref, tg_ref, z_ref, f_ref, g_ref, gm_ref, dz_ref, df_ref, dg_ref, db_ref, dgm_ref, loss_ref = refs
        else:
            dh_ref, z_ref, f_ref, g_ref, gm_ref, dz_ref, df_ref, dg_ref, db_ref, dgm_ref = refs
        i = pl.program_id(0)

        @pl.when(i == 0)
        def _():
            dg_ref[...] = jnp.zeros_like(dg_ref)
            db_ref[...] = jnp.zeros_like(db_ref)
            dgm_ref[...] = jnp.zeros_like(dgm_ref)
            if head:
                loss_ref[...] = jnp.zeros_like(loss_ref)

        dh = dh_ref[...]
        if head:
            err = dh - tg_ref[...]
            loss_ref[...] += 0.5 * jnp.sum(jnp.mean(err * err, axis=-1, keepdims=True))
            dh = err / d
        zv = z_ref[...]
        mu = jnp.mean(zv, axis=-1, keepdims=True)
        zc = zv - mu
        rstd = lax.rsqrt(jnp.mean(zc * zc, axis=-1, keepdims=True) + LN_EPS)
        xhat = zc * rstd
        dxh = dh * g_ref[...]
        dz = rstd * (dxh - jnp.mean(dxh, axis=-1, keepdims=True)
                     - xhat * jnp.mean(dxh * xhat, axis=-1, keepdims=True))
        dz_ref[...] = dz
        df_ref[...] = ((coef * gm_ref[...]) * dz).astype(BF16)
        dg_ref[...] += _colsum(dh * xhat)
        db_ref[...] += _colsum(dh)
        dgm_ref[...] += _colsum(coef * f_ref[...].astype(F32) * dz)

    vec = pl.BlockSpec((1, d), lambda i: (0, 0))
    row = pl.BlockSpec((tm, d), lambda i: (i, 0))
    ins = [dh] + ([target] if head else []) + [z, f, ln_g, gmod]
    in_specs = [row] + ([row] if head else []) + [row, row, vec, vec]
    out_shape = [_sds((t, d), F32), _sds((t, d), BF16)] + [_sds((1, d), F32)] * 3
    out_specs = [row, row, vec, vec, vec]
    if head:
        out_shape.append(_sds((1, LANE), F32))
        out_specs.append(pl.BlockSpec((1, LANE), lambda i: (0, 0)))
    return pl.pallas_call(
        body, name=name, out_shape=out_shape, grid=(t // tm,), in_specs=in_specs, out_specs=out_specs,
        compiler_params=_params("arbitrary"),
    )(*ins)


def ffn_bwd_act(df, w, a, b, name):
    t, d = df.shape
    fdim = w.shape[0]
    bw = fdim // (N_DEV // 2)
    tm = _row_tile(t, 512, 16)

    def body(df_ref, w_ref, a_ref, b_ref, o_ref):
        ds = _dot_nt(df_ref[...], w_ref[...])
        av = a_ref[...].astype(F32)
        sg = _sigmoid(av)
        o_ref[0] = (ds * b_ref[...].astype(F32) * (sg * (1.0 + av * (1.0 - sg)))).astype(BF16)
        o_ref[1] = (ds * (av * sg)).astype(BF16)

    act = pl.BlockSpec((tm, bw), lambda j, i: (i, j))
    return pl.pallas_call(
        body, name=name, out_shape=_sds((2, t, fdim), BF16), grid=(fdim // bw, t // tm),
        in_specs=[pl.BlockSpec((tm, d), lambda j, i: (i, 0)), pl.BlockSpec((bw, d), lambda j, i: (j, 0)), act, act],
        out_specs=pl.BlockSpec((2, tm, bw), lambda j, i: (0, i, j)),
        compiler_params=_params("parallel", "parallel"),
    )(df, w, a, b)


def matmul_tn(name, a, a_block, a_map, b, b_block, b_map, out_shape, o_block, o_map, n_out, mod=None,
              mod_b=False):
    tk = [s for s in a_block if s is not None][0]
    nk = a.shape[-2] // tk
    m, nn = [s for s in o_block if s is not None]

    def body(*refs):
        if mod is None:
            a_ref, b_ref, o_ref, acc = refs
        else:
            a_ref, sh_ref, sc_ref, b_ref, o_ref, acc = refs
        k = pl.program_id(1)

        @pl.when(k == 0)
        def _():
            acc[...] = jnp.zeros_like(acc)

        av, bv = a_ref[...], b_ref[...]
        if mod is not None and mod_b:
            bv = _modulate(bv, sh_ref[...], sc_ref[...])
        elif mod is not None:
            av = _modulate(av, sh_ref[...], sc_ref[...])
        acc[...] += _dot_tn(av, bv)

        @pl.when(k == nk - 1)
        def _():
            o_ref[...] = acc[...].astype(o_ref.dtype)

    ins = [a] + (list(mod) if mod is not None else []) + [b]
    in_specs = [pl.BlockSpec(a_block, a_map)]
    if mod is not None:
        vec = pl.BlockSpec((1, mod[0].shape[1]), lambda n, k: (0, 0))
        in_specs += [vec, vec]
    in_specs.append(pl.BlockSpec(b_block, b_map))
    return pl.pallas_call(
        body, name=name, out_shape=out_shape, grid=(n_out, nk), in_specs=in_specs,
        out_specs=pl.BlockSpec(o_block, o_map), scratch_shapes=[pltpu.VMEM((m, nn), F32)],
        compiler_params=_params("parallel", "arbitrary"),
    )(*ins)


def matmul_nt_blocks(name, dy, dy_block, dy_map, w3, t, resid=None):
    tm, bw = [s for s in dy_block if s is not None]
    rows = w3.ndim == 2
    if rows:
        nk, n = w3.shape[0] // bw, w3.shape[1]
    else:
        nk, n, _ = w3.shape

    def body(*refs):
        if resid is None:
            dy_ref, w_ref, o_ref, acc = refs
        else:
            dy_ref, w_ref, dz_ref, hin_ref, sc_ref, o_ref, dsc_ref, dsh_ref, acc = refs
        i, k = pl.program_id(0), pl.program_id(1)

        @pl.when(k == 0)
        def _():
            acc[...] = jnp.zeros_like(acc)

        if resid is not None:
            @pl.when((k == 0) & (i == 0))
            def _():
                dsc_ref[...] = jnp.zeros_like(dsc_ref)
                dsh_ref[...] = jnp.zeros_like(dsh_ref)

        acc[...] += _dot(dy_ref[...], w_ref[...]) if rows else _dot_nt(dy_ref[...], w_ref[...])

        @pl.when(k == nk - 1)
        def _():
            du = acc[...]
            if resid is None:
                o_ref[...] = du.astype(o_ref.dtype)
            else:
                o_ref[...] = ALPHA * dz_ref[...] + du * (1.0 + sc_ref[...])
                dsc_ref[...] += _colsum(du * hin_ref[...])
                dsh_ref[...] += _colsum(du)

    row = pl.BlockSpec((tm, n), lambda i, k: (i, 0))
    vec = pl.BlockSpec((1, n), lambda i, k: (0, 0))
    w_spec = pl.BlockSpec((bw, n), lambda i, k: (k, 0)) if rows else pl.BlockSpec((None, n, bw), lambda i, k: (k, 0, 0))
    in_specs = [pl.BlockSpec(dy_block, dy_map), w_spec]
    ins = [dy, w3]
    if resid is None:
        out_shape, out_specs = _sds((t, n), BF16), row
    else:
        ins += list(resid)
        once = pl.BlockSpec((tm, n), lambda i, k: (i, 0), pipeline_mode=pl.Buffered(1))
        in_specs += [once, once, vec]
        out_shape = [_sds((t, n), F32), _sds((1, n), F32), _sds((1, n), F32)]
        out_specs = [row, vec, vec]
    return pl.pallas_call(
        body, name=name, out_shape=out_shape, grid=(t // tm, nk), in_specs=in_specs, out_specs=out_specs,
        scratch_shapes=[pltpu.VMEM((tm, n), F32)], compiler_params=_params("arbitrary", "arbitrary"),
    )(*ins)


REL_W = KW + QB


def bias_table(rel_bias):
    nh, n_rel = rel_bias.shape
    lo = KW - QB - REL_CLIP
    hi = KW - lo - n_rel
    assert n_rel == REL_CLIP + CHUNK and lo >= 0 and hi >= 0
    first, last = rel_bias[:, :1], rel_bias[:, -1:]
    row = jnp.concatenate([jnp.broadcast_to(first, (nh, lo)), rel_bias, jnp.broadcast_to(last, (nh, hi)),
                           jnp.broadcast_to(first, (nh, QB))], axis=1)
    table = jnp.tile(row, (1, QB))[:, :QB * (REL_W - 1)].reshape(nh, QB, REL_W - 1)[:, :, :KW]
    q = np.arange(QB)[:, None] // CHUNK
    k = np.arange(KW)[None, :] // CHUNK
    band = (k >= q) & (k <= q + A_PAST_CHUNKS)
    return jnp.where(band[None], table, NEG)


def bias_grad_skew(dbias):
    nh = dbias.shape[0]
    flat = jnp.pad(dbias, ((0, 0), (0, 0), (0, REL_W - 1 - KW))).reshape(nh, QB * (REL_W - 1))
    return jnp.pad(flat, ((0, 0), (0, QB))).reshape(nh, QB, REL_W)


def bias_clip_map(n_rel):
    m = np.arange(REL_W)
    dist = np.where(m < KW, m, m - REL_W) - (KW - QB)
    idx = np.clip(dist, -REL_CLIP, CHUNK - 1) + REL_CLIP
    return (idx[:, None] == np.arange(n_rel)[None, :]).astype(np.float32)


def _kv_specs(dh, order):
    if order == "hi":
        return [pl.BlockSpec((None, QB, dh), functools.partial(lambda r, h, i: (h, jnp.maximum(i - r, 0), 0), r))
                for r in (2, 1, 0)]
    raise ValueError(order)


def _scores(q, ks, bias, i, scale):
    s = jnp.concatenate([_dot_nt(q, kk) for kk in ks], axis=1) * scale + bias
    col = lax.broadcasted_iota(jnp.int32, s.shape, 1)
    return jnp.where(col >= (2 - i) * QB, s, NEG)


def attn_fwd(q, k, v, bias, name):
    nh, t, dh = q.shape
    scale = dh ** -0.5

    def body(q_ref, k0, k1, k2, v0, v1, v2, b_ref, o_ref, lse_ref):
        i = pl.program_id(1)
        s = _scores(q_ref[...], [k0[...], k1[...], k2[...]], b_ref[...], i, scale)
        m = jnp.max(s, axis=-1, keepdims=True)
        p = jnp.exp(s - m)
        l = jnp.sum(p, axis=-1, keepdims=True)
        pb = p.astype(BF16)
        o = sum(_dot(pb[:, r * QB:(r + 1) * QB], vv[...]) for r, vv in enumerate((v0, v1, v2)))
        o_ref[...] = (o / l).astype(BF16)
        lse_ref[...] = m + jnp.log(l)

    qs = pl.BlockSpec((None, QB, dh), lambda h, i: (h, i, 0))
    st = pl.BlockSpec((None, QB, 1), lambda h, i: (h, i, 0))
    kv = _kv_specs(dh, "hi")
    return pl.pallas_call(
        body, name=name, out_shape=[_sds((nh, t, dh), BF16), _sds((nh, t, 1), F32)], grid=(nh, t // QB),
        in_specs=[qs] + kv + kv + [pl.BlockSpec((None, QB, KW), lambda h, i: (h, 0, 0))],
        out_specs=[qs, st], compiler_params=_params("parallel", "parallel"),
    )(q, k, k, k, v, v, v, bias)


def attn_bwd_q(q, k, v, bias, lse, do, name):
    nh, t, dh = q.shape
    scale = dh ** -0.5

    def body(q_ref, k0, k1, k2, v0, v1, v2, b_ref, lse_ref, do_ref, dq_ref, dl_ref, db_ref):
        i = pl.program_id(1)
        ks = [k0[...], k1[...], k2[...]]
        s = _scores(q_ref[...], ks, b_ref[...], i, scale)
        p = jnp.exp(s - lse_ref[...])
        dov = do_ref[...]
        dp = jnp.concatenate([_dot_nt(dov, vv[...]) for vv in (v0, v1, v2)], axis=1)
        delta = jnp.sum(p * dp, axis=-1, keepdims=True)
        ds = p * (dp - delta)
        dsb = ds.astype(BF16)
        dq = sum(_dot(dsb[:, r * QB:(r + 1) * QB], kk) for r, kk in enumerate(ks))
        dq_ref[...] = (dq * scale).astype(BF16)
        dl_ref[...] = delta

        @pl.when(i == 0)
        def _():
            db_ref[...] = jnp.zeros_like(db_ref)

        db_ref[...] += ds

    qs = pl.BlockSpec((None, QB, dh), lambda h, i: (h, i, 0))
    st = pl.BlockSpec((None, QB, 1), lambda h, i: (h, i, 0))
    tab = pl.BlockSpec((None, QB, KW), lambda h, i: (h, 0, 0))
    kv = _kv_specs(dh, "hi")
    return pl.pallas_call(
        body, name=name,
        out_shape=[_sds((nh, t, dh), BF16), _sds((nh, t, 1), F32), _sds((nh, QB, KW), F32)],
        grid=(nh, t // QB), in_specs=[qs] + kv + kv + [tab, st, qs], out_specs=[qs, st, tab],
        compiler_params=_params("parallel", "arbitrary"),
    )(q, k, k, k, v, v, v, bias, lse, do)


def attn_bwd_kv(q, k, v, bias, lse, delta, do, name):
    nh, t, dh = q.shape
    nb = t // QB
    scale = dh ** -0.5

    def body(k_ref, v_ref, b_ref, *refs):
        qr, lr, dr, gr = refs[0:3], refs[3:6], refs[6:9], refs[9:12]
        dk_ref, dv_ref = refs[12:]
        j = pl.program_id(1)
        kk, vv = k_ref[...], v_ref[...]
        dk = jnp.zeros((QB, dh), F32)
        dv = jnp.zeros((QB, dh), F32)
        for r in range(3):
            seg = 2 - r
            qv, dov = qr[r][...], gr[r][...]
            s = _dot_nt(qv, kk) * scale + b_ref[:, seg * QB:(seg + 1) * QB]
            p = jnp.where(j + r < nb, jnp.exp(s - lr[r][...]), 0.0)
            ds = p * (_dot_nt(dov, vv) - dr[r][...])
            dv = dv + _dot_tn(p.astype(BF16), dov)
            dk = dk + _dot_tn(ds.astype(BF16), qv)
        dk_ref[...] = (dk * scale).astype(BF16)
        dv_ref[...] = dv.astype(BF16)

    def later(width):
        return [pl.BlockSpec((None, QB, width),
                             functools.partial(lambda r, h, j: (h, jnp.minimum(j + r, nb - 1), 0), r))
                for r in range(3)]

    own = pl.BlockSpec((None, QB, dh), lambda h, j: (h, j, 0))
    return pl.pallas_call(
        body, name=name, out_shape=[_sds((nh, t, dh), BF16)] * 2, grid=(nh, nb),
        in_specs=[own, own, pl.BlockSpec((None, QB, KW), lambda h, j: (h, 0, 0))]
        + later(dh) + later(1) + later(1) + later(dh),
        out_specs=[own, own], compiler_params=_params("parallel", "parallel"),
    )(k, v, bias, q, q, q, lse, lse, lse, delta, delta, delta, do, do, do)


def _tri(strict):
    r = lax.broadcasted_iota(jnp.int32, (CHUNK, CHUNK), 0)
    c = lax.broadcasted_iota(jnp.int32, (CHUNK, CHUNK), 1)
    return jnp.where((c < r) if strict else (c <= r), 1.0, 0.0).astype(F32)


def _gate(lr, wa, ba):
    y = _dot(lr, wa) + ba
    return (jnp.minimum(y, 0.0) - jnp.log(1.0 + jnp.exp(-jnp.abs(y)))) / GATE_TAU, y


def _decays(la):
    cum = jnp.dot(_tri(False), la, preferred_element_type=F32, precision=HI)
    last = cum[CHUNK - 1:CHUNK, :]
    return jnp.exp(last - cum), jnp.exp(last)


def _gla_specs(cols, hk, hv, order):
    def at(start, width):
        return pl.BlockSpec((GB, width), lambda h, i: (order(i), start // width + h))
    return [at(cols["qb"], hk), at(cols["kb"], hk), at(cols["vb"], hv), at(cols["rb"], hv),
            pl.BlockSpec((GB, LANE), lambda h, i: (order(i), cols["lr"] // LANE))]


def gla_fwd(p, cols, wa, ba, gn, nh, hk, hv, name):
    t = p.shape[0]
    nc = t // CHUNK
    scale = hk ** -0.5
    per = GB // CHUNK

    def body(q_ref, k_ref, v_ref, r_ref, lr_ref, wa_ref, ba_ref, gn_ref, o_ref, y_ref, st_ref, state):
        @pl.when(pl.program_id(1) == 0)
        def _():
            state[...] = jnp.zeros_like(state)

        for c in range(per):
            rows = pl.ds(c * CHUNK, CHUNK)
            la, _ = _gate(lr_ref[rows, :], wa_ref[...], ba_ref[...])
            w, decay = _decays(la)
            kdec = (k_ref[rows, :].astype(F32) * w).astype(BF16)
            st = decay * state[...] + _dot_tn(v_ref[rows, :], kdec)
            state[...] = st
            st_ref[c] = st
            o = _dot_nt(q_ref[rows, :], st.astype(BF16)) * scale
            o_ref[rows, :] = o
            rinv = lax.rsqrt(jnp.mean(o * o, axis=-1, keepdims=True) + RMS_EPS)
            rv = r_ref[rows, :].astype(F32)
            y_ref[rows, :] = (o * rinv * gn_ref[...] * (rv * _sigmoid(rv))).astype(BF16)

    return pl.pallas_call(
        body, name=name,
        out_shape=[_sds((t, nh * hv), F32), _sds((t, nh * hv), BF16), _sds((nh, nc, hv, hk), F32)],
        grid=(nh, t // GB),
        in_specs=_gla_specs(cols, hk, hv, lambda i: i)
        + [pl.BlockSpec((LANE, hk), lambda h, i: (0, h)), pl.BlockSpec((1, hk), lambda h, i: (0, h)),
           pl.BlockSpec((1, hv), lambda h, i: (0, 0))],
        out_specs=[pl.BlockSpec((GB, hv), lambda h, i: (i, h)), pl.BlockSpec((GB, hv), lambda h, i: (i, h)),
                   pl.BlockSpec((None, per, hv, hk), lambda h, i: (h, i, 0, 0))],
        scratch_shapes=[pltpu.VMEM((hv, hk), F32)], compiler_params=_params("parallel", "arbitrary"),
    )(p, p, p, p, p, wa, ba, gn)


def gla_bwd(p, cols, wa, ba, gn, o, states, dy, nh, hk, hv, name):
    t = p.shape[0]
    nblk = t // GB
    scale = hk ** -0.5
    per = GB // CHUNK

    def rev(i):
        return nblk - 1 - i

    def body(q_ref, k_ref, v_ref, r_ref, lr_ref, wa_ref, ba_ref, gn_ref, o_ref, st_ref, sp_ref, dy_ref,
             dq_ref, dk_ref, dv_ref, dr_ref, dg_ref, dgn_ref, carry):
        h, i = pl.program_id(0), pl.program_id(1)

        @pl.when(i == 0)
        def _():
            carry[...] = jnp.zeros_like(carry)

        @pl.when((i == 0) & (h == 0))
        def _():
            dgn_ref[...] = jnp.zeros_like(dgn_ref)

        gnv = gn_ref[...]
        for c in reversed(range(per)):
            rows = pl.ds(c * CHUNK, CHUNK)
            rv = r_ref[rows, :].astype(F32)
            sg = _sigmoid(rv)
            dyv = dy_ref[rows, :].astype(F32)
            ov = o_ref[rows, :]
            rinv = lax.rsqrt(jnp.mean(ov * ov, axis=-1, keepdims=True) + RMS_EPS)
            dn = dyv * (rv * sg)
            dr_ref[rows, :] = (dyv * (ov * rinv * gnv) * (sg * (1.0 + rv * (1.0 - sg)))).astype(BF16)
            dgn_ref[...] += _colsum(dn * ov * rinv)
            dxh = dn * gnv
            do = rinv * dxh - ov * (rinv * rinv * rinv) * jnp.mean(dxh * ov, axis=-1, keepdims=True)
            dob = (do * scale).astype(BF16)
            qv, kv, vv = q_ref[rows, :], k_ref[rows, :], v_ref[rows, :]
            dq_ref[rows, :] = _dot(dob, st_ref[c].astype(BF16)).astype(BF16)
            dst = carry[...] + _dot_tn(dob, qv)
            if c > 0:
                prev = st_ref[c - 1]
            else:
                prev = jnp.where(i == nblk - 1, 0.0, sp_ref[0])
            ddecay = _colsum(dst * prev)
            la, y = _gate(lr_ref[rows, :], wa_ref[...], ba_ref[...])
            w, decay = _decays(la)
            kf = kv.astype(F32)
            kdec = (kf * w).astype(BF16)
            dstb = dst.astype(BF16)
            dkdec = _dot(vv, dstb)
            dv_ref[rows, :] = _dot_nt(kdec, dstb).astype(BF16)
            dk_ref[rows, :] = (dkdec * w).astype(BF16)
            e = dkdec * kf * w
            dla = jnp.dot(_tri(True), e, preferred_element_type=F32, precision=HI) + ddecay * decay
            dg_ref[rows, :] = dla * (1.0 / GATE_TAU) * _sigmoid(-y)
            carry[...] = decay * dst

    per_head = lambda width: pl.BlockSpec((GB, width), lambda h, i: (rev(i), h))
    return pl.pallas_call(
        body, name=name,
        out_shape=[_sds((t, nh * hk), BF16), _sds((t, nh * hk), BF16), _sds((t, nh * hv), BF16),
                   _sds((t, nh * hv), BF16), _sds((t, nh * hk), F32), _sds((1, hv), F32)],
        grid=(nh, nblk),
        in_specs=_gla_specs(cols, hk, hv, rev)
        + [pl.BlockSpec((LANE, hk), lambda h, i: (0, h)), pl.BlockSpec((1, hk), lambda h, i: (0, h)),
           pl.BlockSpec((1, hv), lambda h, i: (0, 0)), per_head(hv),
           pl.BlockSpec((None, per, hv, hk), lambda h, i: (h, rev(i), 0, 0)),
           pl.BlockSpec((None, 1, hv, hk), lambda h, i: (h, jnp.maximum(rev(i) * per - 1, 0), 0, 0)),
           per_head(hv)],
        out_specs=[per_head(hk), per_head(hk), per_head(hv), per_head(hv), per_head(hk),
                   pl.BlockSpec((1, hv), lambda h, i: (0, 0))],
        scratch_shapes=[pltpu.VMEM((hv, hk), F32)], compiler_params=_params("arbitrary", "arbitrary"),
    )(p, p, p, p, p, wa, ba, gn, o, states, states, dy)


def gate_bwd(p, lr_col, dg, wa, name):
    t, kd = dg.shape
    tm = _row_tile(t, 512, 16)

    def body(lr_ref, dg_ref, wa_ref, dlr_ref, dwa_ref, dba_ref):
        @pl.when(pl.program_id(0) == 0)
        def _():
            dwa_ref[...] = jnp.zeros_like(dwa_ref)
            dba_ref[...] = jnp.zeros_like(dba_ref)

        g = dg_ref[...]
        gb = g.astype(BF16)
        dlr_ref[...] = _dot_nt(gb, wa_ref[...]).astype(BF16)
        dwa_ref[...] += _dot_tn(lr_ref[...], gb)
        dba_ref[...] += _colsum(g)

    return pl.pallas_call(
        body, name=name, out_shape=[_sds((t, LANE), BF16), _sds((LANE, kd), F32), _sds((1, kd), F32)],
        grid=(t // tm,),
        in_specs=[pl.BlockSpec((tm, LANE), lambda i: (i, lr_col // LANE)), pl.BlockSpec((tm, kd), lambda i: (i, 0)),
                  pl.BlockSpec((LANE, kd), lambda i: (0, 0))],
        out_specs=[pl.BlockSpec((tm, LANE), lambda i: (i, 0)), pl.BlockSpec((LANE, kd), lambda i: (0, 0)),
                   pl.BlockSpec((1, kd), lambda i: (0, 0))],
        compiler_params=_params("arbitrary"),
    )(p, dg, wa)


def proj_merge(ya, yb, wa3, wb3, p, ga_col, gb_col, name):
    t, kd = ya.shape
    nb, _, bw = wa3.shape
    tm = _row_tile(t, 512, 16)

    def body(ya_ref, yb_ref, wa_ref, wb_ref, ga_ref, gb_ref, pa_ref, pb_ref, mg_ref):
        pa = _dot(ya_ref[...], wa_ref[...])
        pb = _dot(yb_ref[...], wb_ref[...])
        pa_ref[...] = pa.astype(BF16)
        pb_ref[...] = pb.astype(BF16)
        mg_ref[...] = (_sigmoid(ga_ref[...].astype(F32)) * pa + _sigmoid(gb_ref[...].astype(F32)) * pb).astype(BF16)

    act = pl.BlockSpec((tm, kd), lambda j, i: (i, 0))
    wsp = pl.BlockSpec((None, kd, bw), lambda j, i: (j, 0, 0))
    out = pl.BlockSpec((tm, bw), lambda j, i: (i, j))
    return pl.pallas_call(
        body, name=name, out_shape=[_sds((t, nb * bw), BF16)] * 3, grid=(nb, t // tm),
        in_specs=[act, act, wsp, wsp, pl.BlockSpec((tm, bw), lambda j, i: (i, ga_col // bw + j)),
                  pl.BlockSpec((tm, bw), lambda j, i: (i, gb_col // bw + j))],
        out_specs=[out] * 3, compiler_params=_params("parallel", "parallel"),
    )(ya, yb, wa3, wb3, p, p)


def merge_bwd(dm, w, p, ga_col, gb_col, pa, pb, name):
    t, d = dm.shape
    n = w.shape[0]
    tn = _row_tile(n, 512, LANE)
    tm = _row_tile(t, 256, 16)

    def body(dm_ref, w_ref, ga_ref, gb_ref, pa_ref, pb_ref, dpa_ref, dpb_ref, dga_ref, dgb_ref):
        dmg = _dot_nt(dm_ref[...], w_ref[...])
        sa = _sigmoid(ga_ref[...].astype(F32))
        sb = _sigmoid(gb_ref[...].astype(F32))
        dpa_ref[...] = (dmg * sa).astype(BF16)
        dpb_ref[...] = (dmg * sb).astype(BF16)
        dga_ref[...] = (dmg * pa_ref[...].astype(F32) * sa * (1.0 - sa)).astype(BF16)
        dgb_ref[...] = (dmg * pb_ref[...].astype(F32) * sb * (1.0 - sb)).astype(BF16)

    out = pl.BlockSpec((tm, tn), lambda j, i: (i, j))
    return pl.pallas_call(
        body, name=name, out_shape=[_sds((t, n), BF16)] * 4, grid=(n // tn, t // tm),
        in_specs=[pl.BlockSpec((tm, d), lambda j, i: (i, 0)), pl.BlockSpec((tn, d), lambda j, i: (j, 0)),
                  pl.BlockSpec((tm, tn), lambda j, i: (i, ga_col // tn + j)),
                  pl.BlockSpec((tm, tn), lambda j, i: (i, gb_col // tn + j)), out, out],
        out_specs=[out] * 4, compiler_params=_params("parallel", "parallel"),
    )(dm, w, p, p, pa, pb)


def rel_bias_grad(skew, clip_map, name):
    nh, _, jd = skew.shape
    n_rel = clip_map.shape[1]

    def body(s_ref, c_ref, o_ref):
        sums = jnp.concatenate([_colsum(s_ref[h]) for h in range(nh)], axis=0)
        o_ref[...] = jnp.dot(sums, c_ref[...], preferred_element_type=F32, precision=HI)

    return pl.pallas_call(
        body, name=name, out_shape=_sds((nh, n_rel), F32), in_specs=[VMEM_SPEC, VMEM_SPEC], out_specs=VMEM_SPEC,
        compiler_params=pltpu.CompilerParams(vmem_limit_bytes=VMEM_LIMIT),
    )(skew, clip_map)


MIX_BLOCK = 9 * LANE


def mix_layout(d, a_width, bk, bv):
    main = 3 * a_width + 2 * bk + 2 * bv
    cols = {"qa": 0, "ka": a_width, "va": 2 * a_width, "qb": 3 * a_width, "kb": 3 * a_width + bk,
            "vb": 3 * a_width + 2 * bk, "rb": 3 * a_width + 2 * bk + bv, "ga": main, "gb": main + d,
            "lr": main + 2 * d}
    total = main + 2 * d + LANE
    assert total % MIX_BLOCK == 0
    return cols, main, total


def mix_weight_in(wt, main, rank):
    d = wt.shape[1]
    return jnp.concatenate([wt[:main], wt[main + rank:], wt[main:main + rank],
                            jnp.zeros((LANE - rank, d), wt.dtype)], axis=0)


def mix_weight_grad_out(gt, main, rank):
    d = gt.shape[1]
    w = jnp.concatenate([gt[:main], gt[main + 2 * d:main + 2 * d + rank], gt[main:main + 2 * d]], axis=0)
    return w.reshape(N_DEV, -1, d)


def heads_major(x, nh):
    t = x.shape[0]
    return x.reshape(t, nh, -1).transpose(1, 0, 2)


def heads_minor(x):
    nh, t, dh = x.shape
    return x.transpose(1, 0, 2).reshape(t, nh * dh)


def ffn_forward(h, sh, sc, g, w_in3, w_out_of, ln_g, ln_b, tag):
    a, b, s = ffn_in(h, sh, sc, w_in3, f"{tag}_in")
    w_out = w_out_of(s)
    f, z, hout = out_ln(s, w_out, h, g, ln_g, ln_b, 0.5, f"{tag}_out")
    return hout, (h, a, b, s, f, z), w_out


def ffn_backward_weights(dh, saved, sh, sc, g, w_in3, w_out, ln_g, tag, target=None):
    hin, a, b, s, f, z = saved
    t, d = hin.shape
    nb, _, bw = w_in3.shape
    half = nb // 2
    fdim = w_out.shape[0]
    res = ln_bwd(dh, z, f, ln_g, g, 0.5, f"{tag}_ln_bwd", target=target)
    dz, df, dln_g, dln_b, dg = res[:5]
    dab = ffn_bwd_act(df, w_out, a, b, f"{tag}_act_bwd")
    tk = _row_tile(t, 512, 16)
    dw_out = matmul_tn(f"{tag}_dwout", s, (tk, bw), lambda n, k: (k, n), df, (tk, d), lambda n, k: (k, 0),
                       _sds((fdim, d), BF16), (bw, d), lambda n, k: (n, 0), fdim // bw)
    dw_in = matmul_tn(f"{tag}_dwin", hin, (tk, d), lambda n, k: (k, 0), dab, (None, tk, bw),
                      lambda n, k: (n // half, k, n % half), _sds((nb, d, bw), BF16), (None, d, bw),
                      lambda n, k: (n, 0, 0), nb, mod=(sh, sc))
    grads = dict(w_in=dw_in, w_out=dw_out.reshape(N_DEV, fdim // N_DEV, d), ln_g=dln_g, ln_b=dln_b, g=dg)
    return (dab, dz), grads, (res[5] if target is not None else None)


def ffn_backward_input(carry, saved, sc, w_in3, tag):
    dab, dz = carry
    hin = saved[0]
    t = hin.shape[0]
    nb, _, bw = w_in3.shape
    half = nb // 2
    tm = _row_tile(t, 512, 16)
    return matmul_nt_blocks(f"{tag}_du", dab, (None, tm, bw), lambda i, k: (k // half, i, k % half),
                            w_in3, t, resid=(dz, hin, sc))


def _after(v, token):
    return v if token is None else v + token[:1, :1]


def local_step(x, target, mod, weights_of, grads_ready, rel_bias, w_alpha2, b_alpha, gla_norm_g, lns):
    t, d = x.shape
    sh1, sc1, g1, sh2, sc2, g2, sh3, sc3, g3 = [mod[i:i + 1] for i in range(N_MOD)]
    ln1_g, ln1_b, ln2_g, ln2_b, ln3_g, ln3_b = lns
    n_heads_a, n_rel = rel_bias.shape
    rank, bk = w_alpha2.shape
    hv = gla_norm_g.shape[1]

    w1 = weights_of("ffn1_in", x)
    h1, saved1, w1["out"] = ffn_forward(x, sh1, sc1, g1, w1["in"], lambda s: weights_of("ffn1_out", s)["out"],
                                        ln1_g, ln1_b, "ffn1")
    wm = weights_of("mix", h1)
    a_width = wm["proj_a"].shape[1]
    bv = wm["proj_b"].shape[1]
    nh_b = bv // hv
    hk = bk // nh_b
    cols, main, total = mix_layout(d, a_width, bk, bv)
    w_mix = mix_weight_in(wm["in_t"], main, rank)
    p = mod_matmul(h1, sh2, sc2, w_mix, MIX_BLOCK, "mix_in")
    bias = bias_table(rel_bias)
    qa, ka, va = [heads_major(p[:, cols[n]:cols[n] + a_width], n_heads_a) for n in ("qa", "ka", "va")]
    ya_h, lse = attn_fwd(qa, ka, va, bias, "attn_fwd")
    ya = heads_minor(ya_h)
    wa_pad = jnp.zeros((LANE, bk), BF16).at[:rank].set(w_alpha2.astype(BF16))
    o_b, yb, states = gla_fwd(p, cols, wa_pad, b_alpha, gla_norm_g, nh_b, hk, hv, "gla_fwd")
    pa, pb, merged = proj_merge(ya, yb, wm["proj_a"], wm["proj_b"], p, cols["ga"], cols["gb"], "proj_merge")
    m, z2, h2 = out_ln(merged, wm["out"], h1, g2, ln2_g, ln2_b, 1.0, "mix_out")
    w3 = weights_of("ffn2", h2)
    h3, saved3, _ = ffn_forward(h2, sh3, sc3, g3, w3["in"], lambda s: w3["out"], ln3_g, ln3_b, "ffn2")

    carry3, gr3, loss = ffn_backward_weights(h3, saved3, sh3, sc3, g3, w3["in"], w3["out"], ln3_g, "ffn2",
                                             target=target)
    token = grads_ready("ffn2", dict(ffn2_in=gr3["w_in"], ffn2_out=gr3["w_out"]))
    dh2, dsc3, dsh3 = ffn_backward_input(carry3, saved3, _after(sc3, token), w3["in"], "ffn2")
    dz2, dm, dln2_g, dln2_b, dg2 = ln_bwd(dh2, z2, m, ln2_g, g2, 1.0, "mix_ln_bwd")
    dpa, dpb, dga, dgb = merge_bwd(dm, wm["out"], p, cols["ga"], cols["gb"], pa, pb, "merge_bwd")
    tk = _row_tile(t, 512, 16)
    dw_mix_out = matmul_tn("mix_dwout", merged, (tk, 512), lambda n, k: (k, n), dm, (tk, d), lambda n, k: (k, 0),
                           _sds((d, d), BF16), (512, d), lambda n, k: (n, 0), d // 512)
    tm = _row_tile(t, 512, 16)
    pbw = wm["proj_a"].shape[2]
    dya = matmul_nt_blocks("proj_a_dy", dpa, (tm, pbw), lambda i, k: (i, k), wm["proj_a"], t)
    dyb = matmul_nt_blocks("proj_b_dy", dpb, (tm, pbw), lambda i, k: (i, k), wm["proj_b"], t)
    dw_pa = matmul_tn("proj_a_dw", ya, (tk, a_width), lambda n, k: (k, 0), dpa, (tk, pbw), lambda n, k: (k, n),
                      _sds((N_DEV, a_width, pbw), BF16), (None, a_width, pbw), lambda n, k: (n, 0, 0), N_DEV)
    dw_pb = matmul_tn("proj_b_dw", yb, (tk, bv), lambda n, k: (k, 0), dpb, (tk, pbw), lambda n, k: (k, n),
                      _sds((N_DEV, bv, pbw), BF16), (None, bv, pbw), lambda n, k: (n, 0, 0), N_DEV)
    dqb, dkb, dvb, drb, dgate, dgn = gla_bwd(p, cols, wa_pad, b_alpha, gla_norm_g, o_b, states, dyb,
                                             nh_b, hk, hv, "gla_bwd")
    dlr, dwa_pad, dba = gate_bwd(p, cols["lr"], dgate, wa_pad, "gate_bwd")
    do_h = heads_major(dya, n_heads_a)
    dqa_h, delta, dbias = attn_bwd_q(qa, ka, va, bias, lse, do_h, "attn_bwd_q")
    dka_h, dva_h = attn_bwd_kv(qa, ka, va, bias, lse, delta, do_h, "attn_bwd_kv")
    d_rel = rel_bias_grad(bias_grad_skew(dbias), jnp.asarray(bias_clip_map(n_rel)), "rel_bias_grad")
    dp = jnp.concatenate([heads_minor(dqa_h), heads_minor(dka_h), heads_minor(dva_h), dqb, dkb, dvb, drb,
                          dga, dgb, dlr], axis=1)
    dw_mix_t = matmul_tn("mix_dwin", dp, (tk, MIX_BLOCK), lambda n, k: (k, n), h1, (tk, d), lambda n, k: (k, 0),
                         _sds((total, d), BF16), (MIX_BLOCK, d), lambda n, k: (n, 0), total // MIX_BLOCK,
                         mod=(sh2, sc2), mod_b=True)
    token = grads_ready("mix", dict(mix_in=mix_weight_grad_out(dw_mix_t, main, rank), proj_a=dw_pa, proj_b=dw_pb,
                                    mix_out=dw_mix_out.reshape(N_DEV, d // N_DEV, d)))
    tm = _row_tile(t, 512, 16)
    dh1, dsc2, dsh2 = matmul_nt_blocks("mix_du", dp, (tm, MIX_BLOCK), lambda i, k: (i, k), w_mix, t,
                                       resid=(dz2, h1, _after(sc2, token)))
    carry1, gr1, _ = ffn_backward_weights(dh1, saved1, sh1, sc1, g1, w1["in"], w1["out"], ln1_g, "ffn1")
    token = grads_ready("ffn1", dict(ffn1_in=gr1["w_in"], ffn1_out=gr1["w_out"]))
    dx, dsc1, dsh1 = ffn_backward_input(carry1, saved1, _after(sc1, token), w1["in"], "ffn1")

    dmod = [dsh1, dsc1, gr1["g"], dsh2, dsc2, dg2, dsh3, dsc3, gr3["g"]]
    small = dict(ln1_g=gr1["ln_g"], ln1_b=gr1["ln_b"], ln2_g=dln2_g, ln2_b=dln2_b, ln3_g=gr3["ln_g"],
                 ln3_b=gr3["ln_b"], b_alpha=dba, gla_norm_g=dgn, w_alpha2=dwa_pad[:rank], rel_bias=d_rel)
    return loss, dx, dmod, small


GROUPS = dict(ffn1=("ffn1_in", "ffn1_out"), mix=("mix_in", "proj_a", "proj_b", "mix_out"),
              ffn2=("ffn2_in", "ffn2_out"))
GATHERS = dict(ffn1_in=("ffn1_in",), ffn1_out=("ffn1_out",), mix=GROUPS["mix"], ffn2=GROUPS["ffn2"])
SMALL_REPLICATED = ("b_ada", "ln1_g", "ln1_b", "ln2_g", "ln2_b", "ln3_g", "ln3_b", "b_alpha", "gla_norm_g")
SMALL_SHARDED = ("rel_bias", "w_alpha2")
WEIGHT_ORDER = ("w_ada", "b_ada", "ffn1_w_in", "ffn1_w_out", "ln1_g", "ln1_b", "w_mix_in", "rel_bias", "w_alpha2",
                "b_alpha", "gla_norm_g", "w_proj_a", "w_proj_b", "w_mix_out", "ln2_g", "ln2_b", "ffn2_w_in",
                "ffn2_w_out", "ln3_g", "ln3_b")
BIG_NAME = dict(ffn1_in="ffn1_w_in", ffn1_out="ffn1_w_out", mix_in="w_mix_in", proj_a="w_proj_a",
                proj_b="w_proj_b", mix_out="w_mix_out", ffn2_in="ffn2_w_in", ffn2_out="ffn2_w_out")


def kernel(x, c, w_ada, b_ada, ffn1_w_in, ffn1_w_out, ln1_g, ln1_b, w_mix_in, rel_bias, w_alpha2, b_alpha, gla_norm_g, w_proj_a, w_proj_b, w_mix_out, ln2_g, ln2_b, ffn2_w_in, ffn2_w_out, ln3_g, ln3_b, loss_target, m_w_ada, m_b_ada, m_ffn1_w_in, m_ffn1_w_out, m_ln1_g, m_ln1_b, m_w_mix_in, m_rel_bias, m_w_alpha2, m_b_alpha, m_gla_norm_g, m_w_proj_a, m_w_proj_b, m_w_mix_out, m_ln2_g, m_ln2_b, m_ffn2_w_in, m_ffn2_w_out, m_ln3_g, m_ln3_b, v_w_ada, v_b_ada, v_ffn1_w_in, v_ffn1_w_out, v_ln1_g, v_ln1_b, v_w_mix_in, v_rel_bias, v_w_alpha2, v_b_alpha, v_gla_norm_g, v_w_proj_a, v_w_proj_b, v_w_mix_out, v_ln2_g, v_ln2_b, v_ffn2_w_in, v_ffn2_w_out, v_ln3_g, v_ln3_b):
    env = dict(locals())
    w = {n: env[n] for n in WEIGHT_ORDER}
    mom = {n: env["m_" + n] for n in WEIGHT_ORDER}
    var = {n: env["v_" + n] for n in WEIGHT_ORDER}
    me = _me()
    dev = _lin(me)
    core = jnp.reshape(me[2], (1,)).astype(jnp.int32)
    chip = jnp.reshape(2 * me[0] + me[1], (1,)).astype(jnp.int32)
    d = x.shape[-1]

    ada_cols = w_ada.shape[-1]
    c_all = all_gather_small(c, "gather_c")[:, 0, :]
    b_cols = lax.dynamic_slice_in_dim(b_ada, dev * ada_cols, ada_cols, axis=1)
    mod_cols = adaln_cols(c_all, w_ada[0], b_cols, "adaln_cols")
    mod_all = all_gather_small(mod_cols, "gather_mod")
    mod = lax.dynamic_index_in_dim(mod_all, dev, axis=1, keepdims=False).reshape(N_MOD, d)

    small_w = all_gather_small(jnp.concatenate([rel_bias[0], w_alpha2[0]], axis=1), "gather_small_w")
    n_rel_cols = rel_bias.shape[-1]
    rel_full = small_w[:, :, :n_rel_cols].transpose(1, 0, 2).reshape(small_w.shape[1], -1)
    wa2_full = small_w[:, :, n_rel_cols:].transpose(1, 0, 2).reshape(small_w.shape[1], -1)

    def shard(n):
        s = w[BIG_NAME[n]][0].astype(BF16)
        return s.T if n == "mix_in" else s

    dev_idx = jnp.reshape(dev, (1,)).astype(jnp.int32)
    shards = {n: shard(n) for members in GATHERS.values() for n in members}
    lands = {n: place_own(dev_idx, s, f"place_own_{n}") for n, s in shards.items()}
    started = {}
    order = small_w
    for grp, members in GATHERS.items():
        started[grp] = gather_start([shards[n] for n in members], [lands[n] for n in members], order,
                                    f"gather_start_{grp}")
        order = started[grp][-1]
    mod = _after(mod, order)

    def weights_of(grp, after):
        _, zones = gather_wait(started[grp], after, f"gather_wait_{grp}")
        full = dict(zip(GATHERS[grp], gather_forward(zones, f"gather_forward_{grp}")))
        if grp == "mix":
            return dict(in_t=full["mix_in"].reshape(-1, d), proj_a=full["proj_a"], proj_b=full["proj_b"],
                        out=full["mix_out"].reshape(-1, d))
        return {"in" if n.endswith("_in") else "out": v if n.endswith("_in") else v.reshape(-1, d)
                for n, v in full.items()}

    exchanges = {}

    def grads_ready(grp, grads):
        names = GROUPS[grp]
        got = pair_exchange([grads[n] for n in names], f"grad_pair_exchange_{grp}")
        sums = [pair_add(core, grads[n], g, f"grad_pair_add_{n}") for n, g in zip(names, got)]
        exchanges[grp] = chip_exchange_start(sums, f"grad_chip_start_{grp}")
        return exchanges[grp][-1]

    lns = [ln1_g, ln1_b, ln2_g, ln2_b, ln3_g, ln3_b]
    loss, dx, dmod, small = local_step(x[0], loss_target[0], mod, weights_of, grads_ready, rel_full, wa2_full,
                                       b_alpha, gla_norm_g, lns)
    loss = lax.psum(loss[0, 0], ("x", "y", "c"))

    packed = jnp.concatenate([g.reshape(1, -1) for g in dmod]
                             + [small[n].reshape(1, -1) for n in SMALL_REPLICATED[1:] + SMALL_SHARDED], axis=1)
    parts = all_gather_small(packed, "gather_small_grads")
    n_mod = N_MOD * d
    dmod_all = parts[:, 0, :n_mod]
    g_w_ada = adaln_wgrad(c_all, lax.dynamic_slice_in_dim(dmod_all, dev * ada_cols, ada_cols, axis=1), "adaln_wgrad")
    out = {}
    out["w_ada"] =[o[None] for o in adamw_sum(g_w_ada[None], w_ada[0], m_w_ada[0], v_w_ada[0], "adamw_w_ada")]

    def pack(src):
        rows = [src[n].reshape(1, -1) for n in SMALL_REPLICATED]
        return jnp.concatenate(rows + [src[n].reshape(1, -1) for n in SMALL_SHARDED], axis=1)

    n_rep = sum(w[n].size for n in SMALL_REPLICATED)
    rep_parts = parts[:, :, :n_rep]
    off = n_rep
    shard_parts = []
    for n in SMALL_SHARDED:
        rows, cols_local = w[n].shape[1], w[n].shape[2]
        full_part = parts[:, 0, off:off + rows * cols_local * N_DEV].reshape(N_DEV, rows, cols_local * N_DEV)
        mine = lax.dynamic_slice_in_dim(full_part, dev * cols_local, cols_local, axis=2)
        shard_parts.append(mine.reshape(N_DEV, 1, rows * cols_local))
        off += rows * cols_local * N_DEV
    small_parts = jnp.concatenate([rep_parts] + shard_parts, axis=2)
    res = adamw_sum(small_parts, pack(w), pack(mom), pack(var), "adamw_small")
    off = 0
    for n in SMALL_REPLICATED + SMALL_SHARDED:
        size = w[n].size
        out[n] = [r[:, off:off + size].reshape(w[n].shape) for r in res]
        off += size

    order = res[0]
    for grp in reversed(list(GROUPS)):
        sums, recv = chip_exchange_wait(exchanges[grp], order, f"grad_chip_wait_{grp}")
        for n, hsum, r in zip(GROUPS[grp], sums, recv):
            full = BIG_NAME[n]
            if n == "mix_in":
                g = owned_sum(chip, hsum, r, f"owned_sum_{n}").T
                res_n = adamw_sum(g[None], w[full][0], mom[full][0], var[full][0], f"adamw_{n}")
            else:
                res_n = adamw_owned(chip, hsum, r, w[full][0], mom[full][0], var[full][0], f"adamw_{n}")
            out[full] = [o[None] for o in res_n]
            order = res_n[0]

    flat = [loss, dx[None]]
    for k in range(4):
        flat += [out[n][k] for n in WEIGHT_ORDER]
    return tuple(flat)
```

```python
import functools

import numpy as np
import jax
import jax.numpy as jnp
from jax import lax
from jax.experimental import pallas as pl
from jax.experimental.pallas import tpu as pltpu

F32 = jnp.float32
BF16 = jnp.bfloat16
MESH = pl.DeviceIdType.MESH
N_DEV = 8
N_CHIP = 4

CHUNK = 64
A_PAST_CHUNKS = 8
REL_CLIP = 256
GATE_TAU = 16.0
N_MOD = 9
DEPTH = 1
ALPHA = (2.0 * DEPTH) ** 0.25
LN_EPS = 1e-5
RMS_EPS = 1e-6
ADAM_LR = 0.001
ADAM_B1 = 0.9
ADAM_B2 = 0.999
ADAM_EPS = 1e-08
ADAM_WD = 0.01
ADAM_STEP = 10

LANE = 128
VMEM_LIMIT = 56 * 2 ** 20
QB = 4 * CHUNK
KW = 3 * QB
GB = 8 * CHUNK
NEG = -1e30
K_TILE_CAP = 11 * LANE
HI = lax.Precision.HIGHEST

ANY = pl.BlockSpec(memory_space=pl.ANY)
VMEM_SPEC = pl.BlockSpec(memory_space=pltpu.VMEM)


def _params(*sem):
    return pltpu.CompilerParams(dimension_semantics=sem, vmem_limit_bytes=VMEM_LIMIT)


def _sds(shape, dtype):
    return jax.ShapeDtypeStruct(shape, dtype)


def _dot(a, b):
    return jnp.dot(a, b, preferred_element_type=F32)


def _dot_nt(a, b):
    return lax.dot_general(a, b, (((1,), (1,)), ((), ())), preferred_element_type=F32)


def _dot_tn(a, b):
    return lax.dot_general(a, b, (((0,), (0,)), ((), ())), preferred_element_type=F32)


def _sigmoid(x):
    return 1.0 / (1.0 + jnp.exp(-x))


def _colsum(x):
    return jnp.sum(x, axis=0, keepdims=True)


def _row_tile(rows, cap, mult):
    for t in range(min(rows, cap), 0, -1):
        if rows % t == 0 and t % mult == 0:
            return t
    return rows


def _me():
    return lax.axis_index("x"), lax.axis_index("y"), lax.axis_index("c")


def _flip(me, k):
    return tuple((1 - p) if (k >> s) & 1 else p for p, s in zip(me, (2, 1, 0)))


def _lin(p):
    return 4 * p[0] + 2 * p[1] + p[2]


def all_gather_small(x, name):
    r, n = x.shape

    def body(x_ref, out_ref, send_sems, recv_sems, local_sem):
        me = _me()
        mine = pltpu.make_async_copy(x_ref, out_ref.at[_lin(me)], local_sem)
        mine.start()
        sends = []
        for k in range(1, N_DEV):
            cp = pltpu.make_async_remote_copy(
                src_ref=x_ref, dst_ref=out_ref.at[_lin(me)], send_sem=send_sems.at[k - 1],
                recv_sem=recv_sems.at[k - 1], device_id=_flip(me, k), device_id_type=MESH)
            cp.start()
            sends.append(cp)
        for k in range(1, N_DEV):
            peer = _flip(me, k)
            pltpu.make_async_remote_copy(
                src_ref=x_ref, dst_ref=out_ref.at[_lin(peer)], send_sem=send_sems.at[k - 1],
                recv_sem=recv_sems.at[k - 1], device_id=peer, device_id_type=MESH).wait_recv()
        for cp in sends:
            cp.wait_send()
        mine.wait()

    return pl.pallas_call(
        body, name=name, out_shape=_sds((N_DEV, r, n), x.dtype),
        in_specs=[VMEM_SPEC], out_specs=VMEM_SPEC,
        scratch_shapes=[pltpu.SemaphoreType.DMA((N_DEV - 1,)), pltpu.SemaphoreType.DMA((N_DEV - 1,)),
                        pltpu.SemaphoreType.DMA],
    )(x)


HBM_SPEC = pl.BlockSpec(memory_space=pltpu.HBM)
SEM_SPEC = pl.BlockSpec(memory_space=pltpu.SEMAPHORE)
EFFECT = pltpu.SideEffectType.DATAFLOW_SIDE_EFFECTING
FIRST = N_CHIP


def _hbm(v):
    return pltpu.with_memory_space_constraint(v, pltpu.HBM)


def _other_chips(mx, my):
    return [(1 - mx, my), (mx, 1 - my), (1 - mx, 1 - my)]


def place_own(dev, shard, name):
    rows, cols = shard.shape
    tr, tc = _tile2(rows, cols, 16)

    def body(dev_ref, s_ref, land_ref, o_ref):
        o_ref[...] = s_ref[...]

    land = lax.empty((N_DEV, rows, cols), shard.dtype)
    return pl.pallas_call(
        body, name=name, out_shape=_sds(land.shape, land.dtype),
        grid_spec=pltpu.PrefetchScalarGridSpec(
            num_scalar_prefetch=1, grid=(rows // tr, cols // tc),
            in_specs=[pl.BlockSpec((tr, tc), lambda i, j, d: (i, j)), ANY],
            out_specs=pl.BlockSpec((None, tr, tc), lambda i, j, d: (d[0], i, j))),
        input_output_aliases={2: 0}, compiler_params=_params("parallel", "parallel"),
    )(dev, shard, land)


def gather_start(shards, lands, after, name):
    n = len(shards)

    def body(*refs):
        ins, zones = refs[:n], refs[n:2 * n]
        send_sems, recv_sems = refs[2 * n + 1], refs[2 * n + 2]
        token = refs[-1]
        me = _me()
        mx, my, mc = me
        for a in range(n):
            dst = zones[a].at[_lin(me)]
            targets = [(mx, my, 1 - mc)] + [(*chip, mc) for chip in _other_chips(mx, my)]
            for k, to in enumerate(targets):
                pltpu.make_async_remote_copy(
                    src_ref=ins[a], dst_ref=dst, send_sem=send_sems.at[a * FIRST + k],
                    recv_sem=recv_sems.at[a * FIRST + k], device_id=to, device_id_type=MESH).start()
        token[...] = jnp.zeros_like(token)

    sems = pltpu.SemaphoreType.DMA((n * FIRST,))
    out = pl.pallas_call(
        body, name=name,
        out_shape=(sems, sems, *[pltpu.HBM(s.shape, s.dtype) for s in shards],
                   *[pltpu.HBM(z.shape, z.dtype) for z in lands], _sds((8, LANE), F32)),
        in_specs=[HBM_SPEC] * (2 * n) + [ANY],
        out_specs=(SEM_SPEC, SEM_SPEC, *[HBM_SPEC] * (2 * n), VMEM_SPEC),
        input_output_aliases={a: 2 + a for a in range(2 * n)},
        compiler_params=pltpu.CompilerParams(has_side_effects=EFFECT),
    )(*[_hbm(s) for s in shards], *[_hbm(z) for z in lands], after)
    return out[0], out[1], out[2:2 + n], out[2 + n:2 + 2 * n], out[-1]


def gather_wait(started, after, name):
    send_sems, recv_sems, shards, lands, _ = started
    n = len(shards)

    def body(*refs):
        ins, zones = refs[:n], refs[n:2 * n]
        send_ref, recv_ref = refs[2 * n], refs[2 * n + 1]
        mx, my, mc = _me()
        for a in range(n):
            for k in range(FIRST):
                cp = pltpu.make_async_remote_copy(
                    src_ref=ins[a], dst_ref=zones[a].at[0], send_sem=send_ref.at[a * FIRST + k],
                    recv_sem=recv_ref.at[a * FIRST + k], device_id=(mx, my, 1 - mc), device_id_type=MESH)
                cp.wait_send()
                cp.wait_recv()

    out = pl.pallas_call(
        body, name=name,
        out_shape=(*[pltpu.HBM(s.shape, s.dtype) for s in shards], *[pltpu.HBM(z.shape, z.dtype) for z in lands]),
        in_specs=[HBM_SPEC] * (2 * n) + [SEM_SPEC, SEM_SPEC, ANY], out_specs=tuple([HBM_SPEC] * (2 * n)),
        input_output_aliases={a: a for a in range(2 * n)},
        compiler_params=pltpu.CompilerParams(has_side_effects=EFFECT),
    )(*shards, *lands, send_sems, recv_sems, after)
    return out[:n], out[n:]


def gather_forward(lands, name):
    n = len(lands)
    rel = N_CHIP - 1

    def body(*refs):
        zones, outs = refs[:n], refs[n:2 * n]
        send_sems, recv_sems = refs[2 * n:]
        mx, my, mc = _me()
        chips = _other_chips(mx, my)

        def copy(a, j, core):
            blk = _lin((*chips[j], core))
            return pltpu.make_async_remote_copy(
                src_ref=zones[a].at[blk], dst_ref=outs[a].at[blk], send_sem=send_sems.at[a * rel + j],
                recv_sem=recv_sems.at[a * rel + j], device_id=(mx, my, 1 - mc), device_id_type=MESH)

        sends = [copy(a, j, mc) for a in range(n) for j in range(rel)]
        for cp in sends:
            cp.start()
        for a in range(n):
            for j in range(rel):
                copy(a, j, 1 - mc).wait_recv()
        for cp in sends:
            cp.wait_send()

    return pl.pallas_call(
        body, name=name, out_shape=[_sds(z.shape, z.dtype) for z in lands],
        in_specs=[ANY] * n, out_specs=[ANY] * n, input_output_aliases={a: a for a in range(n)},
        scratch_shapes=[pltpu.SemaphoreType.DMA((n * rel,)), pltpu.SemaphoreType.DMA((n * rel,))],
    )(*lands)


def pair_exchange(gs, name):
    n = len(gs)

    def body(*refs):
        ins, outs = refs[:n], refs[n:2 * n]
        send_sems, recv_sems = refs[2 * n:]
        mx, my, mc = _me()
        cps = []
        for a in range(n):
            for q in range(N_CHIP):
                cp = pltpu.make_async_remote_copy(
                    src_ref=ins[a].at[2 * q + (1 - mc)], dst_ref=outs[a].at[q],
                    send_sem=send_sems.at[a * N_CHIP + q], recv_sem=recv_sems.at[a * N_CHIP + q],
                    device_id=(mx, my, 1 - mc), device_id_type=MESH)
                cp.start()
                cps.append(cp)
        for cp in cps:
            cp.wait()

    return pl.pallas_call(
        body, name=name, out_shape=[_sds((N_CHIP,) + g.shape[1:], g.dtype) for g in gs],
        in_specs=[ANY] * n, out_specs=[ANY] * n,
        scratch_shapes=[pltpu.SemaphoreType.DMA((n * N_CHIP,)), pltpu.SemaphoreType.DMA((n * N_CHIP,))],
    )(*gs)


def _tile2(rows, cols, row_mult):
    tr = _row_tile(rows, 512, row_mult)
    if tr < rows or rows * cols <= 2 ** 20:
        return tr, cols
    return rows, _row_tile(cols, 512, LANE)


def pair_add(core, g, got, name):
    _, rows, cols = g.shape
    tr, tc = _tile2(rows, cols, 16)

    def body(core_ref, g_ref, got_ref, h_ref):
        h_ref[...] = (g_ref[...].astype(F32) + got_ref[...].astype(F32)).astype(h_ref.dtype)

    blk = pl.BlockSpec((None, tr, tc), lambda q, i, j, c: (q, i, j))
    return pl.pallas_call(
        body, name=name, out_shape=_sds((N_CHIP, rows, cols), g.dtype),
        grid_spec=pltpu.PrefetchScalarGridSpec(
            num_scalar_prefetch=1, grid=(N_CHIP, rows // tr, cols // tc),
            in_specs=[pl.BlockSpec((None, tr, tc), lambda q, i, j, c: (2 * q + c[0], i, j)), blk],
            out_specs=blk),
        compiler_params=_params("parallel", "parallel", "parallel"),
    )(core, g, got)


def chip_exchange_start(hs, name):
    n = len(hs)
    rel = N_CHIP - 1
    lands = [lax.empty((rel,) + h.shape[1:], h.dtype) for h in hs]

    def body(*refs):
        ins, zones = refs[:n], refs[n:2 * n]
        send_sems, recv_sems = refs[2 * n], refs[2 * n + 1]
        token = refs[-1]
        mx, my, mc = _me()
        for a in range(n):
            for k, (px, py) in enumerate(_other_chips(mx, my)):
                pltpu.make_async_remote_copy(
                    src_ref=ins[a].at[2 * px + py], dst_ref=zones[a].at[k], send_sem=send_sems.at[a * rel + k],
                    recv_sem=recv_sems.at[a * rel + k], device_id=(px, py, mc), device_id_type=MESH).start()
        token[...] = jnp.zeros_like(token)

    sems = pltpu.SemaphoreType.DMA((n * rel,))
    out = pl.pallas_call(
        body, name=name,
        out_shape=(sems, sems, *[pltpu.HBM(h.shape, h.dtype) for h in hs],
                   *[pltpu.HBM(z.shape, z.dtype) for z in lands], _sds((8, LANE), F32)),
        in_specs=[HBM_SPEC] * (2 * n), out_specs=(SEM_SPEC, SEM_SPEC, *[HBM_SPEC] * (2 * n), VMEM_SPEC),
        input_output_aliases={a: 2 + a for a in range(2 * n)},
        compiler_params=pltpu.CompilerParams(has_side_effects=EFFECT),
    )(*[_hbm(h) for h in hs], *[_hbm(z) for z in lands])
    return out[0], out[1], out[2:2 + n], out[2 + n:2 + 2 * n], out[-1]


def chip_exchange_wait(started, after, name):
    send_sems, recv_sems, hs, lands, _ = started
    n = len(hs)
    rel = N_CHIP - 1

    def body(*refs):
        ins, zones = refs[:n], refs[n:2 * n]
        send_ref, recv_ref = refs[2 * n], refs[2 * n + 1]
        mx, my, mc = _me()
        for a in range(n):
            for k, (px, py) in enumerate(_other_chips(mx, my)):
                cp = pltpu.make_async_remote_copy(
                    src_ref=ins[a].at[0], dst_ref=zones[a].at[k], send_sem=send_ref.at[a * rel + k],
                    recv_sem=recv_ref.at[a * rel + k], device_id=(px, py, mc), device_id_type=MESH)
                cp.wait_send()
                cp.wait_recv()

    out = pl.pallas_call(
        body, name=name,
        out_shape=(*[pltpu.HBM(h.shape, h.dtype) for h in hs], *[pltpu.HBM(z.shape, z.dtype) for z in lands]),
        in_specs=[HBM_SPEC] * (2 * n) + [SEM_SPEC, SEM_SPEC, ANY], out_specs=tuple([HBM_SPEC] * (2 * n)),
        input_output_aliases={a: a for a in range(2 * n)},
        compiler_params=pltpu.CompilerParams(has_side_effects=EFFECT),
    )(*hs, *lands, send_sems, recv_sems, after)
    return out[:n], out[n:]


def _adam(w, g, m, v):
    m = ADAM_B1 * m + (1.0 - ADAM_B1) * g
    v = ADAM_B2 * v + (1.0 - ADAM_B2) * (g * g)
    m_hat = m / (1.0 - ADAM_B1 ** ADAM_STEP)
    v_hat = v / (1.0 - ADAM_B2 ** ADAM_STEP)
    delta = -ADAM_LR * (m_hat / (jnp.sqrt(v_hat) + ADAM_EPS) + ADAM_WD * w)
    return delta, m, v


def adamw_owned(chip, h, got, w, m, v, name):
    rows, cols = w.shape
    tr = _row_tile(rows, 256, 16)

    def body(chip_ref, h_ref, got_ref, w_ref, m_ref, v_ref, g_out, d_out, m_out, v_out):
        g = h_ref[...].astype(F32)
        for k in range(N_CHIP - 1):
            g = g + got_ref[k].astype(F32)
        d, mn, vn = _adam(w_ref[...], g, m_ref[...], v_ref[...])
        g_out[...] = g
        d_out[...] = d
        m_out[...] = mn
        v_out[...] = vn

    blk = pl.BlockSpec((tr, cols), lambda i, c: (i, 0))
    return pl.pallas_call(
        body, name=name, out_shape=[_sds((rows, cols), F32)] * 4,
        grid_spec=pltpu.PrefetchScalarGridSpec(
            num_scalar_prefetch=1, grid=(rows // tr,),
            in_specs=[pl.BlockSpec((None, tr, cols), lambda i, c: (c[0], i, 0)),
                      pl.BlockSpec((N_CHIP - 1, tr, cols), lambda i, c: (0, i, 0)), blk, blk, blk],
            out_specs=[blk] * 4),
        compiler_params=_params("parallel"),
    )(chip, h, got, w, m, v)


def owned_sum(chip, h, got, name):
    _, rows, cols = h.shape
    tr, tc = _tile2(rows, cols, 16)

    def body(chip_ref, h_ref, got_ref, g_out):
        g = h_ref[...].astype(F32)
        for k in range(N_CHIP - 1):
            g = g + got_ref[k].astype(F32)
        g_out[...] = g

    return pl.pallas_call(
        body, name=name, out_shape=_sds((rows, cols), F32),
        grid_spec=pltpu.PrefetchScalarGridSpec(
            num_scalar_prefetch=1, grid=(rows // tr, cols // tc),
            in_specs=[pl.BlockSpec((None, tr, tc), lambda i, j, c: (c[0], i, j)),
                      pl.BlockSpec((N_CHIP - 1, tr, tc), lambda i, j, c: (0, i, j))],
            out_specs=pl.BlockSpec((tr, tc), lambda i, j, c: (i, j))),
        compiler_params=_params("parallel", "parallel"),
    )(chip, h, got)


def adamw_sum(parts, w, m, v, name):
    n_parts, rows, cols = parts.shape
    tr = _row_tile(rows, 256, 8)

    def body(p_ref, w_ref, m_ref, v_ref, g_out, d_out, m_out, v_out):
        g = p_ref[0]
        for k in range(1, n_parts):
            g = g + p_ref[k]
        d, mn, vn = _adam(w_ref[...], g, m_ref[...], v_ref[...])
        g_out[...] = g
        d_out[...] = d
        m_out[...] = mn
        v_out[...] = vn

    blk = pl.BlockSpec((tr, cols), lambda i: (i, 0))
    return pl.pallas_call(
        body, name=name, out_shape=[_sds((rows, cols), F32)] * 4, grid=(rows // tr,),
        in_specs=[pl.BlockSpec((n_parts, tr, cols), lambda i: (0, i, 0)), blk, blk, blk],
        out_specs=[blk] * 4, compiler_params=_params("parallel"),
    )(parts, w, m, v)


def adaln_cols(c_all, w, b, name):
    d, n = w.shape
    tn = _row_tile(n, 768, LANE)

    def body(c_ref, w_ref, b_ref, o_ref):
        c = c_ref[...]
        o_ref[...] = jnp.dot(c * _sigmoid(c), w_ref[...], preferred_element_type=F32, precision=HI) + b_ref[...]

    return pl.pallas_call(
        body, name=name, out_shape=_sds((N_DEV, n), F32), grid=(n // tn,),
        in_specs=[pl.BlockSpec((N_DEV, d), lambda j: (0, 0)), pl.BlockSpec((d, tn), lambda j: (0, j)),
                  pl.BlockSpec((1, tn), lambda j: (0, j))],
        out_specs=pl.BlockSpec((N_DEV, tn), lambda j: (0, j)), compiler_params=_params("parallel"),
    )(c_all, w, b)


def adaln_wgrad(c_all, dmod_cols, name):
    d = c_all.shape[1]
    n = dmod_cols.shape[1]
    tn = _row_tile(n, 768, LANE)

    def body(c_ref, g_ref, o_ref):
        c = c_ref[...]
        o_ref[...] = lax.dot_general(c * _sigmoid(c), g_ref[...], (((0,), (0,)), ((), ())),
                                     preferred_element_type=F32, precision=HI)

    return pl.pallas_call(
        body, name=name, out_shape=_sds((d, n), F32), grid=(n // tn,),
        in_specs=[pl.BlockSpec((N_DEV, d), lambda j: (0, 0)), pl.BlockSpec((N_DEV, tn), lambda j: (0, j))],
        out_specs=pl.BlockSpec((d, tn), lambda j: (0, j)), compiler_params=_params("parallel"),
    )(c_all, dmod_cols)


def _modulate(h, sh, sc):
    return (h * (1.0 + sc) + sh).astype(BF16)


def ffn_in(h, sh, sc, w3, name):
    t, d = h.shape
    nb, _, bw = w3.shape
    half = nb // 2
    tm = _row_tile(t, 256, 16)

    def body(h_ref, sh_ref, sc_ref, wa_ref, wb_ref, a_ref, b_ref, s_ref):
        u = _modulate(h_ref[...], sh_ref[...], sc_ref[...])
        a = _dot(u, wa_ref[...])
        b = _dot(u, wb_ref[...])
        a_ref[...] = a.astype(BF16)
        b_ref[...] = b.astype(BF16)
        s_ref[...] = (a * _sigmoid(a) * b).astype(BF16)

    vec = pl.BlockSpec((1, d), lambda j, i: (0, 0))
    out = pl.BlockSpec((tm, bw), lambda j, i: (i, j))
    return pl.pallas_call(
        body, name=name, out_shape=[_sds((t, half * bw), BF16)] * 3, grid=(half, t // tm),
        in_specs=[pl.BlockSpec((tm, d), lambda j, i: (i, 0)), vec, vec,
                  pl.BlockSpec((None, d, bw), lambda j, i: (j, 0, 0)),
                  pl.BlockSpec((None, d, bw), lambda j, i: (j + half, 0, 0))],
        out_specs=[out] * 3, compiler_params=_params("parallel", "parallel"),
    )(h, sh, sc, w3, w3)


def mod_matmul(h, sh, sc, wt, bw, name):
    t, d = h.shape
    n = wt.shape[0]
    tm = _row_tile(t, 256, 16)

    def body(h_ref, sh_ref, sc_ref, w_ref, o_ref):
        o_ref[...] = _dot_nt(_modulate(h_ref[...], sh_ref[...], sc_ref[...]), w_ref[...]).astype(BF16)

    vec = pl.BlockSpec((1, d), lambda j, i: (0, 0))
    return pl.pallas_call(
        body, name=name, out_shape=_sds((t, n), BF16), grid=(n // bw, t // tm),
        in_specs=[pl.BlockSpec((tm, d), lambda j, i: (i, 0)), vec, vec, pl.BlockSpec((bw, d), lambda j, i: (j, 0))],
        out_specs=pl.BlockSpec((tm, bw), lambda j, i: (i, j)), compiler_params=_params("parallel", "parallel"),
    )(h, sh, sc, wt)


def out_ln(s, w, hin, gmod, ln_g, ln_b, coef, name):
    t, kdim = s.shape
    d = w.shape[1]
    tm = _row_tile(t, 512, 16)
    tk = _row_tile(kdim, K_TILE_CAP, LANE)
    nk = kdim // tk

    def body(s_ref, w_ref, hin_ref, gm_ref, g_ref, b_ref, f_ref, z_ref, h_ref, acc):
        k = pl.program_id(1)

        @pl.when(k == 0)
        def _():
            acc[...] = jnp.zeros_like(acc)

        acc[...] += _dot(s_ref[...], w_ref[...])

        @pl.when(k == nk - 1)
        def _():
            f = acc[...]
            z = ALPHA * hin_ref[...] + (coef * gm_ref[...]) * f
            mu = jnp.mean(z, axis=-1, keepdims=True)
            zc = z - mu
            var = jnp.mean(zc * zc, axis=-1, keepdims=True)
            f_ref[...] = f.astype(BF16)
            z_ref[...] = z
            h_ref[...] = zc * lax.rsqrt(var + LN_EPS) * g_ref[...] + b_ref[...]

    vec = pl.BlockSpec((1, d), lambda i, k: (0, 0))
    row = pl.BlockSpec((tm, d), lambda i, k: (i, 0))
    return pl.pallas_call(
        body, name=name, out_shape=[_sds((t, d), BF16), _sds((t, d), F32), _sds((t, d), F32)],
        grid=(t // tm, nk),
        in_specs=[pl.BlockSpec((tm, tk), lambda i, k: (i, k)), pl.BlockSpec((tk, d), lambda i, k: (k, 0)),
                  pl.BlockSpec((tm, d), lambda i, k: (i, 0), pipeline_mode=pl.Buffered(1)), vec, vec, vec],
        out_specs=[row, row, row], scratch_shapes=[pltpu.VMEM((tm, d), F32)],
        compiler_params=_params("parallel", "arbitrary"),
    )(s, w, hin, gmod, ln_g, ln_b)


def ln_bwd(dh, z, f, ln_g, gmod, coef, name, target=None):
    t, d = z.shape
    tm = _row_tile(t, 256, 16)
    head = target is not None

    def body(*refs):
        if head:
            dh_ref, tg_ref, z_ref, f_ref, g_ref, gm_ref, dz_ref, df_ref, dg_ref, db_ref, dgm_ref, loss_ref = refs
        else:
            dh_ref, z_ref, f_ref, g_ref, gm_ref, dz_ref, df_ref, dg_ref, db_ref, dgm_ref = refs
        i = pl.program_id(0)

        @pl.when(i == 0)
        def _():
            dg_ref[...] = jnp.zeros_like(dg_ref)
            db_ref[...] = jnp.zeros_like(db_ref)
            dgm_ref[...] = jnp.zeros_like(dgm_ref)
            if head:
                loss_ref[...] = jnp.zeros_like(loss_ref)

        dh = dh_ref[...]
        if head:
            err = dh - tg_ref[...]
            loss_ref[...] += 0.5 * jnp.sum(jnp.mean(err * err, axis=-1, keepdims=True))
            dh = err / d
        zv = z_ref[...]
        mu = jnp.mean(zv, axis=-1, keepdims=True)
        zc = zv - mu
        rstd = lax.rsqrt(jnp.mean(zc * zc, axis=-1, keepdims=True) + LN_EPS)
        xhat = zc * rstd
        dxh = dh * g_ref[...]
        dz = rstd * (dxh - jnp.mean(dxh, axis=-1, keepdims=True)
                     - xhat * jnp.mean(dxh * xhat, axis=-1, keepdims=True))
        dz_ref[...] = dz
        df_ref[...] = ((coef * gm_ref[...]) * dz).astype(BF16)
        dg_ref[...] += _colsum(dh * xhat)
        db_ref[...] += _colsum(dh)
        dgm_ref[...] += _colsum(coef * f_ref[...].astype(F32) * dz)

    vec = pl.BlockSpec((1, d), lambda i: (0, 0))
    row = pl.BlockSpec((tm, d), lambda i: (i, 0))
    ins = [dh] + ([target] if head else []) + [z, f, ln_g, gmod]
    in_specs = [row] + ([row] if head else []) + [row, row, vec, vec]
    out_shape = [_sds((t, d), F32), _sds((t, d), BF16)] + [_sds((1, d), F32)] * 3
    out_specs = [row, row, vec, vec, vec]
    if head:
        out_shape.append(_sds((1, LANE), F32))
        out_specs.append(pl.BlockSpec((1, LANE), lambda i: (0, 0)))
    return pl.pallas_call(
        body, name=name, out_shape=out_shape, grid=(t // tm,), in_specs=in_specs, out_specs=out_specs,
        compiler_params=_params("arbitrary"),
    )(*ins)


def ffn_bwd_act(df, w, a, b, name):
    t, d = df.shape
    fdim = w.shape[0]
    bw = fdim // (N_DEV // 2)
    tm = _row_tile(t, 512, 16)

    def body(df_ref, w_ref, a_ref, b_ref, o_ref):
        ds = _dot_nt(df_ref[...], w_ref[...])
        av = a_ref[...].astype(F32)
        sg = _sigmoid(av)
        o_ref[0] = (ds * b_ref[...].astype(F32) * (sg * (1.0 + av * (1.0 - sg)))).astype(BF16)
        o_ref[1] = (ds * (av * sg)).astype(BF16)

    act = pl.BlockSpec((tm, bw), lambda j, i: (i, j))
    return pl.pallas_call(
        body, name=name, out_shape=_sds((2, t, fdim), BF16), grid=(fdim // bw, t // tm),
        in_specs=[pl.BlockSpec((tm, d), lambda j, i: (i, 0)), pl.BlockSpec((bw, d), lambda j, i: (j, 0)), act, act],
        out_specs=pl.BlockSpec((2, tm, bw), lambda j, i: (0, i, j)),
        compiler_params=_params("parallel", "parallel"),
    )(df, w, a, b)


def matmul_tn(name, a, a_block, a_map, b, b_block, b_map, out_shape, o_block, o_map, n_out, mod=None,
              mod_b=False):
    tk = [s for s in a_block if s is not None][0]
    nk = a.shape[-2] // tk
    m, nn = [s for s in o_block if s is not None]

    def body(*refs):
        if mod is None:
            a_ref, b_ref, o_ref, acc = refs
        else:
            a_ref, sh_ref, sc_ref, b_ref, o_ref, acc = refs
        k = pl.program_id(1)

        @pl.when(k == 0)
        def _():
            acc[...] = jnp.zeros_like(acc)

        av, bv = a_ref[...], b_ref[...]
        if mod is not None and mod_b:
            bv = _modulate(bv, sh_ref[...], sc_ref[...])
        elif mod is not None:
            av = _modulate(av, sh_ref[...], sc_ref[...])
        acc[...] += _dot_tn(av, bv)

        @pl.when(k == nk - 1)
        def _():
            o_ref[...] = acc[...].astype(o_ref.dtype)

    ins = [a] + (list(mod) if mod is not None else []) + [b]
    in_specs = [pl.BlockSpec(a_block, a_map)]
    if mod is not None:
        vec = pl.BlockSpec((1, mod[0].shape[1]), lambda n, k: (0, 0))
        in_specs += [vec, vec]
    in_specs.append(pl.BlockSpec(b_block, b_map))
    return pl.pallas_call(
        body, name=name, out_shape=out_shape, grid=(n_out, nk), in_specs=in_specs,
        out_specs=pl.BlockSpec(o_block, o_map), scratch_shapes=[pltpu.VMEM((m, nn), F32)],
        compiler_params=_params("parallel", "arbitrary"),
    )(*ins)


def matmul_nt_blocks(name, dy, dy_block, dy_map, w3, t, resid=None):
    tm, bw = [s for s in dy_block if s is not None]
    rows = w3.ndim == 2
    if rows:
        nk, n = w3.shape[0] // bw, w3.shape[1]
    else:
        nk, n, _ = w3.shape

    def body(*refs):
        if resid is None:
            dy_ref, w_ref, o_ref, acc = refs
        else:
            dy_ref, w_ref, dz_ref, hin_ref, sc_ref, o_ref, dsc_ref, dsh_ref, acc = refs
        i, k = pl.program_id(0), pl.program_id(1)

        @pl.when(k == 0)
        def _():
            acc[...] = jnp.zeros_like(acc)

        if resid is not None:
            @pl.when((k == 0) & (i == 0))
            def _():
                dsc_ref[...] = jnp.zeros_like(dsc_ref)
                dsh_ref[...] = jnp.zeros_like(dsh_ref)

        acc[...] += _dot(dy_ref[...], w_ref[...]) if rows else _dot_nt(dy_ref[...], w_ref[...])

        @pl.when(k == nk - 1)
        def _():
            du = acc[...]
            if resid is None:
                o_ref[...] = du.astype(o_ref.dtype)
            else:
                o_ref[...] = ALPHA * dz_ref[...] + du * (1.0 + sc_ref[...])
                dsc_ref[...] += _colsum(du * hin_ref[...])
                dsh_ref[...] += _colsum(du)

    row = pl.BlockSpec((tm, n), lambda i, k: (i, 0))
    vec = pl.BlockSpec((1, n), lambda i, k: (0, 0))
    w_spec = pl.BlockSpec((bw, n), lambda i, k: (k, 0)) if rows else pl.BlockSpec((None, n, bw), lambda i, k: (k, 0, 0))
    in_specs = [pl.BlockSpec(dy_block, dy_map), w_spec]
    ins = [dy, w3]
    if resid is None:
        out_shape, out_specs = _sds((t, n), BF16), row
    else:
        ins += list(resid)
        once = pl.BlockSpec((tm, n), lambda i, k: (i, 0), pipeline_mode=pl.Buffered(1))
        in_specs += [once, once, vec]
        out_shape = [_sds((t, n), F32), _sds((1, n), F32), _sds((1, n), F32)]
        out_specs = [row, vec, vec]
    return pl.pallas_call(
        body, name=name, out_shape=out_shape, grid=(t // tm, nk), in_specs=in_specs, out_specs=out_specs,
        scratch_shapes=[pltpu.VMEM((tm, n), F32)], compiler_params=_params("arbitrary", "arbitrary"),
    )(*ins)


REL_W = KW + QB


def bias_table(rel_bias):
    nh, n_rel = rel_bias.shape
    lo = KW - QB - REL_CLIP
    hi = KW - lo - n_rel
    assert n_rel == REL_CLIP + CHUNK and lo >= 0 and hi >= 0
    first, last = rel_bias[:, :1], rel_bias[:, -1:]
    row = jnp.concatenate([jnp.broadcast_to(first, (nh, lo)), rel_bias, jnp.broadcast_to(last, (nh, hi)),
                           jnp.broadcast_to(first, (nh, QB))], axis=1)
    table = jnp.tile(row, (1, QB))[:, :QB * (REL_W - 1)].reshape(nh, QB, REL_W - 1)[:, :, :KW]
    q = np.arange(QB)[:, None] // CHUNK
    k = np.arange(KW)[None, :] // CHUNK
    band = (k >= q) & (k <= q + A_PAST_CHUNKS)
    return jnp.where(band[None], table, NEG)


def bias_grad_skew(dbias):
    nh = dbias.shape[0]
    flat = jnp.pad(dbias, ((0, 0), (0, 0), (0, REL_W - 1 - KW))).reshape(nh, QB * (REL_W - 1))
    return jnp.pad(flat, ((0, 0), (0, QB))).reshape(nh, QB, REL_W)


def bias_clip_map(n_rel):
    m = np.arange(REL_W)
    dist = np.where(m < KW, m, m - REL_W) - (KW - QB)
    idx = np.clip(dist, -REL_CLIP, CHUNK - 1) + REL_CLIP
    return (idx[:, None] == np.arange(n_rel)[None, :]).astype(np.float32)


PAIR = 2


def _pair_specs(col, rows_of):
    return [pl.BlockSpec((QB, LANE), functools.partial(lambda r, h, i: (rows_of(r, i), col // LANE + h), r))
            for r in range(3)]


def _earlier(r, i):
    return jnp.maximum(i - 2 + r, 0)


def _head_lanes(hh, dh):
    lane = lax.broadcasted_iota(jnp.int32, (1, LANE), 1)
    return (lane < dh) if hh == 0 else (lane >= dh)


def _only(x, lanes):
    return jnp.where(lanes, x, jnp.zeros_like(x))


def _scores(q, ks, bias, i, scale):
    s = jnp.concatenate([_dot_nt(q, kk) for kk in ks], axis=1) * scale + bias
    col = lax.broadcasted_iota(jnp.int32, s.shape, 1)
    return jnp.where(col >= (2 - i) * QB, s, NEG)


def attn_fwd(p, cols, bias, dh, name):
    t = p.shape[0]
    nh = bias.shape[0]
    scale = dh ** -0.5

    def body(q_ref, k0, k1, k2, v0, v1, v2, b_ref, o_ref, lse_ref):
        i = pl.program_id(1)
        q = q_ref[...]
        outs = []
        for hh in range(PAIR):
            lanes = _head_lanes(hh, dh)
            s = _scores(q, [_only(kk[...], lanes) for kk in (k0, k1, k2)], b_ref[hh], i, scale)
            m = jnp.max(s, axis=-1, keepdims=True)
            e = jnp.exp(s - m)
            l = jnp.sum(e, axis=-1, keepdims=True)
            eb = e.astype(BF16)
            o = sum(_dot(eb[:, r * QB:(r + 1) * QB], vv[...]) for r, vv in enumerate((v0, v1, v2)))
            outs.append(o / l)
            lse_ref[hh] = m + jnp.log(l)
        o_ref[...] = jnp.where(_head_lanes(0, dh), outs[0], outs[1]).astype(BF16)

    st = pl.BlockSpec((PAIR, QB, 1), lambda h, i: (h, i, 0))
    return pl.pallas_call(
        body, name=name, out_shape=[_sds((t, nh * dh), BF16), _sds((nh, t, 1), F32)], grid=(nh // PAIR, t // QB),
        in_specs=[pl.BlockSpec((QB, LANE), lambda h, i: (i, cols["qa"] // LANE + h))]
        + _pair_specs(cols["ka"], _earlier) + _pair_specs(cols["va"], _earlier)
        + [pl.BlockSpec((PAIR, QB, KW), lambda h, i: (h, 0, 0))],
        out_specs=[pl.BlockSpec((QB, LANE), lambda h, i: (i, h)), st],
        compiler_params=_params("parallel", "parallel"),
    )(p, p, p, p, p, p, p, bias)


def attn_bwd_q(p, cols, bias, lse, dy, dh, name):
    t = p.shape[0]
    nh = bias.shape[0]
    scale = dh ** -0.5

    def body(q_ref, k0, k1, k2, v0, v1, v2, b_ref, lse_ref, dy_ref, dq_ref, dl_ref, db_ref):
        i = pl.program_id(1)

        @pl.when(i == 0)
        def _():
            db_ref[...] = jnp.zeros_like(db_ref)

        q, dyv = q_ref[...], dy_ref[...]
        ks = [k0[...], k1[...], k2[...]]
        dqs = []
        for hh in range(PAIR):
            lanes = _head_lanes(hh, dh)
            s = _scores(q, [_only(kk, lanes) for kk in ks], b_ref[hh], i, scale)
            prob = jnp.exp(s - lse_ref[hh])
            dprob = jnp.concatenate([_dot_nt(dyv, _only(vv[...], lanes)) for vv in (v0, v1, v2)], axis=1)
            delta = jnp.sum(prob * dprob, axis=-1, keepdims=True)
            ds = prob * (dprob - delta)
            dsb = ds.astype(BF16)
            dqs.append(sum(_dot(dsb[:, r * QB:(r + 1) * QB], kk) for r, kk in enumerate(ks)))
            dl_ref[hh] = delta
            db_ref[hh] += ds
        dq_ref[...] = (jnp.where(_head_lanes(0, dh), dqs[0], dqs[1]) * scale).astype(BF16)

    st = pl.BlockSpec((PAIR, QB, 1), lambda h, i: (h, i, 0))
    tab = pl.BlockSpec((PAIR, QB, KW), lambda h, i: (h, 0, 0))
    own = pl.BlockSpec((QB, LANE), lambda h, i: (i, h))
    return pl.pallas_call(
        body, name=name,
        out_shape=[_sds((t, nh * dh), BF16), _sds((nh, t, 1), F32), _sds((nh, QB, KW), F32)],
        grid=(nh // PAIR, t // QB),
        in_specs=[pl.BlockSpec((QB, LANE), lambda h, i: (i, cols["qa"] // LANE + h))]
        + _pair_specs(cols["ka"], _earlier) + _pair_specs(cols["va"], _earlier) + [tab, st, own],
        out_specs=[own, st, tab], compiler_params=_params("parallel", "arbitrary"),
    )(p, p, p, p, p, p, p, bias, lse, dy)


def attn_bwd_kv(p, cols, bias, lse, delta, dy, dh, name):
    t = p.shape[0]
    nh = bias.shape[0]
    nb = t // QB
    scale = dh ** -0.5

    def later(r, j):
        return jnp.minimum(j + r, nb - 1)

    def body(k_ref, v_ref, b_ref, *refs):
        qr, lr, dr, gr = refs[0:3], refs[3:6], refs[6:9], refs[9:12]
        dk_ref, dv_ref = refs[12:]
        j = pl.program_id(1)
        kk, vv = k_ref[...], v_ref[...]
        dks, dvs = [], []
        for hh in range(PAIR):
            lanes = _head_lanes(hh, dh)
            kh, vh = _only(kk, lanes), _only(vv, lanes)
            dk = jnp.zeros((QB, LANE), F32)
            dv = jnp.zeros((QB, LANE), F32)
            for r in range(3):
                seg = 2 - r
                qv, dyv = qr[r][...], gr[r][...]
                s = _dot_nt(qv, kh) * scale + b_ref[hh, :, seg * QB:(seg + 1) * QB]
                prob = jnp.where(j + r < nb, jnp.exp(s - lr[r][hh]), 0.0)
                ds = prob * (_dot_nt(dyv, vh) - dr[r][hh])
                dv = dv + _dot_tn(prob.astype(BF16), dyv)
                dk = dk + _dot_tn(ds.astype(BF16), qv)
            dks.append(dk)
            dvs.append(dv)
        first = _head_lanes(0, dh)
        dk_ref[...] = (jnp.where(first, dks[0], dks[1]) * scale).astype(BF16)
        dv_ref[...] = jnp.where(first, dvs[0], dvs[1]).astype(BF16)

    def stats():
        return [pl.BlockSpec((PAIR, QB, 1), functools.partial(lambda r, h, j: (h, later(r, j), 0), r))
                for r in range(3)]

    def at(col):
        return pl.BlockSpec((QB, LANE), lambda h, j: (j, col // LANE + h))

    own = pl.BlockSpec((QB, LANE), lambda h, j: (j, h))
    return pl.pallas_call(
        body, name=name, out_shape=[_sds((t, nh * dh), BF16)] * 2, grid=(nh // PAIR, nb),
        in_specs=[at(cols["ka"]), at(cols["va"]), pl.BlockSpec((PAIR, QB, KW), lambda h, j: (h, 0, 0))]
        + _pair_specs(cols["qa"], later) + stats() + stats() + _pair_specs(0, later),
        out_specs=[own, own], compiler_params=_params("parallel", "parallel"),
    )(p, p, bias, p, p, p, lse, lse, lse, delta, delta, delta, dy, dy, dy)


def _tri(strict):
    r = lax.broadcasted_iota(jnp.int32, (CHUNK, CHUNK), 0)
    c = lax.broadcasted_iota(jnp.int32, (CHUNK, CHUNK), 1)
    return jnp.where((c < r) if strict else (c <= r), 1.0, 0.0).astype(F32)


def _gate(lr, wa, ba):
    y = _dot(lr, wa) + ba
    return (jnp.minimum(y, 0.0) - jnp.log(1.0 + jnp.exp(-jnp.abs(y)))) / GATE_TAU, y


def _decays(la):
    cum = jnp.dot(_tri(False), la, preferred_element_type=F32, precision=HI)
    last = cum[CHUNK - 1:CHUNK, :]
    return jnp.exp(last - cum), jnp.exp(last)


def _gla_specs(cols, hk, hv, order):
    def at(start, width):
        return pl.BlockSpec((GB, width), lambda h, i: (order(i), start // width + h))
    return [at(cols["qb"], hk), at(cols["kb"], hk), at(cols["vb"], hv), at(cols["rb"], hv),
            pl.BlockSpec((GB, LANE), lambda h, i: (order(i), cols["lr"] // LANE))]


def gla_fwd(p, cols, wa, ba, gn, nh, hk, hv, name):
    t = p.shape[0]
    nc = t // CHUNK
    scale = hk ** -0.5
    per = GB // CHUNK

    def body(q_ref, k_ref, v_ref, r_ref, lr_ref, wa_ref, ba_ref, gn_ref, o_ref, y_ref, st_ref, state):
        @pl.when(pl.program_id(1) == 0)
        def _():
            state[...] = jnp.zeros_like(state)

        for c in range(per):
            rows = pl.ds(c * CHUNK, CHUNK)
            la, _ = _gate(lr_ref[rows, :], wa_ref[...], ba_ref[...])
            w, decay = _decays(la)
            kdec = (k_ref[rows, :].astype(F32) * w).astype(BF16)
            st = decay * state[...] + _dot_tn(v_ref[rows, :], kdec)
            state[...] = st
            st_ref[c] = st
            o = _dot_nt(q_ref[rows, :], st.astype(BF16)) * scale
            o_ref[rows, :] = o
            rinv = lax.rsqrt(jnp.mean(o * o, axis=-1, keepdims=True) + RMS_EPS)
            rv = r_ref[rows, :].astype(F32)
            y_ref[rows, :] = (o * rinv * gn_ref[...] * (rv * _sigmoid(rv))).astype(BF16)

    return pl.pallas_call(
        body, name=name,
        out_shape=[_sds((t, nh * hv), F32), _sds((t, nh * hv), BF16), _sds((nh, nc, hv, hk), F32)],
        grid=(nh, t // GB),
        in_specs=_gla_specs(cols, hk, hv, lambda i: i)
        + [pl.BlockSpec((LANE, hk), lambda h, i: (0, h)), pl.BlockSpec((1, hk), lambda h, i: (0, h)),
           pl.BlockSpec((1, hv), lambda h, i: (0, 0))],
        out_specs=[pl.BlockSpec((GB, hv), lambda h, i: (i, h)), pl.BlockSpec((GB, hv), lambda h, i: (i, h)),
                   pl.BlockSpec((None, per, hv, hk), lambda h, i: (h, i, 0, 0))],
        scratch_shapes=[pltpu.VMEM((hv, hk), F32)], compiler_params=_params("parallel", "arbitrary"),
    )(p, p, p, p, p, wa, ba, gn)


def gla_bwd(p, cols, wa, ba, gn, o, states, dy, nh, hk, hv, name):
    t = p.shape[0]
    nblk = t // GB
    scale = hk ** -0.5
    per = GB // CHUNK

    def rev(i):
        return nblk - 1 - i

    def body(q_ref, k_ref, v_ref, r_ref, lr_ref, wa_ref, ba_ref, gn_ref, o_ref, st_ref, sp_ref, dy_ref,
             dq_ref, dk_ref, dv_ref, dr_ref, dg_ref, dgn_ref, carry):
        h, i = pl.program_id(0), pl.program_id(1)

        @pl.when(i == 0)
        def _():
            carry[...] = jnp.zeros_like(carry)

        @pl.when((i == 0) & (h == 0))
        def _():
            dgn_ref[...] = jnp.zeros_like(dgn_ref)

        gnv = gn_ref[...]
        for c in reversed(range(per)):
            rows = pl.ds(c * CHUNK, CHUNK)
            rv = r_ref[rows, :].astype(F32)
            sg = _sigmoid(rv)
            dyv = dy_ref[rows, :].astype(F32)
            ov = o_ref[rows, :]
            rinv = lax.rsqrt(jnp.mean(ov * ov, axis=-1, keepdims=True) + RMS_EPS)
            dn = dyv * (rv * sg)
            dr_ref[rows, :] = (dyv * (ov * rinv * gnv) * (sg * (1.0 + rv * (1.0 - sg)))).astype(BF16)
            dgn_ref[...] += _colsum(dn * ov * rinv)
            dxh = dn * gnv
            do = rinv * dxh - ov * (rinv * rinv * rinv) * jnp.mean(dxh * ov, axis=-1, keepdims=True)
            dob = (do * scale).astype(BF16)
            qv, kv, vv = q_ref[rows, :], k_ref[rows, :], v_ref[rows, :]
            dq_ref[rows, :] = _dot(dob, st_ref[c].astype(BF16)).astype(BF16)
            dst = carry[...] + _dot_tn(dob, qv)
            if c > 0:
                prev = st_ref[c - 1]
            else:
                prev = jnp.where(i == nblk - 1, 0.0, sp_ref[0])
            ddecay = _colsum(dst * prev)
            la, y = _gate(lr_ref[rows, :], wa_ref[...], ba_ref[...])
            w, decay = _decays(la)
            kf = kv.astype(F32)
            kdec = (kf * w).astype(BF16)
            dstb = dst.astype(BF16)
            dkdec = _dot(vv, dstb)
            dv_ref[rows, :] = _dot_nt(kdec, dstb).astype(BF16)
            dk_ref[rows, :] = (dkdec * w).astype(BF16)
            e = dkdec * kf * w
            dla = jnp.dot(_tri(True), e, preferred_element_type=F32, precision=HI) + ddecay * decay
            dg_ref[rows, :] = dla * (1.0 / GATE_TAU) * _sigmoid(-y)
            carry[...] = decay * dst

    per_head = lambda width: pl.BlockSpec((GB, width), lambda h, i: (rev(i), h))
    return pl.pallas_call(
        body, name=name,
        out_shape=[_sds((t, nh * hk), BF16), _sds((t, nh * hk), BF16), _sds((t, nh * hv), BF16),
                   _sds((t, nh * hv), BF16), _sds((t, nh * hk), F32), _sds((1, hv), F32)],
        grid=(nh, nblk),
        in_specs=_gla_specs(cols, hk, hv, rev)
        + [pl.BlockSpec((LANE, hk), lambda h, i: (0, h)), pl.BlockSpec((1, hk), lambda h, i: (0, h)),
           pl.BlockSpec((1, hv), lambda h, i: (0, 0)), per_head(hv),
           pl.BlockSpec((None, per, hv, hk), lambda h, i: (h, rev(i), 0, 0)),
           pl.BlockSpec((None, 1, hv, hk), lambda h, i: (h, jnp.maximum(rev(i) * per - 1, 0), 0, 0)),
           per_head(hv)],
        out_specs=[per_head(hk), per_head(hk), per_head(hv), per_head(hv), per_head(hk),
                   pl.BlockSpec((1, hv), lambda h, i: (0, 0))],
        scratch_shapes=[pltpu.VMEM((hv, hk), F32)], compiler_params=_params("arbitrary", "arbitrary"),
    )(p, p, p, p, p, wa, ba, gn, o, states, states, dy)


def gate_bwd(p, lr_col, dg, wa, name):
    t, kd = dg.shape
    tm = _row_tile(t, 512, 16)

    def body(lr_ref, dg_ref, wa_ref, dlr_ref, dwa_ref, dba_ref):
        @pl.when(pl.program_id(0) == 0)
        def _():
            dwa_ref[...] = jnp.zeros_like(dwa_ref)
            dba_ref[...] = jnp.zeros_like(dba_ref)

        g = dg_ref[...]
        gb = g.astype(BF16)
        dlr_ref[...] = _dot_nt(gb, wa_ref[...]).astype(BF16)
        dwa_ref[...] += _dot_tn(lr_ref[...], gb)
        dba_ref[...] += _colsum(g)

    return pl.pallas_call(
        body, name=name, out_shape=[_sds((t, LANE), BF16), _sds((LANE, kd), F32), _sds((1, kd), F32)],
        grid=(t // tm,),
        in_specs=[pl.BlockSpec((tm, LANE), lambda i: (i, lr_col // LANE)), pl.BlockSpec((tm, kd), lambda i: (i, 0)),
                  pl.BlockSpec((LANE, kd), lambda i: (0, 0))],
        out_specs=[pl.BlockSpec((tm, LANE), lambda i: (i, 0)), pl.BlockSpec((LANE, kd), lambda i: (0, 0)),
                   pl.BlockSpec((1, kd), lambda i: (0, 0))],
        compiler_params=_params("arbitrary"),
    )(p, dg, wa)


def proj_merge(ya, yb, wa3, wb3, p, ga_col, gb_col, name):
    t, kd = ya.shape
    nb, _, bw = wa3.shape
    tm = _row_tile(t, 512, 16)

    def body(ya_ref, yb_ref, wa_ref, wb_ref, ga_ref, gb_ref, pa_ref, pb_ref, mg_ref):
        pa = _dot(ya_ref[...], wa_ref[...])
        pb = _dot(yb_ref[...], wb_ref[...])
        pa_ref[...] = pa.astype(BF16)
        pb_ref[...] = pb.astype(BF16)
        mg_ref[...] = (_sigmoid(ga_ref[...].astype(F32)) * pa + _sigmoid(gb_ref[...].astype(F32)) * pb).astype(BF16)

    act = pl.BlockSpec((tm, kd), lambda j, i: (i, 0))
    wsp = pl.BlockSpec((None, kd, bw), lambda j, i: (j, 0, 0))
    out = pl.BlockSpec((tm, bw), lambda j, i: (i, j))
    return pl.pallas_call(
        body, name=name, out_shape=[_sds((t, nb * bw), BF16)] * 3, grid=(nb, t // tm),
        in_specs=[act, act, wsp, wsp, pl.BlockSpec((tm, bw), lambda j, i: (i, ga_col // bw + j)),
                  pl.BlockSpec((tm, bw), lambda j, i: (i, gb_col // bw + j))],
        out_specs=[out] * 3, compiler_params=_params("parallel", "parallel"),
    )(ya, yb, wa3, wb3, p, p)


def merge_bwd(dm, w, p, ga_col, gb_col, pa, pb, name):
    t, d = dm.shape
    n = w.shape[0]
    tn = _row_tile(n, 512, LANE)
    tm = _row_tile(t, 256, 16)

    def body(dm_ref, w_ref, ga_ref, gb_ref, pa_ref, pb_ref, dpa_ref, dpb_ref, dga_ref, dgb_ref):
        dmg = _dot_nt(dm_ref[...], w_ref[...])
        sa = _sigmoid(ga_ref[...].astype(F32))
        sb = _sigmoid(gb_ref[...].astype(F32))
        dpa_ref[...] = (dmg * sa).astype(BF16)
        dpb_ref[...] = (dmg * sb).astype(BF16)
        dga_ref[...] = (dmg * pa_ref[...].astype(F32) * sa * (1.0 - sa)).astype(BF16)
        dgb_ref[...] = (dmg * pb_ref[...].astype(F32) * sb * (1.0 - sb)).astype(BF16)

    out = pl.BlockSpec((tm, tn), lambda j, i: (i, j))
    return pl.pallas_call(
        body, name=name, out_shape=[_sds((t, n), BF16)] * 4, grid=(n // tn, t // tm),
        in_specs=[pl.BlockSpec((tm, d), lambda j, i: (i, 0)), pl.BlockSpec((tn, d), lambda j, i: (j, 0)),
                  pl.BlockSpec((tm, tn), lambda j, i: (i, ga_col // tn + j)),
                  pl.BlockSpec((tm, tn), lambda j, i: (i, gb_col // tn + j)), out, out],
        out_specs=[out] * 4, compiler_params=_params("parallel", "parallel"),
    )(dm, w, p, p, pa, pb)


def rel_bias_grad(skew, clip_map, name):
    nh, _, jd = skew.shape
    n_rel = clip_map.shape[1]

    def body(s_ref, c_ref, o_ref):
        sums = jnp.concatenate([_colsum(s_ref[h]) for h in range(nh)], axis=0)
        o_ref[...] = jnp.dot(sums, c_ref[...], preferred_element_type=F32, precision=HI)

    return pl.pallas_call(
        body, name=name, out_shape=_sds((nh, n_rel), F32), in_specs=[VMEM_SPEC, VMEM_SPEC], out_specs=VMEM_SPEC,
        compiler_params=pltpu.CompilerParams(vmem_limit_bytes=VMEM_LIMIT),
    )(skew, clip_map)


MIX_BLOCK = 9 * LANE


def mix_layout(d, a_width, bk, bv):
    main = 3 * a_width + 2 * bk + 2 * bv
    cols = {"qa": 0, "ka": a_width, "va": 2 * a_width, "qb": 3 * a_width, "kb": 3 * a_width + bk,
            "vb": 3 * a_width + 2 * bk, "rb": 3 * a_width + 2 * bk + bv, "ga": main, "gb": main + d,
            "lr": main + 2 * d}
    total = main + 2 * d + LANE
    assert total % MIX_BLOCK == 0
    return cols, main, total


def _mix_pieces(per, main, rank, d):
    out = []
    for lo, hi in ((0, main), (main + rank, main + rank + 2 * d), (main, main + rank)):
        while lo < hi:
            cut = min(hi, (lo // per + 1) * per)
            out.append((lo, cut))
            lo = cut
    return out


def mix_weight_in(g3, main, rank):
    _, per, d = g3.shape
    pieces = [g3[lo // per, lo % per:lo % per + hi - lo] for lo, hi in _mix_pieces(per, main, rank, d)]
    return jnp.concatenate(pieces + [jnp.zeros((LANE - rank, d), g3.dtype)], axis=0)


def mix_weight_grad_out(gt, main, rank, per):
    d = gt.shape[1]
    blocks = [[] for _ in range(N_DEV)]
    pos = 0
    for lo, hi in _mix_pieces(per, main, rank, d):
        blocks[lo // per].append((lo, gt[pos:pos + hi - lo]))
        pos += hi - lo
    return jnp.stack([jnp.concatenate([x for _, x in sorted(b, key=lambda e: e[0])], axis=0) for b in blocks])


def ffn_forward(h, sh, sc, g, w_in3, w_out_of, ln_g, ln_b, tag):
    a, b, s = ffn_in(h, sh, sc, w_in3, f"{tag}_in")
    w_out = w_out_of(s)
    f, z, hout = out_ln(s, w_out, h, g, ln_g, ln_b, 0.5, f"{tag}_out")
    return hout, (h, a, b, s, f, z), w_out


def ffn_backward_weights(dh, saved, sh, sc, g, w_in3, w_out, ln_g, tag, target=None):
    hin, a, b, s, f, z = saved
    t, d = hin.shape
    nb, _, bw = w_in3.shape
    half = nb // 2
    fdim = w_out.shape[0]
    res = ln_bwd(dh, z, f, ln_g, g, 0.5, f"{tag}_ln_bwd", target=target)
    dz, df, dln_g, dln_b, dg = res[:5]
    dab = ffn_bwd_act(df, w_out, a, b, f"{tag}_act_bwd")
    tk = _row_tile(t, 512, 16)
    dw_out = matmul_tn(f"{tag}_dwout", s, (tk, bw), lambda n, k: (k, n), df, (tk, d), lambda n, k: (k, 0),
                       _sds((fdim, d), BF16), (bw, d), lambda n, k: (n, 0), fdim // bw)
    dw_in = matmul_tn(f"{tag}_dwin", hin, (tk, d), lambda n, k: (k, 0), dab, (None, tk, bw),
                      lambda n, k: (n // half, k, n % half), _sds((nb, d, bw), BF16), (None, d, bw),
                      lambda n, k: (n, 0, 0), nb, mod=(sh, sc))
    grads = dict(w_in=dw_in, w_out=dw_out.reshape(N_DEV, fdim // N_DEV, d), ln_g=dln_g, ln_b=dln_b, g=dg)
    return (dab, dz), grads, (res[5] if target is not None else None)


def ffn_backward_input(carry, saved, sc, w_in3, tag):
    dab, dz = carry
    hin = saved[0]
    t = hin.shape[0]
    nb, _, bw = w_in3.shape
    half = nb // 2
    tm = _row_tile(t, 512, 16)
    return matmul_nt_blocks(f"{tag}_du", dab, (None, tm, bw), lambda i, k: (k // half, i, k % half),
                            w_in3, t, resid=(dz, hin, sc))


def _after(v, token):
    return v if token is None else v + token[:1, :1]


def local_step(x, target, mod, weights_of, grads_ready, rel_bias, w_alpha2, b_alpha, gla_norm_g, lns):
    t, d = x.shape
    sh1, sc1, g1, sh2, sc2, g2, sh3, sc3, g3 = [mod[i:i + 1] for i in range(N_MOD)]
    ln1_g, ln1_b, ln2_g, ln2_b, ln3_g, ln3_b = lns
    n_heads_a, n_rel = rel_bias.shape
    rank, bk = w_alpha2.shape
    hv = gla_norm_g.shape[1]

    w1 = weights_of("ffn1_in", x)
    h1, saved1, w1["out"] = ffn_forward(x, sh1, sc1, g1, w1["in"], lambda s: weights_of("ffn1_out", s)["out"],
                                        ln1_g, ln1_b, "ffn1")
    wm = weights_of("mix", h1)
    a_width = wm["proj_a"].shape[1]
    bv = wm["proj_b"].shape[1]
    nh_b = bv // hv
    hk = bk // nh_b
    cols, main, total = mix_layout(d, a_width, bk, bv)
    w_mix = mix_weight_in(wm["in_t"], main, rank)
    p = mod_matmul(h1, sh2, sc2, w_mix, MIX_BLOCK, "mix_in")
    bias = bias_table(rel_bias)
    dh = a_width // n_heads_a
    assert PAIR * dh == LANE
    ya, lse = attn_fwd(p, cols, bias, dh, "attn_fwd")
    wa_pad = jnp.zeros((LANE, bk), BF16).at[:rank].set(w_alpha2.astype(BF16))
    o_b, yb, states = gla_fwd(p, cols, wa_pad, b_alpha, gla_norm_g, nh_b, hk, hv, "gla_fwd")
    pa, pb, merged = proj_merge(ya, yb, wm["proj_a"], wm["proj_b"], p, cols["ga"], cols["gb"], "proj_merge")
    m, z2, h2 = out_ln(merged, wm["out"], h1, g2, ln2_g, ln2_b, 1.0, "mix_out")
    w3 = weights_of("ffn2", h2)
    h3, saved3, _ = ffn_forward(h2, sh3, sc3, g3, w3["in"], lambda s: w3["out"], ln3_g, ln3_b, "ffn2")

    carry3, gr3, loss = ffn_backward_weights(h3, saved3, sh3, sc3, g3, w3["in"], w3["out"], ln3_g, "ffn2",
                                             target=target)
    token = grads_ready("ffn2", dict(ffn2_in=gr3["w_in"], ffn2_out=gr3["w_out"]))
    dh2, dsc3, dsh3 = ffn_backward_input(carry3, saved3, _after(sc3, token), w3["in"], "ffn2")
    dz2, dm, dln2_g, dln2_b, dg2 = ln_bwd(dh2, z2, m, ln2_g, g2, 1.0, "mix_ln_bwd")
    dpa, dpb, dga, dgb = merge_bwd(dm, wm["out"], p, cols["ga"], cols["gb"], pa, pb, "merge_bwd")
    tk = _row_tile(t, 512, 16)
    dw_mix_out = matmul_tn("mix_dwout", merged, (tk, 512), lambda n, k: (k, n), dm, (tk, d), lambda n, k: (k, 0),
                           _sds((d, d), BF16), (512, d), lambda n, k: (n, 0), d // 512)
    tm = _row_tile(t, 512, 16)
    pbw = wm["proj_a"].shape[2]
    dya = matmul_nt_blocks("proj_a_dy", dpa, (tm, pbw), lambda i, k: (i, k), wm["proj_a"], t)
    dyb = matmul_nt_blocks("proj_b_dy", dpb, (tm, pbw), lambda i, k: (i, k), wm["proj_b"], t)
    dw_pa = matmul_tn("proj_a_dw", ya, (tk, a_width), lambda n, k: (k, 0), dpa, (tk, pbw), lambda n, k: (k, n),
                      _sds((N_DEV, a_width, pbw), BF16), (None, a_width, pbw), lambda n, k: (n, 0, 0), N_DEV)
    dw_pb = matmul_tn("proj_b_dw", yb, (tk, bv), lambda n, k: (k, 0), dpb, (tk, pbw), lambda n, k: (k, n),
                      _sds((N_DEV, bv, pbw), BF16), (None, bv, pbw), lambda n, k: (n, 0, 0), N_DEV)
    dqb, dkb, dvb, drb, dgate, dgn = gla_bwd(p, cols, wa_pad, b_alpha, gla_norm_g, o_b, states, dyb,
                                             nh_b, hk, hv, "gla_bwd")
    dlr, dwa_pad, dba = gate_bwd(p, cols["lr"], dgate, wa_pad, "gate_bwd")
    dqa, delta, dbias = attn_bwd_q(p, cols, bias, lse, dya, dh, "attn_bwd_q")
    dka, dva = attn_bwd_kv(p, cols, bias, lse, delta, dya, dh, "attn_bwd_kv")
    d_rel = rel_bias_grad(bias_grad_skew(dbias), jnp.asarray(bias_clip_map(n_rel)), "rel_bias_grad")
    dp = jnp.concatenate([dqa, dka, dva, dqb, dkb, dvb, drb,
                          dga, dgb, dlr], axis=1)
    dw_mix_t = matmul_tn("mix_dwin", dp, (tk, MIX_BLOCK), lambda n, k: (k, n), h1, (tk, d), lambda n, k: (k, 0),
                         _sds((total, d), BF16), (MIX_BLOCK, d), lambda n, k: (n, 0), total // MIX_BLOCK,
                         mod=(sh2, sc2), mod_b=True)
    dw_mix_in = mix_weight_grad_out(dw_mix_t, main, rank, wm["in_t"].shape[1])
    token = grads_ready("mix", dict(mix_in=dw_mix_in, proj_a=dw_pa, proj_b=dw_pb,
                                    mix_out=dw_mix_out.reshape(N_DEV, d // N_DEV, d)))
    tm = _row_tile(t, 512, 16)
    dh1, dsc2, dsh2 = matmul_nt_blocks("mix_du", dp, (tm, MIX_BLOCK), lambda i, k: (i, k), w_mix, t,
                                       resid=(dz2, h1, _after(sc2, token)))
    carry1, gr1, _ = ffn_backward_weights(dh1, saved1, sh1, sc1, g1, w1["in"], w1["out"], ln1_g, "ffn1")
    token = grads_ready("ffn1", dict(ffn1_in=gr1["w_in"], ffn1_out=gr1["w_out"]))
    dx, dsc1, dsh1 = ffn_backward_input(carry1, saved1, _after(sc1, token), w1["in"], "ffn1")

    dmod = [dsh1, dsc1, gr1["g"], dsh2, dsc2, dg2, dsh3, dsc3, gr3["g"]]
    small = dict(ln1_g=gr1["ln_g"], ln1_b=gr1["ln_b"], ln2_g=dln2_g, ln2_b=dln2_b, ln3_g=gr3["ln_g"],
                 ln3_b=gr3["ln_b"], b_alpha=dba, gla_norm_g=dgn, w_alpha2=dwa_pad[:rank], rel_bias=d_rel)
    return loss, dx, dmod, small


GROUPS = dict(ffn1=("ffn1_in", "ffn1_out"), mix=("mix_in", "proj_a", "proj_b", "mix_out"),
              ffn2=("ffn2_in", "ffn2_out"))
GATHERS = dict(ffn1_in=("ffn1_in",), ffn1_out=("ffn1_out",), mix=GROUPS["mix"], ffn2=GROUPS["ffn2"])
SMALL_REPLICATED = ("b_ada", "ln1_g", "ln1_b", "ln2_g", "ln2_b", "ln3_g", "ln3_b", "b_alpha", "gla_norm_g")
SMALL_SHARDED = ("rel_bias", "w_alpha2")
WEIGHT_ORDER = ("w_ada", "b_ada", "ffn1_w_in", "ffn1_w_out", "ln1_g", "ln1_b", "w_mix_in", "rel_bias", "w_alpha2",
                "b_alpha", "gla_norm_g", "w_proj_a", "w_proj_b", "w_mix_out", "ln2_g", "ln2_b", "ffn2_w_in",
                "ffn2_w_out", "ln3_g", "ln3_b")
BIG_NAME = dict(ffn1_in="ffn1_w_in", ffn1_out="ffn1_w_out", mix_in="w_mix_in", proj_a="w_proj_a",
                proj_b="w_proj_b", mix_out="w_mix_out", ffn2_in="ffn2_w_in", ffn2_out="ffn2_w_out")


def kernel(x, c, w_ada, b_ada, ffn1_w_in, ffn1_w_out, ln1_g, ln1_b, w_mix_in, rel_bias, w_alpha2, b_alpha, gla_norm_g, w_proj_a, w_proj_b, w_mix_out, ln2_g, ln2_b, ffn2_w_in, ffn2_w_out, ln3_g, ln3_b, loss_target, m_w_ada, m_b_ada, m_ffn1_w_in, m_ffn1_w_out, m_ln1_g, m_ln1_b, m_w_mix_in, m_rel_bias, m_w_alpha2, m_b_alpha, m_gla_norm_g, m_w_proj_a, m_w_proj_b, m_w_mix_out, m_ln2_g, m_ln2_b, m_ffn2_w_in, m_ffn2_w_out, m_ln3_g, m_ln3_b, v_w_ada, v_b_ada, v_ffn1_w_in, v_ffn1_w_out, v_ln1_g, v_ln1_b, v_w_mix_in, v_rel_bias, v_w_alpha2, v_b_alpha, v_gla_norm_g, v_w_proj_a, v_w_proj_b, v_w_mix_out, v_ln2_g, v_ln2_b, v_ffn2_w_in, v_ffn2_w_out, v_ln3_g, v_ln3_b):
    env = dict(locals())
    w = {n: env[n] for n in WEIGHT_ORDER}
    mom = {n: env["m_" + n] for n in WEIGHT_ORDER}
    var = {n: env["v_" + n] for n in WEIGHT_ORDER}
    me = _me()
    dev = _lin(me)
    core = jnp.reshape(me[2], (1,)).astype(jnp.int32)
    chip = jnp.reshape(2 * me[0] + me[1], (1,)).astype(jnp.int32)
    d = x.shape[-1]

    ada_cols = w_ada.shape[-1]
    c_all = all_gather_small(c, "gather_c")[:, 0, :]
    b_cols = lax.dynamic_slice_in_dim(b_ada, dev * ada_cols, ada_cols, axis=1)
    mod_cols = adaln_cols(c_all, w_ada[0], b_cols, "adaln_cols")
    mod_all = all_gather_small(mod_cols, "gather_mod")
    mod = lax.dynamic_index_in_dim(mod_all, dev, axis=1, keepdims=False).reshape(N_MOD, d)

    small_w = all_gather_small(jnp.concatenate([rel_bias[0], w_alpha2[0]], axis=1), "gather_small_w")
    n_rel_cols = rel_bias.shape[-1]
    rel_full = small_w[:, :, :n_rel_cols].transpose(1, 0, 2).reshape(small_w.shape[1], -1)
    wa2_full = small_w[:, :, n_rel_cols:].transpose(1, 0, 2).reshape(small_w.shape[1], -1)

    def shard(n):
        s = w[BIG_NAME[n]][0].astype(BF16)
        return s.T if n == "mix_in" else s

    dev_idx = jnp.reshape(dev, (1,)).astype(jnp.int32)
    shards = {n: shard(n) for members in GATHERS.values() for n in members}
    lands = {n: place_own(dev_idx, s, f"place_own_{n}") for n, s in shards.items()}
    started = {}
    order = small_w[0, :1, :1] + mod_all[0, :1, :1]
    for grp, members in GATHERS.items():
        started[grp] = gather_start([shards[n] for n in members], [lands[n] for n in members], order,
                                    f"gather_start_{grp}")
        order = started[grp][-1]
    mod = _after(mod, order)

    def weights_of(grp, after):
        _, zones = gather_wait(started[grp], after, f"gather_wait_{grp}")
        full = dict(zip(GATHERS[grp], gather_forward(zones, f"gather_forward_{grp}")))
        if grp == "mix":
            return dict(in_t=full["mix_in"], proj_a=full["proj_a"], proj_b=full["proj_b"],
                        out=full["mix_out"].reshape(-1, d))
        return {"in" if n.endswith("_in") else "out": v if n.endswith("_in") else v.reshape(-1, d)
                for n, v in full.items()}

    exchanges = {}

    def grads_ready(grp, grads):
        names = GROUPS[grp]
        got = pair_exchange([grads[n] for n in names], f"grad_pair_exchange_{grp}")
        sums = [pair_add(core, grads[n], g, f"grad_pair_add_{n}") for n, g in zip(names, got)]
        exchanges[grp] = chip_exchange_start(sums, f"grad_chip_start_{grp}")
        return exchanges[grp][-1]

    lns = [ln1_g, ln1_b, ln2_g, ln2_b, ln3_g, ln3_b]
    loss, dx, dmod, small = local_step(x[0], loss_target[0], mod, weights_of, grads_ready, rel_full, wa2_full,
                                       b_alpha, gla_norm_g, lns)
    loss = lax.psum(loss[0, 0], ("x", "y", "c"))

    packed = jnp.concatenate([g.reshape(1, -1) for g in dmod]
                             + [small[n].reshape(1, -1) for n in SMALL_REPLICATED[1:] + SMALL_SHARDED], axis=1)
    parts = all_gather_small(packed, "gather_small_grads")
    n_mod = N_MOD * d
    dmod_all = parts[:, 0, :n_mod]
    g_w_ada = adaln_wgrad(c_all, lax.dynamic_slice_in_dim(dmod_all, dev * ada_cols, ada_cols, axis=1), "adaln_wgrad")
    out = {}
    out["w_ada"] =[o[None] for o in adamw_sum(g_w_ada[None], w_ada[0], m_w_ada[0], v_w_ada[0], "adamw_w_ada")]

    def pack(src):
        rows = [src[n].reshape(1, -1) for n in SMALL_REPLICATED]
        return jnp.concatenate(rows + [src[n].reshape(1, -1) for n in SMALL_SHARDED], axis=1)

    n_rep = sum(w[n].size for n in SMALL_REPLICATED)
    rep_parts = parts[:, :, :n_rep]
    off = n_rep
    shard_parts = []
    for n in SMALL_SHARDED:
        rows, cols_local = w[n].shape[1], w[n].shape[2]
        full_part = parts[:, 0, off:off + rows * cols_local * N_DEV].reshape(N_DEV, rows, cols_local * N_DEV)
        mine = lax.dynamic_slice_in_dim(full_part, dev * cols_local, cols_local, axis=2)
        shard_parts.append(mine.reshape(N_DEV, 1, rows * cols_local))
        off += rows * cols_local * N_DEV
    small_parts = jnp.concatenate([rep_parts] + shard_parts, axis=2)
    res = adamw_sum(small_parts, pack(w), pack(mom), pack(var), "adamw_small")
    off = 0
    for n in SMALL_REPLICATED + SMALL_SHARDED:
        size = w[n].size
        out[n] = [r[:, off:off + size].reshape(w[n].shape) for r in res]
        off += size

    order = res[0]
    for grp in reversed(list(GROUPS)):
        sums, recv = chip_exchange_wait(exchanges[grp], order, f"grad_chip_wait_{grp}")
        for n, hsum, r in zip(GROUPS[grp], sums, recv):
            full = BIG_NAME[n]
            if n == "mix_in":
                g = owned_sum(chip, hsum, r, f"owned_sum_{n}").T
                res_n = adamw_sum(g[None], w[full][0], mom[full][0], var[full][0], f"adamw_{n}")
            else:
                res_n = adamw_owned(chip, hsum, r, w[full][0], mom[full][0], var[full][0], f"adamw_{n}")
            out[full] = [o[None] for o in res_n]
            order = res_n[0]

    flat = [loss, dx[None]]
    for k in range(4):
        flat += [out[n][k] for n in WEIGHT_ORDER]
    return tuple(flat)
```

```python
import functools

import numpy as np
import jax
import jax.numpy as jnp
from jax import lax
from jax.experimental import pallas as pl
from jax.experimental.pallas import tpu as pltpu

F32 = jnp.float32
BF16 = jnp.bfloat16
MESH = pl.DeviceIdType.MESH
N_DEV = 8
N_CHIP = 4

CHUNK = 64
A_PAST_CHUNKS = 8
REL_CLIP = 256
GATE_TAU = 16.0
N_MOD = 9
DEPTH = 1
ALPHA = (2.0 * DEPTH) ** 0.25
LN_EPS = 1e-5
RMS_EPS = 1e-6
ADAM_LR = 0.001
ADAM_B1 = 0.9
ADAM_B2 = 0.999
ADAM_EPS = 1e-08
ADAM_WD = 0.01
ADAM_STEP = 10

LANE = 128
VMEM_LIMIT = 56 * 2 ** 20
QB = 4 * CHUNK
KW = 3 * QB
GB = 8 * CHUNK
NEG = -1e30
HI = lax.Precision.HIGHEST

ANY = pl.BlockSpec(memory_space=pl.ANY)
VMEM_SPEC = pl.BlockSpec(memory_space=pltpu.VMEM)


def _params(*sem):
    return pltpu.CompilerParams(dimension_semantics=sem, vmem_limit_bytes=VMEM_LIMIT)


def _sds(shape, dtype):
    return jax.ShapeDtypeStruct(shape, dtype)


def _dot(a, b):
    return jnp.dot(a, b, preferred_element_type=F32)


def _dot_nt(a, b):
    return lax.dot_general(a, b, (((1,), (1,)), ((), ())), preferred_element_type=F32)


def _dot_tn(a, b):
    return lax.dot_general(a, b, (((0,), (0,)), ((), ())), preferred_element_type=F32)


def _sigmoid(x):
    return 0.5 * jnp.tanh(0.5 * x) + 0.5


def _colsum(x):
    return jnp.sum(x, axis=0, keepdims=True)


def _row_tile(rows, cap, mult):
    for t in range(min(rows, cap), 0, -1):
        if rows % t == 0 and t % mult == 0:
            return t
    return rows


def _me():
    return lax.axis_index("x"), lax.axis_index("y"), lax.axis_index("c")


def _flip(me, k):
    return tuple((1 - p) if (k >> s) & 1 else p for p, s in zip(me, (2, 1, 0)))


def _lin(p):
    return 4 * p[0] + 2 * p[1] + p[2]


def all_gather_small(x, name):
    r, n = x.shape

    def body(x_ref, out_ref, send_sems, recv_sems, local_sem):
        me = _me()
        mine = pltpu.make_async_copy(x_ref, out_ref.at[_lin(me)], local_sem)
        mine.start()
        sends = []
        for k in range(1, N_DEV):
            cp = pltpu.make_async_remote_copy(
                src_ref=x_ref, dst_ref=out_ref.at[_lin(me)], send_sem=send_sems.at[k - 1],
                recv_sem=recv_sems.at[k - 1], device_id=_flip(me, k), device_id_type=MESH)
            cp.start()
            sends.append(cp)
        for k in range(1, N_DEV):
            peer = _flip(me, k)
            pltpu.make_async_remote_copy(
                src_ref=x_ref, dst_ref=out_ref.at[_lin(peer)], send_sem=send_sems.at[k - 1],
                recv_sem=recv_sems.at[k - 1], device_id=peer, device_id_type=MESH).wait_recv()
        for cp in sends:
            cp.wait_send()
        mine.wait()

    return pl.pallas_call(
        body, name=name, out_shape=_sds((N_DEV, r, n), x.dtype),
        in_specs=[VMEM_SPEC], out_specs=VMEM_SPEC,
        scratch_shapes=[pltpu.SemaphoreType.DMA((N_DEV - 1,)), pltpu.SemaphoreType.DMA((N_DEV - 1,)),
                        pltpu.SemaphoreType.DMA],
    )(x)


HBM_SPEC = pl.BlockSpec(memory_space=pltpu.HBM)
SEM_SPEC = pl.BlockSpec(memory_space=pltpu.SEMAPHORE)
EFFECT = pltpu.SideEffectType.DATAFLOW_SIDE_EFFECTING
FIRST = N_CHIP


def _hbm(v):
    return pltpu.with_memory_space_constraint(v, pltpu.HBM)


def _other_chips(mx, my):
    return [(1 - mx, my), (mx, 1 - my), (1 - mx, 1 - my)]


def place_own(dev, shard, name):
    rows, cols = shard.shape
    tr, tc = _tile2(rows, cols, 16)

    def body(dev_ref, s_ref, land_ref, o_ref):
        o_ref[...] = s_ref[...]

    land = lax.empty((N_DEV, rows, cols), shard.dtype)
    return pl.pallas_call(
        body, name=name, out_shape=_sds(land.shape, land.dtype),
        grid_spec=pltpu.PrefetchScalarGridSpec(
            num_scalar_prefetch=1, grid=(rows // tr, cols // tc),
            in_specs=[pl.BlockSpec((tr, tc), lambda i, j, d: (i, j)), ANY],
            out_specs=pl.BlockSpec((None, tr, tc), lambda i, j, d: (d[0], i, j))),
        input_output_aliases={2: 0}, compiler_params=_params("parallel", "parallel"),
    )(dev, shard, land)


def gather_start(shards, lands, after, name):
    n = len(shards)

    def body(*refs):
        ins, zones = refs[:n], refs[n:2 * n]
        send_sems, recv_sems = refs[2 * n + 1], refs[2 * n + 2]
        token = refs[-1]
        me = _me()
        mx, my, mc = me
        for a in range(n):
            dst = zones[a].at[_lin(me)]
            targets = [(mx, my, 1 - mc)] + [(*chip, mc) for chip in _other_chips(mx, my)]
            for k, to in enumerate(targets):
                pltpu.make_async_remote_copy(
                    src_ref=ins[a], dst_ref=dst, send_sem=send_sems.at[a * FIRST + k],
                    recv_sem=recv_sems.at[a * FIRST + k], device_id=to, device_id_type=MESH).start()
        token[...] = jnp.zeros_like(token)

    sems = pltpu.SemaphoreType.DMA((n * FIRST,))
    out = pl.pallas_call(
        body, name=name,
        out_shape=(sems, sems, *[pltpu.HBM(s.shape, s.dtype) for s in shards],
                   *[pltpu.HBM(z.shape, z.dtype) for z in lands], _sds((8, LANE), F32)),
        in_specs=[HBM_SPEC] * (2 * n) + [ANY],
        out_specs=(SEM_SPEC, SEM_SPEC, *[HBM_SPEC] * (2 * n), VMEM_SPEC),
        input_output_aliases={a: 2 + a for a in range(2 * n)},
        compiler_params=pltpu.CompilerParams(has_side_effects=EFFECT),
    )(*[_hbm(s) for s in shards], *[_hbm(z) for z in lands], after)
    return out[0], out[1], out[2:2 + n], out[2 + n:2 + 2 * n], out[-1]


def gather_wait(started, after, name):
    send_sems, recv_sems, shards, lands, _ = started
    n = len(shards)

    def body(*refs):
        ins, zones = refs[:n], refs[n:2 * n]
        send_ref, recv_ref = refs[2 * n], refs[2 * n + 1]
        mx, my, mc = _me()
        for a in range(n):
            for k in range(FIRST):
                cp = pltpu.make_async_remote_copy(
                    src_ref=ins[a], dst_ref=zones[a].at[0], send_sem=send_ref.at[a * FIRST + k],
                    recv_sem=recv_ref.at[a * FIRST + k], device_id=(mx, my, 1 - mc), device_id_type=MESH)
                cp.wait_send()
                cp.wait_recv()

    out = pl.pallas_call(
        body, name=name,
        out_shape=(*[pltpu.HBM(s.shape, s.dtype) for s in shards], *[pltpu.HBM(z.shape, z.dtype) for z in lands]),
        in_specs=[HBM_SPEC] * (2 * n) + [SEM_SPEC, SEM_SPEC, ANY], out_specs=tuple([HBM_SPEC] * (2 * n)),
        input_output_aliases={a: a for a in range(2 * n)},
        compiler_params=pltpu.CompilerParams(has_side_effects=EFFECT),
    )(*shards, *lands, send_sems, recv_sems, after)
    return out[:n], out[n:]


def gather_forward(lands, name):
    n = len(lands)
    rel = N_CHIP - 1

    def body(*refs):
        zones, outs = refs[:n], refs[n:2 * n]
        send_sems, recv_sems = refs[2 * n:]
        mx, my, mc = _me()
        chips = _other_chips(mx, my)

        def copy(a, j, core):
            blk = _lin((*chips[j], core))
            return pltpu.make_async_remote_copy(
                src_ref=zones[a].at[blk], dst_ref=outs[a].at[blk], send_sem=send_sems.at[a * rel + j],
                recv_sem=recv_sems.at[a * rel + j], device_id=(mx, my, 1 - mc), device_id_type=MESH)

        sends = [copy(a, j, mc) for a in range(n) for j in range(rel)]
        for cp in sends:
            cp.start()
        for a in range(n):
            for j in range(rel):
                copy(a, j, 1 - mc).wait_recv()
        for cp in sends:
            cp.wait_send()

    return pl.pallas_call(
        body, name=name, out_shape=[_sds(z.shape, z.dtype) for z in lands],
        in_specs=[ANY] * n, out_specs=[ANY] * n, input_output_aliases={a: a for a in range(n)},
        scratch_shapes=[pltpu.SemaphoreType.DMA((n * rel,)), pltpu.SemaphoreType.DMA((n * rel,))],
    )(*lands)


def pair_exchange(gs, name):
    n = len(gs)

    def body(*refs):
        ins, outs = refs[:n], refs[n:2 * n]
        send_sems, recv_sems = refs[2 * n:]
        mx, my, mc = _me()
        cps = []
        for a in range(n):
            for q in range(N_CHIP):
                cp = pltpu.make_async_remote_copy(
                    src_ref=ins[a].at[2 * q + (1 - mc)], dst_ref=outs[a].at[q],
                    send_sem=send_sems.at[a * N_CHIP + q], recv_sem=recv_sems.at[a * N_CHIP + q],
                    device_id=(mx, my, 1 - mc), device_id_type=MESH)
                cp.start()
                cps.append(cp)
        for cp in cps:
            cp.wait()

    return pl.pallas_call(
        body, name=name, out_shape=[_sds((N_CHIP,) + g.shape[1:], g.dtype) for g in gs],
        in_specs=[ANY] * n, out_specs=[ANY] * n,
        scratch_shapes=[pltpu.SemaphoreType.DMA((n * N_CHIP,)), pltpu.SemaphoreType.DMA((n * N_CHIP,))],
    )(*gs)


def _tile2(rows, cols, row_mult):
    tr = _row_tile(rows, 512, row_mult)
    if tr < rows or rows * cols <= 2 ** 20:
        return tr, cols
    return rows, _row_tile(cols, 512, LANE)


def pair_add(core, g, got, name):
    _, rows, cols = g.shape
    tr, tc = _tile2(rows, cols, 16)

    def body(core_ref, g_ref, got_ref, h_ref):
        h_ref[...] = (g_ref[...].astype(F32) + got_ref[...].astype(F32)).astype(h_ref.dtype)

    blk = pl.BlockSpec((None, tr, tc), lambda q, i, j, c: (q, i, j))
    return pl.pallas_call(
        body, name=name, out_shape=_sds((N_CHIP, rows, cols), g.dtype),
        grid_spec=pltpu.PrefetchScalarGridSpec(
            num_scalar_prefetch=1, grid=(N_CHIP, rows // tr, cols // tc),
            in_specs=[pl.BlockSpec((None, tr, tc), lambda q, i, j, c: (2 * q + c[0], i, j)), blk],
            out_specs=blk),
        compiler_params=_params("parallel", "parallel", "parallel"),
    )(core, g, got)


def chip_routes(mx, my, mc):
    return [(2 * px + py, k, (px, py, mc)) for k, (px, py) in enumerate(_other_chips(mx, my))]


def pair_routes(mx, my, mc):
    return [(2 * q + (1 - mc), q, (mx, my, 1 - mc)) for q in range(N_CHIP)]


def exchange_start(hs, routes, name):
    n = len(hs)
    rel = len(routes(0, 0, 0))
    lands = [lax.empty((rel,) + h.shape[1:], h.dtype) for h in hs]

    def body(*refs):
        ins, zones = refs[:n], refs[n:2 * n]
        send_sems, recv_sems = refs[2 * n], refs[2 * n + 1]
        token = refs[-1]
        for a in range(n):
            for k, (src, slot, to) in enumerate(routes(*_me())):
                pltpu.make_async_remote_copy(
                    src_ref=ins[a].at[src], dst_ref=zones[a].at[slot], send_sem=send_sems.at[a * rel + k],
                    recv_sem=recv_sems.at[a * rel + k], device_id=to, device_id_type=MESH).start()
        token[...] = jnp.zeros_like(token)

    sems = pltpu.SemaphoreType.DMA((n * rel,))
    out = pl.pallas_call(
        body, name=name,
        out_shape=(sems, sems, *[pltpu.HBM(h.shape, h.dtype) for h in hs],
                   *[pltpu.HBM(z.shape, z.dtype) for z in lands], _sds((8, LANE), F32)),
        in_specs=[HBM_SPEC] * (2 * n), out_specs=(SEM_SPEC, SEM_SPEC, *[HBM_SPEC] * (2 * n), VMEM_SPEC),
        input_output_aliases={a: 2 + a for a in range(2 * n)},
        compiler_params=pltpu.CompilerParams(has_side_effects=EFFECT),
    )(*[_hbm(h) for h in hs], *[_hbm(z) for z in lands])
    return out[0], out[1], out[2:2 + n], out[2 + n:2 + 2 * n], out[-1]


def exchange_wait(started, routes, after, name):
    send_sems, recv_sems, hs, lands, _ = started
    n = len(hs)
    rel = len(routes(0, 0, 0))

    def body(*refs):
        ins, zones = refs[:n], refs[n:2 * n]
        send_ref, recv_ref = refs[2 * n], refs[2 * n + 1]
        for a in range(n):
            for k, (src, slot, to) in enumerate(routes(*_me())):
                cp = pltpu.make_async_remote_copy(
                    src_ref=ins[a].at[src], dst_ref=zones[a].at[slot], send_sem=send_ref.at[a * rel + k],
                    recv_sem=recv_ref.at[a * rel + k], device_id=to, device_id_type=MESH)
                cp.wait_send()
                cp.wait_recv()

    out = pl.pallas_call(
        body, name=name,
        out_shape=(*[pltpu.HBM(h.shape, h.dtype) for h in hs], *[pltpu.HBM(z.shape, z.dtype) for z in lands]),
        in_specs=[HBM_SPEC] * (2 * n) + [SEM_SPEC, SEM_SPEC, ANY], out_specs=tuple([HBM_SPEC] * (2 * n)),
        input_output_aliases={a: a for a in range(2 * n)},
        compiler_params=pltpu.CompilerParams(has_side_effects=EFFECT),
    )(*hs, *lands, send_sems, recv_sems, after)
    return out[:n], out[n:]


def _adam(w, g, m, v):
    m = ADAM_B1 * m + (1.0 - ADAM_B1) * g
    v = ADAM_B2 * v + (1.0 - ADAM_B2) * (g * g)
    m_hat = m / (1.0 - ADAM_B1 ** ADAM_STEP)
    v_hat = v / (1.0 - ADAM_B2 ** ADAM_STEP)
    delta = -ADAM_LR * (m_hat / (jnp.sqrt(v_hat) + ADAM_EPS) + ADAM_WD * w)
    return delta, m, v


def adamw_owned(chip, h, got, w, m, v, name):
    rows, cols = w.shape
    tr = _row_tile(rows, 256, 16)

    def body(chip_ref, h_ref, got_ref, w_ref, m_ref, v_ref, g_out, d_out, m_out, v_out):
        g = h_ref[...].astype(F32)
        for k in range(N_CHIP - 1):
            g = g + got_ref[k].astype(F32)
        d, mn, vn = _adam(w_ref[...], g, m_ref[...], v_ref[...])
        g_out[...] = g
        d_out[...] = d
        m_out[...] = mn
        v_out[...] = vn

    blk = pl.BlockSpec((tr, cols), lambda i, c: (i, 0))
    return pl.pallas_call(
        body, name=name, out_shape=[_sds((rows, cols), F32)] * 4,
        grid_spec=pltpu.PrefetchScalarGridSpec(
            num_scalar_prefetch=1, grid=(rows // tr,),
            in_specs=[pl.BlockSpec((None, tr, cols), lambda i, c: (c[0], i, 0)),
                      pl.BlockSpec((N_CHIP - 1, tr, cols), lambda i, c: (0, i, 0)), blk, blk, blk],
            out_specs=[blk] * 4),
        compiler_params=_params("parallel"),
    )(chip, h, got, w, m, v)


def owned_sum(chip, h, got, name):
    _, rows, cols = h.shape
    tr, tc = _tile2(rows, cols, 16)

    def body(chip_ref, h_ref, got_ref, g_out):
        g = h_ref[...].astype(F32)
        for k in range(N_CHIP - 1):
            g = g + got_ref[k].astype(F32)
        g_out[...] = g

    return pl.pallas_call(
        body, name=name, out_shape=_sds((rows, cols), F32),
        grid_spec=pltpu.PrefetchScalarGridSpec(
            num_scalar_prefetch=1, grid=(rows // tr, cols // tc),
            in_specs=[pl.BlockSpec((None, tr, tc), lambda i, j, c: (c[0], i, j)),
                      pl.BlockSpec((N_CHIP - 1, tr, tc), lambda i, j, c: (0, i, j))],
            out_specs=pl.BlockSpec((tr, tc), lambda i, j, c: (i, j))),
        compiler_params=_params("parallel", "parallel"),
    )(chip, h, got)


def adamw_sum(parts, w, m, v, name):
    n_parts, rows, cols = parts.shape
    tr = _row_tile(rows, 256, 8)

    def body(p_ref, w_ref, m_ref, v_ref, g_out, d_out, m_out, v_out):
        g = p_ref[0]
        for k in range(1, n_parts):
            g = g + p_ref[k]
        d, mn, vn = _adam(w_ref[...], g, m_ref[...], v_ref[...])
        g_out[...] = g
        d_out[...] = d
        m_out[...] = mn
        v_out[...] = vn

    blk = pl.BlockSpec((tr, cols), lambda i: (i, 0))
    return pl.pallas_call(
        body, name=name, out_shape=[_sds((rows, cols), F32)] * 4, grid=(rows // tr,),
        in_specs=[pl.BlockSpec((n_parts, tr, cols), lambda i: (0, i, 0)), blk, blk, blk],
        out_specs=[blk] * 4, compiler_params=_params("parallel"),
    )(parts, w, m, v)


def adaln_cols(c_all, w, b, name):
    d, n = w.shape
    tn = _row_tile(n, 768, LANE)

    def body(c_ref, w_ref, b_ref, o_ref):
        c = c_ref[...]
        o_ref[...] = jnp.dot(c * _sigmoid(c), w_ref[...], preferred_element_type=F32, precision=HI) + b_ref[...]

    return pl.pallas_call(
        body, name=name, out_shape=_sds((N_DEV, n), F32), grid=(n // tn,),
        in_specs=[pl.BlockSpec((N_DEV, d), lambda j: (0, 0)), pl.BlockSpec((d, tn), lambda j: (0, j)),
                  pl.BlockSpec((1, tn), lambda j: (0, j))],
        out_specs=pl.BlockSpec((N_DEV, tn), lambda j: (0, j)), compiler_params=_params("parallel"),
    )(c_all, w, b)


def adaln_wgrad(c_all, dmod_cols, name):
    d = c_all.shape[1]
    n = dmod_cols.shape[1]
    tn = _row_tile(n, 768, LANE)

    def body(c_ref, g_ref, o_ref):
        c = c_ref[...]
        o_ref[...] = lax.dot_general(c * _sigmoid(c), g_ref[...], (((0,), (0,)), ((), ())),
                                     preferred_element_type=F32, precision=HI)

    return pl.pallas_call(
        body, name=name, out_shape=_sds((d, n), F32), grid=(n // tn,),
        in_specs=[pl.BlockSpec((N_DEV, d), lambda j: (0, 0)), pl.BlockSpec((N_DEV, tn), lambda j: (0, j))],
        out_specs=pl.BlockSpec((d, tn), lambda j: (0, j)), compiler_params=_params("parallel"),
    )(c_all, dmod_cols)


def _modulate(h, sh, sc):
    return (h * (1.0 + sc) + sh).astype(BF16)


def ffn_in(h, sh, sc, w3, name):
    t, d = h.shape
    nb, _, bw = w3.shape
    half = nb // 2
    tm = _row_tile(t, 256, 16)

    def body(h_ref, sh_ref, sc_ref, wa_ref, wb_ref, a_ref, b_ref, s_ref):
        u = _modulate(h_ref[...], sh_ref[...], sc_ref[...])
        a = _dot(u, wa_ref[...])
        b = _dot(u, wb_ref[...])
        a_ref[...] = a.astype(BF16)
        b_ref[...] = b.astype(BF16)
        s_ref[...] = (a * _sigmoid(a) * b).astype(BF16)

    vec = pl.BlockSpec((1, d), lambda j, i: (0, 0))
    out = pl.BlockSpec((tm, bw), lambda j, i: (i, j))
    return pl.pallas_call(
        body, name=name, out_shape=[_sds((t, half * bw), BF16)] * 3, grid=(half, t // tm),
        in_specs=[pl.BlockSpec((tm, d), lambda j, i: (i, 0)), vec, vec,
                  pl.BlockSpec((None, d, bw), lambda j, i: (j, 0, 0)),
                  pl.BlockSpec((None, d, bw), lambda j, i: (j + half, 0, 0))],
        out_specs=[out] * 3, compiler_params=_params("parallel", "parallel"),
    )(h, sh, sc, w3, w3)


def mod_matmul(h, sh, sc, wt, bw, name):
    t, d = h.shape
    n = wt.shape[0]
    tm = _row_tile(t, 256, 16)

    def body(h_ref, sh_ref, sc_ref, w_ref, o_ref):
        o_ref[...] = _dot_nt(_modulate(h_ref[...], sh_ref[...], sc_ref[...]), w_ref[...]).astype(BF16)

    vec = pl.BlockSpec((1, d), lambda j, i: (0, 0))
    return pl.pallas_call(
        body, name=name, out_shape=_sds((t, n), BF16), grid=(n // bw, t // tm),
        in_specs=[pl.BlockSpec((tm, d), lambda j, i: (i, 0)), vec, vec, pl.BlockSpec((bw, d), lambda j, i: (j, 0))],
        out_specs=pl.BlockSpec((tm, bw), lambda j, i: (i, j)), compiler_params=_params("parallel", "parallel"),
    )(h, sh, sc, wt)


def out_ln(s, w, hin, gmod, ln_g, ln_b, coef, name):
    t, kdim = s.shape
    d = w.shape[1]
    tm = _row_tile(t, 256, 16)

    def body(s_ref, w_ref, hin_ref, gm_ref, g_ref, b_ref, f_ref, z_ref, h_ref):
        f = _dot(s_ref[...], w_ref[...])
        z = ALPHA * hin_ref[...] + (coef * gm_ref[...]) * f
        mu = jnp.mean(z, axis=-1, keepdims=True)
        zc = z - mu
        var = jnp.mean(zc * zc, axis=-1, keepdims=True)
        f_ref[...] = f.astype(BF16)
        z_ref[...] = z
        h_ref[...] = zc * lax.rsqrt(var + LN_EPS) * g_ref[...] + b_ref[...]

    vec = pl.BlockSpec((1, d), lambda i: (0, 0))
    row = pl.BlockSpec((tm, d), lambda i: (i, 0))
    return pl.pallas_call(
        body, name=name, out_shape=[_sds((t, d), BF16), _sds((t, d), F32), _sds((t, d), F32)],
        grid=(t // tm,),
        in_specs=[pl.BlockSpec((tm, kdim), lambda i: (i, 0)),
                  pl.BlockSpec((kdim, d), lambda i: (0, 0), pipeline_mode=pl.Buffered(1)), row, vec, vec, vec],
        out_specs=[row, row, row], compiler_params=_params("parallel"),
    )(s, w, hin, gmod, ln_g, ln_b)


def ln_bwd(dh, z, f, ln_g, gmod, coef, name, target=None):
    t, d = z.shape
    tm = _row_tile(t, 256, 16)
    head = target is not None

    def body(*refs):
        if head:
            dh_ref, tg_ref, z_ref, f_ref, g_ref, gm_ref, dz_ref, df_ref, dg_ref, db_ref, dgm_ref, loss_ref = refs
        else:
            dh_ref, z_ref, f_ref, g_ref, gm_ref, dz_ref, df_ref, dg_ref, db_ref, dgm_ref = refs
        i = pl.program_id(0)

        @pl.when(i == 0)
        def _():
            dg_ref[...] = jnp.zeros_like(dg_ref)
            db_ref[...] = jnp.zeros_like(db_ref)
            dgm_ref[...] = jnp.zeros_like(dgm_ref)
            if head:
                loss_ref[...] = jnp.zeros_like(loss_ref)

        dh = dh_ref[...]
        if head:
            err = dh - tg_ref[...]
            loss_ref[...] += 0.5 * jnp.sum(jnp.mean(err * err, axis=-1, keepdims=True))
            dh = err / d
        zv = z_ref[...]
        mu = jnp.mean(zv, axis=-1, keepdims=True)
        zc = zv - mu
        rstd = lax.rsqrt(jnp.mean(zc * zc, axis=-1, keepdims=True) + LN_EPS)
        xhat = zc * rstd
        dxh = dh * g_ref[...]
        dz = rstd * (dxh - jnp.mean(dxh, axis=-1, keepdims=True)
                     - xhat * jnp.mean(dxh * xhat, axis=-1, keepdims=True))
        dz_ref[...] = dz
        df_ref[...] = ((coef * gm_ref[...]) * dz).astype(BF16)
        dg_ref[...] += _colsum(dh * xhat)
        db_ref[...] += _colsum(dh)
        dgm_ref[...] += _colsum(coef * f_ref[...].astype(F32) * dz)

    vec = pl.BlockSpec((1, d), lambda i: (0, 0))
    row = pl.BlockSpec((tm, d), lambda i: (i, 0))
    ins = [dh] + ([target] if head else []) + [z, f, ln_g, gmod]
    in_specs = [row] + ([row] if head else []) + [row, row, vec, vec]
    out_shape = [_sds((t, d), F32), _sds((t, d), BF16)] + [_sds((1, d), F32)] * 3
    out_specs = [row, row, vec, vec, vec]
    if head:
        out_shape.append(_sds((1, LANE), F32))
        out_specs.append(pl.BlockSpec((1, LANE), lambda i: (0, 0)))
    return pl.pallas_call(
        body, name=name, out_shape=out_shape, grid=(t // tm,), in_specs=in_specs, out_specs=out_specs,
        compiler_params=_params("arbitrary"),
    )(*ins)


def ffn_bwd_act(df, w, a, b, name):
    t, d = df.shape
    fdim = w.shape[0]
    bw = fdim // (N_DEV // 2)
    tm = _row_tile(t, 512, 16)

    def body(df_ref, w_ref, a_ref, b_ref, o_ref):
        ds = _dot_nt(df_ref[...], w_ref[...])
        av = a_ref[...].astype(F32)
        sg = _sigmoid(av)
        o_ref[0] = (ds * b_ref[...].astype(F32) * (sg * (1.0 + av * (1.0 - sg)))).astype(BF16)
        o_ref[1] = (ds * (av * sg)).astype(BF16)

    act = pl.BlockSpec((tm, bw), lambda j, i: (i, j))
    return pl.pallas_call(
        body, name=name, out_shape=_sds((2, t, fdim), BF16), grid=(fdim // bw, t // tm),
        in_specs=[pl.BlockSpec((tm, d), lambda j, i: (i, 0)), pl.BlockSpec((bw, d), lambda j, i: (j, 0)), act, act],
        out_specs=pl.BlockSpec((2, tm, bw), lambda j, i: (0, i, j)),
        compiler_params=_params("parallel", "parallel"),
    )(df, w, a, b)


def matmul_tn(name, a, a_block, a_map, b, b_block, b_map, out_shape, o_block, o_map, n_out, mod=None,
              mod_b=False):
    tk = [s for s in a_block if s is not None][0]
    nk = a.shape[-2] // tk
    m, nn = [s for s in o_block if s is not None]

    def body(*refs):
        if mod is None:
            a_ref, b_ref, o_ref, acc = refs
        else:
            a_ref, sh_ref, sc_ref, b_ref, o_ref, acc = refs
        k = pl.program_id(1)

        @pl.when(k == 0)
        def _():
            acc[...] = jnp.zeros_like(acc)

        av, bv = a_ref[...], b_ref[...]
        if mod is not None and mod_b:
            bv = _modulate(bv, sh_ref[...], sc_ref[...])
        elif mod is not None:
            av = _modulate(av, sh_ref[...], sc_ref[...])
        acc[...] += _dot_tn(av, bv)

        @pl.when(k == nk - 1)
        def _():
            o_ref[...] = acc[...].astype(o_ref.dtype)

    ins = [a] + (list(mod) if mod is not None else []) + [b]
    in_specs = [pl.BlockSpec(a_block, a_map)]
    if mod is not None:
        vec = pl.BlockSpec((1, mod[0].shape[1]), lambda n, k: (0, 0))
        in_specs += [vec, vec]
    in_specs.append(pl.BlockSpec(b_block, b_map))
    return pl.pallas_call(
        body, name=name, out_shape=out_shape, grid=(n_out, nk), in_specs=in_specs,
        out_specs=pl.BlockSpec(o_block, o_map), scratch_shapes=[pltpu.VMEM((m, nn), F32)],
        compiler_params=_params("parallel", "arbitrary"),
    )(*ins)


def matmul_nt_blocks(name, dy, dy_block, dy_map, w3, t, resid=None):
    tm, bw = [s for s in dy_block if s is not None]
    rows = w3.ndim == 2
    if rows:
        nk, n = w3.shape[0] // bw, w3.shape[1]
    else:
        nk, n, _ = w3.shape

    def body(*refs):
        if resid is None:
            dy_ref, w_ref, o_ref, acc = refs
        else:
            dy_ref, w_ref, dz_ref, hin_ref, sc_ref, o_ref, dsc_ref, dsh_ref, acc = refs
        i, k = pl.program_id(0), pl.program_id(1)

        @pl.when(k == 0)
        def _():
            acc[...] = jnp.zeros_like(acc)

        if resid is not None:
            @pl.when((k == 0) & (i == 0))
            def _():
                dsc_ref[...] = jnp.zeros_like(dsc_ref)
                dsh_ref[...] = jnp.zeros_like(dsh_ref)

        acc[...] += _dot(dy_ref[...], w_ref[...]) if rows else _dot_nt(dy_ref[...], w_ref[...])

        @pl.when(k == nk - 1)
        def _():
            du = acc[...]
            if resid is None:
                o_ref[...] = du.astype(o_ref.dtype)
            else:
                o_ref[...] = ALPHA * dz_ref[...] + du * (1.0 + sc_ref[...])
                dsc_ref[...] += _colsum(du * hin_ref[...])
                dsh_ref[...] += _colsum(du)

    row = pl.BlockSpec((tm, n), lambda i, k: (i, 0))
    vec = pl.BlockSpec((1, n), lambda i, k: (0, 0))
    w_spec = pl.BlockSpec((bw, n), lambda i, k: (k, 0)) if rows else pl.BlockSpec((None, n, bw), lambda i, k: (k, 0, 0))
    in_specs = [pl.BlockSpec(dy_block, dy_map), w_spec]
    ins = [dy, w3]
    if resid is None:
        out_shape, out_specs = _sds((t, n), BF16), row
    else:
        ins += list(resid)
        once = pl.BlockSpec((tm, n), lambda i, k: (i, 0), pipeline_mode=pl.Buffered(1))
        in_specs += [once, once, vec]
        out_shape = [_sds((t, n), F32), _sds((1, n), F32), _sds((1, n), F32)]
        out_specs = [row, vec, vec]
    return pl.pallas_call(
        body, name=name, out_shape=out_shape, grid=(t // tm, nk), in_specs=in_specs, out_specs=out_specs,
        scratch_shapes=[pltpu.VMEM((tm, n), F32)], compiler_params=_params("arbitrary", "arbitrary"),
    )(*ins)


REL_W = KW + QB


def bias_table(rel_bias):
    nh, n_rel = rel_bias.shape
    lo = KW - QB - REL_CLIP
    hi = KW - lo - n_rel
    assert n_rel == REL_CLIP + CHUNK and lo >= 0 and hi >= 0
    first, last = rel_bias[:, :1], rel_bias[:, -1:]
    row = jnp.concatenate([jnp.broadcast_to(first, (nh, lo)), rel_bias, jnp.broadcast_to(last, (nh, hi)),
                           jnp.broadcast_to(first, (nh, QB))], axis=1)
    table = jnp.tile(row, (1, QB))[:, :QB * (REL_W - 1)].reshape(nh, QB, REL_W - 1)[:, :, :KW]
    q = np.arange(QB)[:, None] // CHUNK
    k = np.arange(KW)[None, :] // CHUNK
    band = (k >= q) & (k <= q + A_PAST_CHUNKS)
    return jnp.where(band[None], table, NEG)


def bias_grad_skew(dbias):
    nh = dbias.shape[0]
    flat = jnp.pad(dbias, ((0, 0), (0, 0), (0, REL_W - 1 - KW))).reshape(nh, QB * (REL_W - 1))
    return jnp.pad(flat, ((0, 0), (0, QB))).reshape(nh, QB, REL_W)


def bias_clip_map(n_rel):
    m = np.arange(REL_W)
    dist = np.where(m < KW, m, m - REL_W) - (KW - QB)
    idx = np.clip(dist, -REL_CLIP, CHUNK - 1) + REL_CLIP
    return (idx[:, None] == np.arange(n_rel)[None, :]).astype(np.float32)


PAIR = 2


def _pair_specs(col, rows_of):
    return [pl.BlockSpec((QB, LANE), functools.partial(lambda r, h, i: (rows_of(r, i), col // LANE + h), r))
            for r in range(3)]


def _earlier(r, i):
    return jnp.maximum(i - 2 + r, 0)


def _head_lanes(hh, dh):
    lane = lax.broadcasted_iota(jnp.int32, (1, LANE), 1)
    return (lane < dh) if hh == 0 else (lane >= dh)


def _only(x, lanes):
    return jnp.where(lanes, x, jnp.zeros_like(x))


def _scores(q, ks, bias, i, scale):
    s = jnp.concatenate([_dot_nt(q, kk) for kk in ks], axis=1) * scale + bias
    col = lax.broadcasted_iota(jnp.int32, s.shape, 1)
    return jnp.where(col >= (2 - i) * QB, s, NEG)


def attn_fwd(p, cols, bias, dh, name):
    t = p.shape[0]
    nh = bias.shape[0]
    scale = dh ** -0.5

    def body(q_ref, k0, k1, k2, v0, v1, v2, b_ref, o_ref, lse_ref):
        i = pl.program_id(1)
        q = q_ref[...]
        outs = []
        for hh in range(PAIR):
            lanes = _head_lanes(hh, dh)
            s = _scores(q, [_only(kk[...], lanes) for kk in (k0, k1, k2)], b_ref[hh], i, scale)
            m = jnp.max(s, axis=-1, keepdims=True)
            e = jnp.exp(s - m)
            l = jnp.sum(e, axis=-1, keepdims=True)
            eb = e.astype(BF16)
            o = sum(_dot(eb[:, r * QB:(r + 1) * QB], vv[...]) for r, vv in enumerate((v0, v1, v2)))
            outs.append(o / l)
            lse_ref[hh] = m + jnp.log(l)
        o_ref[...] = jnp.where(_head_lanes(0, dh), outs[0], outs[1]).astype(BF16)

    st = pl.BlockSpec((PAIR, QB, 1), lambda h, i: (h, i, 0))
    return pl.pallas_call(
        body, name=name, out_shape=[_sds((t, nh * dh), BF16), _sds((nh, t, 1), F32)], grid=(nh // PAIR, t // QB),
        in_specs=[pl.BlockSpec((QB, LANE), lambda h, i: (i, cols["qa"] // LANE + h))]
        + _pair_specs(cols["ka"], _earlier) + _pair_specs(cols["va"], _earlier)
        + [pl.BlockSpec((PAIR, QB, KW), lambda h, i: (h, 0, 0))],
        out_specs=[pl.BlockSpec((QB, LANE), lambda h, i: (i, h)), st],
        compiler_params=_params("parallel", "parallel"),
    )(p, p, p, p, p, p, p, bias)


def attn_bwd_q(p, cols, bias, lse, dy, dh, name):
    t = p.shape[0]
    nh = bias.shape[0]
    scale = dh ** -0.5

    def body(q_ref, k0, k1, k2, v0, v1, v2, b_ref, lse_ref, dy_ref, dq_ref, dl_ref, db_ref):
        i = pl.program_id(1)

        @pl.when(i == 0)
        def _():
            db_ref[...] = jnp.zeros_like(db_ref)

        q, dyv = q_ref[...], dy_ref[...]
        ks = [k0[...], k1[...], k2[...]]
        dqs = []
        for hh in range(PAIR):
            lanes = _head_lanes(hh, dh)
            s = _scores(q, [_only(kk, lanes) for kk in ks], b_ref[hh], i, scale)
            prob = jnp.exp(s - lse_ref[hh])
            dprob = jnp.concatenate([_dot_nt(dyv, _only(vv[...], lanes)) for vv in (v0, v1, v2)], axis=1)
            delta = jnp.sum(prob * dprob, axis=-1, keepdims=True)
            ds = prob * (dprob - delta)
            dsb = ds.astype(BF16)
            dqs.append(sum(_dot(dsb[:, r * QB:(r + 1) * QB], kk) for r, kk in enumerate(ks)))
            dl_ref[hh] = delta
            db_ref[hh] += ds
        dq_ref[...] = (jnp.where(_head_lanes(0, dh), dqs[0], dqs[1]) * scale).astype(BF16)

    st = pl.BlockSpec((PAIR, QB, 1), lambda h, i: (h, i, 0))
    tab = pl.BlockSpec((PAIR, QB, KW), lambda h, i: (h, 0, 0))
    own = pl.BlockSpec((QB, LANE), lambda h, i: (i, h))
    return pl.pallas_call(
        body, name=name,
        out_shape=[_sds((t, nh * dh), BF16), _sds((nh, t, 1), F32), _sds((nh, QB, KW), F32)],
        grid=(nh // PAIR, t // QB),
        in_specs=[pl.BlockSpec((QB, LANE), lambda h, i: (i, cols["qa"] // LANE + h))]
        + _pair_specs(cols["ka"], _earlier) + _pair_specs(cols["va"], _earlier) + [tab, st, own],
        out_specs=[own, st, tab], compiler_params=_params("parallel", "arbitrary"),
    )(p, p, p, p, p, p, p, bias, lse, dy)


def attn_bwd_kv(p, cols, bias, lse, delta, dy, dh, name):
    t = p.shape[0]
    nh = bias.shape[0]
    nb = t // QB
    scale = dh ** -0.5

    def later(r, j):
        return jnp.minimum(j + r, nb - 1)

    def body(k_ref, v_ref, b_ref, *refs):
        qr, lr, dr, gr = refs[0:3], refs[3:6], refs[6:9], refs[9:12]
        dk_ref, dv_ref = refs[12:]
        j = pl.program_id(1)
        kk, vv = k_ref[...], v_ref[...]
        dks, dvs = [], []
        for hh in range(PAIR):
            lanes = _head_lanes(hh, dh)
            kh, vh = _only(kk, lanes), _only(vv, lanes)
            dk = jnp.zeros((QB, LANE), F32)
            dv = jnp.zeros((QB, LANE), F32)
            for r in range(3):
                seg = 2 - r
                qv, dyv = qr[r][...], gr[r][...]
                s = _dot_nt(qv, kh) * scale + b_ref[hh, :, seg * QB:(seg + 1) * QB]
                prob = jnp.where(j + r < nb, jnp.exp(s - lr[r][hh]), 0.0)
                ds = prob * (_dot_nt(dyv, vh) - dr[r][hh])
                dv = dv + _dot_tn(prob.astype(BF16), dyv)
                dk = dk + _dot_tn(ds.astype(BF16), qv)
            dks.append(dk)
            dvs.append(dv)
        first = _head_lanes(0, dh)
        dk_ref[...] = (jnp.where(first, dks[0], dks[1]) * scale).astype(BF16)
        dv_ref[...] = jnp.where(first, dvs[0], dvs[1]).astype(BF16)

    def stats():
        return [pl.BlockSpec((PAIR, QB, 1), functools.partial(lambda r, h, j: (h, later(r, j), 0), r))
                for r in range(3)]

    def at(col):
        return pl.BlockSpec((QB, LANE), lambda h, j: (j, col // LANE + h))

    own = pl.BlockSpec((QB, LANE), lambda h, j: (j, h))
    return pl.pallas_call(
        body, name=name, out_shape=[_sds((t, nh * dh), BF16)] * 2, grid=(nh // PAIR, nb),
        in_specs=[at(cols["ka"]), at(cols["va"]), pl.BlockSpec((PAIR, QB, KW), lambda h, j: (h, 0, 0))]
        + _pair_specs(cols["qa"], later) + stats() + stats() + _pair_specs(0, later),
        out_specs=[own, own], compiler_params=_params("parallel", "parallel"),
    )(p, p, bias, p, p, p, lse, lse, lse, delta, delta, delta, dy, dy, dy)


def _tri(strict):
    r = lax.broadcasted_iota(jnp.int32, (CHUNK, CHUNK), 0)
    c = lax.broadcasted_iota(jnp.int32, (CHUNK, CHUNK), 1)
    return jnp.where((c < r) if strict else (c <= r), 1.0, 0.0).astype(F32)


def _gate(lr, wa, ba):
    y = _dot(lr, wa) + ba
    return (jnp.minimum(y, 0.0) - jnp.log(1.0 + jnp.exp(-jnp.abs(y)))) / GATE_TAU, y


def _decays(la):
    cum = jnp.dot(_tri(False), la, preferred_element_type=F32, precision=HI)
    last = cum[CHUNK - 1:CHUNK, :]
    return jnp.exp(last - cum), jnp.exp(last)


def _gla_specs(cols, hk, hv, order):
    def at(start, width):
        return pl.BlockSpec((GB, width), lambda h, i: (order(i), start // width + h))
    return [at(cols["qb"], hk), at(cols["kb"], hk), at(cols["vb"], hv), at(cols["rb"], hv),
            pl.BlockSpec((GB, LANE), lambda h, i: (order(i), cols["lr"] // LANE))]


def gla_fwd(p, cols, wa, ba, gn, nh, hk, hv, name):
    t = p.shape[0]
    nc = t // CHUNK
    scale = hk ** -0.5
    per = GB // CHUNK

    def body(q_ref, k_ref, v_ref, r_ref, lr_ref, wa_ref, ba_ref, gn_ref, o_ref, y_ref, st_ref, state):
        @pl.when(pl.program_id(1) == 0)
        def _():
            state[...] = jnp.zeros_like(state)

        for c in range(per):
            rows = pl.ds(c * CHUNK, CHUNK)
            la, _ = _gate(lr_ref[rows, :], wa_ref[...], ba_ref[...])
            w, decay = _decays(la)
            kdec = (k_ref[rows, :].astype(F32) * w).astype(BF16)
            st = decay * state[...] + _dot_tn(v_ref[rows, :], kdec)
            state[...] = st
            st_ref[c] = st
            o = _dot_nt(q_ref[rows, :], st.astype(BF16)) * scale
            o_ref[rows, :] = o
            rinv = lax.rsqrt(jnp.mean(o * o, axis=-1, keepdims=True) + RMS_EPS)
            rv = r_ref[rows, :].astype(F32)
            y_ref[rows, :] = (o * rinv * gn_ref[...] * (rv * _sigmoid(rv))).astype(BF16)

    return pl.pallas_call(
        body, name=name,
        out_shape=[_sds((t, nh * hv), F32), _sds((t, nh * hv), BF16), _sds((nh, nc, hv, hk), F32)],
        grid=(nh, t // GB),
        in_specs=_gla_specs(cols, hk, hv, lambda i: i)
        + [pl.BlockSpec((LANE, hk), lambda h, i: (0, h)), pl.BlockSpec((1, hk), lambda h, i: (0, h)),
           pl.BlockSpec((1, hv), lambda h, i: (0, 0))],
        out_specs=[pl.BlockSpec((GB, hv), lambda h, i: (i, h)), pl.BlockSpec((GB, hv), lambda h, i: (i, h)),
                   pl.BlockSpec((None, per, hv, hk), lambda h, i: (h, i, 0, 0))],
        scratch_shapes=[pltpu.VMEM((hv, hk), F32)], compiler_params=_params("parallel", "arbitrary"),
    )(p, p, p, p, p, wa, ba, gn)


def gla_bwd(p, cols, wa, ba, gn, o, states, dy, nh, hk, hv, name):
    t = p.shape[0]
    nblk = t // GB
    scale = hk ** -0.5
    per = GB // CHUNK

    def rev(i):
        return nblk - 1 - i

    def body(q_ref, k_ref, v_ref, r_ref, lr_ref, wa_ref, ba_ref, gn_ref, o_ref, st_ref, sp_ref, dy_ref,
             dq_ref, dk_ref, dv_ref, dr_ref, dg_ref, dgn_ref, carry):
        h, i = pl.program_id(0), pl.program_id(1)

        @pl.when(i == 0)
        def _():
            carry[...] = jnp.zeros_like(carry)

        @pl.when((i == 0) & (h == 0))
        def _():
            dgn_ref[...] = jnp.zeros_like(dgn_ref)

        gnv = gn_ref[...]
        for c in reversed(range(per)):
            rows = pl.ds(c * CHUNK, CHUNK)
            rv = r_ref[rows, :].astype(F32)
            sg = _sigmoid(rv)
            dyv = dy_ref[rows, :].astype(F32)
            ov = o_ref[rows, :]
            rinv = lax.rsqrt(jnp.mean(ov * ov, axis=-1, keepdims=True) + RMS_EPS)
            dn = dyv * (rv * sg)
            dr_ref[rows, :] = (dyv * (ov * rinv * gnv) * (sg * (1.0 + rv * (1.0 - sg)))).astype(BF16)
            dgn_ref[...] += _colsum(dn * ov * rinv)
            dxh = dn * gnv
            do = rinv * dxh - ov * (rinv * rinv * rinv) * jnp.mean(dxh * ov, axis=-1, keepdims=True)
            dob = (do * scale).astype(BF16)
            qv, kv, vv = q_ref[rows, :], k_ref[rows, :], v_ref[rows, :]
            dq_ref[rows, :] = _dot(dob, st_ref[c].astype(BF16)).astype(BF16)
            dst = carry[...] + _dot_tn(dob, qv)
            if c > 0:
                prev = st_ref[c - 1]
            else:
                prev = jnp.where(i == nblk - 1, 0.0, sp_ref[0])
            ddecay = _colsum(dst * prev)
            la, y = _gate(lr_ref[rows, :], wa_ref[...], ba_ref[...])
            w, decay = _decays(la)
            kf = kv.astype(F32)
            kdec = (kf * w).astype(BF16)
            dstb = dst.astype(BF16)
            dkdec = _dot(vv, dstb)
            dv_ref[rows, :] = _dot_nt(kdec, dstb).astype(BF16)
            dk_ref[rows, :] = (dkdec * w).astype(BF16)
            e = dkdec * kf * w
            dla = jnp.dot(_tri(True), e, preferred_element_type=F32, precision=HI) + ddecay * decay
            dg_ref[rows, :] = dla * (1.0 / GATE_TAU) * _sigmoid(-y)
            carry[...] = decay * dst

    per_head = lambda width: pl.BlockSpec((GB, width), lambda h, i: (rev(i), h))
    return pl.pallas_call(
        body, name=name,
        out_shape=[_sds((t, nh * hk), BF16), _sds((t, nh * hk), BF16), _sds((t, nh * hv), BF16),
                   _sds((t, nh * hv), BF16), _sds((t, nh * hk), F32), _sds((1, hv), F32)],
        grid=(nh, nblk),
        in_specs=_gla_specs(cols, hk, hv, rev)
        + [pl.BlockSpec((LANE, hk), lambda h, i: (0, h)), pl.BlockSpec((1, hk), lambda h, i: (0, h)),
           pl.BlockSpec((1, hv), lambda h, i: (0, 0)), per_head(hv),
           pl.BlockSpec((None, per, hv, hk), lambda h, i: (h, rev(i), 0, 0)),
           pl.BlockSpec((None, 1, hv, hk), lambda h, i: (h, jnp.maximum(rev(i) * per - 1, 0), 0, 0)),
           per_head(hv)],
        out_specs=[per_head(hk), per_head(hk), per_head(hv), per_head(hv), per_head(hk),
                   pl.BlockSpec((1, hv), lambda h, i: (0, 0))],
        scratch_shapes=[pltpu.VMEM((hv, hk), F32)], compiler_params=_params("arbitrary", "arbitrary"),
    )(p, p, p, p, p, wa, ba, gn, o, states, states, dy)


def gate_bwd(p, lr_col, dg, wa, name):
    t, kd = dg.shape
    tm = _row_tile(t, 512, 16)

    def body(lr_ref, dg_ref, wa_ref, dlr_ref, dwa_ref, dba_ref):
        @pl.when(pl.program_id(0) == 0)
        def _():
            dwa_ref[...] = jnp.zeros_like(dwa_ref)
            dba_ref[...] = jnp.zeros_like(dba_ref)

        g = dg_ref[...]
        gb = g.astype(BF16)
        dlr_ref[...] = _dot_nt(gb, wa_ref[...]).astype(BF16)
        dwa_ref[...] += _dot_tn(lr_ref[...], gb)
        dba_ref[...] += _colsum(g)

    return pl.pallas_call(
        body, name=name, out_shape=[_sds((t, LANE), BF16), _sds((LANE, kd), F32), _sds((1, kd), F32)],
        grid=(t // tm,),
        in_specs=[pl.BlockSpec((tm, LANE), lambda i: (i, lr_col // LANE)), pl.BlockSpec((tm, kd), lambda i: (i, 0)),
                  pl.BlockSpec((LANE, kd), lambda i: (0, 0))],
        out_specs=[pl.BlockSpec((tm, LANE), lambda i: (i, 0)), pl.BlockSpec((LANE, kd), lambda i: (0, 0)),
                   pl.BlockSpec((1, kd), lambda i: (0, 0))],
        compiler_params=_params("arbitrary"),
    )(p, dg, wa)


def proj_merge(ya, yb, wa3, wb3, p, ga_col, gb_col, name):
    t, kd = ya.shape
    nb, _, bw = wa3.shape
    tm = _row_tile(t, 512, 16)

    def body(ya_ref, yb_ref, wa_ref, wb_ref, ga_ref, gb_ref, pa_ref, pb_ref, mg_ref):
        pa = _dot(ya_ref[...], wa_ref[...])
        pb = _dot(yb_ref[...], wb_ref[...])
        pa_ref[...] = pa.astype(BF16)
        pb_ref[...] = pb.astype(BF16)
        mg_ref[...] = (_sigmoid(ga_ref[...].astype(F32)) * pa + _sigmoid(gb_ref[...].astype(F32)) * pb).astype(BF16)

    act = pl.BlockSpec((tm, kd), lambda j, i: (i, 0))
    wsp = pl.BlockSpec((None, kd, bw), lambda j, i: (j, 0, 0))
    out = pl.BlockSpec((tm, bw), lambda j, i: (i, j))
    return pl.pallas_call(
        body, name=name, out_shape=[_sds((t, nb * bw), BF16)] * 3, grid=(nb, t // tm),
        in_specs=[act, act, wsp, wsp, pl.BlockSpec((tm, bw), lambda j, i: (i, ga_col // bw + j)),
                  pl.BlockSpec((tm, bw), lambda j, i: (i, gb_col // bw + j))],
        out_specs=[out] * 3, compiler_params=_params("parallel", "parallel"),
    )(ya, yb, wa3, wb3, p, p)


def merge_bwd(dm, w, p, ga_col, gb_col, pa, pb, name):
    t, d = dm.shape
    n = w.shape[0]
    tn = _row_tile(n, 512, LANE)
    tm = _row_tile(t, 256, 16)

    def body(dm_ref, w_ref, ga_ref, gb_ref, pa_ref, pb_ref, dpa_ref, dpb_ref, dga_ref, dgb_ref):
        dmg = _dot_nt(dm_ref[...], w_ref[...])
        sa = _sigmoid(ga_ref[...].astype(F32))
        sb = _sigmoid(gb_ref[...].astype(F32))
        dpa_ref[...] = (dmg * sa).astype(BF16)
        dpb_ref[...] = (dmg * sb).astype(BF16)
        dga_ref[...] = (dmg * pa_ref[...].astype(F32) * sa * (1.0 - sa)).astype(BF16)
        dgb_ref[...] = (dmg * pb_ref[...].astype(F32) * sb * (1.0 - sb)).astype(BF16)

    out = pl.BlockSpec((tm, tn), lambda j, i: (i, j))
    return pl.pallas_call(
        body, name=name, out_shape=[_sds((t, n), BF16)] * 4, grid=(n // tn, t // tm),
        in_specs=[pl.BlockSpec((tm, d), lambda j, i: (i, 0)), pl.BlockSpec((tn, d), lambda j, i: (j, 0)),
                  pl.BlockSpec((tm, tn), lambda j, i: (i, ga_col // tn + j)),
                  pl.BlockSpec((tm, tn), lambda j, i: (i, gb_col // tn + j)), out, out],
        out_specs=[out] * 4, compiler_params=_params("parallel", "parallel"),
    )(dm, w, p, p, pa, pb)


def rel_bias_grad(skew, clip_map, name):
    nh, _, jd = skew.shape
    n_rel = clip_map.shape[1]

    def body(s_ref, c_ref, o_ref):
        sums = jnp.concatenate([_colsum(s_ref[h]) for h in range(nh)], axis=0)
        o_ref[...] = jnp.dot(sums, c_ref[...], preferred_element_type=F32, precision=HI)

    return pl.pallas_call(
        body, name=name, out_shape=_sds((nh, n_rel), F32), in_specs=[VMEM_SPEC, VMEM_SPEC], out_specs=VMEM_SPEC,
        compiler_params=pltpu.CompilerParams(vmem_limit_bytes=VMEM_LIMIT),
    )(skew, clip_map)


MIX_BLOCK = 9 * LANE


def mix_layout(d, a_width, bk, bv):
    main = 3 * a_width + 2 * bk + 2 * bv
    cols = {"qa": 0, "ka": a_width, "va": 2 * a_width, "qb": 3 * a_width, "kb": 3 * a_width + bk,
            "vb": 3 * a_width + 2 * bk, "rb": 3 * a_width + 2 * bk + bv, "ga": main, "gb": main + d,
            "lr": main + 2 * d}
    total = main + 2 * d + LANE
    assert total % MIX_BLOCK == 0
    return cols, main, total


def _mix_pieces(per, main, rank, d):
    out = []
    for lo, hi in ((0, main), (main + rank, main + rank + 2 * d), (main, main + rank)):
        while lo < hi:
            cut = min(hi, (lo // per + 1) * per)
            out.append((lo, cut))
            lo = cut
    return out


def mix_weight_in(g3, main, rank):
    d = g3.shape[2]
    flat = g3.reshape(-1, d)
    return jnp.concatenate([flat[:main], flat[main + rank:], flat[main:main + rank],
                            jnp.zeros((LANE - rank, d), g3.dtype)], axis=0)


def mix_weight_grad_out(gt, main, rank, per):
    d = gt.shape[1]
    blocks = [[] for _ in range(N_DEV)]
    pos = 0
    for lo, hi in _mix_pieces(per, main, rank, d):
        blocks[lo // per].append((lo, gt[pos:pos + hi - lo]))
        pos += hi - lo
    return jnp.stack([jnp.concatenate([x for _, x in sorted(b, key=lambda e: e[0])], axis=0) for b in blocks])


def ffn_forward(h, sh, sc, g, w_in3, w_out_of, ln_g, ln_b, tag):
    a, b, s = ffn_in(h, sh, sc, w_in3, f"{tag}_in")
    w_out = w_out_of(s)
    f, z, hout = out_ln(s, w_out, h, g, ln_g, ln_b, 0.5, f"{tag}_out")
    return hout, (h, a, b, s, f, z), w_out


def ffn_backward_weights(dh, saved, sh, sc, g, w_in3, w_out, ln_g, tag, target=None):
    hin, a, b, s, f, z = saved
    t, d = hin.shape
    nb, _, bw = w_in3.shape
    half = nb // 2
    fdim = w_out.shape[0]
    res = ln_bwd(dh, z, f, ln_g, g, 0.5, f"{tag}_ln_bwd", target=target)
    dz, df, dln_g, dln_b, dg = res[:5]
    dab = ffn_bwd_act(df, w_out, a, b, f"{tag}_act_bwd")
    tk = _row_tile(t, 512, 16)
    dw_out = matmul_tn(f"{tag}_dwout", s, (tk, bw), lambda n, k: (k, n), df, (tk, d), lambda n, k: (k, 0),
                       _sds((fdim, d), BF16), (bw, d), lambda n, k: (n, 0), fdim // bw)
    dw_in = matmul_tn(f"{tag}_dwin", hin, (tk, d), lambda n, k: (k, 0), dab, (None, tk, bw),
                      lambda n, k: (n // half, k, n % half), _sds((nb, d, bw), BF16), (None, d, bw),
                      lambda n, k: (n, 0, 0), nb, mod=(sh, sc))
    grads = dict(w_in=dw_in, w_out=dw_out.reshape(N_DEV, fdim // N_DEV, d), ln_g=dln_g, ln_b=dln_b, g=dg)
    return (dab, dz), grads, (res[5] if target is not None else None)


def ffn_backward_input(carry, saved, sc, w_in3, tag):
    dab, dz = carry
    hin = saved[0]
    t = hin.shape[0]
    nb, _, bw = w_in3.shape
    half = nb // 2
    tm = _row_tile(t, 512, 16)
    return matmul_nt_blocks(f"{tag}_du", dab, (None, tm, bw), lambda i, k: (k // half, i, k % half),
                            w_in3, t, resid=(dz, hin, sc))


def _after(v, token):
    return v if token is None else v + token[:1, :1]


def local_step(x, target, mod, weights_of, grads_ready, grads_sent, rel_bias, w_alpha2, b_alpha, gla_norm_g, lns):
    t, d = x.shape
    sh1, sc1, g1, sh2, sc2, g2, sh3, sc3, g3 = [mod[i:i + 1] for i in range(N_MOD)]
    ln1_g, ln1_b, ln2_g, ln2_b, ln3_g, ln3_b = lns
    n_heads_a, n_rel = rel_bias.shape
    rank, bk = w_alpha2.shape
    hv = gla_norm_g.shape[1]

    w1 = weights_of("ffn1_in", x)
    h1, saved1, w1["out"] = ffn_forward(x, sh1, sc1, g1, w1["in"], lambda s: weights_of("ffn1_out", s)["out"],
                                        ln1_g, ln1_b, "ffn1")
    wm = weights_of("mix", h1)
    a_width = wm["proj_a"].shape[1]
    bv = wm["proj_b"].shape[1]
    nh_b = bv // hv
    hk = bk // nh_b
    cols, main, total = mix_layout(d, a_width, bk, bv)
    w_mix = mix_weight_in(wm["in_t"], main, rank)
    p = mod_matmul(h1, sh2, sc2, w_mix, MIX_BLOCK, "mix_in")
    bias = bias_table(rel_bias)
    dh = a_width // n_heads_a
    assert PAIR * dh == LANE
    ya, lse = attn_fwd(p, cols, bias, dh, "attn_fwd")
    wa_pad = jnp.zeros((LANE, bk), BF16).at[:rank].set(w_alpha2.astype(BF16))
    o_b, yb, states = gla_fwd(p, cols, wa_pad, b_alpha, gla_norm_g, nh_b, hk, hv, "gla_fwd")
    pa, pb, merged = proj_merge(ya, yb, wm["proj_a"], wm["proj_b"], p, cols["ga"], cols["gb"], "proj_merge")
    m, z2, h2 = out_ln(merged, wm["out"], h1, g2, ln2_g, ln2_b, 1.0, "mix_out")
    w3 = weights_of("ffn2", h2)
    h3, saved3, _ = ffn_forward(h2, sh3, sc3, g3, w3["in"], lambda s: w3["out"], ln3_g, ln3_b, "ffn2")

    carry3, gr3, loss = ffn_backward_weights(h3, saved3, sh3, sc3, g3, w3["in"], w3["out"], ln3_g, "ffn2",
                                             target=target)
    token = grads_ready("ffn2", dict(ffn2_in=gr3["w_in"], ffn2_out=gr3["w_out"]))
    dh2, dsc3, dsh3 = ffn_backward_input(carry3, saved3, _after(sc3, token), w3["in"], "ffn2")
    token = grads_sent("ffn2", dh2)
    dz2, dm, dln2_g, dln2_b, dg2 = ln_bwd(dh2, z2, m, _after(ln2_g, token), g2, 1.0, "mix_ln_bwd")
    dpa, dpb, dga, dgb = merge_bwd(dm, wm["out"], p, cols["ga"], cols["gb"], pa, pb, "merge_bwd")
    tk = _row_tile(t, 512, 16)
    dw_mix_out = matmul_tn("mix_dwout", merged, (tk, 512), lambda n, k: (k, n), dm, (tk, d), lambda n, k: (k, 0),
                           _sds((d, d), BF16), (512, d), lambda n, k: (n, 0), d // 512)
    tm = _row_tile(t, 512, 16)
    pbw = wm["proj_a"].shape[2]
    dya = matmul_nt_blocks("proj_a_dy", dpa, (tm, pbw), lambda i, k: (i, k), wm["proj_a"], t)
    dyb = matmul_nt_blocks("proj_b_dy", dpb, (tm, pbw), lambda i, k: (i, k), wm["proj_b"], t)
    dw_pa = matmul_tn("proj_a_dw", ya, (tk, a_width), lambda n, k: (k, 0), dpa, (tk, pbw), lambda n, k: (k, n),
                      _sds((N_DEV, a_width, pbw), BF16), (None, a_width, pbw), lambda n, k: (n, 0, 0), N_DEV)
    dw_pb = matmul_tn("proj_b_dw", yb, (tk, bv), lambda n, k: (k, 0), dpb, (tk, pbw), lambda n, k: (k, n),
                      _sds((N_DEV, bv, pbw), BF16), (None, bv, pbw), lambda n, k: (n, 0, 0), N_DEV)
    dqb, dkb, dvb, drb, dgate, dgn = gla_bwd(p, cols, wa_pad, b_alpha, gla_norm_g, o_b, states, dyb,
                                             nh_b, hk, hv, "gla_bwd")
    dlr, dwa_pad, dba = gate_bwd(p, cols["lr"], dgate, wa_pad, "gate_bwd")
    dqa, delta, dbias = attn_bwd_q(p, cols, bias, lse, dya, dh, "attn_bwd_q")
    dka, dva = attn_bwd_kv(p, cols, bias, lse, delta, dya, dh, "attn_bwd_kv")
    d_rel = rel_bias_grad(bias_grad_skew(dbias), jnp.asarray(bias_clip_map(n_rel)), "rel_bias_grad")
    dp = jnp.concatenate([dqa, dka, dva, dqb, dkb, dvb, drb,
                          dga, dgb, dlr], axis=1)
    dw_mix_t = matmul_tn("mix_dwin", dp, (tk, MIX_BLOCK), lambda n, k: (k, n), h1, (tk, d), lambda n, k: (k, 0),
                         _sds((total, d), BF16), (MIX_BLOCK, d), lambda n, k: (n, 0), total // MIX_BLOCK,
                         mod=(sh2, sc2), mod_b=True)
    dw_mix_in = mix_weight_grad_out(dw_mix_t, main, rank, wm["in_t"].shape[1])
    token = grads_ready("mix", dict(mix_in=dw_mix_in, proj_a=dw_pa, proj_b=dw_pb,
                                    mix_out=dw_mix_out.reshape(N_DEV, d // N_DEV, d)))
    tm = _row_tile(t, 512, 16)
    dh1, dsc2, dsh2 = matmul_nt_blocks("mix_du", dp, (tm, MIX_BLOCK), lambda i, k: (i, k), w_mix, t,
                                       resid=(dz2, h1, _after(sc2, token)))
    token = grads_sent("mix", dh1)
    carry1, gr1, _ = ffn_backward_weights(dh1, saved1, sh1, sc1, g1, w1["in"], w1["out"], _after(ln1_g, token),
                                          "ffn1")
    token = grads_ready("ffn1", dict(ffn1_in=gr1["w_in"], ffn1_out=gr1["w_out"]))
    dx, dsc1, dsh1 = ffn_backward_input(carry1, saved1, _after(sc1, token), w1["in"], "ffn1")

    dmod = [dsh1, dsc1, gr1["g"], dsh2, dsc2, dg2, dsh3, dsc3, gr3["g"]]
    small = dict(ln1_g=gr1["ln_g"], ln1_b=gr1["ln_b"], ln2_g=dln2_g, ln2_b=dln2_b, ln3_g=gr3["ln_g"],
                 ln3_b=gr3["ln_b"], b_alpha=dba, gla_norm_g=dgn, w_alpha2=dwa_pad[:rank], rel_bias=d_rel)
    return loss, dx, dmod, small


GROUPS = dict(ffn1=("ffn1_in", "ffn1_out"), mix=("mix_in", "proj_a", "proj_b", "mix_out"),
              ffn2=("ffn2_in", "ffn2_out"))
GATHERS = dict(ffn1_in=("ffn1_in",), ffn1_out=("ffn1_out",), mix=GROUPS["mix"], ffn2=GROUPS["ffn2"])
SMALL_REPLICATED = ("b_ada", "ln1_g", "ln1_b", "ln2_g", "ln2_b", "ln3_g", "ln3_b", "b_alpha", "gla_norm_g")
SMALL_SHARDED = ("rel_bias", "w_alpha2")
WEIGHT_ORDER = ("w_ada", "b_ada", "ffn1_w_in", "ffn1_w_out", "ln1_g", "ln1_b", "w_mix_in", "rel_bias", "w_alpha2",
                "b_alpha", "gla_norm_g", "w_proj_a", "w_proj_b", "w_mix_out", "ln2_g", "ln2_b", "ffn2_w_in",
                "ffn2_w_out", "ln3_g", "ln3_b")
BIG_NAME = dict(ffn1_in="ffn1_w_in", ffn1_out="ffn1_w_out", mix_in="w_mix_in", proj_a="w_proj_a",
                proj_b="w_proj_b", mix_out="w_mix_out", ffn2_in="ffn2_w_in", ffn2_out="ffn2_w_out")


def kernel(x, c, w_ada, b_ada, ffn1_w_in, ffn1_w_out, ln1_g, ln1_b, w_mix_in, rel_bias, w_alpha2, b_alpha, gla_norm_g, w_proj_a, w_proj_b, w_mix_out, ln2_g, ln2_b, ffn2_w_in, ffn2_w_out, ln3_g, ln3_b, loss_target, m_w_ada, m_b_ada, m_ffn1_w_in, m_ffn1_w_out, m_ln1_g, m_ln1_b, m_w_mix_in, m_rel_bias, m_w_alpha2, m_b_alpha, m_gla_norm_g, m_w_proj_a, m_w_proj_b, m_w_mix_out, m_ln2_g, m_ln2_b, m_ffn2_w_in, m_ffn2_w_out, m_ln3_g, m_ln3_b, v_w_ada, v_b_ada, v_ffn1_w_in, v_ffn1_w_out, v_ln1_g, v_ln1_b, v_w_mix_in, v_rel_bias, v_w_alpha2, v_b_alpha, v_gla_norm_g, v_w_proj_a, v_w_proj_b, v_w_mix_out, v_ln2_g, v_ln2_b, v_ffn2_w_in, v_ffn2_w_out, v_ln3_g, v_ln3_b):
    env = dict(locals())
    w = {n: env[n] for n in WEIGHT_ORDER}
    mom = {n: env["m_" + n] for n in WEIGHT_ORDER}
    var = {n: env["v_" + n] for n in WEIGHT_ORDER}
    me = _me()
    dev = _lin(me)
    core = jnp.reshape(me[2], (1,)).astype(jnp.int32)
    chip = jnp.reshape(2 * me[0] + me[1], (1,)).astype(jnp.int32)
    d = x.shape[-1]

    def shard(n):
        s = w[BIG_NAME[n]][0].astype(BF16)
        return s.T if n == "mix_in" else s

    dev_idx = jnp.reshape(dev, (1,)).astype(jnp.int32)
    started = {}

    def start(grp, after):
        shards = [shard(n) for n in GATHERS[grp]]
        lands = [place_own(dev_idx, s, f"place_own_{n}") for n, s in zip(GATHERS[grp], shards)]
        started[grp] = gather_start(shards, lands, after, f"gather_start_{grp}")
        return started[grp][-1]

    first, *rest = list(GATHERS)
    order = start(first, x[0, :1, :1])

    ada_cols = w_ada.shape[-1]
    c_all = all_gather_small(_after(c, order), "gather_c")[:, 0, :]
    b_cols = lax.dynamic_slice_in_dim(b_ada, dev * ada_cols, ada_cols, axis=1)
    mod_cols = adaln_cols(c_all, w_ada[0], b_cols, "adaln_cols")
    mod_all = all_gather_small(mod_cols, "gather_mod")
    mod = lax.dynamic_index_in_dim(mod_all, dev, axis=1, keepdims=False).reshape(N_MOD, d)

    small_w = all_gather_small(_after(jnp.concatenate([rel_bias[0], w_alpha2[0]], axis=1), order), "gather_small_w")
    n_rel_cols = rel_bias.shape[-1]
    rel_full = small_w[:, :, :n_rel_cols].transpose(1, 0, 2).reshape(small_w.shape[1], -1)
    wa2_full = small_w[:, :, n_rel_cols:].transpose(1, 0, 2).reshape(small_w.shape[1], -1)

    order = small_w[0, :1, :1] + mod_all[0, :1, :1]
    for grp in rest:
        order = start(grp, order)
    mod = _after(mod, order)

    def weights_of(grp, after):
        _, zones = gather_wait(started[grp], after, f"gather_wait_{grp}")
        full = dict(zip(GATHERS[grp], gather_forward(zones, f"gather_forward_{grp}")))
        if grp == "mix":
            return dict(in_t=full["mix_in"], proj_a=full["proj_a"], proj_b=full["proj_b"],
                        out=full["mix_out"].reshape(-1, d))
        return {"in" if n.endswith("_in") else "out": v if n.endswith("_in") else v.reshape(-1, d)
                for n, v in full.items()}

    pairs, exchanges = {}, {}
    last = list(GROUPS)[0]

    def to_chips(grp, grads, got):
        sums = [pair_add(core, grads[n], g, f"grad_pair_add_{n}") for n, g in zip(GROUPS[grp], got)]
        exchanges[grp] = exchange_start(sums, chip_routes, f"grad_chip_start_{grp}")
        return exchanges[grp][-1]

    def grads_ready(grp, grads):
        if grp == last:
            return to_chips(grp, grads, pair_exchange([grads[n] for n in GROUPS[grp]], f"grad_pair_exchange_{grp}"))
        pairs[grp] = exchange_start([grads[n] for n in GROUPS[grp]], pair_routes, f"grad_pair_start_{grp}")
        return pairs[grp][-1]

    def grads_sent(grp, after):
        if grp == last:
            return None
        sent, got = exchange_wait(pairs[grp], pair_routes, after, f"grad_pair_wait_{grp}")
        return to_chips(grp, dict(zip(GROUPS[grp], sent)), got)

    lns = [ln1_g, ln1_b, ln2_g, ln2_b, ln3_g, ln3_b]
    loss, dx, dmod, small = local_step(x[0], loss_target[0], mod, weights_of, grads_ready, grads_sent, rel_full,
                                       wa2_full, b_alpha, gla_norm_g, lns)

    packed = jnp.concatenate([g.reshape(1, -1) for g in dmod]
                             + [small[n].reshape(1, -1) for n in SMALL_REPLICATED[1:] + SMALL_SHARDED]
                             + [loss], axis=1)
    parts = all_gather_small(packed, "gather_small_grads")
    loss = jnp.sum(parts[:, 0, packed.shape[1] - loss.shape[1]])
    n_mod = N_MOD * d
    dmod_all = parts[:, 0, :n_mod]
    g_w_ada = adaln_wgrad(c_all, lax.dynamic_slice_in_dim(dmod_all, dev * ada_cols, ada_cols, axis=1), "adaln_wgrad")
    out = {}
    out["w_ada"] =[o[None] for o in adamw_sum(g_w_ada[None], w_ada[0], m_w_ada[0], v_w_ada[0], "adamw_w_ada")]

    def pack(src):
        rows = [src[n].reshape(1, -1) for n in SMALL_REPLICATED]
        return jnp.concatenate(rows + [src[n].reshape(1, -1) for n in SMALL_SHARDED], axis=1)

    n_rep = sum(w[n].size for n in SMALL_REPLICATED)
    rep_parts = parts[:, :, :n_rep]
    off = n_rep
    shard_parts = []
    for n in SMALL_SHARDED:
        rows, cols_local = w[n].shape[1], w[n].shape[2]
        full_part = parts[:, 0, off:off + rows * cols_local * N_DEV].reshape(N_DEV, rows, cols_local * N_DEV)
        mine = lax.dynamic_slice_in_dim(full_part, dev * cols_local, cols_local, axis=2)
        shard_parts.append(mine.reshape(N_DEV, 1, rows * cols_local))
        off += rows * cols_local * N_DEV
    small_parts = jnp.concatenate([rep_parts] + shard_parts, axis=2)
    res = adamw_sum(small_parts, pack(w), pack(mom), pack(var), "adamw_small")
    off = 0
    for n in SMALL_REPLICATED + SMALL_SHARDED:
        size = w[n].size
        out[n] = [r[:, off:off + size].reshape(w[n].shape) for r in res]
        off += size

    order = res[0]
    for grp in reversed(list(GROUPS)):
        sums, recv = exchange_wait(exchanges[grp], chip_routes, order, f"grad_chip_wait_{grp}")
        for n, hsum, r in zip(GROUPS[grp], sums, recv):
            full = BIG_NAME[n]
            if n == "mix_in":
                g = owned_sum(chip, hsum, r, f"owned_sum_{n}").T
                res_n = adamw_sum(g[None], w[full][0], mom[full][0], var[full][0], f"adamw_{n}")
            else:
                res_n = adamw_owned(chip, hsum, r, w[full][0], mom[full][0], var[full][0], f"adamw_{n}")
            out[full] = [o[None] for o in res_n]
            order = res_n[0]

    flat = [loss, dx[None]]
    for k in range(4):
        flat += [out[n][k] for n in WEIGHT_ORDER]
    return tuple(flat)
```

```python
import functools

import numpy as np
import jax
import jax.numpy as jnp
from jax import lax
from jax.experimental import pallas as pl
from jax.experimental.pallas import tpu as pltpu

F32 = jnp.float32
BF16 = jnp.bfloat16
MESH = pl.DeviceIdType.MESH
N_DEV = 8
N_CHIP = 4

CHUNK = 64
A_PAST_CHUNKS = 8
REL_CLIP = 256
GATE_TAU = 16.0
N_MOD = 9
DEPTH = 1
ALPHA = (2.0 * DEPTH) ** 0.25
LN_EPS = 1e-5
RMS_EPS = 1e-6
ADAM_LR = 0.001
ADAM_B1 = 0.9
ADAM_B2 = 0.999
ADAM_EPS = 1e-08
ADAM_WD = 0.01
ADAM_STEP = 10

LANE = 128
VMEM_LIMIT = 56 * 2 ** 20
QB = 4 * CHUNK
KW = 3 * QB
GB = 8 * CHUNK
NEG = -1e30
HI = lax.Precision.HIGHEST

ANY = pl.BlockSpec(memory_space=pl.ANY)
VMEM_SPEC = pl.BlockSpec(memory_space=pltpu.VMEM)


def _params(*sem):
    return pltpu.CompilerParams(dimension_semantics=sem, vmem_limit_bytes=VMEM_LIMIT)


def _sds(shape, dtype):
    return jax.ShapeDtypeStruct(shape, dtype)


def _dot(a, b):
    return jnp.dot(a, b, preferred_element_type=F32)


def _dot_nt(a, b):
    return lax.dot_general(a, b, (((1,), (1,)), ((), ())), preferred_element_type=F32)


def _dot_tn(a, b):
    return lax.dot_general(a, b, (((0,), (0,)), ((), ())), preferred_element_type=F32)


def _sigmoid(x):
    return 0.5 * jnp.tanh(0.5 * x) + 0.5


def _colsum(x):
    return jnp.sum(x, axis=0, keepdims=True)


def _row_tile(rows, cap, mult):
    for t in range(min(rows, cap), 0, -1):
        if rows % t == 0 and t % mult == 0:
            return t
    return rows


def _me():
    return lax.axis_index("x"), lax.axis_index("y"), lax.axis_index("c")


def _flip(me, k):
    return tuple((1 - p) if (k >> s) & 1 else p for p, s in zip(me, (2, 1, 0)))


def _lin(p):
    return 4 * p[0] + 2 * p[1] + p[2]


def _gather_direct(x_ref, out_ref, send_sems, recv_sems, local_sem):
    me = _me()
    mine = pltpu.make_async_copy(x_ref, out_ref.at[_lin(me)], local_sem)
    mine.start()
    sends = []
    for k in range(1, N_DEV):
        cp = pltpu.make_async_remote_copy(
            src_ref=x_ref, dst_ref=out_ref.at[_lin(me)], send_sem=send_sems.at[k - 1],
            recv_sem=recv_sems.at[k - 1], device_id=_flip(me, k), device_id_type=MESH)
        cp.start()
        sends.append(cp)
    for k in range(1, N_DEV):
        peer = _flip(me, k)
        pltpu.make_async_remote_copy(
            src_ref=x_ref, dst_ref=out_ref.at[_lin(peer)], send_sem=send_sems.at[k - 1],
            recv_sem=recv_sems.at[k - 1], device_id=peer, device_id_type=MESH).wait_recv()
    for cp in sends:
        cp.wait_send()
    mine.wait()


GATHER_SEMS = [pltpu.SemaphoreType.DMA((N_DEV - 1,)), pltpu.SemaphoreType.DMA((N_DEV - 1,)), pltpu.SemaphoreType.DMA]


def all_gather_small(x, name):
    r, n = x.shape
    return pl.pallas_call(
        _gather_direct_body(), name=name, out_shape=_sds((N_DEV, r, n), x.dtype),
        in_specs=[VMEM_SPEC], out_specs=VMEM_SPEC, scratch_shapes=GATHER_SEMS,
    )(x)


def _gather_direct_body():
    def body(x_ref, out_ref, send_sems, recv_sems, local_sem):
        _gather_direct(x_ref, out_ref, send_sems, recv_sems, local_sem)
    return body


def all_gather_rows(pieces, name):
    sizes = [x.shape[1] for x in pieces]
    total = sum(sizes)
    assert all(n % LANE == 0 for n in sizes)

    def body(*refs):
        ins = refs[:len(pieces)]
        out_ref, row, send_sems, recv_sems, local_sem = refs[len(pieces):]
        off = 0
        for x_ref, n in zip(ins, sizes):
            row[:, off:off + n] = x_ref[...]
            off += n
        _gather_direct(row, out_ref, send_sems, recv_sems, local_sem)

    return pl.pallas_call(
        body, name=name, out_shape=_sds((N_DEV, 1, total), F32),
        in_specs=[VMEM_SPEC] * len(pieces), out_specs=VMEM_SPEC,
        scratch_shapes=[pltpu.VMEM((1, total), F32)] + GATHER_SEMS,
    )(*pieces)


HBM_SPEC = pl.BlockSpec(memory_space=pltpu.HBM)
SEM_SPEC = pl.BlockSpec(memory_space=pltpu.SEMAPHORE)
EFFECT = pltpu.SideEffectType.DATAFLOW_SIDE_EFFECTING
FIRST = N_CHIP


def _hbm(v):
    return pltpu.with_memory_space_constraint(v, pltpu.HBM)


def _other_chips(mx, my):
    return [(1 - mx, my), (mx, 1 - my), (1 - mx, 1 - my)]


def place_own(dev, shard, name):
    rows, cols = shard.shape
    tr, tc = _tile2(rows, cols, 16)

    def body(dev_ref, s_ref, land_ref, o_ref):
        o_ref[...] = s_ref[...]

    land = lax.empty((N_DEV, rows, cols), shard.dtype)
    return pl.pallas_call(
        body, name=name, out_shape=_sds(land.shape, land.dtype),
        grid_spec=pltpu.PrefetchScalarGridSpec(
            num_scalar_prefetch=1, grid=(rows // tr, cols // tc),
            in_specs=[pl.BlockSpec((tr, tc), lambda i, j, d: (i, j)), ANY],
            out_specs=pl.BlockSpec((None, tr, tc), lambda i, j, d: (d[0], i, j))),
        input_output_aliases={2: 0}, compiler_params=_params("parallel", "parallel"),
    )(dev, shard, land)


def gather_start(shards, lands, after, name):
    n = len(shards)

    def body(*refs):
        ins, zones = refs[:n], refs[n:2 * n]
        send_sems, recv_sems = refs[2 * n + 1], refs[2 * n + 2]
        token = refs[-1]
        me = _me()
        mx, my, mc = me
        for a in range(n):
            dst = zones[a].at[_lin(me)]
            targets = [(mx, my, 1 - mc)] + [(*chip, mc) for chip in _other_chips(mx, my)]
            for k, to in enumerate(targets):
                pltpu.make_async_remote_copy(
                    src_ref=ins[a], dst_ref=dst, send_sem=send_sems.at[a * FIRST + k],
                    recv_sem=recv_sems.at[a * FIRST + k], device_id=to, device_id_type=MESH).start()
        token[...] = jnp.zeros_like(token)

    sems = pltpu.SemaphoreType.DMA((n * FIRST,))
    out = pl.pallas_call(
        body, name=name,
        out_shape=(sems, sems, *[pltpu.HBM(s.shape, s.dtype) for s in shards],
                   *[pltpu.HBM(z.shape, z.dtype) for z in lands], _sds((8, LANE), F32)),
        in_specs=[HBM_SPEC] * (2 * n) + [ANY],
        out_specs=(SEM_SPEC, SEM_SPEC, *[HBM_SPEC] * (2 * n), VMEM_SPEC),
        input_output_aliases={a: 2 + a for a in range(2 * n)},
        compiler_params=pltpu.CompilerParams(has_side_effects=EFFECT),
    )(*[_hbm(s) for s in shards], *[_hbm(z) for z in lands], after)
    return out[0], out[1], out[2:2 + n], out[2 + n:2 + 2 * n], out[-1]


def gather_wait(started, after, name):
    send_sems, recv_sems, shards, lands, _ = started
    n = len(shards)

    def body(*refs):
        ins, zones = refs[:n], refs[n:2 * n]
        send_ref, recv_ref = refs[2 * n], refs[2 * n + 1]
        mx, my, mc = _me()
        for a in range(n):
            for k in range(FIRST):
                cp = pltpu.make_async_remote_copy(
                    src_ref=ins[a], dst_ref=zones[a].at[0], send_sem=send_ref.at[a * FIRST + k],
                    recv_sem=recv_ref.at[a * FIRST + k], device_id=(mx, my, 1 - mc), device_id_type=MESH)
                cp.wait_send()
                cp.wait_recv()

    out = pl.pallas_call(
        body, name=name,
        out_shape=(*[pltpu.HBM(s.shape, s.dtype) for s in shards], *[pltpu.HBM(z.shape, z.dtype) for z in lands]),
        in_specs=[HBM_SPEC] * (2 * n) + [SEM_SPEC, SEM_SPEC, ANY], out_specs=tuple([HBM_SPEC] * (2 * n)),
        input_output_aliases={a: a for a in range(2 * n)},
        compiler_params=pltpu.CompilerParams(has_side_effects=EFFECT),
    )(*shards, *lands, send_sems, recv_sems, after)
    return out[:n], out[n:]


def gather_forward(lands, name):
    n = len(lands)
    rel = N_CHIP - 1

    def body(*refs):
        zones, outs = refs[:n], refs[n:2 * n]
        send_sems, recv_sems = refs[2 * n:]
        mx, my, mc = _me()
        chips = _other_chips(mx, my)

        def copy(a, j, core):
            blk = _lin((*chips[j], core))
            return pltpu.make_async_remote_copy(
                src_ref=zones[a].at[blk], dst_ref=outs[a].at[blk], send_sem=send_sems.at[a * rel + j],
                recv_sem=recv_sems.at[a * rel + j], device_id=(mx, my, 1 - mc), device_id_type=MESH)

        sends = [copy(a, j, mc) for a in range(n) for j in range(rel)]
        for cp in sends:
            cp.start()
        for a in range(n):
            for j in range(rel):
                copy(a, j, 1 - mc).wait_recv()
        for cp in sends:
            cp.wait_send()

    return pl.pallas_call(
        body, name=name, out_shape=[_sds(z.shape, z.dtype) for z in lands],
        in_specs=[ANY] * n, out_specs=[ANY] * n, input_output_aliases={a: a for a in range(n)},
        scratch_shapes=[pltpu.SemaphoreType.DMA((n * rel,)), pltpu.SemaphoreType.DMA((n * rel,))],
    )(*lands)


def pair_exchange(gs, name):
    n = len(gs)

    def body(*refs):
        ins, outs = refs[:n], refs[n:2 * n]
        send_sems, recv_sems = refs[2 * n:]
        mx, my, mc = _me()
        cps = []
        for a in range(n):
            for q in range(N_CHIP):
                cp = pltpu.make_async_remote_copy(
                    src_ref=ins[a].at[2 * q + (1 - mc)], dst_ref=outs[a].at[q],
                    send_sem=send_sems.at[a * N_CHIP + q], recv_sem=recv_sems.at[a * N_CHIP + q],
                    device_id=(mx, my, 1 - mc), device_id_type=MESH)
                cp.start()
                cps.append(cp)
        for cp in cps:
            cp.wait()

    return pl.pallas_call(
        body, name=name, out_shape=[_sds((N_CHIP,) + g.shape[1:], g.dtype) for g in gs],
        in_specs=[ANY] * n, out_specs=[ANY] * n,
        scratch_shapes=[pltpu.SemaphoreType.DMA((n * N_CHIP,)), pltpu.SemaphoreType.DMA((n * N_CHIP,))],
    )(*gs)


def _tile2(rows, cols, row_mult):
    tr = _row_tile(rows, 512, row_mult)
    if tr < rows or rows * cols <= 2 ** 20:
        return tr, cols
    return rows, _row_tile(cols, 512, LANE)


def pair_add(core, g, got, name):
    _, rows, cols = g.shape
    tr, tc = _tile2(rows, cols, 16)

    def body(core_ref, g_ref, got_ref, h_ref):
        h_ref[...] = (g_ref[...].astype(F32) + got_ref[...].astype(F32)).astype(h_ref.dtype)

    blk = pl.BlockSpec((None, tr, tc), lambda q, i, j, c: (q, i, j))
    return pl.pallas_call(
        body, name=name, out_shape=_sds((N_CHIP, rows, cols), g.dtype),
        grid_spec=pltpu.PrefetchScalarGridSpec(
            num_scalar_prefetch=1, grid=(N_CHIP, rows // tr, cols // tc),
            in_specs=[pl.BlockSpec((None, tr, tc), lambda q, i, j, c: (2 * q + c[0], i, j)), blk],
            out_specs=blk),
        compiler_params=_params("parallel", "parallel", "parallel"),
    )(core, g, got)


def chip_routes(mx, my, mc):
    return [(2 * px + py, k, (px, py, mc)) for k, (px, py) in enumerate(_other_chips(mx, my))]


def pair_routes(mx, my, mc):
    return [(2 * q + (1 - mc), q, (mx, my, 1 - mc)) for q in range(N_CHIP)]


def exchange_start(hs, routes, name):
    n = len(hs)
    rel = len(routes(0, 0, 0))
    lands = [lax.empty((rel,) + h.shape[1:], h.dtype) for h in hs]

    def body(*refs):
        ins, zones = refs[:n], refs[n:2 * n]
        send_sems, recv_sems = refs[2 * n], refs[2 * n + 1]
        token = refs[-1]
        for a in range(n):
            for k, (src, slot, to) in enumerate(routes(*_me())):
                pltpu.make_async_remote_copy(
                    src_ref=ins[a].at[src], dst_ref=zones[a].at[slot], send_sem=send_sems.at[a * rel + k],
                    recv_sem=recv_sems.at[a * rel + k], device_id=to, device_id_type=MESH).start()
        token[...] = jnp.zeros_like(token)

    sems = pltpu.SemaphoreType.DMA((n * rel,))
    out = pl.pallas_call(
        body, name=name,
        out_shape=(sems, sems, *[pltpu.HBM(h.shape, h.dtype) for h in hs],
                   *[pltpu.HBM(z.shape, z.dtype) for z in lands], _sds((8, LANE), F32)),
        in_specs=[HBM_SPEC] * (2 * n), out_specs=(SEM_SPEC, SEM_SPEC, *[HBM_SPEC] * (2 * n), VMEM_SPEC),
        input_output_aliases={a: 2 + a for a in range(2 * n)},
        compiler_params=pltpu.CompilerParams(has_side_effects=EFFECT),
    )(*[_hbm(h) for h in hs], *[_hbm(z) for z in lands])
    return out[0], out[1], out[2:2 + n], out[2 + n:2 + 2 * n], out[-1]


def exchange_wait(started, routes, after, name):
    send_sems, recv_sems, hs, lands, _ = started
    n = len(hs)
    rel = len(routes(0, 0, 0))

    def body(*refs):
        ins, zones = refs[:n], refs[n:2 * n]
        send_ref, recv_ref = refs[2 * n], refs[2 * n + 1]
        for a in range(n):
            for k, (src, slot, to) in enumerate(routes(*_me())):
                cp = pltpu.make_async_remote_copy(
                    src_ref=ins[a].at[src], dst_ref=zones[a].at[slot], send_sem=send_ref.at[a * rel + k],
                    recv_sem=recv_ref.at[a * rel + k], device_id=to, device_id_type=MESH)
                cp.wait_send()
                cp.wait_recv()

    out = pl.pallas_call(
        body, name=name,
        out_shape=(*[pltpu.HBM(h.shape, h.dtype) for h in hs], *[pltpu.HBM(z.shape, z.dtype) for z in lands]),
        in_specs=[HBM_SPEC] * (2 * n) + [SEM_SPEC, SEM_SPEC, ANY], out_specs=tuple([HBM_SPEC] * (2 * n)),
        input_output_aliases={a: a for a in range(2 * n)},
        compiler_params=pltpu.CompilerParams(has_side_effects=EFFECT),
    )(*hs, *lands, send_sems, recv_sems, after)
    return out[:n], out[n:]


def _adam(w, g, m, v):
    m = ADAM_B1 * m + (1.0 - ADAM_B1) * g
    v = ADAM_B2 * v + (1.0 - ADAM_B2) * (g * g)
    m_hat = m / (1.0 - ADAM_B1 ** ADAM_STEP)
    v_hat = v / (1.0 - ADAM_B2 ** ADAM_STEP)
    delta = -ADAM_LR * (m_hat / (jnp.sqrt(v_hat) + ADAM_EPS) + ADAM_WD * w)
    return delta, m, v


def adamw_owned(chip, h, got, w, m, v, name):
    rows, cols = w.shape
    tr = _row_tile(rows, 256, 16)

    def body(chip_ref, h_ref, got_ref, w_ref, m_ref, v_ref, g_out, d_out, m_out, v_out):
        g = h_ref[...].astype(F32)
        for k in range(N_CHIP - 1):
            g = g + got_ref[k].astype(F32)
        d, mn, vn = _adam(w_ref[...], g, m_ref[...], v_ref[...])
        g_out[...] = g
        d_out[...] = d
        m_out[...] = mn
        v_out[...] = vn

    blk = pl.BlockSpec((tr, cols), lambda i, c: (i, 0))
    return pl.pallas_call(
        body, name=name, out_shape=[_sds((rows, cols), F32)] * 4,
        grid_spec=pltpu.PrefetchScalarGridSpec(
            num_scalar_prefetch=1, grid=(rows // tr,),
            in_specs=[pl.BlockSpec((None, tr, cols), lambda i, c: (c[0], i, 0)),
                      pl.BlockSpec((N_CHIP - 1, tr, cols), lambda i, c: (0, i, 0)), blk, blk, blk],
            out_specs=[blk] * 4),
        compiler_params=_params("parallel"),
    )(chip, h, got, w, m, v)


def owned_sum(chip, h, got, name):
    _, rows, cols = h.shape
    tr, tc = _tile2(rows, cols, 16)

    def body(chip_ref, h_ref, got_ref, g_out):
        g = h_ref[...].astype(F32)
        for k in range(N_CHIP - 1):
            g = g + got_ref[k].astype(F32)
        g_out[...] = g

    return pl.pallas_call(
        body, name=name, out_shape=_sds((rows, cols), F32),
        grid_spec=pltpu.PrefetchScalarGridSpec(
            num_scalar_prefetch=1, grid=(rows // tr, cols // tc),
            in_specs=[pl.BlockSpec((None, tr, tc), lambda i, j, c: (c[0], i, j)),
                      pl.BlockSpec((N_CHIP - 1, tr, tc), lambda i, j, c: (0, i, j))],
            out_specs=pl.BlockSpec((tr, tc), lambda i, j, c: (i, j))),
        compiler_params=_params("parallel", "parallel"),
    )(chip, h, got)


def adamw_rows(sources, where, ws, ms, vs, name):
    n_src, n_par = len(sources), len(ws)

    def body(*refs):
        srcs = refs[:n_src]
        w_refs, m_refs, v_refs = (refs[n_src + j * n_par:n_src + (j + 1) * n_par] for j in range(3))
        outs = refs[n_src + 3 * n_par:]
        for k in range(n_par):
            src, off = srcs[where[k][0]], where[k][1]
            n = w_refs[k].shape[1]
            g = src[0, :, off:off + n]
            for dev in range(1, N_DEV):
                g = g + src[dev, :, off:off + n]
            d, mn, vn = _adam(w_refs[k][...], g, m_refs[k][...], v_refs[k][...])
            for o_ref, val in zip(outs[4 * k:4 * k + 4], (g, d, mn, vn)):
                o_ref[...] = val

    flat = pl.pallas_call(
        body, name=name, out_shape=[_sds(x.shape, F32) for x in ws for _ in range(4)],
        in_specs=[VMEM_SPEC] * (n_src + 3 * n_par), out_specs=[VMEM_SPEC] * (4 * n_par),
        compiler_params=pltpu.CompilerParams(vmem_limit_bytes=VMEM_LIMIT),
    )(*sources, *ws, *ms, *vs)
    return [flat[4 * k:4 * k + 4] for k in range(n_par)]


def adamw_sum(parts, w, m, v, name):
    n_parts, rows, cols = parts.shape
    tr = _row_tile(rows, 256, 8)

    def body(p_ref, w_ref, m_ref, v_ref, g_out, d_out, m_out, v_out):
        g = p_ref[0]
        for k in range(1, n_parts):
            g = g + p_ref[k]
        d, mn, vn = _adam(w_ref[...], g, m_ref[...], v_ref[...])
        g_out[...] = g
        d_out[...] = d
        m_out[...] = mn
        v_out[...] = vn

    blk = pl.BlockSpec((tr, cols), lambda i: (i, 0))
    return pl.pallas_call(
        body, name=name, out_shape=[_sds((rows, cols), F32)] * 4, grid=(rows // tr,),
        in_specs=[pl.BlockSpec((n_parts, tr, cols), lambda i: (0, i, 0)), blk, blk, blk],
        out_specs=[blk] * 4, compiler_params=_params("parallel"),
    )(parts, w, m, v)


def adaln_cols(c_all, w, b, name):
    d, n = w.shape
    tn = _row_tile(n, 768, LANE)

    def body(c_ref, w_ref, b_ref, o_ref):
        c = c_ref[...]
        o_ref[...] = jnp.dot(c * _sigmoid(c), w_ref[...], preferred_element_type=F32, precision=HI) + b_ref[...]

    return pl.pallas_call(
        body, name=name, out_shape=_sds((N_DEV, n), F32), grid=(n // tn,),
        in_specs=[pl.BlockSpec((N_DEV, d), lambda j: (0, 0)), pl.BlockSpec((d, tn), lambda j: (0, j)),
                  pl.BlockSpec((1, tn), lambda j: (0, j))],
        out_specs=pl.BlockSpec((N_DEV, tn), lambda j: (0, j)), compiler_params=_params("parallel"),
    )(c_all, w, b)


def adaln_wgrad(c_all, dmod_cols, name):
    d = c_all.shape[1]
    n = dmod_cols.shape[1]
    tn = _row_tile(n, 768, LANE)

    def body(c_ref, g_ref, o_ref):
        c = c_ref[...]
        o_ref[...] = lax.dot_general(c * _sigmoid(c), g_ref[...], (((0,), (0,)), ((), ())),
                                     preferred_element_type=F32, precision=HI)

    return pl.pallas_call(
        body, name=name, out_shape=_sds((d, n), F32), grid=(n // tn,),
        in_specs=[pl.BlockSpec((N_DEV, d), lambda j: (0, 0)), pl.BlockSpec((N_DEV, tn), lambda j: (0, j))],
        out_specs=pl.BlockSpec((d, tn), lambda j: (0, j)), compiler_params=_params("parallel"),
    )(c_all, dmod_cols)


def _modulate(h, sh, sc):
    return (h * (1.0 + sc) + sh).astype(BF16)


def ffn_in(h, sh, sc, w3, name):
    t, d = h.shape
    nb, _, bw = w3.shape
    half = nb // 2
    tm = _row_tile(t, 256, 16)

    def body(h_ref, sh_ref, sc_ref, wa_ref, wb_ref, a_ref, b_ref, s_ref):
        u = _modulate(h_ref[...], sh_ref[...], sc_ref[...])
        a = _dot(u, wa_ref[...])
        b = _dot(u, wb_ref[...])
        a_ref[...] = a.astype(BF16)
        b_ref[...] = b.astype(BF16)
        s_ref[...] = (a * _sigmoid(a) * b).astype(BF16)

    vec = pl.BlockSpec((1, d), lambda j, i: (0, 0))
    out = pl.BlockSpec((tm, bw), lambda j, i: (i, j))
    return pl.pallas_call(
        body, name=name, out_shape=[_sds((t, half * bw), BF16)] * 3, grid=(half, t // tm),
        in_specs=[pl.BlockSpec((tm, d), lambda j, i: (i, 0)), vec, vec,
                  pl.BlockSpec((None, d, bw), lambda j, i: (j, 0, 0)),
                  pl.BlockSpec((None, d, bw), lambda j, i: (j + half, 0, 0))],
        out_specs=[out] * 3, compiler_params=_params("parallel", "parallel"),
    )(h, sh, sc, w3, w3)


def mod_matmul(h, sh, sc, wt, bw, name):
    t, d = h.shape
    n = wt.shape[0]
    tm = _row_tile(t, 256, 16)

    def body(h_ref, sh_ref, sc_ref, w_ref, o_ref):
        o_ref[...] = _dot_nt(_modulate(h_ref[...], sh_ref[...], sc_ref[...]), w_ref[...]).astype(BF16)

    vec = pl.BlockSpec((1, d), lambda j, i: (0, 0))
    return pl.pallas_call(
        body, name=name, out_shape=_sds((t, n), BF16), grid=(n // bw, t // tm),
        in_specs=[pl.BlockSpec((tm, d), lambda j, i: (i, 0)), vec, vec, pl.BlockSpec((bw, d), lambda j, i: (j, 0))],
        out_specs=pl.BlockSpec((tm, bw), lambda j, i: (i, j)), compiler_params=_params("parallel", "parallel"),
    )(h, sh, sc, wt)


def out_ln(s, w, hin, gmod, ln_g, ln_b, coef, name):
    t, kdim = s.shape
    d = w.shape[1]
    tm = _row_tile(t, 256, 16)

    def body(s_ref, w_ref, hin_ref, gm_ref, g_ref, b_ref, f_ref, z_ref, h_ref):
        f = _dot(s_ref[...], w_ref[...])
        z = ALPHA * hin_ref[...] + (coef * gm_ref[...]) * f
        mu = jnp.mean(z, axis=-1, keepdims=True)
        zc = z - mu
        var = jnp.mean(zc * zc, axis=-1, keepdims=True)
        f_ref[...] = f.astype(BF16)
        z_ref[...] = z
        h_ref[...] = zc * lax.rsqrt(var + LN_EPS) * g_ref[...] + b_ref[...]

    vec = pl.BlockSpec((1, d), lambda i: (0, 0))
    row = pl.BlockSpec((tm, d), lambda i: (i, 0))
    return pl.pallas_call(
        body, name=name, out_shape=[_sds((t, d), BF16), _sds((t, d), F32), _sds((t, d), F32)],
        grid=(t // tm,),
        in_specs=[pl.BlockSpec((tm, kdim), lambda i: (i, 0)),
                  pl.BlockSpec((kdim, d), lambda i: (0, 0), pipeline_mode=pl.Buffered(1)), row, vec, vec, vec],
        out_specs=[row, row, row], compiler_params=_params("parallel"),
    )(s, w, hin, gmod, ln_g, ln_b)


def ln_bwd(dh, z, f, ln_g, gmod, coef, name, target=None):
    t, d = z.shape
    tm = _row_tile(t, 256, 16)
    head = target is not None

    def body(*refs):
        if head:
            dh_ref, tg_ref, z_ref, f_ref, g_ref, gm_ref, dz_ref, df_ref, dg_ref, db_ref, dgm_ref, loss_ref = refs
        else:
            dh_ref, z_ref, f_ref, g_ref, gm_ref, dz_ref, df_ref, dg_ref, db_ref, dgm_ref = refs
        i = pl.program_id(0)

        @pl.when(i == 0)
        def _():
            dg_ref[...] = jnp.zeros_like(dg_ref)
            db_ref[...] = jnp.zeros_like(db_ref)
            dgm_ref[...] = jnp.zeros_like(dgm_ref)
            if head:
                loss_ref[...] = jnp.zeros_like(loss_ref)

        dh = dh_ref[...]
        if head:
            err = dh - tg_ref[...]
            loss_ref[...] += 0.5 * jnp.sum(jnp.mean(err * err, axis=-1, keepdims=True))
            dh = err / d
        zv = z_ref[...]
        mu = jnp.mean(zv, axis=-1, keepdims=True)
        zc = zv - mu
        rstd = lax.rsqrt(jnp.mean(zc * zc, axis=-1, keepdims=True) + LN_EPS)
        xhat = zc * rstd
        dxh = dh * g_ref[...]
        dz = rstd * (dxh - jnp.mean(dxh, axis=-1, keepdims=True)
                     - xhat * jnp.mean(dxh * xhat, axis=-1, keepdims=True))
        dz_ref[...] = dz
        df_ref[...] = ((coef * gm_ref[...]) * dz).astype(BF16)
        dg_ref[...] += _colsum(dh * xhat)
        db_ref[...] += _colsum(dh)
        dgm_ref[...] += _colsum(coef * f_ref[...].astype(F32) * dz)

    vec = pl.BlockSpec((1, d), lambda i: (0, 0))
    row = pl.BlockSpec((tm, d), lambda i: (i, 0))
    ins = [dh] + ([target] if head else []) + [z, f, ln_g, gmod]
    in_specs = [row] + ([row] if head else []) + [row, row, vec, vec]
    out_shape = [_sds((t, d), F32), _sds((t, d), BF16)] + [_sds((1, d), F32)] * 3
    out_specs = [row, row, vec, vec, vec]
    if head:
        out_shape.append(_sds((1, LANE), F32))
        out_specs.append(pl.BlockSpec((1, LANE), lambda i: (0, 0)))
    return pl.pallas_call(
        body, name=name, out_shape=out_shape, grid=(t // tm,), in_specs=in_specs, out_specs=out_specs,
        compiler_params=_params("arbitrary"),
    )(*ins)


def ffn_bwd_act(df, w, a, b, name):
    t, d = df.shape
    fdim = w.shape[0]
    bw = fdim // (N_DEV // 2)
    tm = _row_tile(t, 512, 16)

    def body(df_ref, w_ref, a_ref, b_ref, o_ref):
        ds = _dot_nt(df_ref[...], w_ref[...])
        av = a_ref[...].astype(F32)
        sg = _sigmoid(av)
        o_ref[0] = (ds * b_ref[...].astype(F32) * (sg * (1.0 + av * (1.0 - sg)))).astype(BF16)
        o_ref[1] = (ds * (av * sg)).astype(BF16)

    act = pl.BlockSpec((tm, bw), lambda j, i: (i, j))
    return pl.pallas_call(
        body, name=name, out_shape=_sds((2, t, fdim), BF16), grid=(fdim // bw, t // tm),
        in_specs=[pl.BlockSpec((tm, d), lambda j, i: (i, 0)), pl.BlockSpec((bw, d), lambda j, i: (j, 0)), act, act],
        out_specs=pl.BlockSpec((2, tm, bw), lambda j, i: (0, i, j)),
        compiler_params=_params("parallel", "parallel"),
    )(df, w, a, b)


def matmul_tn(name, a, a_block, a_map, b, b_block, b_map, out_shape, o_block, o_map, n_out, mod=None,
              mod_b=False):
    tk = [s for s in a_block if s is not None][0]
    nk = a.shape[-2] // tk
    m, nn = [s for s in o_block if s is not None]

    def body(*refs):
        if mod is None:
            a_ref, b_ref, o_ref, acc = refs
        else:
            a_ref, sh_ref, sc_ref, b_ref, o_ref, acc = refs
        k = pl.program_id(1)

        @pl.when(k == 0)
        def _():
            acc[...] = jnp.zeros_like(acc)

        av, bv = a_ref[...], b_ref[...]
        if mod is not None and mod_b:
            bv = _modulate(bv, sh_ref[...], sc_ref[...])
        elif mod is not None:
            av = _modulate(av, sh_ref[...], sc_ref[...])
        acc[...] += _dot_tn(av, bv)

        @pl.when(k == nk - 1)
        def _():
            o_ref[...] = acc[...].astype(o_ref.dtype)

    ins = [a] + (list(mod) if mod is not None else []) + [b]
    in_specs = [pl.BlockSpec(a_block, a_map)]
    if mod is not None:
        vec = pl.BlockSpec((1, mod[0].shape[1]), lambda n, k: (0, 0))
        in_specs += [vec, vec]
    in_specs.append(pl.BlockSpec(b_block, b_map))
    return pl.pallas_call(
        body, name=name, out_shape=out_shape, grid=(n_out, nk), in_specs=in_specs,
        out_specs=pl.BlockSpec(o_block, o_map), scratch_shapes=[pltpu.VMEM((m, nn), F32)],
        compiler_params=_params("parallel", "arbitrary"),
    )(*ins)


def matmul_nt_blocks(name, dy, dy_block, dy_map, w3, t, resid=None):
    tm, bw = [s for s in dy_block if s is not None]
    rows = w3.ndim == 2
    if rows:
        nk, n = w3.shape[0] // bw, w3.shape[1]
    else:
        nk, n, _ = w3.shape

    def body(*refs):
        if resid is None:
            dy_ref, w_ref, o_ref, acc = refs
        else:
            dy_ref, w_ref, dz_ref, hin_ref, sc_ref, o_ref, dsc_ref, dsh_ref, acc = refs
        i, k = pl.program_id(0), pl.program_id(1)

        @pl.when(k == 0)
        def _():
            acc[...] = jnp.zeros_like(acc)

        if resid is not None:
            @pl.when((k == 0) & (i == 0))
            def _():
                dsc_ref[...] = jnp.zeros_like(dsc_ref)
                dsh_ref[...] = jnp.zeros_like(dsh_ref)

        acc[...] += _dot(dy_ref[...], w_ref[...]) if rows else _dot_nt(dy_ref[...], w_ref[...])

        @pl.when(k == nk - 1)
        def _():
            du = acc[...]
            if resid is None:
                o_ref[...] = du.astype(o_ref.dtype)
            else:
                o_ref[...] = ALPHA * dz_ref[...] + du * (1.0 + sc_ref[...])
                dsc_ref[...] += _colsum(du * hin_ref[...])
                dsh_ref[...] += _colsum(du)

    row = pl.BlockSpec((tm, n), lambda i, k: (i, 0))
    vec = pl.BlockSpec((1, n), lambda i, k: (0, 0))
    w_spec = pl.BlockSpec((bw, n), lambda i, k: (k, 0)) if rows else pl.BlockSpec((None, n, bw), lambda i, k: (k, 0, 0))
    in_specs = [pl.BlockSpec(dy_block, dy_map), w_spec]
    ins = [dy, w3]
    if resid is None:
        out_shape, out_specs = _sds((t, n), BF16), row
    else:
        ins += list(resid)
        once = pl.BlockSpec((tm, n), lambda i, k: (i, 0), pipeline_mode=pl.Buffered(1))
        in_specs += [once, once, vec]
        out_shape = [_sds((t, n), F32), _sds((1, n), F32), _sds((1, n), F32)]
        out_specs = [row, vec, vec]
    return pl.pallas_call(
        body, name=name, out_shape=out_shape, grid=(t // tm, nk), in_specs=in_specs, out_specs=out_specs,
        scratch_shapes=[pltpu.VMEM((tm, n), F32)], compiler_params=_params("arbitrary", "arbitrary"),
    )(*ins)


REL_W = KW + QB


def bias_table(rel_bias):
    nh, n_rel = rel_bias.shape
    lo = KW - QB - REL_CLIP
    hi = KW - lo - n_rel
    assert n_rel == REL_CLIP + CHUNK and lo >= 0 and hi >= 0
    first, last = rel_bias[:, :1], rel_bias[:, -1:]
    row = jnp.concatenate([jnp.broadcast_to(first, (nh, lo)), rel_bias, jnp.broadcast_to(last, (nh, hi)),
                           jnp.broadcast_to(first, (nh, QB))], axis=1)
    table = jnp.tile(row, (1, QB))[:, :QB * (REL_W - 1)].reshape(nh, QB, REL_W - 1)[:, :, :KW]
    q = np.arange(QB)[:, None] // CHUNK
    k = np.arange(KW)[None, :] // CHUNK
    band = (k >= q) & (k <= q + A_PAST_CHUNKS)
    return jnp.where(band[None], table, NEG)


def bias_grad_skew(dbias):
    nh = dbias.shape[0]
    flat = jnp.pad(dbias, ((0, 0), (0, 0), (0, REL_W - 1 - KW))).reshape(nh, QB * (REL_W - 1))
    return jnp.pad(flat, ((0, 0), (0, QB))).reshape(nh, QB, REL_W)


def bias_clip_map(n_rel):
    m = np.arange(REL_W)
    dist = np.where(m < KW, m, m - REL_W) - (KW - QB)
    idx = np.clip(dist, -REL_CLIP, CHUNK - 1) + REL_CLIP
    return (idx[:, None] == np.arange(n_rel)[None, :]).astype(np.float32)


PAIR = 2


def _pair_specs(col, rows_of):
    return [pl.BlockSpec((QB, LANE), functools.partial(lambda r, h, i: (rows_of(r, i), col // LANE + h), r))
            for r in range(3)]


def _earlier(r, i):
    return jnp.maximum(i - 2 + r, 0)


def _head_lanes(hh, dh):
    lane = lax.broadcasted_iota(jnp.int32, (1, LANE), 1)
    return (lane < dh) if hh == 0 else (lane >= dh)


def _only(x, lanes):
    return jnp.where(lanes, x, jnp.zeros_like(x))


def _scores(q, ks, bias, i, scale):
    s = jnp.concatenate([_dot_nt(q, kk) for kk in ks], axis=1) * scale + bias
    col = lax.broadcasted_iota(jnp.int32, s.shape, 1)
    return jnp.where(col >= (2 - i) * QB, s, NEG)


def attn_fwd(p, cols, bias, dh, name):
    t = p.shape[0]
    nh = bias.shape[0]
    scale = dh ** -0.5

    def body(q_ref, k0, k1, k2, v0, v1, v2, b_ref, o_ref, lse_ref):
        i = pl.program_id(1)
        q = q_ref[...]
        outs = []
        for hh in range(PAIR):
            lanes = _head_lanes(hh, dh)
            s = _scores(q, [_only(kk[...], lanes) for kk in (k0, k1, k2)], b_ref[hh], i, scale)
            m = jnp.max(s, axis=-1, keepdims=True)
            e = jnp.exp(s - m)
            l = jnp.sum(e, axis=-1, keepdims=True)
            eb = e.astype(BF16)
            o = sum(_dot(eb[:, r * QB:(r + 1) * QB], vv[...]) for r, vv in enumerate((v0, v1, v2)))
            outs.append(o / l)
            lse_ref[hh] = m + jnp.log(l)
        o_ref[...] = jnp.where(_head_lanes(0, dh), outs[0], outs[1]).astype(BF16)

    st = pl.BlockSpec((PAIR, QB, 1), lambda h, i: (h, i, 0))
    return pl.pallas_call(
        body, name=name, out_shape=[_sds((t, nh * dh), BF16), _sds((nh, t, 1), F32)], grid=(nh // PAIR, t // QB),
        in_specs=[pl.BlockSpec((QB, LANE), lambda h, i: (i, cols["qa"] // LANE + h))]
        + _pair_specs(cols["ka"], _earlier) + _pair_specs(cols["va"], _earlier)
        + [pl.BlockSpec((PAIR, QB, KW), lambda h, i: (h, 0, 0))],
        out_specs=[pl.BlockSpec((QB, LANE), lambda h, i: (i, h)), st],
        compiler_params=_params("parallel", "parallel"),
    )(p, p, p, p, p, p, p, bias)


def attn_bwd(p, cols, bias, lse, dy, dh, name):
    t = p.shape[0]
    nh = bias.shape[0]
    nb = t // QB
    scale = dh ** -0.5

    def body(q_ref, k0, k1, k2, v0, v1, v2, b_ref, lse_ref, dy_ref, dq_ref, dk_ref, dv_ref, db_ref, dk_acc, dv_acc):
        i = pl.program_id(1)

        @pl.when(i == 0)
        def _():
            db_ref[...] = jnp.zeros_like(db_ref)
            dk_acc[...] = jnp.zeros_like(dk_acc)
            dv_acc[...] = jnp.zeros_like(dv_acc)

        q, dyv = q_ref[...], dy_ref[...]
        ks = [k0[...], k1[...], k2[...]]
        dqs, dks, dvs = [], [], []
        for hh in range(PAIR):
            lanes = _head_lanes(hh, dh)
            s = _scores(q, [_only(kk, lanes) for kk in ks], b_ref[hh], i, scale)
            prob = jnp.exp(s - lse_ref[hh])
            dprob = jnp.concatenate([_dot_nt(dyv, _only(vv[...], lanes)) for vv in (v0, v1, v2)], axis=1)
            delta = jnp.sum(prob * dprob, axis=-1, keepdims=True)
            ds = prob * (dprob - delta)
            dsb, pb = ds.astype(BF16), prob.astype(BF16)
            dqs.append(sum(_dot(dsb[:, r * QB:(r + 1) * QB], kk) for r, kk in enumerate(ks)))
            dks.append([_dot_tn(dsb[:, r * QB:(r + 1) * QB], q) for r in range(3)])
            dvs.append([_dot_tn(pb[:, r * QB:(r + 1) * QB], dyv) for r in range(3)])
            db_ref[hh] += ds
        first = _head_lanes(0, dh)
        dq_ref[...] = (jnp.where(first, dqs[0], dqs[1]) * scale).astype(BF16)
        for r in range(3):
            rows = pl.ds(pl.multiple_of(_earlier(r, i) * QB, QB), QB)
            dk_acc[rows, :] += jnp.where(first, dks[0][r], dks[1][r])
            dv_acc[rows, :] += jnp.where(first, dvs[0][r], dvs[1][r])

        @pl.when(i == nb - 1)
        def _():
            dk_ref[...] = (dk_acc[...] * scale).astype(BF16)
            dv_ref[...] = dv_acc[...].astype(BF16)

    st = pl.BlockSpec((PAIR, QB, 1), lambda h, i: (h, i, 0))
    tab = pl.BlockSpec((PAIR, QB, KW), lambda h, i: (h, 0, 0))
    own = pl.BlockSpec((QB, LANE), lambda h, i: (i, h))
    whole = pl.BlockSpec((t, LANE), lambda h, i: (0, h))
    return pl.pallas_call(
        body, name=name,
        out_shape=[_sds((t, nh * dh), BF16)] * 3 + [_sds((nh, QB, KW), F32)],
        grid=(nh // PAIR, nb),
        in_specs=[pl.BlockSpec((QB, LANE), lambda h, i: (i, cols["qa"] // LANE + h))]
        + _pair_specs(cols["ka"], _earlier) + _pair_specs(cols["va"], _earlier) + [tab, st, own],
        out_specs=[own, whole, whole, tab],
        scratch_shapes=[pltpu.VMEM((t, LANE), F32), pltpu.VMEM((t, LANE), F32)],
        compiler_params=_params("parallel", "arbitrary"),
    )(p, p, p, p, p, p, p, bias, lse, dy)


def _tri(strict):
    r = lax.broadcasted_iota(jnp.int32, (CHUNK, CHUNK), 0)
    c = lax.broadcasted_iota(jnp.int32, (CHUNK, CHUNK), 1)
    return jnp.where((c < r) if strict else (c <= r), 1.0, 0.0).astype(F32)


def _gate(lr, wa, ba):
    y = _dot(lr, wa) + ba
    return (jnp.minimum(y, 0.0) - jnp.log(1.0 + jnp.exp(-jnp.abs(y)))) / GATE_TAU, y


def _decays(la):
    cum = jnp.dot(_tri(False), la, preferred_element_type=F32, precision=HI)
    last = cum[CHUNK - 1:CHUNK, :]
    return jnp.exp(last - cum), jnp.exp(last)


def _gla_specs(cols, hk, hv, order):
    def at(start, width):
        return pl.BlockSpec((GB, width), lambda h, i: (order(i), start // width + h))
    return [at(cols["qb"], hk), at(cols["kb"], hk), at(cols["vb"], hv), at(cols["rb"], hv),
            pl.BlockSpec((GB, LANE), lambda h, i: (order(i), cols["lr"] // LANE))]


def gla_fwd(p, cols, wa, ba, gn, nh, hk, hv, name):
    t = p.shape[0]
    nc = t // CHUNK
    scale = hk ** -0.5
    per = GB // CHUNK

    def body(q_ref, k_ref, v_ref, r_ref, lr_ref, wa_ref, ba_ref, gn_ref, o_ref, y_ref, st_ref, state):
        @pl.when(pl.program_id(1) == 0)
        def _():
            state[...] = jnp.zeros_like(state)

        for c in range(per):
            rows = pl.ds(c * CHUNK, CHUNK)
            la, _ = _gate(lr_ref[rows, :], wa_ref[...], ba_ref[...])
            w, decay = _decays(la)
            kdec = (k_ref[rows, :].astype(F32) * w).astype(BF16)
            st = decay * state[...] + _dot_tn(v_ref[rows, :], kdec)
            state[...] = st
            st_ref[c] = st
            o = _dot_nt(q_ref[rows, :], st.astype(BF16)) * scale
            o_ref[rows, :] = o
            rinv = lax.rsqrt(jnp.mean(o * o, axis=-1, keepdims=True) + RMS_EPS)
            rv = r_ref[rows, :].astype(F32)
            y_ref[rows, :] = (o * rinv * gn_ref[...] * (rv * _sigmoid(rv))).astype(BF16)

    return pl.pallas_call(
        body, name=name,
        out_shape=[_sds((t, nh * hv), F32), _sds((t, nh * hv), BF16), _sds((nh, nc, hv, hk), F32)],
        grid=(nh, t // GB),
        in_specs=_gla_specs(cols, hk, hv, lambda i: i)
        + [pl.BlockSpec((LANE, hk), lambda h, i: (0, h)), pl.BlockSpec((1, hk), lambda h, i: (0, h)),
           pl.BlockSpec((1, hv), lambda h, i: (0, 0))],
        out_specs=[pl.BlockSpec((GB, hv), lambda h, i: (i, h)), pl.BlockSpec((GB, hv), lambda h, i: (i, h)),
                   pl.BlockSpec((None, per, hv, hk), lambda h, i: (h, i, 0, 0))],
        scratch_shapes=[pltpu.VMEM((hv, hk), F32)], compiler_params=_params("parallel", "arbitrary"),
    )(p, p, p, p, p, wa, ba, gn)


def gla_bwd(p, cols, wa, ba, gn, o, states, dy, nh, hk, hv, name):
    t = p.shape[0]
    nblk = t // GB
    scale = hk ** -0.5
    per = GB // CHUNK

    def rev(i):
        return nblk - 1 - i

    def body(q_ref, k_ref, v_ref, r_ref, lr_ref, wa_ref, ba_ref, gn_ref, o_ref, st_ref, sp_ref, dy_ref,
             dq_ref, dk_ref, dv_ref, dr_ref, dg_ref, dgn_ref, carry):
        h, i = pl.program_id(0), pl.program_id(1)

        @pl.when(i == 0)
        def _():
            carry[...] = jnp.zeros_like(carry)

        @pl.when((i == 0) & (h == 0))
        def _():
            dgn_ref[...] = jnp.zeros_like(dgn_ref)

        gnv = gn_ref[...]
        for c in reversed(range(per)):
            rows = pl.ds(c * CHUNK, CHUNK)
            rv = r_ref[rows, :].astype(F32)
            sg = _sigmoid(rv)
            dyv = dy_ref[rows, :].astype(F32)
            ov = o_ref[rows, :]
            rinv = lax.rsqrt(jnp.mean(ov * ov, axis=-1, keepdims=True) + RMS_EPS)
            dn = dyv * (rv * sg)
            dr_ref[rows, :] = (dyv * (ov * rinv * gnv) * (sg * (1.0 + rv * (1.0 - sg)))).astype(BF16)
            dgn_ref[...] += _colsum(dn * ov * rinv)
            dxh = dn * gnv
            do = rinv * dxh - ov * (rinv * rinv * rinv) * jnp.mean(dxh * ov, axis=-1, keepdims=True)
            dob = (do * scale).astype(BF16)
            qv, kv, vv = q_ref[rows, :], k_ref[rows, :], v_ref[rows, :]
            dq_ref[rows, :] = _dot(dob, st_ref[c].astype(BF16)).astype(BF16)
            dst = carry[...] + _dot_tn(dob, qv)
            if c > 0:
                prev = st_ref[c - 1]
            else:
                prev = jnp.where(i == nblk - 1, 0.0, sp_ref[0])
            ddecay = _colsum(dst * prev)
            la, y = _gate(lr_ref[rows, :], wa_ref[...], ba_ref[...])
            w, decay = _decays(la)
            kf = kv.astype(F32)
            kdec = (kf * w).astype(BF16)
            dstb = dst.astype(BF16)
            dkdec = _dot(vv, dstb)
            dv_ref[rows, :] = _dot_nt(kdec, dstb).astype(BF16)
            dk_ref[rows, :] = (dkdec * w).astype(BF16)
            e = dkdec * kf * w
            dla = jnp.dot(_tri(True), e, preferred_element_type=F32, precision=HI) + ddecay * decay
            dg_ref[rows, :] = dla * (1.0 / GATE_TAU) * _sigmoid(-y)
            carry[...] = decay * dst

    per_head = lambda width: pl.BlockSpec((GB, width), lambda h, i: (rev(i), h))
    return pl.pallas_call(
        body, name=name,
        out_shape=[_sds((t, nh * hk), BF16), _sds((t, nh * hk), BF16), _sds((t, nh * hv), BF16),
                   _sds((t, nh * hv), BF16), _sds((t, nh * hk), F32), _sds((1, hv), F32)],
        grid=(nh, nblk),
        in_specs=_gla_specs(cols, hk, hv, rev)
        + [pl.BlockSpec((LANE, hk), lambda h, i: (0, h)), pl.BlockSpec((1, hk), lambda h, i: (0, h)),
           pl.BlockSpec((1, hv), lambda h, i: (0, 0)), per_head(hv),
           pl.BlockSpec((None, per, hv, hk), lambda h, i: (h, rev(i), 0, 0)),
           pl.BlockSpec((None, 1, hv, hk), lambda h, i: (h, jnp.maximum(rev(i) * per - 1, 0), 0, 0)),
           per_head(hv)],
        out_specs=[per_head(hk), per_head(hk), per_head(hv), per_head(hv), per_head(hk),
                   pl.BlockSpec((1, hv), lambda h, i: (0, 0))],
        scratch_shapes=[pltpu.VMEM((hv, hk), F32)], compiler_params=_params("arbitrary", "arbitrary"),
    )(p, p, p, p, p, wa, ba, gn, o, states, states, dy)


def gate_bwd(p, lr_col, dg, wa, name):
    t, kd = dg.shape
    tm = _row_tile(t, 512, 16)

    def body(lr_ref, dg_ref, wa_ref, dlr_ref, dwa_ref, dba_ref):
        @pl.when(pl.program_id(0) == 0)
        def _():
            dwa_ref[...] = jnp.zeros_like(dwa_ref)
            dba_ref[...] = jnp.zeros_like(dba_ref)

        g = dg_ref[...]
        gb = g.astype(BF16)
        dlr_ref[...] = _dot_nt(gb, wa_ref[...]).astype(BF16)
        dwa_ref[...] += _dot_tn(lr_ref[...], gb)
        dba_ref[...] += _colsum(g)

    return pl.pallas_call(
        body, name=name, out_shape=[_sds((t, LANE), BF16), _sds((LANE, kd), F32), _sds((1, kd), F32)],
        grid=(t // tm,),
        in_specs=[pl.BlockSpec((tm, LANE), lambda i: (i, lr_col // LANE)), pl.BlockSpec((tm, kd), lambda i: (i, 0)),
                  pl.BlockSpec((LANE, kd), lambda i: (0, 0))],
        out_specs=[pl.BlockSpec((tm, LANE), lambda i: (i, 0)), pl.BlockSpec((LANE, kd), lambda i: (0, 0)),
                   pl.BlockSpec((1, kd), lambda i: (0, 0))],
        compiler_params=_params("arbitrary"),
    )(p, dg, wa)


def proj_merge(ya, yb, wa3, wb3, p, ga_col, gb_col, name):
    t, kd = ya.shape
    nb, _, bw = wa3.shape
    tm = _row_tile(t, 512, 16)

    def body(ya_ref, yb_ref, wa_ref, wb_ref, ga_ref, gb_ref, pa_ref, pb_ref, mg_ref):
        pa = _dot(ya_ref[...], wa_ref[...])
        pb = _dot(yb_ref[...], wb_ref[...])
        pa_ref[...] = pa.astype(BF16)
        pb_ref[...] = pb.astype(BF16)
        mg_ref[...] = (_sigmoid(ga_ref[...].astype(F32)) * pa + _sigmoid(gb_ref[...].astype(F32)) * pb).astype(BF16)

    act = pl.BlockSpec((tm, kd), lambda j, i: (i, 0))
    wsp = pl.BlockSpec((None, kd, bw), lambda j, i: (j, 0, 0))
    out = pl.BlockSpec((tm, bw), lambda j, i: (i, j))
    return pl.pallas_call(
        body, name=name, out_shape=[_sds((t, nb * bw), BF16)] * 3, grid=(nb, t // tm),
        in_specs=[act, act, wsp, wsp, pl.BlockSpec((tm, bw), lambda j, i: (i, ga_col // bw + j)),
                  pl.BlockSpec((tm, bw), lambda j, i: (i, gb_col // bw + j))],
        out_specs=[out] * 3, compiler_params=_params("parallel", "parallel"),
    )(ya, yb, wa3, wb3, p, p)


def merge_bwd(dm, w, p, ga_col, gb_col, pa, pb, name):
    t, d = dm.shape
    n = w.shape[0]
    tn = _row_tile(n, 512, LANE)
    tm = _row_tile(t, 256, 16)

    def body(dm_ref, w_ref, ga_ref, gb_ref, pa_ref, pb_ref, dpa_ref, dpb_ref, dga_ref, dgb_ref):
        dmg = _dot_nt(dm_ref[...], w_ref[...])
        sa = _sigmoid(ga_ref[...].astype(F32))
        sb = _sigmoid(gb_ref[...].astype(F32))
        dpa_ref[...] = (dmg * sa).astype(BF16)
        dpb_ref[...] = (dmg * sb).astype(BF16)
        dga_ref[...] = (dmg * pa_ref[...].astype(F32) * sa * (1.0 - sa)).astype(BF16)
        dgb_ref[...] = (dmg * pb_ref[...].astype(F32) * sb * (1.0 - sb)).astype(BF16)

    out = pl.BlockSpec((tm, tn), lambda j, i: (i, j))
    return pl.pallas_call(
        body, name=name, out_shape=[_sds((t, n), BF16)] * 4, grid=(n // tn, t // tm),
        in_specs=[pl.BlockSpec((tm, d), lambda j, i: (i, 0)), pl.BlockSpec((tn, d), lambda j, i: (j, 0)),
                  pl.BlockSpec((tm, tn), lambda j, i: (i, ga_col // tn + j)),
                  pl.BlockSpec((tm, tn), lambda j, i: (i, gb_col // tn + j)), out, out],
        out_specs=[out] * 4, compiler_params=_params("parallel", "parallel"),
    )(dm, w, p, p, pa, pb)


def rel_bias_grad(skew, clip_map, name):
    nh, _, jd = skew.shape
    n_rel = clip_map.shape[1]

    def body(s_ref, c_ref, o_ref):
        sums = jnp.concatenate([_colsum(s_ref[h]) for h in range(nh)], axis=0)
        o_ref[...] = jnp.dot(sums, c_ref[...], preferred_element_type=F32, precision=HI)

    return pl.pallas_call(
        body, name=name, out_shape=_sds((nh, n_rel), F32), in_specs=[VMEM_SPEC, VMEM_SPEC], out_specs=VMEM_SPEC,
        compiler_params=pltpu.CompilerParams(vmem_limit_bytes=VMEM_LIMIT),
    )(skew, clip_map)


MIX_BLOCK = 9 * LANE


def mix_layout(d, a_width, bk, bv):
    main = 3 * a_width + 2 * bk + 2 * bv
    cols = {"qa": 0, "ka": a_width, "va": 2 * a_width, "qb": 3 * a_width, "kb": 3 * a_width + bk,
            "vb": 3 * a_width + 2 * bk, "rb": 3 * a_width + 2 * bk + bv, "ga": main, "gb": main + d,
            "lr": main + 2 * d}
    total = main + 2 * d + LANE
    assert total % MIX_BLOCK == 0
    return cols, main, total


def _mix_pieces(per, main, rank, d):
    out = []
    for lo, hi in ((0, main), (main + rank, main + rank + 2 * d), (main, main + rank)):
        while lo < hi:
            cut = min(hi, (lo // per + 1) * per)
            out.append((lo, cut))
            lo = cut
    return out


def mix_weight_in(g3, main, rank):
    d = g3.shape[2]
    flat = g3.reshape(-1, d)
    return jnp.concatenate([flat[:main], flat[main + rank:], flat[main:main + rank],
                            jnp.zeros((LANE - rank, d), g3.dtype)], axis=0)


def mix_weight_grad_out(gt, main, rank, per):
    d = gt.shape[1]
    blocks = [[] for _ in range(N_DEV)]
    pos = 0
    for lo, hi in _mix_pieces(per, main, rank, d):
        blocks[lo // per].append((lo, gt[pos:pos + hi - lo]))
        pos += hi - lo
    return jnp.stack([jnp.concatenate([x for _, x in sorted(b, key=lambda e: e[0])], axis=0) for b in blocks])


def ffn_forward(h, sh, sc, g, w_in3, w_out_of, ln_g, ln_b, tag):
    a, b, s = ffn_in(h, sh, sc, w_in3, f"{tag}_in")
    w_out = w_out_of(s)
    f, z, hout = out_ln(s, w_out, h, g, ln_g, ln_b, 0.5, f"{tag}_out")
    return hout, (h, a, b, s, f, z), w_out


def ffn_backward_weights(dh, saved, sh, sc, g, w_in3, w_out, ln_g, tag, target=None):
    hin, a, b, s, f, z = saved
    t, d = hin.shape
    nb, _, bw = w_in3.shape
    half = nb // 2
    fdim = w_out.shape[0]
    res = ln_bwd(dh, z, f, ln_g, g, 0.5, f"{tag}_ln_bwd", target=target)
    dz, df, dln_g, dln_b, dg = res[:5]
    dab = ffn_bwd_act(df, w_out, a, b, f"{tag}_act_bwd")
    tk = _row_tile(t, 512, 16)
    dw_out = matmul_tn(f"{tag}_dwout", s, (tk, bw), lambda n, k: (k, n), df, (tk, d), lambda n, k: (k, 0),
                       _sds((fdim, d), BF16), (bw, d), lambda n, k: (n, 0), fdim // bw)
    dw_in = matmul_tn(f"{tag}_dwin", hin, (tk, d), lambda n, k: (k, 0), dab, (None, tk, bw),
                      lambda n, k: (n // half, k, n % half), _sds((nb, d, bw), BF16), (None, d, bw),
                      lambda n, k: (n, 0, 0), nb, mod=(sh, sc))
    grads = dict(w_in=dw_in, w_out=dw_out.reshape(N_DEV, fdim // N_DEV, d), ln_g=dln_g, ln_b=dln_b, g=dg)
    return (dab, dz), grads, (res[5] if target is not None else None)


def ffn_backward_input(carry, saved, sc, w_in3, tag):
    dab, dz = carry
    hin = saved[0]
    t = hin.shape[0]
    nb, _, bw = w_in3.shape
    half = nb // 2
    tm = _row_tile(t, 512, 16)
    return matmul_nt_blocks(f"{tag}_du", dab, (None, tm, bw), lambda i, k: (k // half, i, k % half),
                            w_in3, t, resid=(dz, hin, sc))


def _after(v, token):
    return v if token is None else v + token[:1, :1]


def local_step(x, target, mod, weights_of, grads_ready, grads_sent, rel_bias, w_alpha2, b_alpha, gla_norm_g, lns):
    t, d = x.shape
    sh1, sc1, g1, sh2, sc2, g2, sh3, sc3, g3 = [mod[i:i + 1] for i in range(N_MOD)]
    ln1_g, ln1_b, ln2_g, ln2_b, ln3_g, ln3_b = lns
    n_heads_a, n_rel = rel_bias.shape
    rank, bk = w_alpha2.shape
    hv = gla_norm_g.shape[1]

    w1 = weights_of("ffn1_in", x)
    h1, saved1, w1["out"] = ffn_forward(x, sh1, sc1, g1, w1["in"], lambda s: weights_of("ffn1_out", s)["out"],
                                        ln1_g, ln1_b, "ffn1")
    wm = weights_of("mix", h1)
    a_width = wm["proj_a"].shape[1]
    bv = wm["proj_b"].shape[1]
    nh_b = bv // hv
    hk = bk // nh_b
    cols, main, total = mix_layout(d, a_width, bk, bv)
    w_mix = mix_weight_in(wm["in_t"], main, rank)
    p = mod_matmul(h1, sh2, sc2, w_mix, MIX_BLOCK, "mix_in")
    bias = bias_table(rel_bias)
    dh = a_width // n_heads_a
    assert PAIR * dh == LANE
    ya, lse = attn_fwd(p, cols, bias, dh, "attn_fwd")
    wa_pad = jnp.zeros((LANE, bk), BF16).at[:rank].set(w_alpha2.astype(BF16))
    o_b, yb, states = gla_fwd(p, cols, wa_pad, b_alpha, gla_norm_g, nh_b, hk, hv, "gla_fwd")
    pa, pb, merged = proj_merge(ya, yb, wm["proj_a"], wm["proj_b"], p, cols["ga"], cols["gb"], "proj_merge")
    m, z2, h2 = out_ln(merged, wm["out"], h1, g2, ln2_g, ln2_b, 1.0, "mix_out")
    w3 = weights_of("ffn2", h2)
    h3, saved3, _ = ffn_forward(h2, sh3, sc3, g3, w3["in"], lambda s: w3["out"], ln3_g, ln3_b, "ffn2")

    carry3, gr3, loss = ffn_backward_weights(h3, saved3, sh3, sc3, g3, w3["in"], w3["out"], ln3_g, "ffn2",
                                             target=target)
    token = grads_ready("ffn2", dict(ffn2_in=gr3["w_in"], ffn2_out=gr3["w_out"]))
    dh2, dsc3, dsh3 = ffn_backward_input(carry3, saved3, _after(sc3, token), w3["in"], "ffn2")
    token = grads_sent("ffn2", dh2)
    dz2, dm, dln2_g, dln2_b, dg2 = ln_bwd(dh2, z2, m, _after(ln2_g, token), g2, 1.0, "mix_ln_bwd")
    dpa, dpb, dga, dgb = merge_bwd(dm, wm["out"], p, cols["ga"], cols["gb"], pa, pb, "merge_bwd")
    tk = _row_tile(t, 512, 16)
    dw_mix_out = matmul_tn("mix_dwout", merged, (tk, 512), lambda n, k: (k, n), dm, (tk, d), lambda n, k: (k, 0),
                           _sds((d, d), BF16), (512, d), lambda n, k: (n, 0), d // 512)
    tm = _row_tile(t, 512, 16)
    pbw = wm["proj_a"].shape[2]
    dya = matmul_nt_blocks("proj_a_dy", dpa, (tm, pbw), lambda i, k: (i, k), wm["proj_a"], t)
    dyb = matmul_nt_blocks("proj_b_dy", dpb, (tm, pbw), lambda i, k: (i, k), wm["proj_b"], t)
    dw_pa = matmul_tn("proj_a_dw", ya, (tk, a_width), lambda n, k: (k, 0), dpa, (tk, pbw), lambda n, k: (k, n),
                      _sds((N_DEV, a_width, pbw), BF16), (None, a_width, pbw), lambda n, k: (n, 0, 0), N_DEV)
    dw_pb = matmul_tn("proj_b_dw", yb, (tk, bv), lambda n, k: (k, 0), dpb, (tk, pbw), lambda n, k: (k, n),
                      _sds((N_DEV, bv, pbw), BF16), (None, bv, pbw), lambda n, k: (n, 0, 0), N_DEV)
    dqb, dkb, dvb, drb, dgate, dgn = gla_bwd(p, cols, wa_pad, b_alpha, gla_norm_g, o_b, states, dyb,
                                             nh_b, hk, hv, "gla_bwd")
    dlr, dwa_pad, dba = gate_bwd(p, cols["lr"], dgate, wa_pad, "gate_bwd")
    dqa, dka, dva, dbias = attn_bwd(p, cols, bias, lse, dya, dh, "attn_bwd")
    d_rel = rel_bias_grad(bias_grad_skew(dbias), jnp.asarray(bias_clip_map(n_rel)), "rel_bias_grad")
    dp = jnp.concatenate([dqa, dka, dva, dqb, dkb, dvb, drb,
                          dga, dgb, dlr], axis=1)
    dw_mix_t = matmul_tn("mix_dwin", dp, (tk, MIX_BLOCK), lambda n, k: (k, n), h1, (tk, d), lambda n, k: (k, 0),
                         _sds((total, d), BF16), (MIX_BLOCK, d), lambda n, k: (n, 0), total // MIX_BLOCK,
                         mod=(sh2, sc2), mod_b=True)
    dw_mix_in = mix_weight_grad_out(dw_mix_t, main, rank, wm["in_t"].shape[1])
    token = grads_ready("mix", dict(mix_in=dw_mix_in, proj_a=dw_pa, proj_b=dw_pb,
                                    mix_out=dw_mix_out.reshape(N_DEV, d // N_DEV, d)))
    tm = _row_tile(t, 512, 16)
    dh1, dsc2, dsh2 = matmul_nt_blocks("mix_du", dp, (tm, MIX_BLOCK), lambda i, k: (i, k), w_mix, t,
                                       resid=(dz2, h1, _after(sc2, token)))
    token = grads_sent("mix", dh1)
    carry1, gr1, _ = ffn_backward_weights(dh1, saved1, sh1, sc1, g1, w1["in"], w1["out"], _after(ln1_g, token),
                                          "ffn1")
    token = grads_ready("ffn1", dict(ffn1_in=gr1["w_in"], ffn1_out=gr1["w_out"]))
    dx, dsc1, dsh1 = ffn_backward_input(carry1, saved1, _after(sc1, token), w1["in"], "ffn1")

    dmod = [dsh1, dsc1, gr1["g"], dsh2, dsc2, dg2, dsh3, dsc3, gr3["g"]]
    small = dict(ln1_g=gr1["ln_g"], ln1_b=gr1["ln_b"], ln2_g=dln2_g, ln2_b=dln2_b, ln3_g=gr3["ln_g"],
                 ln3_b=gr3["ln_b"], b_alpha=dba, gla_norm_g=dgn, w_alpha2=dwa_pad[:rank], rel_bias=d_rel)
    return loss, dx, dmod, small


GROUPS = dict(ffn1=("ffn1_in", "ffn1_out"), mix=("mix_in", "proj_a", "proj_b", "mix_out"),
              ffn2=("ffn2_in", "ffn2_out"))
GATHERS = dict(ffn1_in=("ffn1_in",), ffn1_out=("ffn1_out",), mix=GROUPS["mix"], ffn2=GROUPS["ffn2"])
SMALL_REPLICATED = ("b_ada", "ln1_g", "ln1_b", "ln2_g", "ln2_b", "ln3_g", "ln3_b", "b_alpha", "gla_norm_g")
SMALL_SHARDED = ("rel_bias", "w_alpha2")
WEIGHT_ORDER = ("w_ada", "b_ada", "ffn1_w_in", "ffn1_w_out", "ln1_g", "ln1_b", "w_mix_in", "rel_bias", "w_alpha2",
                "b_alpha", "gla_norm_g", "w_proj_a", "w_proj_b", "w_mix_out", "ln2_g", "ln2_b", "ffn2_w_in",
                "ffn2_w_out", "ln3_g", "ln3_b")
BIG_NAME = dict(ffn1_in="ffn1_w_in", ffn1_out="ffn1_w_out", mix_in="w_mix_in", proj_a="w_proj_a",
                proj_b="w_proj_b", mix_out="w_mix_out", ffn2_in="ffn2_w_in", ffn2_out="ffn2_w_out")


def kernel(x, c, w_ada, b_ada, ffn1_w_in, ffn1_w_out, ln1_g, ln1_b, w_mix_in, rel_bias, w_alpha2, b_alpha, gla_norm_g, w_proj_a, w_proj_b, w_mix_out, ln2_g, ln2_b, ffn2_w_in, ffn2_w_out, ln3_g, ln3_b, loss_target, m_w_ada, m_b_ada, m_ffn1_w_in, m_ffn1_w_out, m_ln1_g, m_ln1_b, m_w_mix_in, m_rel_bias, m_w_alpha2, m_b_alpha, m_gla_norm_g, m_w_proj_a, m_w_proj_b, m_w_mix_out, m_ln2_g, m_ln2_b, m_ffn2_w_in, m_ffn2_w_out, m_ln3_g, m_ln3_b, v_w_ada, v_b_ada, v_ffn1_w_in, v_ffn1_w_out, v_ln1_g, v_ln1_b, v_w_mix_in, v_rel_bias, v_w_alpha2, v_b_alpha, v_gla_norm_g, v_w_proj_a, v_w_proj_b, v_w_mix_out, v_ln2_g, v_ln2_b, v_ffn2_w_in, v_ffn2_w_out, v_ln3_g, v_ln3_b):
    env = dict(locals())
    w = {n: env[n] for n in WEIGHT_ORDER}
    mom = {n: env["m_" + n] for n in WEIGHT_ORDER}
    var = {n: env["v_" + n] for n in WEIGHT_ORDER}
    me = _me()
    dev = _lin(me)
    core = jnp.reshape(me[2], (1,)).astype(jnp.int32)
    chip = jnp.reshape(2 * me[0] + me[1], (1,)).astype(jnp.int32)
    d = x.shape[-1]

    def shard(n):
        s = w[BIG_NAME[n]][0].astype(BF16)
        return s.T if n == "mix_in" else s

    dev_idx = jnp.reshape(dev, (1,)).astype(jnp.int32)
    started = {}

    def start(grp, after):
        shards = [shard(n) for n in GATHERS[grp]]
        lands = [place_own(dev_idx, s, f"place_own_{n}") for n, s in zip(GATHERS[grp], shards)]
        started[grp] = gather_start(shards, lands, after, f"gather_start_{grp}")
        return started[grp][-1]

    first, *rest = list(GATHERS)
    order = start(first, x[0, :1, :1])

    ada_cols = w_ada.shape[-1]
    c_all = all_gather_small(_after(c, order), "gather_c")[:, 0, :]
    b_cols = lax.dynamic_slice_in_dim(b_ada, dev * ada_cols, ada_cols, axis=1)
    mod_cols = adaln_cols(c_all, w_ada[0], b_cols, "adaln_cols")
    mod_all = all_gather_small(mod_cols, "gather_mod")
    mod = lax.dynamic_index_in_dim(mod_all, dev, axis=1, keepdims=False).reshape(N_MOD, d)

    small_w = all_gather_small(_after(jnp.concatenate([rel_bias[0], w_alpha2[0]], axis=1), order), "gather_small_w")
    n_rel_cols = rel_bias.shape[-1]
    rel_full = small_w[:, :, :n_rel_cols].transpose(1, 0, 2).reshape(small_w.shape[1], -1)
    wa2_full = small_w[:, :, n_rel_cols:].transpose(1, 0, 2).reshape(small_w.shape[1], -1)

    order = small_w[0, :1, :1] + mod_all[0, :1, :1]
    for grp in rest:
        order = start(grp, order)
    mod = _after(mod, order)

    def weights_of(grp, after):
        _, zones = gather_wait(started[grp], after, f"gather_wait_{grp}")
        full = dict(zip(GATHERS[grp], gather_forward(zones, f"gather_forward_{grp}")))
        if grp == "mix":
            return dict(in_t=full["mix_in"], proj_a=full["proj_a"], proj_b=full["proj_b"],
                        out=full["mix_out"].reshape(-1, d))
        return {"in" if n.endswith("_in") else "out": v if n.endswith("_in") else v.reshape(-1, d)
                for n, v in full.items()}

    pairs, exchanges = {}, {}
    last = list(GROUPS)[0]

    def to_chips(grp, grads, got):
        sums = [pair_add(core, grads[n], g, f"grad_pair_add_{n}") for n, g in zip(GROUPS[grp], got)]
        exchanges[grp] = exchange_start(sums, chip_routes, f"grad_chip_start_{grp}")
        return exchanges[grp][-1]

    def grads_ready(grp, grads):
        if grp == last:
            return to_chips(grp, grads, pair_exchange([grads[n] for n in GROUPS[grp]], f"grad_pair_exchange_{grp}"))
        pairs[grp] = exchange_start([grads[n] for n in GROUPS[grp]], pair_routes, f"grad_pair_start_{grp}")
        return pairs[grp][-1]

    def grads_sent(grp, after):
        if grp == last:
            return None
        sent, got = exchange_wait(pairs[grp], pair_routes, after, f"grad_pair_wait_{grp}")
        return to_chips(grp, dict(zip(GROUPS[grp], sent)), got)

    lns = [ln1_g, ln1_b, ln2_g, ln2_b, ln3_g, ln3_b]
    loss, dx, dmod, small = local_step(x[0], loss_target[0], mod, weights_of, grads_ready, grads_sent, rel_full,
                                       wa2_full, b_alpha, gla_norm_g, lns)

    pieces = (list(dmod) + [small[n].reshape(1, -1) for n in SMALL_REPLICATED[1:] + SMALL_SHARDED] + [loss])
    parts = all_gather_rows(pieces, "gather_small_grads")
    loss = jnp.sum(parts[:, 0, parts.shape[2] - loss.shape[1]])
    n_mod = N_MOD * d
    dmod_all = parts[:, 0, :n_mod]
    g_w_ada = adaln_wgrad(c_all, lax.dynamic_slice_in_dim(dmod_all, dev * ada_cols, ada_cols, axis=1), "adaln_wgrad")
    out = {}
    out["w_ada"] =[o[None] for o in adamw_sum(g_w_ada[None], w_ada[0], m_w_ada[0], v_w_ada[0], "adamw_w_ada")]

    sources, where, off = [parts], [], 0
    for n in SMALL_REPLICATED:
        where.append((0, off))
        off += w[n].size
    for n in SMALL_SHARDED:
        rows, cols_local = w[n].shape[1], w[n].shape[2]
        full_part = parts[:, 0, off:off + rows * cols_local * N_DEV].reshape(N_DEV, rows, cols_local * N_DEV)
        mine = lax.dynamic_slice_in_dim(full_part, dev * cols_local, cols_local, axis=2)
        where.append((len(sources), 0))
        sources.append(mine.reshape(N_DEV, 1, rows * cols_local))
        off += rows * cols_local * N_DEV
    names = SMALL_REPLICATED + SMALL_SHARDED
    res = adamw_rows(sources, where, *[[src[n].reshape(1, -1) for n in names] for src in (w, mom, var)],
                     "adamw_small")
    for n, res_n in zip(names, res):
        out[n] = [r.reshape(w[n].shape) for r in res_n]

    order = res[0][0]
    for grp in reversed(list(GROUPS)):
        sums, recv = exchange_wait(exchanges[grp], chip_routes, order, f"grad_chip_wait_{grp}")
        for n, hsum, r in zip(GROUPS[grp], sums, recv):
            full = BIG_NAME[n]
            if n == "mix_in":
                g = owned_sum(chip, hsum, r, f"owned_sum_{n}").T
                res_n = adamw_sum(g[None], w[full][0], mom[full][0], var[full][0], f"adamw_{n}")
            else:
                res_n = adamw_owned(chip, hsum, r, w[full][0], mom[full][0], var[full][0], f"adamw_{n}")
            out[full] = [o[None] for o in res_n]
            order = res_n[0]

    flat = [loss, dx[None]]
    for k in range(4):
        flat += [out[n][k] for n in WEIGHT_ORDER]
    return tuple(flat)
```

```python
import functools

import numpy as np
import jax
import jax.numpy as jnp
from jax import lax
from jax.experimental import pallas as pl
from jax.experimental.pallas import tpu as pltpu

F32 = jnp.float32
BF16 = jnp.bfloat16
MESH = pl.DeviceIdType.MESH
N_DEV = 8
N_CHIP = 4

CHUNK = 64
A_PAST_CHUNKS = 8
REL_CLIP = 256
GATE_TAU = 16.0
N_MOD = 9
DEPTH = 1
ALPHA = (2.0 * DEPTH) ** 0.25
LN_EPS = 1e-5
RMS_EPS = 1e-6
ADAM_LR = 0.001
ADAM_B1 = 0.9
ADAM_B2 = 0.999
ADAM_EPS = 1e-08
ADAM_WD = 0.01
ADAM_STEP = 10

LANE = 128
VMEM_LIMIT = 56 * 2 ** 20
QB = 4 * CHUNK
KW = 3 * QB
GB = 8 * CHUNK
NEG = -1e30
HI = lax.Precision.HIGHEST

ANY = pl.BlockSpec(memory_space=pl.ANY)
VMEM_SPEC = pl.BlockSpec(memory_space=pltpu.VMEM)


def _params(*sem):
    return pltpu.CompilerParams(dimension_semantics=sem, vmem_limit_bytes=VMEM_LIMIT)


def _sds(shape, dtype):
    return jax.ShapeDtypeStruct(shape, dtype)


def _dot(a, b):
    return jnp.dot(a, b, preferred_element_type=F32)


def _dot_nt(a, b):
    return lax.dot_general(a, b, (((1,), (1,)), ((), ())), preferred_element_type=F32)


def _dot_tn(a, b):
    return lax.dot_general(a, b, (((0,), (0,)), ((), ())), preferred_element_type=F32)


def _sigmoid(x):
    return 0.5 * jnp.tanh(0.5 * x) + 0.5


def _colsum(x):
    return jnp.sum(x, axis=0, keepdims=True)


def _row_tile(rows, cap, mult):
    for t in range(min(rows, cap), 0, -1):
        if rows % t == 0 and t % mult == 0:
            return t
    return rows


def _me():
    return lax.axis_index("x"), lax.axis_index("y"), lax.axis_index("c")


def _flip(me, k):
    return tuple((1 - p) if (k >> s) & 1 else p for p, s in zip(me, (2, 1, 0)))


def _lin(p):
    return 4 * p[0] + 2 * p[1] + p[2]


def _gather_direct(x_ref, out_ref, send_sems, recv_sems, local_sem):
    me = _me()
    mine = pltpu.make_async_copy(x_ref, out_ref.at[_lin(me)], local_sem)
    mine.start()
    sends = []
    for k in range(1, N_DEV):
        cp = pltpu.make_async_remote_copy(
            src_ref=x_ref, dst_ref=out_ref.at[_lin(me)], send_sem=send_sems.at[k - 1],
            recv_sem=recv_sems.at[k - 1], device_id=_flip(me, k), device_id_type=MESH)
        cp.start()
        sends.append(cp)
    for k in range(1, N_DEV):
        peer = _flip(me, k)
        pltpu.make_async_remote_copy(
            src_ref=x_ref, dst_ref=out_ref.at[_lin(peer)], send_sem=send_sems.at[k - 1],
            recv_sem=recv_sems.at[k - 1], device_id=peer, device_id_type=MESH).wait_recv()
    for cp in sends:
        cp.wait_send()
    mine.wait()


GATHER_SEMS = [pltpu.SemaphoreType.DMA((N_DEV - 1,)), pltpu.SemaphoreType.DMA((N_DEV - 1,)), pltpu.SemaphoreType.DMA]


def all_gather_small(x, name):
    r, n = x.shape
    return pl.pallas_call(
        _gather_direct_body(), name=name, out_shape=_sds((N_DEV, r, n), x.dtype),
        in_specs=[VMEM_SPEC], out_specs=VMEM_SPEC, scratch_shapes=GATHER_SEMS,
    )(x)


def _gather_direct_body():
    def body(x_ref, out_ref, send_sems, recv_sems, local_sem):
        _gather_direct(x_ref, out_ref, send_sems, recv_sems, local_sem)
    return body


def all_gather_rows(pieces, name):
    sizes = [x.shape[1] for x in pieces]
    total = sum(sizes)
    assert all(n % LANE == 0 for n in sizes)

    def body(*refs):
        ins = refs[:len(pieces)]
        out_ref, row, send_sems, recv_sems, local_sem = refs[len(pieces):]
        off = 0
        for x_ref, n in zip(ins, sizes):
            row[:, off:off + n] = x_ref[...]
            off += n
        _gather_direct(row, out_ref, send_sems, recv_sems, local_sem)

    return pl.pallas_call(
        body, name=name, out_shape=_sds((N_DEV, 1, total), F32),
        in_specs=[VMEM_SPEC] * len(pieces), out_specs=VMEM_SPEC,
        scratch_shapes=[pltpu.VMEM((1, total), F32)] + GATHER_SEMS,
    )(*pieces)


HBM_SPEC = pl.BlockSpec(memory_space=pltpu.HBM)
SEM_SPEC = pl.BlockSpec(memory_space=pltpu.SEMAPHORE)
EFFECT = pltpu.SideEffectType.DATAFLOW_SIDE_EFFECTING
FIRST = N_CHIP


def _hbm(v):
    return pltpu.with_memory_space_constraint(v, pltpu.HBM)


def _other_chips(mx, my):
    return [(1 - mx, my), (mx, 1 - my), (1 - mx, 1 - my)]


def place_own(dev, shard, name):
    rows, cols = shard.shape
    tr, tc = _tile2(rows, cols, 16)

    def body(dev_ref, s_ref, land_ref, o_ref):
        o_ref[...] = s_ref[...]

    land = lax.empty((N_DEV, rows, cols), shard.dtype)
    return pl.pallas_call(
        body, name=name, out_shape=_sds(land.shape, land.dtype),
        grid_spec=pltpu.PrefetchScalarGridSpec(
            num_scalar_prefetch=1, grid=(rows // tr, cols // tc),
            in_specs=[pl.BlockSpec((tr, tc), lambda i, j, d: (i, j)), ANY],
            out_specs=pl.BlockSpec((None, tr, tc), lambda i, j, d: (d[0], i, j))),
        input_output_aliases={2: 0}, compiler_params=_params("parallel", "parallel"),
    )(dev, shard, land)


def gather_start(shards, lands, after, name):
    n = len(shards)

    def body(*refs):
        ins, zones = refs[:n], refs[n:2 * n]
        send_sems, recv_sems = refs[2 * n + 1], refs[2 * n + 2]
        token = refs[-1]
        me = _me()
        mx, my, mc = me
        for a in range(n):
            dst = zones[a].at[_lin(me)]
            targets = [(mx, my, 1 - mc)] + [(*chip, mc) for chip in _other_chips(mx, my)]
            for k, to in enumerate(targets):
                pltpu.make_async_remote_copy(
                    src_ref=ins[a], dst_ref=dst, send_sem=send_sems.at[a * FIRST + k],
                    recv_sem=recv_sems.at[a * FIRST + k], device_id=to, device_id_type=MESH).start()
        token[...] = jnp.zeros_like(token)

    sems = pltpu.SemaphoreType.DMA((n * FIRST,))
    out = pl.pallas_call(
        body, name=name,
        out_shape=(sems, sems, *[pltpu.HBM(s.shape, s.dtype) for s in shards],
                   *[pltpu.HBM(z.shape, z.dtype) for z in lands], _sds((8, LANE), F32)),
        in_specs=[HBM_SPEC] * (2 * n) + [ANY],
        out_specs=(SEM_SPEC, SEM_SPEC, *[HBM_SPEC] * (2 * n), VMEM_SPEC),
        input_output_aliases={a: 2 + a for a in range(2 * n)},
        compiler_params=pltpu.CompilerParams(has_side_effects=EFFECT),
    )(*[_hbm(s) for s in shards], *[_hbm(z) for z in lands], after)
    return out[0], out[1], out[2:2 + n], out[2 + n:2 + 2 * n], out[-1]


def gather_wait(started, after, name):
    send_sems, recv_sems, shards, lands, _ = started
    n = len(shards)

    def body(*refs):
        ins, zones = refs[:n], refs[n:2 * n]
        send_ref, recv_ref = refs[2 * n], refs[2 * n + 1]
        mx, my, mc = _me()
        for a in range(n):
            for k in range(FIRST):
                cp = pltpu.make_async_remote_copy(
                    src_ref=ins[a], dst_ref=zones[a].at[0], send_sem=send_ref.at[a * FIRST + k],
                    recv_sem=recv_ref.at[a * FIRST + k], device_id=(mx, my, 1 - mc), device_id_type=MESH)
                cp.wait_send()
                cp.wait_recv()

    out = pl.pallas_call(
        body, name=name,
        out_shape=(*[pltpu.HBM(s.shape, s.dtype) for s in shards], *[pltpu.HBM(z.shape, z.dtype) for z in lands]),
        in_specs=[HBM_SPEC] * (2 * n) + [SEM_SPEC, SEM_SPEC, ANY], out_specs=tuple([HBM_SPEC] * (2 * n)),
        input_output_aliases={a: a for a in range(2 * n)},
        compiler_params=pltpu.CompilerParams(has_side_effects=EFFECT),
    )(*shards, *lands, send_sems, recv_sems, after)
    return out[:n], out[n:]


def gather_forward(lands, name):
    n = len(lands)
    rel = N_CHIP - 1

    def body(*refs):
        zones, outs = refs[:n], refs[n:2 * n]
        send_sems, recv_sems = refs[2 * n:]
        mx, my, mc = _me()
        chips = _other_chips(mx, my)

        def copy(a, j, core):
            blk = _lin((*chips[j], core))
            return pltpu.make_async_remote_copy(
                src_ref=zones[a].at[blk], dst_ref=outs[a].at[blk], send_sem=send_sems.at[a * rel + j],
                recv_sem=recv_sems.at[a * rel + j], device_id=(mx, my, 1 - mc), device_id_type=MESH)

        sends = [copy(a, j, mc) for a in range(n) for j in range(rel)]
        for cp in sends:
            cp.start()
        for a in range(n):
            for j in range(rel):
                copy(a, j, 1 - mc).wait_recv()
        for cp in sends:
            cp.wait_send()

    return pl.pallas_call(
        body, name=name, out_shape=[_sds(z.shape, z.dtype) for z in lands],
        in_specs=[ANY] * n, out_specs=[ANY] * n, input_output_aliases={a: a for a in range(n)},
        scratch_shapes=[pltpu.SemaphoreType.DMA((n * rel,)), pltpu.SemaphoreType.DMA((n * rel,))],
    )(*lands)


def pair_exchange(gs, name):
    n = len(gs)

    def body(*refs):
        ins, outs = refs[:n], refs[n:2 * n]
        send_sems, recv_sems = refs[2 * n:]
        mx, my, mc = _me()
        cps = []
        for a in range(n):
            for q in range(N_CHIP):
                cp = pltpu.make_async_remote_copy(
                    src_ref=ins[a].at[2 * q + (1 - mc)], dst_ref=outs[a].at[q],
                    send_sem=send_sems.at[a * N_CHIP + q], recv_sem=recv_sems.at[a * N_CHIP + q],
                    device_id=(mx, my, 1 - mc), device_id_type=MESH)
                cp.start()
                cps.append(cp)
        for cp in cps:
            cp.wait()

    return pl.pallas_call(
        body, name=name, out_shape=[_sds((N_CHIP,) + g.shape[1:], g.dtype) for g in gs],
        in_specs=[ANY] * n, out_specs=[ANY] * n,
        scratch_shapes=[pltpu.SemaphoreType.DMA((n * N_CHIP,)), pltpu.SemaphoreType.DMA((n * N_CHIP,))],
    )(*gs)


def _tile2(rows, cols, row_mult):
    tr = _row_tile(rows, 512, row_mult)
    if tr < rows or rows * cols <= 2 ** 20:
        return tr, cols
    return rows, _row_tile(cols, 512, LANE)


def pair_add(core, g, got, name):
    _, rows, cols = g.shape
    tr, tc = _tile2(rows, cols, 16)

    def body(core_ref, g_ref, got_ref, h_ref):
        h_ref[...] = (g_ref[...].astype(F32) + got_ref[...].astype(F32)).astype(h_ref.dtype)

    blk = pl.BlockSpec((None, tr, tc), lambda q, i, j, c: (q, i, j))
    return pl.pallas_call(
        body, name=name, out_shape=_sds((N_CHIP, rows, cols), g.dtype),
        grid_spec=pltpu.PrefetchScalarGridSpec(
            num_scalar_prefetch=1, grid=(N_CHIP, rows // tr, cols // tc),
            in_specs=[pl.BlockSpec((None, tr, tc), lambda q, i, j, c: (2 * q + c[0], i, j)), blk],
            out_specs=blk),
        compiler_params=_params("parallel", "parallel", "parallel"),
    )(core, g, got)


def chip_routes(mx, my, mc):
    return [(2 * px + py, k, (px, py, mc)) for k, (px, py) in enumerate(_other_chips(mx, my))]


def pair_routes(mx, my, mc):
    return [(2 * q + (1 - mc), q, (mx, my, 1 - mc)) for q in range(N_CHIP)]


def exchange_start(hs, routes, name):
    n = len(hs)
    rel = len(routes(0, 0, 0))
    lands = [lax.empty((rel,) + h.shape[1:], h.dtype) for h in hs]

    def body(*refs):
        ins, zones = refs[:n], refs[n:2 * n]
        send_sems, recv_sems = refs[2 * n], refs[2 * n + 1]
        token = refs[-1]
        for a in range(n):
            for k, (src, slot, to) in enumerate(routes(*_me())):
                pltpu.make_async_remote_copy(
                    src_ref=ins[a].at[src], dst_ref=zones[a].at[slot], send_sem=send_sems.at[a * rel + k],
                    recv_sem=recv_sems.at[a * rel + k], device_id=to, device_id_type=MESH).start()
        token[...] = jnp.zeros_like(token)

    sems = pltpu.SemaphoreType.DMA((n * rel,))
    out = pl.pallas_call(
        body, name=name,
        out_shape=(sems, sems, *[pltpu.HBM(h.shape, h.dtype) for h in hs],
                   *[pltpu.HBM(z.shape, z.dtype) for z in lands], _sds((8, LANE), F32)),
        in_specs=[HBM_SPEC] * (2 * n), out_specs=(SEM_SPEC, SEM_SPEC, *[HBM_SPEC] * (2 * n), VMEM_SPEC),
        input_output_aliases={a: 2 + a for a in range(2 * n)},
        compiler_params=pltpu.CompilerParams(has_side_effects=EFFECT),
    )(*[_hbm(h) for h in hs], *[_hbm(z) for z in lands])
    return out[0], out[1], out[2:2 + n], out[2 + n:2 + 2 * n], out[-1]


def exchange_wait(started, routes, after, name):
    send_sems, recv_sems, hs, lands, _ = started
    n = len(hs)
    rel = len(routes(0, 0, 0))

    def body(*refs):
        ins, zones = refs[:n], refs[n:2 * n]
        send_ref, recv_ref = refs[2 * n], refs[2 * n + 1]
        for a in range(n):
            for k, (src, slot, to) in enumerate(routes(*_me())):
                cp = pltpu.make_async_remote_copy(
                    src_ref=ins[a].at[src], dst_ref=zones[a].at[slot], send_sem=send_ref.at[a * rel + k],
                    recv_sem=recv_ref.at[a * rel + k], device_id=to, device_id_type=MESH)
                cp.wait_send()
                cp.wait_recv()

    out = pl.pallas_call(
        body, name=name,
        out_shape=(*[pltpu.HBM(h.shape, h.dtype) for h in hs], *[pltpu.HBM(z.shape, z.dtype) for z in lands]),
        in_specs=[HBM_SPEC] * (2 * n) + [SEM_SPEC, SEM_SPEC, ANY], out_specs=tuple([HBM_SPEC] * (2 * n)),
        input_output_aliases={a: a for a in range(2 * n)},
        compiler_params=pltpu.CompilerParams(has_side_effects=EFFECT),
    )(*hs, *lands, send_sems, recv_sems, after)
    return out[:n], out[n:]


def _adam(w, g, m, v):
    m = ADAM_B1 * m + (1.0 - ADAM_B1) * g
    v = ADAM_B2 * v + (1.0 - ADAM_B2) * (g * g)
    m_hat = m / (1.0 - ADAM_B1 ** ADAM_STEP)
    v_hat = v / (1.0 - ADAM_B2 ** ADAM_STEP)
    delta = -ADAM_LR * (m_hat / (jnp.sqrt(v_hat) + ADAM_EPS) + ADAM_WD * w)
    return delta, m, v


def adamw_owned(chip, h, got, w, m, v, name):
    rows, cols = w.shape
    tr = _row_tile(rows, 256, 16)

    def body(chip_ref, h_ref, got_ref, w_ref, m_ref, v_ref, g_out, d_out, m_out, v_out):
        g = h_ref[...].astype(F32)
        for k in range(N_CHIP - 1):
            g = g + got_ref[k].astype(F32)
        d, mn, vn = _adam(w_ref[...], g, m_ref[...], v_ref[...])
        g_out[...] = g
        d_out[...] = d
        m_out[...] = mn
        v_out[...] = vn

    blk = pl.BlockSpec((tr, cols), lambda i, c: (i, 0))
    return pl.pallas_call(
        body, name=name, out_shape=[_sds((rows, cols), F32)] * 4,
        grid_spec=pltpu.PrefetchScalarGridSpec(
            num_scalar_prefetch=1, grid=(rows // tr,),
            in_specs=[pl.BlockSpec((None, tr, cols), lambda i, c: (c[0], i, 0)),
                      pl.BlockSpec((N_CHIP - 1, tr, cols), lambda i, c: (0, i, 0)), blk, blk, blk],
            out_specs=[blk] * 4),
        compiler_params=_params("parallel"),
    )(chip, h, got, w, m, v)


def owned_sum(chip, h, got, name):
    _, rows, cols = h.shape
    tr, tc = _tile2(rows, cols, 16)

    def body(chip_ref, h_ref, got_ref, g_out):
        g = h_ref[...].astype(F32)
        for k in range(N_CHIP - 1):
            g = g + got_ref[k].astype(F32)
        g_out[...] = g

    return pl.pallas_call(
        body, name=name, out_shape=_sds((rows, cols), F32),
        grid_spec=pltpu.PrefetchScalarGridSpec(
            num_scalar_prefetch=1, grid=(rows // tr, cols // tc),
            in_specs=[pl.BlockSpec((None, tr, tc), lambda i, j, c: (c[0], i, j)),
                      pl.BlockSpec((N_CHIP - 1, tr, tc), lambda i, j, c: (0, i, j))],
            out_specs=pl.BlockSpec((tr, tc), lambda i, j, c: (i, j))),
        compiler_params=_params("parallel", "parallel"),
    )(chip, h, got)


def adamw_rows(sources, where, ws, ms, vs, name):
    n_src, n_par = len(sources), len(ws)

    def body(*refs):
        srcs = refs[:n_src]
        w_refs, m_refs, v_refs = (refs[n_src + j * n_par:n_src + (j + 1) * n_par] for j in range(3))
        outs = refs[n_src + 3 * n_par:]
        for k in range(n_par):
            src, off = srcs[where[k][0]], where[k][1]
            n = w_refs[k].shape[1]
            g = src[0, :, off:off + n]
            for dev in range(1, N_DEV):
                g = g + src[dev, :, off:off + n]
            d, mn, vn = _adam(w_refs[k][...], g, m_refs[k][...], v_refs[k][...])
            for o_ref, val in zip(outs[4 * k:4 * k + 4], (g, d, mn, vn)):
                o_ref[...] = val

    flat = pl.pallas_call(
        body, name=name, out_shape=[_sds(x.shape, F32) for x in ws for _ in range(4)],
        in_specs=[VMEM_SPEC] * (n_src + 3 * n_par), out_specs=[VMEM_SPEC] * (4 * n_par),
        compiler_params=pltpu.CompilerParams(vmem_limit_bytes=VMEM_LIMIT),
    )(*sources, *ws, *ms, *vs)
    return [flat[4 * k:4 * k + 4] for k in range(n_par)]


def adamw_sum(parts, w, m, v, name):
    n_parts, rows, cols = parts.shape
    tr = _row_tile(rows, 256, 8)

    def body(p_ref, w_ref, m_ref, v_ref, g_out, d_out, m_out, v_out):
        g = p_ref[0]
        for k in range(1, n_parts):
            g = g + p_ref[k]
        d, mn, vn = _adam(w_ref[...], g, m_ref[...], v_ref[...])
        g_out[...] = g
        d_out[...] = d
        m_out[...] = mn
        v_out[...] = vn

    blk = pl.BlockSpec((tr, cols), lambda i: (i, 0))
    return pl.pallas_call(
        body, name=name, out_shape=[_sds((rows, cols), F32)] * 4, grid=(rows // tr,),
        in_specs=[pl.BlockSpec((n_parts, tr, cols), lambda i: (0, i, 0)), blk, blk, blk],
        out_specs=[blk] * 4, compiler_params=_params("parallel"),
    )(parts, w, m, v)


def adaln_cols(c_all, w, b, name):
    d, n = w.shape
    tn = _row_tile(n, 768, LANE)

    def body(c_ref, w_ref, b_ref, o_ref):
        c = c_ref[...]
        o_ref[...] = jnp.dot(c * _sigmoid(c), w_ref[...], preferred_element_type=F32, precision=HI) + b_ref[...]

    return pl.pallas_call(
        body, name=name, out_shape=_sds((N_DEV, n), F32), grid=(n // tn,),
        in_specs=[pl.BlockSpec((N_DEV, d), lambda j: (0, 0)), pl.BlockSpec((d, tn), lambda j: (0, j)),
                  pl.BlockSpec((1, tn), lambda j: (0, j))],
        out_specs=pl.BlockSpec((N_DEV, tn), lambda j: (0, j)), compiler_params=_params("parallel"),
    )(c_all, w, b)


def adaln_wgrad(c_all, dmod_cols, name):
    d = c_all.shape[1]
    n = dmod_cols.shape[1]
    tn = _row_tile(n, 768, LANE)

    def body(c_ref, g_ref, o_ref):
        c = c_ref[...]
        o_ref[...] = lax.dot_general(c * _sigmoid(c), g_ref[...], (((0,), (0,)), ((), ())),
                                     preferred_element_type=F32, precision=HI)

    return pl.pallas_call(
        body, name=name, out_shape=_sds((d, n), F32), grid=(n // tn,),
        in_specs=[pl.BlockSpec((N_DEV, d), lambda j: (0, 0)), pl.BlockSpec((N_DEV, tn), lambda j: (0, j))],
        out_specs=pl.BlockSpec((d, tn), lambda j: (0, j)), compiler_params=_params("parallel"),
    )(c_all, dmod_cols)


def _modulate(h, sh, sc):
    return (h * (1.0 + sc) + sh).astype(BF16)


def ffn_in(h, sh, sc, w3, name):
    t, d = h.shape
    nb, _, bw = w3.shape
    half = nb // 2
    tm = _row_tile(t, 512, 16)

    def body(h_ref, sh_ref, sc_ref, wa_ref, wb_ref, a_ref, b_ref, s_ref):
        u = _modulate(h_ref[...], sh_ref[...], sc_ref[...])
        a = _dot(u, wa_ref[...])
        b = _dot(u, wb_ref[...])
        a_ref[...] = a.astype(BF16)
        b_ref[...] = b.astype(BF16)
        s_ref[...] = (a * _sigmoid(a) * b).astype(BF16)

    vec = pl.BlockSpec((1, d), lambda j, i: (0, 0))
    out = pl.BlockSpec((tm, bw), lambda j, i: (i, j))
    return pl.pallas_call(
        body, name=name, out_shape=[_sds((t, half * bw), BF16)] * 3, grid=(half, t // tm),
        in_specs=[pl.BlockSpec((tm, d), lambda j, i: (i, 0)), vec, vec,
                  pl.BlockSpec((None, d, bw), lambda j, i: (j, 0, 0), pipeline_mode=pl.Buffered(1)),
                  pl.BlockSpec((None, d, bw), lambda j, i: (j + half, 0, 0), pipeline_mode=pl.Buffered(1))],
        out_specs=[out] * 3, compiler_params=_params("parallel", "parallel"),
    )(h, sh, sc, w3, w3)


def mod_matmul(h, sh, sc, wt, bw, name):
    t, d = h.shape
    n = wt.shape[0]
    tm = _row_tile(t, 512, 16)

    def body(h_ref, sh_ref, sc_ref, w_ref, o_ref):
        o_ref[...] = _dot_nt(_modulate(h_ref[...], sh_ref[...], sc_ref[...]), w_ref[...]).astype(BF16)

    vec = pl.BlockSpec((1, d), lambda j, i: (0, 0))
    return pl.pallas_call(
        body, name=name, out_shape=_sds((t, n), BF16), grid=(n // bw, t // tm),
        in_specs=[pl.BlockSpec((tm, d), lambda j, i: (i, 0)), vec, vec, pl.BlockSpec((bw, d), lambda j, i: (j, 0))],
        out_specs=pl.BlockSpec((tm, bw), lambda j, i: (i, j)), compiler_params=_params("parallel", "parallel"),
    )(h, sh, sc, wt)


def out_ln(s, w, hin, gmod, ln_g, ln_b, coef, name):
    t, kdim = s.shape
    d = w.shape[1]
    tm = _row_tile(t, 256, 16)

    def body(s_ref, w_ref, hin_ref, gm_ref, g_ref, b_ref, f_ref, z_ref, h_ref):
        f = _dot(s_ref[...], w_ref[...])
        z = ALPHA * hin_ref[...] + (coef * gm_ref[...]) * f
        mu = jnp.mean(z, axis=-1, keepdims=True)
        zc = z - mu
        var = jnp.mean(zc * zc, axis=-1, keepdims=True)
        f_ref[...] = f.astype(BF16)
        z_ref[...] = z
        h_ref[...] = zc * lax.rsqrt(var + LN_EPS) * g_ref[...] + b_ref[...]

    vec = pl.BlockSpec((1, d), lambda i: (0, 0))
    row = pl.BlockSpec((tm, d), lambda i: (i, 0))
    return pl.pallas_call(
        body, name=name, out_shape=[_sds((t, d), BF16), _sds((t, d), F32), _sds((t, d), F32)],
        grid=(t // tm,),
        in_specs=[pl.BlockSpec((tm, kdim), lambda i: (i, 0)),
                  pl.BlockSpec((kdim, d), lambda i: (0, 0), pipeline_mode=pl.Buffered(1)), row, vec, vec, vec],
        out_specs=[row, row, row], compiler_params=_params("parallel"),
    )(s, w, hin, gmod, ln_g, ln_b)


def ln_bwd(dh, z, f, ln_g, gmod, coef, name, target=None):
    t, d = z.shape
    tm = _row_tile(t, 256, 16)
    head = target is not None

    def body(*refs):
        if head:
            dh_ref, tg_ref, z_ref, f_ref, g_ref, gm_ref, dz_ref, df_ref, dg_ref, db_ref, dgm_ref, loss_ref = refs
        else:
            dh_ref, z_ref, f_ref, g_ref, gm_ref, dz_ref, df_ref, dg_ref, db_ref, dgm_ref = refs
        i = pl.program_id(0)

        @pl.when(i == 0)
        def _():
            dg_ref[...] = jnp.zeros_like(dg_ref)
            db_ref[...] = jnp.zeros_like(db_ref)
            dgm_ref[...] = jnp.zeros_like(dgm_ref)
            if head:
                loss_ref[...] = jnp.zeros_like(loss_ref)

        dh = dh_ref[...]
        if head:
            err = dh - tg_ref[...]
            loss_ref[...] += 0.5 * jnp.sum(jnp.mean(err * err, axis=-1, keepdims=True))
            dh = err / d
        zv = z_ref[...]
        mu = jnp.mean(zv, axis=-1, keepdims=True)
        zc = zv - mu
        rstd = lax.rsqrt(jnp.mean(zc * zc, axis=-1, keepdims=True) + LN_EPS)
        xhat = zc * rstd
        dxh = dh * g_ref[...]
        dz = rstd * (dxh - jnp.mean(dxh, axis=-1, keepdims=True)
                     - xhat * jnp.mean(dxh * xhat, axis=-1, keepdims=True))
        dz_ref[...] = dz
        df_ref[...] = ((coef * gm_ref[...]) * dz).astype(BF16)
        dg_ref[...] += _colsum(dh * xhat)
        db_ref[...] += _colsum(dh)
        dgm_ref[...] += _colsum(coef * f_ref[...].astype(F32) * dz)

    vec = pl.BlockSpec((1, d), lambda i: (0, 0))
    row = pl.BlockSpec((tm, d), lambda i: (i, 0))
    ins = [dh] + ([target] if head else []) + [z, f, ln_g, gmod]
    in_specs = [row] + ([row] if head else []) + [row, row, vec, vec]
    out_shape = [_sds((t, d), F32), _sds((t, d), BF16)] + [_sds((1, d), F32)] * 3
    out_specs = [row, row, vec, vec, vec]
    if head:
        out_shape.append(_sds((1, LANE), F32))
        out_specs.append(pl.BlockSpec((1, LANE), lambda i: (0, 0)))
    return pl.pallas_call(
        body, name=name, out_shape=out_shape, grid=(t // tm,), in_specs=in_specs, out_specs=out_specs,
        compiler_params=_params("arbitrary"),
    )(*ins)


def ffn_bwd_act(df, w, a, b, name):
    t, d = df.shape
    fdim = w.shape[0]
    bw = fdim // (N_DEV // 2)
    tm = _row_tile(t, 512, 16)

    def body(df_ref, w_ref, a_ref, b_ref, o_ref):
        ds = _dot_nt(df_ref[...], w_ref[...])
        av = a_ref[...].astype(F32)
        sg = _sigmoid(av)
        o_ref[0] = (ds * b_ref[...].astype(F32) * (sg * (1.0 + av * (1.0 - sg)))).astype(BF16)
        o_ref[1] = (ds * (av * sg)).astype(BF16)

    act = pl.BlockSpec((tm, bw), lambda j, i: (i, j))
    return pl.pallas_call(
        body, name=name, out_shape=_sds((2, t, fdim), BF16), grid=(fdim // bw, t // tm),
        in_specs=[pl.BlockSpec((tm, d), lambda j, i: (i, 0)), pl.BlockSpec((bw, d), lambda j, i: (j, 0)), act, act],
        out_specs=pl.BlockSpec((2, tm, bw), lambda j, i: (0, i, j)),
        compiler_params=_params("parallel", "parallel"),
    )(df, w, a, b)


def matmul_tn(name, a, a_block, a_map, b, b_block, b_map, out_shape, o_block, o_map, n_out, mod=None,
              mod_b=False):
    tk = [s for s in a_block if s is not None][0]
    nk = a.shape[-2] // tk
    m, nn = [s for s in o_block if s is not None]

    def body(*refs):
        if mod is None:
            a_ref, b_ref, o_ref, acc = refs
        else:
            a_ref, sh_ref, sc_ref, b_ref, o_ref, acc = refs
        k = pl.program_id(1)

        @pl.when(k == 0)
        def _():
            acc[...] = jnp.zeros_like(acc)

        av, bv = a_ref[...], b_ref[...]
        if mod is not None and mod_b:
            bv = _modulate(bv, sh_ref[...], sc_ref[...])
        elif mod is not None:
            av = _modulate(av, sh_ref[...], sc_ref[...])
        acc[...] += _dot_tn(av, bv)

        @pl.when(k == nk - 1)
        def _():
            o_ref[...] = acc[...].astype(o_ref.dtype)

    ins = [a] + (list(mod) if mod is not None else []) + [b]
    in_specs = [pl.BlockSpec(a_block, a_map)]
    if mod is not None:
        vec = pl.BlockSpec((1, mod[0].shape[1]), lambda n, k: (0, 0))
        in_specs += [vec, vec]
    in_specs.append(pl.BlockSpec(b_block, b_map))
    return pl.pallas_call(
        body, name=name, out_shape=out_shape, grid=(n_out, nk), in_specs=in_specs,
        out_specs=pl.BlockSpec(o_block, o_map), scratch_shapes=[pltpu.VMEM((m, nn), F32)],
        compiler_params=_params("parallel", "arbitrary"),
    )(*ins)


def matmul_nt_blocks(name, dy, dy_block, dy_map, w3, t, resid=None):
    tm, bw = [s for s in dy_block if s is not None]
    rows = w3.ndim == 2
    if rows:
        nk, n = w3.shape[0] // bw, w3.shape[1]
    else:
        nk, n, _ = w3.shape

    def body(*refs):
        if resid is None:
            dy_ref, w_ref, o_ref, acc = refs
        else:
            dy_ref, w_ref, dz_ref, hin_ref, sc_ref, o_ref, dsc_ref, dsh_ref, acc = refs
        i, k = pl.program_id(0), pl.program_id(1)

        @pl.when(k == 0)
        def _():
            acc[...] = jnp.zeros_like(acc)

        if resid is not None:
            @pl.when((k == 0) & (i == 0))
            def _():
                dsc_ref[...] = jnp.zeros_like(dsc_ref)
                dsh_ref[...] = jnp.zeros_like(dsh_ref)

        acc[...] += _dot(dy_ref[...], w_ref[...]) if rows else _dot_nt(dy_ref[...], w_ref[...])

        @pl.when(k == nk - 1)
        def _():
            du = acc[...]
            if resid is None:
                o_ref[...] = du.astype(o_ref.dtype)
            else:
                o_ref[...] = ALPHA * dz_ref[...] + du * (1.0 + sc_ref[...])
                dsc_ref[...] += _colsum(du * hin_ref[...])
                dsh_ref[...] += _colsum(du)

    row = pl.BlockSpec((tm, n), lambda i, k: (i, 0))
    vec = pl.BlockSpec((1, n), lambda i, k: (0, 0))
    w_spec = pl.BlockSpec((bw, n), lambda i, k: (k, 0)) if rows else pl.BlockSpec((None, n, bw), lambda i, k: (k, 0, 0))
    in_specs = [pl.BlockSpec(dy_block, dy_map), w_spec]
    ins = [dy, w3]
    if resid is None:
        out_shape, out_specs = _sds((t, n), BF16), row
    else:
        ins += list(resid)
        once = pl.BlockSpec((tm, n), lambda i, k: (i, 0), pipeline_mode=pl.Buffered(1))
        in_specs += [once, once, vec]
        out_shape = [_sds((t, n), F32), _sds((1, n), F32), _sds((1, n), F32)]
        out_specs = [row, vec, vec]
    return pl.pallas_call(
        body, name=name, out_shape=out_shape, grid=(t // tm, nk), in_specs=in_specs, out_specs=out_specs,
        scratch_shapes=[pltpu.VMEM((tm, n), F32)], compiler_params=_params("arbitrary", "arbitrary"),
    )(*ins)


REL_W = KW + QB


def bias_table(rel_bias):
    nh, n_rel = rel_bias.shape
    lo = KW - QB - REL_CLIP
    hi = KW - lo - n_rel
    assert n_rel == REL_CLIP + CHUNK and lo >= 0 and hi >= 0
    first, last = rel_bias[:, :1], rel_bias[:, -1:]
    row = jnp.concatenate([jnp.broadcast_to(first, (nh, lo)), rel_bias, jnp.broadcast_to(last, (nh, hi)),
                           jnp.broadcast_to(first, (nh, QB))], axis=1)
    table = jnp.tile(row, (1, QB))[:, :QB * (REL_W - 1)].reshape(nh, QB, REL_W - 1)[:, :, :KW]
    q = np.arange(QB)[:, None] // CHUNK
    k = np.arange(KW)[None, :] // CHUNK
    band = (k >= q) & (k <= q + A_PAST_CHUNKS)
    return jnp.where(band[None], table, NEG)


def bias_grad_skew(dbias):
    nh = dbias.shape[0]
    flat = jnp.pad(dbias, ((0, 0), (0, 0), (0, REL_W - 1 - KW))).reshape(nh, QB * (REL_W - 1))
    return jnp.pad(flat, ((0, 0), (0, QB))).reshape(nh, QB, REL_W)


def bias_clip_map(n_rel):
    m = np.arange(REL_W)
    dist = np.where(m < KW, m, m - REL_W) - (KW - QB)
    idx = np.clip(dist, -REL_CLIP, CHUNK - 1) + REL_CLIP
    return (idx[:, None] == np.arange(n_rel)[None, :]).astype(np.float32)


PAIR = 2


def _pair_specs(col, rows_of):
    return [pl.BlockSpec((QB, LANE), functools.partial(lambda r, h, i: (rows_of(r, i), col // LANE + h), r))
            for r in range(3)]


def _earlier(r, i):
    return jnp.maximum(i - 2 + r, 0)


def _head_lanes(hh, dh):
    lane = lax.broadcasted_iota(jnp.int32, (1, LANE), 1)
    return (lane < dh) if hh == 0 else (lane >= dh)


def _only(x, lanes):
    return jnp.where(lanes, x, jnp.zeros_like(x))


def _scores(q, ks, bias, i, scale):
    s = jnp.concatenate([_dot_nt(q, kk) for kk in ks], axis=1) * scale + bias
    col = lax.broadcasted_iota(jnp.int32, s.shape, 1)
    return jnp.where(col >= (2 - i) * QB, s, NEG)


def attn_fwd(p, cols, bias, dh, name):
    t = p.shape[0]
    nh = bias.shape[0]
    scale = dh ** -0.5

    def body(q_ref, k0, k1, k2, v0, v1, v2, b_ref, o_ref, lse_ref):
        i = pl.program_id(1)
        q = q_ref[...]
        outs = []
        for hh in range(PAIR):
            lanes = _head_lanes(hh, dh)
            s = _scores(q, [_only(kk[...], lanes) for kk in (k0, k1, k2)], b_ref[hh], i, scale)
            m = jnp.max(s, axis=-1, keepdims=True)
            e = jnp.exp(s - m)
            l = jnp.sum(e, axis=-1, keepdims=True)
            eb = e.astype(BF16)
            o = sum(_dot(eb[:, r * QB:(r + 1) * QB], vv[...]) for r, vv in enumerate((v0, v1, v2)))
            outs.append(o / l)
            lse_ref[hh] = m + jnp.log(l)
        o_ref[...] = jnp.where(_head_lanes(0, dh), outs[0], outs[1]).astype(BF16)

    st = pl.BlockSpec((PAIR, QB, 1), lambda h, i: (h, i, 0))
    return pl.pallas_call(
        body, name=name, out_shape=[_sds((t, nh * dh), BF16), _sds((nh, t, 1), F32)], grid=(nh // PAIR, t // QB),
        in_specs=[pl.BlockSpec((QB, LANE), lambda h, i: (i, cols["qa"] // LANE + h))]
        + _pair_specs(cols["ka"], _earlier) + _pair_specs(cols["va"], _earlier)
        + [pl.BlockSpec((PAIR, QB, KW), lambda h, i: (h, 0, 0))],
        out_specs=[pl.BlockSpec((QB, LANE), lambda h, i: (i, h)), st],
        compiler_params=_params("parallel", "parallel"),
    )(p, p, p, p, p, p, p, bias)


def attn_bwd(p, cols, bias, lse, dy, dh, name):
    t = p.shape[0]
    nh = bias.shape[0]
    nb = t // QB
    scale = dh ** -0.5

    def body(q_ref, k0, k1, k2, v0, v1, v2, b_ref, lse_ref, dy_ref, dq_ref, dk_ref, dv_ref, db_ref, dk_acc, dv_acc):
        i = pl.program_id(1)

        @pl.when(i == 0)
        def _():
            db_ref[...] = jnp.zeros_like(db_ref)
            dk_acc[...] = jnp.zeros_like(dk_acc)
            dv_acc[...] = jnp.zeros_like(dv_acc)

        q, dyv = q_ref[...], dy_ref[...]
        ks = [k0[...], k1[...], k2[...]]
        dqs, dks, dvs = [], [], []
        for hh in range(PAIR):
            lanes = _head_lanes(hh, dh)
            s = _scores(q, [_only(kk, lanes) for kk in ks], b_ref[hh], i, scale)
            prob = jnp.exp(s - lse_ref[hh])
            dprob = jnp.concatenate([_dot_nt(dyv, _only(vv[...], lanes)) for vv in (v0, v1, v2)], axis=1)
            delta = jnp.sum(prob * dprob, axis=-1, keepdims=True)
            ds = prob * (dprob - delta)
            dsb, pb = ds.astype(BF16), prob.astype(BF16)
            dqs.append(sum(_dot(dsb[:, r * QB:(r + 1) * QB], kk) for r, kk in enumerate(ks)))
            dks.append([_dot_tn(dsb[:, r * QB:(r + 1) * QB], q) for r in range(3)])
            dvs.append([_dot_tn(pb[:, r * QB:(r + 1) * QB], dyv) for r in range(3)])
            db_ref[hh] += ds
        first = _head_lanes(0, dh)
        dq_ref[...] = (jnp.where(first, dqs[0], dqs[1]) * scale).astype(BF16)
        for r in range(3):
            rows = pl.ds(pl.multiple_of(_earlier(r, i) * QB, QB), QB)
            dk_acc[rows, :] += jnp.where(first, dks[0][r], dks[1][r])
            dv_acc[rows, :] += jnp.where(first, dvs[0][r], dvs[1][r])

        @pl.when(i == nb - 1)
        def _():
            dk_ref[...] = (dk_acc[...] * scale).astype(BF16)
            dv_ref[...] = dv_acc[...].astype(BF16)

    st = pl.BlockSpec((PAIR, QB, 1), lambda h, i: (h, i, 0))
    tab = pl.BlockSpec((PAIR, QB, KW), lambda h, i: (h, 0, 0))
    own = pl.BlockSpec((QB, LANE), lambda h, i: (i, h))
    whole = pl.BlockSpec((t, LANE), lambda h, i: (0, h))
    return pl.pallas_call(
        body, name=name,
        out_shape=[_sds((t, nh * dh), BF16)] * 3 + [_sds((nh, QB, KW), F32)],
        grid=(nh // PAIR, nb),
        in_specs=[pl.BlockSpec((QB, LANE), lambda h, i: (i, cols["qa"] // LANE + h))]
        + _pair_specs(cols["ka"], _earlier) + _pair_specs(cols["va"], _earlier) + [tab, st, own],
        out_specs=[own, whole, whole, tab],
        scratch_shapes=[pltpu.VMEM((t, LANE), F32), pltpu.VMEM((t, LANE), F32)],
        compiler_params=_params("parallel", "arbitrary"),
    )(p, p, p, p, p, p, p, bias, lse, dy)


def _tri(strict):
    r = lax.broadcasted_iota(jnp.int32, (CHUNK, CHUNK), 0)
    c = lax.broadcasted_iota(jnp.int32, (CHUNK, CHUNK), 1)
    return jnp.where((c < r) if strict else (c <= r), 1.0, 0.0).astype(F32)


def _gate(lr, wa, ba):
    y = _dot(lr, wa) + ba
    return (jnp.minimum(y, 0.0) - jnp.log(1.0 + jnp.exp(-jnp.abs(y)))) / GATE_TAU, y


def _decays(la):
    cum = jnp.dot(_tri(False), la, preferred_element_type=F32, precision=HI)
    last = cum[CHUNK - 1:CHUNK, :]
    return jnp.exp(last - cum), jnp.exp(last)


def _gla_specs(cols, hk, hv, order):
    def at(start, width):
        return pl.BlockSpec((GB, width), lambda h, i: (order(i), start // width + h))
    return [at(cols["qb"], hk), at(cols["kb"], hk), at(cols["vb"], hv), at(cols["rb"], hv),
            pl.BlockSpec((GB, LANE), lambda h, i: (order(i), cols["lr"] // LANE))]


def gla_fwd(p, cols, wa, ba, gn, nh, hk, hv, name):
    t = p.shape[0]
    nc = t // CHUNK
    scale = hk ** -0.5
    per = GB // CHUNK

    def body(q_ref, k_ref, v_ref, r_ref, lr_ref, wa_ref, ba_ref, gn_ref, o_ref, y_ref, st_ref, state):
        @pl.when(pl.program_id(1) == 0)
        def _():
            state[...] = jnp.zeros_like(state)

        for c in range(per):
            rows = pl.ds(c * CHUNK, CHUNK)
            la, _ = _gate(lr_ref[rows, :], wa_ref[...], ba_ref[...])
            w, decay = _decays(la)
            kdec = (k_ref[rows, :].astype(F32) * w).astype(BF16)
            st = decay * state[...] + _dot_tn(v_ref[rows, :], kdec)
            state[...] = st
            st_ref[c] = st
            o = _dot_nt(q_ref[rows, :], st.astype(BF16)) * scale
            o_ref[rows, :] = o
            rinv = lax.rsqrt(jnp.mean(o * o, axis=-1, keepdims=True) + RMS_EPS)
            rv = r_ref[rows, :].astype(F32)
            y_ref[rows, :] = (o * rinv * gn_ref[...] * (rv * _sigmoid(rv))).astype(BF16)

    return pl.pallas_call(
        body, name=name,
        out_shape=[_sds((t, nh * hv), F32), _sds((t, nh * hv), BF16), _sds((nh, nc, hv, hk), F32)],
        grid=(nh, t // GB),
        in_specs=_gla_specs(cols, hk, hv, lambda i: i)
        + [pl.BlockSpec((LANE, hk), lambda h, i: (0, h)), pl.BlockSpec((1, hk), lambda h, i: (0, h)),
           pl.BlockSpec((1, hv), lambda h, i: (0, 0))],
        out_specs=[pl.BlockSpec((GB, hv), lambda h, i: (i, h)), pl.BlockSpec((GB, hv), lambda h, i: (i, h)),
                   pl.BlockSpec((None, per, hv, hk), lambda h, i: (h, i, 0, 0))],
        scratch_shapes=[pltpu.VMEM((hv, hk), F32)], compiler_params=_params("parallel", "arbitrary"),
    )(p, p, p, p, p, wa, ba, gn)


def gla_bwd(p, cols, wa, ba, gn, o, states, dy, nh, hk, hv, name):
    t = p.shape[0]
    nblk = t // GB
    scale = hk ** -0.5
    per = GB // CHUNK

    def rev(i):
        return nblk - 1 - i

    def body(q_ref, k_ref, v_ref, r_ref, lr_ref, wa_ref, ba_ref, gn_ref, o_ref, st_ref, sp_ref, dy_ref,
             dq_ref, dk_ref, dv_ref, dr_ref, dg_ref, dgn_ref, carry):
        h, i = pl.program_id(0), pl.program_id(1)

        @pl.when(i == 0)
        def _():
            carry[...] = jnp.zeros_like(carry)

        @pl.when((i == 0) & (h == 0))
        def _():
            dgn_ref[...] = jnp.zeros_like(dgn_ref)

        gnv = gn_ref[...]
        for c in reversed(range(per)):
            rows = pl.ds(c * CHUNK, CHUNK)
            rv = r_ref[rows, :].astype(F32)
            sg = _sigmoid(rv)
            dyv = dy_ref[rows, :].astype(F32)
            ov = o_ref[rows, :]
            rinv = lax.rsqrt(jnp.mean(ov * ov, axis=-1, keepdims=True) + RMS_EPS)
            dn = dyv * (rv * sg)
            dr_ref[rows, :] = (dyv * (ov * rinv * gnv) * (sg * (1.0 + rv * (1.0 - sg)))).astype(BF16)
            dgn_ref[...] += _colsum(dn * ov * rinv)
            dxh = dn * gnv
            do = rinv * dxh - ov * (rinv * rinv * rinv) * jnp.mean(dxh * ov, axis=-1, keepdims=True)
            dob = (do * scale).astype(BF16)
            qv, kv, vv = q_ref[rows, :], k_ref[rows, :], v_ref[rows, :]
            dq_ref[rows, :] = _dot(dob, st_ref[c].astype(BF16)).astype(BF16)
            dst = carry[...] + _dot_tn(dob, qv)
            if c > 0:
                prev = st_ref[c - 1]
            else:
                prev = jnp.where(i == nblk - 1, 0.0, sp_ref[0])
            ddecay = _colsum(dst * prev)
            la, y = _gate(lr_ref[rows, :], wa_ref[...], ba_ref[...])
            w, decay = _decays(la)
            kf = kv.astype(F32)
            kdec = (kf * w).astype(BF16)
            dstb = dst.astype(BF16)
            dkdec = _dot(vv, dstb)
            dv_ref[rows, :] = _dot_nt(kdec, dstb).astype(BF16)
            dk_ref[rows, :] = (dkdec * w).astype(BF16)
            e = dkdec * kf * w
            dla = jnp.dot(_tri(True), e, preferred_element_type=F32, precision=HI) + ddecay * decay
            dg_ref[rows, :] = dla * (1.0 / GATE_TAU) * _sigmoid(-y)
            carry[...] = decay * dst

    per_head = lambda width: pl.BlockSpec((GB, width), lambda h, i: (rev(i), h))
    return pl.pallas_call(
        body, name=name,
        out_shape=[_sds((t, nh * hk), BF16), _sds((t, nh * hk), BF16), _sds((t, nh * hv), BF16),
                   _sds((t, nh * hv), BF16), _sds((t, nh * hk), F32), _sds((1, hv), F32)],
        grid=(nh, nblk),
        in_specs=_gla_specs(cols, hk, hv, rev)
        + [pl.BlockSpec((LANE, hk), lambda h, i: (0, h)), pl.BlockSpec((1, hk), lambda h, i: (0, h)),
           pl.BlockSpec((1, hv), lambda h, i: (0, 0)), per_head(hv),
           pl.BlockSpec((None, per, hv, hk), lambda h, i: (h, rev(i), 0, 0)),
           pl.BlockSpec((None, 1, hv, hk), lambda h, i: (h, jnp.maximum(rev(i) * per - 1, 0), 0, 0)),
           per_head(hv)],
        out_specs=[per_head(hk), per_head(hk), per_head(hv), per_head(hv), per_head(hk),
                   pl.BlockSpec((1, hv), lambda h, i: (0, 0))],
        scratch_shapes=[pltpu.VMEM((hv, hk), F32)], compiler_params=_params("arbitrary", "arbitrary"),
    )(p, p, p, p, p, wa, ba, gn, o, states, states, dy)


def gate_bwd(p, lr_col, dg, wa, name):
    t, kd = dg.shape
    tm = _row_tile(t, 512, 16)

    def body(lr_ref, dg_ref, wa_ref, dlr_ref, dwa_ref, dba_ref):
        @pl.when(pl.program_id(0) == 0)
        def _():
            dwa_ref[...] = jnp.zeros_like(dwa_ref)
            dba_ref[...] = jnp.zeros_like(dba_ref)

        g = dg_ref[...]
        gb = g.astype(BF16)
        dlr_ref[...] = _dot_nt(gb, wa_ref[...]).astype(BF16)
        dwa_ref[...] += _dot_tn(lr_ref[...], gb)
        dba_ref[...] += _colsum(g)

    return pl.pallas_call(
        body, name=name, out_shape=[_sds((t, LANE), BF16), _sds((LANE, kd), F32), _sds((1, kd), F32)],
        grid=(t // tm,),
        in_specs=[pl.BlockSpec((tm, LANE), lambda i: (i, lr_col // LANE)), pl.BlockSpec((tm, kd), lambda i: (i, 0)),
                  pl.BlockSpec((LANE, kd), lambda i: (0, 0))],
        out_specs=[pl.BlockSpec((tm, LANE), lambda i: (i, 0)), pl.BlockSpec((LANE, kd), lambda i: (0, 0)),
                   pl.BlockSpec((1, kd), lambda i: (0, 0))],
        compiler_params=_params("arbitrary"),
    )(p, dg, wa)


def proj_merge(ya, yb, wa3, wb3, p, ga_col, gb_col, name):
    t, kd = ya.shape
    nb, _, bw = wa3.shape
    tm = _row_tile(t, 512, 16)

    def body(ya_ref, yb_ref, wa_ref, wb_ref, ga_ref, gb_ref, pa_ref, pb_ref, mg_ref):
        pa = _dot(ya_ref[...], wa_ref[...])
        pb = _dot(yb_ref[...], wb_ref[...])
        pa_ref[...] = pa.astype(BF16)
        pb_ref[...] = pb.astype(BF16)
        mg_ref[...] = (_sigmoid(ga_ref[...].astype(F32)) * pa + _sigmoid(gb_ref[...].astype(F32)) * pb).astype(BF16)

    act = pl.BlockSpec((tm, kd), lambda j, i: (i, 0))
    wsp = pl.BlockSpec((None, kd, bw), lambda j, i: (j, 0, 0))
    out = pl.BlockSpec((tm, bw), lambda j, i: (i, j))
    return pl.pallas_call(
        body, name=name, out_shape=[_sds((t, nb * bw), BF16)] * 3, grid=(nb, t // tm),
        in_specs=[act, act, wsp, wsp, pl.BlockSpec((tm, bw), lambda j, i: (i, ga_col // bw + j)),
                  pl.BlockSpec((tm, bw), lambda j, i: (i, gb_col // bw + j))],
        out_specs=[out] * 3, compiler_params=_params("parallel", "parallel"),
    )(ya, yb, wa3, wb3, p, p)


def merge_bwd(dm, w, p, ga_col, gb_col, pa, pb, name):
    t, d = dm.shape
    n = w.shape[0]
    tn = _row_tile(n, 512, LANE)
    tm = _row_tile(t, 256, 16)

    def body(dm_ref, w_ref, ga_ref, gb_ref, pa_ref, pb_ref, dpa_ref, dpb_ref, dga_ref, dgb_ref):
        dmg = _dot_nt(dm_ref[...], w_ref[...])
        sa = _sigmoid(ga_ref[...].astype(F32))
        sb = _sigmoid(gb_ref[...].astype(F32))
        dpa_ref[...] = (dmg * sa).astype(BF16)
        dpb_ref[...] = (dmg * sb).astype(BF16)
        dga_ref[...] = (dmg * pa_ref[...].astype(F32) * sa * (1.0 - sa)).astype(BF16)
        dgb_ref[...] = (dmg * pb_ref[...].astype(F32) * sb * (1.0 - sb)).astype(BF16)

    out = pl.BlockSpec((tm, tn), lambda j, i: (i, j))
    return pl.pallas_call(
        body, name=name, out_shape=[_sds((t, n), BF16)] * 4, grid=(n // tn, t // tm),
        in_specs=[pl.BlockSpec((tm, d), lambda j, i: (i, 0)), pl.BlockSpec((tn, d), lambda j, i: (j, 0)),
                  pl.BlockSpec((tm, tn), lambda j, i: (i, ga_col // tn + j)),
                  pl.BlockSpec((tm, tn), lambda j, i: (i, gb_col // tn + j)), out, out],
        out_specs=[out] * 4, compiler_params=_params("parallel", "parallel"),
    )(dm, w, p, p, pa, pb)


def rel_bias_grad(skew, clip_map, name):
    nh, _, jd = skew.shape
    n_rel = clip_map.shape[1]

    def body(s_ref, c_ref, o_ref):
        sums = jnp.concatenate([_colsum(s_ref[h]) for h in range(nh)], axis=0)
        o_ref[...] = jnp.dot(sums, c_ref[...], preferred_element_type=F32, precision=HI)

    return pl.pallas_call(
        body, name=name, out_shape=_sds((nh, n_rel), F32), in_specs=[VMEM_SPEC, VMEM_SPEC], out_specs=VMEM_SPEC,
        compiler_params=pltpu.CompilerParams(vmem_limit_bytes=VMEM_LIMIT),
    )(skew, clip_map)


MIX_BLOCK = 9 * LANE


def mix_layout(d, a_width, bk, bv):
    main = 3 * a_width + 2 * bk + 2 * bv
    cols = {"qa": 0, "ka": a_width, "va": 2 * a_width, "qb": 3 * a_width, "kb": 3 * a_width + bk,
            "vb": 3 * a_width + 2 * bk, "rb": 3 * a_width + 2 * bk + bv, "ga": main, "gb": main + d,
            "lr": main + 2 * d}
    total = main + 2 * d + LANE
    assert total % MIX_BLOCK == 0
    return cols, main, total


def _mix_pieces(per, main, rank, d):
    out = []
    for lo, hi in ((0, main), (main + rank, main + rank + 2 * d), (main, main + rank)):
        while lo < hi:
            cut = min(hi, (lo // per + 1) * per)
            out.append((lo, cut))
            lo = cut
    return out


def mix_weight_in(g3, main, rank):
    d = g3.shape[2]
    flat = g3.reshape(-1, d)
    return jnp.concatenate([flat[:main], flat[main + rank:], flat[main:main + rank],
                            jnp.zeros((LANE - rank, d), g3.dtype)], axis=0)


def mix_weight_grad_out(gt, main, rank, per):
    d = gt.shape[1]
    blocks = [[] for _ in range(N_DEV)]
    pos = 0
    for lo, hi in _mix_pieces(per, main, rank, d):
        blocks[lo // per].append((lo, gt[pos:pos + hi - lo]))
        pos += hi - lo
    return jnp.stack([jnp.concatenate([x for _, x in sorted(b, key=lambda e: e[0])], axis=0) for b in blocks])


def ffn_forward(h, sh, sc, g, w_in3, w_out_of, ln_g, ln_b, tag):
    a, b, s = ffn_in(h, sh, sc, w_in3, f"{tag}_in")
    w_out = w_out_of(s)
    f, z, hout = out_ln(s, w_out, h, g, ln_g, ln_b, 0.5, f"{tag}_out")
    return hout, (h, a, b, s, f, z), w_out


def ffn_backward_weights(dh, saved, sh, sc, g, w_in3, w_out, ln_g, tag, target=None):
    hin, a, b, s, f, z = saved
    t, d = hin.shape
    nb, _, bw = w_in3.shape
    half = nb // 2
    fdim = w_out.shape[0]
    res = ln_bwd(dh, z, f, ln_g, g, 0.5, f"{tag}_ln_bwd", target=target)
    dz, df, dln_g, dln_b, dg = res[:5]
    dab = ffn_bwd_act(df, w_out, a, b, f"{tag}_act_bwd")
    tk = _row_tile(t, 512, 16)
    dw_out = matmul_tn(f"{tag}_dwout", s, (tk, bw), lambda n, k: (k, n), df, (tk, d), lambda n, k: (k, 0),
                       _sds((fdim, d), BF16), (bw, d), lambda n, k: (n, 0), fdim // bw)
    dw_in = matmul_tn(f"{tag}_dwin", hin, (tk, d), lambda n, k: (k, 0), dab, (None, tk, bw),
                      lambda n, k: (n // half, k, n % half), _sds((nb, d, bw), BF16), (None, d, bw),
                      lambda n, k: (n, 0, 0), nb, mod=(sh, sc))
    grads = dict(w_in=dw_in, w_out=dw_out.reshape(N_DEV, fdim // N_DEV, d), ln_g=dln_g, ln_b=dln_b, g=dg)
    return (dab, dz), grads, (res[5] if target is not None else None)


def ffn_backward_input(carry, saved, sc, w_in3, tag):
    dab, dz = carry
    hin = saved[0]
    t, d = hin.shape
    nb, _, bw = w_in3.shape
    half = nb // 2
    fdim = half * bw
    tm = _row_tile(t, 256, 16)

    def contract(dy_ref, w_ref):
        return sum(_dot_nt(dy_ref[:, j * bw:(j + 1) * bw], w_ref[j]) for j in range(half))

    def first(dy_ref, w_ref, o_ref):
        o_ref[...] = contract(dy_ref, w_ref)

    def second(dy_ref, w_ref, part_ref, dz_ref, hin_ref, sc_ref, o_ref, dsc_ref, dsh_ref):
        @pl.when(pl.program_id(0) == 0)
        def _():
            dsc_ref[...] = jnp.zeros_like(dsc_ref)
            dsh_ref[...] = jnp.zeros_like(dsh_ref)

        du = part_ref[...] + contract(dy_ref, w_ref)
        o_ref[...] = ALPHA * dz_ref[...] + du * (1.0 + sc_ref[...])
        dsc_ref[...] += _colsum(du * hin_ref[...])
        dsh_ref[...] += _colsum(du)

    def specs(which):
        return [pl.BlockSpec((None, tm, fdim), lambda i: (which, i, 0)),
                pl.BlockSpec((half, d, bw), lambda i: (which, 0, 0), pipeline_mode=pl.Buffered(1))]

    row = pl.BlockSpec((tm, d), lambda i: (i, 0))
    vec = pl.BlockSpec((1, d), lambda i: (0, 0))
    part = pl.pallas_call(
        first, name=f"{tag}_du_a", out_shape=_sds((t, d), F32), grid=(t // tm,), in_specs=specs(0),
        out_specs=row, compiler_params=_params("parallel"),
    )(dab, w_in3)
    return pl.pallas_call(
        second, name=f"{tag}_du_b", out_shape=[_sds((t, d), F32), _sds((1, d), F32), _sds((1, d), F32)],
        grid=(t // tm,), in_specs=specs(1) + [row, row, row, vec], out_specs=[row, vec, vec],
        compiler_params=_params("arbitrary"),
    )(dab, w_in3, part, dz, hin, sc)


def _after(v, token):
    return v if token is None else v + token[:1, :1]


def local_step(x, target, mod, weights_of, grads_ready, grads_sent, rel_bias, w_alpha2, b_alpha, gla_norm_g, lns):
    t, d = x.shape
    sh1, sc1, g1, sh2, sc2, g2, sh3, sc3, g3 = [mod[i:i + 1] for i in range(N_MOD)]
    ln1_g, ln1_b, ln2_g, ln2_b, ln3_g, ln3_b = lns
    n_heads_a, n_rel = rel_bias.shape
    rank, bk = w_alpha2.shape
    hv = gla_norm_g.shape[1]

    w1 = weights_of("ffn1_in", x)
    h1, saved1, w1["out"] = ffn_forward(x, sh1, sc1, g1, w1["in"], lambda s: weights_of("ffn1_out", s)["out"],
                                        ln1_g, ln1_b, "ffn1")
    wm = weights_of("mix", h1)
    a_width = wm["proj_a"].shape[1]
    bv = wm["proj_b"].shape[1]
    nh_b = bv // hv
    hk = bk // nh_b
    cols, main, total = mix_layout(d, a_width, bk, bv)
    w_mix = mix_weight_in(wm["in_t"], main, rank)
    p = mod_matmul(h1, sh2, sc2, w_mix, MIX_BLOCK, "mix_in")
    bias = bias_table(rel_bias)
    dh = a_width // n_heads_a
    assert PAIR * dh == LANE
    ya, lse = attn_fwd(p, cols, bias, dh, "attn_fwd")
    wa_pad = jnp.zeros((LANE, bk), BF16).at[:rank].set(w_alpha2.astype(BF16))
    o_b, yb, states = gla_fwd(p, cols, wa_pad, b_alpha, gla_norm_g, nh_b, hk, hv, "gla_fwd")
    pa, pb, merged = proj_merge(ya, yb, wm["proj_a"], wm["proj_b"], p, cols["ga"], cols["gb"], "proj_merge")
    m, z2, h2 = out_ln(merged, wm["out"], h1, g2, ln2_g, ln2_b, 1.0, "mix_out")
    w3 = weights_of("ffn2", h2)
    h3, saved3, _ = ffn_forward(h2, sh3, sc3, g3, w3["in"], lambda s: w3["out"], ln3_g, ln3_b, "ffn2")

    carry3, gr3, loss = ffn_backward_weights(h3, saved3, sh3, sc3, g3, w3["in"], w3["out"], ln3_g, "ffn2",
                                             target=target)
    token = grads_ready("ffn2", dict(ffn2_in=gr3["w_in"], ffn2_out=gr3["w_out"]))
    dh2, dsc3, dsh3 = ffn_backward_input(carry3, saved3, _after(sc3, token), w3["in"], "ffn2")
    token = grads_sent("ffn2", dh2)
    dz2, dm, dln2_g, dln2_b, dg2 = ln_bwd(dh2, z2, m, _after(ln2_g, token), g2, 1.0, "mix_ln_bwd")
    dpa, dpb, dga, dgb = merge_bwd(dm, wm["out"], p, cols["ga"], cols["gb"], pa, pb, "merge_bwd")
    tk = _row_tile(t, 512, 16)
    dw_mix_out = matmul_tn("mix_dwout", merged, (tk, 512), lambda n, k: (k, n), dm, (tk, d), lambda n, k: (k, 0),
                           _sds((d, d), BF16), (512, d), lambda n, k: (n, 0), d // 512)
    tm = _row_tile(t, 512, 16)
    pbw = wm["proj_a"].shape[2]
    dya = matmul_nt_blocks("proj_a_dy", dpa, (tm, pbw), lambda i, k: (i, k), wm["proj_a"], t)
    dyb = matmul_nt_blocks("proj_b_dy", dpb, (tm, pbw), lambda i, k: (i, k), wm["proj_b"], t)
    dw_pa = matmul_tn("proj_a_dw", ya, (tk, a_width), lambda n, k: (k, 0), dpa, (tk, pbw), lambda n, k: (k, n),
                      _sds((N_DEV, a_width, pbw), BF16), (None, a_width, pbw), lambda n, k: (n, 0, 0), N_DEV)
    dw_pb = matmul_tn("proj_b_dw", yb, (tk, bv), lambda n, k: (k, 0), dpb, (tk, pbw), lambda n, k: (k, n),
                      _sds((N_DEV, bv, pbw), BF16), (None, bv, pbw), lambda n, k: (n, 0, 0), N_DEV)
    dqb, dkb, dvb, drb, dgate, dgn = gla_bwd(p, cols, wa_pad, b_alpha, gla_norm_g, o_b, states, dyb,
                                             nh_b, hk, hv, "gla_bwd")
    dlr, dwa_pad, dba = gate_bwd(p, cols["lr"], dgate, wa_pad, "gate_bwd")
    dqa, dka, dva, dbias = attn_bwd(p, cols, bias, lse, dya, dh, "attn_bwd")
    d_rel = rel_bias_grad(bias_grad_skew(dbias), jnp.asarray(bias_clip_map(n_rel)), "rel_bias_grad")
    dp = jnp.concatenate([dqa, dka, dva, dqb, dkb, dvb, drb,
                          dga, dgb, dlr], axis=1)
    dw_mix_t = matmul_tn("mix_dwin", dp, (tk, MIX_BLOCK), lambda n, k: (k, n), h1, (tk, d), lambda n, k: (k, 0),
                         _sds((total, d), BF16), (MIX_BLOCK, d), lambda n, k: (n, 0), total // MIX_BLOCK,
                         mod=(sh2, sc2), mod_b=True)
    dw_mix_in = mix_weight_grad_out(dw_mix_t, main, rank, wm["in_t"].shape[1])
    token = grads_ready("mix", dict(mix_in=dw_mix_in, proj_a=dw_pa, proj_b=dw_pb,
                                    mix_out=dw_mix_out.reshape(N_DEV, d // N_DEV, d)))
    tm = _row_tile(t, 512, 16)
    dh1, dsc2, dsh2 = matmul_nt_blocks("mix_du", dp, (tm, MIX_BLOCK), lambda i, k: (i, k), w_mix, t,
                                       resid=(dz2, h1, _after(sc2, token)))
    token = grads_sent("mix", dh1)
    carry1, gr1, _ = ffn_backward_weights(dh1, saved1, sh1, sc1, g1, w1["in"], w1["out"], _after(ln1_g, token),
                                          "ffn1")
    token = grads_ready("ffn1", dict(ffn1_in=gr1["w_in"], ffn1_out=gr1["w_out"]))
    dx, dsc1, dsh1 = ffn_backward_input(carry1, saved1, _after(sc1, token), w1["in"], "ffn1")

    dmod = [dsh1, dsc1, gr1["g"], dsh2, dsc2, dg2, dsh3, dsc3, gr3["g"]]
    small = dict(ln1_g=gr1["ln_g"], ln1_b=gr1["ln_b"], ln2_g=dln2_g, ln2_b=dln2_b, ln3_g=gr3["ln_g"],
                 ln3_b=gr3["ln_b"], b_alpha=dba, gla_norm_g=dgn, w_alpha2=dwa_pad[:rank], rel_bias=d_rel)
    return loss, dx, dmod, small


GROUPS = dict(ffn1=("ffn1_in", "ffn1_out"), mix=("mix_in", "proj_a", "proj_b", "mix_out"),
              ffn2=("ffn2_in", "ffn2_out"))
GATHERS = dict(ffn1_in=("ffn1_in",), ffn1_out=("ffn1_out",), mix=GROUPS["mix"], ffn2=GROUPS["ffn2"])
SMALL_REPLICATED = ("b_ada", "ln1_g", "ln1_b", "ln2_g", "ln2_b", "ln3_g", "ln3_b", "b_alpha", "gla_norm_g")
SMALL_SHARDED = ("rel_bias", "w_alpha2")
WEIGHT_ORDER = ("w_ada", "b_ada", "ffn1_w_in", "ffn1_w_out", "ln1_g", "ln1_b", "w_mix_in", "rel_bias", "w_alpha2",
                "b_alpha", "gla_norm_g", "w_proj_a", "w_proj_b", "w_mix_out", "ln2_g", "ln2_b", "ffn2_w_in",
                "ffn2_w_out", "ln3_g", "ln3_b")
BIG_NAME = dict(ffn1_in="ffn1_w_in", ffn1_out="ffn1_w_out", mix_in="w_mix_in", proj_a="w_proj_a",
                proj_b="w_proj_b", mix_out="w_mix_out", ffn2_in="ffn2_w_in", ffn2_out="ffn2_w_out")


def kernel(x, c, w_ada, b_ada, ffn1_w_in, ffn1_w_out, ln1_g, ln1_b, w_mix_in, rel_bias, w_alpha2, b_alpha, gla_norm_g, w_proj_a, w_proj_b, w_mix_out, ln2_g, ln2_b, ffn2_w_in, ffn2_w_out, ln3_g, ln3_b, loss_target, m_w_ada, m_b_ada, m_ffn1_w_in, m_ffn1_w_out, m_ln1_g, m_ln1_b, m_w_mix_in, m_rel_bias, m_w_alpha2, m_b_alpha, m_gla_norm_g, m_w_proj_a, m_w_proj_b, m_w_mix_out, m_ln2_g, m_ln2_b, m_ffn2_w_in, m_ffn2_w_out, m_ln3_g, m_ln3_b, v_w_ada, v_b_ada, v_ffn1_w_in, v_ffn1_w_out, v_ln1_g, v_ln1_b, v_w_mix_in, v_rel_bias, v_w_alpha2, v_b_alpha, v_gla_norm_g, v_w_proj_a, v_w_proj_b, v_w_mix_out, v_ln2_g, v_ln2_b, v_ffn2_w_in, v_ffn2_w_out, v_ln3_g, v_ln3_b):
    env = dict(locals())
    w = {n: env[n] for n in WEIGHT_ORDER}
    mom = {n: env["m_" + n] for n in WEIGHT_ORDER}
    var = {n: env["v_" + n] for n in WEIGHT_ORDER}
    me = _me()
    dev = _lin(me)
    core = jnp.reshape(me[2], (1,)).astype(jnp.int32)
    chip = jnp.reshape(2 * me[0] + me[1], (1,)).astype(jnp.int32)
    d = x.shape[-1]

    def shard(n):
        s = w[BIG_NAME[n]][0].astype(BF16)
        return s.T if n == "mix_in" else s

    dev_idx = jnp.reshape(dev, (1,)).astype(jnp.int32)
    started = {}

    def start(grp, after):
        shards = [shard(n) for n in GATHERS[grp]]
        lands = [place_own(dev_idx, s, f"place_own_{n}") for n, s in zip(GATHERS[grp], shards)]
        started[grp] = gather_start(shards, lands, after, f"gather_start_{grp}")
        return started[grp][-1]

    first, *rest = list(GATHERS)
    order = start(first, x[0, :1, :1])

    ada_cols = w_ada.shape[-1]
    c_all = all_gather_small(_after(c, order), "gather_c")[:, 0, :]
    b_cols = lax.dynamic_slice_in_dim(b_ada, dev * ada_cols, ada_cols, axis=1)
    mod_cols = adaln_cols(c_all, w_ada[0], b_cols, "adaln_cols")
    mod_all = all_gather_small(mod_cols, "gather_mod")
    mod = lax.dynamic_index_in_dim(mod_all, dev, axis=1, keepdims=False).reshape(N_MOD, d)

    small_w = all_gather_small(_after(jnp.concatenate([rel_bias[0], w_alpha2[0]], axis=1), order), "gather_small_w")
    n_rel_cols = rel_bias.shape[-1]
    rel_full = small_w[:, :, :n_rel_cols].transpose(1, 0, 2).reshape(small_w.shape[1], -1)
    wa2_full = small_w[:, :, n_rel_cols:].transpose(1, 0, 2).reshape(small_w.shape[1], -1)

    order = small_w[0, :1, :1] + mod_all[0, :1, :1]
    for grp in rest:
        order = start(grp, order)
    mod = _after(mod, order)

    def weights_of(grp, after):
        _, zones = gather_wait(started[grp], after, f"gather_wait_{grp}")
        full = dict(zip(GATHERS[grp], gather_forward(zones, f"gather_forward_{grp}")))
        if grp == "mix":
            return dict(in_t=full["mix_in"], proj_a=full["proj_a"], proj_b=full["proj_b"],
                        out=full["mix_out"].reshape(-1, d))
        return {"in" if n.endswith("_in") else "out": v if n.endswith("_in") else v.reshape(-1, d)
                for n, v in full.items()}

    pairs, exchanges = {}, {}
    last = list(GROUPS)[0]

    def to_chips(grp, grads, got):
        sums = [pair_add(core, grads[n], g, f"grad_pair_add_{n}") for n, g in zip(GROUPS[grp], got)]
        exchanges[grp] = exchange_start(sums, chip_routes, f"grad_chip_start_{grp}")
        return exchanges[grp][-1]

    def grads_ready(grp, grads):
        if grp == last:
            return to_chips(grp, grads, pair_exchange([grads[n] for n in GROUPS[grp]], f"grad_pair_exchange_{grp}"))
        pairs[grp] = exchange_start([grads[n] for n in GROUPS[grp]], pair_routes, f"grad_pair_start_{grp}")
        return pairs[grp][-1]

    def grads_sent(grp, after):
        if grp == last:
            return None
        sent, got = exchange_wait(pairs[grp], pair_routes, after, f"grad_pair_wait_{grp}")
        return to_chips(grp, dict(zip(GROUPS[grp], sent)), got)

    lns = [ln1_g, ln1_b, ln2_g, ln2_b, ln3_g, ln3_b]
    loss, dx, dmod, small = local_step(x[0], loss_target[0], mod, weights_of, grads_ready, grads_sent, rel_full,
                                       wa2_full, b_alpha, gla_norm_g, lns)

    pieces = (list(dmod) + [small[n].reshape(1, -1) for n in SMALL_REPLICATED[1:] + SMALL_SHARDED] + [loss])
    parts = all_gather_rows(pieces, "gather_small_grads")
    loss = jnp.sum(parts[:, 0, parts.shape[2] - loss.shape[1]])
    n_mod = N_MOD * d
    dmod_all = parts[:, 0, :n_mod]
    g_w_ada = adaln_wgrad(c_all, lax.dynamic_slice_in_dim(dmod_all, dev * ada_cols, ada_cols, axis=1), "adaln_wgrad")
    out = {}
    out["w_ada"] =[o[None] for o in adamw_sum(g_w_ada[None], w_ada[0], m_w_ada[0], v_w_ada[0], "adamw_w_ada")]

    sources, where, off = [parts], [], 0
    for n in SMALL_REPLICATED:
        where.append((0, off))
        off += w[n].size
    for n in SMALL_SHARDED:
        rows, cols_local = w[n].shape[1], w[n].shape[2]
        full_part = parts[:, 0, off:off + rows * cols_local * N_DEV].reshape(N_DEV, rows, cols_local * N_DEV)
        mine = lax.dynamic_slice_in_dim(full_part, dev * cols_local, cols_local, axis=2)
        where.append((len(sources), 0))
        sources.append(mine.reshape(N_DEV, 1, rows * cols_local))
        off += rows * cols_local * N_DEV
    names = SMALL_REPLICATED + SMALL_SHARDED
    res = adamw_rows(sources, where, *[[src[n].reshape(1, -1) for n in names] for src in (w, mom, var)],
                     "adamw_small")
    for n, res_n in zip(names, res):
        out[n] = [r.reshape(w[n].shape) for r in res_n]

    order = res[0][0]
    for grp in reversed(list(GROUPS)):
        sums, recv = exchange_wait(exchanges[grp], chip_routes, order, f"grad_chip_wait_{grp}")
        for n, hsum, r in zip(GROUPS[grp], sums, recv):
            full = BIG_NAME[n]
            if n == "mix_in":
                g = owned_sum(chip, hsum, r, f"owned_sum_{n}").T
                res_n = adamw_sum(g[None], w[full][0], mom[full][0], var[full][0], f"adamw_{n}")
            else:
                res_n = adamw_owned(chip, hsum, r, w[full][0], mom[full][0], var[full][0], f"adamw_{n}")
            out[full] = [o[None] for o in res_n]
            order = res_n[0]

    flat = [loss, dx[None]]
    for k in range(4):
        flat += [out[n][k] for n in WEIGHT_ORDER]
    return tuple(flat)
```

```python
import functools

import numpy as np
import jax
import jax.numpy as jnp
from jax import lax
from jax.experimental import pallas as pl
from jax.experimental.pallas import tpu as pltpu

F32 = jnp.float32
BF16 = jnp.bfloat16
MESH = pl.DeviceIdType.MESH
N_DEV = 8
N_CHIP = 4

CHUNK = 64
A_PAST_CHUNKS = 8
REL_CLIP = 256
GATE_TAU = 16.0
N_MOD = 9
DEPTH = 1
ALPHA = (2.0 * DEPTH) ** 0.25
LN_EPS = 1e-5
RMS_EPS = 1e-6
ADAM_LR = 0.001
ADAM_B1 = 0.9
ADAM_B2 = 0.999
ADAM_EPS = 1e-08
ADAM_WD = 0.01
ADAM_STEP = 10

LANE = 128
VMEM_LIMIT = 56 * 2 ** 20
QB = 4 * CHUNK
KW = 3 * QB
GB = 8 * CHUNK
NEG = -1e30
HI = lax.Precision.HIGHEST

ANY = pl.BlockSpec(memory_space=pl.ANY)
VMEM_SPEC = pl.BlockSpec(memory_space=pltpu.VMEM)


def _params(*sem):
    return pltpu.CompilerParams(dimension_semantics=sem, vmem_limit_bytes=VMEM_LIMIT)


def _sds(shape, dtype):
    return jax.ShapeDtypeStruct(shape, dtype)


def _dot(a, b):
    return jnp.dot(a, b, preferred_element_type=F32)


def _dot_nt(a, b):
    return lax.dot_general(a, b, (((1,), (1,)), ((), ())), preferred_element_type=F32)


def _dot_tn(a, b):
    return lax.dot_general(a, b, (((0,), (0,)), ((), ())), preferred_element_type=F32)


def _sigmoid(x):
    return 0.5 * jnp.tanh(0.5 * x) + 0.5


def _colsum(x):
    return jnp.sum(x, axis=0, keepdims=True)


def _row_tile(rows, cap, mult):
    for t in range(min(rows, cap), 0, -1):
        if rows % t == 0 and t % mult == 0:
            return t
    return rows


def _me():
    return lax.axis_index("x"), lax.axis_index("y"), lax.axis_index("c")


def _flip(me, k):
    return tuple((1 - p) if (k >> s) & 1 else p for p, s in zip(me, (2, 1, 0)))


def _lin(p):
    return 4 * p[0] + 2 * p[1] + p[2]


def _gather_direct(x_ref, out_ref, send_sems, recv_sems, local_sem):
    me = _me()
    mine = pltpu.make_async_copy(x_ref, out_ref.at[_lin(me)], local_sem)
    mine.start()
    sends = []
    for k in range(1, N_DEV):
        cp = pltpu.make_async_remote_copy(
            src_ref=x_ref, dst_ref=out_ref.at[_lin(me)], send_sem=send_sems.at[k - 1],
            recv_sem=recv_sems.at[k - 1], device_id=_flip(me, k), device_id_type=MESH)
        cp.start()
        sends.append(cp)
    for k in range(1, N_DEV):
        peer = _flip(me, k)
        pltpu.make_async_remote_copy(
            src_ref=x_ref, dst_ref=out_ref.at[_lin(peer)], send_sem=send_sems.at[k - 1],
            recv_sem=recv_sems.at[k - 1], device_id=peer, device_id_type=MESH).wait_recv()
    for cp in sends:
        cp.wait_send()
    mine.wait()


GATHER_SEMS = [pltpu.SemaphoreType.DMA((N_DEV - 1,)), pltpu.SemaphoreType.DMA((N_DEV - 1,)), pltpu.SemaphoreType.DMA]


def all_gather_small(x, name, after=()):
    r, n = x.shape

    def body(x_ref, *refs):
        _gather_direct(x_ref, *refs[len(after):])

    return pl.pallas_call(
        body, name=name, out_shape=_sds((N_DEV, r, n), x.dtype),
        in_specs=[VMEM_SPEC] + [ANY] * len(after), out_specs=VMEM_SPEC, scratch_shapes=GATHER_SEMS,
    )(x, *after)


def all_gather_rows(pieces, after, name):
    sizes = [x.shape[1] for x in pieces]
    total = sum(sizes)
    assert all(n % LANE == 0 for n in sizes)

    def body(*refs):
        ins = refs[:len(pieces)]
        out_ref, row, send_sems, recv_sems, local_sem = refs[len(pieces) + 1:]
        off = 0
        for x_ref, n in zip(ins, sizes):
            row[:, off:off + n] = x_ref[...]
            off += n
        _gather_direct(row, out_ref, send_sems, recv_sems, local_sem)

    return pl.pallas_call(
        body, name=name, out_shape=_sds((N_DEV, 1, total), F32),
        in_specs=[VMEM_SPEC] * len(pieces) + [ANY], out_specs=VMEM_SPEC,
        scratch_shapes=[pltpu.VMEM((1, total), F32)] + GATHER_SEMS,
    )(*pieces, after)


HBM_SPEC = pl.BlockSpec(memory_space=pltpu.HBM)
SEM_SPEC = pl.BlockSpec(memory_space=pltpu.SEMAPHORE)
EFFECT = pltpu.SideEffectType.DATAFLOW_SIDE_EFFECTING
FIRST = N_CHIP


def _hbm(v):
    return pltpu.with_memory_space_constraint(v, pltpu.HBM)


def _other_chips(mx, my):
    return [(1 - mx, my), (mx, 1 - my), (1 - mx, 1 - my)]


def place_own(dev, shard, name):
    rows, cols = shard.shape
    tr, tc = _tile2(rows, cols, 16)

    def body(dev_ref, s_ref, land_ref, o_ref):
        o_ref[...] = s_ref[...]

    land = lax.empty((N_DEV, rows, cols), shard.dtype)
    return pl.pallas_call(
        body, name=name, out_shape=_sds(land.shape, land.dtype),
        grid_spec=pltpu.PrefetchScalarGridSpec(
            num_scalar_prefetch=1, grid=(rows // tr, cols // tc),
            in_specs=[pl.BlockSpec((tr, tc), lambda i, j, d: (i, j)), ANY],
            out_specs=pl.BlockSpec((None, tr, tc), lambda i, j, d: (d[0], i, j))),
        input_output_aliases={2: 0}, compiler_params=_params("parallel", "parallel"),
    )(dev, shard, land)


def gather_start(shards, lands, after, name):
    n = len(shards)

    def body(*refs):
        ins, zones = refs[:n], refs[n:2 * n]
        send_sems, recv_sems = refs[2 * n + 1], refs[2 * n + 2]
        token = refs[-1]
        me = _me()
        mx, my, mc = me
        for a in range(n):
            dst = zones[a].at[_lin(me)]
            targets = [(mx, my, 1 - mc)] + [(*chip, mc) for chip in _other_chips(mx, my)]
            for k, to in enumerate(targets):
                pltpu.make_async_remote_copy(
                    src_ref=ins[a], dst_ref=dst, send_sem=send_sems.at[a * FIRST + k],
                    recv_sem=recv_sems.at[a * FIRST + k], device_id=to, device_id_type=MESH).start()
        token[...] = jnp.zeros_like(token)

    sems = pltpu.SemaphoreType.DMA((n * FIRST,))
    out = pl.pallas_call(
        body, name=name,
        out_shape=(sems, sems, *[pltpu.HBM(s.shape, s.dtype) for s in shards],
                   *[pltpu.HBM(z.shape, z.dtype) for z in lands], _sds((8, LANE), F32)),
        in_specs=[HBM_SPEC] * (2 * n) + [ANY],
        out_specs=(SEM_SPEC, SEM_SPEC, *[HBM_SPEC] * (2 * n), VMEM_SPEC),
        input_output_aliases={a: 2 + a for a in range(2 * n)},
        compiler_params=pltpu.CompilerParams(has_side_effects=EFFECT),
    )(*[_hbm(s) for s in shards], *[_hbm(z) for z in lands], after)
    return out[0], out[1], out[2:2 + n], out[2 + n:2 + 2 * n], out[-1]


def gather_wait(started, after, name):
    send_sems, recv_sems, shards, lands, _ = started
    n = len(shards)

    def body(*refs):
        ins, zones = refs[:n], refs[n:2 * n]
        send_ref, recv_ref = refs[2 * n], refs[2 * n + 1]
        mx, my, mc = _me()
        for a in range(n):
            for k in range(FIRST):
                cp = pltpu.make_async_remote_copy(
                    src_ref=ins[a], dst_ref=zones[a].at[0], send_sem=send_ref.at[a * FIRST + k],
                    recv_sem=recv_ref.at[a * FIRST + k], device_id=(mx, my, 1 - mc), device_id_type=MESH)
                cp.wait_send()
                cp.wait_recv()

    out = pl.pallas_call(
        body, name=name,
        out_shape=(*[pltpu.HBM(s.shape, s.dtype) for s in shards], *[pltpu.HBM(z.shape, z.dtype) for z in lands]),
        in_specs=[HBM_SPEC] * (2 * n) + [SEM_SPEC, SEM_SPEC, ANY], out_specs=tuple([HBM_SPEC] * (2 * n)),
        input_output_aliases={a: a for a in range(2 * n)},
        compiler_params=pltpu.CompilerParams(has_side_effects=EFFECT),
    )(*shards, *lands, send_sems, recv_sems, after)
    return out[:n], out[n:]


def gather_forward(lands, name):
    n = len(lands)
    rel = N_CHIP - 1

    def body(*refs):
        zones, outs = refs[:n], refs[n:2 * n]
        send_sems, recv_sems = refs[2 * n:]
        mx, my, mc = _me()
        chips = _other_chips(mx, my)

        def copy(a, j, core):
            blk = _lin((*chips[j], core))
            return pltpu.make_async_remote_copy(
                src_ref=zones[a].at[blk], dst_ref=outs[a].at[blk], send_sem=send_sems.at[a * rel + j],
                recv_sem=recv_sems.at[a * rel + j], device_id=(mx, my, 1 - mc), device_id_type=MESH)

        sends = [copy(a, j, mc) for a in range(n) for j in range(rel)]
        for cp in sends:
            cp.start()
        for a in range(n):
            for j in range(rel):
                copy(a, j, 1 - mc).wait_recv()
        for cp in sends:
            cp.wait_send()

    return pl.pallas_call(
        body, name=name, out_shape=[_sds(z.shape, z.dtype) for z in lands],
        in_specs=[ANY] * n, out_specs=[ANY] * n, input_output_aliases={a: a for a in range(n)},
        scratch_shapes=[pltpu.SemaphoreType.DMA((n * rel,)), pltpu.SemaphoreType.DMA((n * rel,))],
    )(*lands)


def pair_exchange(gs, name):
    n = len(gs)

    def body(*refs):
        ins, outs = refs[:n], refs[n:2 * n]
        send_sems, recv_sems = refs[2 * n:]
        mx, my, mc = _me()
        cps = []
        for a in range(n):
            for q in range(N_CHIP):
                cp = pltpu.make_async_remote_copy(
                    src_ref=ins[a].at[2 * q + (1 - mc)], dst_ref=outs[a].at[q],
                    send_sem=send_sems.at[a * N_CHIP + q], recv_sem=recv_sems.at[a * N_CHIP + q],
                    device_id=(mx, my, 1 - mc), device_id_type=MESH)
                cp.start()
                cps.append(cp)
        for cp in cps:
            cp.wait()

    return pl.pallas_call(
        body, name=name, out_shape=[_sds((N_CHIP,) + g.shape[1:], g.dtype) for g in gs],
        in_specs=[ANY] * n, out_specs=[ANY] * n,
        scratch_shapes=[pltpu.SemaphoreType.DMA((n * N_CHIP,)), pltpu.SemaphoreType.DMA((n * N_CHIP,))],
    )(*gs)


def _tile2(rows, cols, row_mult):
    tr = _row_tile(rows, 512, row_mult)
    if tr < rows or rows * cols <= 2 ** 20:
        return tr, cols
    return rows, _row_tile(cols, 512, LANE)


def pair_add(core, g, got, name):
    _, rows, cols = g.shape
    tr, tc = _tile2(rows, cols, 16)

    def body(core_ref, g_ref, got_ref, h_ref):
        h_ref[...] = (g_ref[...].astype(F32) + got_ref[...].astype(F32)).astype(h_ref.dtype)

    blk = pl.BlockSpec((None, tr, tc), lambda q, i, j, c: (q, i, j))
    return pl.pallas_call(
        body, name=name, out_shape=_sds((N_CHIP, rows, cols), g.dtype),
        grid_spec=pltpu.PrefetchScalarGridSpec(
            num_scalar_prefetch=1, grid=(N_CHIP, rows // tr, cols // tc),
            in_specs=[pl.BlockSpec((None, tr, tc), lambda q, i, j, c: (2 * q + c[0], i, j)), blk],
            out_specs=blk),
        compiler_params=_params("parallel", "parallel", "parallel"),
    )(core, g, got)


def chip_routes(mx, my, mc):
    return [(2 * px + py, k, (px, py, mc)) for k, (px, py) in enumerate(_other_chips(mx, my))]


def pair_routes(mx, my, mc):
    return [(2 * q + (1 - mc), q, (mx, my, 1 - mc)) for q in range(N_CHIP)]


def exchange_start(hs, routes, name):
    n = len(hs)
    rel = len(routes(0, 0, 0))
    lands = [lax.empty((rel,) + h.shape[1:], h.dtype) for h in hs]

    def body(*refs):
        ins, zones = refs[:n], refs[n:2 * n]
        send_sems, recv_sems = refs[2 * n], refs[2 * n + 1]
        token = refs[-1]
        for a in range(n):
            for k, (src, slot, to) in enumerate(routes(*_me())):
                pltpu.make_async_remote_copy(
                    src_ref=ins[a].at[src], dst_ref=zones[a].at[slot], send_sem=send_sems.at[a * rel + k],
                    recv_sem=recv_sems.at[a * rel + k], device_id=to, device_id_type=MESH).start()
        token[...] = jnp.zeros_like(token)

    sems = pltpu.SemaphoreType.DMA((n * rel,))
    out = pl.pallas_call(
        body, name=name,
        out_shape=(sems, sems, *[pltpu.HBM(h.shape, h.dtype) for h in hs],
                   *[pltpu.HBM(z.shape, z.dtype) for z in lands], _sds((8, LANE), F32)),
        in_specs=[HBM_SPEC] * (2 * n), out_specs=(SEM_SPEC, SEM_SPEC, *[HBM_SPEC] * (2 * n), VMEM_SPEC),
        input_output_aliases={a: 2 + a for a in range(2 * n)},
        compiler_params=pltpu.CompilerParams(has_side_effects=EFFECT),
    )(*[_hbm(h) for h in hs], *[_hbm(z) for z in lands])
    return out[0], out[1], out[2:2 + n], out[2 + n:2 + 2 * n], out[-1]


def exchange_wait(started, routes, after, name):
    send_sems, recv_sems, hs, lands, _ = started
    n = len(hs)
    rel = len(routes(0, 0, 0))

    def body(*refs):
        ins, zones = refs[:n], refs[n:2 * n]
        send_ref, recv_ref = refs[2 * n], refs[2 * n + 1]
        for a in range(n):
            for k, (src, slot, to) in enumerate(routes(*_me())):
                cp = pltpu.make_async_remote_copy(
                    src_ref=ins[a].at[src], dst_ref=zones[a].at[slot], send_sem=send_ref.at[a * rel + k],
                    recv_sem=recv_ref.at[a * rel + k], device_id=to, device_id_type=MESH)
                cp.wait_send()
                cp.wait_recv()

    out = pl.pallas_call(
        body, name=name,
        out_shape=(*[pltpu.HBM(h.shape, h.dtype) for h in hs], *[pltpu.HBM(z.shape, z.dtype) for z in lands]),
        in_specs=[HBM_SPEC] * (2 * n) + [SEM_SPEC, SEM_SPEC, ANY], out_specs=tuple([HBM_SPEC] * (2 * n)),
        input_output_aliases={a: a for a in range(2 * n)},
        compiler_params=pltpu.CompilerParams(has_side_effects=EFFECT),
    )(*hs, *lands, send_sems, recv_sems, after)
    return out[:n], out[n:]


def _adam(w, g, m, v):
    m = ADAM_B1 * m + (1.0 - ADAM_B1) * g
    v = ADAM_B2 * v + (1.0 - ADAM_B2) * (g * g)
    m_hat = m / (1.0 - ADAM_B1 ** ADAM_STEP)
    v_hat = v / (1.0 - ADAM_B2 ** ADAM_STEP)
    delta = -ADAM_LR * (m_hat / (jnp.sqrt(v_hat) + ADAM_EPS) + ADAM_WD * w)
    return delta, m, v


def adamw_owned(chip, h, got, w, m, v, name):
    rows, cols = w.shape
    tr = _row_tile(rows, 256, 16)

    def body(chip_ref, h_ref, got_ref, w_ref, m_ref, v_ref, g_out, d_out, m_out, v_out):
        g = h_ref[...].astype(F32)
        for k in range(N_CHIP - 1):
            g = g + got_ref[k].astype(F32)
        d, mn, vn = _adam(w_ref[...], g, m_ref[...], v_ref[...])
        g_out[...] = g
        d_out[...] = d
        m_out[...] = mn
        v_out[...] = vn

    blk = pl.BlockSpec((tr, cols), lambda i, c: (i, 0))
    return pl.pallas_call(
        body, name=name, out_shape=[_sds((rows, cols), F32)] * 4,
        grid_spec=pltpu.PrefetchScalarGridSpec(
            num_scalar_prefetch=1, grid=(rows // tr,),
            in_specs=[pl.BlockSpec((None, tr, cols), lambda i, c: (c[0], i, 0)),
                      pl.BlockSpec((N_CHIP - 1, tr, cols), lambda i, c: (0, i, 0)), blk, blk, blk],
            out_specs=[blk] * 4),
        compiler_params=_params("parallel"),
    )(chip, h, got, w, m, v)


def owned_sum(chip, h, got, name):
    _, rows, cols = h.shape
    tr, tc = _tile2(rows, cols, 16)

    def body(chip_ref, h_ref, got_ref, g_out):
        g = h_ref[...].astype(F32)
        for k in range(N_CHIP - 1):
            g = g + got_ref[k].astype(F32)
        g_out[...] = g

    return pl.pallas_call(
        body, name=name, out_shape=_sds((rows, cols), F32),
        grid_spec=pltpu.PrefetchScalarGridSpec(
            num_scalar_prefetch=1, grid=(rows // tr, cols // tc),
            in_specs=[pl.BlockSpec((None, tr, tc), lambda i, j, c: (c[0], i, j)),
                      pl.BlockSpec((N_CHIP - 1, tr, tc), lambda i, j, c: (0, i, j))],
            out_specs=pl.BlockSpec((tr, tc), lambda i, j, c: (i, j))),
        compiler_params=_params("parallel", "parallel"),
    )(chip, h, got)


def adamw_rows(sources, where, ws, ms, vs, name):
    n_src, n_par = len(sources), len(ws)

    def body(*refs):
        srcs = refs[:n_src]
        w_refs, m_refs, v_refs = (refs[n_src + j * n_par:n_src + (j + 1) * n_par] for j in range(3))
        outs = refs[n_src + 3 * n_par:]
        for k in range(n_par):
            src, off = srcs[where[k][0]], where[k][1]
            n = w_refs[k].shape[1]
            g = src[0, :, off:off + n]
            for dev in range(1, N_DEV):
                g = g + src[dev, :, off:off + n]
            d, mn, vn = _adam(w_refs[k][...], g, m_refs[k][...], v_refs[k][...])
            for o_ref, val in zip(outs[4 * k:4 * k + 4], (g, d, mn, vn)):
                o_ref[...] = val

    flat = pl.pallas_call(
        body, name=name, out_shape=[_sds(x.shape, F32) for x in ws for _ in range(4)],
        in_specs=[VMEM_SPEC] * (n_src + 3 * n_par), out_specs=[VMEM_SPEC] * (4 * n_par),
        compiler_params=pltpu.CompilerParams(vmem_limit_bytes=VMEM_LIMIT),
    )(*sources, *ws, *ms, *vs)
    return [flat[4 * k:4 * k + 4] for k in range(n_par)]


def adamw_sum(parts, w, m, v, name):
    n_parts, rows, cols = parts.shape
    tr = _row_tile(rows, 256, 8)

    def body(p_ref, w_ref, m_ref, v_ref, g_out, d_out, m_out, v_out):
        g = p_ref[0]
        for k in range(1, n_parts):
            g = g + p_ref[k]
        d, mn, vn = _adam(w_ref[...], g, m_ref[...], v_ref[...])
        g_out[...] = g
        d_out[...] = d
        m_out[...] = mn
        v_out[...] = vn

    blk = pl.BlockSpec((tr, cols), lambda i: (i, 0))
    return pl.pallas_call(
        body, name=name, out_shape=[_sds((rows, cols), F32)] * 4, grid=(rows // tr,),
        in_specs=[pl.BlockSpec((n_parts, tr, cols), lambda i: (0, i, 0)), blk, blk, blk],
        out_specs=[blk] * 4, compiler_params=_params("parallel"),
    )(parts, w, m, v)


def adaln_cols(c_all, w, b, name):
    d, n = w.shape
    tn = _row_tile(n, 768, LANE)

    def body(c_ref, w_ref, b_ref, o_ref):
        c = c_ref[...]
        o_ref[...] = jnp.dot(c * _sigmoid(c), w_ref[...], preferred_element_type=F32, precision=HI) + b_ref[...]

    return pl.pallas_call(
        body, name=name, out_shape=_sds((N_DEV, n), F32), grid=(n // tn,),
        in_specs=[pl.BlockSpec((N_DEV, d), lambda j: (0, 0)), pl.BlockSpec((d, tn), lambda j: (0, j)),
                  pl.BlockSpec((1, tn), lambda j: (0, j))],
        out_specs=pl.BlockSpec((N_DEV, tn), lambda j: (0, j)), compiler_params=_params("parallel"),
    )(c_all, w, b)


def adaln_wgrad(c_all, dmod_cols, name):
    d = c_all.shape[1]
    n = dmod_cols.shape[1]
    tn = _row_tile(n, 768, LANE)

    def body(c_ref, g_ref, o_ref):
        c = c_ref[...]
        o_ref[...] = lax.dot_general(c * _sigmoid(c), g_ref[...], (((0,), (0,)), ((), ())),
                                     preferred_element_type=F32, precision=HI)

    return pl.pallas_call(
        body, name=name, out_shape=_sds((d, n), F32), grid=(n // tn,),
        in_specs=[pl.BlockSpec((N_DEV, d), lambda j: (0, 0)), pl.BlockSpec((N_DEV, tn), lambda j: (0, j))],
        out_specs=pl.BlockSpec((d, tn), lambda j: (0, j)), compiler_params=_params("parallel"),
    )(c_all, dmod_cols)


def _modulate(h, sh, sc):
    return (h * (1.0 + sc) + sh).astype(BF16)


def ffn_in(h, sh, sc, w3, name):
    t, d = h.shape
    nb, _, bw = w3.shape
    half = nb // 2
    tm = _row_tile(t, 512, 16)

    def body(h_ref, sh_ref, sc_ref, wa_ref, wb_ref, a_ref, b_ref, s_ref):
        u = _modulate(h_ref[...], sh_ref[...], sc_ref[...])
        a = _dot(u, wa_ref[...])
        b = _dot(u, wb_ref[...])
        a_ref[...] = a.astype(BF16)
        b_ref[...] = b.astype(BF16)
        s_ref[...] = (a * _sigmoid(a) * b).astype(BF16)

    vec = pl.BlockSpec((1, d), lambda j, i: (0, 0))
    out = pl.BlockSpec((tm, bw), lambda j, i: (i, j))
    return pl.pallas_call(
        body, name=name, out_shape=[_sds((t, half * bw), BF16)] * 3, grid=(half, t // tm),
        in_specs=[pl.BlockSpec((tm, d), lambda j, i: (i, 0)), vec, vec,
                  pl.BlockSpec((None, d, bw), lambda j, i: (j, 0, 0), pipeline_mode=pl.Buffered(1)),
                  pl.BlockSpec((None, d, bw), lambda j, i: (j + half, 0, 0), pipeline_mode=pl.Buffered(1))],
        out_specs=[out] * 3, compiler_params=_params("parallel", "parallel"),
    )(h, sh, sc, w3, w3)


def mod_matmul(h, sh, sc, wt, bw, name):
    t, d = h.shape
    n = wt.shape[0]
    tm = _row_tile(t, 512, 16)

    def body(h_ref, sh_ref, sc_ref, w_ref, o_ref):
        o_ref[...] = _dot_nt(_modulate(h_ref[...], sh_ref[...], sc_ref[...]), w_ref[...]).astype(BF16)

    vec = pl.BlockSpec((1, d), lambda j, i: (0, 0))
    return pl.pallas_call(
        body, name=name, out_shape=_sds((t, n), BF16), grid=(n // bw, t // tm),
        in_specs=[pl.BlockSpec((tm, d), lambda j, i: (i, 0)), vec, vec, pl.BlockSpec((bw, d), lambda j, i: (j, 0))],
        out_specs=pl.BlockSpec((tm, bw), lambda j, i: (i, j)), compiler_params=_params("parallel", "parallel"),
    )(h, sh, sc, wt)


def out_ln(s, w, hin, gmod, ln_g, ln_b, coef, name):
    t, kdim = s.shape
    d = w.shape[1]
    tm = _row_tile(t, 256, 16)

    def body(s_ref, w_ref, hin_ref, gm_ref, g_ref, b_ref, f_ref, z_ref, h_ref):
        f = _dot(s_ref[...], w_ref[...])
        z = ALPHA * hin_ref[...] + (coef * gm_ref[...]) * f
        mu = jnp.mean(z, axis=-1, keepdims=True)
        zc = z - mu
        var = jnp.mean(zc * zc, axis=-1, keepdims=True)
        f_ref[...] = f.astype(BF16)
        z_ref[...] = z
        h_ref[...] = zc * lax.rsqrt(var + LN_EPS) * g_ref[...] + b_ref[...]

    vec = pl.BlockSpec((1, d), lambda i: (0, 0))
    row = pl.BlockSpec((tm, d), lambda i: (i, 0))
    return pl.pallas_call(
        body, name=name, out_shape=[_sds((t, d), BF16), _sds((t, d), F32), _sds((t, d), F32)],
        grid=(t // tm,),
        in_specs=[pl.BlockSpec((tm, kdim), lambda i: (i, 0)),
                  pl.BlockSpec((kdim, d), lambda i: (0, 0), pipeline_mode=pl.Buffered(1)), row, vec, vec, vec],
        out_specs=[row, row, row], compiler_params=_params("parallel"),
    )(s, w, hin, gmod, ln_g, ln_b)


def ln_bwd(dh, z, f, ln_g, gmod, coef, name, target=None):
    t, d = z.shape
    tm = _row_tile(t, 256, 16)
    head = target is not None

    def body(*refs):
        if head:
            dh_ref, tg_ref, z_ref, f_ref, g_ref, gm_ref, dz_ref, df_ref, dg_ref, db_ref, dgm_ref, loss_ref = refs
        else:
            dh_ref, z_ref, f_ref, g_ref, gm_ref, dz_ref, df_ref, dg_ref, db_ref, dgm_ref = refs
        i = pl.program_id(0)

        @pl.when(i == 0)
        def _():
            dg_ref[...] = jnp.zeros_like(dg_ref)
            db_ref[...] = jnp.zeros_like(db_ref)
            dgm_ref[...] = jnp.zeros_like(dgm_ref)
            if head:
                loss_ref[...] = jnp.zeros_like(loss_ref)

        dh = dh_ref[...]
        if head:
            err = dh - tg_ref[...]
            loss_ref[...] += 0.5 * jnp.sum(jnp.mean(err * err, axis=-1, keepdims=True))
            dh = err / d
        zv = z_ref[...]
        mu = jnp.mean(zv, axis=-1, keepdims=True)
        zc = zv - mu
        rstd = lax.rsqrt(jnp.mean(zc * zc, axis=-1, keepdims=True) + LN_EPS)
        xhat = zc * rstd
        dxh = dh * g_ref[...]
        dz = rstd * (dxh - jnp.mean(dxh, axis=-1, keepdims=True)
                     - xhat * jnp.mean(dxh * xhat, axis=-1, keepdims=True))
        dz_ref[...] = dz
        df_ref[...] = ((coef * gm_ref[...]) * dz).astype(BF16)
        dg_ref[...] += _colsum(dh * xhat)
        db_ref[...] += _colsum(dh)
        dgm_ref[...] += _colsum(coef * f_ref[...].astype(F32) * dz)

    vec = pl.BlockSpec((1, d), lambda i: (0, 0))
    row = pl.BlockSpec((tm, d), lambda i: (i, 0))
    ins = [dh] + ([target] if head else []) + [z, f, ln_g, gmod]
    in_specs = [row] + ([row] if head else []) + [row, row, vec, vec]
    out_shape = [_sds((t, d), F32), _sds((t, d), BF16)] + [_sds((1, d), F32)] * 3
    out_specs = [row, row, vec, vec, vec]
    if head:
        out_shape.append(_sds((1, LANE), F32))
        out_specs.append(pl.BlockSpec((1, LANE), lambda i: (0, 0)))
    return pl.pallas_call(
        body, name=name, out_shape=out_shape, grid=(t // tm,), in_specs=in_specs, out_specs=out_specs,
        compiler_params=_params("arbitrary"),
    )(*ins)


def ffn_bwd_act(df, w, a, b, name):
    t, d = df.shape
    fdim = w.shape[0]
    bw = fdim // (N_DEV // 2)
    tm = _row_tile(t, 512, 16)

    def body(df_ref, w_ref, a_ref, b_ref, o_ref):
        ds = _dot_nt(df_ref[...], w_ref[...])
        av = a_ref[...].astype(F32)
        sg = _sigmoid(av)
        o_ref[0] = (ds * b_ref[...].astype(F32) * (sg * (1.0 + av * (1.0 - sg)))).astype(BF16)
        o_ref[1] = (ds * (av * sg)).astype(BF16)

    act = pl.BlockSpec((tm, bw), lambda j, i: (i, j))
    return pl.pallas_call(
        body, name=name, out_shape=_sds((2, t, fdim), BF16), grid=(fdim // bw, t // tm),
        in_specs=[pl.BlockSpec((tm, d), lambda j, i: (i, 0)), pl.BlockSpec((bw, d), lambda j, i: (j, 0)), act, act],
        out_specs=pl.BlockSpec((2, tm, bw), lambda j, i: (0, i, j)),
        compiler_params=_params("parallel", "parallel"),
    )(df, w, a, b)


def matmul_tn(name, a, a_block, a_map, b, b_block, b_map, out_shape, o_block, o_map, n_out, mod=None,
              mod_b=False):
    tk = [s for s in a_block if s is not None][0]
    nk = a.shape[-2] // tk
    m, nn = [s for s in o_block if s is not None]

    def body(*refs):
        if mod is None:
            a_ref, b_ref, o_ref, acc = refs
        else:
            a_ref, sh_ref, sc_ref, b_ref, o_ref, acc = refs
        k = pl.program_id(1)

        @pl.when(k == 0)
        def _():
            acc[...] = jnp.zeros_like(acc)

        av, bv = a_ref[...], b_ref[...]
        if mod is not None and mod_b:
            bv = _modulate(bv, sh_ref[...], sc_ref[...])
        elif mod is not None:
            av = _modulate(av, sh_ref[...], sc_ref[...])
        acc[...] += _dot_tn(av, bv)

        @pl.when(k == nk - 1)
        def _():
            o_ref[...] = acc[...].astype(o_ref.dtype)

    ins = [a] + (list(mod) if mod is not None else []) + [b]
    in_specs = [pl.BlockSpec(a_block, a_map)]
    if mod is not None:
        vec = pl.BlockSpec((1, mod[0].shape[1]), lambda n, k: (0, 0))
        in_specs += [vec, vec]
    in_specs.append(pl.BlockSpec(b_block, b_map))
    return pl.pallas_call(
        body, name=name, out_shape=out_shape, grid=(n_out, nk), in_specs=in_specs,
        out_specs=pl.BlockSpec(o_block, o_map), scratch_shapes=[pltpu.VMEM((m, nn), F32)],
        compiler_params=_params("parallel", "arbitrary"),
    )(*ins)


def matmul_nt_blocks(name, dy, dy_block, dy_map, w3, t, resid=None):
    tm, bw = [s for s in dy_block if s is not None]
    rows = w3.ndim == 2
    if rows:
        nk, n = w3.shape[0] // bw, w3.shape[1]
    else:
        nk, n, _ = w3.shape

    def body(*refs):
        if resid is None:
            dy_ref, w_ref, o_ref, acc = refs
        else:
            dy_ref, w_ref, dz_ref, hin_ref, sc_ref, o_ref, dsc_ref, dsh_ref, acc = refs
        i, k = pl.program_id(0), pl.program_id(1)

        @pl.when(k == 0)
        def _():
            acc[...] = jnp.zeros_like(acc)

        if resid is not None:
            @pl.when((k == 0) & (i == 0))
            def _():
                dsc_ref[...] = jnp.zeros_like(dsc_ref)
                dsh_ref[...] = jnp.zeros_like(dsh_ref)

        acc[...] += _dot(dy_ref[...], w_ref[...]) if rows else _dot_nt(dy_ref[...], w_ref[...])

        @pl.when(k == nk - 1)
        def _():
            du = acc[...]
            if resid is None:
                o_ref[...] = du.astype(o_ref.dtype)
            else:
                o_ref[...] = ALPHA * dz_ref[...] + du * (1.0 + sc_ref[...])
                dsc_ref[...] += _colsum(du * hin_ref[...])
                dsh_ref[...] += _colsum(du)

    row = pl.BlockSpec((tm, n), lambda i, k: (i, 0))
    vec = pl.BlockSpec((1, n), lambda i, k: (0, 0))
    w_spec = pl.BlockSpec((bw, n), lambda i, k: (k, 0)) if rows else pl.BlockSpec((None, n, bw), lambda i, k: (k, 0, 0))
    in_specs = [pl.BlockSpec(dy_block, dy_map), w_spec]
    ins = [dy, w3]
    if resid is None:
        out_shape, out_specs = _sds((t, n), BF16), row
    else:
        ins += list(resid)
        once = pl.BlockSpec((tm, n), lambda i, k: (i, 0), pipeline_mode=pl.Buffered(1))
        in_specs += [once, once, vec]
        out_shape = [_sds((t, n), F32), _sds((1, n), F32), _sds((1, n), F32)]
        out_specs = [row, vec, vec]
    return pl.pallas_call(
        body, name=name, out_shape=out_shape, grid=(t // tm, nk), in_specs=in_specs, out_specs=out_specs,
        scratch_shapes=[pltpu.VMEM((tm, n), F32)], compiler_params=_params("arbitrary", "arbitrary"),
    )(*ins)


REL_W = KW + QB


def bias_table(rel_bias):
    nh, n_rel = rel_bias.shape
    lo = KW - QB - REL_CLIP
    hi = KW - lo - n_rel
    assert n_rel == REL_CLIP + CHUNK and lo >= 0 and hi >= 0
    first, last = rel_bias[:, :1], rel_bias[:, -1:]
    row = jnp.concatenate([jnp.broadcast_to(first, (nh, lo)), rel_bias, jnp.broadcast_to(last, (nh, hi)),
                           jnp.broadcast_to(first, (nh, QB))], axis=1)
    table = jnp.tile(row, (1, QB))[:, :QB * (REL_W - 1)].reshape(nh, QB, REL_W - 1)[:, :, :KW]
    q = np.arange(QB)[:, None] // CHUNK
    k = np.arange(KW)[None, :] // CHUNK
    band = (k >= q) & (k <= q + A_PAST_CHUNKS)
    return jnp.where(band[None], table, NEG)


def bias_grad_skew(dbias):
    nh = dbias.shape[0]
    flat = jnp.pad(dbias, ((0, 0), (0, 0), (0, REL_W - 1 - KW))).reshape(nh, QB * (REL_W - 1))
    return jnp.pad(flat, ((0, 0), (0, QB))).reshape(nh, QB, REL_W)


def bias_clip_map(n_rel):
    m = np.arange(REL_W)
    dist = np.where(m < KW, m, m - REL_W) - (KW - QB)
    idx = np.clip(dist, -REL_CLIP, CHUNK - 1) + REL_CLIP
    return (idx[:, None] == np.arange(n_rel)[None, :]).astype(np.float32)


PAIR = 2


def _pair_specs(col, rows_of):
    return [pl.BlockSpec((QB, LANE), functools.partial(lambda r, h, i: (rows_of(r, i), col // LANE + h), r))
            for r in range(3)]


def _earlier(r, i):
    return jnp.maximum(i - 2 + r, 0)


def _head_lanes(hh, dh):
    lane = lax.broadcasted_iota(jnp.int32, (1, LANE), 1)
    return (lane < dh) if hh == 0 else (lane >= dh)


def _only(x, lanes):
    return jnp.where(lanes, x, jnp.zeros_like(x))


def _scores(q, ks, bias, i, scale):
    s = jnp.concatenate([_dot_nt(q, kk) for kk in ks], axis=1) * scale + bias
    col = lax.broadcasted_iota(jnp.int32, s.shape, 1)
    return jnp.where(col >= (2 - i) * QB, s, NEG)


def attn_fwd(p, cols, bias, dh, name):
    t = p.shape[0]
    nh = bias.shape[0]
    scale = dh ** -0.5

    def body(q_ref, k0, k1, k2, v0, v1, v2, b_ref, o_ref, lse_ref):
        i = pl.program_id(1)
        q = q_ref[...]
        outs = []
        for hh in range(PAIR):
            lanes = _head_lanes(hh, dh)
            s = _scores(q, [_only(kk[...], lanes) for kk in (k0, k1, k2)], b_ref[hh], i, scale)
            m = jnp.max(s, axis=-1, keepdims=True)
            e = jnp.exp(s - m)
            l = jnp.sum(e, axis=-1, keepdims=True)
            eb = e.astype(BF16)
            o = sum(_dot(eb[:, r * QB:(r + 1) * QB], vv[...]) for r, vv in enumerate((v0, v1, v2)))
            outs.append(o / l)
            lse_ref[hh] = m + jnp.log(l)
        o_ref[...] = jnp.where(_head_lanes(0, dh), outs[0], outs[1]).astype(BF16)

    st = pl.BlockSpec((PAIR, QB, 1), lambda h, i: (h, i, 0))
    return pl.pallas_call(
        body, name=name, out_shape=[_sds((t, nh * dh), BF16), _sds((nh, t, 1), F32)], grid=(nh // PAIR, t // QB),
        in_specs=[pl.BlockSpec((QB, LANE), lambda h, i: (i, cols["qa"] // LANE + h))]
        + _pair_specs(cols["ka"], _earlier) + _pair_specs(cols["va"], _earlier)
        + [pl.BlockSpec((PAIR, QB, KW), lambda h, i: (h, 0, 0))],
        out_specs=[pl.BlockSpec((QB, LANE), lambda h, i: (i, h)), st],
        compiler_params=_params("parallel", "parallel"),
    )(p, p, p, p, p, p, p, bias)


def attn_bwd(p, cols, bias, lse, dy, dh, name):
    t = p.shape[0]
    nh = bias.shape[0]
    nb = t // QB
    scale = dh ** -0.5

    def body(q_ref, k0, k1, k2, v0, v1, v2, b_ref, lse_ref, dy_ref, dq_ref, dk_ref, dv_ref, db_ref, dk_acc, dv_acc):
        i = pl.program_id(1)

        @pl.when(i == 0)
        def _():
            db_ref[...] = jnp.zeros_like(db_ref)
            dk_acc[...] = jnp.zeros_like(dk_acc)
            dv_acc[...] = jnp.zeros_like(dv_acc)

        q, dyv = q_ref[...], dy_ref[...]
        ks = [k0[...], k1[...], k2[...]]
        dqs, dks, dvs = [], [], []
        for hh in range(PAIR):
            lanes = _head_lanes(hh, dh)
            s = _scores(q, [_only(kk, lanes) for kk in ks], b_ref[hh], i, scale)
            prob = jnp.exp(s - lse_ref[hh])
            dprob = jnp.concatenate([_dot_nt(dyv, _only(vv[...], lanes)) for vv in (v0, v1, v2)], axis=1)
            delta = jnp.sum(prob * dprob, axis=-1, keepdims=True)
            ds = prob * (dprob - delta)
            dsb, pb = ds.astype(BF16), prob.astype(BF16)
            dqs.append(sum(_dot(dsb[:, r * QB:(r + 1) * QB], kk) for r, kk in enumerate(ks)))
            dks.append([_dot_tn(dsb[:, r * QB:(r + 1) * QB], q) for r in range(3)])
            dvs.append([_dot_tn(pb[:, r * QB:(r + 1) * QB], dyv) for r in range(3)])
            db_ref[hh] += ds
        first = _head_lanes(0, dh)
        dq_ref[...] = (jnp.where(first, dqs[0], dqs[1]) * scale).astype(BF16)
        for r in range(3):
            rows = pl.ds(pl.multiple_of(_earlier(r, i) * QB, QB), QB)
            dk_acc[rows, :] += jnp.where(first, dks[0][r], dks[1][r])
            dv_acc[rows, :] += jnp.where(first, dvs[0][r], dvs[1][r])

        @pl.when(i == nb - 1)
        def _():
            dk_ref[...] = (dk_acc[...] * scale).astype(BF16)
            dv_ref[...] = dv_acc[...].astype(BF16)

    st = pl.BlockSpec((PAIR, QB, 1), lambda h, i: (h, i, 0))
    tab = pl.BlockSpec((PAIR, QB, KW), lambda h, i: (h, 0, 0))
    own = pl.BlockSpec((QB, LANE), lambda h, i: (i, h))
    whole = pl.BlockSpec((t, LANE), lambda h, i: (0, h))
    return pl.pallas_call(
        body, name=name,
        out_shape=[_sds((t, nh * dh), BF16)] * 3 + [_sds((nh, QB, KW), F32)],
        grid=(nh // PAIR, nb),
        in_specs=[pl.BlockSpec((QB, LANE), lambda h, i: (i, cols["qa"] // LANE + h))]
        + _pair_specs(cols["ka"], _earlier) + _pair_specs(cols["va"], _earlier) + [tab, st, own],
        out_specs=[own, whole, whole, tab],
        scratch_shapes=[pltpu.VMEM((t, LANE), F32), pltpu.VMEM((t, LANE), F32)],
        compiler_params=_params("parallel", "arbitrary"),
    )(p, p, p, p, p, p, p, bias, lse, dy)


def _tri(strict):
    r = lax.broadcasted_iota(jnp.int32, (CHUNK, CHUNK), 0)
    c = lax.broadcasted_iota(jnp.int32, (CHUNK, CHUNK), 1)
    return jnp.where((c < r) if strict else (c <= r), 1.0, 0.0).astype(F32)


def _gate(lr, wa, ba):
    y = _dot(lr, wa) + ba
    return (jnp.minimum(y, 0.0) - jnp.log(1.0 + jnp.exp(-jnp.abs(y)))) / GATE_TAU, y


def _decays(la):
    cum = jnp.dot(_tri(False), la, preferred_element_type=F32, precision=HI)
    last = cum[CHUNK - 1:CHUNK, :]
    return jnp.exp(last - cum), jnp.exp(last)


def _gla_specs(cols, hk, hv, order):
    def at(start, width):
        return pl.BlockSpec((GB, width), lambda h, i: (order(i), start // width + h))
    return [at(cols["qb"], hk), at(cols["kb"], hk), at(cols["vb"], hv), at(cols["rb"], hv),
            pl.BlockSpec((GB, LANE), lambda h, i: (order(i), cols["lr"] // LANE))]


def gla_fwd(p, cols, wa, ba, gn, nh, hk, hv, name):
    t = p.shape[0]
    nc = t // CHUNK
    scale = hk ** -0.5
    per = GB // CHUNK

    def body(q_ref, k_ref, v_ref, r_ref, lr_ref, wa_ref, ba_ref, gn_ref, o_ref, y_ref, st_ref, state):
        @pl.when(pl.program_id(1) == 0)
        def _():
            state[...] = jnp.zeros_like(state)

        for c in range(per):
            rows = pl.ds(c * CHUNK, CHUNK)
            la, _ = _gate(lr_ref[rows, :], wa_ref[...], ba_ref[...])
            w, decay = _decays(la)
            kdec = (k_ref[rows, :].astype(F32) * w).astype(BF16)
            st = decay * state[...] + _dot_tn(v_ref[rows, :], kdec)
            state[...] = st
            st_ref[c] = st
            o = _dot_nt(q_ref[rows, :], st.astype(BF16)) * scale
            o_ref[rows, :] = o
            rinv = lax.rsqrt(jnp.mean(o * o, axis=-1, keepdims=True) + RMS_EPS)
            rv = r_ref[rows, :].astype(F32)
            y_ref[rows, :] = (o * rinv * gn_ref[...] * (rv * _sigmoid(rv))).astype(BF16)

    return pl.pallas_call(
        body, name=name,
        out_shape=[_sds((t, nh * hv), F32), _sds((t, nh * hv), BF16), _sds((nh, nc, hv, hk), F32)],
        grid=(nh, t // GB),
        in_specs=_gla_specs(cols, hk, hv, lambda i: i)
        + [pl.BlockSpec((LANE, hk), lambda h, i: (0, h)), pl.BlockSpec((1, hk), lambda h, i: (0, h)),
           pl.BlockSpec((1, hv), lambda h, i: (0, 0))],
        out_specs=[pl.BlockSpec((GB, hv), lambda h, i: (i, h)), pl.BlockSpec((GB, hv), lambda h, i: (i, h)),
                   pl.BlockSpec((None, per, hv, hk), lambda h, i: (h, i, 0, 0))],
        scratch_shapes=[pltpu.VMEM((hv, hk), F32)], compiler_params=_params("parallel", "arbitrary"),
    )(p, p, p, p, p, wa, ba, gn)


def gla_bwd(p, cols, wa, ba, gn, o, states, dy, nh, hk, hv, name):
    t = p.shape[0]
    nblk = t // GB
    scale = hk ** -0.5
    per = GB // CHUNK

    def rev(i):
        return nblk - 1 - i

    def body(q_ref, k_ref, v_ref, r_ref, lr_ref, wa_ref, ba_ref, gn_ref, o_ref, st_ref, sp_ref, dy_ref,
             dq_ref, dk_ref, dv_ref, dr_ref, dg_ref, dgn_ref, carry):
        h, i = pl.program_id(0), pl.program_id(1)

        @pl.when(i == 0)
        def _():
            carry[...] = jnp.zeros_like(carry)

        @pl.when((i == 0) & (h == 0))
        def _():
            dgn_ref[...] = jnp.zeros_like(dgn_ref)

        gnv = gn_ref[...]
        for c in reversed(range(per)):
            rows = pl.ds(c * CHUNK, CHUNK)
            rv = r_ref[rows, :].astype(F32)
            sg = _sigmoid(rv)
            dyv = dy_ref[rows, :].astype(F32)
            ov = o_ref[rows, :]
            rinv = lax.rsqrt(jnp.mean(ov * ov, axis=-1, keepdims=True) + RMS_EPS)
            dn = dyv * (rv * sg)
            dr_ref[rows, :] = (dyv * (ov * rinv * gnv) * (sg * (1.0 + rv * (1.0 - sg)))).astype(BF16)
            dgn_ref[...] += _colsum(dn * ov * rinv)
            dxh = dn * gnv
            do = rinv * dxh - ov * (rinv * rinv * rinv) * jnp.mean(dxh * ov, axis=-1, keepdims=True)
            dob = (do * scale).astype(BF16)
            qv, kv, vv = q_ref[rows, :], k_ref[rows, :], v_ref[rows, :]
            dq_ref[rows, :] = _dot(dob, st_ref[c].astype(BF16)).astype(BF16)
            dst = carry[...] + _dot_tn(dob, qv)
            if c > 0:
                prev = st_ref[c - 1]
            else:
                prev = jnp.where(i == nblk - 1, 0.0, sp_ref[0])
            ddecay = _colsum(dst * prev)
            la, y = _gate(lr_ref[rows, :], wa_ref[...], ba_ref[...])
            w, decay = _decays(la)
            kf = kv.astype(F32)
            kdec = (kf * w).astype(BF16)
            dstb = dst.astype(BF16)
            dkdec = _dot(vv, dstb)
            dv_ref[rows, :] = _dot_nt(kdec, dstb).astype(BF16)
            dk_ref[rows, :] = (dkdec * w).astype(BF16)
            e = dkdec * kf * w
            dla = jnp.dot(_tri(True), e, preferred_element_type=F32, precision=HI) + ddecay * decay
            dg_ref[rows, :] = dla * (1.0 / GATE_TAU) * _sigmoid(-y)
            carry[...] = decay * dst

    per_head = lambda width: pl.BlockSpec((GB, width), lambda h, i: (rev(i), h))
    return pl.pallas_call(
        body, name=name,
        out_shape=[_sds((t, nh * hk), BF16), _sds((t, nh * hk), BF16), _sds((t, nh * hv), BF16),
                   _sds((t, nh * hv), BF16), _sds((t, nh * hk), F32), _sds((1, hv), F32)],
        grid=(nh, nblk),
        in_specs=_gla_specs(cols, hk, hv, rev)
        + [pl.BlockSpec((LANE, hk), lambda h, i: (0, h)), pl.BlockSpec((1, hk), lambda h, i: (0, h)),
           pl.BlockSpec((1, hv), lambda h, i: (0, 0)), per_head(hv),
           pl.BlockSpec((None, per, hv, hk), lambda h, i: (h, rev(i), 0, 0)),
           pl.BlockSpec((None, 1, hv, hk), lambda h, i: (h, jnp.maximum(rev(i) * per - 1, 0), 0, 0)),
           per_head(hv)],
        out_specs=[per_head(hk), per_head(hk), per_head(hv), per_head(hv), per_head(hk),
                   pl.BlockSpec((1, hv), lambda h, i: (0, 0))],
        scratch_shapes=[pltpu.VMEM((hv, hk), F32)], compiler_params=_params("arbitrary", "arbitrary"),
    )(p, p, p, p, p, wa, ba, gn, o, states, states, dy)


def gate_bwd(p, lr_col, dg, wa, name):
    t, kd = dg.shape
    tm = _row_tile(t, 512, 16)

    def body(lr_ref, dg_ref, wa_ref, dlr_ref, dwa_ref, dba_ref):
        @pl.when(pl.program_id(0) == 0)
        def _():
            dwa_ref[...] = jnp.zeros_like(dwa_ref)
            dba_ref[...] = jnp.zeros_like(dba_ref)

        g = dg_ref[...]
        gb = g.astype(BF16)
        dlr_ref[...] = _dot_nt(gb, wa_ref[...]).astype(BF16)
        dwa_ref[...] += _dot_tn(lr_ref[...], gb)
        dba_ref[...] += _colsum(g)

    return pl.pallas_call(
        body, name=name, out_shape=[_sds((t, LANE), BF16), _sds((LANE, kd), F32), _sds((1, kd), F32)],
        grid=(t // tm,),
        in_specs=[pl.BlockSpec((tm, LANE), lambda i: (i, lr_col // LANE)), pl.BlockSpec((tm, kd), lambda i: (i, 0)),
                  pl.BlockSpec((LANE, kd), lambda i: (0, 0))],
        out_specs=[pl.BlockSpec((tm, LANE), lambda i: (i, 0)), pl.BlockSpec((LANE, kd), lambda i: (0, 0)),
                   pl.BlockSpec((1, kd), lambda i: (0, 0))],
        compiler_params=_params("arbitrary"),
    )(p, dg, wa)


def proj_merge(ya, yb, wa3, wb3, p, ga_col, gb_col, name):
    t, kd = ya.shape
    nb, _, bw = wa3.shape
    tm = _row_tile(t, 512, 16)

    def body(ya_ref, yb_ref, wa_ref, wb_ref, ga_ref, gb_ref, pa_ref, pb_ref, mg_ref):
        pa = _dot(ya_ref[...], wa_ref[...])
        pb = _dot(yb_ref[...], wb_ref[...])
        pa_ref[...] = pa.astype(BF16)
        pb_ref[...] = pb.astype(BF16)
        mg_ref[...] = (_sigmoid(ga_ref[...].astype(F32)) * pa + _sigmoid(gb_ref[...].astype(F32)) * pb).astype(BF16)

    act = pl.BlockSpec((tm, kd), lambda j, i: (i, 0))
    wsp = pl.BlockSpec((None, kd, bw), lambda j, i: (j, 0, 0))
    out = pl.BlockSpec((tm, bw), lambda j, i: (i, j))
    return pl.pallas_call(
        body, name=name, out_shape=[_sds((t, nb * bw), BF16)] * 3, grid=(nb, t // tm),
        in_specs=[act, act, wsp, wsp, pl.BlockSpec((tm, bw), lambda j, i: (i, ga_col // bw + j)),
                  pl.BlockSpec((tm, bw), lambda j, i: (i, gb_col // bw + j))],
        out_specs=[out] * 3, compiler_params=_params("parallel", "parallel"),
    )(ya, yb, wa3, wb3, p, p)


def merge_bwd(dm, w, p, ga_col, gb_col, pa, pb, name):
    t, d = dm.shape
    n = w.shape[0]
    tn = _row_tile(n, 512, LANE)
    tm = _row_tile(t, 256, 16)

    def body(dm_ref, w_ref, ga_ref, gb_ref, pa_ref, pb_ref, dpa_ref, dpb_ref, dga_ref, dgb_ref):
        dmg = _dot_nt(dm_ref[...], w_ref[...])
        sa = _sigmoid(ga_ref[...].astype(F32))
        sb = _sigmoid(gb_ref[...].astype(F32))
        dpa_ref[...] = (dmg * sa).astype(BF16)
        dpb_ref[...] = (dmg * sb).astype(BF16)
        dga_ref[...] = (dmg * pa_ref[...].astype(F32) * sa * (1.0 - sa)).astype(BF16)
        dgb_ref[...] = (dmg * pb_ref[...].astype(F32) * sb * (1.0 - sb)).astype(BF16)

    out = pl.BlockSpec((tm, tn), lambda j, i: (i, j))
    return pl.pallas_call(
        body, name=name, out_shape=[_sds((t, n), BF16)] * 4, grid=(n // tn, t // tm),
        in_specs=[pl.BlockSpec((tm, d), lambda j, i: (i, 0)), pl.BlockSpec((tn, d), lambda j, i: (j, 0)),
                  pl.BlockSpec((tm, tn), lambda j, i: (i, ga_col // tn + j)),
                  pl.BlockSpec((tm, tn), lambda j, i: (i, gb_col // tn + j)), out, out],
        out_specs=[out] * 4, compiler_params=_params("parallel", "parallel"),
    )(dm, w, p, p, pa, pb)


def rel_bias_grad(skew, clip_map, name):
    nh, _, jd = skew.shape
    n_rel = clip_map.shape[1]

    def body(s_ref, c_ref, o_ref):
        sums = jnp.concatenate([_colsum(s_ref[h]) for h in range(nh)], axis=0)
        o_ref[...] = jnp.dot(sums, c_ref[...], preferred_element_type=F32, precision=HI)

    return pl.pallas_call(
        body, name=name, out_shape=_sds((nh, n_rel), F32), in_specs=[VMEM_SPEC, VMEM_SPEC], out_specs=VMEM_SPEC,
        compiler_params=pltpu.CompilerParams(vmem_limit_bytes=VMEM_LIMIT),
    )(skew, clip_map)


MIX_BLOCK = 9 * LANE


def mix_layout(d, a_width, bk, bv):
    main = 3 * a_width + 2 * bk + 2 * bv
    cols = {"qa": 0, "ka": a_width, "va": 2 * a_width, "qb": 3 * a_width, "kb": 3 * a_width + bk,
            "vb": 3 * a_width + 2 * bk, "rb": 3 * a_width + 2 * bk + bv, "ga": main, "gb": main + d,
            "lr": main + 2 * d}
    total = main + 2 * d + LANE
    assert total % MIX_BLOCK == 0
    return cols, main, total


def _mix_pieces(per, main, rank, d):
    out = []
    for lo, hi in ((0, main), (main + rank, main + rank + 2 * d), (main, main + rank)):
        while lo < hi:
            cut = min(hi, (lo // per + 1) * per)
            out.append((lo, cut))
            lo = cut
    return out


def mix_weight_in(g3, main, rank):
    d = g3.shape[2]
    flat = g3.reshape(-1, d)
    return jnp.concatenate([flat[:main], flat[main + rank:], flat[main:main + rank],
                            jnp.zeros((LANE - rank, d), g3.dtype)], axis=0)


def mix_weight_grad_out(gt, main, rank, per):
    d = gt.shape[1]
    blocks = [[] for _ in range(N_DEV)]
    pos = 0
    for lo, hi in _mix_pieces(per, main, rank, d):
        blocks[lo // per].append((lo, gt[pos:pos + hi - lo]))
        pos += hi - lo
    return jnp.stack([jnp.concatenate([x for _, x in sorted(b, key=lambda e: e[0])], axis=0) for b in blocks])


def ffn_forward(h, sh, sc, g, w_in3, w_out_of, ln_g, ln_b, tag):
    a, b, s = ffn_in(h, sh, sc, w_in3, f"{tag}_in")
    w_out = w_out_of(s)
    f, z, hout = out_ln(s, w_out, h, g, ln_g, ln_b, 0.5, f"{tag}_out")
    return hout, (h, a, b, s, f, z), w_out


def ffn_backward_weights(dh, saved, sh, sc, g, w_in3, w_out, ln_g, tag, target=None):
    hin, a, b, s, f, z = saved
    t, d = hin.shape
    nb, _, bw = w_in3.shape
    half = nb // 2
    fdim = w_out.shape[0]
    res = ln_bwd(dh, z, f, ln_g, g, 0.5, f"{tag}_ln_bwd", target=target)
    dz, df, dln_g, dln_b, dg = res[:5]
    dab = ffn_bwd_act(df, w_out, a, b, f"{tag}_act_bwd")
    tk = _row_tile(t, 512, 16)
    dw_out = matmul_tn(f"{tag}_dwout", s, (tk, bw), lambda n, k: (k, n), df, (tk, d), lambda n, k: (k, 0),
                       _sds((fdim, d), BF16), (bw, d), lambda n, k: (n, 0), fdim // bw)
    dw_in = matmul_tn(f"{tag}_dwin", hin, (tk, d), lambda n, k: (k, 0), dab, (None, tk, bw),
                      lambda n, k: (n // half, k, n % half), _sds((nb, d, bw), BF16), (None, d, bw),
                      lambda n, k: (n, 0, 0), nb, mod=(sh, sc))
    grads = dict(w_in=dw_in, w_out=dw_out.reshape(N_DEV, fdim // N_DEV, d), ln_g=dln_g, ln_b=dln_b, g=dg)
    return (dab, dz), grads, (res[5] if target is not None else None)


def ffn_backward_input(carry, saved, sc, w_in3, tag):
    dab, dz = carry
    hin = saved[0]
    t, d = hin.shape
    nb, _, bw = w_in3.shape
    half = nb // 2
    fdim = half * bw
    tm = _row_tile(t, 256, 16)

    def contract(dy_ref, w_ref):
        return sum(_dot_nt(dy_ref[:, j * bw:(j + 1) * bw], w_ref[j]) for j in range(half))

    def first(dy_ref, w_ref, o_ref):
        o_ref[...] = contract(dy_ref, w_ref)

    def second(dy_ref, w_ref, part_ref, dz_ref, hin_ref, sc_ref, o_ref, dsc_ref, dsh_ref):
        @pl.when(pl.program_id(0) == 0)
        def _():
            dsc_ref[...] = jnp.zeros_like(dsc_ref)
            dsh_ref[...] = jnp.zeros_like(dsh_ref)

        du = part_ref[...] + contract(dy_ref, w_ref)
        o_ref[...] = ALPHA * dz_ref[...] + du * (1.0 + sc_ref[...])
        dsc_ref[...] += _colsum(du * hin_ref[...])
        dsh_ref[...] += _colsum(du)

    def specs(which):
        return [pl.BlockSpec((None, tm, fdim), lambda i: (which, i, 0)),
                pl.BlockSpec((half, d, bw), lambda i: (which, 0, 0), pipeline_mode=pl.Buffered(1))]

    row = pl.BlockSpec((tm, d), lambda i: (i, 0))
    vec = pl.BlockSpec((1, d), lambda i: (0, 0))
    part = pl.pallas_call(
        first, name=f"{tag}_du_a", out_shape=_sds((t, d), F32), grid=(t // tm,), in_specs=specs(0),
        out_specs=row, compiler_params=_params("parallel"),
    )(dab, w_in3)
    return pl.pallas_call(
        second, name=f"{tag}_du_b", out_shape=[_sds((t, d), F32), _sds((1, d), F32), _sds((1, d), F32)],
        grid=(t // tm,), in_specs=specs(1) + [row, row, row, vec], out_specs=[row, vec, vec],
        compiler_params=_params("arbitrary"),
    )(dab, w_in3, part, dz, hin, sc)


def _after(v, token):
    return v if token is None else v + token[:1, :1]


def local_step(x, target, mod, weights_of, grads_ready, grads_sent, rel_bias, w_alpha2, b_alpha, gla_norm_g, lns,
               bias=None):
    t, d = x.shape
    sh1, sc1, g1, sh2, sc2, g2, sh3, sc3, g3 = [mod[i:i + 1] for i in range(N_MOD)]
    ln1_g, ln1_b, ln2_g, ln2_b, ln3_g, ln3_b = lns
    n_heads_a, n_rel = rel_bias.shape
    rank, bk = w_alpha2.shape
    hv = gla_norm_g.shape[1]

    w1 = weights_of("ffn1_in", x)
    h1, saved1, w1["out"] = ffn_forward(x, sh1, sc1, g1, w1["in"], lambda s: weights_of("ffn1_out", s)["out"],
                                        ln1_g, ln1_b, "ffn1")
    wm = weights_of("mix", h1)
    a_width = wm["proj_a"].shape[1]
    bv = wm["proj_b"].shape[1]
    nh_b = bv // hv
    hk = bk // nh_b
    cols, main, total = mix_layout(d, a_width, bk, bv)
    w_mix = mix_weight_in(wm["in_t"], main, rank)
    p = mod_matmul(h1, sh2, sc2, w_mix, MIX_BLOCK, "mix_in")
    bias = bias_table(rel_bias) if bias is None else bias
    dh = a_width // n_heads_a
    assert PAIR * dh == LANE
    ya, lse = attn_fwd(p, cols, bias, dh, "attn_fwd")
    wa_pad = jnp.zeros((LANE, bk), BF16).at[:rank].set(w_alpha2.astype(BF16))
    o_b, yb, states = gla_fwd(p, cols, wa_pad, b_alpha, gla_norm_g, nh_b, hk, hv, "gla_fwd")
    pa, pb, merged = proj_merge(ya, yb, wm["proj_a"], wm["proj_b"], p, cols["ga"], cols["gb"], "proj_merge")
    m, z2, h2 = out_ln(merged, wm["out"], h1, g2, ln2_g, ln2_b, 1.0, "mix_out")
    w3 = weights_of("ffn2", h2)
    h3, saved3, _ = ffn_forward(h2, sh3, sc3, g3, w3["in"], lambda s: w3["out"], ln3_g, ln3_b, "ffn2")

    carry3, gr3, loss = ffn_backward_weights(h3, saved3, sh3, sc3, g3, w3["in"], w3["out"], ln3_g, "ffn2",
                                             target=target)
    token = grads_ready("ffn2", dict(ffn2_in=gr3["w_in"], ffn2_out=gr3["w_out"]))
    dh2, dsc3, dsh3 = ffn_backward_input(carry3, saved3, _after(sc3, token), w3["in"], "ffn2")
    token = grads_sent("ffn2", dh2)
    dz2, dm, dln2_g, dln2_b, dg2 = ln_bwd(dh2, z2, m, _after(ln2_g, token), g2, 1.0, "mix_ln_bwd")
    dpa, dpb, dga, dgb = merge_bwd(dm, wm["out"], p, cols["ga"], cols["gb"], pa, pb, "merge_bwd")
    tk = _row_tile(t, 512, 16)
    dw_mix_out = matmul_tn("mix_dwout", merged, (tk, 512), lambda n, k: (k, n), dm, (tk, d), lambda n, k: (k, 0),
                           _sds((d, d), BF16), (512, d), lambda n, k: (n, 0), d // 512)
    tm = _row_tile(t, 512, 16)
    pbw = wm["proj_a"].shape[2]
    dya = matmul_nt_blocks("proj_a_dy", dpa, (tm, pbw), lambda i, k: (i, k), wm["proj_a"], t)
    dyb = matmul_nt_blocks("proj_b_dy", dpb, (tm, pbw), lambda i, k: (i, k), wm["proj_b"], t)
    dw_pa = matmul_tn("proj_a_dw", ya, (tk, a_width), lambda n, k: (k, 0), dpa, (tk, pbw), lambda n, k: (k, n),
                      _sds((N_DEV, a_width, pbw), BF16), (None, a_width, pbw), lambda n, k: (n, 0, 0), N_DEV)
    dw_pb = matmul_tn("proj_b_dw", yb, (tk, bv), lambda n, k: (k, 0), dpb, (tk, pbw), lambda n, k: (k, n),
                      _sds((N_DEV, bv, pbw), BF16), (None, bv, pbw), lambda n, k: (n, 0, 0), N_DEV)
    dqb, dkb, dvb, drb, dgate, dgn = gla_bwd(p, cols, wa_pad, b_alpha, gla_norm_g, o_b, states, dyb,
                                             nh_b, hk, hv, "gla_bwd")
    dlr, dwa_pad, dba = gate_bwd(p, cols["lr"], dgate, wa_pad, "gate_bwd")
    dqa, dka, dva, dbias = attn_bwd(p, cols, bias, lse, dya, dh, "attn_bwd")
    d_rel = rel_bias_grad(bias_grad_skew(dbias), jnp.asarray(bias_clip_map(n_rel)), "rel_bias_grad")
    dp = jnp.concatenate([dqa, dka, dva, dqb, dkb, dvb, drb,
                          dga, dgb, dlr], axis=1)
    dw_mix_t = matmul_tn("mix_dwin", dp, (tk, MIX_BLOCK), lambda n, k: (k, n), h1, (tk, d), lambda n, k: (k, 0),
                         _sds((total, d), BF16), (MIX_BLOCK, d), lambda n, k: (n, 0), total // MIX_BLOCK,
                         mod=(sh2, sc2), mod_b=True)
    dw_mix_in = mix_weight_grad_out(dw_mix_t, main, rank, wm["in_t"].shape[1])
    token = grads_ready("mix", dict(mix_in=dw_mix_in, proj_a=dw_pa, proj_b=dw_pb,
                                    mix_out=dw_mix_out.reshape(N_DEV, d // N_DEV, d)))
    tm = _row_tile(t, 512, 16)
    dh1, dsc2, dsh2 = matmul_nt_blocks("mix_du", dp, (tm, MIX_BLOCK), lambda i, k: (i, k), w_mix, t,
                                       resid=(dz2, h1, _after(sc2, token)))
    token = grads_sent("mix", dh1)
    carry1, gr1, _ = ffn_backward_weights(dh1, saved1, sh1, sc1, g1, w1["in"], w1["out"], _after(ln1_g, token),
                                          "ffn1")
    token = grads_ready("ffn1", dict(ffn1_in=gr1["w_in"], ffn1_out=gr1["w_out"]))
    dx, dsc1, dsh1 = ffn_backward_input(carry1, saved1, _after(sc1, token), w1["in"], "ffn1")

    dmod = [dsh1, dsc1, gr1["g"], dsh2, dsc2, dg2, dsh3, dsc3, gr3["g"]]
    small = dict(ln1_g=gr1["ln_g"], ln1_b=gr1["ln_b"], ln2_g=dln2_g, ln2_b=dln2_b, ln3_g=gr3["ln_g"],
                 ln3_b=gr3["ln_b"], b_alpha=dba, gla_norm_g=dgn, w_alpha2=dwa_pad[:rank], rel_bias=d_rel)
    return loss, dx, dmod, small


GROUPS = dict(ffn1=("ffn1_in", "ffn1_out"), mix=("mix_in", "proj_a", "proj_b", "mix_out"),
              ffn2=("ffn2_in", "ffn2_out"))
GATHERS = dict(ffn1_in=("ffn1_in",), ffn1_out=("ffn1_out",), mix=GROUPS["mix"], ffn2=GROUPS["ffn2"])
SMALL_REPLICATED = ("b_ada", "ln1_g", "ln1_b", "ln2_g", "ln2_b", "ln3_g", "ln3_b", "b_alpha", "gla_norm_g")
SMALL_SHARDED = ("rel_bias", "w_alpha2")
WEIGHT_ORDER = ("w_ada", "b_ada", "ffn1_w_in", "ffn1_w_out", "ln1_g", "ln1_b", "w_mix_in", "rel_bias", "w_alpha2",
                "b_alpha", "gla_norm_g", "w_proj_a", "w_proj_b", "w_mix_out", "ln2_g", "ln2_b", "ffn2_w_in",
                "ffn2_w_out", "ln3_g", "ln3_b")
BIG_NAME = dict(ffn1_in="ffn1_w_in", ffn1_out="ffn1_w_out", mix_in="w_mix_in", proj_a="w_proj_a",
                proj_b="w_proj_b", mix_out="w_mix_out", ffn2_in="ffn2_w_in", ffn2_out="ffn2_w_out")


def kernel(x, c, w_ada, b_ada, ffn1_w_in, ffn1_w_out, ln1_g, ln1_b, w_mix_in, rel_bias, w_alpha2, b_alpha, gla_norm_g, w_proj_a, w_proj_b, w_mix_out, ln2_g, ln2_b, ffn2_w_in, ffn2_w_out, ln3_g, ln3_b, loss_target, m_w_ada, m_b_ada, m_ffn1_w_in, m_ffn1_w_out, m_ln1_g, m_ln1_b, m_w_mix_in, m_rel_bias, m_w_alpha2, m_b_alpha, m_gla_norm_g, m_w_proj_a, m_w_proj_b, m_w_mix_out, m_ln2_g, m_ln2_b, m_ffn2_w_in, m_ffn2_w_out, m_ln3_g, m_ln3_b, v_w_ada, v_b_ada, v_ffn1_w_in, v_ffn1_w_out, v_ln1_g, v_ln1_b, v_w_mix_in, v_rel_bias, v_w_alpha2, v_b_alpha, v_gla_norm_g, v_w_proj_a, v_w_proj_b, v_w_mix_out, v_ln2_g, v_ln2_b, v_ffn2_w_in, v_ffn2_w_out, v_ln3_g, v_ln3_b):
    env = dict(locals())
    w = {n: env[n] for n in WEIGHT_ORDER}
    mom = {n: env["m_" + n] for n in WEIGHT_ORDER}
    var = {n: env["v_" + n] for n in WEIGHT_ORDER}
    me = _me()
    dev = _lin(me)
    core = jnp.reshape(me[2], (1,)).astype(jnp.int32)
    chip = jnp.reshape(2 * me[0] + me[1], (1,)).astype(jnp.int32)
    d = x.shape[-1]

    def shard(n):
        s = w[BIG_NAME[n]][0].astype(BF16)
        return s.T if n == "mix_in" else s

    dev_idx = jnp.reshape(dev, (1,)).astype(jnp.int32)
    started = {}

    def start(grp, after):
        shards = [shard(n) for n in GATHERS[grp]]
        lands = [place_own(dev_idx, s, f"place_own_{n}") for n, s in zip(GATHERS[grp], shards)]
        started[grp] = gather_start(shards, lands, after, f"gather_start_{grp}")
        return started[grp][-1]

    small_w = all_gather_small(jnp.concatenate([rel_bias[0], w_alpha2[0]], axis=1), "gather_small_w")
    n_rel_cols = rel_bias.shape[-1]
    rel_full = small_w[:, :, :n_rel_cols].transpose(1, 0, 2).reshape(small_w.shape[1], -1)
    wa2_full = small_w[:, :, n_rel_cols:].transpose(1, 0, 2).reshape(small_w.shape[1], -1)

    first, *rest = list(GATHERS)
    order = start(first, small_w[0, :1, :1])

    ada_cols = w_ada.shape[-1]
    bias = bias_table(rel_full)
    c_all = all_gather_small(c, "gather_c", after=(order, bias))[:, 0, :]
    b_cols = lax.dynamic_slice_in_dim(b_ada, dev * ada_cols, ada_cols, axis=1)
    mod_cols = adaln_cols(c_all, w_ada[0], b_cols, "adaln_cols")
    mod_all = all_gather_small(mod_cols, "gather_mod")
    mod = lax.dynamic_index_in_dim(mod_all, dev, axis=1, keepdims=False).reshape(N_MOD, d)

    order = mod_all[0, :1, :1]
    for grp in rest:
        order = start(grp, order)
    mod = _after(mod, order)

    def weights_of(grp, after):
        _, zones = gather_wait(started[grp], after, f"gather_wait_{grp}")
        full = dict(zip(GATHERS[grp], gather_forward(zones, f"gather_forward_{grp}")))
        if grp == "mix":
            return dict(in_t=full["mix_in"], proj_a=full["proj_a"], proj_b=full["proj_b"],
                        out=full["mix_out"].reshape(-1, d))
        return {"in" if n.endswith("_in") else "out": v if n.endswith("_in") else v.reshape(-1, d)
                for n, v in full.items()}

    pairs, exchanges = {}, {}
    last = list(GROUPS)[0]

    def to_chips(grp, grads, got):
        sums = [pair_add(core, grads[n], g, f"grad_pair_add_{n}") for n, g in zip(GROUPS[grp], got)]
        exchanges[grp] = exchange_start(sums, chip_routes, f"grad_chip_start_{grp}")
        return exchanges[grp][-1]

    def grads_ready(grp, grads):
        if grp == last:
            return to_chips(grp, grads, pair_exchange([grads[n] for n in GROUPS[grp]], f"grad_pair_exchange_{grp}"))
        pairs[grp] = exchange_start([grads[n] for n in GROUPS[grp]], pair_routes, f"grad_pair_start_{grp}")
        return pairs[grp][-1]

    def grads_sent(grp, after):
        if grp == last:
            return None
        sent, got = exchange_wait(pairs[grp], pair_routes, after, f"grad_pair_wait_{grp}")
        return to_chips(grp, dict(zip(GROUPS[grp], sent)), got)

    lns = [ln1_g, ln1_b, ln2_g, ln2_b, ln3_g, ln3_b]
    loss, dx, dmod, small = local_step(x[0], loss_target[0], mod, weights_of, grads_ready, grads_sent, rel_full,
                                       wa2_full, b_alpha, gla_norm_g, lns, bias=bias)

    out = {}

    def finish(grp, after):
        sums, recv = exchange_wait(exchanges[grp], chip_routes, after, f"grad_chip_wait_{grp}")
        for n, hsum, r in zip(GROUPS[grp], sums, recv):
            full = BIG_NAME[n]
            if n == "mix_in":
                g = owned_sum(chip, hsum, r, f"owned_sum_{n}").T
                res_n = adamw_sum(g[None], w[full][0], mom[full][0], var[full][0], f"adamw_{n}")
            else:
                res_n = adamw_owned(chip, hsum, r, w[full][0], mom[full][0], var[full][0], f"adamw_{n}")
            out[full] = [o[None] for o in res_n]
            after = res_n[0]
        return after

    order = dx
    for grp in reversed(list(GROUPS)[1:]):
        order = finish(grp, order)

    pieces = (list(dmod) + [small[n].reshape(1, -1) for n in SMALL_REPLICATED[1:] + SMALL_SHARDED] + [loss])
    parts = all_gather_rows(pieces, order, "gather_small_grads")
    loss = jnp.sum(parts[:, 0, parts.shape[2] - loss.shape[1]])
    n_mod = N_MOD * d
    dmod_all = parts[:, 0, :n_mod]
    g_w_ada = adaln_wgrad(c_all, lax.dynamic_slice_in_dim(dmod_all, dev * ada_cols, ada_cols, axis=1), "adaln_wgrad")
    out["w_ada"] = [o[None] for o in adamw_sum(g_w_ada[None], w_ada[0], m_w_ada[0], v_w_ada[0], "adamw_w_ada")]

    sources, where, off = [parts], [], 0
    for n in SMALL_REPLICATED:
        where.append((0, off))
        off += w[n].size
    for n in SMALL_SHARDED:
        rows, cols_local = w[n].shape[1], w[n].shape[2]
        full_part = parts[:, 0, off:off + rows * cols_local * N_DEV].reshape(N_DEV, rows, cols_local * N_DEV)
        mine = lax.dynamic_slice_in_dim(full_part, dev * cols_local, cols_local, axis=2)
        where.append((len(sources), 0))
        sources.append(mine.reshape(N_DEV, 1, rows * cols_local))
        off += rows * cols_local * N_DEV
    names = SMALL_REPLICATED + SMALL_SHARDED
    res = adamw_rows(sources, where, *[[src[n].reshape(1, -1) for n in names] for src in (w, mom, var)],
                     "adamw_small")
    for n, res_n in zip(names, res):
        out[n] = [r.reshape(w[n].shape) for r in res_n]

    finish(last, res[0][0])

    flat = [loss, dx[None]]
    for k in range(4):
        flat += [out[n][k] for n in WEIGHT_ORDER]
    return tuple(flat)
```

```python
import functools

import numpy as np
import jax
import jax.numpy as jnp
from jax import lax
from jax.experimental import pallas as pl
from jax.experimental.pallas import tpu as pltpu

F32 = jnp.float32
BF16 = jnp.bfloat16
MESH = pl.DeviceIdType.MESH
N_DEV = 8
N_CHIP = 4

CHUNK = 64
A_PAST_CHUNKS = 8
REL_CLIP = 256
GATE_TAU = 16.0
N_MOD = 9
DEPTH = 1
ALPHA = (2.0 * DEPTH) ** 0.25
LN_EPS = 1e-5
RMS_EPS = 1e-6
ADAM_LR = 0.001
ADAM_B1 = 0.9
ADAM_B2 = 0.999
ADAM_EPS = 1e-08
ADAM_WD = 0.01
ADAM_STEP = 10

LANE = 128
VMEM_LIMIT = 56 * 2 ** 20
QB = 4 * CHUNK
KW = 3 * QB
GB = 8 * CHUNK
NEG = -1e30
HI = lax.Precision.HIGHEST

ANY = pl.BlockSpec(memory_space=pl.ANY)
VMEM_SPEC = pl.BlockSpec(memory_space=pltpu.VMEM)


def _params(*sem):
    return pltpu.CompilerParams(dimension_semantics=sem, vmem_limit_bytes=VMEM_LIMIT)


def _sds(shape, dtype):
    return jax.ShapeDtypeStruct(shape, dtype)


def _dot(a, b):
    return jnp.dot(a, b, preferred_element_type=F32)


def _dot_nt(a, b):
    return lax.dot_general(a, b, (((1,), (1,)), ((), ())), preferred_element_type=F32)


def _dot_tn(a, b):
    return lax.dot_general(a, b, (((0,), (0,)), ((), ())), preferred_element_type=F32)


def _sigmoid(x):
    return 0.5 * jnp.tanh(0.5 * x) + 0.5


def _colsum(x):
    return jnp.sum(x, axis=0, keepdims=True)


def _row_tile(rows, cap, mult):
    for t in range(min(rows, cap), 0, -1):
        if rows % t == 0 and t % mult == 0:
            return t
    return rows


def _me():
    return lax.axis_index("x"), lax.axis_index("y"), lax.axis_index("c")


def _flip(me, k):
    return tuple((1 - p) if (k >> s) & 1 else p for p, s in zip(me, (2, 1, 0)))


def _lin(p):
    return 4 * p[0] + 2 * p[1] + p[2]


def _gather_direct(x_ref, out_ref, send_sems, recv_sems, local_sem):
    me = _me()
    mine = pltpu.make_async_copy(x_ref, out_ref.at[_lin(me)], local_sem)
    mine.start()
    sends = []
    for k in range(1, N_DEV):
        cp = pltpu.make_async_remote_copy(
            src_ref=x_ref, dst_ref=out_ref.at[_lin(me)], send_sem=send_sems.at[k - 1],
            recv_sem=recv_sems.at[k - 1], device_id=_flip(me, k), device_id_type=MESH)
        cp.start()
        sends.append(cp)
    for k in range(1, N_DEV):
        peer = _flip(me, k)
        pltpu.make_async_remote_copy(
            src_ref=x_ref, dst_ref=out_ref.at[_lin(peer)], send_sem=send_sems.at[k - 1],
            recv_sem=recv_sems.at[k - 1], device_id=peer, device_id_type=MESH).wait_recv()
    for cp in sends:
        cp.wait_send()
    mine.wait()


GATHER_SEMS = [pltpu.SemaphoreType.DMA((N_DEV - 1,)), pltpu.SemaphoreType.DMA((N_DEV - 1,)), pltpu.SemaphoreType.DMA]


def all_gather_small(x, name, after=()):
    r, n = x.shape

    def body(x_ref, *refs):
        _gather_direct(x_ref, *refs[len(after):])

    return pl.pallas_call(
        body, name=name, out_shape=_sds((N_DEV, r, n), x.dtype),
        in_specs=[VMEM_SPEC] + [ANY] * len(after), out_specs=VMEM_SPEC, scratch_shapes=GATHER_SEMS,
    )(x, *after)


def all_gather_rows(pieces, after, name):
    sizes = [x.shape[1] for x in pieces]
    total = sum(sizes)
    assert all(n % LANE == 0 for n in sizes)

    def body(*refs):
        ins = refs[:len(pieces)]
        out_ref, row, send_sems, recv_sems, local_sem = refs[len(pieces) + 1:]
        off = 0
        for x_ref, n in zip(ins, sizes):
            row[:, off:off + n] = x_ref[...]
            off += n
        _gather_direct(row, out_ref, send_sems, recv_sems, local_sem)

    return pl.pallas_call(
        body, name=name, out_shape=_sds((N_DEV, 1, total), F32),
        in_specs=[VMEM_SPEC] * len(pieces) + [ANY], out_specs=VMEM_SPEC,
        scratch_shapes=[pltpu.VMEM((1, total), F32)] + GATHER_SEMS,
    )(*pieces, after)


HBM_SPEC = pl.BlockSpec(memory_space=pltpu.HBM)
SEM_SPEC = pl.BlockSpec(memory_space=pltpu.SEMAPHORE)
EFFECT = pltpu.SideEffectType.DATAFLOW_SIDE_EFFECTING
FIRST = N_CHIP


def _hbm(v):
    return pltpu.with_memory_space_constraint(v, pltpu.HBM)


def _other_chips(mx, my):
    return [(1 - mx, my), (mx, 1 - my), (1 - mx, 1 - my)]


def place_own(dev, shard, name):
    rows, cols = shard.shape
    tr, tc = _tile2(rows, cols, 16)

    def body(dev_ref, s_ref, land_ref, o_ref):
        o_ref[...] = s_ref[...]

    land = lax.empty((N_DEV, rows, cols), shard.dtype)
    return pl.pallas_call(
        body, name=name, out_shape=_sds(land.shape, land.dtype),
        grid_spec=pltpu.PrefetchScalarGridSpec(
            num_scalar_prefetch=1, grid=(rows // tr, cols // tc),
            in_specs=[pl.BlockSpec((tr, tc), lambda i, j, d: (i, j)), ANY],
            out_specs=pl.BlockSpec((None, tr, tc), lambda i, j, d: (d[0], i, j))),
        input_output_aliases={2: 0}, compiler_params=_params("parallel", "parallel"),
    )(dev, shard, land)


def gather_start(shards, lands, after, name):
    n = len(shards)

    def body(*refs):
        ins, zones = refs[:n], refs[n:2 * n]
        send_sems, recv_sems = refs[2 * n + 1], refs[2 * n + 2]
        token = refs[-1]
        me = _me()
        mx, my, mc = me
        for a in range(n):
            dst = zones[a].at[_lin(me)]
            targets = [(mx, my, 1 - mc)] + [(*chip, mc) for chip in _other_chips(mx, my)]
            for k, to in enumerate(targets):
                pltpu.make_async_remote_copy(
                    src_ref=ins[a], dst_ref=dst, send_sem=send_sems.at[a * FIRST + k],
                    recv_sem=recv_sems.at[a * FIRST + k], device_id=to, device_id_type=MESH).start()
        token[...] = jnp.zeros_like(token)

    sems = pltpu.SemaphoreType.DMA((n * FIRST,))
    out = pl.pallas_call(
        body, name=name,
        out_shape=(sems, sems, *[pltpu.HBM(s.shape, s.dtype) for s in shards],
                   *[pltpu.HBM(z.shape, z.dtype) for z in lands], _sds((8, LANE), F32)),
        in_specs=[HBM_SPEC] * (2 * n) + [ANY],
        out_specs=(SEM_SPEC, SEM_SPEC, *[HBM_SPEC] * (2 * n), VMEM_SPEC),
        input_output_aliases={a: 2 + a for a in range(2 * n)},
        compiler_params=pltpu.CompilerParams(has_side_effects=EFFECT),
    )(*[_hbm(s) for s in shards], *[_hbm(z) for z in lands], after)
    return out[0], out[1], out[2:2 + n], out[2 + n:2 + 2 * n], out[-1]


def gather_wait(started, after, name):
    send_sems, recv_sems, shards, lands, _ = started
    n = len(shards)

    def body(*refs):
        ins, zones = refs[:n], refs[n:2 * n]
        send_ref, recv_ref = refs[2 * n], refs[2 * n + 1]
        mx, my, mc = _me()
        for a in range(n):
            for k in range(FIRST):
                cp = pltpu.make_async_remote_copy(
                    src_ref=ins[a], dst_ref=zones[a].at[0], send_sem=send_ref.at[a * FIRST + k],
                    recv_sem=recv_ref.at[a * FIRST + k], device_id=(mx, my, 1 - mc), device_id_type=MESH)
                cp.wait_send()
                cp.wait_recv()

    out = pl.pallas_call(
        body, name=name,
        out_shape=(*[pltpu.HBM(s.shape, s.dtype) for s in shards], *[pltpu.HBM(z.shape, z.dtype) for z in lands]),
        in_specs=[HBM_SPEC] * (2 * n) + [SEM_SPEC, SEM_SPEC, ANY], out_specs=tuple([HBM_SPEC] * (2 * n)),
        input_output_aliases={a: a for a in range(2 * n)},
        compiler_params=pltpu.CompilerParams(has_side_effects=EFFECT),
    )(*shards, *lands, send_sems, recv_sems, after)
    return out[:n], out[n:]


def gather_forward(lands, name):
    n = len(lands)
    rel = N_CHIP - 1

    def body(*refs):
        zones, outs = refs[:n], refs[n:2 * n]
        send_sems, recv_sems = refs[2 * n:]
        mx, my, mc = _me()
        chips = _other_chips(mx, my)

        def copy(a, j, core):
            blk = _lin((*chips[j], core))
            return pltpu.make_async_remote_copy(
                src_ref=zones[a].at[blk], dst_ref=outs[a].at[blk], send_sem=send_sems.at[a * rel + j],
                recv_sem=recv_sems.at[a * rel + j], device_id=(mx, my, 1 - mc), device_id_type=MESH)

        sends = [copy(a, j, mc) for a in range(n) for j in range(rel)]
        for cp in sends:
            cp.start()
        for a in range(n):
            for j in range(rel):
                copy(a, j, 1 - mc).wait_recv()
        for cp in sends:
            cp.wait_send()

    return pl.pallas_call(
        body, name=name, out_shape=[_sds(z.shape, z.dtype) for z in lands],
        in_specs=[ANY] * n, out_specs=[ANY] * n, input_output_aliases={a: a for a in range(n)},
        scratch_shapes=[pltpu.SemaphoreType.DMA((n * rel,)), pltpu.SemaphoreType.DMA((n * rel,))],
    )(*lands)


def pair_exchange(gs, name):
    n = len(gs)

    def body(*refs):
        ins, outs = refs[:n], refs[n:2 * n]
        send_sems, recv_sems = refs[2 * n:]
        mx, my, mc = _me()
        cps = []
        for a in range(n):
            for q in range(N_CHIP):
                cp = pltpu.make_async_remote_copy(
                    src_ref=ins[a].at[2 * q + (1 - mc)], dst_ref=outs[a].at[q],
                    send_sem=send_sems.at[a * N_CHIP + q], recv_sem=recv_sems.at[a * N_CHIP + q],
                    device_id=(mx, my, 1 - mc), device_id_type=MESH)
                cp.start()
                cps.append(cp)
        for cp in cps:
            cp.wait()

    return pl.pallas_call(
        body, name=name, out_shape=[_sds((N_CHIP,) + g.shape[1:], g.dtype) for g in gs],
        in_specs=[ANY] * n, out_specs=[ANY] * n,
        scratch_shapes=[pltpu.SemaphoreType.DMA((n * N_CHIP,)), pltpu.SemaphoreType.DMA((n * N_CHIP,))],
    )(*gs)


def _tile2(rows, cols, row_mult):
    tr = _row_tile(rows, 512, row_mult)
    if tr < rows or rows * cols <= 2 ** 20:
        return tr, cols
    return rows, _row_tile(cols, 512, LANE)


def pair_add(core, g, got, name):
    _, rows, cols = g.shape
    tr, tc = _tile2(rows, cols, 16)

    def body(core_ref, g_ref, got_ref, h_ref):
        h_ref[...] = (g_ref[...].astype(F32) + got_ref[...].astype(F32)).astype(h_ref.dtype)

    blk = pl.BlockSpec((None, tr, tc), lambda q, i, j, c: (q, i, j))
    return pl.pallas_call(
        body, name=name, out_shape=_sds((N_CHIP, rows, cols), g.dtype),
        grid_spec=pltpu.PrefetchScalarGridSpec(
            num_scalar_prefetch=1, grid=(N_CHIP, rows // tr, cols // tc),
            in_specs=[pl.BlockSpec((None, tr, tc), lambda q, i, j, c: (2 * q + c[0], i, j)), blk],
            out_specs=blk),
        compiler_params=_params("parallel", "parallel", "parallel"),
    )(core, g, got)


def chip_routes(mx, my, mc):
    return [(2 * px + py, k, (px, py, mc)) for k, (px, py) in enumerate(_other_chips(mx, my))]


def pair_routes(mx, my, mc):
    return [(2 * q + (1 - mc), q, (mx, my, 1 - mc)) for q in range(N_CHIP)]


def exchange_start(hs, routes, name):
    n = len(hs)
    rel = len(routes(0, 0, 0))
    lands = [lax.empty((rel,) + h.shape[1:], h.dtype) for h in hs]

    def body(*refs):
        ins, zones = refs[:n], refs[n:2 * n]
        send_sems, recv_sems = refs[2 * n], refs[2 * n + 1]
        token = refs[-1]
        for a in range(n):
            for k, (src, slot, to) in enumerate(routes(*_me())):
                pltpu.make_async_remote_copy(
                    src_ref=ins[a].at[src], dst_ref=zones[a].at[slot], send_sem=send_sems.at[a * rel + k],
                    recv_sem=recv_sems.at[a * rel + k], device_id=to, device_id_type=MESH).start()
        token[...] = jnp.zeros_like(token)

    sems = pltpu.SemaphoreType.DMA((n * rel,))
    out = pl.pallas_call(
        body, name=name,
        out_shape=(sems, sems, *[pltpu.HBM(h.shape, h.dtype) for h in hs],
                   *[pltpu.HBM(z.shape, z.dtype) for z in lands], _sds((8, LANE), F32)),
        in_specs=[HBM_SPEC] * (2 * n), out_specs=(SEM_SPEC, SEM_SPEC, *[HBM_SPEC] * (2 * n), VMEM_SPEC),
        input_output_aliases={a: 2 + a for a in range(2 * n)},
        compiler_params=pltpu.CompilerParams(has_side_effects=EFFECT),
    )(*[_hbm(h) for h in hs], *[_hbm(z) for z in lands])
    return out[0], out[1], out[2:2 + n], out[2 + n:2 + 2 * n], out[-1]


def exchange_wait(started, routes, after, name):
    send_sems, recv_sems, hs, lands, _ = started
    n = len(hs)
    rel = len(routes(0, 0, 0))

    def body(*refs):
        ins, zones = refs[:n], refs[n:2 * n]
        send_ref, recv_ref = refs[2 * n], refs[2 * n + 1]
        for a in range(n):
            for k, (src, slot, to) in enumerate(routes(*_me())):
                cp = pltpu.make_async_remote_copy(
                    src_ref=ins[a].at[src], dst_ref=zones[a].at[slot], send_sem=send_ref.at[a * rel + k],
                    recv_sem=recv_ref.at[a * rel + k], device_id=to, device_id_type=MESH)
                cp.wait_send()
                cp.wait_recv()

    out = pl.pallas_call(
        body, name=name,
        out_shape=(*[pltpu.HBM(h.shape, h.dtype) for h in hs], *[pltpu.HBM(z.shape, z.dtype) for z in lands]),
        in_specs=[HBM_SPEC] * (2 * n) + [SEM_SPEC, SEM_SPEC, ANY], out_specs=tuple([HBM_SPEC] * (2 * n)),
        input_output_aliases={a: a for a in range(2 * n)},
        compiler_params=pltpu.CompilerParams(has_side_effects=EFFECT),
    )(*hs, *lands, send_sems, recv_sems, after)
    return out[:n], out[n:]


def _adam(w, g, m, v):
    m = ADAM_B1 * m + (1.0 - ADAM_B1) * g
    v = ADAM_B2 * v + (1.0 - ADAM_B2) * (g * g)
    m_hat = m / (1.0 - ADAM_B1 ** ADAM_STEP)
    v_hat = v / (1.0 - ADAM_B2 ** ADAM_STEP)
    delta = -ADAM_LR * (m_hat / (jnp.sqrt(v_hat) + ADAM_EPS) + ADAM_WD * w)
    return delta, m, v


def adamw_owned(chip, h, got, w, m, v, name):
    rows, cols = w.shape
    tr = _row_tile(rows, 256, 16)

    def body(chip_ref, h_ref, got_ref, w_ref, m_ref, v_ref, g_out, d_out, m_out, v_out):
        g = h_ref[...].astype(F32)
        for k in range(N_CHIP - 1):
            g = g + got_ref[k].astype(F32)
        d, mn, vn = _adam(w_ref[...], g, m_ref[...], v_ref[...])
        g_out[...] = g
        d_out[...] = d
        m_out[...] = mn
        v_out[...] = vn

    blk = pl.BlockSpec((tr, cols), lambda i, c: (i, 0))
    return pl.pallas_call(
        body, name=name, out_shape=[_sds((rows, cols), F32)] * 4,
        grid_spec=pltpu.PrefetchScalarGridSpec(
            num_scalar_prefetch=1, grid=(rows // tr,),
            in_specs=[pl.BlockSpec((None, tr, cols), lambda i, c: (c[0], i, 0)),
                      pl.BlockSpec((N_CHIP - 1, tr, cols), lambda i, c: (0, i, 0)), blk, blk, blk],
            out_specs=[blk] * 4),
        compiler_params=_params("parallel"),
    )(chip, h, got, w, m, v)


def owned_sum(chip, h, got, name):
    _, rows, cols = h.shape
    tr, tc = _tile2(rows, cols, 16)

    def body(chip_ref, h_ref, got_ref, g_out):
        g = h_ref[...].astype(F32)
        for k in range(N_CHIP - 1):
            g = g + got_ref[k].astype(F32)
        g_out[...] = g

    return pl.pallas_call(
        body, name=name, out_shape=_sds((rows, cols), F32),
        grid_spec=pltpu.PrefetchScalarGridSpec(
            num_scalar_prefetch=1, grid=(rows // tr, cols // tc),
            in_specs=[pl.BlockSpec((None, tr, tc), lambda i, j, c: (c[0], i, j)),
                      pl.BlockSpec((N_CHIP - 1, tr, tc), lambda i, j, c: (0, i, j))],
            out_specs=pl.BlockSpec((tr, tc), lambda i, j, c: (i, j))),
        compiler_params=_params("parallel", "parallel"),
    )(chip, h, got)


def adamw_rows(sources, where, ws, ms, vs, name):
    n_src, n_par = len(sources), len(ws)

    def body(*refs):
        srcs = refs[:n_src]
        w_refs, m_refs, v_refs = (refs[n_src + j * n_par:n_src + (j + 1) * n_par] for j in range(3))
        outs = refs[n_src + 3 * n_par:]
        for k in range(n_par):
            src, off = srcs[where[k][0]], where[k][1]
            n = w_refs[k].shape[1]
            g = src[0, :, off:off + n]
            for dev in range(1, N_DEV):
                g = g + src[dev, :, off:off + n]
            d, mn, vn = _adam(w_refs[k][...], g, m_refs[k][...], v_refs[k][...])
            for o_ref, val in zip(outs[4 * k:4 * k + 4], (g, d, mn, vn)):
                o_ref[...] = val

    flat = pl.pallas_call(
        body, name=name, out_shape=[_sds(x.shape, F32) for x in ws for _ in range(4)],
        in_specs=[VMEM_SPEC] * (n_src + 3 * n_par), out_specs=[VMEM_SPEC] * (4 * n_par),
        compiler_params=pltpu.CompilerParams(vmem_limit_bytes=VMEM_LIMIT),
    )(*sources, *ws, *ms, *vs)
    return [flat[4 * k:4 * k + 4] for k in range(n_par)]


def adamw_sum(parts, w, m, v, name):
    n_parts, rows, cols = parts.shape
    tr = _row_tile(rows, 256, 8)

    def body(p_ref, w_ref, m_ref, v_ref, g_out, d_out, m_out, v_out):
        g = p_ref[0]
        for k in range(1, n_parts):
            g = g + p_ref[k]
        d, mn, vn = _adam(w_ref[...], g, m_ref[...], v_ref[...])
        g_out[...] = g
        d_out[...] = d
        m_out[...] = mn
        v_out[...] = vn

    blk = pl.BlockSpec((tr, cols), lambda i: (i, 0))
    return pl.pallas_call(
        body, name=name, out_shape=[_sds((rows, cols), F32)] * 4, grid=(rows // tr,),
        in_specs=[pl.BlockSpec((n_parts, tr, cols), lambda i: (0, i, 0)), blk, blk, blk],
        out_specs=[blk] * 4, compiler_params=_params("parallel"),
    )(parts, w, m, v)


def adaln_cols(c_all, w, b, name):
    d, n = w.shape
    tn = _row_tile(n, 768, LANE)

    def body(c_ref, w_ref, b_ref, o_ref):
        c = c_ref[...]
        o_ref[...] = jnp.dot(c * _sigmoid(c), w_ref[...], preferred_element_type=F32, precision=HI) + b_ref[...]

    return pl.pallas_call(
        body, name=name, out_shape=_sds((N_DEV, n), F32), grid=(n // tn,),
        in_specs=[pl.BlockSpec((N_DEV, d), lambda j: (0, 0)), pl.BlockSpec((d, tn), lambda j: (0, j)),
                  pl.BlockSpec((1, tn), lambda j: (0, j))],
        out_specs=pl.BlockSpec((N_DEV, tn), lambda j: (0, j)), compiler_params=_params("parallel"),
    )(c_all, w, b)


def adaln_wgrad(c_all, dmod_cols, name):
    d = c_all.shape[1]
    n = dmod_cols.shape[1]
    tn = _row_tile(n, 768, LANE)

    def body(c_ref, g_ref, o_ref):
        c = c_ref[...]
        o_ref[...] = lax.dot_general(c * _sigmoid(c), g_ref[...], (((0,), (0,)), ((), ())),
                                     preferred_element_type=F32, precision=HI)

    return pl.pallas_call(
        body, name=name, out_shape=_sds((d, n), F32), grid=(n // tn,),
        in_specs=[pl.BlockSpec((N_DEV, d), lambda j: (0, 0)), pl.BlockSpec((N_DEV, tn), lambda j: (0, j))],
        out_specs=pl.BlockSpec((d, tn), lambda j: (0, j)), compiler_params=_params("parallel"),
    )(c_all, dmod_cols)


def _modulate(h, sh, sc):
    return (h * (1.0 + sc) + sh).astype(BF16)


def ffn_in(h, sh, sc, w3, name):
    t, d = h.shape
    nb, _, bw = w3.shape
    half = nb // 2
    tm = _row_tile(t, 512, 16)

    def body(h_ref, sh_ref, sc_ref, wa_ref, wb_ref, a_ref, b_ref, s_ref):
        u = _modulate(h_ref[...], sh_ref[...], sc_ref[...])
        a = _dot(u, wa_ref[...])
        b = _dot(u, wb_ref[...])
        a_ref[...] = a.astype(BF16)
        b_ref[...] = b.astype(BF16)
        s_ref[...] = (a * _sigmoid(a) * b).astype(BF16)

    vec = pl.BlockSpec((1, d), lambda j, i: (0, 0))
    out = pl.BlockSpec((tm, bw), lambda j, i: (i, j))
    return pl.pallas_call(
        body, name=name, out_shape=[_sds((t, half * bw), BF16)] * 3, grid=(half, t // tm),
        in_specs=[pl.BlockSpec((tm, d), lambda j, i: (i, 0)), vec, vec,
                  pl.BlockSpec((None, d, bw), lambda j, i: (j, 0, 0), pipeline_mode=pl.Buffered(1)),
                  pl.BlockSpec((None, d, bw), lambda j, i: (j + half, 0, 0), pipeline_mode=pl.Buffered(1))],
        out_specs=[out] * 3, compiler_params=_params("parallel", "parallel"),
    )(h, sh, sc, w3, w3)


def mod_matmul(h, sh, sc, wt, bw, name):
    t, d = h.shape
    n = wt.shape[0]
    tm = _row_tile(t, 512, 16)

    def body(h_ref, sh_ref, sc_ref, w_ref, o_ref):
        o_ref[...] = _dot_nt(_modulate(h_ref[...], sh_ref[...], sc_ref[...]), w_ref[...]).astype(BF16)

    vec = pl.BlockSpec((1, d), lambda j, i: (0, 0))
    return pl.pallas_call(
        body, name=name, out_shape=_sds((t, n), BF16), grid=(n // bw, t // tm),
        in_specs=[pl.BlockSpec((tm, d), lambda j, i: (i, 0)), vec, vec, pl.BlockSpec((bw, d), lambda j, i: (j, 0))],
        out_specs=pl.BlockSpec((tm, bw), lambda j, i: (i, j)), compiler_params=_params("parallel", "parallel"),
    )(h, sh, sc, wt)


def out_ln(s, w, hin, gmod, ln_g, ln_b, coef, name):
    t, kdim = s.shape
    d = w.shape[1]
    tm = _row_tile(t, 256, 16)

    def body(s_ref, w_ref, hin_ref, gm_ref, g_ref, b_ref, f_ref, z_ref, h_ref):
        f = _dot(s_ref[...], w_ref[...])
        z = ALPHA * hin_ref[...] + (coef * gm_ref[...]) * f
        mu = jnp.mean(z, axis=-1, keepdims=True)
        zc = z - mu
        var = jnp.mean(zc * zc, axis=-1, keepdims=True)
        f_ref[...] = f.astype(BF16)
        z_ref[...] = z
        h_ref[...] = zc * lax.rsqrt(var + LN_EPS) * g_ref[...] + b_ref[...]

    vec = pl.BlockSpec((1, d), lambda i: (0, 0))
    row = pl.BlockSpec((tm, d), lambda i: (i, 0))
    return pl.pallas_call(
        body, name=name, out_shape=[_sds((t, d), BF16), _sds((t, d), F32), _sds((t, d), F32)],
        grid=(t // tm,),
        in_specs=[pl.BlockSpec((tm, kdim), lambda i: (i, 0)),
                  pl.BlockSpec((kdim, d), lambda i: (0, 0), pipeline_mode=pl.Buffered(1)), row, vec, vec, vec],
        out_specs=[row, row, row], compiler_params=_params("parallel"),
    )(s, w, hin, gmod, ln_g, ln_b)


def ln_bwd(dh, z, f, ln_g, gmod, coef, name, target=None):
    t, d = z.shape
    tm = _row_tile(t, 256, 16)
    head = target is not None

    def body(*refs):
        if head:
            dh_ref, tg_ref, z_ref, f_ref, g_ref, gm_ref, dz_ref, df_ref, dg_ref, db_ref, dgm_ref, loss_ref = refs
        else:
            dh_ref, z_ref, f_ref, g_ref, gm_ref, dz_ref, df_ref, dg_ref, db_ref, dgm_ref = refs
        i = pl.program_id(0)

        @pl.when(i == 0)
        def _():
            dg_ref[...] = jnp.zeros_like(dg_ref)
            db_ref[...] = jnp.zeros_like(db_ref)
            dgm_ref[...] = jnp.zeros_like(dgm_ref)
            if head:
                loss_ref[...] = jnp.zeros_like(loss_ref)

        dh = dh_ref[...]
        if head:
            err = dh - tg_ref[...]
            loss_ref[...] += 0.5 * jnp.sum(jnp.mean(err * err, axis=-1, keepdims=True))
            dh = err / d
        zv = z_ref[...]
        mu = jnp.mean(zv, axis=-1, keepdims=True)
        zc = zv - mu
        rstd = lax.rsqrt(jnp.mean(zc * zc, axis=-1, keepdims=True) + LN_EPS)
        xhat = zc * rstd
        dxh = dh * g_ref[...]
        dz = rstd * (dxh - jnp.mean(dxh, axis=-1, keepdims=True)
                     - xhat * jnp.mean(dxh * xhat, axis=-1, keepdims=True))
        dz_ref[...] = dz
        df_ref[...] = ((coef * gm_ref[...]) * dz).astype(BF16)
        dg_ref[...] += _colsum(dh * xhat)
        db_ref[...] += _colsum(dh)
        dgm_ref[...] += _colsum(coef * f_ref[...].astype(F32) * dz)

    vec = pl.BlockSpec((1, d), lambda i: (0, 0))
    row = pl.BlockSpec((tm, d), lambda i: (i, 0))
    ins = [dh] + ([target] if head else []) + [z, f, ln_g, gmod]
    in_specs = [row] + ([row] if head else []) + [row, row, vec, vec]
    out_shape = [_sds((t, d), F32), _sds((t, d), BF16)] + [_sds((1, d), F32)] * 3
    out_specs = [row, row, vec, vec, vec]
    if head:
        out_shape.append(_sds((1, LANE), F32))
        out_specs.append(pl.BlockSpec((1, LANE), lambda i: (0, 0)))
    return pl.pallas_call(
        body, name=name, out_shape=out_shape, grid=(t // tm,), in_specs=in_specs, out_specs=out_specs,
        compiler_params=_params("arbitrary"),
    )(*ins)


def ffn_bwd_act(df, w, a, b, name):
    t, d = df.shape
    fdim = w.shape[0]
    bw = fdim // (N_DEV // 2)
    tm = _row_tile(t, 512, 16)

    def body(df_ref, w_ref, a_ref, b_ref, o_ref):
        ds = _dot_nt(df_ref[...], w_ref[...])
        av = a_ref[...].astype(F32)
        sg = _sigmoid(av)
        o_ref[0] = (ds * b_ref[...].astype(F32) * (sg * (1.0 + av * (1.0 - sg)))).astype(BF16)
        o_ref[1] = (ds * (av * sg)).astype(BF16)

    act = pl.BlockSpec((tm, bw), lambda j, i: (i, j))
    return pl.pallas_call(
        body, name=name, out_shape=_sds((2, t, fdim), BF16), grid=(fdim // bw, t // tm),
        in_specs=[pl.BlockSpec((tm, d), lambda j, i: (i, 0)), pl.BlockSpec((bw, d), lambda j, i: (j, 0)), act, act],
        out_specs=pl.BlockSpec((2, tm, bw), lambda j, i: (0, i, j)),
        compiler_params=_params("parallel", "parallel"),
    )(df, w, a, b)


def matmul_tn(name, a, a_block, a_map, b, b_block, b_map, out_shape, o_block, o_map, n_out, mod=None,
              mod_b=False):
    tk = [s for s in a_block if s is not None][0]
    nk = a.shape[-2] // tk
    m, nn = [s for s in o_block if s is not None]

    def body(*refs):
        if mod is None:
            a_ref, b_ref, o_ref, acc = refs
        else:
            a_ref, sh_ref, sc_ref, b_ref, o_ref, acc = refs
        k = pl.program_id(1)

        @pl.when(k == 0)
        def _():
            acc[...] = jnp.zeros_like(acc)

        av, bv = a_ref[...], b_ref[...]
        if mod is not None and mod_b:
            bv = _modulate(bv, sh_ref[...], sc_ref[...])
        elif mod is not None:
            av = _modulate(av, sh_ref[...], sc_ref[...])
        acc[...] += _dot_tn(av, bv)

        @pl.when(k == nk - 1)
        def _():
            o_ref[...] = acc[...].astype(o_ref.dtype)

    ins = [a] + (list(mod) if mod is not None else []) + [b]
    in_specs = [pl.BlockSpec(a_block, a_map)]
    if mod is not None:
        vec = pl.BlockSpec((1, mod[0].shape[1]), lambda n, k: (0, 0))
        in_specs += [vec, vec]
    in_specs.append(pl.BlockSpec(b_block, b_map))
    return pl.pallas_call(
        body, name=name, out_shape=out_shape, grid=(n_out, nk), in_specs=in_specs,
        out_specs=pl.BlockSpec(o_block, o_map), scratch_shapes=[pltpu.VMEM((m, nn), F32)],
        compiler_params=_params("parallel", "arbitrary"),
    )(*ins)


def matmul_nt_blocks(name, dy, dy_block, dy_map, w3, t, resid=None):
    tm, bw = [s for s in dy_block if s is not None]
    rows = w3.ndim == 2
    if rows:
        nk, n = w3.shape[0] // bw, w3.shape[1]
    else:
        nk, n, _ = w3.shape

    def body(*refs):
        if resid is None:
            dy_ref, w_ref, o_ref, acc = refs
        else:
            dy_ref, w_ref, dz_ref, hin_ref, sc_ref, o_ref, dsc_ref, dsh_ref, acc = refs
        i, k = pl.program_id(0), pl.program_id(1)

        @pl.when(k == 0)
        def _():
            acc[...] = jnp.zeros_like(acc)

        if resid is not None:
            @pl.when((k == 0) & (i == 0))
            def _():
                dsc_ref[...] = jnp.zeros_like(dsc_ref)
                dsh_ref[...] = jnp.zeros_like(dsh_ref)

        acc[...] += _dot(dy_ref[...], w_ref[...]) if rows else _dot_nt(dy_ref[...], w_ref[...])

        @pl.when(k == nk - 1)
        def _():
            du = acc[...]
            if resid is None:
                o_ref[...] = du.astype(o_ref.dtype)
            else:
                o_ref[...] = ALPHA * dz_ref[...] + du * (1.0 + sc_ref[...])
                dsc_ref[...] += _colsum(du * hin_ref[...])
                dsh_ref[...] += _colsum(du)

    row = pl.BlockSpec((tm, n), lambda i, k: (i, 0))
    vec = pl.BlockSpec((1, n), lambda i, k: (0, 0))
    w_spec = pl.BlockSpec((bw, n), lambda i, k: (k, 0)) if rows else pl.BlockSpec((None, n, bw), lambda i, k: (k, 0, 0))
    in_specs = [pl.BlockSpec(dy_block, dy_map), w_spec]
    ins = [dy, w3]
    if resid is None:
        out_shape, out_specs = _sds((t, n), BF16), row
    else:
        ins += list(resid)
        once = pl.BlockSpec((tm, n), lambda i, k: (i, 0), pipeline_mode=pl.Buffered(1))
        in_specs += [once, once, vec]
        out_shape = [_sds((t, n), F32), _sds((1, n), F32), _sds((1, n), F32)]
        out_specs = [row, vec, vec]
    return pl.pallas_call(
        body, name=name, out_shape=out_shape, grid=(t // tm, nk), in_specs=in_specs, out_specs=out_specs,
        scratch_shapes=[pltpu.VMEM((tm, n), F32)], compiler_params=_params("arbitrary", "arbitrary"),
    )(*ins)


REL_W = KW + QB


def bias_table(rel_bias):
    nh, n_rel = rel_bias.shape
    lo = KW - QB - REL_CLIP
    hi = KW - lo - n_rel
    assert n_rel == REL_CLIP + CHUNK and lo >= 0 and hi >= 0
    first, last = rel_bias[:, :1], rel_bias[:, -1:]
    row = jnp.concatenate([jnp.broadcast_to(first, (nh, lo)), rel_bias, jnp.broadcast_to(last, (nh, hi)),
                           jnp.broadcast_to(first, (nh, QB))], axis=1)
    table = jnp.tile(row, (1, QB))[:, :QB * (REL_W - 1)].reshape(nh, QB, REL_W - 1)[:, :, :KW]
    q = np.arange(QB)[:, None] // CHUNK
    k = np.arange(KW)[None, :] // CHUNK
    band = (k >= q) & (k <= q + A_PAST_CHUNKS)
    return jnp.where(band[None], table, NEG)


def bias_grad_skew(dbias):
    nh = dbias.shape[0]
    flat = jnp.pad(dbias, ((0, 0), (0, 0), (0, REL_W - 1 - KW))).reshape(nh, QB * (REL_W - 1))
    return jnp.pad(flat, ((0, 0), (0, QB))).reshape(nh, QB, REL_W)


def bias_clip_map(n_rel):
    m = np.arange(REL_W)
    dist = np.where(m < KW, m, m - REL_W) - (KW - QB)
    idx = np.clip(dist, -REL_CLIP, CHUNK - 1) + REL_CLIP
    return (idx[:, None] == np.arange(n_rel)[None, :]).astype(np.float32)


PAIR = 2


def _pair_specs(col, rows_of):
    return [pl.BlockSpec((QB, LANE), functools.partial(lambda r, h, i: (rows_of(r, i), col // LANE + h), r))
            for r in range(3)]


def _earlier(r, i):
    return jnp.maximum(i - 2 + r, 0)


def _head_lanes(hh, dh):
    lane = lax.broadcasted_iota(jnp.int32, (1, LANE), 1)
    return (lane < dh) if hh == 0 else (lane >= dh)


def _only(x, lanes):
    return jnp.where(lanes, x, jnp.zeros_like(x))


def _scores(q, ks, bias, i, scale):
    s = jnp.concatenate([_dot_nt(q, kk) for kk in ks], axis=1) * scale + bias
    col = lax.broadcasted_iota(jnp.int32, s.shape, 1)
    return jnp.where(col >= (2 - i) * QB, s, NEG)


def attn_fwd(p, cols, bias, dh, name):
    t = p.shape[0]
    nh = bias.shape[0]
    scale = dh ** -0.5

    def body(q_ref, k0, k1, k2, v0, v1, v2, b_ref, o_ref, lse_ref):
        i = pl.program_id(1)
        q = q_ref[...]
        outs = []
        for hh in range(PAIR):
            lanes = _head_lanes(hh, dh)
            s = _scores(q, [_only(kk[...], lanes) for kk in (k0, k1, k2)], b_ref[hh], i, scale)
            m = jnp.max(s, axis=-1, keepdims=True)
            e = jnp.exp(s - m)
            l = jnp.sum(e, axis=-1, keepdims=True)
            eb = e.astype(BF16)
            o = sum(_dot(eb[:, r * QB:(r + 1) * QB], vv[...]) for r, vv in enumerate((v0, v1, v2)))
            outs.append(o / l)
            lse_ref[hh] = m + jnp.log(l)
        o_ref[...] = jnp.where(_head_lanes(0, dh), outs[0], outs[1]).astype(BF16)

    st = pl.BlockSpec((PAIR, QB, 1), lambda h, i: (h, i, 0))
    return pl.pallas_call(
        body, name=name, out_shape=[_sds((t, nh * dh), BF16), _sds((nh, t, 1), F32)], grid=(nh // PAIR, t // QB),
        in_specs=[pl.BlockSpec((QB, LANE), lambda h, i: (i, cols["qa"] // LANE + h))]
        + _pair_specs(cols["ka"], _earlier) + _pair_specs(cols["va"], _earlier)
        + [pl.BlockSpec((PAIR, QB, KW), lambda h, i: (h, 0, 0))],
        out_specs=[pl.BlockSpec((QB, LANE), lambda h, i: (i, h)), st],
        compiler_params=_params("parallel", "parallel"),
    )(p, p, p, p, p, p, p, bias)


def attn_bwd(p, cols, bias, lse, dy, dh, name):
    t = p.shape[0]
    nh = bias.shape[0]
    nb = t // QB
    scale = dh ** -0.5

    def body(q_ref, k0, k1, k2, v0, v1, v2, b_ref, lse_ref, dy_ref, dq_ref, dk_ref, dv_ref, db_ref, dk_acc, dv_acc):
        i = pl.program_id(1)

        @pl.when(i == 0)
        def _():
            db_ref[...] = jnp.zeros_like(db_ref)
            dk_acc[...] = jnp.zeros_like(dk_acc)
            dv_acc[...] = jnp.zeros_like(dv_acc)

        q, dyv = q_ref[...], dy_ref[...]
        ks = [k0[...], k1[...], k2[...]]
        dqs, dks, dvs = [], [], []
        for hh in range(PAIR):
            lanes = _head_lanes(hh, dh)
            s = _scores(q, [_only(kk, lanes) for kk in ks], b_ref[hh], i, scale)
            prob = jnp.exp(s - lse_ref[hh])
            dprob = jnp.concatenate([_dot_nt(dyv, _only(vv[...], lanes)) for vv in (v0, v1, v2)], axis=1)
            delta = jnp.sum(prob * dprob, axis=-1, keepdims=True)
            ds = prob * (dprob - delta)
            dsb, pb = ds.astype(BF16), prob.astype(BF16)
            dqs.append(sum(_dot(dsb[:, r * QB:(r + 1) * QB], kk) for r, kk in enumerate(ks)))
            dks.append([_dot_tn(dsb[:, r * QB:(r + 1) * QB], q) for r in range(3)])
            dvs.append([_dot_tn(pb[:, r * QB:(r + 1) * QB], dyv) for r in range(3)])
            db_ref[hh] += ds
        first = _head_lanes(0, dh)
        dq_ref[...] = (jnp.where(first, dqs[0], dqs[1]) * scale).astype(BF16)
        for r in range(3):
            rows = pl.ds(pl.multiple_of(_earlier(r, i) * QB, QB), QB)
            dk_acc[rows, :] += jnp.where(first, dks[0][r], dks[1][r])
            dv_acc[rows, :] += jnp.where(first, dvs[0][r], dvs[1][r])

        @pl.when(i == nb - 1)
        def _():
            dk_ref[...] = (dk_acc[...] * scale).astype(BF16)
            dv_ref[...] = dv_acc[...].astype(BF16)

    st = pl.BlockSpec((PAIR, QB, 1), lambda h, i: (h, i, 0))
    tab = pl.BlockSpec((PAIR, QB, KW), lambda h, i: (h, 0, 0))
    own = pl.BlockSpec((QB, LANE), lambda h, i: (i, h))
    whole = pl.BlockSpec((t, LANE), lambda h, i: (0, h))
    return pl.pallas_call(
        body, name=name,
        out_shape=[_sds((t, nh * dh), BF16)] * 3 + [_sds((nh, QB, KW), F32)],
        grid=(nh // PAIR, nb),
        in_specs=[pl.BlockSpec((QB, LANE), lambda h, i: (i, cols["qa"] // LANE + h))]
        + _pair_specs(cols["ka"], _earlier) + _pair_specs(cols["va"], _earlier) + [tab, st, own],
        out_specs=[own, whole, whole, tab],
        scratch_shapes=[pltpu.VMEM((t, LANE), F32), pltpu.VMEM((t, LANE), F32)],
        compiler_params=_params("parallel", "arbitrary"),
    )(p, p, p, p, p, p, p, bias, lse, dy)


def _tri(strict):
    r = lax.broadcasted_iota(jnp.int32, (CHUNK, CHUNK), 0)
    c = lax.broadcasted_iota(jnp.int32, (CHUNK, CHUNK), 1)
    return jnp.where((c < r) if strict else (c <= r), 1.0, 0.0).astype(F32)


def _gate(lr, wa, ba):
    y = _dot(lr, wa) + ba
    return (jnp.minimum(y, 0.0) - jnp.log(1.0 + jnp.exp(-jnp.abs(y)))) / GATE_TAU, y


def _decays(la):
    cum = jnp.dot(_tri(False), la, preferred_element_type=F32, precision=HI)
    last = cum[CHUNK - 1:CHUNK, :]
    return jnp.exp(last - cum), jnp.exp(last)


def _gla_specs(cols, hk, hv, order):
    def at(start, width):
        return pl.BlockSpec((GB, width), lambda h, i: (order(i), start // width + h))
    return [at(cols["qb"], hk), at(cols["kb"], hk), at(cols["vb"], hv), at(cols["rb"], hv),
            pl.BlockSpec((GB, LANE), lambda h, i: (order(i), cols["lr"] // LANE))]


def gla_fwd(p, cols, wa, ba, gn, nh, hk, hv, name):
    t = p.shape[0]
    nc = t // CHUNK
    scale = hk ** -0.5
    per = GB // CHUNK

    def body(q_ref, k_ref, v_ref, r_ref, lr_ref, wa_ref, ba_ref, gn_ref, o_ref, y_ref, st_ref, state):
        @pl.when(pl.program_id(1) == 0)
        def _():
            state[...] = jnp.zeros_like(state)

        for c in range(per):
            rows = pl.ds(c * CHUNK, CHUNK)
            la, _ = _gate(lr_ref[rows, :], wa_ref[...], ba_ref[...])
            w, decay = _decays(la)
            kdec = (k_ref[rows, :].astype(F32) * w).astype(BF16)
            st = decay * state[...] + _dot_tn(v_ref[rows, :], kdec)
            state[...] = st
            st_ref[c] = st
            o = _dot_nt(q_ref[rows, :], st.astype(BF16)) * scale
            o_ref[rows, :] = o
            rinv = lax.rsqrt(jnp.mean(o * o, axis=-1, keepdims=True) + RMS_EPS)
            rv = r_ref[rows, :].astype(F32)
            y_ref[rows, :] = (o * rinv * gn_ref[...] * (rv * _sigmoid(rv))).astype(BF16)

    return pl.pallas_call(
        body, name=name,
        out_shape=[_sds((t, nh * hv), F32), _sds((t, nh * hv), BF16), _sds((nh, nc, hv, hk), F32)],
        grid=(nh, t // GB),
        in_specs=_gla_specs(cols, hk, hv, lambda i: i)
        + [pl.BlockSpec((LANE, hk), lambda h, i: (0, h)), pl.BlockSpec((1, hk), lambda h, i: (0, h)),
           pl.BlockSpec((1, hv), lambda h, i: (0, 0))],
        out_specs=[pl.BlockSpec((GB, hv), lambda h, i: (i, h)), pl.BlockSpec((GB, hv), lambda h, i: (i, h)),
                   pl.BlockSpec((None, per, hv, hk), lambda h, i: (h, i, 0, 0))],
        scratch_shapes=[pltpu.VMEM((hv, hk), F32)], compiler_params=_params("parallel", "arbitrary"),
    )(p, p, p, p, p, wa, ba, gn)


def gla_bwd(p, cols, wa, ba, gn, o, states, dy, nh, hk, hv, name):
    t = p.shape[0]
    nblk = t // GB
    scale = hk ** -0.5
    per = GB // CHUNK

    def rev(i):
        return nblk - 1 - i

    def body(q_ref, k_ref, v_ref, r_ref, lr_ref, wa_ref, ba_ref, gn_ref, o_ref, st_ref, sp_ref, dy_ref,
             dq_ref, dk_ref, dv_ref, dr_ref, dg_ref, dgn_ref, carry):
        h, i = pl.program_id(0), pl.program_id(1)

        @pl.when(i == 0)
        def _():
            carry[...] = jnp.zeros_like(carry)

        @pl.when((i == 0) & (h == 0))
        def _():
            dgn_ref[...] = jnp.zeros_like(dgn_ref)

        gnv = gn_ref[...]
        for c in reversed(range(per)):
            rows = pl.ds(c * CHUNK, CHUNK)
            rv = r_ref[rows, :].astype(F32)
            sg = _sigmoid(rv)
            dyv = dy_ref[rows, :].astype(F32)
            ov = o_ref[rows, :]
            rinv = lax.rsqrt(jnp.mean(ov * ov, axis=-1, keepdims=True) + RMS_EPS)
            dn = dyv * (rv * sg)
            dr_ref[rows, :] = (dyv * (ov * rinv * gnv) * (sg * (1.0 + rv * (1.0 - sg)))).astype(BF16)
            dgn_ref[...] += _colsum(dn * ov * rinv)
            dxh = dn * gnv
            do = rinv * dxh - ov * (rinv * rinv * rinv) * jnp.mean(dxh * ov, axis=-1, keepdims=True)
            dob = (do * scale).astype(BF16)
            qv, kv, vv = q_ref[rows, :], k_ref[rows, :], v_ref[rows, :]
            dq_ref[rows, :] = _dot(dob, st_ref[c].astype(BF16)).astype(BF16)
            dst = carry[...] + _dot_tn(dob, qv)
            if c > 0:
                prev = st_ref[c - 1]
            else:
                prev = jnp.where(i == nblk - 1, 0.0, sp_ref[0])
            ddecay = _colsum(dst * prev)
            la, y = _gate(lr_ref[rows, :], wa_ref[...], ba_ref[...])
            w, decay = _decays(la)
            kf = kv.astype(F32)
            kdec = (kf * w).astype(BF16)
            dstb = dst.astype(BF16)
            dkdec = _dot(vv, dstb)
            dv_ref[rows, :] = _dot_nt(kdec, dstb).astype(BF16)
            dk_ref[rows, :] = (dkdec * w).astype(BF16)
            e = dkdec * kf * w
            dla = jnp.dot(_tri(True), e, preferred_element_type=F32, precision=HI) + ddecay * decay
            dg_ref[rows, :] = dla * (1.0 / GATE_TAU) * _sigmoid(-y)
            carry[...] = decay * dst

    per_head = lambda width: pl.BlockSpec((GB, width), lambda h, i: (rev(i), h))
    return pl.pallas_call(
        body, name=name,
        out_shape=[_sds((t, nh * hk), BF16), _sds((t, nh * hk), BF16), _sds((t, nh * hv), BF16),
                   _sds((t, nh * hv), BF16), _sds((t, nh * hk), F32), _sds((1, hv), F32)],
        grid=(nh, nblk),
        in_specs=_gla_specs(cols, hk, hv, rev)
        + [pl.BlockSpec((LANE, hk), lambda h, i: (0, h)), pl.BlockSpec((1, hk), lambda h, i: (0, h)),
           pl.BlockSpec((1, hv), lambda h, i: (0, 0)), per_head(hv),
           pl.BlockSpec((None, per, hv, hk), lambda h, i: (h, rev(i), 0, 0)),
           pl.BlockSpec((None, 1, hv, hk), lambda h, i: (h, jnp.maximum(rev(i) * per - 1, 0), 0, 0)),
           per_head(hv)],
        out_specs=[per_head(hk), per_head(hk), per_head(hv), per_head(hv), per_head(hk),
                   pl.BlockSpec((1, hv), lambda h, i: (0, 0))],
        scratch_shapes=[pltpu.VMEM((hv, hk), F32)], compiler_params=_params("arbitrary", "arbitrary"),
    )(p, p, p, p, p, wa, ba, gn, o, states, states, dy)


def gate_bwd(p, lr_col, dg, wa, name):
    t, kd = dg.shape
    tm = _row_tile(t, 512, 16)

    def body(lr_ref, dg_ref, wa_ref, dlr_ref, dwa_ref, dba_ref):
        @pl.when(pl.program_id(0) == 0)
        def _():
            dwa_ref[...] = jnp.zeros_like(dwa_ref)
            dba_ref[...] = jnp.zeros_like(dba_ref)

        g = dg_ref[...]
        gb = g.astype(BF16)
        dlr_ref[...] = _dot_nt(gb, wa_ref[...]).astype(BF16)
        dwa_ref[...] += _dot_tn(lr_ref[...], gb)
        dba_ref[...] += _colsum(g)

    return pl.pallas_call(
        body, name=name, out_shape=[_sds((t, LANE), BF16), _sds((LANE, kd), F32), _sds((1, kd), F32)],
        grid=(t // tm,),
        in_specs=[pl.BlockSpec((tm, LANE), lambda i: (i, lr_col // LANE)), pl.BlockSpec((tm, kd), lambda i: (i, 0)),
                  pl.BlockSpec((LANE, kd), lambda i: (0, 0))],
        out_specs=[pl.BlockSpec((tm, LANE), lambda i: (i, 0)), pl.BlockSpec((LANE, kd), lambda i: (0, 0)),
                   pl.BlockSpec((1, kd), lambda i: (0, 0))],
        compiler_params=_params("arbitrary"),
    )(p, dg, wa)


def proj_merge(ya, yb, wa3, wb3, p, ga_col, gb_col, name):
    t, kd = ya.shape
    nb, _, bw = wa3.shape
    tm = _row_tile(t, 512, 16)

    def body(ya_ref, yb_ref, wa_ref, wb_ref, ga_ref, gb_ref, pa_ref, pb_ref, mg_ref):
        pa = _dot(ya_ref[...], wa_ref[...])
        pb = _dot(yb_ref[...], wb_ref[...])
        pa_ref[...] = pa.astype(BF16)
        pb_ref[...] = pb.astype(BF16)
        mg_ref[...] = (_sigmoid(ga_ref[...].astype(F32)) * pa + _sigmoid(gb_ref[...].astype(F32)) * pb).astype(BF16)

    act = pl.BlockSpec((tm, kd), lambda j, i: (i, 0))
    wsp = pl.BlockSpec((None, kd, bw), lambda j, i: (j, 0, 0))
    out = pl.BlockSpec((tm, bw), lambda j, i: (i, j))
    return pl.pallas_call(
        body, name=name, out_shape=[_sds((t, nb * bw), BF16)] * 3, grid=(nb, t // tm),
        in_specs=[act, act, wsp, wsp, pl.BlockSpec((tm, bw), lambda j, i: (i, ga_col // bw + j)),
                  pl.BlockSpec((tm, bw), lambda j, i: (i, gb_col // bw + j))],
        out_specs=[out] * 3, compiler_params=_params("parallel", "parallel"),
    )(ya, yb, wa3, wb3, p, p)


def merge_bwd(dm, w, p, ga_col, gb_col, pa, pb, name):
    t, d = dm.shape
    n = w.shape[0]
    tn = _row_tile(n, 512, LANE)
    tm = _row_tile(t, 256, 16)

    def body(dm_ref, w_ref, ga_ref, gb_ref, pa_ref, pb_ref, dpa_ref, dpb_ref, dga_ref, dgb_ref):
        dmg = _dot_nt(dm_ref[...], w_ref[...])
        sa = _sigmoid(ga_ref[...].astype(F32))
        sb = _sigmoid(gb_ref[...].astype(F32))
        dpa_ref[...] = (dmg * sa).astype(BF16)
        dpb_ref[...] = (dmg * sb).astype(BF16)
        dga_ref[...] = (dmg * pa_ref[...].astype(F32) * sa * (1.0 - sa)).astype(BF16)
        dgb_ref[...] = (dmg * pb_ref[...].astype(F32) * sb * (1.0 - sb)).astype(BF16)

    out = pl.BlockSpec((tm, tn), lambda j, i: (i, j))
    return pl.pallas_call(
        body, name=name, out_shape=[_sds((t, n), BF16)] * 4, grid=(n // tn, t // tm),
        in_specs=[pl.BlockSpec((tm, d), lambda j, i: (i, 0)), pl.BlockSpec((tn, d), lambda j, i: (j, 0)),
                  pl.BlockSpec((tm, tn), lambda j, i: (i, ga_col // tn + j)),
                  pl.BlockSpec((tm, tn), lambda j, i: (i, gb_col // tn + j)), out, out],
        out_specs=[out] * 4, compiler_params=_params("parallel", "parallel"),
    )(dm, w, p, p, pa, pb)


def rel_bias_grad(skew, clip_map, name):
    nh, _, jd = skew.shape
    n_rel = clip_map.shape[1]

    def body(s_ref, c_ref, o_ref):
        sums = jnp.concatenate([_colsum(s_ref[h]) for h in range(nh)], axis=0)
        o_ref[...] = jnp.dot(sums, c_ref[...], preferred_element_type=F32, precision=HI)

    return pl.pallas_call(
        body, name=name, out_shape=_sds((nh, n_rel), F32), in_specs=[VMEM_SPEC, VMEM_SPEC], out_specs=VMEM_SPEC,
        compiler_params=pltpu.CompilerParams(vmem_limit_bytes=VMEM_LIMIT),
    )(skew, clip_map)


MIX_BLOCK = 9 * LANE


def mix_layout(d, a_width, bk, bv):
    main = 3 * a_width + 2 * bk + 2 * bv
    cols = {"qa": 0, "ka": a_width, "va": 2 * a_width, "qb": 3 * a_width, "kb": 3 * a_width + bk,
            "vb": 3 * a_width + 2 * bk, "rb": 3 * a_width + 2 * bk + bv, "ga": main, "gb": main + d,
            "lr": main + 2 * d}
    total = main + 2 * d + LANE
    assert total % MIX_BLOCK == 0
    return cols, main, total


def _mix_pieces(per, main, rank, d):
    out = []
    for lo, hi in ((0, main), (main + rank, main + rank + 2 * d), (main, main + rank)):
        while lo < hi:
            cut = min(hi, (lo // per + 1) * per)
            out.append((lo, cut))
            lo = cut
    return out


def mix_weight_in(g3, main, rank):
    d = g3.shape[2]
    flat = g3.reshape(-1, d)
    return jnp.concatenate([flat[:main], flat[main + rank:], flat[main:main + rank],
                            jnp.zeros((LANE - rank, d), g3.dtype)], axis=0)


def mix_weight_grad_out(gt, main, rank, per):
    d = gt.shape[1]
    blocks = [[] for _ in range(N_DEV)]
    pos = 0
    for lo, hi in _mix_pieces(per, main, rank, d):
        blocks[lo // per].append((lo, gt[pos:pos + hi - lo]))
        pos += hi - lo
    return jnp.stack([jnp.concatenate([x for _, x in sorted(b, key=lambda e: e[0])], axis=0) for b in blocks])


def ffn_forward(h, sh, sc, g, w_in3, w_out_of, ln_g, ln_b, tag):
    a, b, s = ffn_in(h, sh, sc, w_in3, f"{tag}_in")
    w_out = w_out_of(s)
    f, z, hout = out_ln(s, w_out, h, g, ln_g, ln_b, 0.5, f"{tag}_out")
    return hout, (h, a, b, s, f, z), w_out


def ffn_backward_weights(dh, saved, sh, sc, g, w_in3, w_out, ln_g, tag, target=None):
    hin, a, b, s, f, z = saved
    t, d = hin.shape
    nb, _, bw = w_in3.shape
    half = nb // 2
    fdim = w_out.shape[0]
    res = ln_bwd(dh, z, f, ln_g, g, 0.5, f"{tag}_ln_bwd", target=target)
    dz, df, dln_g, dln_b, dg = res[:5]
    dab = ffn_bwd_act(df, w_out, a, b, f"{tag}_act_bwd")
    tk = _row_tile(t, 512, 16)
    dw_out = matmul_tn(f"{tag}_dwout", s, (tk, bw), lambda n, k: (k, n), df, (tk, d), lambda n, k: (k, 0),
                       _sds((fdim, d), BF16), (bw, d), lambda n, k: (n, 0), fdim // bw)
    dw_in = matmul_tn(f"{tag}_dwin", hin, (tk, d), lambda n, k: (k, 0), dab, (None, tk, bw),
                      lambda n, k: (n // half, k, n % half), _sds((nb, d, bw), BF16), (None, d, bw),
                      lambda n, k: (n, 0, 0), nb, mod=(sh, sc))
    grads = dict(w_in=dw_in, w_out=dw_out.reshape(N_DEV, fdim // N_DEV, d), ln_g=dln_g, ln_b=dln_b, g=dg)
    return (dab, dz), grads, (res[5] if target is not None else None)


def ffn_backward_input(carry, saved, sc, w_in3, tag, after=None):
    dab, dz = carry
    hin = saved[0]
    t, d = hin.shape
    nb, _, bw = w_in3.shape
    half = nb // 2
    fdim = half * bw
    tm = _row_tile(t, 256, 16)

    def contract(dy_ref, w_ref):
        return sum(_dot_nt(dy_ref[:, j * bw:(j + 1) * bw], w_ref[j]) for j in range(half))

    order = [] if after is None else [after]

    def first(dy_ref, w_ref, *refs):
        refs[-1][...] = contract(dy_ref, w_ref)

    def second(dy_ref, w_ref, part_ref, dz_ref, hin_ref, sc_ref, o_ref, dsc_ref, dsh_ref):
        @pl.when(pl.program_id(0) == 0)
        def _():
            dsc_ref[...] = jnp.zeros_like(dsc_ref)
            dsh_ref[...] = jnp.zeros_like(dsh_ref)

        du = part_ref[...] + contract(dy_ref, w_ref)
        o_ref[...] = ALPHA * dz_ref[...] + du * (1.0 + sc_ref[...])
        dsc_ref[...] += _colsum(du * hin_ref[...])
        dsh_ref[...] += _colsum(du)

    def specs(which):
        return [pl.BlockSpec((None, tm, fdim), lambda i: (which, i, 0)),
                pl.BlockSpec((half, d, bw), lambda i: (which, 0, 0), pipeline_mode=pl.Buffered(1))]

    row = pl.BlockSpec((tm, d), lambda i: (i, 0))
    vec = pl.BlockSpec((1, d), lambda i: (0, 0))
    part = pl.pallas_call(
        first, name=f"{tag}_du_a", out_shape=_sds((t, d), F32), grid=(t // tm,),
        in_specs=specs(0) + [ANY] * len(order), out_specs=row, compiler_params=_params("parallel"),
    )(dab, w_in3, *order)
    return pl.pallas_call(
        second, name=f"{tag}_du_b", out_shape=[_sds((t, d), F32), _sds((1, d), F32), _sds((1, d), F32)],
        grid=(t // tm,), in_specs=specs(1) + [row, row, row, vec], out_specs=[row, vec, vec],
        compiler_params=_params("arbitrary"),
    )(dab, w_in3, part, dz, hin, sc)


def _after(v, token):
    return v if token is None else v + token[:1, :1]


def local_step(x, target, mod, weights_of, grads_ready, grads_sent, rel_bias, w_alpha2, b_alpha, gla_norm_g, lns,
               bias=None):
    t, d = x.shape
    sh1, sc1, g1, sh2, sc2, g2, sh3, sc3, g3 = [mod[i:i + 1] for i in range(N_MOD)]
    ln1_g, ln1_b, ln2_g, ln2_b, ln3_g, ln3_b = lns
    n_heads_a, n_rel = rel_bias.shape
    rank, bk = w_alpha2.shape
    hv = gla_norm_g.shape[1]

    w1 = weights_of("ffn1_in", mod)
    h1, saved1, w1["out"] = ffn_forward(x, sh1, sc1, g1, w1["in"], lambda s: weights_of("ffn1_out", s)["out"],
                                        ln1_g, ln1_b, "ffn1")
    wm = weights_of("mix", h1)
    a_width = wm["proj_a"].shape[1]
    bv = wm["proj_b"].shape[1]
    nh_b = bv // hv
    hk = bk // nh_b
    cols, main, total = mix_layout(d, a_width, bk, bv)
    w_mix = mix_weight_in(wm["in_t"], main, rank)
    p = mod_matmul(h1, sh2, sc2, w_mix, MIX_BLOCK, "mix_in")
    bias = bias_table(rel_bias) if bias is None else bias
    dh = a_width // n_heads_a
    assert PAIR * dh == LANE
    ya, lse = attn_fwd(p, cols, bias, dh, "attn_fwd")
    wa_pad = jnp.zeros((LANE, bk), BF16).at[:rank].set(w_alpha2.astype(BF16))
    o_b, yb, states = gla_fwd(p, cols, wa_pad, b_alpha, gla_norm_g, nh_b, hk, hv, "gla_fwd")
    pa, pb, merged = proj_merge(ya, yb, wm["proj_a"], wm["proj_b"], p, cols["ga"], cols["gb"], "proj_merge")
    m, z2, h2 = out_ln(merged, wm["out"], h1, g2, ln2_g, ln2_b, 1.0, "mix_out")
    w3 = weights_of("ffn2", h2)
    h3, saved3, _ = ffn_forward(h2, sh3, sc3, g3, w3["in"], lambda s: w3["out"], ln3_g, ln3_b, "ffn2")

    carry3, gr3, loss = ffn_backward_weights(h3, saved3, sh3, sc3, g3, w3["in"], w3["out"], ln3_g, "ffn2",
                                             target=target)
    token = grads_ready("ffn2", dict(ffn2_in=gr3["w_in"], ffn2_out=gr3["w_out"]))
    dh2, dsc3, dsh3 = ffn_backward_input(carry3, saved3, sc3, w3["in"], "ffn2", after=token)
    token = grads_sent("ffn2", dh2)
    dz2, dm, dln2_g, dln2_b, dg2 = ln_bwd(dh2, z2, m, _after(ln2_g, token), g2, 1.0, "mix_ln_bwd")
    dpa, dpb, dga, dgb = merge_bwd(dm, wm["out"], p, cols["ga"], cols["gb"], pa, pb, "merge_bwd")
    tk = _row_tile(t, 512, 16)
    dw_mix_out = matmul_tn("mix_dwout", merged, (tk, 512), lambda n, k: (k, n), dm, (tk, d), lambda n, k: (k, 0),
                           _sds((d, d), BF16), (512, d), lambda n, k: (n, 0), d // 512)
    tm = _row_tile(t, 512, 16)
    pbw = wm["proj_a"].shape[2]
    dya = matmul_nt_blocks("proj_a_dy", dpa, (tm, pbw), lambda i, k: (i, k), wm["proj_a"], t)
    dyb = matmul_nt_blocks("proj_b_dy", dpb, (tm, pbw), lambda i, k: (i, k), wm["proj_b"], t)
    dw_pa = matmul_tn("proj_a_dw", ya, (tk, a_width), lambda n, k: (k, 0), dpa, (tk, pbw), lambda n, k: (k, n),
                      _sds((N_DEV, a_width, pbw), BF16), (None, a_width, pbw), lambda n, k: (n, 0, 0), N_DEV)
    dw_pb = matmul_tn("proj_b_dw", yb, (tk, bv), lambda n, k: (k, 0), dpb, (tk, pbw), lambda n, k: (k, n),
                      _sds((N_DEV, bv, pbw), BF16), (None, bv, pbw), lambda n, k: (n, 0, 0), N_DEV)
    dqb, dkb, dvb, drb, dgate, dgn = gla_bwd(p, cols, wa_pad, b_alpha, gla_norm_g, o_b, states, dyb,
                                             nh_b, hk, hv, "gla_bwd")
    dlr, dwa_pad, dba = gate_bwd(p, cols["lr"], dgate, wa_pad, "gate_bwd")
    dqa, dka, dva, dbias = attn_bwd(p, cols, bias, lse, dya, dh, "attn_bwd")
    d_rel = rel_bias_grad(bias_grad_skew(dbias), jnp.asarray(bias_clip_map(n_rel)), "rel_bias_grad")
    dp = jnp.concatenate([dqa, dka, dva, dqb, dkb, dvb, drb,
                          dga, dgb, dlr], axis=1)
    dw_mix_t = matmul_tn("mix_dwin", dp, (tk, MIX_BLOCK), lambda n, k: (k, n), h1, (tk, d), lambda n, k: (k, 0),
                         _sds((total, d), BF16), (MIX_BLOCK, d), lambda n, k: (n, 0), total // MIX_BLOCK,
                         mod=(sh2, sc2), mod_b=True)
    dw_mix_in = mix_weight_grad_out(dw_mix_t, main, rank, wm["in_t"].shape[1])
    token = grads_ready("mix", dict(mix_in=dw_mix_in, proj_a=dw_pa, proj_b=dw_pb,
                                    mix_out=dw_mix_out.reshape(N_DEV, d // N_DEV, d)))
    tm = _row_tile(t, 512, 16)
    dh1, dsc2, dsh2 = matmul_nt_blocks("mix_du", dp, (tm, MIX_BLOCK), lambda i, k: (i, k), w_mix, t,
                                       resid=(dz2, h1, _after(sc2, token)))
    token = grads_sent("mix", dh1)
    carry1, gr1, _ = ffn_backward_weights(dh1, saved1, sh1, sc1, g1, w1["in"], w1["out"], _after(ln1_g, token),
                                          "ffn1")
    token = grads_ready("ffn1", dict(ffn1_in=gr1["w_in"], ffn1_out=gr1["w_out"]))
    dx, dsc1, dsh1 = ffn_backward_input(carry1, saved1, sc1, w1["in"], "ffn1", after=token)

    dmod = [dsh1, dsc1, gr1["g"], dsh2, dsc2, dg2, dsh3, dsc3, gr3["g"]]
    small = dict(ln1_g=gr1["ln_g"], ln1_b=gr1["ln_b"], ln2_g=dln2_g, ln2_b=dln2_b, ln3_g=gr3["ln_g"],
                 ln3_b=gr3["ln_b"], b_alpha=dba, gla_norm_g=dgn, w_alpha2=dwa_pad[:rank], rel_bias=d_rel)
    return loss, dx, dmod, small


GROUPS = dict(ffn1=("ffn1_in", "ffn1_out"), mix=("mix_in", "proj_a", "proj_b", "mix_out"),
              ffn2=("ffn2_in", "ffn2_out"))
GATHERS = dict(ffn1_in=("ffn1_in",), ffn1_out=("ffn1_out",), mix=GROUPS["mix"], ffn2=GROUPS["ffn2"])
SMALL_REPLICATED = ("b_ada", "ln1_g", "ln1_b", "ln2_g", "ln2_b", "ln3_g", "ln3_b", "b_alpha", "gla_norm_g")
SMALL_SHARDED = ("rel_bias", "w_alpha2")
WEIGHT_ORDER = ("w_ada", "b_ada", "ffn1_w_in", "ffn1_w_out", "ln1_g", "ln1_b", "w_mix_in", "rel_bias", "w_alpha2",
                "b_alpha", "gla_norm_g", "w_proj_a", "w_proj_b", "w_mix_out", "ln2_g", "ln2_b", "ffn2_w_in",
                "ffn2_w_out", "ln3_g", "ln3_b")
BIG_NAME = dict(ffn1_in="ffn1_w_in", ffn1_out="ffn1_w_out", mix_in="w_mix_in", proj_a="w_proj_a",
                proj_b="w_proj_b", mix_out="w_mix_out", ffn2_in="ffn2_w_in", ffn2_out="ffn2_w_out")


def kernel(x, c, w_ada, b_ada, ffn1_w_in, ffn1_w_out, ln1_g, ln1_b, w_mix_in, rel_bias, w_alpha2, b_alpha, gla_norm_g, w_proj_a, w_proj_b, w_mix_out, ln2_g, ln2_b, ffn2_w_in, ffn2_w_out, ln3_g, ln3_b, loss_target, m_w_ada, m_b_ada, m_ffn1_w_in, m_ffn1_w_out, m_ln1_g, m_ln1_b, m_w_mix_in, m_rel_bias, m_w_alpha2, m_b_alpha, m_gla_norm_g, m_w_proj_a, m_w_proj_b, m_w_mix_out, m_ln2_g, m_ln2_b, m_ffn2_w_in, m_ffn2_w_out, m_ln3_g, m_ln3_b, v_w_ada, v_b_ada, v_ffn1_w_in, v_ffn1_w_out, v_ln1_g, v_ln1_b, v_w_mix_in, v_rel_bias, v_w_alpha2, v_b_alpha, v_gla_norm_g, v_w_proj_a, v_w_proj_b, v_w_mix_out, v_ln2_g, v_ln2_b, v_ffn2_w_in, v_ffn2_w_out, v_ln3_g, v_ln3_b):
    env = dict(locals())
    w = {n: env[n] for n in WEIGHT_ORDER}
    mom = {n: env["m_" + n] for n in WEIGHT_ORDER}
    var = {n: env["v_" + n] for n in WEIGHT_ORDER}
    me = _me()
    dev = _lin(me)
    core = jnp.reshape(me[2], (1,)).astype(jnp.int32)
    chip = jnp.reshape(2 * me[0] + me[1], (1,)).astype(jnp.int32)
    d = x.shape[-1]

    def shard(n):
        s = w[BIG_NAME[n]][0].astype(BF16)
        return s.T if n == "mix_in" else s

    dev_idx = jnp.reshape(dev, (1,)).astype(jnp.int32)
    started = {}

    def start(grp, after):
        shards = [shard(n) for n in GATHERS[grp]]
        lands = [place_own(dev_idx, s, f"place_own_{n}") for n, s in zip(GATHERS[grp], shards)]
        started[grp] = gather_start(shards, lands, after, f"gather_start_{grp}")
        return started[grp][-1]

    small_w = all_gather_small(jnp.concatenate([rel_bias[0], w_alpha2[0]], axis=1), "gather_small_w")
    n_rel_cols = rel_bias.shape[-1]
    rel_full = small_w[:, :, :n_rel_cols].transpose(1, 0, 2).reshape(small_w.shape[1], -1)
    wa2_full = small_w[:, :, n_rel_cols:].transpose(1, 0, 2).reshape(small_w.shape[1], -1)

    first, *rest = list(GATHERS)
    order = start(first, small_w[0, :1, :1])

    ada_cols = w_ada.shape[-1]
    bias = bias_table(rel_full)
    c_all = all_gather_small(c, "gather_c", after=(order, bias))[:, 0, :]
    b_cols = lax.dynamic_slice_in_dim(b_ada, dev * ada_cols, ada_cols, axis=1)
    mod_cols = adaln_cols(c_all, w_ada[0], b_cols, "adaln_cols")
    mod_all = all_gather_small(mod_cols, "gather_mod")
    mod = lax.dynamic_index_in_dim(mod_all, dev, axis=1, keepdims=False).reshape(N_MOD, d)

    order = mod_all[0, :1, :1]
    for grp in rest:
        order = start(grp, order)
    mod = _after(mod, order)

    def weights_of(grp, after):
        _, zones = gather_wait(started[grp], after, f"gather_wait_{grp}")
        full = dict(zip(GATHERS[grp], gather_forward(zones, f"gather_forward_{grp}")))
        if grp == "mix":
            return dict(in_t=full["mix_in"], proj_a=full["proj_a"], proj_b=full["proj_b"],
                        out=full["mix_out"].reshape(-1, d))
        return {"in" if n.endswith("_in") else "out": v if n.endswith("_in") else v.reshape(-1, d)
                for n, v in full.items()}

    pairs, exchanges = {}, {}
    last = list(GROUPS)[0]

    def to_chips(grp, grads, got):
        sums = [pair_add(core, grads[n], g, f"grad_pair_add_{n}") for n, g in zip(GROUPS[grp], got)]
        exchanges[grp] = exchange_start(sums, chip_routes, f"grad_chip_start_{grp}")
        return exchanges[grp][-1]

    def grads_ready(grp, grads):
        if grp == last:
            return to_chips(grp, grads, pair_exchange([grads[n] for n in GROUPS[grp]], f"grad_pair_exchange_{grp}"))
        pairs[grp] = exchange_start([grads[n] for n in GROUPS[grp]], pair_routes, f"grad_pair_start_{grp}")
        return pairs[grp][-1]

    def grads_sent(grp, after):
        if grp == last:
            return None
        sent, got = exchange_wait(pairs[grp], pair_routes, after, f"grad_pair_wait_{grp}")
        return to_chips(grp, dict(zip(GROUPS[grp], sent)), got)

    lns = [ln1_g, ln1_b, ln2_g, ln2_b, ln3_g, ln3_b]
    loss, dx, dmod, small = local_step(x[0], loss_target[0], mod, weights_of, grads_ready, grads_sent, rel_full,
                                       wa2_full, b_alpha, gla_norm_g, lns, bias=bias)

    out = {}

    def finish(grp, after):
        sums, recv = exchange_wait(exchanges[grp], chip_routes, after, f"grad_chip_wait_{grp}")
        for n, hsum, r in zip(GROUPS[grp], sums, recv):
            full = BIG_NAME[n]
            if n == "mix_in":
                g = owned_sum(chip, hsum, r, f"owned_sum_{n}").T
                res_n = adamw_sum(g[None], w[full][0], mom[full][0], var[full][0], f"adamw_{n}")
            else:
                res_n = adamw_owned(chip, hsum, r, w[full][0], mom[full][0], var[full][0], f"adamw_{n}")
            out[full] = [o[None] for o in res_n]
            after = res_n[0]
        return after

    order = dx
    for grp in reversed(list(GROUPS)[1:]):
        order = finish(grp, order)

    pieces = (list(dmod) + [small[n].reshape(1, -1) for n in SMALL_REPLICATED[1:] + SMALL_SHARDED] + [loss])
    parts = all_gather_rows(pieces, order, "gather_small_grads")
    loss = jnp.sum(parts[:, 0, parts.shape[2] - loss.shape[1]])
    n_mod = N_MOD * d
    dmod_all = parts[:, 0, :n_mod]
    g_w_ada = adaln_wgrad(c_all, lax.dynamic_slice_in_dim(dmod_all, dev * ada_cols, ada_cols, axis=1), "adaln_wgrad")
    out["w_ada"] = [o[None] for o in adamw_sum(g_w_ada[None], w_ada[0], m_w_ada[0], v_w_ada[0], "adamw_w_ada")]

    sources, where, off = [parts], [], 0
    for n in SMALL_REPLICATED:
        where.append((0, off))
        off += w[n].size
    for n in SMALL_SHARDED:
        rows, cols_local = w[n].shape[1], w[n].shape[2]
        full_part = parts[:, 0, off:off + rows * cols_local * N_DEV].reshape(N_DEV, rows, cols_local * N_DEV)
        mine = lax.dynamic_slice_in_dim(full_part, dev * cols_local, cols_local, axis=2)
        where.append((len(sources), 0))
        sources.append(mine.reshape(N_DEV, 1, rows * cols_local))
        off += rows * cols_local * N_DEV
    names = SMALL_REPLICATED + SMALL_SHARDED
    res = adamw_rows(sources, where, *[[src[n].reshape(1, -1) for n in names] for src in (w, mom, var)],
                     "adamw_small")
    for n, res_n in zip(names, res):
        out[n] = [r.reshape(w[n].shape) for r in res_n]

    finish(last, res[0][0])

    flat = [loss, dx[None]]
    for k in range(4):
        flat += [out[n][k] for n in WEIGHT_ORDER]
    return tuple(flat)
```

```python
import functools

import numpy as np
import jax
import jax.numpy as jnp
from jax import lax
from jax.experimental import pallas as pl
from jax.experimental.pallas import tpu as pltpu

F32 = jnp.float32
BF16 = jnp.bfloat16
MESH = pl.DeviceIdType.MESH
N_DEV = 8
N_CHIP = 4

CHUNK = 64
A_PAST_CHUNKS = 8
REL_CLIP = 256
GATE_TAU = 16.0
N_MOD = 9
DEPTH = 1
ALPHA = (2.0 * DEPTH) ** 0.25
LN_EPS = 1e-5
RMS_EPS = 1e-6
ADAM_LR = 0.001
ADAM_B1 = 0.9
ADAM_B2 = 0.999
ADAM_EPS = 1e-08
ADAM_WD = 0.01
ADAM_STEP = 10

LANE = 128
VMEM_LIMIT = 56 * 2 ** 20
QB = 4 * CHUNK
KW = 3 * QB
GB = 8 * CHUNK
NEG = -1e30
HI = lax.Precision.HIGHEST

ANY = pl.BlockSpec(memory_space=pl.ANY)
VMEM_SPEC = pl.BlockSpec(memory_space=pltpu.VMEM)


def _params(*sem):
    return pltpu.CompilerParams(dimension_semantics=sem, vmem_limit_bytes=VMEM_LIMIT)


def _sds(shape, dtype):
    return jax.ShapeDtypeStruct(shape, dtype)


def _dot(a, b):
    return jnp.dot(a, b, preferred_element_type=F32)


def _dot_nt(a, b):
    return lax.dot_general(a, b, (((1,), (1,)), ((), ())), preferred_element_type=F32)


def _dot_tn(a, b):
    return lax.dot_general(a, b, (((0,), (0,)), ((), ())), preferred_element_type=F32)


def _sigmoid(x):
    return 0.5 * jnp.tanh(0.5 * x) + 0.5


def _colsum(x):
    return jnp.sum(x, axis=0, keepdims=True)


def _row_tile(rows, cap, mult):
    for t in range(min(rows, cap), 0, -1):
        if rows % t == 0 and t % mult == 0:
            return t
    return rows


def _me():
    return lax.axis_index("x"), lax.axis_index("y"), lax.axis_index("c")


def _flip(me, k):
    return tuple((1 - p) if (k >> s) & 1 else p for p, s in zip(me, (2, 1, 0)))


def _lin(p):
    return 4 * p[0] + 2 * p[1] + p[2]


def _gather_direct(x_ref, out_ref, send_sems, recv_sems, local_sem):
    me = _me()
    mine = pltpu.make_async_copy(x_ref, out_ref.at[_lin(me)], local_sem)
    mine.start()
    sends = []
    for k in range(1, N_DEV):
        cp = pltpu.make_async_remote_copy(
            src_ref=x_ref, dst_ref=out_ref.at[_lin(me)], send_sem=send_sems.at[k - 1],
            recv_sem=recv_sems.at[k - 1], device_id=_flip(me, k), device_id_type=MESH)
        cp.start()
        sends.append(cp)
    for k in range(1, N_DEV):
        peer = _flip(me, k)
        pltpu.make_async_remote_copy(
            src_ref=x_ref, dst_ref=out_ref.at[_lin(peer)], send_sem=send_sems.at[k - 1],
            recv_sem=recv_sems.at[k - 1], device_id=peer, device_id_type=MESH).wait_recv()
    for cp in sends:
        cp.wait_send()
    mine.wait()


GATHER_SEMS = [pltpu.SemaphoreType.DMA((N_DEV - 1,)), pltpu.SemaphoreType.DMA((N_DEV - 1,)), pltpu.SemaphoreType.DMA]


def all_gather_small(x, name, after=()):
    r, n = x.shape

    def body(x_ref, *refs):
        _gather_direct(x_ref, *refs[len(after):])

    return pl.pallas_call(
        body, name=name, out_shape=_sds((N_DEV, r, n), x.dtype),
        in_specs=[VMEM_SPEC] + [ANY] * len(after), out_specs=VMEM_SPEC, scratch_shapes=GATHER_SEMS,
    )(x, *after)


def all_gather_rows(pieces, after, name):
    sizes = [x.shape[1] for x in pieces]
    total = sum(sizes)
    assert all(n % LANE == 0 for n in sizes)

    def body(*refs):
        ins = refs[:len(pieces)]
        out_ref, row, send_sems, recv_sems, local_sem = refs[len(pieces) + 1:]
        off = 0
        for x_ref, n in zip(ins, sizes):
            row[:, off:off + n] = x_ref[...]
            off += n
        _gather_direct(row, out_ref, send_sems, recv_sems, local_sem)

    return pl.pallas_call(
        body, name=name, out_shape=_sds((N_DEV, 1, total), F32),
        in_specs=[VMEM_SPEC] * len(pieces) + [ANY], out_specs=VMEM_SPEC,
        scratch_shapes=[pltpu.VMEM((1, total), F32)] + GATHER_SEMS,
    )(*pieces, after)


HBM_SPEC = pl.BlockSpec(memory_space=pltpu.HBM)
SEM_SPEC = pl.BlockSpec(memory_space=pltpu.SEMAPHORE)
EFFECT = pltpu.SideEffectType.DATAFLOW_SIDE_EFFECTING
FIRST = N_CHIP


def _hbm(v):
    return pltpu.with_memory_space_constraint(v, pltpu.HBM)


def _other_chips(mx, my):
    return [(1 - mx, my), (mx, 1 - my), (1 - mx, 1 - my)]


def place_own(dev, shard, name):
    rows, cols = shard.shape
    tr, tc = _tile2(rows, cols, 16)

    def body(dev_ref, s_ref, land_ref, o_ref):
        o_ref[...] = s_ref[...]

    land = lax.empty((N_DEV, rows, cols), shard.dtype)
    return pl.pallas_call(
        body, name=name, out_shape=_sds(land.shape, land.dtype),
        grid_spec=pltpu.PrefetchScalarGridSpec(
            num_scalar_prefetch=1, grid=(rows // tr, cols // tc),
            in_specs=[pl.BlockSpec((tr, tc), lambda i, j, d: (i, j)), ANY],
            out_specs=pl.BlockSpec((None, tr, tc), lambda i, j, d: (d[0], i, j))),
        input_output_aliases={2: 0}, compiler_params=_params("parallel", "parallel"),
    )(dev, shard, land)


def gather_start(shards, lands, after, name):
    n = len(shards)

    def body(*refs):
        ins, zones = refs[:n], refs[n:2 * n]
        send_sems, recv_sems = refs[2 * n + 1], refs[2 * n + 2]
        token = refs[-1]
        me = _me()
        mx, my, mc = me
        for a in range(n):
            dst = zones[a].at[_lin(me)]
            targets = [(mx, my, 1 - mc)] + [(*chip, mc) for chip in _other_chips(mx, my)]
            for k, to in enumerate(targets):
                pltpu.make_async_remote_copy(
                    src_ref=ins[a], dst_ref=dst, send_sem=send_sems.at[a * FIRST + k],
                    recv_sem=recv_sems.at[a * FIRST + k], device_id=to, device_id_type=MESH).start()
        token[...] = jnp.zeros_like(token)

    sems = pltpu.SemaphoreType.DMA((n * FIRST,))
    out = pl.pallas_call(
        body, name=name,
        out_shape=(sems, sems, *[pltpu.HBM(s.shape, s.dtype) for s in shards],
                   *[pltpu.HBM(z.shape, z.dtype) for z in lands], _sds((8, LANE), F32)),
        in_specs=[HBM_SPEC] * (2 * n) + [ANY],
        out_specs=(SEM_SPEC, SEM_SPEC, *[HBM_SPEC] * (2 * n), VMEM_SPEC),
        input_output_aliases={a: 2 + a for a in range(2 * n)},
        compiler_params=pltpu.CompilerParams(has_side_effects=EFFECT),
    )(*[_hbm(s) for s in shards], *[_hbm(z) for z in lands], after)
    return out[0], out[1], out[2:2 + n], out[2 + n:2 + 2 * n], out[-1]


def gather_wait(started, after, name):
    send_sems, recv_sems, shards, lands, _ = started
    n = len(shards)

    def body(*refs):
        ins, zones = refs[:n], refs[n:2 * n]
        send_ref, recv_ref = refs[2 * n], refs[2 * n + 1]
        mx, my, mc = _me()
        for a in range(n):
            for k in range(FIRST):
                cp = pltpu.make_async_remote_copy(
                    src_ref=ins[a], dst_ref=zones[a].at[0], send_sem=send_ref.at[a * FIRST + k],
                    recv_sem=recv_ref.at[a * FIRST + k], device_id=(mx, my, 1 - mc), device_id_type=MESH)
                cp.wait_send()
                cp.wait_recv()

    out = pl.pallas_call(
        body, name=name,
        out_shape=(*[pltpu.HBM(s.shape, s.dtype) for s in shards], *[pltpu.HBM(z.shape, z.dtype) for z in lands]),
        in_specs=[HBM_SPEC] * (2 * n) + [SEM_SPEC, SEM_SPEC, ANY], out_specs=tuple([HBM_SPEC] * (2 * n)),
        input_output_aliases={a: a for a in range(2 * n)},
        compiler_params=pltpu.CompilerParams(has_side_effects=EFFECT),
    )(*shards, *lands, send_sems, recv_sems, after)
    return out[:n], out[n:]


def gather_forward(lands, name):
    n = len(lands)
    rel = N_CHIP - 1

    def body(*refs):
        zones, outs = refs[:n], refs[n:2 * n]
        send_sems, recv_sems = refs[2 * n:]
        mx, my, mc = _me()
        chips = _other_chips(mx, my)

        def copy(a, j, core):
            blk = _lin((*chips[j], core))
            return pltpu.make_async_remote_copy(
                src_ref=zones[a].at[blk], dst_ref=outs[a].at[blk], send_sem=send_sems.at[a * rel + j],
                recv_sem=recv_sems.at[a * rel + j], device_id=(mx, my, 1 - mc), device_id_type=MESH)

        sends = [copy(a, j, mc) for a in range(n) for j in range(rel)]
        for cp in sends:
            cp.start()
        for a in range(n):
            for j in range(rel):
                copy(a, j, 1 - mc).wait_recv()
        for cp in sends:
            cp.wait_send()

    return pl.pallas_call(
        body, name=name, out_shape=[_sds(z.shape, z.dtype) for z in lands],
        in_specs=[ANY] * n, out_specs=[ANY] * n, input_output_aliases={a: a for a in range(n)},
        scratch_shapes=[pltpu.SemaphoreType.DMA((n * rel,)), pltpu.SemaphoreType.DMA((n * rel,))],
    )(*lands)


def pair_exchange(gs, name):
    n = len(gs)

    def body(*refs):
        ins, outs = refs[:n], refs[n:2 * n]
        send_sems, recv_sems = refs[2 * n:]
        mx, my, mc = _me()
        cps = []
        for a in range(n):
            for q in range(N_CHIP):
                cp = pltpu.make_async_remote_copy(
                    src_ref=ins[a].at[2 * q + (1 - mc)], dst_ref=outs[a].at[q],
                    send_sem=send_sems.at[a * N_CHIP + q], recv_sem=recv_sems.at[a * N_CHIP + q],
                    device_id=(mx, my, 1 - mc), device_id_type=MESH)
                cp.start()
                cps.append(cp)
        for cp in cps:
            cp.wait()

    return pl.pallas_call(
        body, name=name, out_shape=[_sds((N_CHIP,) + g.shape[1:], g.dtype) for g in gs],
        in_specs=[ANY] * n, out_specs=[ANY] * n,
        scratch_shapes=[pltpu.SemaphoreType.DMA((n * N_CHIP,)), pltpu.SemaphoreType.DMA((n * N_CHIP,))],
    )(*gs)


def _tile2(rows, cols, row_mult):
    tr = _row_tile(rows, 512, row_mult)
    if tr < rows or rows * cols <= 2 ** 20:
        return tr, cols
    return rows, _row_tile(cols, 512, LANE)


def pair_add(core, g, got, name):
    _, rows, cols = g.shape
    tr, tc = _tile2(rows, cols, 16)

    def body(core_ref, g_ref, got_ref, h_ref):
        h_ref[...] = (g_ref[...].astype(F32) + got_ref[...].astype(F32)).astype(h_ref.dtype)

    blk = pl.BlockSpec((None, tr, tc), lambda q, i, j, c: (q, i, j))
    return pl.pallas_call(
        body, name=name, out_shape=_sds((N_CHIP, rows, cols), g.dtype),
        grid_spec=pltpu.PrefetchScalarGridSpec(
            num_scalar_prefetch=1, grid=(N_CHIP, rows // tr, cols // tc),
            in_specs=[pl.BlockSpec((None, tr, tc), lambda q, i, j, c: (2 * q + c[0], i, j)), blk],
            out_specs=blk),
        compiler_params=_params("parallel", "parallel", "parallel"),
    )(core, g, got)


def chip_routes(mx, my, mc):
    return [(2 * px + py, k, (px, py, mc)) for k, (px, py) in enumerate(_other_chips(mx, my))]


def pair_routes(mx, my, mc):
    return [(2 * q + (1 - mc), q, (mx, my, 1 - mc)) for q in range(N_CHIP)]


def exchange_start(hs, routes, name):
    n = len(hs)
    rel = len(routes(0, 0, 0))
    lands = [lax.empty((rel,) + h.shape[1:], h.dtype) for h in hs]

    def body(*refs):
        ins, zones = refs[:n], refs[n:2 * n]
        send_sems, recv_sems = refs[2 * n], refs[2 * n + 1]
        token = refs[-1]
        for a in range(n):
            for k, (src, slot, to) in enumerate(routes(*_me())):
                pltpu.make_async_remote_copy(
                    src_ref=ins[a].at[src], dst_ref=zones[a].at[slot], send_sem=send_sems.at[a * rel + k],
                    recv_sem=recv_sems.at[a * rel + k], device_id=to, device_id_type=MESH).start()
        token[...] = jnp.zeros_like(token)

    sems = pltpu.SemaphoreType.DMA((n * rel,))
    out = pl.pallas_call(
        body, name=name,
        out_shape=(sems, sems, *[pltpu.HBM(h.shape, h.dtype) for h in hs],
                   *[pltpu.HBM(z.shape, z.dtype) for z in lands], _sds((8, LANE), F32)),
        in_specs=[HBM_SPEC] * (2 * n), out_specs=(SEM_SPEC, SEM_SPEC, *[HBM_SPEC] * (2 * n), VMEM_SPEC),
        input_output_aliases={a: 2 + a for a in range(2 * n)},
        compiler_params=pltpu.CompilerParams(has_side_effects=EFFECT),
    )(*[_hbm(h) for h in hs], *[_hbm(z) for z in lands])
    return out[0], out[1], out[2:2 + n], out[2 + n:2 + 2 * n], out[-1]


def exchange_wait(started, routes, after, name):
    send_sems, recv_sems, hs, lands, _ = started
    n = len(hs)
    rel = len(routes(0, 0, 0))

    def body(*refs):
        ins, zones = refs[:n], refs[n:2 * n]
        send_ref, recv_ref = refs[2 * n], refs[2 * n + 1]
        for a in range(n):
            for k, (src, slot, to) in enumerate(routes(*_me())):
                cp = pltpu.make_async_remote_copy(
                    src_ref=ins[a].at[src], dst_ref=zones[a].at[slot], send_sem=send_ref.at[a * rel + k],
                    recv_sem=recv_ref.at[a * rel + k], device_id=to, device_id_type=MESH)
                cp.wait_send()
                cp.wait_recv()

    out = pl.pallas_call(
        body, name=name,
        out_shape=(*[pltpu.HBM(h.shape, h.dtype) for h in hs], *[pltpu.HBM(z.shape, z.dtype) for z in lands]),
        in_specs=[HBM_SPEC] * (2 * n) + [SEM_SPEC, SEM_SPEC, ANY], out_specs=tuple([HBM_SPEC] * (2 * n)),
        input_output_aliases={a: a for a in range(2 * n)},
        compiler_params=pltpu.CompilerParams(has_side_effects=EFFECT),
    )(*hs, *lands, send_sems, recv_sems, after)
    return out[:n], out[n:]


def _adam(w, g, m, v):
    m = ADAM_B1 * m + (1.0 - ADAM_B1) * g
    v = ADAM_B2 * v + (1.0 - ADAM_B2) * (g * g)
    m_hat = m / (1.0 - ADAM_B1 ** ADAM_STEP)
    v_hat = v / (1.0 - ADAM_B2 ** ADAM_STEP)
    delta = -ADAM_LR * (m_hat / (jnp.sqrt(v_hat) + ADAM_EPS) + ADAM_WD * w)
    return delta, m, v


def adamw_owned(chip, h, got, w, m, v, name):
    rows, cols = w.shape
    tr = _row_tile(rows, 256, 16)

    def body(chip_ref, h_ref, got_ref, w_ref, m_ref, v_ref, g_out, d_out, m_out, v_out):
        g = h_ref[...].astype(F32)
        for k in range(N_CHIP - 1):
            g = g + got_ref[k].astype(F32)
        d, mn, vn = _adam(w_ref[...], g, m_ref[...], v_ref[...])
        g_out[...] = g
        d_out[...] = d
        m_out[...] = mn
        v_out[...] = vn

    blk = pl.BlockSpec((tr, cols), lambda i, c: (i, 0))
    return pl.pallas_call(
        body, name=name, out_shape=[_sds((rows, cols), F32)] * 4,
        grid_spec=pltpu.PrefetchScalarGridSpec(
            num_scalar_prefetch=1, grid=(rows // tr,),
            in_specs=[pl.BlockSpec((None, tr, cols), lambda i, c: (c[0], i, 0)),
                      pl.BlockSpec((N_CHIP - 1, tr, cols), lambda i, c: (0, i, 0)), blk, blk, blk],
            out_specs=[blk] * 4),
        compiler_params=_params("parallel"),
    )(chip, h, got, w, m, v)


def owned_sum(chip, h, got, name):
    _, rows, cols = h.shape
    tr, tc = _tile2(rows, cols, 16)

    def body(chip_ref, h_ref, got_ref, g_out):
        g = h_ref[...].astype(F32)
        for k in range(N_CHIP - 1):
            g = g + got_ref[k].astype(F32)
        g_out[...] = g

    return pl.pallas_call(
        body, name=name, out_shape=_sds((rows, cols), F32),
        grid_spec=pltpu.PrefetchScalarGridSpec(
            num_scalar_prefetch=1, grid=(rows // tr, cols // tc),
            in_specs=[pl.BlockSpec((None, tr, tc), lambda i, j, c: (c[0], i, j)),
                      pl.BlockSpec((N_CHIP - 1, tr, tc), lambda i, j, c: (0, i, j))],
            out_specs=pl.BlockSpec((tr, tc), lambda i, j, c: (i, j))),
        compiler_params=_params("parallel", "parallel"),
    )(chip, h, got)


def adamw_rows(sources, where, ws, ms, vs, name):
    n_src, n_par = len(sources), len(ws)

    def body(*refs):
        srcs = refs[:n_src]
        w_refs, m_refs, v_refs = (refs[n_src + j * n_par:n_src + (j + 1) * n_par] for j in range(3))
        outs = refs[n_src + 3 * n_par:]
        for k in range(n_par):
            src, off = srcs[where[k][0]], where[k][1]
            n = w_refs[k].shape[1]
            g = src[0, :, off:off + n]
            for dev in range(1, N_DEV):
                g = g + src[dev, :, off:off + n]
            d, mn, vn = _adam(w_refs[k][...], g, m_refs[k][...], v_refs[k][...])
            for o_ref, val in zip(outs[4 * k:4 * k + 4], (g, d, mn, vn)):
                o_ref[...] = val

    flat = pl.pallas_call(
        body, name=name, out_shape=[_sds(x.shape, F32) for x in ws for _ in range(4)],
        in_specs=[VMEM_SPEC] * (n_src + 3 * n_par), out_specs=[VMEM_SPEC] * (4 * n_par),
        compiler_params=pltpu.CompilerParams(vmem_limit_bytes=VMEM_LIMIT),
    )(*sources, *ws, *ms, *vs)
    return [flat[4 * k:4 * k + 4] for k in range(n_par)]


def adamw_sum(parts, w, m, v, name):
    n_parts, rows, cols = parts.shape
    tr = _row_tile(rows, 256, 8)

    def body(p_ref, w_ref, m_ref, v_ref, g_out, d_out, m_out, v_out):
        g = p_ref[0]
        for k in range(1, n_parts):
            g = g + p_ref[k]
        d, mn, vn = _adam(w_ref[...], g, m_ref[...], v_ref[...])
        g_out[...] = g
        d_out[...] = d
        m_out[...] = mn
        v_out[...] = vn

    blk = pl.BlockSpec((tr, cols), lambda i: (i, 0))
    return pl.pallas_call(
        body, name=name, out_shape=[_sds((rows, cols), F32)] * 4, grid=(rows // tr,),
        in_specs=[pl.BlockSpec((n_parts, tr, cols), lambda i: (0, i, 0)), blk, blk, blk],
        out_specs=[blk] * 4, compiler_params=_params("parallel"),
    )(parts, w, m, v)


def adaln_cols(c_all, w, b, name):
    d, n = w.shape
    tn = _row_tile(n, 768, LANE)

    def body(c_ref, w_ref, b_ref, o_ref):
        c = c_ref[...]
        o_ref[...] = jnp.dot(c * _sigmoid(c), w_ref[...], preferred_element_type=F32, precision=HI) + b_ref[...]

    return pl.pallas_call(
        body, name=name, out_shape=_sds((N_DEV, n), F32), grid=(n // tn,),
        in_specs=[pl.BlockSpec((N_DEV, d), lambda j: (0, 0)), pl.BlockSpec((d, tn), lambda j: (0, j)),
                  pl.BlockSpec((1, tn), lambda j: (0, j))],
        out_specs=pl.BlockSpec((N_DEV, tn), lambda j: (0, j)), compiler_params=_params("parallel"),
    )(c_all, w, b)


def adaln_wgrad(c_all, dmod_cols, name):
    d = c_all.shape[1]
    n = dmod_cols.shape[1]
    tn = _row_tile(n, 768, LANE)

    def body(c_ref, g_ref, o_ref):
        c = c_ref[...]
        o_ref[...] = lax.dot_general(c * _sigmoid(c), g_ref[...], (((0,), (0,)), ((), ())),
                                     preferred_element_type=F32, precision=HI)

    return pl.pallas_call(
        body, name=name, out_shape=_sds((d, n), F32), grid=(n // tn,),
        in_specs=[pl.BlockSpec((N_DEV, d), lambda j: (0, 0)), pl.BlockSpec((N_DEV, tn), lambda j: (0, j))],
        out_specs=pl.BlockSpec((d, tn), lambda j: (0, j)), compiler_params=_params("parallel"),
    )(c_all, dmod_cols)


def _modulate(h, sh, sc):
    return (h * (1.0 + sc) + sh).astype(BF16)


def ffn_in(h, sh, sc, w3, name):
    t, d = h.shape
    nb, _, bw = w3.shape
    half = nb // 2
    tm = _row_tile(t, 512, 16)

    def body(h_ref, sh_ref, sc_ref, wa_ref, wb_ref, a_ref, b_ref, s_ref):
        u = _modulate(h_ref[...], sh_ref[...], sc_ref[...])
        a = _dot(u, wa_ref[...])
        b = _dot(u, wb_ref[...])
        a_ref[...] = a.astype(BF16)
        b_ref[...] = b.astype(BF16)
        s_ref[...] = (a * _sigmoid(a) * b).astype(BF16)

    vec = pl.BlockSpec((1, d), lambda j, i: (0, 0))
    out = pl.BlockSpec((tm, bw), lambda j, i: (i, j))
    return pl.pallas_call(
        body, name=name, out_shape=[_sds((t, half * bw), BF16)] * 3, grid=(half, t // tm),
        in_specs=[pl.BlockSpec((tm, d), lambda j, i: (i, 0)), vec, vec,
                  pl.BlockSpec((None, d, bw), lambda j, i: (j, 0, 0), pipeline_mode=pl.Buffered(1)),
                  pl.BlockSpec((None, d, bw), lambda j, i: (j + half, 0, 0), pipeline_mode=pl.Buffered(1))],
        out_specs=[out] * 3, compiler_params=_params("parallel", "parallel"),
    )(h, sh, sc, w3, w3)


def mod_matmul(h, sh, sc, wt, bw, name):
    t, d = h.shape
    n = wt.shape[0]
    tm = _row_tile(t, 512, 16)

    def body(h_ref, sh_ref, sc_ref, w_ref, o_ref):
        o_ref[...] = _dot_nt(_modulate(h_ref[...], sh_ref[...], sc_ref[...]), w_ref[...]).astype(BF16)

    vec = pl.BlockSpec((1, d), lambda j, i: (0, 0))
    return pl.pallas_call(
        body, name=name, out_shape=_sds((t, n), BF16), grid=(n // bw, t // tm),
        in_specs=[pl.BlockSpec((tm, d), lambda j, i: (i, 0)), vec, vec, pl.BlockSpec((bw, d), lambda j, i: (j, 0))],
        out_specs=pl.BlockSpec((tm, bw), lambda j, i: (i, j)), compiler_params=_params("parallel", "parallel"),
    )(h, sh, sc, wt)


def out_ln(s, w, hin, gmod, ln_g, ln_b, coef, name):
    t, kdim = s.shape
    d = w.shape[1]
    tm = _row_tile(t, 256, 16)

    def body(s_ref, w_ref, hin_ref, gm_ref, g_ref, b_ref, f_ref, z_ref, h_ref):
        f = _dot(s_ref[...], w_ref[...])
        z = ALPHA * hin_ref[...] + (coef * gm_ref[...]) * f
        mu = jnp.mean(z, axis=-1, keepdims=True)
        zc = z - mu
        var = jnp.mean(zc * zc, axis=-1, keepdims=True)
        f_ref[...] = f.astype(BF16)
        z_ref[...] = z
        h_ref[...] = zc * lax.rsqrt(var + LN_EPS) * g_ref[...] + b_ref[...]

    vec = pl.BlockSpec((1, d), lambda i: (0, 0))
    row = pl.BlockSpec((tm, d), lambda i: (i, 0))
    return pl.pallas_call(
        body, name=name, out_shape=[_sds((t, d), BF16), _sds((t, d), F32), _sds((t, d), F32)],
        grid=(t // tm,),
        in_specs=[pl.BlockSpec((tm, kdim), lambda i: (i, 0)),
                  pl.BlockSpec((kdim, d), lambda i: (0, 0), pipeline_mode=pl.Buffered(1)), row, vec, vec, vec],
        out_specs=[row, row, row], compiler_params=_params("parallel"),
    )(s, w, hin, gmod, ln_g, ln_b)


def ln_bwd(dh, z, f, ln_g, gmod, coef, name, target=None):
    t, d = z.shape
    tm = _row_tile(t, 256, 16)
    head = target is not None

    def body(*refs):
        if head:
            dh_ref, tg_ref, z_ref, f_ref, g_ref, gm_ref, dz_ref, df_ref, dg_ref, db_ref, dgm_ref, loss_ref = refs
        else:
            dh_ref, z_ref, f_ref, g_ref, gm_ref, dz_ref, df_ref, dg_ref, db_ref, dgm_ref = refs
        i = pl.program_id(0)

        @pl.when(i == 0)
        def _():
            dg_ref[...] = jnp.zeros_like(dg_ref)
            db_ref[...] = jnp.zeros_like(db_ref)
            dgm_ref[...] = jnp.zeros_like(dgm_ref)
            if head:
                loss_ref[...] = jnp.zeros_like(loss_ref)

        dh = dh_ref[...]
        if head:
            err = dh - tg_ref[...]
            loss_ref[...] += 0.5 * jnp.sum(jnp.mean(err * err, axis=-1, keepdims=True))
            dh = err / d
        zv = z_ref[...]
        mu = jnp.mean(zv, axis=-1, keepdims=True)
        zc = zv - mu
        rstd = lax.rsqrt(jnp.mean(zc * zc, axis=-1, keepdims=True) + LN_EPS)
        xhat = zc * rstd
        dxh = dh * g_ref[...]
        dz = rstd * (dxh - jnp.mean(dxh, axis=-1, keepdims=True)
                     - xhat * jnp.mean(dxh * xhat, axis=-1, keepdims=True))
        dz_ref[...] = dz
        df_ref[...] = ((coef * gm_ref[...]) * dz).astype(BF16)
        dg_ref[...] += _colsum(dh * xhat)
        db_ref[...] += _colsum(dh)
        dgm_ref[...] += _colsum(coef * f_ref[...].astype(F32) * dz)

    vec = pl.BlockSpec((1, d), lambda i: (0, 0))
    row = pl.BlockSpec((tm, d), lambda i: (i, 0))
    ins = [dh] + ([target] if head else []) + [z, f, ln_g, gmod]
    in_specs = [row] + ([row] if head else []) + [row, row, vec, vec]
    out_shape = [_sds((t, d), F32), _sds((t, d), BF16)] + [_sds((1, d), F32)] * 3
    out_specs = [row, row, vec, vec, vec]
    if head:
        out_shape.append(_sds((1, LANE), F32))
        out_specs.append(pl.BlockSpec((1, LANE), lambda i: (0, 0)))
    return pl.pallas_call(
        body, name=name, out_shape=out_shape, grid=(t // tm,), in_specs=in_specs, out_specs=out_specs,
        compiler_params=_params("arbitrary"),
    )(*ins)


def ffn_bwd_act(df, w, a, b, name):
    t, d = df.shape
    fdim = w.shape[0]
    bw = fdim // (N_DEV // 2)
    tm = _row_tile(t, 512, 16)

    def body(df_ref, w_ref, a_ref, b_ref, o_ref):
        ds = _dot_nt(df_ref[...], w_ref[...])
        av = a_ref[...].astype(F32)
        sg = _sigmoid(av)
        o_ref[0] = (ds * b_ref[...].astype(F32) * (sg * (1.0 + av * (1.0 - sg)))).astype(BF16)
        o_ref[1] = (ds * (av * sg)).astype(BF16)

    act = pl.BlockSpec((tm, bw), lambda j, i: (i, j))
    return pl.pallas_call(
        body, name=name, out_shape=_sds((2, t, fdim), BF16), grid=(fdim // bw, t // tm),
        in_specs=[pl.BlockSpec((tm, d), lambda j, i: (i, 0)), pl.BlockSpec((bw, d), lambda j, i: (j, 0)), act, act],
        out_specs=pl.BlockSpec((2, tm, bw), lambda j, i: (0, i, j)),
        compiler_params=_params("parallel", "parallel"),
    )(df, w, a, b)


def matmul_tn(name, a, a_block, a_map, b, b_block, b_map, out_shape, o_block, o_map, n_out, mod=None,
              mod_b=False):
    tk = [s for s in a_block if s is not None][0]
    nk = a.shape[-2] // tk
    m, nn = [s for s in o_block if s is not None]

    def body(*refs):
        if mod is None:
            a_ref, b_ref, o_ref, acc = refs
        else:
            a_ref, sh_ref, sc_ref, b_ref, o_ref, acc = refs
        k = pl.program_id(1)

        @pl.when(k == 0)
        def _():
            acc[...] = jnp.zeros_like(acc)

        av, bv = a_ref[...], b_ref[...]
        if mod is not None and mod_b:
            bv = _modulate(bv, sh_ref[...], sc_ref[...])
        elif mod is not None:
            av = _modulate(av, sh_ref[...], sc_ref[...])
        acc[...] += _dot_tn(av, bv)

        @pl.when(k == nk - 1)
        def _():
            o_ref[...] = acc[...].astype(o_ref.dtype)

    ins = [a] + (list(mod) if mod is not None else []) + [b]
    in_specs = [pl.BlockSpec(a_block, a_map)]
    if mod is not None:
        vec = pl.BlockSpec((1, mod[0].shape[1]), lambda n, k: (0, 0))
        in_specs += [vec, vec]
    in_specs.append(pl.BlockSpec(b_block, b_map))
    return pl.pallas_call(
        body, name=name, out_shape=out_shape, grid=(n_out, nk), in_specs=in_specs,
        out_specs=pl.BlockSpec(o_block, o_map), scratch_shapes=[pltpu.VMEM((m, nn), F32)],
        compiler_params=_params("parallel", "arbitrary"),
    )(*ins)


def matmul_nt_blocks(name, dy, dy_block, dy_map, w3, t, resid=None):
    tm, bw = [s for s in dy_block if s is not None]
    rows = w3.ndim == 2
    if rows:
        nk, n = w3.shape[0] // bw, w3.shape[1]
    else:
        nk, n, _ = w3.shape

    def body(*refs):
        if resid is None:
            dy_ref, w_ref, o_ref, acc = refs
        else:
            dy_ref, w_ref, dz_ref, hin_ref, sc_ref, o_ref, dsc_ref, dsh_ref, acc = refs
        i, k = pl.program_id(0), pl.program_id(1)

        @pl.when(k == 0)
        def _():
            acc[...] = jnp.zeros_like(acc)

        if resid is not None:
            @pl.when((k == 0) & (i == 0))
            def _():
                dsc_ref[...] = jnp.zeros_like(dsc_ref)
                dsh_ref[...] = jnp.zeros_like(dsh_ref)

        acc[...] += _dot(dy_ref[...], w_ref[...]) if rows else _dot_nt(dy_ref[...], w_ref[...])

        @pl.when(k == nk - 1)
        def _():
            du = acc[...]
            if resid is None:
                o_ref[...] = du.astype(o_ref.dtype)
            else:
                o_ref[...] = ALPHA * dz_ref[...] + du * (1.0 + sc_ref[...])
                dsc_ref[...] += _colsum(du * hin_ref[...])
                dsh_ref[...] += _colsum(du)

    row = pl.BlockSpec((tm, n), lambda i, k: (i, 0))
    vec = pl.BlockSpec((1, n), lambda i, k: (0, 0))
    w_spec = pl.BlockSpec((bw, n), lambda i, k: (k, 0)) if rows else pl.BlockSpec((None, n, bw), lambda i, k: (k, 0, 0))
    in_specs = [pl.BlockSpec(dy_block, dy_map), w_spec]
    ins = [dy, w3]
    if resid is None:
        out_shape, out_specs = _sds((t, n), BF16), row
    else:
        ins += list(resid)
        once = pl.BlockSpec((tm, n), lambda i, k: (i, 0), pipeline_mode=pl.Buffered(1))
        in_specs += [once, once, vec]
        out_shape = [_sds((t, n), F32), _sds((1, n), F32), _sds((1, n), F32)]
        out_specs = [row, vec, vec]
    return pl.pallas_call(
        body, name=name, out_shape=out_shape, grid=(t // tm, nk), in_specs=in_specs, out_specs=out_specs,
        scratch_shapes=[pltpu.VMEM((tm, n), F32)], compiler_params=_params("arbitrary", "arbitrary"),
    )(*ins)


REL_W = KW + QB


def bias_table(rel_bias):
    nh, n_rel = rel_bias.shape
    lo = KW - QB - REL_CLIP
    hi = KW - lo - n_rel
    assert n_rel == REL_CLIP + CHUNK and lo >= 0 and hi >= 0
    first, last = rel_bias[:, :1], rel_bias[:, -1:]
    row = jnp.concatenate([jnp.broadcast_to(first, (nh, lo)), rel_bias, jnp.broadcast_to(last, (nh, hi)),
                           jnp.broadcast_to(first, (nh, QB))], axis=1)
    table = jnp.tile(row, (1, QB))[:, :QB * (REL_W - 1)].reshape(nh, QB, REL_W - 1)[:, :, :KW]
    q = np.arange(QB)[:, None] // CHUNK
    k = np.arange(KW)[None, :] // CHUNK
    band = (k >= q) & (k <= q + A_PAST_CHUNKS)
    return jnp.where(band[None], table, NEG)


def bias_grad_skew(dbias):
    nh = dbias.shape[0]
    flat = jnp.pad(dbias, ((0, 0), (0, 0), (0, REL_W - 1 - KW))).reshape(nh, QB * (REL_W - 1))
    return jnp.pad(flat, ((0, 0), (0, QB))).reshape(nh, QB, REL_W)


def bias_clip_map(n_rel):
    m = np.arange(REL_W)
    dist = np.where(m < KW, m, m - REL_W) - (KW - QB)
    idx = np.clip(dist, -REL_CLIP, CHUNK - 1) + REL_CLIP
    return (idx[:, None] == np.arange(n_rel)[None, :]).astype(np.float32)


PAIR = 2


def _pair_specs(col, rows_of):
    return [pl.BlockSpec((QB, LANE), functools.partial(lambda r, h, i: (rows_of(r, i), col // LANE + h), r))
            for r in range(3)]


def _earlier(r, i):
    return jnp.maximum(i - 2 + r, 0)


def _head_lanes(hh, dh):
    lane = lax.broadcasted_iota(jnp.int32, (1, LANE), 1)
    return (lane < dh) if hh == 0 else (lane >= dh)


def _only(x, lanes):
    return jnp.where(lanes, x, jnp.zeros_like(x))


def _scores(q, ks, bias, i, scale):
    s = jnp.concatenate([_dot_nt(q, kk) for kk in ks], axis=1) * scale + bias
    col = lax.broadcasted_iota(jnp.int32, s.shape, 1)
    return jnp.where(col >= (2 - i) * QB, s, NEG)


def attn_fwd(p, cols, bias, dh, name):
    t = p.shape[0]
    nh = bias.shape[0]
    scale = dh ** -0.5

    def body(q_ref, k0, k1, k2, v0, v1, v2, b_ref, o_ref, lse_ref):
        i = pl.program_id(1)
        q = q_ref[...]
        outs = []
        for hh in range(PAIR):
            lanes = _head_lanes(hh, dh)
            s = _scores(q, [_only(kk[...], lanes) for kk in (k0, k1, k2)], b_ref[hh], i, scale)
            m = jnp.max(s, axis=-1, keepdims=True)
            e = jnp.exp(s - m)
            l = jnp.sum(e, axis=-1, keepdims=True)
            eb = e.astype(BF16)
            o = sum(_dot(eb[:, r * QB:(r + 1) * QB], vv[...]) for r, vv in enumerate((v0, v1, v2)))
            outs.append(o / l)
            lse_ref[hh] = m + jnp.log(l)
        o_ref[...] = jnp.where(_head_lanes(0, dh), outs[0], outs[1]).astype(BF16)

    st = pl.BlockSpec((PAIR, QB, 1), lambda h, i: (h, i, 0))
    return pl.pallas_call(
        body, name=name, out_shape=[_sds((t, nh * dh), BF16), _sds((nh, t, 1), F32)], grid=(nh // PAIR, t // QB),
        in_specs=[pl.BlockSpec((QB, LANE), lambda h, i: (i, cols["qa"] // LANE + h))]
        + _pair_specs(cols["ka"], _earlier) + _pair_specs(cols["va"], _earlier)
        + [pl.BlockSpec((PAIR, QB, KW), lambda h, i: (h, 0, 0))],
        out_specs=[pl.BlockSpec((QB, LANE), lambda h, i: (i, h)), st],
        compiler_params=_params("parallel", "parallel"),
    )(p, p, p, p, p, p, p, bias)


def attn_bwd(p, cols, bias, lse, dy, dh, name):
    t = p.shape[0]
    nh = bias.shape[0]
    nb = t // QB
    scale = dh ** -0.5

    def body(q_ref, k0, k1, k2, v0, v1, v2, b_ref, lse_ref, dy_ref, dq_ref, dk_ref, dv_ref, db_ref, dk_acc, dv_acc):
        i = pl.program_id(1)

        @pl.when(i == 0)
        def _():
            db_ref[...] = jnp.zeros_like(db_ref)
            dk_acc[...] = jnp.zeros_like(dk_acc)
            dv_acc[...] = jnp.zeros_like(dv_acc)

        q, dyv = q_ref[...], dy_ref[...]
        ks = [k0[...], k1[...], k2[...]]
        dqs, dks, dvs = [], [], []
        for hh in range(PAIR):
            lanes = _head_lanes(hh, dh)
            s = _scores(q, [_only(kk, lanes) for kk in ks], b_ref[hh], i, scale)
            prob = jnp.exp(s - lse_ref[hh])
            dprob = jnp.concatenate([_dot_nt(dyv, _only(vv[...], lanes)) for vv in (v0, v1, v2)], axis=1)
            delta = jnp.sum(prob * dprob, axis=-1, keepdims=True)
            ds = prob * (dprob - delta)
            dsb, pb = ds.astype(BF16), prob.astype(BF16)
            dqs.append(sum(_dot(dsb[:, r * QB:(r + 1) * QB], kk) for r, kk in enumerate(ks)))
            dks.append([_dot_tn(dsb[:, r * QB:(r + 1) * QB], q) for r in range(3)])
            dvs.append([_dot_tn(pb[:, r * QB:(r + 1) * QB], dyv) for r in range(3)])
            db_ref[hh] += ds
        first = _head_lanes(0, dh)
        dq_ref[...] = (jnp.where(first, dqs[0], dqs[1]) * scale).astype(BF16)
        for r in range(3):
            rows = pl.ds(pl.multiple_of(_earlier(r, i) * QB, QB), QB)
            dk_acc[rows, :] += jnp.where(first, dks[0][r], dks[1][r])
            dv_acc[rows, :] += jnp.where(first, dvs[0][r], dvs[1][r])

        @pl.when(i == nb - 1)
        def _():
            dk_ref[...] = (dk_acc[...] * scale).astype(BF16)
            dv_ref[...] = dv_acc[...].astype(BF16)

    st = pl.BlockSpec((PAIR, QB, 1), lambda h, i: (h, i, 0))
    tab = pl.BlockSpec((PAIR, QB, KW), lambda h, i: (h, 0, 0))
    own = pl.BlockSpec((QB, LANE), lambda h, i: (i, h))
    whole = pl.BlockSpec((t, LANE), lambda h, i: (0, h))
    return pl.pallas_call(
        body, name=name,
        out_shape=[_sds((t, nh * dh), BF16)] * 3 + [_sds((nh, QB, KW), F32)],
        grid=(nh // PAIR, nb),
        in_specs=[pl.BlockSpec((QB, LANE), lambda h, i: (i, cols["qa"] // LANE + h))]
        + _pair_specs(cols["ka"], _earlier) + _pair_specs(cols["va"], _earlier) + [tab, st, own],
        out_specs=[own, whole, whole, tab],
        scratch_shapes=[pltpu.VMEM((t, LANE), F32), pltpu.VMEM((t, LANE), F32)],
        compiler_params=_params("parallel", "arbitrary"),
    )(p, p, p, p, p, p, p, bias, lse, dy)


def _prefix_sums(x, strict):
    r = lax.broadcasted_iota(jnp.int32, (CHUNK, CHUNK), 0)
    c = lax.broadcasted_iota(jnp.int32, (CHUNK, CHUNK), 1)
    tri = jnp.where((c < r) if strict else (c <= r), 1.0, 0.0).astype(BF16)
    n = x.shape[1]
    hi = x.astype(BF16)
    rest = x - hi.astype(F32)
    mid = rest.astype(BF16)
    lo = (rest - mid.astype(F32)).astype(BF16)
    out = _dot(tri, jnp.concatenate([hi, mid, lo], axis=1))
    return out[:, :n] + out[:, n:2 * n] + out[:, 2 * n:]


def _gate(lr, wa, ba):
    y = _dot(lr, wa) + ba
    return (jnp.minimum(y, 0.0) - jnp.log(1.0 + jnp.exp(-jnp.abs(y)))) / GATE_TAU, y


def _decays(la):
    cum = _prefix_sums(la, strict=False)
    last = cum[CHUNK - 1:CHUNK, :]
    return jnp.exp(last - cum), jnp.exp(last)


def _gla_specs(cols, hk, hv, order):
    def at(start, width):
        return pl.BlockSpec((GB, width), lambda h, i: (order(i), start // width + h))
    return [at(cols["qb"], hk), at(cols["kb"], hk), at(cols["vb"], hv), at(cols["rb"], hv),
            pl.BlockSpec((GB, LANE), lambda h, i: (order(i), cols["lr"] // LANE))]


def gla_fwd(p, cols, wa, ba, gn, nh, hk, hv, name):
    t = p.shape[0]
    nc = t // CHUNK
    scale = hk ** -0.5
    per = GB // CHUNK

    def body(q_ref, k_ref, v_ref, r_ref, lr_ref, wa_ref, ba_ref, gn_ref, o_ref, y_ref, st_ref, state):
        @pl.when(pl.program_id(1) == 0)
        def _():
            state[...] = jnp.zeros_like(state)

        for c in range(per):
            rows = pl.ds(c * CHUNK, CHUNK)
            la, _ = _gate(lr_ref[rows, :], wa_ref[...], ba_ref[...])
            w, decay = _decays(la)
            kdec = (k_ref[rows, :].astype(F32) * w).astype(BF16)
            st = decay * state[...] + _dot_tn(v_ref[rows, :], kdec)
            state[...] = st
            st_ref[c] = st
            o = _dot_nt(q_ref[rows, :], st.astype(BF16)) * scale
            o_ref[rows, :] = o
            rinv = lax.rsqrt(jnp.mean(o * o, axis=-1, keepdims=True) + RMS_EPS)
            rv = r_ref[rows, :].astype(F32)
            y_ref[rows, :] = (o * rinv * gn_ref[...] * (rv * _sigmoid(rv))).astype(BF16)

    return pl.pallas_call(
        body, name=name,
        out_shape=[_sds((t, nh * hv), F32), _sds((t, nh * hv), BF16), _sds((nh, nc, hv, hk), F32)],
        grid=(nh, t // GB),
        in_specs=_gla_specs(cols, hk, hv, lambda i: i)
        + [pl.BlockSpec((LANE, hk), lambda h, i: (0, h)), pl.BlockSpec((1, hk), lambda h, i: (0, h)),
           pl.BlockSpec((1, hv), lambda h, i: (0, 0))],
        out_specs=[pl.BlockSpec((GB, hv), lambda h, i: (i, h)), pl.BlockSpec((GB, hv), lambda h, i: (i, h)),
                   pl.BlockSpec((None, per, hv, hk), lambda h, i: (h, i, 0, 0))],
        scratch_shapes=[pltpu.VMEM((hv, hk), F32)], compiler_params=_params("parallel", "arbitrary"),
    )(p, p, p, p, p, wa, ba, gn)


def gla_bwd(p, cols, wa, ba, gn, o, states, dy, nh, hk, hv, name):
    t = p.shape[0]
    nblk = t // GB
    scale = hk ** -0.5
    per = GB // CHUNK

    def rev(i):
        return nblk - 1 - i

    def body(q_ref, k_ref, v_ref, r_ref, lr_ref, wa_ref, ba_ref, gn_ref, o_ref, st_ref, sp_ref, dy_ref,
             dq_ref, dk_ref, dv_ref, dr_ref, dg_ref, dgn_ref, carry):
        h, i = pl.program_id(0), pl.program_id(1)

        @pl.when(i == 0)
        def _():
            carry[...] = jnp.zeros_like(carry)

        @pl.when((i == 0) & (h == 0))
        def _():
            dgn_ref[...] = jnp.zeros_like(dgn_ref)

        gnv = gn_ref[...]
        for c in reversed(range(per)):
            rows = pl.ds(c * CHUNK, CHUNK)
            rv = r_ref[rows, :].astype(F32)
            sg = _sigmoid(rv)
            dyv = dy_ref[rows, :].astype(F32)
            ov = o_ref[rows, :]
            rinv = lax.rsqrt(jnp.mean(ov * ov, axis=-1, keepdims=True) + RMS_EPS)
            dn = dyv * (rv * sg)
            dr_ref[rows, :] = (dyv * (ov * rinv * gnv) * (sg * (1.0 + rv * (1.0 - sg)))).astype(BF16)
            dgn_ref[...] += _colsum(dn * ov * rinv)
            dxh = dn * gnv
            do = rinv * dxh - ov * (rinv * rinv * rinv) * jnp.mean(dxh * ov, axis=-1, keepdims=True)
            dob = (do * scale).astype(BF16)
            qv, kv, vv = q_ref[rows, :], k_ref[rows, :], v_ref[rows, :]
            dq_ref[rows, :] = _dot(dob, st_ref[c].astype(BF16)).astype(BF16)
            dst = carry[...] + _dot_tn(dob, qv)
            if c > 0:
                prev = st_ref[c - 1]
            else:
                prev = jnp.where(i == nblk - 1, 0.0, sp_ref[0])
            ddecay = _colsum(dst * prev)
            la, y = _gate(lr_ref[rows, :], wa_ref[...], ba_ref[...])
            w, decay = _decays(la)
            kf = kv.astype(F32)
            kdec = (kf * w).astype(BF16)
            dstb = dst.astype(BF16)
            dkdec = _dot(vv, dstb)
            dv_ref[rows, :] = _dot_nt(kdec, dstb).astype(BF16)
            dk_ref[rows, :] = (dkdec * w).astype(BF16)
            e = dkdec * kf * w
            dla = _prefix_sums(e, strict=True) + ddecay * decay
            dg_ref[rows, :] = dla * (1.0 / GATE_TAU) * _sigmoid(-y)
            carry[...] = decay * dst

    per_head = lambda width: pl.BlockSpec((GB, width), lambda h, i: (rev(i), h))
    return pl.pallas_call(
        body, name=name,
        out_shape=[_sds((t, nh * hk), BF16), _sds((t, nh * hk), BF16), _sds((t, nh * hv), BF16),
                   _sds((t, nh * hv), BF16), _sds((t, nh * hk), F32), _sds((1, hv), F32)],
        grid=(nh, nblk),
        in_specs=_gla_specs(cols, hk, hv, rev)
        + [pl.BlockSpec((LANE, hk), lambda h, i: (0, h)), pl.BlockSpec((1, hk), lambda h, i: (0, h)),
           pl.BlockSpec((1, hv), lambda h, i: (0, 0)), per_head(hv),
           pl.BlockSpec((None, per, hv, hk), lambda h, i: (h, rev(i), 0, 0)),
           pl.BlockSpec((None, 1, hv, hk), lambda h, i: (h, jnp.maximum(rev(i) * per - 1, 0), 0, 0)),
           per_head(hv)],
        out_specs=[per_head(hk), per_head(hk), per_head(hv), per_head(hv), per_head(hk),
                   pl.BlockSpec((1, hv), lambda h, i: (0, 0))],
        scratch_shapes=[pltpu.VMEM((hv, hk), F32)], compiler_params=_params("arbitrary", "arbitrary"),
    )(p, p, p, p, p, wa, ba, gn, o, states, states, dy)


def gate_bwd(p, lr_col, dg, wa, name):
    t, kd = dg.shape
    tm = _row_tile(t, 512, 16)

    def body(lr_ref, dg_ref, wa_ref, dlr_ref, dwa_ref, dba_ref):
        @pl.when(pl.program_id(0) == 0)
        def _():
            dwa_ref[...] = jnp.zeros_like(dwa_ref)
            dba_ref[...] = jnp.zeros_like(dba_ref)

        g = dg_ref[...]
        gb = g.astype(BF16)
        dlr_ref[...] = _dot_nt(gb, wa_ref[...]).astype(BF16)
        dwa_ref[...] += _dot_tn(lr_ref[...], gb)
        dba_ref[...] += _colsum(g)

    return pl.pallas_call(
        body, name=name, out_shape=[_sds((t, LANE), BF16), _sds((LANE, kd), F32), _sds((1, kd), F32)],
        grid=(t // tm,),
        in_specs=[pl.BlockSpec((tm, LANE), lambda i: (i, lr_col // LANE)), pl.BlockSpec((tm, kd), lambda i: (i, 0)),
                  pl.BlockSpec((LANE, kd), lambda i: (0, 0))],
        out_specs=[pl.BlockSpec((tm, LANE), lambda i: (i, 0)), pl.BlockSpec((LANE, kd), lambda i: (0, 0)),
                   pl.BlockSpec((1, kd), lambda i: (0, 0))],
        compiler_params=_params("arbitrary"),
    )(p, dg, wa)


def proj_merge(ya, yb, wa3, wb3, p, ga_col, gb_col, name):
    t, kd = ya.shape
    nb, _, bw = wa3.shape
    tm = _row_tile(t, 512, 16)

    def body(ya_ref, yb_ref, wa_ref, wb_ref, ga_ref, gb_ref, pa_ref, pb_ref, mg_ref):
        pa = _dot(ya_ref[...], wa_ref[...])
        pb = _dot(yb_ref[...], wb_ref[...])
        pa_ref[...] = pa.astype(BF16)
        pb_ref[...] = pb.astype(BF16)
        mg_ref[...] = (_sigmoid(ga_ref[...].astype(F32)) * pa + _sigmoid(gb_ref[...].astype(F32)) * pb).astype(BF16)

    act = pl.BlockSpec((tm, kd), lambda j, i: (i, 0))
    wsp = pl.BlockSpec((None, kd, bw), lambda j, i: (j, 0, 0))
    out = pl.BlockSpec((tm, bw), lambda j, i: (i, j))
    return pl.pallas_call(
        body, name=name, out_shape=[_sds((t, nb * bw), BF16)] * 3, grid=(nb, t // tm),
        in_specs=[act, act, wsp, wsp, pl.BlockSpec((tm, bw), lambda j, i: (i, ga_col // bw + j)),
                  pl.BlockSpec((tm, bw), lambda j, i: (i, gb_col // bw + j))],
        out_specs=[out] * 3, compiler_params=_params("parallel", "parallel"),
    )(ya, yb, wa3, wb3, p, p)


def merge_bwd(dm, w, p, ga_col, gb_col, pa, pb, name):
    t, d = dm.shape
    n = w.shape[0]
    tn = _row_tile(n, 512, LANE)
    tm = _row_tile(t, 256, 16)

    def body(dm_ref, w_ref, ga_ref, gb_ref, pa_ref, pb_ref, dpa_ref, dpb_ref, dga_ref, dgb_ref):
        dmg = _dot_nt(dm_ref[...], w_ref[...])
        sa = _sigmoid(ga_ref[...].astype(F32))
        sb = _sigmoid(gb_ref[...].astype(F32))
        dpa_ref[...] = (dmg * sa).astype(BF16)
        dpb_ref[...] = (dmg * sb).astype(BF16)
        dga_ref[...] = (dmg * pa_ref[...].astype(F32) * sa * (1.0 - sa)).astype(BF16)
        dgb_ref[...] = (dmg * pb_ref[...].astype(F32) * sb * (1.0 - sb)).astype(BF16)

    out = pl.BlockSpec((tm, tn), lambda j, i: (i, j))
    return pl.pallas_call(
        body, name=name, out_shape=[_sds((t, n), BF16)] * 4, grid=(n // tn, t // tm),
        in_specs=[pl.BlockSpec((tm, d), lambda j, i: (i, 0)), pl.BlockSpec((tn, d), lambda j, i: (j, 0)),
                  pl.BlockSpec((tm, tn), lambda j, i: (i, ga_col // tn + j)),
                  pl.BlockSpec((tm, tn), lambda j, i: (i, gb_col // tn + j)), out, out],
        out_specs=[out] * 4, compiler_params=_params("parallel", "parallel"),
    )(dm, w, p, p, pa, pb)


def rel_bias_grad(skew, clip_map, name):
    nh, _, jd = skew.shape
    n_rel = clip_map.shape[1]

    def body(s_ref, c_ref, o_ref):
        sums = jnp.concatenate([_colsum(s_ref[h]) for h in range(nh)], axis=0)
        o_ref[...] = jnp.dot(sums, c_ref[...], preferred_element_type=F32, precision=HI)

    return pl.pallas_call(
        body, name=name, out_shape=_sds((nh, n_rel), F32), in_specs=[VMEM_SPEC, VMEM_SPEC], out_specs=VMEM_SPEC,
        compiler_params=pltpu.CompilerParams(vmem_limit_bytes=VMEM_LIMIT),
    )(skew, clip_map)


MIX_BLOCK = 9 * LANE


def mix_layout(d, a_width, bk, bv):
    main = 3 * a_width + 2 * bk + 2 * bv
    cols = {"qa": 0, "ka": a_width, "va": 2 * a_width, "qb": 3 * a_width, "kb": 3 * a_width + bk,
            "vb": 3 * a_width + 2 * bk, "rb": 3 * a_width + 2 * bk + bv, "ga": main, "gb": main + d,
            "lr": main + 2 * d}
    total = main + 2 * d + LANE
    assert total % MIX_BLOCK == 0
    return cols, main, total


def _mix_pieces(per, main, rank, d):
    out = []
    for lo, hi in ((0, main), (main + rank, main + rank + 2 * d), (main, main + rank)):
        while lo < hi:
            cut = min(hi, (lo // per + 1) * per)
            out.append((lo, cut))
            lo = cut
    return out


def mix_weight_in(g3, main, rank):
    d = g3.shape[2]
    flat = g3.reshape(-1, d)
    return jnp.concatenate([flat[:main], flat[main + rank:], flat[main:main + rank],
                            jnp.zeros((LANE - rank, d), g3.dtype)], axis=0)


def mix_weight_grad_out(gt, main, rank, per):
    d = gt.shape[1]
    blocks = [[] for _ in range(N_DEV)]
    pos = 0
    for lo, hi in _mix_pieces(per, main, rank, d):
        blocks[lo // per].append((lo, gt[pos:pos + hi - lo]))
        pos += hi - lo
    return jnp.stack([jnp.concatenate([x for _, x in sorted(b, key=lambda e: e[0])], axis=0) for b in blocks])


def ffn_forward(h, sh, sc, g, w_in3, w_out_of, ln_g, ln_b, tag):
    a, b, s = ffn_in(h, sh, sc, w_in3, f"{tag}_in")
    w_out = w_out_of(s)
    f, z, hout = out_ln(s, w_out, h, g, ln_g, ln_b, 0.5, f"{tag}_out")
    return hout, (h, a, b, s, f, z), w_out


def ffn_backward_weights(dh, saved, sh, sc, g, w_in3, w_out, ln_g, tag, target=None):
    hin, a, b, s, f, z = saved
    t, d = hin.shape
    nb, _, bw = w_in3.shape
    half = nb // 2
    fdim = w_out.shape[0]
    res = ln_bwd(dh, z, f, ln_g, g, 0.5, f"{tag}_ln_bwd", target=target)
    dz, df, dln_g, dln_b, dg = res[:5]
    dab = ffn_bwd_act(df, w_out, a, b, f"{tag}_act_bwd")
    tk = _row_tile(t, 512, 16)
    dw_out = matmul_tn(f"{tag}_dwout", s, (tk, bw), lambda n, k: (k, n), df, (tk, d), lambda n, k: (k, 0),
                       _sds((fdim, d), BF16), (bw, d), lambda n, k: (n, 0), fdim // bw)
    dw_in = matmul_tn(f"{tag}_dwin", hin, (tk, d), lambda n, k: (k, 0), dab, (None, tk, bw),
                      lambda n, k: (n // half, k, n % half), _sds((nb, d, bw), BF16), (None, d, bw),
                      lambda n, k: (n, 0, 0), nb, mod=(sh, sc))
    grads = dict(w_in=dw_in, w_out=dw_out.reshape(N_DEV, fdim // N_DEV, d), ln_g=dln_g, ln_b=dln_b, g=dg)
    return (dab, dz), grads, (res[5] if target is not None else None)


def ffn_backward_input(carry, saved, sc, w_in3, tag, after=None):
    dab, dz = carry
    hin = saved[0]
    t, d = hin.shape
    nb, _, bw = w_in3.shape
    half = nb // 2
    fdim = half * bw
    tm = _row_tile(t, 256, 16)

    def contract(dy_ref, w_ref):
        return sum(_dot_nt(dy_ref[:, j * bw:(j + 1) * bw], w_ref[j]) for j in range(half))

    order = [] if after is None else [after]

    def first(dy_ref, w_ref, *refs):
        refs[-1][...] = contract(dy_ref, w_ref)

    def second(dy_ref, w_ref, part_ref, dz_ref, hin_ref, sc_ref, o_ref, dsc_ref, dsh_ref):
        @pl.when(pl.program_id(0) == 0)
        def _():
            dsc_ref[...] = jnp.zeros_like(dsc_ref)
            dsh_ref[...] = jnp.zeros_like(dsh_ref)

        du = part_ref[...] + contract(dy_ref, w_ref)
        o_ref[...] = ALPHA * dz_ref[...] + du * (1.0 + sc_ref[...])
        dsc_ref[...] += _colsum(du * hin_ref[...])
        dsh_ref[...] += _colsum(du)

    def specs(which):
        return [pl.BlockSpec((None, tm, fdim), lambda i: (which, i, 0)),
                pl.BlockSpec((half, d, bw), lambda i: (which, 0, 0), pipeline_mode=pl.Buffered(1))]

    row = pl.BlockSpec((tm, d), lambda i: (i, 0))
    vec = pl.BlockSpec((1, d), lambda i: (0, 0))
    part = pl.pallas_call(
        first, name=f"{tag}_du_a", out_shape=_sds((t, d), F32), grid=(t // tm,),
        in_specs=specs(0) + [ANY] * len(order), out_specs=row, compiler_params=_params("parallel"),
    )(dab, w_in3, *order)
    return pl.pallas_call(
        second, name=f"{tag}_du_b", out_shape=[_sds((t, d), F32), _sds((1, d), F32), _sds((1, d), F32)],
        grid=(t // tm,), in_specs=specs(1) + [row, row, row, vec], out_specs=[row, vec, vec],
        compiler_params=_params("arbitrary"),
    )(dab, w_in3, part, dz, hin, sc)


def _after(v, token):
    return v if token is None else v + token[:1, :1]


def local_step(x, target, mod, weights_of, grads_ready, grads_sent, rel_bias, w_alpha2, b_alpha, gla_norm_g, lns,
               bias=None):
    t, d = x.shape
    sh1, sc1, g1, sh2, sc2, g2, sh3, sc3, g3 = [mod[i:i + 1] for i in range(N_MOD)]
    ln1_g, ln1_b, ln2_g, ln2_b, ln3_g, ln3_b = lns
    n_heads_a, n_rel = rel_bias.shape
    rank, bk = w_alpha2.shape
    hv = gla_norm_g.shape[1]

    w1 = weights_of("ffn1_in", mod)
    h1, saved1, w1["out"] = ffn_forward(x, sh1, sc1, g1, w1["in"], lambda s: weights_of("ffn1_out", s)["out"],
                                        ln1_g, ln1_b, "ffn1")
    wm = weights_of("mix", h1)
    a_width = wm["proj_a"].shape[1]
    bv = wm["proj_b"].shape[1]
    nh_b = bv // hv
    hk = bk // nh_b
    cols, main, total = mix_layout(d, a_width, bk, bv)
    w_mix = mix_weight_in(wm["in_t"], main, rank)
    p = mod_matmul(h1, sh2, sc2, w_mix, MIX_BLOCK, "mix_in")
    bias = bias_table(rel_bias) if bias is None else bias
    dh = a_width // n_heads_a
    assert PAIR * dh == LANE
    ya, lse = attn_fwd(p, cols, bias, dh, "attn_fwd")
    wa_pad = jnp.zeros((LANE, bk), BF16).at[:rank].set(w_alpha2.astype(BF16))
    o_b, yb, states = gla_fwd(p, cols, wa_pad, b_alpha, gla_norm_g, nh_b, hk, hv, "gla_fwd")
    pa, pb, merged = proj_merge(ya, yb, wm["proj_a"], wm["proj_b"], p, cols["ga"], cols["gb"], "proj_merge")
    m, z2, h2 = out_ln(merged, wm["out"], h1, g2, ln2_g, ln2_b, 1.0, "mix_out")
    w3 = weights_of("ffn2", h2)
    h3, saved3, _ = ffn_forward(h2, sh3, sc3, g3, w3["in"], lambda s: w3["out"], ln3_g, ln3_b, "ffn2")

    carry3, gr3, loss = ffn_backward_weights(h3, saved3, sh3, sc3, g3, w3["in"], w3["out"], ln3_g, "ffn2",
                                             target=target)
    token = grads_ready("ffn2", dict(ffn2_in=gr3["w_in"], ffn2_out=gr3["w_out"]))
    dh2, dsc3, dsh3 = ffn_backward_input(carry3, saved3, sc3, w3["in"], "ffn2", after=token)
    token = grads_sent("ffn2", dh2)
    dz2, dm, dln2_g, dln2_b, dg2 = ln_bwd(dh2, z2, m, _after(ln2_g, token), g2, 1.0, "mix_ln_bwd")
    dpa, dpb, dga, dgb = merge_bwd(dm, wm["out"], p, cols["ga"], cols["gb"], pa, pb, "merge_bwd")
    tk = _row_tile(t, 512, 16)
    dw_mix_out = matmul_tn("mix_dwout", merged, (tk, 512), lambda n, k: (k, n), dm, (tk, d), lambda n, k: (k, 0),
                           _sds((d, d), BF16), (512, d), lambda n, k: (n, 0), d // 512)
    tm = _row_tile(t, 512, 16)
    pbw = wm["proj_a"].shape[2]
    dya = matmul_nt_blocks("proj_a_dy", dpa, (tm, pbw), lambda i, k: (i, k), wm["proj_a"], t)
    dyb = matmul_nt_blocks("proj_b_dy", dpb, (tm, pbw), lambda i, k: (i, k), wm["proj_b"], t)
    dw_pa = matmul_tn("proj_a_dw", ya, (tk, a_width), lambda n, k: (k, 0), dpa, (tk, pbw), lambda n, k: (k, n),
                      _sds((N_DEV, a_width, pbw), BF16), (None, a_width, pbw), lambda n, k: (n, 0, 0), N_DEV)
    dw_pb = matmul_tn("proj_b_dw", yb, (tk, bv), lambda n, k: (k, 0), dpb, (tk, pbw), lambda n, k: (k, n),
                      _sds((N_DEV, bv, pbw), BF16), (None, bv, pbw), lambda n, k: (n, 0, 0), N_DEV)
    dqb, dkb, dvb, drb, dgate, dgn = gla_bwd(p, cols, wa_pad, b_alpha, gla_norm_g, o_b, states, dyb,
                                             nh_b, hk, hv, "gla_bwd")
    dlr, dwa_pad, dba = gate_bwd(p, cols["lr"], dgate, wa_pad, "gate_bwd")
    dqa, dka, dva, dbias = attn_bwd(p, cols, bias, lse, dya, dh, "attn_bwd")
    d_rel = rel_bias_grad(bias_grad_skew(dbias), jnp.asarray(bias_clip_map(n_rel)), "rel_bias_grad")
    dp = jnp.concatenate([dqa, dka, dva, dqb, dkb, dvb, drb,
                          dga, dgb, dlr], axis=1)
    dw_mix_t = matmul_tn("mix_dwin", dp, (tk, MIX_BLOCK), lambda n, k: (k, n), h1, (tk, d), lambda n, k: (k, 0),
                         _sds((total, d), BF16), (MIX_BLOCK, d), lambda n, k: (n, 0), total // MIX_BLOCK,
                         mod=(sh2, sc2), mod_b=True)
    dw_mix_in = mix_weight_grad_out(dw_mix_t, main, rank, wm["in_t"].shape[1])
    token = grads_ready("mix", dict(mix_in=dw_mix_in, proj_a=dw_pa, proj_b=dw_pb,
                                    mix_out=dw_mix_out.reshape(N_DEV, d // N_DEV, d)))
    tm = _row_tile(t, 512, 16)
    dh1, dsc2, dsh2 = matmul_nt_blocks("mix_du", dp, (tm, MIX_BLOCK), lambda i, k: (i, k), w_mix, t,
                                       resid=(dz2, h1, _after(sc2, token)))
    token = grads_sent("mix", dh1)
    carry1, gr1, _ = ffn_backward_weights(dh1, saved1, sh1, sc1, g1, w1["in"], w1["out"], _after(ln1_g, token),
                                          "ffn1")
    token = grads_ready("ffn1", dict(ffn1_in=gr1["w_in"], ffn1_out=gr1["w_out"]))
    dx, dsc1, dsh1 = ffn_backward_input(carry1, saved1, sc1, w1["in"], "ffn1", after=token)

    dmod = [dsh1, dsc1, gr1["g"], dsh2, dsc2, dg2, dsh3, dsc3, gr3["g"]]
    small = dict(ln1_g=gr1["ln_g"], ln1_b=gr1["ln_b"], ln2_g=dln2_g, ln2_b=dln2_b, ln3_g=gr3["ln_g"],
                 ln3_b=gr3["ln_b"], b_alpha=dba, gla_norm_g=dgn, w_alpha2=dwa_pad[:rank], rel_bias=d_rel)
    return loss, dx, dmod, small


GROUPS = dict(ffn1=("ffn1_in", "ffn1_out"), mix=("mix_in", "proj_a", "proj_b", "mix_out"),
              ffn2=("ffn2_in", "ffn2_out"))
GATHERS = dict(ffn1_in=("ffn1_in",), ffn1_out=("ffn1_out",), mix=GROUPS["mix"], ffn2=GROUPS["ffn2"])
SMALL_REPLICATED = ("b_ada", "ln1_g", "ln1_b", "ln2_g", "ln2_b", "ln3_g", "ln3_b", "b_alpha", "gla_norm_g")
SMALL_SHARDED = ("rel_bias", "w_alpha2")
WEIGHT_ORDER = ("w_ada", "b_ada", "ffn1_w_in", "ffn1_w_out", "ln1_g", "ln1_b", "w_mix_in", "rel_bias", "w_alpha2",
                "b_alpha", "gla_norm_g", "w_proj_a", "w_proj_b", "w_mix_out", "ln2_g", "ln2_b", "ffn2_w_in",
                "ffn2_w_out", "ln3_g", "ln3_b")
BIG_NAME = dict(ffn1_in="ffn1_w_in", ffn1_out="ffn1_w_out", mix_in="w_mix_in", proj_a="w_proj_a",
                proj_b="w_proj_b", mix_out="w_mix_out", ffn2_in="ffn2_w_in", ffn2_out="ffn2_w_out")


def kernel(x, c, w_ada, b_ada, ffn1_w_in, ffn1_w_out, ln1_g, ln1_b, w_mix_in, rel_bias, w_alpha2, b_alpha, gla_norm_g, w_proj_a, w_proj_b, w_mix_out, ln2_g, ln2_b, ffn2_w_in, ffn2_w_out, ln3_g, ln3_b, loss_target, m_w_ada, m_b_ada, m_ffn1_w_in, m_ffn1_w_out, m_ln1_g, m_ln1_b, m_w_mix_in, m_rel_bias, m_w_alpha2, m_b_alpha, m_gla_norm_g, m_w_proj_a, m_w_proj_b, m_w_mix_out, m_ln2_g, m_ln2_b, m_ffn2_w_in, m_ffn2_w_out, m_ln3_g, m_ln3_b, v_w_ada, v_b_ada, v_ffn1_w_in, v_ffn1_w_out, v_ln1_g, v_ln1_b, v_w_mix_in, v_rel_bias, v_w_alpha2, v_b_alpha, v_gla_norm_g, v_w_proj_a, v_w_proj_b, v_w_mix_out, v_ln2_g, v_ln2_b, v_ffn2_w_in, v_ffn2_w_out, v_ln3_g, v_ln3_b):
    env = dict(locals())
    w = {n: env[n] for n in WEIGHT_ORDER}
    mom = {n: env["m_" + n] for n in WEIGHT_ORDER}
    var = {n: env["v_" + n] for n in WEIGHT_ORDER}
    me = _me()
    dev = _lin(me)
    core = jnp.reshape(me[2], (1,)).astype(jnp.int32)
    chip = jnp.reshape(2 * me[0] + me[1], (1,)).astype(jnp.int32)
    d = x.shape[-1]

    def shard(n):
        s = w[BIG_NAME[n]][0].astype(BF16)
        return s.T if n == "mix_in" else s

    dev_idx = jnp.reshape(dev, (1,)).astype(jnp.int32)
    started = {}

    def start(grp, after):
        shards = [shard(n) for n in GATHERS[grp]]
        lands = [place_own(dev_idx, s, f"place_own_{n}") for n, s in zip(GATHERS[grp], shards)]
        started[grp] = gather_start(shards, lands, after, f"gather_start_{grp}")
        return started[grp][-1]

    small_w = all_gather_small(jnp.concatenate([rel_bias[0], w_alpha2[0]], axis=1), "gather_small_w")
    n_rel_cols = rel_bias.shape[-1]
    rel_full = small_w[:, :, :n_rel_cols].transpose(1, 0, 2).reshape(small_w.shape[1], -1)
    wa2_full = small_w[:, :, n_rel_cols:].transpose(1, 0, 2).reshape(small_w.shape[1], -1)

    first, *rest = list(GATHERS)
    order = start(first, small_w[0, :1, :1])
    order, w, mom, var = lax.optimization_barrier((order, w, mom, var))

    ada_cols = w_ada.shape[-1]
    bias = bias_table(rel_full)
    c_all = all_gather_small(c, "gather_c", after=(order, bias))[:, 0, :]
    b_cols = lax.dynamic_slice_in_dim(b_ada, dev * ada_cols, ada_cols, axis=1)
    mod_cols = adaln_cols(c_all, w_ada[0], b_cols, "adaln_cols")
    mod_all = all_gather_small(mod_cols, "gather_mod")
    mod = lax.dynamic_index_in_dim(mod_all, dev, axis=1, keepdims=False).reshape(N_MOD, d)

    order = mod_all[0, :1, :1]
    for grp in rest:
        order = start(grp, order)
    mod = _after(mod, order)

    def weights_of(grp, after):
        _, zones = gather_wait(started[grp], after, f"gather_wait_{grp}")
        full = dict(zip(GATHERS[grp], gather_forward(zones, f"gather_forward_{grp}")))
        if grp == "mix":
            return dict(in_t=full["mix_in"], proj_a=full["proj_a"], proj_b=full["proj_b"],
                        out=full["mix_out"].reshape(-1, d))
        return {"in" if n.endswith("_in") else "out": v if n.endswith("_in") else v.reshape(-1, d)
                for n, v in full.items()}

    pairs, exchanges = {}, {}
    last = list(GROUPS)[0]

    def to_chips(grp, grads, got):
        sums = [pair_add(core, grads[n], g, f"grad_pair_add_{n}") for n, g in zip(GROUPS[grp], got)]
        exchanges[grp] = exchange_start(sums, chip_routes, f"grad_chip_start_{grp}")
        return exchanges[grp][-1]

    def grads_ready(grp, grads):
        if grp == last:
            return to_chips(grp, grads, pair_exchange([grads[n] for n in GROUPS[grp]], f"grad_pair_exchange_{grp}"))
        pairs[grp] = exchange_start([grads[n] for n in GROUPS[grp]], pair_routes, f"grad_pair_start_{grp}")
        return pairs[grp][-1]

    def grads_sent(grp, after):
        if grp == last:
            return None
        sent, got = exchange_wait(pairs[grp], pair_routes, after, f"grad_pair_wait_{grp}")
        return to_chips(grp, dict(zip(GROUPS[grp], sent)), got)

    lns = [ln1_g, ln1_b, ln2_g, ln2_b, ln3_g, ln3_b]
    loss, dx, dmod, small = local_step(x[0], loss_target[0], mod, weights_of, grads_ready, grads_sent, rel_full,
                                       wa2_full, b_alpha, gla_norm_g, lns, bias=bias)

    out = {}

    def finish(grp, after):
        sums, recv = exchange_wait(exchanges[grp], chip_routes, after, f"grad_chip_wait_{grp}")
        for n, hsum, r in zip(GROUPS[grp], sums, recv):
            full = BIG_NAME[n]
            if n == "mix_in":
                g = owned_sum(chip, hsum, r, f"owned_sum_{n}").T
                res_n = adamw_sum(g[None], w[full][0], mom[full][0], var[full][0], f"adamw_{n}")
            else:
                res_n = adamw_owned(chip, hsum, r, w[full][0], mom[full][0], var[full][0], f"adamw_{n}")
            out[full] = [o[None] for o in res_n]
            after = res_n[0]
        return after

    order = dx
    for grp in reversed(list(GROUPS)[1:]):
        order = finish(grp, order)

    pieces = (list(dmod) + [small[n].reshape(1, -1) for n in SMALL_REPLICATED[1:] + SMALL_SHARDED] + [loss])
    parts = all_gather_rows(pieces, order, "gather_small_grads")
    loss = jnp.sum(parts[:, 0, parts.shape[2] - loss.shape[1]])
    n_mod = N_MOD * d
    dmod_all = parts[:, 0, :n_mod]
    g_w_ada = adaln_wgrad(c_all, lax.dynamic_slice_in_dim(dmod_all, dev * ada_cols, ada_cols, axis=1), "adaln_wgrad")
    out["w_ada"] = [o[None] for o in adamw_sum(g_w_ada[None], w_ada[0], m_w_ada[0], v_w_ada[0], "adamw_w_ada")]

    sources, where, off = [parts], [], 0
    for n in SMALL_REPLICATED:
        where.append((0, off))
        off += w[n].size
    for n in SMALL_SHARDED:
        rows, cols_local = w[n].shape[1], w[n].shape[2]
        full_part = parts[:, 0, off:off + rows * cols_local * N_DEV].reshape(N_DEV, rows, cols_local * N_DEV)
        mine = lax.dynamic_slice_in_dim(full_part, dev * cols_local, cols_local, axis=2)
        where.append((len(sources), 0))
        sources.append(mine.reshape(N_DEV, 1, rows * cols_local))
        off += rows * cols_local * N_DEV
    names = SMALL_REPLICATED + SMALL_SHARDED
    res = adamw_rows(sources, where, *[[src[n].reshape(1, -1) for n in names] for src in (w, mom, var)],
                     "adamw_small")
    for n, res_n in zip(names, res):
        out[n] = [r.reshape(w[n].shape) for r in res_n]

    finish(last, res[0][0])

    flat = [loss, dx[None]]
    for k in range(4):
        flat += [out[n][k] for n in WEIGHT_ORDER]
    return tuple(flat)
```

```python
import functools

import numpy as np
import jax
import jax.numpy as jnp
from jax import lax
from jax.experimental import pallas as pl
from jax.experimental.pallas import tpu as pltpu

F32 = jnp.float32
BF16 = jnp.bfloat16
MESH = pl.DeviceIdType.MESH
N_DEV = 8
N_CHIP = 4

CHUNK = 64
A_PAST_CHUNKS = 8
REL_CLIP = 256
GATE_TAU = 16.0
N_MOD = 9
DEPTH = 1
ALPHA = (2.0 * DEPTH) ** 0.25
LN_EPS = 1e-5
RMS_EPS = 1e-6
ADAM_LR = 0.001
ADAM_B1 = 0.9
ADAM_B2 = 0.999
ADAM_EPS = 1e-08
ADAM_WD = 0.01
ADAM_STEP = 10

LANE = 128
VMEM_LIMIT = 56 * 2 ** 20
QB = 4 * CHUNK
KW = 3 * QB
GB = 8 * CHUNK
NEG = -1e30
HI = lax.Precision.HIGHEST

ANY = pl.BlockSpec(memory_space=pl.ANY)
VMEM_SPEC = pl.BlockSpec(memory_space=pltpu.VMEM)


def _params(*sem):
    return pltpu.CompilerParams(dimension_semantics=sem, vmem_limit_bytes=VMEM_LIMIT)


def _sds(shape, dtype):
    return jax.ShapeDtypeStruct(shape, dtype)


def _dot(a, b):
    return jnp.dot(a, b, preferred_element_type=F32)


def _dot_nt(a, b):
    return lax.dot_general(a, b, (((1,), (1,)), ((), ())), preferred_element_type=F32)


def _dot_tn(a, b):
    return lax.dot_general(a, b, (((0,), (0,)), ((), ())), preferred_element_type=F32)


def _sigmoid(x):
    return 0.5 * jnp.tanh(0.5 * x) + 0.5


def _colsum(x):
    return jnp.sum(x, axis=0, keepdims=True)


def _row_tile(rows, cap, mult):
    for t in range(min(rows, cap), 0, -1):
        if rows % t == 0 and t % mult == 0:
            return t
    return rows


def _me():
    return lax.axis_index("x"), lax.axis_index("y"), lax.axis_index("c")


def _flip(me, k):
    return tuple((1 - p) if (k >> s) & 1 else p for p, s in zip(me, (2, 1, 0)))


def _lin(p):
    return 4 * p[0] + 2 * p[1] + p[2]


def _gather_direct(x_ref, out_ref, send_sems, recv_sems, local_sem):
    me = _me()
    mine = pltpu.make_async_copy(x_ref, out_ref.at[_lin(me)], local_sem)
    mine.start()
    sends = []
    for k in range(1, N_DEV):
        cp = pltpu.make_async_remote_copy(
            src_ref=x_ref, dst_ref=out_ref.at[_lin(me)], send_sem=send_sems.at[k - 1],
            recv_sem=recv_sems.at[k - 1], device_id=_flip(me, k), device_id_type=MESH)
        cp.start()
        sends.append(cp)
    for k in range(1, N_DEV):
        peer = _flip(me, k)
        pltpu.make_async_remote_copy(
            src_ref=x_ref, dst_ref=out_ref.at[_lin(peer)], send_sem=send_sems.at[k - 1],
            recv_sem=recv_sems.at[k - 1], device_id=peer, device_id_type=MESH).wait_recv()
    for cp in sends:
        cp.wait_send()
    mine.wait()


GATHER_SEMS = [pltpu.SemaphoreType.DMA((N_DEV - 1,)), pltpu.SemaphoreType.DMA((N_DEV - 1,)), pltpu.SemaphoreType.DMA]


def all_gather_small(x, name, after=()):
    r, n = x.shape

    def body(x_ref, *refs):
        _gather_direct(x_ref, *refs[len(after):])

    return pl.pallas_call(
        body, name=name, out_shape=_sds((N_DEV, r, n), x.dtype),
        in_specs=[VMEM_SPEC] + [ANY] * len(after), out_specs=VMEM_SPEC, scratch_shapes=GATHER_SEMS,
    )(x, *after)


def all_gather_rows(pieces, after, name):
    sizes = [x.shape[1] for x in pieces]
    total = sum(sizes)
    assert all(n % LANE == 0 for n in sizes)

    def body(*refs):
        ins = refs[:len(pieces)]
        out_ref, row, send_sems, recv_sems, local_sem = refs[len(pieces) + 1:]
        off = 0
        for x_ref, n in zip(ins, sizes):
            row[:, off:off + n] = x_ref[...]
            off += n
        _gather_direct(row, out_ref, send_sems, recv_sems, local_sem)

    return pl.pallas_call(
        body, name=name, out_shape=_sds((N_DEV, 1, total), F32),
        in_specs=[VMEM_SPEC] * len(pieces) + [ANY], out_specs=VMEM_SPEC,
        scratch_shapes=[pltpu.VMEM((1, total), F32)] + GATHER_SEMS,
    )(*pieces, after)


HBM_SPEC = pl.BlockSpec(memory_space=pltpu.HBM)
SEM_SPEC = pl.BlockSpec(memory_space=pltpu.SEMAPHORE)
EFFECT = pltpu.SideEffectType.DATAFLOW_SIDE_EFFECTING
FIRST = N_CHIP


def _hbm(v):
    return pltpu.with_memory_space_constraint(v, pltpu.HBM)


def _other_chips(mx, my):
    return [(1 - mx, my), (mx, 1 - my), (1 - mx, 1 - my)]


def place_own(dev, shard, name):
    rows, cols = shard.shape
    tr, tc = _tile2(rows, cols, 16)

    def body(dev_ref, s_ref, land_ref, o_ref):
        o_ref[...] = s_ref[...]

    land = lax.empty((N_DEV, rows, cols), shard.dtype)
    return pl.pallas_call(
        body, name=name, out_shape=_sds(land.shape, land.dtype),
        grid_spec=pltpu.PrefetchScalarGridSpec(
            num_scalar_prefetch=1, grid=(rows // tr, cols // tc),
            in_specs=[pl.BlockSpec((tr, tc), lambda i, j, d: (i, j)), ANY],
            out_specs=pl.BlockSpec((None, tr, tc), lambda i, j, d: (d[0], i, j))),
        input_output_aliases={2: 0}, compiler_params=_params("parallel", "parallel"),
    )(dev, shard, land)


def gather_start(shards, lands, after, name):
    n = len(shards)

    def body(*refs):
        ins, zones = refs[:n], refs[n:2 * n]
        send_sems, recv_sems = refs[2 * n + 1], refs[2 * n + 2]
        token = refs[-1]
        me = _me()
        mx, my, mc = me
        for a in range(n):
            dst = zones[a].at[_lin(me)]
            targets = [(mx, my, 1 - mc)] + [(*chip, mc) for chip in _other_chips(mx, my)]
            for k, to in enumerate(targets):
                pltpu.make_async_remote_copy(
                    src_ref=ins[a], dst_ref=dst, send_sem=send_sems.at[a * FIRST + k],
                    recv_sem=recv_sems.at[a * FIRST + k], device_id=to, device_id_type=MESH).start()
        token[...] = jnp.zeros_like(token)

    sems = pltpu.SemaphoreType.DMA((n * FIRST,))
    out = pl.pallas_call(
        body, name=name,
        out_shape=(sems, sems, *[pltpu.HBM(s.shape, s.dtype) for s in shards],
                   *[pltpu.HBM(z.shape, z.dtype) for z in lands], _sds((8, LANE), F32)),
        in_specs=[HBM_SPEC] * (2 * n) + [ANY],
        out_specs=(SEM_SPEC, SEM_SPEC, *[HBM_SPEC] * (2 * n), VMEM_SPEC),
        input_output_aliases={a: 2 + a for a in range(2 * n)},
        compiler_params=pltpu.CompilerParams(has_side_effects=EFFECT),
    )(*[_hbm(s) for s in shards], *[_hbm(z) for z in lands], after)
    return out[0], out[1], out[2:2 + n], out[2 + n:2 + 2 * n], out[-1]


def gather_wait(started, after, name):
    send_sems, recv_sems, shards, lands, _ = started
    n = len(shards)

    def body(*refs):
        ins, zones = refs[:n], refs[n:2 * n]
        send_ref, recv_ref = refs[2 * n], refs[2 * n + 1]
        mx, my, mc = _me()
        for a in range(n):
            for k in range(FIRST):
                cp = pltpu.make_async_remote_copy(
                    src_ref=ins[a], dst_ref=zones[a].at[0], send_sem=send_ref.at[a * FIRST + k],
                    recv_sem=recv_ref.at[a * FIRST + k], device_id=(mx, my, 1 - mc), device_id_type=MESH)
                cp.wait_send()
                cp.wait_recv()

    out = pl.pallas_call(
        body, name=name,
        out_shape=(*[pltpu.HBM(s.shape, s.dtype) for s in shards], *[pltpu.HBM(z.shape, z.dtype) for z in lands]),
        in_specs=[HBM_SPEC] * (2 * n) + [SEM_SPEC, SEM_SPEC, ANY], out_specs=tuple([HBM_SPEC] * (2 * n)),
        input_output_aliases={a: a for a in range(2 * n)},
        compiler_params=pltpu.CompilerParams(has_side_effects=EFFECT),
    )(*shards, *lands, send_sems, recv_sems, after)
    return out[:n], out[n:]


def gather_forward(lands, name):
    n = len(lands)
    rel = N_CHIP - 1

    def body(*refs):
        zones, outs = refs[:n], refs[n:2 * n]
        send_sems, recv_sems = refs[2 * n:]
        mx, my, mc = _me()
        chips = _other_chips(mx, my)

        def copy(a, j, core):
            blk = _lin((*chips[j], core))
            return pltpu.make_async_remote_copy(
                src_ref=zones[a].at[blk], dst_ref=outs[a].at[blk], send_sem=send_sems.at[a * rel + j],
                recv_sem=recv_sems.at[a * rel + j], device_id=(mx, my, 1 - mc), device_id_type=MESH)

        sends = [copy(a, j, mc) for a in range(n) for j in range(rel)]
        for cp in sends:
            cp.start()
        for a in range(n):
            for j in range(rel):
                copy(a, j, 1 - mc).wait_recv()
        for cp in sends:
            cp.wait_send()

    return pl.pallas_call(
        body, name=name, out_shape=[_sds(z.shape, z.dtype) for z in lands],
        in_specs=[ANY] * n, out_specs=[ANY] * n, input_output_aliases={a: a for a in range(n)},
        scratch_shapes=[pltpu.SemaphoreType.DMA((n * rel,)), pltpu.SemaphoreType.DMA((n * rel,))],
    )(*lands)


def forward_start(lands, name):
    n = len(lands)
    rel = N_CHIP - 1

    def body(*refs):
        zones = refs[:n]
        send_sems, recv_sems = refs[n], refs[n + 1]
        token = refs[-1]
        mx, my, mc = _me()
        for a in range(n):
            for j, chip in enumerate(_other_chips(mx, my)):
                blk = zones[a].at[_lin((*chip, mc))]
                pltpu.make_async_remote_copy(
                    src_ref=blk, dst_ref=blk, send_sem=send_sems.at[a * rel + j], recv_sem=recv_sems.at[a * rel + j],
                    device_id=(mx, my, 1 - mc), device_id_type=MESH).start()
        token[...] = jnp.zeros_like(token)

    sems = pltpu.SemaphoreType.DMA((n * rel,))
    out = pl.pallas_call(
        body, name=name,
        out_shape=(sems, sems, *[pltpu.HBM(z.shape, z.dtype) for z in lands], _sds((8, LANE), F32)),
        in_specs=[HBM_SPEC] * n, out_specs=(SEM_SPEC, SEM_SPEC, *[HBM_SPEC] * n, VMEM_SPEC),
        input_output_aliases={a: 2 + a for a in range(n)},
        compiler_params=pltpu.CompilerParams(has_side_effects=EFFECT),
    )(*[_hbm(z) for z in lands])
    return out[0], out[1], out[2:2 + n], out[-1]


def forward_wait(started, after, name):
    send_sems, recv_sems, lands, _ = started
    n = len(lands)
    rel = N_CHIP - 1

    def body(*refs):
        zones = refs[:n]
        send_ref, recv_ref = refs[n], refs[n + 1]
        mx, my, mc = _me()
        for a in range(n):
            for j, chip in enumerate(_other_chips(mx, my)):
                cp = pltpu.make_async_remote_copy(
                    src_ref=zones[a].at[_lin((*chip, mc))], dst_ref=zones[a].at[_lin((*chip, 1 - mc))],
                    send_sem=send_ref.at[a * rel + j], recv_sem=recv_ref.at[a * rel + j],
                    device_id=(mx, my, 1 - mc), device_id_type=MESH)
                cp.wait_send()
                cp.wait_recv()

    out = pl.pallas_call(
        body, name=name, out_shape=tuple(pltpu.HBM(z.shape, z.dtype) for z in lands),
        in_specs=[HBM_SPEC] * n + [SEM_SPEC, SEM_SPEC, ANY], out_specs=tuple([HBM_SPEC] * n),
        input_output_aliases={a: a for a in range(n)},
        compiler_params=pltpu.CompilerParams(has_side_effects=EFFECT),
    )(*lands, send_sems, recv_sems, after)
    return list(out)


def pair_exchange(gs, name):
    n = len(gs)

    def body(*refs):
        ins, outs = refs[:n], refs[n:2 * n]
        send_sems, recv_sems = refs[2 * n:]
        mx, my, mc = _me()
        cps = []
        for a in range(n):
            for q in range(N_CHIP):
                cp = pltpu.make_async_remote_copy(
                    src_ref=ins[a].at[2 * q + (1 - mc)], dst_ref=outs[a].at[q],
                    send_sem=send_sems.at[a * N_CHIP + q], recv_sem=recv_sems.at[a * N_CHIP + q],
                    device_id=(mx, my, 1 - mc), device_id_type=MESH)
                cp.start()
                cps.append(cp)
        for cp in cps:
            cp.wait()

    return pl.pallas_call(
        body, name=name, out_shape=[_sds((N_CHIP,) + g.shape[1:], g.dtype) for g in gs],
        in_specs=[ANY] * n, out_specs=[ANY] * n,
        scratch_shapes=[pltpu.SemaphoreType.DMA((n * N_CHIP,)), pltpu.SemaphoreType.DMA((n * N_CHIP,))],
    )(*gs)


def _tile2(rows, cols, row_mult):
    tr = _row_tile(rows, 512, row_mult)
    if tr < rows or rows * cols <= 2 ** 20:
        return tr, cols
    return rows, _row_tile(cols, 512, LANE)


def pair_add(core, g, got, name):
    _, rows, cols = g.shape
    tr, tc = _tile2(rows, cols, 16)

    def body(core_ref, g_ref, got_ref, h_ref):
        h_ref[...] = (g_ref[...].astype(F32) + got_ref[...].astype(F32)).astype(h_ref.dtype)

    blk = pl.BlockSpec((None, tr, tc), lambda q, i, j, c: (q, i, j))
    return pl.pallas_call(
        body, name=name, out_shape=_sds((N_CHIP, rows, cols), g.dtype),
        grid_spec=pltpu.PrefetchScalarGridSpec(
            num_scalar_prefetch=1, grid=(N_CHIP, rows // tr, cols // tc),
            in_specs=[pl.BlockSpec((None, tr, tc), lambda q, i, j, c: (2 * q + c[0], i, j)), blk],
            out_specs=blk),
        compiler_params=_params("parallel", "parallel", "parallel"),
    )(core, g, got)


def chip_routes(mx, my, mc):
    return [(2 * px + py, k, (px, py, mc)) for k, (px, py) in enumerate(_other_chips(mx, my))]


def pair_routes(mx, my, mc):
    return [(2 * q + (1 - mc), q, (mx, my, 1 - mc)) for q in range(N_CHIP)]


def exchange_start(hs, routes, name):
    n = len(hs)
    rel = len(routes(0, 0, 0))
    lands = [lax.empty((rel,) + h.shape[1:], h.dtype) for h in hs]

    def body(*refs):
        ins, zones = refs[:n], refs[n:2 * n]
        send_sems, recv_sems = refs[2 * n], refs[2 * n + 1]
        token = refs[-1]
        for a in range(n):
            for k, (src, slot, to) in enumerate(routes(*_me())):
                pltpu.make_async_remote_copy(
                    src_ref=ins[a].at[src], dst_ref=zones[a].at[slot], send_sem=send_sems.at[a * rel + k],
                    recv_sem=recv_sems.at[a * rel + k], device_id=to, device_id_type=MESH).start()
        token[...] = jnp.zeros_like(token)

    sems = pltpu.SemaphoreType.DMA((n * rel,))
    out = pl.pallas_call(
        body, name=name,
        out_shape=(sems, sems, *[pltpu.HBM(h.shape, h.dtype) for h in hs],
                   *[pltpu.HBM(z.shape, z.dtype) for z in lands], _sds((8, LANE), F32)),
        in_specs=[HBM_SPEC] * (2 * n), out_specs=(SEM_SPEC, SEM_SPEC, *[HBM_SPEC] * (2 * n), VMEM_SPEC),
        input_output_aliases={a: 2 + a for a in range(2 * n)},
        compiler_params=pltpu.CompilerParams(has_side_effects=EFFECT),
    )(*[_hbm(h) for h in hs], *[_hbm(z) for z in lands])
    return out[0], out[1], out[2:2 + n], out[2 + n:2 + 2 * n], out[-1]


def exchange_wait(started, routes, after, name):
    send_sems, recv_sems, hs, lands, _ = started
    n = len(hs)
    rel = len(routes(0, 0, 0))

    def body(*refs):
        ins, zones = refs[:n], refs[n:2 * n]
        send_ref, recv_ref = refs[2 * n], refs[2 * n + 1]
        for a in range(n):
            for k, (src, slot, to) in enumerate(routes(*_me())):
                cp = pltpu.make_async_remote_copy(
                    src_ref=ins[a].at[src], dst_ref=zones[a].at[slot], send_sem=send_ref.at[a * rel + k],
                    recv_sem=recv_ref.at[a * rel + k], device_id=to, device_id_type=MESH)
                cp.wait_send()
                cp.wait_recv()

    out = pl.pallas_call(
        body, name=name,
        out_shape=(*[pltpu.HBM(h.shape, h.dtype) for h in hs], *[pltpu.HBM(z.shape, z.dtype) for z in lands]),
        in_specs=[HBM_SPEC] * (2 * n) + [SEM_SPEC, SEM_SPEC, ANY], out_specs=tuple([HBM_SPEC] * (2 * n)),
        input_output_aliases={a: a for a in range(2 * n)},
        compiler_params=pltpu.CompilerParams(has_side_effects=EFFECT),
    )(*hs, *lands, send_sems, recv_sems, after)
    return out[:n], out[n:]


def _adam(w, g, m, v):
    m = ADAM_B1 * m + (1.0 - ADAM_B1) * g
    v = ADAM_B2 * v + (1.0 - ADAM_B2) * (g * g)
    m_hat = m / (1.0 - ADAM_B1 ** ADAM_STEP)
    v_hat = v / (1.0 - ADAM_B2 ** ADAM_STEP)
    delta = -ADAM_LR * (m_hat / (jnp.sqrt(v_hat) + ADAM_EPS) + ADAM_WD * w)
    return delta, m, v


def adamw_owned(chip, h, got, w, m, v, name):
    rows, cols = w.shape
    tr = _row_tile(rows, 256, 16)

    def body(chip_ref, h_ref, got_ref, w_ref, m_ref, v_ref, g_out, d_out, m_out, v_out):
        g = h_ref[...].astype(F32)
        for k in range(N_CHIP - 1):
            g = g + got_ref[k].astype(F32)
        d, mn, vn = _adam(w_ref[...], g, m_ref[...], v_ref[...])
        g_out[...] = g
        d_out[...] = d
        m_out[...] = mn
        v_out[...] = vn

    blk = pl.BlockSpec((tr, cols), lambda i, c: (i, 0))
    return pl.pallas_call(
        body, name=name, out_shape=[_sds((rows, cols), F32)] * 4,
        grid_spec=pltpu.PrefetchScalarGridSpec(
            num_scalar_prefetch=1, grid=(rows // tr,),
            in_specs=[pl.BlockSpec((None, tr, cols), lambda i, c: (c[0], i, 0)),
                      pl.BlockSpec((N_CHIP - 1, tr, cols), lambda i, c: (0, i, 0)), blk, blk, blk],
            out_specs=[blk] * 4),
        compiler_params=_params("parallel"),
    )(chip, h, got, w, m, v)


def owned_sum(chip, h, got, name):
    _, rows, cols = h.shape
    tr, tc = _tile2(rows, cols, 16)

    def body(chip_ref, h_ref, got_ref, g_out):
        g = h_ref[...].astype(F32)
        for k in range(N_CHIP - 1):
            g = g + got_ref[k].astype(F32)
        g_out[...] = g

    return pl.pallas_call(
        body, name=name, out_shape=_sds((rows, cols), F32),
        grid_spec=pltpu.PrefetchScalarGridSpec(
            num_scalar_prefetch=1, grid=(rows // tr, cols // tc),
            in_specs=[pl.BlockSpec((None, tr, tc), lambda i, j, c: (c[0], i, j)),
                      pl.BlockSpec((N_CHIP - 1, tr, tc), lambda i, j, c: (0, i, j))],
            out_specs=pl.BlockSpec((tr, tc), lambda i, j, c: (i, j))),
        compiler_params=_params("parallel", "parallel"),
    )(chip, h, got)


def adamw_rows(sources, where, ws, ms, vs, name):
    n_src, n_par = len(sources), len(ws)

    def body(*refs):
        srcs = refs[:n_src]
        w_refs, m_refs, v_refs = (refs[n_src + j * n_par:n_src + (j + 1) * n_par] for j in range(3))
        outs = refs[n_src + 3 * n_par:]
        for k in range(n_par):
            src, off = srcs[where[k][0]], where[k][1]
            n = w_refs[k].shape[1]
            g = src[0, :, off:off + n]
            for dev in range(1, N_DEV):
                g = g + src[dev, :, off:off + n]
            d, mn, vn = _adam(w_refs[k][...], g, m_refs[k][...], v_refs[k][...])
            for o_ref, val in zip(outs[4 * k:4 * k + 4], (g, d, mn, vn)):
                o_ref[...] = val

    flat = pl.pallas_call(
        body, name=name, out_shape=[_sds(x.shape, F32) for x in ws for _ in range(4)],
        in_specs=[VMEM_SPEC] * (n_src + 3 * n_par), out_specs=[VMEM_SPEC] * (4 * n_par),
        compiler_params=pltpu.CompilerParams(vmem_limit_bytes=VMEM_LIMIT),
    )(*sources, *ws, *ms, *vs)
    return [flat[4 * k:4 * k + 4] for k in range(n_par)]


def adamw_sum(parts, w, m, v, name):
    n_parts, rows, cols = parts.shape
    tr = _row_tile(rows, 256, 8)

    def body(p_ref, w_ref, m_ref, v_ref, g_out, d_out, m_out, v_out):
        g = p_ref[0]
        for k in range(1, n_parts):
            g = g + p_ref[k]
        d, mn, vn = _adam(w_ref[...], g, m_ref[...], v_ref[...])
        g_out[...] = g
        d_out[...] = d
        m_out[...] = mn
        v_out[...] = vn

    blk = pl.BlockSpec((tr, cols), lambda i: (i, 0))
    return pl.pallas_call(
        body, name=name, out_shape=[_sds((rows, cols), F32)] * 4, grid=(rows // tr,),
        in_specs=[pl.BlockSpec((n_parts, tr, cols), lambda i: (0, i, 0)), blk, blk, blk],
        out_specs=[blk] * 4, compiler_params=_params("parallel"),
    )(parts, w, m, v)


def adaln_cols(c_all, w, b, name):
    d, n = w.shape
    tn = _row_tile(n, 768, LANE)

    def body(c_ref, w_ref, b_ref, o_ref):
        c = c_ref[...]
        o_ref[...] = jnp.dot(c * _sigmoid(c), w_ref[...], preferred_element_type=F32, precision=HI) + b_ref[...]

    return pl.pallas_call(
        body, name=name, out_shape=_sds((N_DEV, n), F32), grid=(n // tn,),
        in_specs=[pl.BlockSpec((N_DEV, d), lambda j: (0, 0)), pl.BlockSpec((d, tn), lambda j: (0, j)),
                  pl.BlockSpec((1, tn), lambda j: (0, j))],
        out_specs=pl.BlockSpec((N_DEV, tn), lambda j: (0, j)), compiler_params=_params("parallel"),
    )(c_all, w, b)


def adaln_wgrad(c_all, dmod_cols, name):
    d = c_all.shape[1]
    n = dmod_cols.shape[1]
    tn = _row_tile(n, 768, LANE)

    def body(c_ref, g_ref, o_ref):
        c = c_ref[...]
        o_ref[...] = lax.dot_general(c * _sigmoid(c), g_ref[...], (((0,), (0,)), ((), ())),
                                     preferred_element_type=F32, precision=HI)

    return pl.pallas_call(
        body, name=name, out_shape=_sds((d, n), F32), grid=(n // tn,),
        in_specs=[pl.BlockSpec((N_DEV, d), lambda j: (0, 0)), pl.BlockSpec((N_DEV, tn), lambda j: (0, j))],
        out_specs=pl.BlockSpec((d, tn), lambda j: (0, j)), compiler_params=_params("parallel"),
    )(c_all, dmod_cols)


def _modulate(h, sh, sc):
    return (h * (1.0 + sc) + sh).astype(BF16)


def ffn_in(h, sh, sc, w3, name):
    t, d = h.shape
    nb, _, bw = w3.shape
    half = nb // 2
    tm = _row_tile(t, 512, 16)

    def body(h_ref, sh_ref, sc_ref, wa_ref, wb_ref, a_ref, b_ref, s_ref):
        u = _modulate(h_ref[...], sh_ref[...], sc_ref[...])
        a = _dot(u, wa_ref[...])
        b = _dot(u, wb_ref[...])
        a_ref[...] = a.astype(BF16)
        b_ref[...] = b.astype(BF16)
        s_ref[...] = (a * _sigmoid(a) * b).astype(BF16)

    vec = pl.BlockSpec((1, d), lambda j, i: (0, 0))
    out = pl.BlockSpec((tm, bw), lambda j, i: (i, j))
    return pl.pallas_call(
        body, name=name, out_shape=[_sds((t, half * bw), BF16)] * 3, grid=(half, t // tm),
        in_specs=[pl.BlockSpec((tm, d), lambda j, i: (i, 0)), vec, vec,
                  pl.BlockSpec((None, d, bw), lambda j, i: (j, 0, 0), pipeline_mode=pl.Buffered(1)),
                  pl.BlockSpec((None, d, bw), lambda j, i: (j + half, 0, 0), pipeline_mode=pl.Buffered(1))],
        out_specs=[out] * 3, compiler_params=_params("parallel", "parallel"),
    )(h, sh, sc, w3, w3)


def mod_matmul(h, sh, sc, wt, bw, name):
    t, d = h.shape
    n = wt.shape[0]
    tm = _row_tile(t, 512, 16)

    def body(h_ref, sh_ref, sc_ref, w_ref, o_ref):
        o_ref[...] = _dot_nt(_modulate(h_ref[...], sh_ref[...], sc_ref[...]), w_ref[...]).astype(BF16)

    vec = pl.BlockSpec((1, d), lambda j, i: (0, 0))
    return pl.pallas_call(
        body, name=name, out_shape=_sds((t, n), BF16), grid=(n // bw, t // tm),
        in_specs=[pl.BlockSpec((tm, d), lambda j, i: (i, 0)), vec, vec, pl.BlockSpec((bw, d), lambda j, i: (j, 0))],
        out_specs=pl.BlockSpec((tm, bw), lambda j, i: (i, j)), compiler_params=_params("parallel", "parallel"),
    )(h, sh, sc, wt)


def out_ln(s, w, hin, gmod, ln_g, ln_b, coef, name):
    t, kdim = s.shape
    d = w.shape[1]
    tm = _row_tile(t, 256, 16)

    def body(s_ref, w_ref, hin_ref, gm_ref, g_ref, b_ref, f_ref, z_ref, h_ref):
        f = _dot(s_ref[...], w_ref[...])
        z = ALPHA * hin_ref[...] + (coef * gm_ref[...]) * f
        mu = jnp.mean(z, axis=-1, keepdims=True)
        zc = z - mu
        var = jnp.mean(zc * zc, axis=-1, keepdims=True)
        f_ref[...] = f.astype(BF16)
        z_ref[...] = z
        h_ref[...] = zc * lax.rsqrt(var + LN_EPS) * g_ref[...] + b_ref[...]

    vec = pl.BlockSpec((1, d), lambda i: (0, 0))
    row = pl.BlockSpec((tm, d), lambda i: (i, 0))
    return pl.pallas_call(
        body, name=name, out_shape=[_sds((t, d), BF16), _sds((t, d), F32), _sds((t, d), F32)],
        grid=(t // tm,),
        in_specs=[pl.BlockSpec((tm, kdim), lambda i: (i, 0)),
                  pl.BlockSpec((kdim, d), lambda i: (0, 0), pipeline_mode=pl.Buffered(1)), row, vec, vec, vec],
        out_specs=[row, row, row], compiler_params=_params("parallel"),
    )(s, w, hin, gmod, ln_g, ln_b)


def ln_bwd(dh, z, f, ln_g, gmod, coef, name, target=None):
    t, d = z.shape
    tm = _row_tile(t, 256, 16)
    head = target is not None

    def body(*refs):
        if head:
            dh_ref, tg_ref, z_ref, f_ref, g_ref, gm_ref, dz_ref, df_ref, dg_ref, db_ref, dgm_ref, loss_ref = refs
        else:
            dh_ref, z_ref, f_ref, g_ref, gm_ref, dz_ref, df_ref, dg_ref, db_ref, dgm_ref = refs
        i = pl.program_id(0)

        @pl.when(i == 0)
        def _():
            dg_ref[...] = jnp.zeros_like(dg_ref)
            db_ref[...] = jnp.zeros_like(db_ref)
            dgm_ref[...] = jnp.zeros_like(dgm_ref)
            if head:
                loss_ref[...] = jnp.zeros_like(loss_ref)

        dh = dh_ref[...]
        if head:
            err = dh - tg_ref[...]
            loss_ref[...] += 0.5 * jnp.sum(jnp.mean(err * err, axis=-1, keepdims=True))
            dh = err / d
        zv = z_ref[...]
        mu = jnp.mean(zv, axis=-1, keepdims=True)
        zc = zv - mu
        rstd = lax.rsqrt(jnp.mean(zc * zc, axis=-1, keepdims=True) + LN_EPS)
        xhat = zc * rstd
        dxh = dh * g_ref[...]
        dz = rstd * (dxh - jnp.mean(dxh, axis=-1, keepdims=True)
                     - xhat * jnp.mean(dxh * xhat, axis=-1, keepdims=True))
        dz_ref[...] = dz
        df_ref[...] = ((coef * gm_ref[...]) * dz).astype(BF16)
        dg_ref[...] += _colsum(dh * xhat)
        db_ref[...] += _colsum(dh)
        dgm_ref[...] += _colsum(coef * f_ref[...].astype(F32) * dz)

    vec = pl.BlockSpec((1, d), lambda i: (0, 0))
    row = pl.BlockSpec((tm, d), lambda i: (i, 0))
    ins = [dh] + ([target] if head else []) + [z, f, ln_g, gmod]
    in_specs = [row] + ([row] if head else []) + [row, row, vec, vec]
    out_shape = [_sds((t, d), F32), _sds((t, d), BF16)] + [_sds((1, d), F32)] * 3
    out_specs = [row, row, vec, vec, vec]
    if head:
        out_shape.append(_sds((1, LANE), F32))
        out_specs.append(pl.BlockSpec((1, LANE), lambda i: (0, 0)))
    return pl.pallas_call(
        body, name=name, out_shape=out_shape, grid=(t // tm,), in_specs=in_specs, out_specs=out_specs,
        compiler_params=_params("arbitrary"),
    )(*ins)


def ffn_bwd_act(df, w, a, b, name):
    t, d = df.shape
    fdim = w.shape[0]
    bw = fdim // (N_DEV // 2)
    tm = _row_tile(t, 512, 16)

    def body(df_ref, w_ref, a_ref, b_ref, o_ref):
        ds = _dot_nt(df_ref[...], w_ref[...])
        av = a_ref[...].astype(F32)
        sg = _sigmoid(av)
        o_ref[0] = (ds * b_ref[...].astype(F32) * (sg * (1.0 + av * (1.0 - sg)))).astype(BF16)
        o_ref[1] = (ds * (av * sg)).astype(BF16)

    act = pl.BlockSpec((tm, bw), lambda j, i: (i, j))
    return pl.pallas_call(
        body, name=name, out_shape=_sds((2, t, fdim), BF16), grid=(fdim // bw, t // tm),
        in_specs=[pl.BlockSpec((tm, d), lambda j, i: (i, 0)), pl.BlockSpec((bw, d), lambda j, i: (j, 0)), act, act],
        out_specs=pl.BlockSpec((2, tm, bw), lambda j, i: (0, i, j)),
        compiler_params=_params("parallel", "parallel"),
    )(df, w, a, b)


def matmul_tn(name, a, a_block, a_map, b, b_block, b_map, out_shape, o_block, o_map, n_out, mod=None,
              mod_b=False):
    tk = [s for s in a_block if s is not None][0]
    nk = a.shape[-2] // tk
    m, nn = [s for s in o_block if s is not None]

    def body(*refs):
        if mod is None:
            a_ref, b_ref, o_ref, acc = refs
        else:
            a_ref, sh_ref, sc_ref, b_ref, o_ref, acc = refs
        k = pl.program_id(1)

        @pl.when(k == 0)
        def _():
            acc[...] = jnp.zeros_like(acc)

        av, bv = a_ref[...], b_ref[...]
        if mod is not None and mod_b:
            bv = _modulate(bv, sh_ref[...], sc_ref[...])
        elif mod is not None:
            av = _modulate(av, sh_ref[...], sc_ref[...])
        acc[...] += _dot_tn(av, bv)

        @pl.when(k == nk - 1)
        def _():
            o_ref[...] = acc[...].astype(o_ref.dtype)

    ins = [a] + (list(mod) if mod is not None else []) + [b]
    in_specs = [pl.BlockSpec(a_block, a_map)]
    if mod is not None:
        vec = pl.BlockSpec((1, mod[0].shape[1]), lambda n, k: (0, 0))
        in_specs += [vec, vec]
    in_specs.append(pl.BlockSpec(b_block, b_map))
    return pl.pallas_call(
        body, name=name, out_shape=out_shape, grid=(n_out, nk), in_specs=in_specs,
        out_specs=pl.BlockSpec(o_block, o_map), scratch_shapes=[pltpu.VMEM((m, nn), F32)],
        compiler_params=_params("parallel", "arbitrary"),
    )(*ins)


def matmul_nt_blocks(name, dy, dy_block, dy_map, w3, t, resid=None):
    tm, bw = [s for s in dy_block if s is not None]
    rows = w3.ndim == 2
    if rows:
        nk, n = w3.shape[0] // bw, w3.shape[1]
    else:
        nk, n, _ = w3.shape

    def body(*refs):
        if resid is None:
            dy_ref, w_ref, o_ref, acc = refs
        else:
            dy_ref, w_ref, dz_ref, hin_ref, sc_ref, o_ref, dsc_ref, dsh_ref, acc = refs
        i, k = pl.program_id(0), pl.program_id(1)

        @pl.when(k == 0)
        def _():
            acc[...] = jnp.zeros_like(acc)

        if resid is not None:
            @pl.when((k == 0) & (i == 0))
            def _():
                dsc_ref[...] = jnp.zeros_like(dsc_ref)
                dsh_ref[...] = jnp.zeros_like(dsh_ref)

        acc[...] += _dot(dy_ref[...], w_ref[...]) if rows else _dot_nt(dy_ref[...], w_ref[...])

        @pl.when(k == nk - 1)
        def _():
            du = acc[...]
            if resid is None:
                o_ref[...] = du.astype(o_ref.dtype)
            else:
                o_ref[...] = ALPHA * dz_ref[...] + du * (1.0 + sc_ref[...])
                dsc_ref[...] += _colsum(du * hin_ref[...])
                dsh_ref[...] += _colsum(du)

    row = pl.BlockSpec((tm, n), lambda i, k: (i, 0))
    vec = pl.BlockSpec((1, n), lambda i, k: (0, 0))
    w_spec = pl.BlockSpec((bw, n), lambda i, k: (k, 0)) if rows else pl.BlockSpec((None, n, bw), lambda i, k: (k, 0, 0))
    in_specs = [pl.BlockSpec(dy_block, dy_map), w_spec]
    ins = [dy, w3]
    if resid is None:
        out_shape, out_specs = _sds((t, n), BF16), row
    else:
        ins += list(resid)
        once = pl.BlockSpec((tm, n), lambda i, k: (i, 0), pipeline_mode=pl.Buffered(1))
        in_specs += [once, once, vec]
        out_shape = [_sds((t, n), F32), _sds((1, n), F32), _sds((1, n), F32)]
        out_specs = [row, vec, vec]
    return pl.pallas_call(
        body, name=name, out_shape=out_shape, grid=(t // tm, nk), in_specs=in_specs, out_specs=out_specs,
        scratch_shapes=[pltpu.VMEM((tm, n), F32)], compiler_params=_params("arbitrary", "arbitrary"),
    )(*ins)


REL_W = KW + QB


def bias_table(rel_bias):
    nh, n_rel = rel_bias.shape
    lo = KW - QB - REL_CLIP
    hi = KW - lo - n_rel
    assert n_rel == REL_CLIP + CHUNK and lo >= 0 and hi >= 0
    first, last = rel_bias[:, :1], rel_bias[:, -1:]
    row = jnp.concatenate([jnp.broadcast_to(first, (nh, lo)), rel_bias, jnp.broadcast_to(last, (nh, hi)),
                           jnp.broadcast_to(first, (nh, QB))], axis=1)
    table = jnp.tile(row, (1, QB))[:, :QB * (REL_W - 1)].reshape(nh, QB, REL_W - 1)[:, :, :KW]
    q = np.arange(QB)[:, None] // CHUNK
    k = np.arange(KW)[None, :] // CHUNK
    band = (k >= q) & (k <= q + A_PAST_CHUNKS)
    return jnp.where(band[None], table, NEG)


def bias_grad_skew(dbias):
    nh = dbias.shape[0]
    flat = jnp.pad(dbias, ((0, 0), (0, 0), (0, REL_W - 1 - KW))).reshape(nh, QB * (REL_W - 1))
    return jnp.pad(flat, ((0, 0), (0, QB))).reshape(nh, QB, REL_W)


def bias_clip_map(n_rel):
    m = np.arange(REL_W)
    dist = np.where(m < KW, m, m - REL_W) - (KW - QB)
    idx = np.clip(dist, -REL_CLIP, CHUNK - 1) + REL_CLIP
    return (idx[:, None] == np.arange(n_rel)[None, :]).astype(np.float32)


PAIR = 2


def _pair_specs(col, rows_of):
    return [pl.BlockSpec((QB, LANE), functools.partial(lambda r, h, i: (rows_of(r, i), col // LANE + h), r))
            for r in range(3)]


def _earlier(r, i):
    return jnp.maximum(i - 2 + r, 0)


def _head_lanes(hh, dh):
    lane = lax.broadcasted_iota(jnp.int32, (1, LANE), 1)
    return (lane < dh) if hh == 0 else (lane >= dh)


def _only(x, lanes):
    return jnp.where(lanes, x, jnp.zeros_like(x))


def _scores(q, ks, bias, i, scale):
    s = jnp.concatenate([_dot_nt(q, kk) for kk in ks], axis=1) * scale + bias
    col = lax.broadcasted_iota(jnp.int32, s.shape, 1)
    return jnp.where(col >= (2 - i) * QB, s, NEG)


def attn_fwd(p, cols, bias, dh, name):
    t = p.shape[0]
    nh = bias.shape[0]
    scale = dh ** -0.5

    def body(q_ref, k0, k1, k2, v0, v1, v2, b_ref, o_ref, lse_ref):
        i = pl.program_id(1)
        q = q_ref[...]
        outs = []
        for hh in range(PAIR):
            lanes = _head_lanes(hh, dh)
            s = _scores(q, [_only(kk[...], lanes) for kk in (k0, k1, k2)], b_ref[hh], i, scale)
            m = jnp.max(s, axis=-1, keepdims=True)
            e = jnp.exp(s - m)
            l = jnp.sum(e, axis=-1, keepdims=True)
            eb = e.astype(BF16)
            o = sum(_dot(eb[:, r * QB:(r + 1) * QB], vv[...]) for r, vv in enumerate((v0, v1, v2)))
            outs.append(o / l)
            lse_ref[hh] = m + jnp.log(l)
        o_ref[...] = jnp.where(_head_lanes(0, dh), outs[0], outs[1]).astype(BF16)

    st = pl.BlockSpec((PAIR, QB, 1), lambda h, i: (h, i, 0))
    return pl.pallas_call(
        body, name=name, out_shape=[_sds((t, nh * dh), BF16), _sds((nh, t, 1), F32)], grid=(nh // PAIR, t // QB),
        in_specs=[pl.BlockSpec((QB, LANE), lambda h, i: (i, cols["qa"] // LANE + h))]
        + _pair_specs(cols["ka"], _earlier) + _pair_specs(cols["va"], _earlier)
        + [pl.BlockSpec((PAIR, QB, KW), lambda h, i: (h, 0, 0))],
        out_specs=[pl.BlockSpec((QB, LANE), lambda h, i: (i, h)), st],
        compiler_params=_params("parallel", "parallel"),
    )(p, p, p, p, p, p, p, bias)


def attn_bwd(p, cols, bias, lse, dy, dh, name):
    t = p.shape[0]
    nh = bias.shape[0]
    nb = t // QB
    scale = dh ** -0.5

    def body(q_ref, k0, k1, k2, v0, v1, v2, b_ref, lse_ref, dy_ref, dq_ref, dk_ref, dv_ref, db_ref, dk_acc, dv_acc):
        i = pl.program_id(1)

        @pl.when(i == 0)
        def _():
            db_ref[...] = jnp.zeros_like(db_ref)
            dk_acc[...] = jnp.zeros_like(dk_acc)
            dv_acc[...] = jnp.zeros_like(dv_acc)

        q, dyv = q_ref[...], dy_ref[...]
        ks = [k0[...], k1[...], k2[...]]
        dqs, dks, dvs = [], [], []
        for hh in range(PAIR):
            lanes = _head_lanes(hh, dh)
            s = _scores(q, [_only(kk, lanes) for kk in ks], b_ref[hh], i, scale)
            prob = jnp.exp(s - lse_ref[hh])
            dprob = jnp.concatenate([_dot_nt(dyv, _only(vv[...], lanes)) for vv in (v0, v1, v2)], axis=1)
            delta = jnp.sum(prob * dprob, axis=-1, keepdims=True)
            ds = prob * (dprob - delta)
            dsb, pb = ds.astype(BF16), prob.astype(BF16)
            dqs.append(sum(_dot(dsb[:, r * QB:(r + 1) * QB], kk) for r, kk in enumerate(ks)))
            dks.append([_dot_tn(dsb[:, r * QB:(r + 1) * QB], q) for r in range(3)])
            dvs.append([_dot_tn(pb[:, r * QB:(r + 1) * QB], dyv) for r in range(3)])
            db_ref[hh] += ds
        first = _head_lanes(0, dh)
        dq_ref[...] = (jnp.where(first, dqs[0], dqs[1]) * scale).astype(BF16)
        for r in range(3):
            rows = pl.ds(pl.multiple_of(_earlier(r, i) * QB, QB), QB)
            dk_acc[rows, :] += jnp.where(first, dks[0][r], dks[1][r])
            dv_acc[rows, :] += jnp.where(first, dvs[0][r], dvs[1][r])

        @pl.when(i == nb - 1)
        def _():
            dk_ref[...] = (dk_acc[...] * scale).astype(BF16)
            dv_ref[...] = dv_acc[...].astype(BF16)

    st = pl.BlockSpec((PAIR, QB, 1), lambda h, i: (h, i, 0))
    tab = pl.BlockSpec((PAIR, QB, KW), lambda h, i: (h, 0, 0))
    own = pl.BlockSpec((QB, LANE), lambda h, i: (i, h))
    whole = pl.BlockSpec((t, LANE), lambda h, i: (0, h))
    return pl.pallas_call(
        body, name=name,
        out_shape=[_sds((t, nh * dh), BF16)] * 3 + [_sds((nh, QB, KW), F32)],
        grid=(nh // PAIR, nb),
        in_specs=[pl.BlockSpec((QB, LANE), lambda h, i: (i, cols["qa"] // LANE + h))]
        + _pair_specs(cols["ka"], _earlier) + _pair_specs(cols["va"], _earlier) + [tab, st, own],
        out_specs=[own, whole, whole, tab],
        scratch_shapes=[pltpu.VMEM((t, LANE), F32), pltpu.VMEM((t, LANE), F32)],
        compiler_params=_params("parallel", "arbitrary"),
    )(p, p, p, p, p, p, p, bias, lse, dy)


def _prefix_sums(x, strict):
    r = lax.broadcasted_iota(jnp.int32, (CHUNK, CHUNK), 0)
    c = lax.broadcasted_iota(jnp.int32, (CHUNK, CHUNK), 1)
    tri = jnp.where((c < r) if strict else (c <= r), 1.0, 0.0).astype(BF16)
    n = x.shape[1]
    hi = x.astype(BF16)
    rest = x - hi.astype(F32)
    mid = rest.astype(BF16)
    lo = (rest - mid.astype(F32)).astype(BF16)
    out = _dot(tri, jnp.concatenate([hi, mid, lo], axis=1))
    return out[:, :n] + out[:, n:2 * n] + out[:, 2 * n:]


def _gate(lr, wa, ba):
    y = _dot(lr, wa) + ba
    return (jnp.minimum(y, 0.0) - jnp.log(1.0 + jnp.exp(-jnp.abs(y)))) / GATE_TAU, y


def _decays(la):
    cum = _prefix_sums(la, strict=False)
    last = cum[CHUNK - 1:CHUNK, :]
    return jnp.exp(last - cum), jnp.exp(last)


def _gla_specs(cols, hk, hv, order):
    def at(start, width):
        return pl.BlockSpec((GB, width), lambda h, i: (order(i), start // width + h))
    return [at(cols["qb"], hk), at(cols["kb"], hk), at(cols["vb"], hv), at(cols["rb"], hv),
            pl.BlockSpec((GB, LANE), lambda h, i: (order(i), cols["lr"] // LANE))]


def gla_fwd(p, cols, wa, ba, gn, nh, hk, hv, name):
    t = p.shape[0]
    nc = t // CHUNK
    scale = hk ** -0.5
    per = GB // CHUNK

    def body(q_ref, k_ref, v_ref, r_ref, lr_ref, wa_ref, ba_ref, gn_ref, o_ref, y_ref, st_ref, state):
        @pl.when(pl.program_id(1) == 0)
        def _():
            state[...] = jnp.zeros_like(state)

        for c in range(per):
            rows = pl.ds(c * CHUNK, CHUNK)
            la, _ = _gate(lr_ref[rows, :], wa_ref[...], ba_ref[...])
            w, decay = _decays(la)
            kdec = (k_ref[rows, :].astype(F32) * w).astype(BF16)
            st = decay * state[...] + _dot_tn(v_ref[rows, :], kdec)
            state[...] = st
            st_ref[c] = st
            o = _dot_nt(q_ref[rows, :], st.astype(BF16)) * scale
            o_ref[rows, :] = o
            rinv = lax.rsqrt(jnp.mean(o * o, axis=-1, keepdims=True) + RMS_EPS)
            rv = r_ref[rows, :].astype(F32)
            y_ref[rows, :] = (o * rinv * gn_ref[...] * (rv * _sigmoid(rv))).astype(BF16)

    return pl.pallas_call(
        body, name=name,
        out_shape=[_sds((t, nh * hv), F32), _sds((t, nh * hv), BF16), _sds((nh, nc, hv, hk), F32)],
        grid=(nh, t // GB),
        in_specs=_gla_specs(cols, hk, hv, lambda i: i)
        + [pl.BlockSpec((LANE, hk), lambda h, i: (0, h)), pl.BlockSpec((1, hk), lambda h, i: (0, h)),
           pl.BlockSpec((1, hv), lambda h, i: (0, 0))],
        out_specs=[pl.BlockSpec((GB, hv), lambda h, i: (i, h)), pl.BlockSpec((GB, hv), lambda h, i: (i, h)),
                   pl.BlockSpec((None, per, hv, hk), lambda h, i: (h, i, 0, 0))],
        scratch_shapes=[pltpu.VMEM((hv, hk), F32)], compiler_params=_params("parallel", "arbitrary"),
    )(p, p, p, p, p, wa, ba, gn)


def gla_bwd(p, cols, wa, ba, gn, o, states, dy, nh, hk, hv, name):
    t = p.shape[0]
    nblk = t // GB
    scale = hk ** -0.5
    per = GB // CHUNK

    def rev(i):
        return nblk - 1 - i

    def body(q_ref, k_ref, v_ref, r_ref, lr_ref, wa_ref, ba_ref, gn_ref, o_ref, st_ref, sp_ref, dy_ref,
             dq_ref, dk_ref, dv_ref, dr_ref, dg_ref, dgn_ref, carry):
        h, i = pl.program_id(0), pl.program_id(1)

        @pl.when(i == 0)
        def _():
            carry[...] = jnp.zeros_like(carry)

        @pl.when((i == 0) & (h == 0))
        def _():
            dgn_ref[...] = jnp.zeros_like(dgn_ref)

        gnv = gn_ref[...]
        for c in reversed(range(per)):
            rows = pl.ds(c * CHUNK, CHUNK)
            rv = r_ref[rows, :].astype(F32)
            sg = _sigmoid(rv)
            dyv = dy_ref[rows, :].astype(F32)
            ov = o_ref[rows, :]
            rinv = lax.rsqrt(jnp.mean(ov * ov, axis=-1, keepdims=True) + RMS_EPS)
            dn = dyv * (rv * sg)
            dr_ref[rows, :] = (dyv * (ov * rinv * gnv) * (sg * (1.0 + rv * (1.0 - sg)))).astype(BF16)
            dgn_ref[...] += _colsum(dn * ov * rinv)
            dxh = dn * gnv
            do = rinv * dxh - ov * (rinv * rinv * rinv) * jnp.mean(dxh * ov, axis=-1, keepdims=True)
            dob = (do * scale).astype(BF16)
            qv, kv, vv = q_ref[rows, :], k_ref[rows, :], v_ref[rows, :]
            dq_ref[rows, :] = _dot(dob, st_ref[c].astype(BF16)).astype(BF16)
            dst = carry[...] + _dot_tn(dob, qv)
            if c > 0:
                prev = st_ref[c - 1]
            else:
                prev = jnp.where(i == nblk - 1, 0.0, sp_ref[0])
            ddecay = _colsum(dst * prev)
            la, y = _gate(lr_ref[rows, :], wa_ref[...], ba_ref[...])
            w, decay = _decays(la)
            kf = kv.astype(F32)
            kdec = (kf * w).astype(BF16)
            dstb = dst.astype(BF16)
            dkdec = _dot(vv, dstb)
            dv_ref[rows, :] = _dot_nt(kdec, dstb).astype(BF16)
            dk_ref[rows, :] = (dkdec * w).astype(BF16)
            e = dkdec * kf * w
            dla = _prefix_sums(e, strict=True) + ddecay * decay
            dg_ref[rows, :] = dla * (1.0 / GATE_TAU) * _sigmoid(-y)
            carry[...] = decay * dst

    per_head = lambda width: pl.BlockSpec((GB, width), lambda h, i: (rev(i), h))
    return pl.pallas_call(
        body, name=name,
        out_shape=[_sds((t, nh * hk), BF16), _sds((t, nh * hk), BF16), _sds((t, nh * hv), BF16),
                   _sds((t, nh * hv), BF16), _sds((t, nh * hk), F32), _sds((1, hv), F32)],
        grid=(nh, nblk),
        in_specs=_gla_specs(cols, hk, hv, rev)
        + [pl.BlockSpec((LANE, hk), lambda h, i: (0, h)), pl.BlockSpec((1, hk), lambda h, i: (0, h)),
           pl.BlockSpec((1, hv), lambda h, i: (0, 0)), per_head(hv),
           pl.BlockSpec((None, per, hv, hk), lambda h, i: (h, rev(i), 0, 0)),
           pl.BlockSpec((None, 1, hv, hk), lambda h, i: (h, jnp.maximum(rev(i) * per - 1, 0), 0, 0)),
           per_head(hv)],
        out_specs=[per_head(hk), per_head(hk), per_head(hv), per_head(hv), per_head(hk),
                   pl.BlockSpec((1, hv), lambda h, i: (0, 0))],
        scratch_shapes=[pltpu.VMEM((hv, hk), F32)], compiler_params=_params("arbitrary", "arbitrary"),
    )(p, p, p, p, p, wa, ba, gn, o, states, states, dy)


def gate_bwd(p, lr_col, dg, wa, name):
    t, kd = dg.shape
    tm = _row_tile(t, 512, 16)

    def body(lr_ref, dg_ref, wa_ref, dlr_ref, dwa_ref, dba_ref):
        @pl.when(pl.program_id(0) == 0)
        def _():
            dwa_ref[...] = jnp.zeros_like(dwa_ref)
            dba_ref[...] = jnp.zeros_like(dba_ref)

        g = dg_ref[...]
        gb = g.astype(BF16)
        dlr_ref[...] = _dot_nt(gb, wa_ref[...]).astype(BF16)
        dwa_ref[...] += _dot_tn(lr_ref[...], gb)
        dba_ref[...] += _colsum(g)

    return pl.pallas_call(
        body, name=name, out_shape=[_sds((t, LANE), BF16), _sds((LANE, kd), F32), _sds((1, kd), F32)],
        grid=(t // tm,),
        in_specs=[pl.BlockSpec((tm, LANE), lambda i: (i, lr_col // LANE)), pl.BlockSpec((tm, kd), lambda i: (i, 0)),
                  pl.BlockSpec((LANE, kd), lambda i: (0, 0))],
        out_specs=[pl.BlockSpec((tm, LANE), lambda i: (i, 0)), pl.BlockSpec((LANE, kd), lambda i: (0, 0)),
                   pl.BlockSpec((1, kd), lambda i: (0, 0))],
        compiler_params=_params("arbitrary"),
    )(p, dg, wa)


def proj_merge(ya, yb, wa3, wb3, p, ga_col, gb_col, name):
    t, kd = ya.shape
    nb, _, bw = wa3.shape
    tm = _row_tile(t, 1024, 16)

    def body(ya_ref, yb_ref, wa_ref, wb_ref, ga_ref, gb_ref, pa_ref, pb_ref, mg_ref):
        pa = _dot(ya_ref[...], wa_ref[...])
        pb = _dot(yb_ref[...], wb_ref[...])
        pa_ref[...] = pa.astype(BF16)
        pb_ref[...] = pb.astype(BF16)
        mg_ref[...] = (_sigmoid(ga_ref[...].astype(F32)) * pa + _sigmoid(gb_ref[...].astype(F32)) * pb).astype(BF16)

    act = pl.BlockSpec((tm, kd), lambda j, i: (i, 0))
    wsp = pl.BlockSpec((None, kd, bw), lambda j, i: (j, 0, 0))
    out = pl.BlockSpec((tm, bw), lambda j, i: (i, j))
    return pl.pallas_call(
        body, name=name, out_shape=[_sds((t, nb * bw), BF16)] * 3, grid=(nb, t // tm),
        in_specs=[act, act, wsp, wsp, pl.BlockSpec((tm, bw), lambda j, i: (i, ga_col // bw + j)),
                  pl.BlockSpec((tm, bw), lambda j, i: (i, gb_col // bw + j))],
        out_specs=[out] * 3, compiler_params=_params("parallel", "parallel"),
    )(ya, yb, wa3, wb3, p, p)


def merge_bwd(dm, w, p, ga_col, gb_col, pa, pb, name):
    t, d = dm.shape
    n = w.shape[0]
    tn = _row_tile(n, 512, LANE)
    tm = _row_tile(t, 512, 16)

    def body(dm_ref, w_ref, ga_ref, gb_ref, pa_ref, pb_ref, dpa_ref, dpb_ref, dga_ref, dgb_ref):
        dmg = _dot_nt(dm_ref[...], w_ref[...])
        sa = _sigmoid(ga_ref[...].astype(F32))
        sb = _sigmoid(gb_ref[...].astype(F32))
        dpa_ref[...] = (dmg * sa).astype(BF16)
        dpb_ref[...] = (dmg * sb).astype(BF16)
        dga_ref[...] = (dmg * pa_ref[...].astype(F32) * sa * (1.0 - sa)).astype(BF16)
        dgb_ref[...] = (dmg * pb_ref[...].astype(F32) * sb * (1.0 - sb)).astype(BF16)

    out = pl.BlockSpec((tm, tn), lambda j, i: (i, j))
    return pl.pallas_call(
        body, name=name, out_shape=[_sds((t, n), BF16)] * 4, grid=(n // tn, t // tm),
        in_specs=[pl.BlockSpec((tm, d), lambda j, i: (i, 0)), pl.BlockSpec((tn, d), lambda j, i: (j, 0)),
                  pl.BlockSpec((tm, tn), lambda j, i: (i, ga_col // tn + j)),
                  pl.BlockSpec((tm, tn), lambda j, i: (i, gb_col // tn + j)), out, out],
        out_specs=[out] * 4, compiler_params=_params("parallel", "parallel"),
    )(dm, w, p, p, pa, pb)


def rel_bias_grad(skew, clip_map, name):
    nh, _, jd = skew.shape
    n_rel = clip_map.shape[1]

    def body(s_ref, c_ref, o_ref):
        sums = jnp.concatenate([_colsum(s_ref[h]) for h in range(nh)], axis=0)
        o_ref[...] = jnp.dot(sums, c_ref[...], preferred_element_type=F32, precision=HI)

    return pl.pallas_call(
        body, name=name, out_shape=_sds((nh, n_rel), F32), in_specs=[VMEM_SPEC, VMEM_SPEC], out_specs=VMEM_SPEC,
        compiler_params=pltpu.CompilerParams(vmem_limit_bytes=VMEM_LIMIT),
    )(skew, clip_map)


MIX_BLOCK = 9 * LANE


def mix_layout(d, a_width, bk, bv):
    main = 3 * a_width + 2 * bk + 2 * bv
    cols = {"qa": 0, "ka": a_width, "va": 2 * a_width, "qb": 3 * a_width, "kb": 3 * a_width + bk,
            "vb": 3 * a_width + 2 * bk, "rb": 3 * a_width + 2 * bk + bv, "ga": main, "gb": main + d,
            "lr": main + 2 * d}
    total = main + 2 * d + LANE
    assert total % MIX_BLOCK == 0
    return cols, main, total


def _mix_pieces(per, main, rank, d):
    out = []
    for lo, hi in ((0, main), (main + rank, main + rank + 2 * d), (main, main + rank)):
        while lo < hi:
            cut = min(hi, (lo // per + 1) * per)
            out.append((lo, cut))
            lo = cut
    return out


def mix_weight_in(g3, main, rank):
    d = g3.shape[2]
    flat = g3.reshape(-1, d)
    return jnp.concatenate([flat[:main], flat[main + rank:], flat[main:main + rank],
                            jnp.zeros((LANE - rank, d), g3.dtype)], axis=0)


def mix_weight_grad_out(gt, main, rank, per):
    d = gt.shape[1]
    blocks = [[] for _ in range(N_DEV)]
    pos = 0
    for lo, hi in _mix_pieces(per, main, rank, d):
        blocks[lo // per].append((lo, gt[pos:pos + hi - lo]))
        pos += hi - lo
    return jnp.stack([jnp.concatenate([x for _, x in sorted(b, key=lambda e: e[0])], axis=0) for b in blocks])


def ffn_forward(h, sh, sc, g, w_in3, w_out_of, ln_g, ln_b, tag):
    a, b, s = ffn_in(h, sh, sc, w_in3, f"{tag}_in")
    w_out = w_out_of(s)
    f, z, hout = out_ln(s, w_out, h, g, ln_g, ln_b, 0.5, f"{tag}_out")
    return hout, (h, a, b, s, f, z), w_out


def ffn_backward_weights(dh, saved, sh, sc, g, w_in3, w_out, ln_g, tag, target=None):
    hin, a, b, s, f, z = saved
    t, d = hin.shape
    nb, _, bw = w_in3.shape
    half = nb // 2
    fdim = w_out.shape[0]
    res = ln_bwd(dh, z, f, ln_g, g, 0.5, f"{tag}_ln_bwd", target=target)
    dz, df, dln_g, dln_b, dg = res[:5]
    dab = ffn_bwd_act(df, w_out, a, b, f"{tag}_act_bwd")
    tk = _row_tile(t, 512, 16)
    dw_out = matmul_tn(f"{tag}_dwout", s, (tk, bw), lambda n, k: (k, n), df, (tk, d), lambda n, k: (k, 0),
                       _sds((fdim, d), BF16), (bw, d), lambda n, k: (n, 0), fdim // bw)
    dw_in = matmul_tn(f"{tag}_dwin", hin, (tk, d), lambda n, k: (k, 0), dab, (None, tk, bw),
                      lambda n, k: (n // half, k, n % half), _sds((nb, d, bw), BF16), (None, d, bw),
                      lambda n, k: (n, 0, 0), nb, mod=(sh, sc))
    grads = dict(w_in=dw_in, w_out=dw_out.reshape(N_DEV, fdim // N_DEV, d), ln_g=dln_g, ln_b=dln_b, g=dg)
    return (dab, dz), grads, (res[5] if target is not None else None)


def ffn_backward_input(carry, saved, sc, w_in3, tag, after=None):
    dab, dz = carry
    hin = saved[0]
    t, d = hin.shape
    nb, _, bw = w_in3.shape
    half = nb // 2
    fdim = half * bw
    tm = _row_tile(t, 256, 16)

    def contract(dy_ref, w_ref):
        return sum(_dot_nt(dy_ref[:, j * bw:(j + 1) * bw], w_ref[j]) for j in range(half))

    order = [] if after is None else [after]

    def first(dy_ref, w_ref, *refs):
        refs[-1][...] = contract(dy_ref, w_ref)

    def second(dy_ref, w_ref, part_ref, dz_ref, hin_ref, sc_ref, o_ref, dsc_ref, dsh_ref):
        @pl.when(pl.program_id(0) == 0)
        def _():
            dsc_ref[...] = jnp.zeros_like(dsc_ref)
            dsh_ref[...] = jnp.zeros_like(dsh_ref)

        du = part_ref[...] + contract(dy_ref, w_ref)
        o_ref[...] = ALPHA * dz_ref[...] + du * (1.0 + sc_ref[...])
        dsc_ref[...] += _colsum(du * hin_ref[...])
        dsh_ref[...] += _colsum(du)

    def specs(which):
        return [pl.BlockSpec((None, tm, fdim), lambda i: (which, i, 0)),
                pl.BlockSpec((half, d, bw), lambda i: (which, 0, 0), pipeline_mode=pl.Buffered(1))]

    row = pl.BlockSpec((tm, d), lambda i: (i, 0))
    vec = pl.BlockSpec((1, d), lambda i: (0, 0))
    part = pl.pallas_call(
        first, name=f"{tag}_du_a", out_shape=_sds((t, d), F32), grid=(t // tm,),
        in_specs=specs(0) + [ANY] * len(order), out_specs=row, compiler_params=_params("parallel"),
    )(dab, w_in3, *order)
    return pl.pallas_call(
        second, name=f"{tag}_du_b", out_shape=[_sds((t, d), F32), _sds((1, d), F32), _sds((1, d), F32)],
        grid=(t // tm,), in_specs=specs(1) + [row, row, row, vec], out_specs=[row, vec, vec],
        compiler_params=_params("arbitrary"),
    )(dab, w_in3, part, dz, hin, sc)


def _after(v, token):
    return v if token is None else v + token[:1, :1]


def local_step(x, target, mod, weights_of, grads_ready, grads_sent, rel_bias, w_alpha2, b_alpha, gla_norm_g, lns,
               bias=None, weights_early=None):
    t, d = x.shape
    sh1, sc1, g1, sh2, sc2, g2, sh3, sc3, g3 = [mod[i:i + 1] for i in range(N_MOD)]
    ln1_g, ln1_b, ln2_g, ln2_b, ln3_g, ln3_b = lns
    n_heads_a, n_rel = rel_bias.shape
    rank, bk = w_alpha2.shape
    hv = gla_norm_g.shape[1]

    w1 = weights_of("ffn1_in", mod)
    h1, saved1, w1["out"] = ffn_forward(x, sh1, sc1, g1, w1["in"], lambda s: weights_of("ffn1_out", s)["out"],
                                        ln1_g, ln1_b, "ffn1")
    wm = weights_of("mix", h1)
    a_width = wm["proj_a"].shape[1]
    bv = wm["proj_b"].shape[1]
    nh_b = bv // hv
    hk = bk // nh_b
    cols, main, total = mix_layout(d, a_width, bk, bv)
    w_mix = mix_weight_in(wm["in_t"], main, rank)
    p = mod_matmul(h1, sh2, sc2, w_mix, MIX_BLOCK, "mix_in")
    bias = bias_table(rel_bias) if bias is None else bias
    dh = a_width // n_heads_a
    assert PAIR * dh == LANE
    ya, lse = attn_fwd(p, cols, bias, dh, "attn_fwd")
    token = None if weights_early is None else weights_early("ffn2", ya)
    b_alpha = _after(b_alpha, token)
    wa_pad = jnp.zeros((LANE, bk), BF16).at[:rank].set(w_alpha2.astype(BF16))
    o_b, yb, states = gla_fwd(p, cols, wa_pad, b_alpha, gla_norm_g, nh_b, hk, hv, "gla_fwd")
    pa, pb, merged = proj_merge(ya, yb, wm["proj_a"], wm["proj_b"], p, cols["ga"], cols["gb"], "proj_merge")
    m, z2, h2 = out_ln(merged, wm["out"], h1, g2, ln2_g, ln2_b, 1.0, "mix_out")
    w3 = weights_of("ffn2", h2)
    h3, saved3, _ = ffn_forward(h2, sh3, sc3, g3, w3["in"], lambda s: w3["out"], ln3_g, ln3_b, "ffn2")

    carry3, gr3, loss = ffn_backward_weights(h3, saved3, sh3, sc3, g3, w3["in"], w3["out"], ln3_g, "ffn2",
                                             target=target)
    token = grads_ready("ffn2", dict(ffn2_in=gr3["w_in"], ffn2_out=gr3["w_out"]))
    dh2, dsc3, dsh3 = ffn_backward_input(carry3, saved3, sc3, w3["in"], "ffn2", after=token)
    token = grads_sent("ffn2", dh2)
    dz2, dm, dln2_g, dln2_b, dg2 = ln_bwd(dh2, z2, m, _after(ln2_g, token), g2, 1.0, "mix_ln_bwd")
    dpa, dpb, dga, dgb = merge_bwd(dm, wm["out"], p, cols["ga"], cols["gb"], pa, pb, "merge_bwd")
    tk = _row_tile(t, 512, 16)
    dw_mix_out = matmul_tn("mix_dwout", merged, (tk, 512), lambda n, k: (k, n), dm, (tk, d), lambda n, k: (k, 0),
                           _sds((d, d), BF16), (512, d), lambda n, k: (n, 0), d // 512)
    tm = _row_tile(t, 512, 16)
    pbw = wm["proj_a"].shape[2]
    dya = matmul_nt_blocks("proj_a_dy", dpa, (tm, pbw), lambda i, k: (i, k), wm["proj_a"], t)
    dyb = matmul_nt_blocks("proj_b_dy", dpb, (tm, pbw), lambda i, k: (i, k), wm["proj_b"], t)
    dw_pa = matmul_tn("proj_a_dw", ya, (tk, a_width), lambda n, k: (k, 0), dpa, (tk, pbw), lambda n, k: (k, n),
                      _sds((N_DEV, a_width, pbw), BF16), (None, a_width, pbw), lambda n, k: (n, 0, 0), N_DEV)
    dw_pb = matmul_tn("proj_b_dw", yb, (tk, bv), lambda n, k: (k, 0), dpb, (tk, pbw), lambda n, k: (k, n),
                      _sds((N_DEV, bv, pbw), BF16), (None, bv, pbw), lambda n, k: (n, 0, 0), N_DEV)
    dqb, dkb, dvb, drb, dgate, dgn = gla_bwd(p, cols, wa_pad, b_alpha, gla_norm_g, o_b, states, dyb,
                                             nh_b, hk, hv, "gla_bwd")
    dlr, dwa_pad, dba = gate_bwd(p, cols["lr"], dgate, wa_pad, "gate_bwd")
    dqa, dka, dva, dbias = attn_bwd(p, cols, bias, lse, dya, dh, "attn_bwd")
    d_rel = rel_bias_grad(bias_grad_skew(dbias), jnp.asarray(bias_clip_map(n_rel)), "rel_bias_grad")
    dp = jnp.concatenate([dqa, dka, dva, dqb, dkb, dvb, drb,
                          dga, dgb, dlr], axis=1)
    dw_mix_t = matmul_tn("mix_dwin", dp, (tk, MIX_BLOCK), lambda n, k: (k, n), h1, (tk, d), lambda n, k: (k, 0),
                         _sds((total, d), BF16), (MIX_BLOCK, d), lambda n, k: (n, 0), total // MIX_BLOCK,
                         mod=(sh2, sc2), mod_b=True)
    dw_mix_in = mix_weight_grad_out(dw_mix_t, main, rank, wm["in_t"].shape[1])
    token = grads_ready("mix", dict(mix_in=dw_mix_in, proj_a=dw_pa, proj_b=dw_pb,
                                    mix_out=dw_mix_out.reshape(N_DEV, d // N_DEV, d)))
    tm = _row_tile(t, 512, 16)
    dh1, dsc2, dsh2 = matmul_nt_blocks("mix_du", dp, (tm, MIX_BLOCK), lambda i, k: (i, k), w_mix, t,
                                       resid=(dz2, h1, _after(sc2, token)))
    token = grads_sent("mix", dh1)
    carry1, gr1, _ = ffn_backward_weights(dh1, saved1, sh1, sc1, g1, w1["in"], w1["out"], _after(ln1_g, token),
                                          "ffn1")
    token = grads_ready("ffn1", dict(ffn1_in=gr1["w_in"], ffn1_out=gr1["w_out"]))
    dx, dsc1, dsh1 = ffn_backward_input(carry1, saved1, sc1, w1["in"], "ffn1", after=token)

    dmod = [dsh1, dsc1, gr1["g"], dsh2, dsc2, dg2, dsh3, dsc3, gr3["g"]]
    small = dict(ln1_g=gr1["ln_g"], ln1_b=gr1["ln_b"], ln2_g=dln2_g, ln2_b=dln2_b, ln3_g=gr3["ln_g"],
                 ln3_b=gr3["ln_b"], b_alpha=dba, gla_norm_g=dgn, w_alpha2=dwa_pad[:rank], rel_bias=d_rel)
    return loss, dx, dmod, small


GROUPS = dict(ffn1=("ffn1_in", "ffn1_out"), mix=("mix_in", "proj_a", "proj_b", "mix_out"),
              ffn2=("ffn2_in", "ffn2_out"))
GATHERS = dict(ffn1_in=("ffn1_in",), ffn1_out=("ffn1_out",), mix=GROUPS["mix"], ffn2=GROUPS["ffn2"])
SMALL_REPLICATED = ("b_ada", "ln1_g", "ln1_b", "ln2_g", "ln2_b", "ln3_g", "ln3_b", "b_alpha", "gla_norm_g")
SMALL_SHARDED = ("rel_bias", "w_alpha2")
WEIGHT_ORDER = ("w_ada", "b_ada", "ffn1_w_in", "ffn1_w_out", "ln1_g", "ln1_b", "w_mix_in", "rel_bias", "w_alpha2",
                "b_alpha", "gla_norm_g", "w_proj_a", "w_proj_b", "w_mix_out", "ln2_g", "ln2_b", "ffn2_w_in",
                "ffn2_w_out", "ln3_g", "ln3_b")
BIG_NAME = dict(ffn1_in="ffn1_w_in", ffn1_out="ffn1_w_out", mix_in="w_mix_in", proj_a="w_proj_a",
                proj_b="w_proj_b", mix_out="w_mix_out", ffn2_in="ffn2_w_in", ffn2_out="ffn2_w_out")


def kernel(x, c, w_ada, b_ada, ffn1_w_in, ffn1_w_out, ln1_g, ln1_b, w_mix_in, rel_bias, w_alpha2, b_alpha, gla_norm_g, w_proj_a, w_proj_b, w_mix_out, ln2_g, ln2_b, ffn2_w_in, ffn2_w_out, ln3_g, ln3_b, loss_target, m_w_ada, m_b_ada, m_ffn1_w_in, m_ffn1_w_out, m_ln1_g, m_ln1_b, m_w_mix_in, m_rel_bias, m_w_alpha2, m_b_alpha, m_gla_norm_g, m_w_proj_a, m_w_proj_b, m_w_mix_out, m_ln2_g, m_ln2_b, m_ffn2_w_in, m_ffn2_w_out, m_ln3_g, m_ln3_b, v_w_ada, v_b_ada, v_ffn1_w_in, v_ffn1_w_out, v_ln1_g, v_ln1_b, v_w_mix_in, v_rel_bias, v_w_alpha2, v_b_alpha, v_gla_norm_g, v_w_proj_a, v_w_proj_b, v_w_mix_out, v_ln2_g, v_ln2_b, v_ffn2_w_in, v_ffn2_w_out, v_ln3_g, v_ln3_b):
    env = dict(locals())
    w = {n: env[n] for n in WEIGHT_ORDER}
    mom = {n: env["m_" + n] for n in WEIGHT_ORDER}
    var = {n: env["v_" + n] for n in WEIGHT_ORDER}
    me = _me()
    dev = _lin(me)
    core = jnp.reshape(me[2], (1,)).astype(jnp.int32)
    chip = jnp.reshape(2 * me[0] + me[1], (1,)).astype(jnp.int32)
    d = x.shape[-1]

    def shard(n):
        s = w[BIG_NAME[n]][0].astype(BF16)
        return s.T if n == "mix_in" else s

    dev_idx = jnp.reshape(dev, (1,)).astype(jnp.int32)
    started = {}

    def start(grp, after):
        shards = [shard(n) for n in GATHERS[grp]]
        lands = [place_own(dev_idx, s, f"place_own_{n}") for n, s in zip(GATHERS[grp], shards)]
        started[grp] = gather_start(shards, lands, after, f"gather_start_{grp}")
        return started[grp][-1]

    small_w = all_gather_small(jnp.concatenate([rel_bias[0], w_alpha2[0]], axis=1), "gather_small_w")
    n_rel_cols = rel_bias.shape[-1]
    rel_full = small_w[:, :, :n_rel_cols].transpose(1, 0, 2).reshape(small_w.shape[1], -1)
    wa2_full = small_w[:, :, n_rel_cols:].transpose(1, 0, 2).reshape(small_w.shape[1], -1)

    first, *rest = list(GATHERS)
    order = start(first, small_w[0, :1, :1])
    order, w, mom, var = lax.optimization_barrier((order, w, mom, var))

    ada_cols = w_ada.shape[-1]
    bias = bias_table(rel_full)
    c_all = all_gather_small(c, "gather_c", after=(order, bias))[:, 0, :]
    b_cols = lax.dynamic_slice_in_dim(b_ada, dev * ada_cols, ada_cols, axis=1)
    mod_cols = adaln_cols(c_all, w_ada[0], b_cols, "adaln_cols")
    mod_all = all_gather_small(mod_cols, "gather_mod")
    mod = lax.dynamic_index_in_dim(mod_all, dev, axis=1, keepdims=False).reshape(N_MOD, d)

    order = mod_all[0, :1, :1]
    for grp in rest:
        order = start(grp, order)
    mod = _after(mod, order)

    forwards = {}

    def weights_early(grp, after):
        _, zones = gather_wait(started[grp], after, f"gather_wait_{grp}")
        forwards[grp] = forward_start(zones, f"forward_start_{grp}")
        return forwards[grp][-1]

    def weights_of(grp, after):
        if grp in forwards:
            arrays = forward_wait(forwards[grp], after, f"forward_wait_{grp}")
        else:
            _, zones = gather_wait(started[grp], after, f"gather_wait_{grp}")
            arrays = gather_forward(zones, f"gather_forward_{grp}")
        full = dict(zip(GATHERS[grp], arrays))
        if grp == "mix":
            return dict(in_t=full["mix_in"], proj_a=full["proj_a"], proj_b=full["proj_b"],
                        out=full["mix_out"].reshape(-1, d))
        return {"in" if n.endswith("_in") else "out": v if n.endswith("_in") else v.reshape(-1, d)
                for n, v in full.items()}

    pairs, exchanges = {}, {}
    last = list(GROUPS)[0]

    def to_chips(grp, grads, got):
        sums = [pair_add(core, grads[n], g, f"grad_pair_add_{n}") for n, g in zip(GROUPS[grp], got)]
        exchanges[grp] = exchange_start(sums, chip_routes, f"grad_chip_start_{grp}")
        return exchanges[grp][-1]

    def grads_ready(grp, grads):
        if grp == last:
            return to_chips(grp, grads, pair_exchange([grads[n] for n in GROUPS[grp]], f"grad_pair_exchange_{grp}"))
        pairs[grp] = exchange_start([grads[n] for n in GROUPS[grp]], pair_routes, f"grad_pair_start_{grp}")
        return pairs[grp][-1]

    def grads_sent(grp, after):
        if grp == last:
            return None
        sent, got = exchange_wait(pairs[grp], pair_routes, after, f"grad_pair_wait_{grp}")
        return to_chips(grp, dict(zip(GROUPS[grp], sent)), got)

    lns = [ln1_g, ln1_b, ln2_g, ln2_b, ln3_g, ln3_b]
    loss, dx, dmod, small = local_step(x[0], loss_target[0], mod, weights_of, grads_ready, grads_sent, rel_full,
                                       wa2_full, b_alpha, gla_norm_g, lns, bias=bias, weights_early=weights_early)

    out = {}

    def finish(grp, after):
        sums, recv = exchange_wait(exchanges[grp], chip_routes, after, f"grad_chip_wait_{grp}")
        for n, hsum, r in zip(GROUPS[grp], sums, recv):
            full = BIG_NAME[n]
            if n == "mix_in":
                g = owned_sum(chip, hsum, r, f"owned_sum_{n}").T
                res_n = adamw_sum(g[None], w[full][0], mom[full][0], var[full][0], f"adamw_{n}")
            else:
                res_n = adamw_owned(chip, hsum, r, w[full][0], mom[full][0], var[full][0], f"adamw_{n}")
            out[full] = [o[None] for o in res_n]
            after = res_n[0]
        return after

    order = dx
    for grp in reversed(list(GROUPS)[1:]):
        order = finish(grp, order)

    pieces = (list(dmod) + [small[n].reshape(1, -1) for n in SMALL_REPLICATED[1:] + SMALL_SHARDED] + [loss])
    parts = all_gather_rows(pieces, order, "gather_small_grads")
    loss = jnp.sum(parts[:, 0, parts.shape[2] - loss.shape[1]])
    n_mod = N_MOD * d
    dmod_all = parts[:, 0, :n_mod]
    g_w_ada = adaln_wgrad(c_all, lax.dynamic_slice_in_dim(dmod_all, dev * ada_cols, ada_cols, axis=1), "adaln_wgrad")
    out["w_ada"] = [o[None] for o in adamw_sum(g_w_ada[None], w_ada[0], m_w_ada[0], v_w_ada[0], "adamw_w_ada")]

    sources, where, off = [parts], [], 0
    for n in SMALL_REPLICATED:
        where.append((0, off))
        off += w[n].size
    for n in SMALL_SHARDED:
        rows, cols_local = w[n].shape[1], w[n].shape[2]
        full_part = parts[:, 0, off:off + rows * cols_local * N_DEV].reshape(N_DEV, rows, cols_local * N_DEV)
        mine = lax.dynamic_slice_in_dim(full_part, dev * cols_local, cols_local, axis=2)
        where.append((len(sources), 0))
        sources.append(mine.reshape(N_DEV, 1, rows * cols_local))
        off += rows * cols_local * N_DEV
    names = SMALL_REPLICATED + SMALL_SHARDED
    res = adamw_rows(sources, where, *[[src[n].reshape(1, -1) for n in names] for src in (w, mom, var)],
                     "adamw_small")
    for n, res_n in zip(names, res):
        out[n] = [r.reshape(w[n].shape) for r in res_n]

    finish(last, res[0][0])

    flat = [loss, dx[None]]
    for k in range(4):
        flat += [out[n][k] for n in WEIGHT_ORDER]
    return tuple(flat)
```

```python
import functools

import numpy as np
import jax
import jax.numpy as jnp
from jax import lax
from jax.experimental import pallas as pl
from jax.experimental.pallas import tpu as pltpu

F32 = jnp.float32
BF16 = jnp.bfloat16
MESH = pl.DeviceIdType.MESH
N_DEV = 8
N_CHIP = 4

CHUNK = 64
A_PAST_CHUNKS = 8
REL_CLIP = 256
GATE_TAU = 16.0
N_MOD = 9
DEPTH = 1
ALPHA = (2.0 * DEPTH) ** 0.25
LN_EPS = 1e-5
RMS_EPS = 1e-6
ADAM_LR = 0.001
ADAM_B1 = 0.9
ADAM_B2 = 0.999
ADAM_EPS = 1e-08
ADAM_WD = 0.01
ADAM_STEP = 10

LANE = 128
VMEM_LIMIT = 56 * 2 ** 20
QB = 4 * CHUNK
KW = 3 * QB
GB = 8 * CHUNK
NEG = -1e30
HI = lax.Precision.HIGHEST

ANY = pl.BlockSpec(memory_space=pl.ANY)
VMEM_SPEC = pl.BlockSpec(memory_space=pltpu.VMEM)


def _params(*sem):
    return pltpu.CompilerParams(dimension_semantics=sem, vmem_limit_bytes=VMEM_LIMIT)


def _sds(shape, dtype):
    return jax.ShapeDtypeStruct(shape, dtype)


def _dot(a, b):
    return jnp.dot(a, b, preferred_element_type=F32)


def _dot_nt(a, b):
    return lax.dot_general(a, b, (((1,), (1,)), ((), ())), preferred_element_type=F32)


def _dot_tn(a, b):
    return lax.dot_general(a, b, (((0,), (0,)), ((), ())), preferred_element_type=F32)


def _sigmoid(x):
    return 0.5 * jnp.tanh(0.5 * x) + 0.5


def _colsum(x):
    return jnp.sum(x, axis=0, keepdims=True)


def _row_tile(rows, cap, mult):
    for t in range(min(rows, cap), 0, -1):
        if rows % t == 0 and t % mult == 0:
            return t
    return rows


def _me():
    return lax.axis_index("x"), lax.axis_index("y"), lax.axis_index("c")


def _flip(me, k):
    return tuple((1 - p) if (k >> s) & 1 else p for p, s in zip(me, (2, 1, 0)))


def _lin(p):
    return 4 * p[0] + 2 * p[1] + p[2]


def _gather_direct(x_ref, out_ref, send_sems, recv_sems, local_sem):
    me = _me()
    mine = pltpu.make_async_copy(x_ref, out_ref.at[_lin(me)], local_sem)
    mine.start()
    sends = []
    for k in range(1, N_DEV):
        cp = pltpu.make_async_remote_copy(
            src_ref=x_ref, dst_ref=out_ref.at[_lin(me)], send_sem=send_sems.at[k - 1],
            recv_sem=recv_sems.at[k - 1], device_id=_flip(me, k), device_id_type=MESH)
        cp.start()
        sends.append(cp)
    for k in range(1, N_DEV):
        peer = _flip(me, k)
        pltpu.make_async_remote_copy(
            src_ref=x_ref, dst_ref=out_ref.at[_lin(peer)], send_sem=send_sems.at[k - 1],
            recv_sem=recv_sems.at[k - 1], device_id=peer, device_id_type=MESH).wait_recv()
    for cp in sends:
        cp.wait_send()
    mine.wait()


GATHER_SEMS = [pltpu.SemaphoreType.DMA((N_DEV - 1,)), pltpu.SemaphoreType.DMA((N_DEV - 1,)), pltpu.SemaphoreType.DMA]


def all_gather_small(x, name, after=()):
    r, n = x.shape

    def body(x_ref, *refs):
        _gather_direct(x_ref, *refs[len(after):])

    return pl.pallas_call(
        body, name=name, out_shape=_sds((N_DEV, r, n), x.dtype),
        in_specs=[VMEM_SPEC] + [ANY] * len(after), out_specs=VMEM_SPEC, scratch_shapes=GATHER_SEMS,
    )(x, *after)


def all_gather_rows(pieces, after, name):
    sizes = [x.shape[1] for x in pieces]
    total = sum(sizes)
    assert all(n % LANE == 0 for n in sizes)

    def body(*refs):
        ins = refs[:len(pieces)]
        out_ref, row, send_sems, recv_sems, local_sem = refs[len(pieces) + 1:]
        off = 0
        for x_ref, n in zip(ins, sizes):
            row[:, off:off + n] = x_ref[...]
            off += n
        _gather_direct(row, out_ref, send_sems, recv_sems, local_sem)

    return pl.pallas_call(
        body, name=name, out_shape=_sds((N_DEV, 1, total), F32),
        in_specs=[VMEM_SPEC] * len(pieces) + [ANY], out_specs=VMEM_SPEC,
        scratch_shapes=[pltpu.VMEM((1, total), F32)] + GATHER_SEMS,
    )(*pieces, after)


HBM_SPEC = pl.BlockSpec(memory_space=pltpu.HBM)
SEM_SPEC = pl.BlockSpec(memory_space=pltpu.SEMAPHORE)
EFFECT = pltpu.SideEffectType.DATAFLOW_SIDE_EFFECTING
FIRST = N_CHIP


def _hbm(v):
    return pltpu.with_memory_space_constraint(v, pltpu.HBM)


def _other_chips(mx, my):
    return [(1 - mx, my), (mx, 1 - my), (1 - mx, 1 - my)]


def place_own(dev, shard, name):
    rows, cols = shard.shape
    tr, tc = _tile2(rows, cols, 16)

    def body(dev_ref, s_ref, land_ref, o_ref):
        o_ref[...] = s_ref[...]

    land = lax.empty((N_DEV, rows, cols), shard.dtype)
    return pl.pallas_call(
        body, name=name, out_shape=_sds(land.shape, land.dtype),
        grid_spec=pltpu.PrefetchScalarGridSpec(
            num_scalar_prefetch=1, grid=(rows // tr, cols // tc),
            in_specs=[pl.BlockSpec((tr, tc), lambda i, j, d: (i, j)), ANY],
            out_specs=pl.BlockSpec((None, tr, tc), lambda i, j, d: (d[0], i, j))),
        input_output_aliases={2: 0}, compiler_params=_params("parallel", "parallel"),
    )(dev, shard, land)


def gather_start(shards, lands, after, name):
    n = len(shards)

    def body(*refs):
        ins, zones = refs[:n], refs[n:2 * n]
        send_sems, recv_sems = refs[2 * n + 1], refs[2 * n + 2]
        token = refs[-1]
        me = _me()
        mx, my, mc = me
        for a in range(n):
            dst = zones[a].at[_lin(me)]
            targets = [(mx, my, 1 - mc)] + [(*chip, mc) for chip in _other_chips(mx, my)]
            for k, to in enumerate(targets):
                pltpu.make_async_remote_copy(
                    src_ref=ins[a], dst_ref=dst, send_sem=send_sems.at[a * FIRST + k],
                    recv_sem=recv_sems.at[a * FIRST + k], device_id=to, device_id_type=MESH).start()
        token[...] = jnp.zeros_like(token)

    sems = pltpu.SemaphoreType.DMA((n * FIRST,))
    out = pl.pallas_call(
        body, name=name,
        out_shape=(sems, sems, *[pltpu.HBM(s.shape, s.dtype) for s in shards],
                   *[pltpu.HBM(z.shape, z.dtype) for z in lands], _sds((8, LANE), F32)),
        in_specs=[HBM_SPEC] * (2 * n) + [ANY],
        out_specs=(SEM_SPEC, SEM_SPEC, *[HBM_SPEC] * (2 * n), VMEM_SPEC),
        input_output_aliases={a: 2 + a for a in range(2 * n)},
        compiler_params=pltpu.CompilerParams(has_side_effects=EFFECT),
    )(*[_hbm(s) for s in shards], *[_hbm(z) for z in lands], after)
    return out[0], out[1], out[2:2 + n], out[2 + n:2 + 2 * n], out[-1]


def gather_wait(started, after, name):
    send_sems, recv_sems, shards, lands, _ = started
    n = len(shards)

    def body(*refs):
        ins, zones = refs[:n], refs[n:2 * n]
        send_ref, recv_ref = refs[2 * n], refs[2 * n + 1]
        mx, my, mc = _me()
        for a in range(n):
            for k in range(FIRST):
                cp = pltpu.make_async_remote_copy(
                    src_ref=ins[a], dst_ref=zones[a].at[0], send_sem=send_ref.at[a * FIRST + k],
                    recv_sem=recv_ref.at[a * FIRST + k], device_id=(mx, my, 1 - mc), device_id_type=MESH)
                cp.wait_send()
                cp.wait_recv()

    out = pl.pallas_call(
        body, name=name,
        out_shape=(*[pltpu.HBM(s.shape, s.dtype) for s in shards], *[pltpu.HBM(z.shape, z.dtype) for z in lands]),
        in_specs=[HBM_SPEC] * (2 * n) + [SEM_SPEC, SEM_SPEC, ANY], out_specs=tuple([HBM_SPEC] * (2 * n)),
        input_output_aliases={a: a for a in range(2 * n)},
        compiler_params=pltpu.CompilerParams(has_side_effects=EFFECT),
    )(*shards, *lands, send_sems, recv_sems, after)
    return out[:n], out[n:]


def gather_forward(lands, name):
    n = len(lands)
    rel = N_CHIP - 1

    def body(*refs):
        zones, outs = refs[:n], refs[n:2 * n]
        send_sems, recv_sems = refs[2 * n:]
        mx, my, mc = _me()
        chips = _other_chips(mx, my)

        def copy(a, j, core):
            blk = _lin((*chips[j], core))
            return pltpu.make_async_remote_copy(
                src_ref=zones[a].at[blk], dst_ref=outs[a].at[blk], send_sem=send_sems.at[a * rel + j],
                recv_sem=recv_sems.at[a * rel + j], device_id=(mx, my, 1 - mc), device_id_type=MESH)

        sends = [copy(a, j, mc) for a in range(n) for j in range(rel)]
        for cp in sends:
            cp.start()
        for a in range(n):
            for j in range(rel):
                copy(a, j, 1 - mc).wait_recv()
        for cp in sends:
            cp.wait_send()

    return pl.pallas_call(
        body, name=name, out_shape=[_sds(z.shape, z.dtype) for z in lands],
        in_specs=[ANY] * n, out_specs=[ANY] * n, input_output_aliases={a: a for a in range(n)},
        scratch_shapes=[pltpu.SemaphoreType.DMA((n * rel,)), pltpu.SemaphoreType.DMA((n * rel,))],
    )(*lands)


def forward_start(lands, name):
    n = len(lands)
    rel = N_CHIP - 1

    def body(*refs):
        zones = refs[:n]
        send_sems, recv_sems = refs[n], refs[n + 1]
        token = refs[-1]
        mx, my, mc = _me()
        for a in range(n):
            for j, chip in enumerate(_other_chips(mx, my)):
                blk = zones[a].at[_lin((*chip, mc))]
                pltpu.make_async_remote_copy(
                    src_ref=blk, dst_ref=blk, send_sem=send_sems.at[a * rel + j], recv_sem=recv_sems.at[a * rel + j],
                    device_id=(mx, my, 1 - mc), device_id_type=MESH).start()
        token[...] = jnp.zeros_like(token)

    sems = pltpu.SemaphoreType.DMA((n * rel,))
    out = pl.pallas_call(
        body, name=name,
        out_shape=(sems, sems, *[pltpu.HBM(z.shape, z.dtype) for z in lands], _sds((8, LANE), F32)),
        in_specs=[HBM_SPEC] * n, out_specs=(SEM_SPEC, SEM_SPEC, *[HBM_SPEC] * n, VMEM_SPEC),
        input_output_aliases={a: 2 + a for a in range(n)},
        compiler_params=pltpu.CompilerParams(has_side_effects=EFFECT),
    )(*[_hbm(z) for z in lands])
    return out[0], out[1], out[2:2 + n], out[-1]


def forward_wait(started, after, name):
    send_sems, recv_sems, lands, _ = started
    n = len(lands)
    rel = N_CHIP - 1

    def body(*refs):
        zones = refs[:n]
        send_ref, recv_ref = refs[n], refs[n + 1]
        mx, my, mc = _me()
        for a in range(n):
            for j, chip in enumerate(_other_chips(mx, my)):
                cp = pltpu.make_async_remote_copy(
                    src_ref=zones[a].at[_lin((*chip, mc))], dst_ref=zones[a].at[_lin((*chip, 1 - mc))],
                    send_sem=send_ref.at[a * rel + j], recv_sem=recv_ref.at[a * rel + j],
                    device_id=(mx, my, 1 - mc), device_id_type=MESH)
                cp.wait_send()
                cp.wait_recv()

    out = pl.pallas_call(
        body, name=name, out_shape=tuple(pltpu.HBM(z.shape, z.dtype) for z in lands),
        in_specs=[HBM_SPEC] * n + [SEM_SPEC, SEM_SPEC, ANY], out_specs=tuple([HBM_SPEC] * n),
        input_output_aliases={a: a for a in range(n)},
        compiler_params=pltpu.CompilerParams(has_side_effects=EFFECT),
    )(*lands, send_sems, recv_sems, after)
    return list(out)


def pair_exchange(gs, name):
    n = len(gs)

    def body(*refs):
        ins, outs = refs[:n], refs[n:2 * n]
        send_sems, recv_sems = refs[2 * n:]
        mx, my, mc = _me()
        cps = []
        for a in range(n):
            for q in range(N_CHIP):
                cp = pltpu.make_async_remote_copy(
                    src_ref=ins[a].at[2 * q + (1 - mc)], dst_ref=outs[a].at[q],
                    send_sem=send_sems.at[a * N_CHIP + q], recv_sem=recv_sems.at[a * N_CHIP + q],
                    device_id=(mx, my, 1 - mc), device_id_type=MESH)
                cp.start()
                cps.append(cp)
        for cp in cps:
            cp.wait()

    return pl.pallas_call(
        body, name=name, out_shape=[_sds((N_CHIP,) + g.shape[1:], g.dtype) for g in gs],
        in_specs=[ANY] * n, out_specs=[ANY] * n,
        scratch_shapes=[pltpu.SemaphoreType.DMA((n * N_CHIP,)), pltpu.SemaphoreType.DMA((n * N_CHIP,))],
    )(*gs)


def _tile2(rows, cols, row_mult):
    tr = _row_tile(rows, 512, row_mult)
    if tr < rows or rows * cols <= 2 ** 20:
        return tr, cols
    return rows, _row_tile(cols, 512, LANE)


def pair_add(core, g, got, name):
    _, rows, cols = g.shape
    tr, tc = _tile2(rows, cols, 16)

    def body(core_ref, g_ref, got_ref, h_ref):
        h_ref[...] = (g_ref[...].astype(F32) + got_ref[...].astype(F32)).astype(h_ref.dtype)

    blk = pl.BlockSpec((None, tr, tc), lambda q, i, j, c: (q, i, j))
    return pl.pallas_call(
        body, name=name, out_shape=_sds((N_CHIP, rows, cols), g.dtype),
        grid_spec=pltpu.PrefetchScalarGridSpec(
            num_scalar_prefetch=1, grid=(N_CHIP, rows // tr, cols // tc),
            in_specs=[pl.BlockSpec((None, tr, tc), lambda q, i, j, c: (2 * q + c[0], i, j)), blk],
            out_specs=blk),
        compiler_params=_params("parallel", "parallel", "parallel"),
    )(core, g, got)


def chip_routes(mx, my, mc):
    return [(2 * px + py, k, (px, py, mc)) for k, (px, py) in enumerate(_other_chips(mx, my))]


def pair_routes(mx, my, mc):
    return [(2 * q + (1 - mc), q, (mx, my, 1 - mc)) for q in range(N_CHIP)]


def exchange_start(hs, routes, name):
    n = len(hs)
    rel = len(routes(0, 0, 0))
    lands = [lax.empty((rel,) + h.shape[1:], h.dtype) for h in hs]

    def body(*refs):
        ins, zones = refs[:n], refs[n:2 * n]
        send_sems, recv_sems = refs[2 * n], refs[2 * n + 1]
        token = refs[-1]
        for a in range(n):
            for k, (src, slot, to) in enumerate(routes(*_me())):
                pltpu.make_async_remote_copy(
                    src_ref=ins[a].at[src], dst_ref=zones[a].at[slot], send_sem=send_sems.at[a * rel + k],
                    recv_sem=recv_sems.at[a * rel + k], device_id=to, device_id_type=MESH).start()
        token[...] = jnp.zeros_like(token)

    sems = pltpu.SemaphoreType.DMA((n * rel,))
    out = pl.pallas_call(
        body, name=name,
        out_shape=(sems, sems, *[pltpu.HBM(h.shape, h.dtype) for h in hs],
                   *[pltpu.HBM(z.shape, z.dtype) for z in lands], _sds((8, LANE), F32)),
        in_specs=[HBM_SPEC] * (2 * n), out_specs=(SEM_SPEC, SEM_SPEC, *[HBM_SPEC] * (2 * n), VMEM_SPEC),
        input_output_aliases={a: 2 + a for a in range(2 * n)},
        compiler_params=pltpu.CompilerParams(has_side_effects=EFFECT),
    )(*[_hbm(h) for h in hs], *[_hbm(z) for z in lands])
    return out[0], out[1], out[2:2 + n], out[2 + n:2 + 2 * n], out[-1]


def exchange_wait(started, routes, after, name):
    send_sems, recv_sems, hs, lands, _ = started
    n = len(hs)
    rel = len(routes(0, 0, 0))

    def body(*refs):
        ins, zones = refs[:n], refs[n:2 * n]
        send_ref, recv_ref = refs[2 * n], refs[2 * n + 1]
        for a in range(n):
            for k, (src, slot, to) in enumerate(routes(*_me())):
                cp = pltpu.make_async_remote_copy(
                    src_ref=ins[a].at[src], dst_ref=zones[a].at[slot], send_sem=send_ref.at[a * rel + k],
                    recv_sem=recv_ref.at[a * rel + k], device_id=to, device_id_type=MESH)
                cp.wait_send()
                cp.wait_recv()

    out = pl.pallas_call(
        body, name=name,
        out_shape=(*[pltpu.HBM(h.shape, h.dtype) for h in hs], *[pltpu.HBM(z.shape, z.dtype) for z in lands]),
        in_specs=[HBM_SPEC] * (2 * n) + [SEM_SPEC, SEM_SPEC, ANY], out_specs=tuple([HBM_SPEC] * (2 * n)),
        input_output_aliases={a: a for a in range(2 * n)},
        compiler_params=pltpu.CompilerParams(has_side_effects=EFFECT),
    )(*hs, *lands, send_sems, recv_sems, after)
    return out[:n], out[n:]


def _adam(w, g, m, v):
    m = ADAM_B1 * m + (1.0 - ADAM_B1) * g
    v = ADAM_B2 * v + (1.0 - ADAM_B2) * (g * g)
    m_hat = m / (1.0 - ADAM_B1 ** ADAM_STEP)
    v_hat = v / (1.0 - ADAM_B2 ** ADAM_STEP)
    delta = -ADAM_LR * (m_hat / (jnp.sqrt(v_hat) + ADAM_EPS) + ADAM_WD * w)
    return delta, m, v


def adamw_owned(chip, h, got, w, m, v, name):
    rows, cols = w.shape
    tr = _row_tile(rows, 256, 16)

    def body(chip_ref, h_ref, got_ref, w_ref, m_ref, v_ref, g_out, d_out, m_out, v_out):
        g = h_ref[...].astype(F32)
        for k in range(N_CHIP - 1):
            g = g + got_ref[k].astype(F32)
        d, mn, vn = _adam(w_ref[...], g, m_ref[...], v_ref[...])
        g_out[...] = g
        d_out[...] = d
        m_out[...] = mn
        v_out[...] = vn

    blk = pl.BlockSpec((tr, cols), lambda i, c: (i, 0))
    return pl.pallas_call(
        body, name=name, out_shape=[_sds((rows, cols), F32)] * 4,
        grid_spec=pltpu.PrefetchScalarGridSpec(
            num_scalar_prefetch=1, grid=(rows // tr,),
            in_specs=[pl.BlockSpec((None, tr, cols), lambda i, c: (c[0], i, 0)),
                      pl.BlockSpec((N_CHIP - 1, tr, cols), lambda i, c: (0, i, 0)), blk, blk, blk],
            out_specs=[blk] * 4),
        compiler_params=_params("parallel"),
    )(chip, h, got, w, m, v)


def owned_sum(chip, h, got, name):
    _, rows, cols = h.shape
    tr, tc = _tile2(rows, cols, 16)

    def body(chip_ref, h_ref, got_ref, g_out):
        g = h_ref[...].astype(F32)
        for k in range(N_CHIP - 1):
            g = g + got_ref[k].astype(F32)
        g_out[...] = g

    return pl.pallas_call(
        body, name=name, out_shape=_sds((rows, cols), F32),
        grid_spec=pltpu.PrefetchScalarGridSpec(
            num_scalar_prefetch=1, grid=(rows // tr, cols // tc),
            in_specs=[pl.BlockSpec((None, tr, tc), lambda i, j, c: (c[0], i, j)),
                      pl.BlockSpec((N_CHIP - 1, tr, tc), lambda i, j, c: (0, i, j))],
            out_specs=pl.BlockSpec((tr, tc), lambda i, j, c: (i, j))),
        compiler_params=_params("parallel", "parallel"),
    )(chip, h, got)


def adamw_rows(sources, where, ws, ms, vs, name):
    n_src, n_par = len(sources), len(ws)

    def body(*refs):
        srcs = refs[:n_src]
        w_refs, m_refs, v_refs = (refs[n_src + j * n_par:n_src + (j + 1) * n_par] for j in range(3))
        outs = refs[n_src + 3 * n_par:]
        for k in range(n_par):
            src, off = srcs[where[k][0]], where[k][1]
            n = w_refs[k].shape[1]
            g = src[0, :, off:off + n]
            for dev in range(1, N_DEV):
                g = g + src[dev, :, off:off + n]
            d, mn, vn = _adam(w_refs[k][...], g, m_refs[k][...], v_refs[k][...])
            for o_ref, val in zip(outs[4 * k:4 * k + 4], (g, d, mn, vn)):
                o_ref[...] = val

    flat = pl.pallas_call(
        body, name=name, out_shape=[_sds(x.shape, F32) for x in ws for _ in range(4)],
        in_specs=[VMEM_SPEC] * (n_src + 3 * n_par), out_specs=[VMEM_SPEC] * (4 * n_par),
        compiler_params=pltpu.CompilerParams(vmem_limit_bytes=VMEM_LIMIT),
    )(*sources, *ws, *ms, *vs)
    return [flat[4 * k:4 * k + 4] for k in range(n_par)]


def adamw_sum(parts, w, m, v, name):
    n_parts, rows, cols = parts.shape
    tr = _row_tile(rows, 256, 8)

    def body(p_ref, w_ref, m_ref, v_ref, g_out, d_out, m_out, v_out):
        g = p_ref[0]
        for k in range(1, n_parts):
            g = g + p_ref[k]
        d, mn, vn = _adam(w_ref[...], g, m_ref[...], v_ref[...])
        g_out[...] = g
        d_out[...] = d
        m_out[...] = mn
        v_out[...] = vn

    blk = pl.BlockSpec((tr, cols), lambda i: (i, 0))
    return pl.pallas_call(
        body, name=name, out_shape=[_sds((rows, cols), F32)] * 4, grid=(rows // tr,),
        in_specs=[pl.BlockSpec((n_parts, tr, cols), lambda i: (0, i, 0)), blk, blk, blk],
        out_specs=[blk] * 4, compiler_params=_params("parallel"),
    )(parts, w, m, v)


def adaln_cols(c_all, w, b, name):
    d, n = w.shape
    tn = _row_tile(n, 768, LANE)

    def body(c_ref, w_ref, b_ref, o_ref):
        c = c_ref[...]
        o_ref[...] = jnp.dot(c * _sigmoid(c), w_ref[...], preferred_element_type=F32, precision=HI) + b_ref[...]

    return pl.pallas_call(
        body, name=name, out_shape=_sds((N_DEV, n), F32), grid=(n // tn,),
        in_specs=[pl.BlockSpec((N_DEV, d), lambda j: (0, 0)), pl.BlockSpec((d, tn), lambda j: (0, j)),
                  pl.BlockSpec((1, tn), lambda j: (0, j))],
        out_specs=pl.BlockSpec((N_DEV, tn), lambda j: (0, j)), compiler_params=_params("parallel"),
    )(c_all, w, b)


def adaln_wgrad(c_all, dmod_cols, name):
    d = c_all.shape[1]
    n = dmod_cols.shape[1]
    tn = _row_tile(n, 768, LANE)

    def body(c_ref, g_ref, o_ref):
        c = c_ref[...]
        o_ref[...] = lax.dot_general(c * _sigmoid(c), g_ref[...], (((0,), (0,)), ((), ())),
                                     preferred_element_type=F32, precision=HI)

    return pl.pallas_call(
        body, name=name, out_shape=_sds((d, n), F32), grid=(n // tn,),
        in_specs=[pl.BlockSpec((N_DEV, d), lambda j: (0, 0)), pl.BlockSpec((N_DEV, tn), lambda j: (0, j))],
        out_specs=pl.BlockSpec((d, tn), lambda j: (0, j)), compiler_params=_params("parallel"),
    )(c_all, dmod_cols)


def _modulate(h, sh, sc):
    return (h * (1.0 + sc) + sh).astype(BF16)


def ffn_in(h, sh, sc, w3, name):
    t, d = h.shape
    nb, _, bw = w3.shape
    half = nb // 2
    tm = _row_tile(t, 512, 16)

    def body(h_ref, sh_ref, sc_ref, wa_ref, wb_ref, a_ref, b_ref, s_ref):
        u = _modulate(h_ref[...], sh_ref[...], sc_ref[...])
        a = _dot(u, wa_ref[...])
        b = _dot(u, wb_ref[...])
        a_ref[...] = a.astype(BF16)
        b_ref[...] = b.astype(BF16)
        s_ref[...] = (a * _sigmoid(a) * b).astype(BF16)

    vec = pl.BlockSpec((1, d), lambda j, i: (0, 0))
    out = pl.BlockSpec((tm, bw), lambda j, i: (i, j))
    return pl.pallas_call(
        body, name=name, out_shape=[_sds((t, half * bw), BF16)] * 3, grid=(half, t // tm),
        in_specs=[pl.BlockSpec((tm, d), lambda j, i: (i, 0)), vec, vec,
                  pl.BlockSpec((None, d, bw), lambda j, i: (j, 0, 0), pipeline_mode=pl.Buffered(1)),
                  pl.BlockSpec((None, d, bw), lambda j, i: (j + half, 0, 0), pipeline_mode=pl.Buffered(1))],
        out_specs=[out] * 3, compiler_params=_params("parallel", "parallel"),
    )(h, sh, sc, w3, w3)


def mod_matmul(h, sh, sc, wt, bw, name):
    t, d = h.shape
    n = wt.shape[0]
    tm = _row_tile(t, 512, 16)

    def body(h_ref, sh_ref, sc_ref, w_ref, o_ref):
        o_ref[...] = _dot_nt(_modulate(h_ref[...], sh_ref[...], sc_ref[...]), w_ref[...]).astype(BF16)

    vec = pl.BlockSpec((1, d), lambda j, i: (0, 0))
    return pl.pallas_call(
        body, name=name, out_shape=_sds((t, n), BF16), grid=(n // bw, t // tm),
        in_specs=[pl.BlockSpec((tm, d), lambda j, i: (i, 0)), vec, vec, pl.BlockSpec((bw, d), lambda j, i: (j, 0))],
        out_specs=pl.BlockSpec((tm, bw), lambda j, i: (i, j)), compiler_params=_params("parallel", "parallel"),
    )(h, sh, sc, wt)


def out_ln(s, w, hin, gmod, ln_g, ln_b, coef, name):
    t, kdim = s.shape
    d = w.shape[1]
    tm = _row_tile(t, 256, 16)

    def body(s_ref, w_ref, hin_ref, gm_ref, g_ref, b_ref, f_ref, z_ref, h_ref):
        f = _dot(s_ref[...], w_ref[...])
        z = ALPHA * hin_ref[...] + (coef * gm_ref[...]) * f
        mu = jnp.mean(z, axis=-1, keepdims=True)
        zc = z - mu
        var = jnp.mean(zc * zc, axis=-1, keepdims=True)
        f_ref[...] = f.astype(BF16)
        z_ref[...] = z
        h_ref[...] = zc * lax.rsqrt(var + LN_EPS) * g_ref[...] + b_ref[...]

    vec = pl.BlockSpec((1, d), lambda i: (0, 0))
    row = pl.BlockSpec((tm, d), lambda i: (i, 0))
    return pl.pallas_call(
        body, name=name, out_shape=[_sds((t, d), BF16), _sds((t, d), F32), _sds((t, d), F32)],
        grid=(t // tm,),
        in_specs=[pl.BlockSpec((tm, kdim), lambda i: (i, 0)),
                  pl.BlockSpec((kdim, d), lambda i: (0, 0), pipeline_mode=pl.Buffered(1)), row, vec, vec, vec],
        out_specs=[row, row, row], compiler_params=_params("parallel"),
    )(s, w, hin, gmod, ln_g, ln_b)


def ln_bwd(dh, z, f, ln_g, gmod, coef, name, target=None):
    t, d = z.shape
    tm = _row_tile(t, 256, 16)
    head = target is not None

    def body(*refs):
        if head:
            dh_ref, tg_ref, z_ref, f_ref, g_ref, gm_ref, dz_ref, df_ref, dg_ref, db_ref, dgm_ref, loss_ref = refs
        else:
            dh_ref, z_ref, f_ref, g_ref, gm_ref, dz_ref, df_ref, dg_ref, db_ref, dgm_ref = refs
        i = pl.program_id(0)

        @pl.when(i == 0)
        def _():
            dg_ref[...] = jnp.zeros_like(dg_ref)
            db_ref[...] = jnp.zeros_like(db_ref)
            dgm_ref[...] = jnp.zeros_like(dgm_ref)
            if head:
                loss_ref[...] = jnp.zeros_like(loss_ref)

        dh = dh_ref[...]
        if head:
            err = dh - tg_ref[...]
            loss_ref[...] += 0.5 * jnp.sum(jnp.mean(err * err, axis=-1, keepdims=True))
            dh = err / d
        zv = z_ref[...]
        mu = jnp.mean(zv, axis=-1, keepdims=True)
        zc = zv - mu
        rstd = lax.rsqrt(jnp.mean(zc * zc, axis=-1, keepdims=True) + LN_EPS)
        xhat = zc * rstd
        dxh = dh * g_ref[...]
        dz = rstd * (dxh - jnp.mean(dxh, axis=-1, keepdims=True)
                     - xhat * jnp.mean(dxh * xhat, axis=-1, keepdims=True))
        dz_ref[...] = dz
        df_ref[...] = ((coef * gm_ref[...]) * dz).astype(BF16)
        dg_ref[...] += _colsum(dh * xhat)
        db_ref[...] += _colsum(dh)
        dgm_ref[...] += _colsum(coef * f_ref[...].astype(F32) * dz)

    vec = pl.BlockSpec((1, d), lambda i: (0, 0))
    row = pl.BlockSpec((tm, d), lambda i: (i, 0))
    ins = [dh] + ([target] if head else []) + [z, f, ln_g, gmod]
    in_specs = [row] + ([row] if head else []) + [row, row, vec, vec]
    out_shape = [_sds((t, d), F32), _sds((t, d), BF16)] + [_sds((1, d), F32)] * 3
    out_specs = [row, row, vec, vec, vec]
    if head:
        out_shape.append(_sds((1, LANE), F32))
        out_specs.append(pl.BlockSpec((1, LANE), lambda i: (0, 0)))
    return pl.pallas_call(
        body, name=name, out_shape=out_shape, grid=(t // tm,), in_specs=in_specs, out_specs=out_specs,
        compiler_params=_params("arbitrary"),
    )(*ins)


def ffn_bwd_act(df, w, a, b, name):
    t, d = df.shape
    fdim = w.shape[0]
    bw = fdim // (N_DEV // 2)
    tm = _row_tile(t, 512, 16)

    def body(df_ref, w_ref, a_ref, b_ref, o_ref):
        ds = _dot_nt(df_ref[...], w_ref[...])
        av = a_ref[...].astype(F32)
        sg = _sigmoid(av)
        o_ref[0] = (ds * b_ref[...].astype(F32) * (sg * (1.0 + av * (1.0 - sg)))).astype(BF16)
        o_ref[1] = (ds * (av * sg)).astype(BF16)

    act = pl.BlockSpec((tm, bw), lambda j, i: (i, j))
    return pl.pallas_call(
        body, name=name, out_shape=_sds((2, t, fdim), BF16), grid=(fdim // bw, t // tm),
        in_specs=[pl.BlockSpec((tm, d), lambda j, i: (i, 0)), pl.BlockSpec((bw, d), lambda j, i: (j, 0)), act, act],
        out_specs=pl.BlockSpec((2, tm, bw), lambda j, i: (0, i, j)),
        compiler_params=_params("parallel", "parallel"),
    )(df, w, a, b)


def matmul_tn(name, a, a_block, a_map, b, b_block, b_map, out_shape, o_block, o_map, n_out, mod=None,
              mod_b=False):
    tk = [s for s in a_block if s is not None][0]
    nk = a.shape[-2] // tk
    m, nn = [s for s in o_block if s is not None]

    def body(*refs):
        if mod is None:
            a_ref, b_ref, o_ref, acc = refs
        else:
            a_ref, sh_ref, sc_ref, b_ref, o_ref, acc = refs
        k = pl.program_id(1)

        @pl.when(k == 0)
        def _():
            acc[...] = jnp.zeros_like(acc)

        av, bv = a_ref[...], b_ref[...]
        if mod is not None and mod_b:
            bv = _modulate(bv, sh_ref[...], sc_ref[...])
        elif mod is not None:
            av = _modulate(av, sh_ref[...], sc_ref[...])
        acc[...] += _dot_tn(av, bv)

        @pl.when(k == nk - 1)
        def _():
            o_ref[...] = acc[...].astype(o_ref.dtype)

    ins = [a] + (list(mod) if mod is not None else []) + [b]
    in_specs = [pl.BlockSpec(a_block, a_map)]
    if mod is not None:
        vec = pl.BlockSpec((1, mod[0].shape[1]), lambda n, k: (0, 0))
        in_specs += [vec, vec]
    in_specs.append(pl.BlockSpec(b_block, b_map))
    return pl.pallas_call(
        body, name=name, out_shape=out_shape, grid=(n_out, nk), in_specs=in_specs,
        out_specs=pl.BlockSpec(o_block, o_map), scratch_shapes=[pltpu.VMEM((m, nn), F32)],
        compiler_params=_params("parallel", "arbitrary"),
    )(*ins)


def block_input_grad(dy, w3, name):
    t = dy.shape[0]
    nb, n, bw = w3.shape
    tm = _row_tile(t, 512, 16)

    def body(dy_ref, w_ref, o_ref):
        o_ref[...] = sum(_dot_nt(dy_ref[:, k * bw:(k + 1) * bw], w_ref[k]) for k in range(nb)).astype(BF16)

    return pl.pallas_call(
        body, name=name, out_shape=_sds((t, n), BF16), grid=(t // tm,),
        in_specs=[pl.BlockSpec((tm, nb * bw), lambda i: (i, 0)),
                  pl.BlockSpec((nb, n, bw), lambda i: (0, 0, 0), pipeline_mode=pl.Buffered(1))],
        out_specs=pl.BlockSpec((tm, n), lambda i: (i, 0)), compiler_params=_params("parallel"),
    )(dy, w3)


def row_input_grad(dy, w, parts, dz, hin, sc, name, after=None):
    t, kdim = dy.shape
    n = w.shape[1]
    kp = kdim // parts
    tm = _row_tile(t, 256, 16)
    row = pl.BlockSpec((tm, n), lambda i: (i, 0))
    vec = pl.BlockSpec((1, n), lambda i: (0, 0))

    def call(c, part):
        final = c == parts - 1

        order = [after] if c == 0 and after is not None else []

        def body(dy_ref, w_ref, *refs):
            du = _dot(dy_ref[...], w_ref[...])
            if c > 0:
                du = du + refs[0][...]
            if not final:
                refs[-1][...] = du
                return
            dz_ref, hin_ref, sc_ref, o_ref, dsc_ref, dsh_ref = refs[-6:]

            @pl.when(pl.program_id(0) == 0)
            def _():
                dsc_ref[...] = jnp.zeros_like(dsc_ref)
                dsh_ref[...] = jnp.zeros_like(dsh_ref)

            o_ref[...] = ALPHA * dz_ref[...] + du * (1.0 + sc_ref[...])
            dsc_ref[...] += _colsum(du * hin_ref[...])
            dsh_ref[...] += _colsum(du)

        ins = [dy, w] + ([part] if c > 0 else []) + order + ([dz, hin, sc] if final else [])
        in_specs = [pl.BlockSpec((tm, kp), lambda i: (i, c)),
                    pl.BlockSpec((kp, n), lambda i: (c, 0), pipeline_mode=pl.Buffered(1))]
        in_specs += ([row] if c > 0 else []) + [ANY] * len(order) + ([row, row, vec] if final else [])
        return pl.pallas_call(
            body, name=f"{name}_{c}", grid=(t // tm,), in_specs=in_specs,
            out_shape=[_sds((t, n), F32), _sds((1, n), F32), _sds((1, n), F32)] if final else _sds((t, n), F32),
            out_specs=[row, vec, vec] if final else row,
            compiler_params=_params("arbitrary" if final else "parallel"),
        )(*ins)

    part = None
    for c in range(parts):
        part = call(c, part)
    return part


REL_W = KW + QB


def bias_table(rel_bias):
    nh, n_rel = rel_bias.shape
    lo = KW - QB - REL_CLIP
    hi = KW - lo - n_rel
    assert n_rel == REL_CLIP + CHUNK and lo >= 0 and hi >= 0
    first, last = rel_bias[:, :1], rel_bias[:, -1:]
    row = jnp.concatenate([jnp.broadcast_to(first, (nh, lo)), rel_bias, jnp.broadcast_to(last, (nh, hi)),
                           jnp.broadcast_to(first, (nh, QB))], axis=1)
    table = jnp.tile(row, (1, QB))[:, :QB * (REL_W - 1)].reshape(nh, QB, REL_W - 1)[:, :, :KW]
    q = np.arange(QB)[:, None] // CHUNK
    k = np.arange(KW)[None, :] // CHUNK
    band = (k >= q) & (k <= q + A_PAST_CHUNKS)
    return jnp.where(band[None], table, NEG)


def bias_grad_skew(dbias):
    nh = dbias.shape[0]
    flat = jnp.pad(dbias, ((0, 0), (0, 0), (0, REL_W - 1 - KW))).reshape(nh, QB * (REL_W - 1))
    return jnp.pad(flat, ((0, 0), (0, QB))).reshape(nh, QB, REL_W)


def bias_clip_map(n_rel):
    m = np.arange(REL_W)
    dist = np.where(m < KW, m, m - REL_W) - (KW - QB)
    idx = np.clip(dist, -REL_CLIP, CHUNK - 1) + REL_CLIP
    return (idx[:, None] == np.arange(n_rel)[None, :]).astype(np.float32)


PAIR = 2


def _pair_specs(col, rows_of):
    return [pl.BlockSpec((QB, LANE), functools.partial(lambda r, h, i: (rows_of(r, i), col // LANE + h), r))
            for r in range(3)]


def _earlier(r, i):
    return jnp.maximum(i - 2 + r, 0)


def _head_lanes(hh, dh):
    lane = lax.broadcasted_iota(jnp.int32, (1, LANE), 1)
    return (lane < dh) if hh == 0 else (lane >= dh)


def _only(x, lanes):
    return jnp.where(lanes, x, jnp.zeros_like(x))


def _scores(q, ks, bias, i, scale):
    s = jnp.concatenate([_dot_nt(q, kk) for kk in ks], axis=1) * scale + bias
    col = lax.broadcasted_iota(jnp.int32, s.shape, 1)
    return jnp.where(col >= (2 - i) * QB, s, NEG)


def attn_fwd(p, cols, bias, dh, name):
    t = p.shape[0]
    nh = bias.shape[0]
    scale = dh ** -0.5

    def body(q_ref, k0, k1, k2, v0, v1, v2, b_ref, o_ref, lse_ref):
        i = pl.program_id(1)
        q = q_ref[...]
        outs = []
        for hh in range(PAIR):
            lanes = _head_lanes(hh, dh)
            s = _scores(q, [_only(kk[...], lanes) for kk in (k0, k1, k2)], b_ref[hh], i, scale)
            m = jnp.max(s, axis=-1, keepdims=True)
            e = jnp.exp(s - m)
            l = jnp.sum(e, axis=-1, keepdims=True)
            eb = e.astype(BF16)
            o = sum(_dot(eb[:, r * QB:(r + 1) * QB], vv[...]) for r, vv in enumerate((v0, v1, v2)))
            outs.append(o / l)
            lse_ref[hh] = m + jnp.log(l)
        o_ref[...] = jnp.where(_head_lanes(0, dh), outs[0], outs[1]).astype(BF16)

    st = pl.BlockSpec((PAIR, QB, 1), lambda h, i: (h, i, 0))
    return pl.pallas_call(
        body, name=name, out_shape=[_sds((t, nh * dh), BF16), _sds((nh, t, 1), F32)], grid=(nh // PAIR, t // QB),
        in_specs=[pl.BlockSpec((QB, LANE), lambda h, i: (i, cols["qa"] // LANE + h))]
        + _pair_specs(cols["ka"], _earlier) + _pair_specs(cols["va"], _earlier)
        + [pl.BlockSpec((PAIR, QB, KW), lambda h, i: (h, 0, 0))],
        out_specs=[pl.BlockSpec((QB, LANE), lambda h, i: (i, h)), st],
        compiler_params=_params("parallel", "parallel"),
    )(p, p, p, p, p, p, p, bias)


def attn_bwd(p, cols, bias, lse, dy, dh, name):
    t = p.shape[0]
    nh = bias.shape[0]
    nb = t // QB
    scale = dh ** -0.5

    def body(q_ref, k0, k1, k2, v0, v1, v2, b_ref, lse_ref, dy_ref, dq_ref, dk_ref, dv_ref, db_ref, dk_acc, dv_acc):
        i = pl.program_id(1)

        @pl.when(i == 0)
        def _():
            db_ref[...] = jnp.zeros_like(db_ref)
            dk_acc[...] = jnp.zeros_like(dk_acc)
            dv_acc[...] = jnp.zeros_like(dv_acc)

        q, dyv = q_ref[...], dy_ref[...]
        ks = [k0[...], k1[...], k2[...]]
        dqs, dks, dvs = [], [], []
        for hh in range(PAIR):
            lanes = _head_lanes(hh, dh)
            s = _scores(q, [_only(kk, lanes) for kk in ks], b_ref[hh], i, scale)
            prob = jnp.exp(s - lse_ref[hh])
            dprob = jnp.concatenate([_dot_nt(dyv, _only(vv[...], lanes)) for vv in (v0, v1, v2)], axis=1)
            delta = jnp.sum(prob * dprob, axis=-1, keepdims=True)
            ds = prob * (dprob - delta)
            dsb, pb = ds.astype(BF16), prob.astype(BF16)
            dqs.append(sum(_dot(dsb[:, r * QB:(r + 1) * QB], kk) for r, kk in enumerate(ks)))
            dks.append([_dot_tn(dsb[:, r * QB:(r + 1) * QB], q) for r in range(3)])
            dvs.append([_dot_tn(pb[:, r * QB:(r + 1) * QB], dyv) for r in range(3)])
            db_ref[hh] += ds
        first = _head_lanes(0, dh)
        dq_ref[...] = (jnp.where(first, dqs[0], dqs[1]) * scale).astype(BF16)
        for r in range(3):
            rows = pl.ds(pl.multiple_of(_earlier(r, i) * QB, QB), QB)
            dk_acc[rows, :] += jnp.where(first, dks[0][r], dks[1][r])
            dv_acc[rows, :] += jnp.where(first, dvs[0][r], dvs[1][r])

        @pl.when(i == nb - 1)
        def _():
            dk_ref[...] = (dk_acc[...] * scale).astype(BF16)
            dv_ref[...] = dv_acc[...].astype(BF16)

    st = pl.BlockSpec((PAIR, QB, 1), lambda h, i: (h, i, 0))
    tab = pl.BlockSpec((PAIR, QB, KW), lambda h, i: (h, 0, 0))
    own = pl.BlockSpec((QB, LANE), lambda h, i: (i, h))
    whole = pl.BlockSpec((t, LANE), lambda h, i: (0, h))
    return pl.pallas_call(
        body, name=name,
        out_shape=[_sds((t, nh * dh), BF16)] * 3 + [_sds((nh, QB, KW), F32)],
        grid=(nh // PAIR, nb),
        in_specs=[pl.BlockSpec((QB, LANE), lambda h, i: (i, cols["qa"] // LANE + h))]
        + _pair_specs(cols["ka"], _earlier) + _pair_specs(cols["va"], _earlier) + [tab, st, own],
        out_specs=[own, whole, whole, tab],
        scratch_shapes=[pltpu.VMEM((t, LANE), F32), pltpu.VMEM((t, LANE), F32)],
        compiler_params=_params("parallel", "arbitrary"),
    )(p, p, p, p, p, p, p, bias, lse, dy)


def _prefix_sums(x, strict):
    r = lax.broadcasted_iota(jnp.int32, (CHUNK, CHUNK), 0)
    c = lax.broadcasted_iota(jnp.int32, (CHUNK, CHUNK), 1)
    tri = jnp.where((c < r) if strict else (c <= r), 1.0, 0.0).astype(BF16)
    n = x.shape[1]
    hi = x.astype(BF16)
    rest = x - hi.astype(F32)
    mid = rest.astype(BF16)
    lo = (rest - mid.astype(F32)).astype(BF16)
    out = _dot(tri, jnp.concatenate([hi, mid, lo], axis=1))
    return out[:, :n] + out[:, n:2 * n] + out[:, 2 * n:]


def _gate(lr, wa, ba):
    y = _dot(lr, wa) + ba
    return (jnp.minimum(y, 0.0) - jnp.log(1.0 + jnp.exp(-jnp.abs(y)))) / GATE_TAU, y


def _decays(la):
    cum = _prefix_sums(la, strict=False)
    last = cum[CHUNK - 1:CHUNK, :]
    return jnp.exp(last - cum), jnp.exp(last)


def _gla_specs(cols, hk, hv, order):
    def at(start, width):
        return pl.BlockSpec((GB, width), lambda h, i: (order(i), start // width + h))
    return [at(cols["qb"], hk), at(cols["kb"], hk), at(cols["vb"], hv), at(cols["rb"], hv),
            pl.BlockSpec((GB, LANE), lambda h, i: (order(i), cols["lr"] // LANE))]


def gla_fwd(p, cols, wa, ba, gn, nh, hk, hv, name):
    t = p.shape[0]
    nc = t // CHUNK
    scale = hk ** -0.5
    per = GB // CHUNK

    def body(q_ref, k_ref, v_ref, r_ref, lr_ref, wa_ref, ba_ref, gn_ref, o_ref, y_ref, st_ref, state):
        @pl.when(pl.program_id(1) == 0)
        def _():
            state[...] = jnp.zeros_like(state)

        for c in range(per):
            rows = pl.ds(c * CHUNK, CHUNK)
            la, _ = _gate(lr_ref[rows, :], wa_ref[...], ba_ref[...])
            w, decay = _decays(la)
            kdec = (k_ref[rows, :].astype(F32) * w).astype(BF16)
            st = decay * state[...] + _dot_tn(v_ref[rows, :], kdec)
            state[...] = st
            st_ref[c] = st
            o = _dot_nt(q_ref[rows, :], st.astype(BF16)) * scale
            o_ref[rows, :] = o
            rinv = lax.rsqrt(jnp.mean(o * o, axis=-1, keepdims=True) + RMS_EPS)
            rv = r_ref[rows, :].astype(F32)
            y_ref[rows, :] = (o * rinv * gn_ref[...] * (rv * _sigmoid(rv))).astype(BF16)

    return pl.pallas_call(
        body, name=name,
        out_shape=[_sds((t, nh * hv), F32), _sds((t, nh * hv), BF16), _sds((nh, nc, hv, hk), F32)],
        grid=(nh, t // GB),
        in_specs=_gla_specs(cols, hk, hv, lambda i: i)
        + [pl.BlockSpec((LANE, hk), lambda h, i: (0, h)), pl.BlockSpec((1, hk), lambda h, i: (0, h)),
           pl.BlockSpec((1, hv), lambda h, i: (0, 0))],
        out_specs=[pl.BlockSpec((GB, hv), lambda h, i: (i, h)), pl.BlockSpec((GB, hv), lambda h, i: (i, h)),
                   pl.BlockSpec((None, per, hv, hk), lambda h, i: (h, i, 0, 0))],
        scratch_shapes=[pltpu.VMEM((hv, hk), F32)], compiler_params=_params("parallel", "arbitrary"),
    )(p, p, p, p, p, wa, ba, gn)


def gla_bwd(p, cols, wa, ba, gn, o, states, dy, nh, hk, hv, name):
    t = p.shape[0]
    nblk = t // GB
    scale = hk ** -0.5
    per = GB // CHUNK

    def rev(i):
        return nblk - 1 - i

    def body(q_ref, k_ref, v_ref, r_ref, lr_ref, wa_ref, ba_ref, gn_ref, o_ref, st_ref, sp_ref, dy_ref,
             dq_ref, dk_ref, dv_ref, dr_ref, dg_ref, dgn_ref, carry):
        h, i = pl.program_id(0), pl.program_id(1)

        @pl.when(i == 0)
        def _():
            carry[...] = jnp.zeros_like(carry)

        @pl.when((i == 0) & (h == 0))
        def _():
            dgn_ref[...] = jnp.zeros_like(dgn_ref)

        gnv = gn_ref[...]
        for c in reversed(range(per)):
            rows = pl.ds(c * CHUNK, CHUNK)
            rv = r_ref[rows, :].astype(F32)
            sg = _sigmoid(rv)
            dyv = dy_ref[rows, :].astype(F32)
            ov = o_ref[rows, :]
            rinv = lax.rsqrt(jnp.mean(ov * ov, axis=-1, keepdims=True) + RMS_EPS)
            dn = dyv * (rv * sg)
            dr_ref[rows, :] = (dyv * (ov * rinv * gnv) * (sg * (1.0 + rv * (1.0 - sg)))).astype(BF16)
            dgn_ref[...] += _colsum(dn * ov * rinv)
            dxh = dn * gnv
            do = rinv * dxh - ov * (rinv * rinv * rinv) * jnp.mean(dxh * ov, axis=-1, keepdims=True)
            dob = (do * scale).astype(BF16)
            qv, kv, vv = q_ref[rows, :], k_ref[rows, :], v_ref[rows, :]
            dq_ref[rows, :] = _dot(dob, st_ref[c].astype(BF16)).astype(BF16)
            dst = carry[...] + _dot_tn(dob, qv)
            if c > 0:
                prev = st_ref[c - 1]
            else:
                prev = jnp.where(i == nblk - 1, 0.0, sp_ref[0])
            ddecay = _colsum(dst * prev)
            la, y = _gate(lr_ref[rows, :], wa_ref[...], ba_ref[...])
            w, decay = _decays(la)
            kf = kv.astype(F32)
            kdec = (kf * w).astype(BF16)
            dstb = dst.astype(BF16)
            dkdec = _dot(vv, dstb)
            dv_ref[rows, :] = _dot_nt(kdec, dstb).astype(BF16)
            dk_ref[rows, :] = (dkdec * w).astype(BF16)
            e = dkdec * kf * w
            dla = _prefix_sums(e, strict=True) + ddecay * decay
            dg_ref[rows, :] = dla * (1.0 / GATE_TAU) * _sigmoid(-y)
            carry[...] = decay * dst

    per_head = lambda width: pl.BlockSpec((GB, width), lambda h, i: (rev(i), h))
    return pl.pallas_call(
        body, name=name,
        out_shape=[_sds((t, nh * hk), BF16), _sds((t, nh * hk), BF16), _sds((t, nh * hv), BF16),
                   _sds((t, nh * hv), BF16), _sds((t, nh * hk), F32), _sds((1, hv), F32)],
        grid=(nh, nblk),
        in_specs=_gla_specs(cols, hk, hv, rev)
        + [pl.BlockSpec((LANE, hk), lambda h, i: (0, h)), pl.BlockSpec((1, hk), lambda h, i: (0, h)),
           pl.BlockSpec((1, hv), lambda h, i: (0, 0)), per_head(hv),
           pl.BlockSpec((None, per, hv, hk), lambda h, i: (h, rev(i), 0, 0)),
           pl.BlockSpec((None, 1, hv, hk), lambda h, i: (h, jnp.maximum(rev(i) * per - 1, 0), 0, 0)),
           per_head(hv)],
        out_specs=[per_head(hk), per_head(hk), per_head(hv), per_head(hv), per_head(hk),
                   pl.BlockSpec((1, hv), lambda h, i: (0, 0))],
        scratch_shapes=[pltpu.VMEM((hv, hk), F32)], compiler_params=_params("arbitrary", "arbitrary"),
    )(p, p, p, p, p, wa, ba, gn, o, states, states, dy)


def gate_bwd(p, lr_col, dg, wa, name):
    t, kd = dg.shape
    tm = _row_tile(t, 512, 16)

    def body(lr_ref, dg_ref, wa_ref, dlr_ref, dwa_ref, dba_ref):
        @pl.when(pl.program_id(0) == 0)
        def _():
            dwa_ref[...] = jnp.zeros_like(dwa_ref)
            dba_ref[...] = jnp.zeros_like(dba_ref)

        g = dg_ref[...]
        gb = g.astype(BF16)
        dlr_ref[...] = _dot_nt(gb, wa_ref[...]).astype(BF16)
        dwa_ref[...] += _dot_tn(lr_ref[...], gb)
        dba_ref[...] += _colsum(g)

    return pl.pallas_call(
        body, name=name, out_shape=[_sds((t, LANE), BF16), _sds((LANE, kd), F32), _sds((1, kd), F32)],
        grid=(t // tm,),
        in_specs=[pl.BlockSpec((tm, LANE), lambda i: (i, lr_col // LANE)), pl.BlockSpec((tm, kd), lambda i: (i, 0)),
                  pl.BlockSpec((LANE, kd), lambda i: (0, 0))],
        out_specs=[pl.BlockSpec((tm, LANE), lambda i: (i, 0)), pl.BlockSpec((LANE, kd), lambda i: (0, 0)),
                   pl.BlockSpec((1, kd), lambda i: (0, 0))],
        compiler_params=_params("arbitrary"),
    )(p, dg, wa)


def proj_merge(ya, yb, wa3, wb3, p, ga_col, gb_col, name):
    t, kd = ya.shape
    nb, _, bw = wa3.shape
    tm = _row_tile(t, 1024, 16)

    def body(ya_ref, yb_ref, wa_ref, wb_ref, ga_ref, gb_ref, pa_ref, pb_ref, mg_ref):
        pa = _dot(ya_ref[...], wa_ref[...])
        pb = _dot(yb_ref[...], wb_ref[...])
        pa_ref[...] = pa.astype(BF16)
        pb_ref[...] = pb.astype(BF16)
        mg_ref[...] = (_sigmoid(ga_ref[...].astype(F32)) * pa + _sigmoid(gb_ref[...].astype(F32)) * pb).astype(BF16)

    act = pl.BlockSpec((tm, kd), lambda j, i: (i, 0))
    wsp = pl.BlockSpec((None, kd, bw), lambda j, i: (j, 0, 0))
    out = pl.BlockSpec((tm, bw), lambda j, i: (i, j))
    return pl.pallas_call(
        body, name=name, out_shape=[_sds((t, nb * bw), BF16)] * 3, grid=(nb, t // tm),
        in_specs=[act, act, wsp, wsp, pl.BlockSpec((tm, bw), lambda j, i: (i, ga_col // bw + j)),
                  pl.BlockSpec((tm, bw), lambda j, i: (i, gb_col // bw + j))],
        out_specs=[out] * 3, compiler_params=_params("parallel", "parallel"),
    )(ya, yb, wa3, wb3, p, p)


def merge_bwd(dm, w, p, ga_col, gb_col, pa, pb, name):
    t, d = dm.shape
    n = w.shape[0]
    tn = _row_tile(n, 512, LANE)
    tm = _row_tile(t, 512, 16)

    def body(dm_ref, w_ref, ga_ref, gb_ref, pa_ref, pb_ref, dpa_ref, dpb_ref, dga_ref, dgb_ref):
        dmg = _dot_nt(dm_ref[...], w_ref[...])
        sa = _sigmoid(ga_ref[...].astype(F32))
        sb = _sigmoid(gb_ref[...].astype(F32))
        dpa_ref[...] = (dmg * sa).astype(BF16)
        dpb_ref[...] = (dmg * sb).astype(BF16)
        dga_ref[...] = (dmg * pa_ref[...].astype(F32) * sa * (1.0 - sa)).astype(BF16)
        dgb_ref[...] = (dmg * pb_ref[...].astype(F32) * sb * (1.0 - sb)).astype(BF16)

    out = pl.BlockSpec((tm, tn), lambda j, i: (i, j))
    return pl.pallas_call(
        body, name=name, out_shape=[_sds((t, n), BF16)] * 4, grid=(n // tn, t // tm),
        in_specs=[pl.BlockSpec((tm, d), lambda j, i: (i, 0)), pl.BlockSpec((tn, d), lambda j, i: (j, 0)),
                  pl.BlockSpec((tm, tn), lambda j, i: (i, ga_col // tn + j)),
                  pl.BlockSpec((tm, tn), lambda j, i: (i, gb_col // tn + j)), out, out],
        out_specs=[out] * 4, compiler_params=_params("parallel", "parallel"),
    )(dm, w, p, p, pa, pb)


def rel_bias_grad(skew, clip_map, name):
    nh, _, jd = skew.shape
    n_rel = clip_map.shape[1]

    def body(s_ref, c_ref, o_ref):
        sums = jnp.concatenate([_colsum(s_ref[h]) for h in range(nh)], axis=0)
        o_ref[...] = jnp.dot(sums, c_ref[...], preferred_element_type=F32, precision=HI)

    return pl.pallas_call(
        body, name=name, out_shape=_sds((nh, n_rel), F32), in_specs=[VMEM_SPEC, VMEM_SPEC], out_specs=VMEM_SPEC,
        compiler_params=pltpu.CompilerParams(vmem_limit_bytes=VMEM_LIMIT),
    )(skew, clip_map)


MIX_BLOCK = 9 * LANE
MIX_PARTS = 3


def mix_layout(d, a_width, bk, bv):
    main = 3 * a_width + 2 * bk + 2 * bv
    cols = {"qa": 0, "ka": a_width, "va": 2 * a_width, "qb": 3 * a_width, "kb": 3 * a_width + bk,
            "vb": 3 * a_width + 2 * bk, "rb": 3 * a_width + 2 * bk + bv, "ga": main, "gb": main + d,
            "lr": main + 2 * d}
    total = main + 2 * d + LANE
    assert total % MIX_BLOCK == 0
    return cols, main, total


def _mix_pieces(per, main, rank, d):
    out = []
    for lo, hi in ((0, main), (main + rank, main + rank + 2 * d), (main, main + rank)):
        while lo < hi:
            cut = min(hi, (lo // per + 1) * per)
            out.append((lo, cut))
            lo = cut
    return out


def mix_weight_in(g3, main, rank):
    d = g3.shape[2]
    flat = g3.reshape(-1, d)
    return jnp.concatenate([flat[:main], flat[main + rank:], flat[main:main + rank],
                            jnp.zeros((LANE - rank, d), g3.dtype)], axis=0)


def mix_weight_grad_out(gt, main, rank, per):
    d = gt.shape[1]
    blocks = [[] for _ in range(N_DEV)]
    pos = 0
    for lo, hi in _mix_pieces(per, main, rank, d):
        blocks[lo // per].append((lo, gt[pos:pos + hi - lo]))
        pos += hi - lo
    return jnp.stack([jnp.concatenate([x for _, x in sorted(b, key=lambda e: e[0])], axis=0) for b in blocks])


def ffn_forward(h, sh, sc, g, w_in3, w_out_of, ln_g, ln_b, tag):
    a, b, s = ffn_in(h, sh, sc, w_in3, f"{tag}_in")
    w_out = w_out_of(s)
    f, z, hout = out_ln(s, w_out, h, g, ln_g, ln_b, 0.5, f"{tag}_out")
    return hout, (h, a, b, s, f, z), w_out


def ffn_backward_weights(dh, saved, sh, sc, g, w_in3, w_out, ln_g, tag, target=None):
    hin, a, b, s, f, z = saved
    t, d = hin.shape
    nb, _, bw = w_in3.shape
    half = nb // 2
    fdim = w_out.shape[0]
    res = ln_bwd(dh, z, f, ln_g, g, 0.5, f"{tag}_ln_bwd", target=target)
    dz, df, dln_g, dln_b, dg = res[:5]
    dab = ffn_bwd_act(df, w_out, a, b, f"{tag}_act_bwd")
    tk = _row_tile(t, 512, 16)
    dw_out = matmul_tn(f"{tag}_dwout", s, (tk, bw), lambda n, k: (k, n), df, (tk, d), lambda n, k: (k, 0),
                       _sds((fdim, d), BF16), (bw, d), lambda n, k: (n, 0), fdim // bw)
    dw_in = matmul_tn(f"{tag}_dwin", hin, (tk, d), lambda n, k: (k, 0), dab, (None, tk, bw),
                      lambda n, k: (n // half, k, n % half), _sds((nb, d, bw), BF16), (None, d, bw),
                      lambda n, k: (n, 0, 0), nb, mod=(sh, sc))
    grads = dict(w_in=dw_in, w_out=dw_out.reshape(N_DEV, fdim // N_DEV, d), ln_g=dln_g, ln_b=dln_b, g=dg)
    return (dab, dz), grads, (res[5] if target is not None else None)


def ffn_backward_input(carry, saved, sc, w_in3, tag, after=None):
    dab, dz = carry
    hin = saved[0]
    t, d = hin.shape
    nb, _, bw = w_in3.shape
    half = nb // 2
    fdim = half * bw
    tm = _row_tile(t, 256, 16)

    def contract(dy_ref, w_ref):
        return sum(_dot_nt(dy_ref[:, j * bw:(j + 1) * bw], w_ref[j]) for j in range(half))

    order = [] if after is None else [after]

    def first(dy_ref, w_ref, *refs):
        refs[-1][...] = contract(dy_ref, w_ref)

    def second(dy_ref, w_ref, part_ref, dz_ref, hin_ref, sc_ref, o_ref, dsc_ref, dsh_ref):
        @pl.when(pl.program_id(0) == 0)
        def _():
            dsc_ref[...] = jnp.zeros_like(dsc_ref)
            dsh_ref[...] = jnp.zeros_like(dsh_ref)

        du = part_ref[...] + contract(dy_ref, w_ref)
        o_ref[...] = ALPHA * dz_ref[...] + du * (1.0 + sc_ref[...])
        dsc_ref[...] += _colsum(du * hin_ref[...])
        dsh_ref[...] += _colsum(du)

    def specs(which):
        return [pl.BlockSpec((None, tm, fdim), lambda i: (which, i, 0)),
                pl.BlockSpec((half, d, bw), lambda i: (which, 0, 0), pipeline_mode=pl.Buffered(1))]

    row = pl.BlockSpec((tm, d), lambda i: (i, 0))
    vec = pl.BlockSpec((1, d), lambda i: (0, 0))
    part = pl.pallas_call(
        first, name=f"{tag}_du_a", out_shape=_sds((t, d), F32), grid=(t // tm,),
        in_specs=specs(0) + [ANY] * len(order), out_specs=row, compiler_params=_params("parallel"),
    )(dab, w_in3, *order)
    return pl.pallas_call(
        second, name=f"{tag}_du_b", out_shape=[_sds((t, d), F32), _sds((1, d), F32), _sds((1, d), F32)],
        grid=(t // tm,), in_specs=specs(1) + [row, row, row, vec], out_specs=[row, vec, vec],
        compiler_params=_params("arbitrary"),
    )(dab, w_in3, part, dz, hin, sc)


def _after(v, token):
    return v if token is None else v + token[:1, :1]


def local_step(x, target, mod, weights_of, grads_ready, grads_sent, rel_bias, w_alpha2, b_alpha, gla_norm_g, lns,
               bias=None, weights_early=None):
    t, d = x.shape
    sh1, sc1, g1, sh2, sc2, g2, sh3, sc3, g3 = [mod[i:i + 1] for i in range(N_MOD)]
    ln1_g, ln1_b, ln2_g, ln2_b, ln3_g, ln3_b = lns
    n_heads_a, n_rel = rel_bias.shape
    rank, bk = w_alpha2.shape
    hv = gla_norm_g.shape[1]

    w1 = weights_of("ffn1_in", mod)
    h1, saved1, w1["out"] = ffn_forward(x, sh1, sc1, g1, w1["in"], lambda s: weights_of("ffn1_out", s)["out"],
                                        ln1_g, ln1_b, "ffn1")
    wm = weights_of("mix", h1)
    a_width = wm["proj_a"].shape[1]
    bv = wm["proj_b"].shape[1]
    nh_b = bv // hv
    hk = bk // nh_b
    cols, main, total = mix_layout(d, a_width, bk, bv)
    w_mix = mix_weight_in(wm["in_t"], main, rank)
    p = mod_matmul(h1, sh2, sc2, w_mix, MIX_BLOCK, "mix_in")
    bias = bias_table(rel_bias) if bias is None else bias
    dh = a_width // n_heads_a
    assert PAIR * dh == LANE
    ya, lse = attn_fwd(p, cols, bias, dh, "attn_fwd")
    token = None if weights_early is None else weights_early("ffn2", ya)
    b_alpha = _after(b_alpha, token)
    wa_pad = jnp.zeros((LANE, bk), BF16).at[:rank].set(w_alpha2.astype(BF16))
    o_b, yb, states = gla_fwd(p, cols, wa_pad, b_alpha, gla_norm_g, nh_b, hk, hv, "gla_fwd")
    pa, pb, merged = proj_merge(ya, yb, wm["proj_a"], wm["proj_b"], p, cols["ga"], cols["gb"], "proj_merge")
    m, z2, h2 = out_ln(merged, wm["out"], h1, g2, ln2_g, ln2_b, 1.0, "mix_out")
    w3 = weights_of("ffn2", h2)
    h3, saved3, _ = ffn_forward(h2, sh3, sc3, g3, w3["in"], lambda s: w3["out"], ln3_g, ln3_b, "ffn2")

    carry3, gr3, loss = ffn_backward_weights(h3, saved3, sh3, sc3, g3, w3["in"], w3["out"], ln3_g, "ffn2",
                                             target=target)
    token = grads_ready("ffn2", dict(ffn2_in=gr3["w_in"], ffn2_out=gr3["w_out"]))
    dh2, dsc3, dsh3 = ffn_backward_input(carry3, saved3, sc3, w3["in"], "ffn2", after=token)
    token = grads_sent("ffn2", dh2)
    dz2, dm, dln2_g, dln2_b, dg2 = ln_bwd(dh2, z2, m, _after(ln2_g, token), g2, 1.0, "mix_ln_bwd")
    dpa, dpb, dga, dgb = merge_bwd(dm, wm["out"], p, cols["ga"], cols["gb"], pa, pb, "merge_bwd")
    tk = _row_tile(t, 512, 16)
    dw_mix_out = matmul_tn("mix_dwout", merged, (tk, 512), lambda n, k: (k, n), dm, (tk, d), lambda n, k: (k, 0),
                           _sds((d, d), BF16), (512, d), lambda n, k: (n, 0), d // 512)
    pbw = wm["proj_a"].shape[2]
    dya = block_input_grad(dpa, wm["proj_a"], "proj_a_dy")
    dyb = block_input_grad(dpb, wm["proj_b"], "proj_b_dy")
    dw_pa = matmul_tn("proj_a_dw", ya, (tk, a_width), lambda n, k: (k, 0), dpa, (tk, pbw), lambda n, k: (k, n),
                      _sds((N_DEV, a_width, pbw), BF16), (None, a_width, pbw), lambda n, k: (n, 0, 0), N_DEV)
    dw_pb = matmul_tn("proj_b_dw", yb, (tk, bv), lambda n, k: (k, 0), dpb, (tk, pbw), lambda n, k: (k, n),
                      _sds((N_DEV, bv, pbw), BF16), (None, bv, pbw), lambda n, k: (n, 0, 0), N_DEV)
    dqb, dkb, dvb, drb, dgate, dgn = gla_bwd(p, cols, wa_pad, b_alpha, gla_norm_g, o_b, states, dyb,
                                             nh_b, hk, hv, "gla_bwd")
    dlr, dwa_pad, dba = gate_bwd(p, cols["lr"], dgate, wa_pad, "gate_bwd")
    dqa, dka, dva, dbias = attn_bwd(p, cols, bias, lse, dya, dh, "attn_bwd")
    d_rel = rel_bias_grad(bias_grad_skew(dbias), jnp.asarray(bias_clip_map(n_rel)), "rel_bias_grad")
    dp = jnp.concatenate([dqa, dka, dva, dqb, dkb, dvb, drb,
                          dga, dgb, dlr], axis=1)
    dw_mix_t = matmul_tn("mix_dwin", dp, (tk, MIX_BLOCK), lambda n, k: (k, n), h1, (tk, d), lambda n, k: (k, 0),
                         _sds((total, d), BF16), (MIX_BLOCK, d), lambda n, k: (n, 0), total // MIX_BLOCK,
                         mod=(sh2, sc2), mod_b=True)
    dw_mix_in = mix_weight_grad_out(dw_mix_t, main, rank, wm["in_t"].shape[1])
    token = grads_ready("mix", dict(mix_in=dw_mix_in, proj_a=dw_pa, proj_b=dw_pb,
                                    mix_out=dw_mix_out.reshape(N_DEV, d // N_DEV, d)))
    dh1, dsc2, dsh2 = row_input_grad(dp, w_mix, MIX_PARTS, dz2, h1, sc2, "mix_du", after=token)
    token = grads_sent("mix", dh1)
    carry1, gr1, _ = ffn_backward_weights(dh1, saved1, sh1, sc1, g1, w1["in"], w1["out"], _after(ln1_g, token),
                                          "ffn1")
    token = grads_ready("ffn1", dict(ffn1_in=gr1["w_in"], ffn1_out=gr1["w_out"]))
    dx, dsc1, dsh1 = ffn_backward_input(carry1, saved1, sc1, w1["in"], "ffn1", after=token)

    dmod = [dsh1, dsc1, gr1["g"], dsh2, dsc2, dg2, dsh3, dsc3, gr3["g"]]
    small = dict(ln1_g=gr1["ln_g"], ln1_b=gr1["ln_b"], ln2_g=dln2_g, ln2_b=dln2_b, ln3_g=gr3["ln_g"],
                 ln3_b=gr3["ln_b"], b_alpha=dba, gla_norm_g=dgn, w_alpha2=dwa_pad[:rank], rel_bias=d_rel)
    return loss, dx, dmod, small


GROUPS = dict(ffn1=("ffn1_in", "ffn1_out"), mix=("mix_in", "proj_a", "proj_b", "mix_out"),
              ffn2=("ffn2_in", "ffn2_out"))
GATHERS = dict(ffn1_in=("ffn1_in",), ffn1_out=("ffn1_out",), mix=GROUPS["mix"], ffn2=GROUPS["ffn2"])
SMALL_REPLICATED = ("b_ada", "ln1_g", "ln1_b", "ln2_g", "ln2_b", "ln3_g", "ln3_b", "b_alpha", "gla_norm_g")
SMALL_SHARDED = ("rel_bias", "w_alpha2")
WEIGHT_ORDER = ("w_ada", "b_ada", "ffn1_w_in", "ffn1_w_out", "ln1_g", "ln1_b", "w_mix_in", "rel_bias", "w_alpha2",
                "b_alpha", "gla_norm_g", "w_proj_a", "w_proj_b", "w_mix_out", "ln2_g", "ln2_b", "ffn2_w_in",
                "ffn2_w_out", "ln3_g", "ln3_b")
BIG_NAME = dict(ffn1_in="ffn1_w_in", ffn1_out="ffn1_w_out", mix_in="w_mix_in", proj_a="w_proj_a",
                proj_b="w_proj_b", mix_out="w_mix_out", ffn2_in="ffn2_w_in", ffn2_out="ffn2_w_out")


def kernel(x, c, w_ada, b_ada, ffn1_w_in, ffn1_w_out, ln1_g, ln1_b, w_mix_in, rel_bias, w_alpha2, b_alpha, gla_norm_g, w_proj_a, w_proj_b, w_mix_out, ln2_g, ln2_b, ffn2_w_in, ffn2_w_out, ln3_g, ln3_b, loss_target, m_w_ada, m_b_ada, m_ffn1_w_in, m_ffn1_w_out, m_ln1_g, m_ln1_b, m_w_mix_in, m_rel_bias, m_w_alpha2, m_b_alpha, m_gla_norm_g, m_w_proj_a, m_w_proj_b, m_w_mix_out, m_ln2_g, m_ln2_b, m_ffn2_w_in, m_ffn2_w_out, m_ln3_g, m_ln3_b, v_w_ada, v_b_ada, v_ffn1_w_in, v_ffn1_w_out, v_ln1_g, v_ln1_b, v_w_mix_in, v_rel_bias, v_w_alpha2, v_b_alpha, v_gla_norm_g, v_w_proj_a, v_w_proj_b, v_w_mix_out, v_ln2_g, v_ln2_b, v_ffn2_w_in, v_ffn2_w_out, v_ln3_g, v_ln3_b):
    env = dict(locals())
    w = {n: env[n] for n in WEIGHT_ORDER}
    mom = {n: env["m_" + n] for n in WEIGHT_ORDER}
    var = {n: env["v_" + n] for n in WEIGHT_ORDER}
    me = _me()
    dev = _lin(me)
    core = jnp.reshape(me[2], (1,)).astype(jnp.int32)
    chip = jnp.reshape(2 * me[0] + me[1], (1,)).astype(jnp.int32)
    d = x.shape[-1]

    def shard(n):
        s = w[BIG_NAME[n]][0].astype(BF16)
        return s.T if n == "mix_in" else s

    dev_idx = jnp.reshape(dev, (1,)).astype(jnp.int32)
    started = {}

    def start(grp, after):
        shards = [shard(n) for n in GATHERS[grp]]
        lands = [place_own(dev_idx, s, f"place_own_{n}") for n, s in zip(GATHERS[grp], shards)]
        started[grp] = gather_start(shards, lands, after, f"gather_start_{grp}")
        return started[grp][-1]

    small_w = all_gather_small(jnp.concatenate([rel_bias[0], w_alpha2[0]], axis=1), "gather_small_w")
    n_rel_cols = rel_bias.shape[-1]
    rel_full = small_w[:, :, :n_rel_cols].transpose(1, 0, 2).reshape(small_w.shape[1], -1)
    wa2_full = small_w[:, :, n_rel_cols:].transpose(1, 0, 2).reshape(small_w.shape[1], -1)

    first, *rest = list(GATHERS)
    order = start(first, small_w[0, :1, :1])
    order, w, mom, var = lax.optimization_barrier((order, w, mom, var))

    ada_cols = w_ada.shape[-1]
    bias = bias_table(rel_full)
    c_all = all_gather_small(c, "gather_c", after=(order, bias))[:, 0, :]
    b_cols = lax.dynamic_slice_in_dim(b_ada, dev * ada_cols, ada_cols, axis=1)
    mod_cols = adaln_cols(c_all, w_ada[0], b_cols, "adaln_cols")
    mod_all = all_gather_small(mod_cols, "gather_mod")
    mod = lax.dynamic_index_in_dim(mod_all, dev, axis=1, keepdims=False).reshape(N_MOD, d)

    order = mod_all[0, :1, :1]
    for grp in rest:
        order = start(grp, order)
    mod = _after(mod, order)

    forwards = {}

    def weights_early(grp, after):
        _, zones = gather_wait(started[grp], after, f"gather_wait_{grp}")
        forwards[grp] = forward_start(zones, f"forward_start_{grp}")
        return forwards[grp][-1]

    def weights_of(grp, after):
        if grp in forwards:
            arrays = forward_wait(forwards[grp], after, f"forward_wait_{grp}")
        else:
            _, zones = gather_wait(started[grp], after, f"gather_wait_{grp}")
            arrays = gather_forward(zones, f"gather_forward_{grp}")
        full = dict(zip(GATHERS[grp], arrays))
        if grp == "mix":
            return dict(in_t=full["mix_in"], proj_a=full["proj_a"], proj_b=full["proj_b"],
                        out=full["mix_out"].reshape(-1, d))
        return {"in" if n.endswith("_in") else "out": v if n.endswith("_in") else v.reshape(-1, d)
                for n, v in full.items()}

    pairs, exchanges = {}, {}
    last = list(GROUPS)[0]

    def to_chips(grp, grads, got):
        sums = [pair_add(core, grads[n], g, f"grad_pair_add_{n}") for n, g in zip(GROUPS[grp], got)]
        exchanges[grp] = exchange_start(sums, chip_routes, f"grad_chip_start_{grp}")
        return exchanges[grp][-1]

    def grads_ready(grp, grads):
        if grp == last:
            return to_chips(grp, grads, pair_exchange([grads[n] for n in GROUPS[grp]], f"grad_pair_exchange_{grp}"))
        pairs[grp] = exchange_start([grads[n] for n in GROUPS[grp]], pair_routes, f"grad_pair_start_{grp}")
        return pairs[grp][-1]

    def grads_sent(grp, after):
        if grp == last:
            return None
        sent, got = exchange_wait(pairs[grp], pair_routes, after, f"grad_pair_wait_{grp}")
        return to_chips(grp, dict(zip(GROUPS[grp], sent)), got)

    lns = [ln1_g, ln1_b, ln2_g, ln2_b, ln3_g, ln3_b]
    loss, dx, dmod, small = local_step(x[0], loss_target[0], mod, weights_of, grads_ready, grads_sent, rel_full,
                                       wa2_full, b_alpha, gla_norm_g, lns, bias=bias, weights_early=weights_early)

    out = {}

    def finish(grp, after):
        sums, recv = exchange_wait(exchanges[grp], chip_routes, after, f"grad_chip_wait_{grp}")
        for n, hsum, r in zip(GROUPS[grp], sums, recv):
            full = BIG_NAME[n]
            if n == "mix_in":
                g = owned_sum(chip, hsum, r, f"owned_sum_{n}").T
                res_n = adamw_sum(g[None], w[full][0], mom[full][0], var[full][0], f"adamw_{n}")
            else:
                res_n = adamw_owned(chip, hsum, r, w[full][0], mom[full][0], var[full][0], f"adamw_{n}")
            out[full] = [o[None] for o in res_n]
            after = res_n[0]
        return after

    order = dx
    for grp in reversed(list(GROUPS)[1:]):
        order = finish(grp, order)

    pieces = (list(dmod) + [small[n].reshape(1, -1) for n in SMALL_REPLICATED[1:] + SMALL_SHARDED] + [loss])
    parts = all_gather_rows(pieces, order, "gather_small_grads")
    loss = jnp.sum(parts[:, 0, parts.shape[2] - loss.shape[1]])
    n_mod = N_MOD * d
    dmod_all = parts[:, 0, :n_mod]
    g_w_ada = adaln_wgrad(c_all, lax.dynamic_slice_in_dim(dmod_all, dev * ada_cols, ada_cols, axis=1), "adaln_wgrad")
    out["w_ada"] = [o[None] for o in adamw_sum(g_w_ada[None], w_ada[0], m_w_ada[0], v_w_ada[0], "adamw_w_ada")]

    sources, where, off = [parts], [], 0
    for n in SMALL_REPLICATED:
        where.append((0, off))
        off += w[n].size
    for n in SMALL_SHARDED:
        rows, cols_local = w[n].shape[1], w[n].shape[2]
        full_part = parts[:, 0, off:off + rows * cols_local * N_DEV].reshape(N_DEV, rows, cols_local * N_DEV)
        mine = lax.dynamic_slice_in_dim(full_part, dev * cols_local, cols_local, axis=2)
        where.append((len(sources), 0))
        sources.append(mine.reshape(N_DEV, 1, rows * cols_local))
        off += rows * cols_local * N_DEV
    names = SMALL_REPLICATED + SMALL_SHARDED
    res = adamw_rows(sources, where, *[[src[n].reshape(1, -1) for n in names] for src in (w, mom, var)],
                     "adamw_small")
    for n, res_n in zip(names, res):
        out[n] = [r.reshape(w[n].shape) for r in res_n]

    finish(last, res[0][0])

    flat = [loss, dx[None]]
    for k in range(4):
        flat += [out[n][k] for n in WEIGHT_ORDER]
    return tuple(flat)
```

```python
import functools

import numpy as np
import jax
import jax.numpy as jnp
from jax import lax
from jax.experimental import pallas as pl
from jax.experimental.pallas import tpu as pltpu

F32 = jnp.float32
BF16 = jnp.bfloat16
MESH = pl.DeviceIdType.MESH
N_DEV = 8
N_CHIP = 4

CHUNK = 64
A_PAST_CHUNKS = 8
REL_CLIP = 256
GATE_TAU = 16.0
N_MOD = 9
DEPTH = 1
ALPHA = (2.0 * DEPTH) ** 0.25
LN_EPS = 1e-5
RMS_EPS = 1e-6
ADAM_LR = 0.001
ADAM_B1 = 0.9
ADAM_B2 = 0.999
ADAM_EPS = 1e-08
ADAM_WD = 0.01
ADAM_STEP = 10

LANE = 128
VMEM_LIMIT = 56 * 2 ** 20
QB = 4 * CHUNK
KW = 3 * QB
GB = 8 * CHUNK
NEG = -1e30
HI = lax.Precision.HIGHEST

ANY = pl.BlockSpec(memory_space=pl.ANY)
VMEM_SPEC = pl.BlockSpec(memory_space=pltpu.VMEM)


def _params(*sem):
    return pltpu.CompilerParams(dimension_semantics=sem, vmem_limit_bytes=VMEM_LIMIT)


def _sds(shape, dtype):
    return jax.ShapeDtypeStruct(shape, dtype)


def _dot(a, b):
    return jnp.dot(a, b, preferred_element_type=F32)


def _dot_nt(a, b):
    return lax.dot_general(a, b, (((1,), (1,)), ((), ())), preferred_element_type=F32)


def _dot_tn(a, b):
    return lax.dot_general(a, b, (((0,), (0,)), ((), ())), preferred_element_type=F32)


def _sigmoid(x):
    return 0.5 * jnp.tanh(0.5 * x) + 0.5


def _colsum(x):
    return jnp.sum(x, axis=0, keepdims=True)


def _row_tile(rows, cap, mult):
    for t in range(min(rows, cap), 0, -1):
        if rows % t == 0 and t % mult == 0:
            return t
    return rows


def _me():
    return lax.axis_index("x"), lax.axis_index("y"), lax.axis_index("c")


def _flip(me, k):
    return tuple((1 - p) if (k >> s) & 1 else p for p, s in zip(me, (2, 1, 0)))


def _lin(p):
    return 4 * p[0] + 2 * p[1] + p[2]


def _gather_direct(x_ref, out_ref, send_sems, recv_sems, local_sem):
    me = _me()
    mine = pltpu.make_async_copy(x_ref, out_ref.at[_lin(me)], local_sem)
    mine.start()
    sends = []
    for k in range(1, N_DEV):
        cp = pltpu.make_async_remote_copy(
            src_ref=x_ref, dst_ref=out_ref.at[_lin(me)], send_sem=send_sems.at[k - 1],
            recv_sem=recv_sems.at[k - 1], device_id=_flip(me, k), device_id_type=MESH)
        cp.start()
        sends.append(cp)
    for k in range(1, N_DEV):
        peer = _flip(me, k)
        pltpu.make_async_remote_copy(
            src_ref=x_ref, dst_ref=out_ref.at[_lin(peer)], send_sem=send_sems.at[k - 1],
            recv_sem=recv_sems.at[k - 1], device_id=peer, device_id_type=MESH).wait_recv()
    for cp in sends:
        cp.wait_send()
    mine.wait()


GATHER_SEMS = [pltpu.SemaphoreType.DMA((N_DEV - 1,)), pltpu.SemaphoreType.DMA((N_DEV - 1,)), pltpu.SemaphoreType.DMA]


def all_gather_small(x, name, after=()):
    r, n = x.shape

    def body(x_ref, *refs):
        _gather_direct(x_ref, *refs[len(after):])

    return pl.pallas_call(
        body, name=name, out_shape=_sds((N_DEV, r, n), x.dtype),
        in_specs=[VMEM_SPEC] + [ANY] * len(after), out_specs=VMEM_SPEC, scratch_shapes=GATHER_SEMS,
    )(x, *after)


def all_gather_rows(pieces, after, name):
    sizes = [x.shape[1] for x in pieces]
    total = sum(sizes)
    assert all(n % LANE == 0 for n in sizes)

    def body(*refs):
        ins = refs[:len(pieces)]
        out_ref, row, send_sems, recv_sems, local_sem = refs[len(pieces) + 1:]
        off = 0
        for x_ref, n in zip(ins, sizes):
            row[:, off:off + n] = x_ref[...]
            off += n
        _gather_direct(row, out_ref, send_sems, recv_sems, local_sem)

    return pl.pallas_call(
        body, name=name, out_shape=_sds((N_DEV, 1, total), F32),
        in_specs=[VMEM_SPEC] * len(pieces) + [ANY], out_specs=VMEM_SPEC,
        scratch_shapes=[pltpu.VMEM((1, total), F32)] + GATHER_SEMS,
    )(*pieces, after)


HBM_SPEC = pl.BlockSpec(memory_space=pltpu.HBM)
SEM_SPEC = pl.BlockSpec(memory_space=pltpu.SEMAPHORE)
EFFECT = pltpu.SideEffectType.DATAFLOW_SIDE_EFFECTING
FIRST = N_CHIP


def _hbm(v):
    return pltpu.with_memory_space_constraint(v, pltpu.HBM)


def _other_chips(mx, my):
    return [(1 - mx, my), (mx, 1 - my), (1 - mx, 1 - my)]


def place_own(dev, shard, name):
    rows, cols = shard.shape
    tr, tc = _tile2(rows, cols, 16)

    def body(dev_ref, s_ref, land_ref, o_ref):
        o_ref[...] = s_ref[...]

    land = lax.empty((N_DEV, rows, cols), shard.dtype)
    return pl.pallas_call(
        body, name=name, out_shape=_sds(land.shape, land.dtype),
        grid_spec=pltpu.PrefetchScalarGridSpec(
            num_scalar_prefetch=1, grid=(rows // tr, cols // tc),
            in_specs=[pl.BlockSpec((tr, tc), lambda i, j, d: (i, j)), ANY],
            out_specs=pl.BlockSpec((None, tr, tc), lambda i, j, d: (d[0], i, j))),
        input_output_aliases={2: 0}, compiler_params=_params("parallel", "parallel"),
    )(dev, shard, land)


def gather_start(shards, lands, after, name):
    n = len(shards)

    def body(*refs):
        ins, zones = refs[:n], refs[n:2 * n]
        send_sems, recv_sems = refs[2 * n + 1], refs[2 * n + 2]
        token = refs[-1]
        me = _me()
        mx, my, mc = me
        for a in range(n):
            dst = zones[a].at[_lin(me)]
            targets = [(mx, my, 1 - mc)] + [(*chip, mc) for chip in _other_chips(mx, my)]
            for k, to in enumerate(targets):
                pltpu.make_async_remote_copy(
                    src_ref=ins[a], dst_ref=dst, send_sem=send_sems.at[a * FIRST + k],
                    recv_sem=recv_sems.at[a * FIRST + k], device_id=to, device_id_type=MESH).start()
        token[...] = jnp.zeros_like(token)

    sems = pltpu.SemaphoreType.DMA((n * FIRST,))
    out = pl.pallas_call(
        body, name=name,
        out_shape=(sems, sems, *[pltpu.HBM(s.shape, s.dtype) for s in shards],
                   *[pltpu.HBM(z.shape, z.dtype) for z in lands], _sds((8, LANE), F32)),
        in_specs=[HBM_SPEC] * (2 * n) + [ANY],
        out_specs=(SEM_SPEC, SEM_SPEC, *[HBM_SPEC] * (2 * n), VMEM_SPEC),
        input_output_aliases={a: 2 + a for a in range(2 * n)},
        compiler_params=pltpu.CompilerParams(has_side_effects=EFFECT),
    )(*[_hbm(s) for s in shards], *[_hbm(z) for z in lands], after)
    return out[0], out[1], out[2:2 + n], out[2 + n:2 + 2 * n], out[-1]


def gather_wait(started, after, name):
    send_sems, recv_sems, shards, lands, _ = started
    n = len(shards)

    def body(*refs):
        ins, zones = refs[:n], refs[n:2 * n]
        send_ref, recv_ref = refs[2 * n], refs[2 * n + 1]
        mx, my, mc = _me()
        for a in range(n):
            for k in range(FIRST):
                cp = pltpu.make_async_remote_copy(
                    src_ref=ins[a], dst_ref=zones[a].at[0], send_sem=send_ref.at[a * FIRST + k],
                    recv_sem=recv_ref.at[a * FIRST + k], device_id=(mx, my, 1 - mc), device_id_type=MESH)
                cp.wait_send()
                cp.wait_recv()

    out = pl.pallas_call(
        body, name=name,
        out_shape=(*[pltpu.HBM(s.shape, s.dtype) for s in shards], *[pltpu.HBM(z.shape, z.dtype) for z in lands]),
        in_specs=[HBM_SPEC] * (2 * n) + [SEM_SPEC, SEM_SPEC, ANY], out_specs=tuple([HBM_SPEC] * (2 * n)),
        input_output_aliases={a: a for a in range(2 * n)},
        compiler_params=pltpu.CompilerParams(has_side_effects=EFFECT),
    )(*shards, *lands, send_sems, recv_sems, after)
    return out[:n], out[n:]


def gather_forward(lands, name):
    n = len(lands)
    rel = N_CHIP - 1

    def body(*refs):
        zones, outs = refs[:n], refs[n:2 * n]
        send_sems, recv_sems = refs[2 * n:]
        mx, my, mc = _me()
        chips = _other_chips(mx, my)

        def copy(a, j, core):
            blk = _lin((*chips[j], core))
            return pltpu.make_async_remote_copy(
                src_ref=zones[a].at[blk], dst_ref=outs[a].at[blk], send_sem=send_sems.at[a * rel + j],
                recv_sem=recv_sems.at[a * rel + j], device_id=(mx, my, 1 - mc), device_id_type=MESH)

        sends = [copy(a, j, mc) for a in range(n) for j in range(rel)]
        for cp in sends:
            cp.start()
        for a in range(n):
            for j in range(rel):
                copy(a, j, 1 - mc).wait_recv()
        for cp in sends:
            cp.wait_send()

    return pl.pallas_call(
        body, name=name, out_shape=[_sds(z.shape, z.dtype) for z in lands],
        in_specs=[ANY] * n, out_specs=[ANY] * n, input_output_aliases={a: a for a in range(n)},
        scratch_shapes=[pltpu.SemaphoreType.DMA((n * rel,)), pltpu.SemaphoreType.DMA((n * rel,))],
    )(*lands)


def forward_start(lands, name):
    n = len(lands)
    rel = N_CHIP - 1

    def body(*refs):
        zones = refs[:n]
        send_sems, recv_sems = refs[n], refs[n + 1]
        token = refs[-1]
        mx, my, mc = _me()
        for a in range(n):
            for j, chip in enumerate(_other_chips(mx, my)):
                blk = zones[a].at[_lin((*chip, mc))]
                pltpu.make_async_remote_copy(
                    src_ref=blk, dst_ref=blk, send_sem=send_sems.at[a * rel + j], recv_sem=recv_sems.at[a * rel + j],
                    device_id=(mx, my, 1 - mc), device_id_type=MESH).start()
        token[...] = jnp.zeros_like(token)

    sems = pltpu.SemaphoreType.DMA((n * rel,))
    out = pl.pallas_call(
        body, name=name,
        out_shape=(sems, sems, *[pltpu.HBM(z.shape, z.dtype) for z in lands], _sds((8, LANE), F32)),
        in_specs=[HBM_SPEC] * n, out_specs=(SEM_SPEC, SEM_SPEC, *[HBM_SPEC] * n, VMEM_SPEC),
        input_output_aliases={a: 2 + a for a in range(n)},
        compiler_params=pltpu.CompilerParams(has_side_effects=EFFECT),
    )(*[_hbm(z) for z in lands])
    return out[0], out[1], out[2:2 + n], out[-1]


def forward_wait(started, after, name):
    send_sems, recv_sems, lands, _ = started
    n = len(lands)
    rel = N_CHIP - 1

    def body(*refs):
        zones = refs[:n]
        send_ref, recv_ref = refs[n], refs[n + 1]
        mx, my, mc = _me()
        for a in range(n):
            for j, chip in enumerate(_other_chips(mx, my)):
                cp = pltpu.make_async_remote_copy(
                    src_ref=zones[a].at[_lin((*chip, mc))], dst_ref=zones[a].at[_lin((*chip, 1 - mc))],
                    send_sem=send_ref.at[a * rel + j], recv_sem=recv_ref.at[a * rel + j],
                    device_id=(mx, my, 1 - mc), device_id_type=MESH)
                cp.wait_send()
                cp.wait_recv()

    out = pl.pallas_call(
        body, name=name, out_shape=tuple(pltpu.HBM(z.shape, z.dtype) for z in lands),
        in_specs=[HBM_SPEC] * n + [SEM_SPEC, SEM_SPEC, ANY], out_specs=tuple([HBM_SPEC] * n),
        input_output_aliases={a: a for a in range(n)},
        compiler_params=pltpu.CompilerParams(has_side_effects=EFFECT),
    )(*lands, send_sems, recv_sems, after)
    return list(out)


def pair_exchange(gs, name):
    n = len(gs)

    def body(*refs):
        ins, outs = refs[:n], refs[n:2 * n]
        send_sems, recv_sems = refs[2 * n:]
        mx, my, mc = _me()
        cps = []
        for a in range(n):
            for q in range(N_CHIP):
                cp = pltpu.make_async_remote_copy(
                    src_ref=ins[a].at[2 * q + (1 - mc)], dst_ref=outs[a].at[q],
                    send_sem=send_sems.at[a * N_CHIP + q], recv_sem=recv_sems.at[a * N_CHIP + q],
                    device_id=(mx, my, 1 - mc), device_id_type=MESH)
                cp.start()
                cps.append(cp)
        for cp in cps:
            cp.wait()

    return pl.pallas_call(
        body, name=name, out_shape=[_sds((N_CHIP,) + g.shape[1:], g.dtype) for g in gs],
        in_specs=[ANY] * n, out_specs=[ANY] * n,
        scratch_shapes=[pltpu.SemaphoreType.DMA((n * N_CHIP,)), pltpu.SemaphoreType.DMA((n * N_CHIP,))],
    )(*gs)


def _tile2(rows, cols, row_mult):
    tr = _row_tile(rows, 512, row_mult)
    if tr < rows or rows * cols <= 2 ** 20:
        return tr, cols
    return rows, _row_tile(cols, 512, LANE)


def pair_add(core, g, got, name):
    _, rows, cols = g.shape
    tr, tc = _tile2(rows, cols, 16)

    def body(core_ref, g_ref, got_ref, h_ref):
        h_ref[...] = (g_ref[...].astype(F32) + got_ref[...].astype(F32)).astype(h_ref.dtype)

    blk = pl.BlockSpec((None, tr, tc), lambda q, i, j, c: (q, i, j))
    return pl.pallas_call(
        body, name=name, out_shape=_sds((N_CHIP, rows, cols), g.dtype),
        grid_spec=pltpu.PrefetchScalarGridSpec(
            num_scalar_prefetch=1, grid=(N_CHIP, rows // tr, cols // tc),
            in_specs=[pl.BlockSpec((None, tr, tc), lambda q, i, j, c: (2 * q + c[0], i, j)), blk],
            out_specs=blk),
        compiler_params=_params("parallel", "parallel", "parallel"),
    )(core, g, got)


def chip_routes(mx, my, mc):
    return [(2 * px + py, k, (px, py, mc)) for k, (px, py) in enumerate(_other_chips(mx, my))]


def pair_routes(mx, my, mc):
    return [(2 * q + (1 - mc), q, (mx, my, 1 - mc)) for q in range(N_CHIP)]


def exchange_start(hs, routes, name):
    n = len(hs)
    rel = len(routes(0, 0, 0))
    lands = [lax.empty((rel,) + h.shape[1:], h.dtype) for h in hs]

    def body(*refs):
        ins, zones = refs[:n], refs[n:2 * n]
        send_sems, recv_sems = refs[2 * n], refs[2 * n + 1]
        token = refs[-1]
        for a in range(n):
            for k, (src, slot, to) in enumerate(routes(*_me())):
                pltpu.make_async_remote_copy(
                    src_ref=ins[a].at[src], dst_ref=zones[a].at[slot], send_sem=send_sems.at[a * rel + k],
                    recv_sem=recv_sems.at[a * rel + k], device_id=to, device_id_type=MESH).start()
        token[...] = jnp.zeros_like(token)

    sems = pltpu.SemaphoreType.DMA((n * rel,))
    out = pl.pallas_call(
        body, name=name,
        out_shape=(sems, sems, *[pltpu.HBM(h.shape, h.dtype) for h in hs],
                   *[pltpu.HBM(z.shape, z.dtype) for z in lands], _sds((8, LANE), F32)),
        in_specs=[HBM_SPEC] * (2 * n), out_specs=(SEM_SPEC, SEM_SPEC, *[HBM_SPEC] * (2 * n), VMEM_SPEC),
        input_output_aliases={a: 2 + a for a in range(2 * n)},
        compiler_params=pltpu.CompilerParams(has_side_effects=EFFECT),
    )(*[_hbm(h) for h in hs], *[_hbm(z) for z in lands])
    return out[0], out[1], out[2:2 + n], out[2 + n:2 + 2 * n], out[-1]


def exchange_wait(started, routes, after, name):
    send_sems, recv_sems, hs, lands, _ = started
    n = len(hs)
    rel = len(routes(0, 0, 0))

    def body(*refs):
        ins, zones = refs[:n], refs[n:2 * n]
        send_ref, recv_ref = refs[2 * n], refs[2 * n + 1]
        for a in range(n):
            for k, (src, slot, to) in enumerate(routes(*_me())):
                cp = pltpu.make_async_remote_copy(
                    src_ref=ins[a].at[src], dst_ref=zones[a].at[slot], send_sem=send_ref.at[a * rel + k],
                    recv_sem=recv_ref.at[a * rel + k], device_id=to, device_id_type=MESH)
                cp.wait_send()
                cp.wait_recv()

    out = pl.pallas_call(
        body, name=name,
        out_shape=(*[pltpu.HBM(h.shape, h.dtype) for h in hs], *[pltpu.HBM(z.shape, z.dtype) for z in lands]),
        in_specs=[HBM_SPEC] * (2 * n) + [SEM_SPEC, SEM_SPEC, ANY], out_specs=tuple([HBM_SPEC] * (2 * n)),
        input_output_aliases={a: a for a in range(2 * n)},
        compiler_params=pltpu.CompilerParams(has_side_effects=EFFECT),
    )(*hs, *lands, send_sems, recv_sems, after)
    return out[:n], out[n:]


def _adam(w, g, m, v):
    m = ADAM_B1 * m + (1.0 - ADAM_B1) * g
    v = ADAM_B2 * v + (1.0 - ADAM_B2) * (g * g)
    m_hat = m / (1.0 - ADAM_B1 ** ADAM_STEP)
    v_hat = v / (1.0 - ADAM_B2 ** ADAM_STEP)
    delta = -ADAM_LR * (m_hat / (jnp.sqrt(v_hat) + ADAM_EPS) + ADAM_WD * w)
    return delta, m, v


def adamw_owned(chip, h, got, w, m, v, name):
    rows, cols = w.shape
    tr = _row_tile(rows, 256, 16)

    def body(chip_ref, h_ref, got_ref, w_ref, m_ref, v_ref, g_out, d_out, m_out, v_out):
        g = h_ref[...].astype(F32)
        for k in range(N_CHIP - 1):
            g = g + got_ref[k].astype(F32)
        d, mn, vn = _adam(w_ref[...], g, m_ref[...], v_ref[...])
        g_out[...] = g
        d_out[...] = d
        m_out[...] = mn
        v_out[...] = vn

    blk = pl.BlockSpec((tr, cols), lambda i, c: (i, 0))
    return pl.pallas_call(
        body, name=name, out_shape=[_sds((rows, cols), F32)] * 4,
        grid_spec=pltpu.PrefetchScalarGridSpec(
            num_scalar_prefetch=1, grid=(rows // tr,),
            in_specs=[pl.BlockSpec((None, tr, cols), lambda i, c: (c[0], i, 0)),
                      pl.BlockSpec((N_CHIP - 1, tr, cols), lambda i, c: (0, i, 0)), blk, blk, blk],
            out_specs=[blk] * 4),
        compiler_params=_params("parallel"),
    )(chip, h, got, w, m, v)


def owned_sum(chip, h, got, name):
    _, rows, cols = h.shape
    tr, tc = _tile2(rows, cols, 16)

    def body(chip_ref, h_ref, got_ref, g_out):
        g = h_ref[...].astype(F32)
        for k in range(N_CHIP - 1):
            g = g + got_ref[k].astype(F32)
        g_out[...] = g

    return pl.pallas_call(
        body, name=name, out_shape=_sds((rows, cols), F32),
        grid_spec=pltpu.PrefetchScalarGridSpec(
            num_scalar_prefetch=1, grid=(rows // tr, cols // tc),
            in_specs=[pl.BlockSpec((None, tr, tc), lambda i, j, c: (c[0], i, j)),
                      pl.BlockSpec((N_CHIP - 1, tr, tc), lambda i, j, c: (0, i, j))],
            out_specs=pl.BlockSpec((tr, tc), lambda i, j, c: (i, j))),
        compiler_params=_params("parallel", "parallel"),
    )(chip, h, got)


def adamw_rows(sources, where, ws, ms, vs, name):
    n_src, n_par = len(sources), len(ws)

    def body(*refs):
        srcs = refs[:n_src]
        w_refs, m_refs, v_refs = (refs[n_src + j * n_par:n_src + (j + 1) * n_par] for j in range(3))
        outs = refs[n_src + 3 * n_par:]
        for k in range(n_par):
            src, off = srcs[where[k][0]], where[k][1]
            n = w_refs[k].shape[1]
            g = src[0, :, off:off + n]
            for dev in range(1, N_DEV):
                g = g + src[dev, :, off:off + n]
            d, mn, vn = _adam(w_refs[k][...], g, m_refs[k][...], v_refs[k][...])
            for o_ref, val in zip(outs[4 * k:4 * k + 4], (g, d, mn, vn)):
                o_ref[...] = val

    flat = pl.pallas_call(
        body, name=name, out_shape=[_sds(x.shape, F32) for x in ws for _ in range(4)],
        in_specs=[VMEM_SPEC] * (n_src + 3 * n_par), out_specs=[VMEM_SPEC] * (4 * n_par),
        compiler_params=pltpu.CompilerParams(vmem_limit_bytes=VMEM_LIMIT),
    )(*sources, *ws, *ms, *vs)
    return [flat[4 * k:4 * k + 4] for k in range(n_par)]


def adamw_sum(parts, w, m, v, name):
    n_parts, rows, cols = parts.shape
    tr = _row_tile(rows, 256, 8)

    def body(p_ref, w_ref, m_ref, v_ref, g_out, d_out, m_out, v_out):
        g = p_ref[0]
        for k in range(1, n_parts):
            g = g + p_ref[k]
        d, mn, vn = _adam(w_ref[...], g, m_ref[...], v_ref[...])
        g_out[...] = g
        d_out[...] = d
        m_out[...] = mn
        v_out[...] = vn

    blk = pl.BlockSpec((tr, cols), lambda i: (i, 0))
    return pl.pallas_call(
        body, name=name, out_shape=[_sds((rows, cols), F32)] * 4, grid=(rows // tr,),
        in_specs=[pl.BlockSpec((n_parts, tr, cols), lambda i: (0, i, 0)), blk, blk, blk],
        out_specs=[blk] * 4, compiler_params=_params("parallel"),
    )(parts, w, m, v)


def adaln_cols(c_all, w, b, name):
    d, n = w.shape
    tn = _row_tile(n, 768, LANE)

    def body(c_ref, w_ref, b_ref, o_ref):
        c = c_ref[...]
        o_ref[...] = jnp.dot(c * _sigmoid(c), w_ref[...], preferred_element_type=F32, precision=HI) + b_ref[...]

    return pl.pallas_call(
        body, name=name, out_shape=_sds((N_DEV, n), F32), grid=(n // tn,),
        in_specs=[pl.BlockSpec((N_DEV, d), lambda j: (0, 0)), pl.BlockSpec((d, tn), lambda j: (0, j)),
                  pl.BlockSpec((1, tn), lambda j: (0, j))],
        out_specs=pl.BlockSpec((N_DEV, tn), lambda j: (0, j)), compiler_params=_params("parallel"),
    )(c_all, w, b)


def adaln_wgrad(c_all, dmod_cols, name):
    d = c_all.shape[1]
    n = dmod_cols.shape[1]
    tn = _row_tile(n, 768, LANE)

    def body(c_ref, g_ref, o_ref):
        c = c_ref[...]
        o_ref[...] = lax.dot_general(c * _sigmoid(c), g_ref[...], (((0,), (0,)), ((), ())),
                                     preferred_element_type=F32, precision=HI)

    return pl.pallas_call(
        body, name=name, out_shape=_sds((d, n), F32), grid=(n // tn,),
        in_specs=[pl.BlockSpec((N_DEV, d), lambda j: (0, 0)), pl.BlockSpec((N_DEV, tn), lambda j: (0, j))],
        out_specs=pl.BlockSpec((d, tn), lambda j: (0, j)), compiler_params=_params("parallel"),
    )(c_all, dmod_cols)


def _modulate(h, sh, sc):
    return (h * (1.0 + sc) + sh).astype(BF16)


def ffn_in(h, sh, sc, w3, name):
    t, d = h.shape
    nb, _, bw = w3.shape
    half = nb // 2
    tm = _row_tile(t, 512, 16)

    def body(h_ref, sh_ref, sc_ref, wa_ref, wb_ref, a_ref, b_ref, s_ref):
        u = _modulate(h_ref[...], sh_ref[...], sc_ref[...])
        a = _dot(u, wa_ref[...])
        b = _dot(u, wb_ref[...])
        a_ref[...] = a.astype(BF16)
        b_ref[...] = b.astype(BF16)
        s_ref[...] = (a * _sigmoid(a) * b).astype(BF16)

    vec = pl.BlockSpec((1, d), lambda j, i: (0, 0))
    out = pl.BlockSpec((tm, bw), lambda j, i: (i, j))
    return pl.pallas_call(
        body, name=name, out_shape=[_sds((t, half * bw), BF16)] * 3, grid=(half, t // tm),
        in_specs=[pl.BlockSpec((tm, d), lambda j, i: (i, 0)), vec, vec,
                  pl.BlockSpec((None, d, bw), lambda j, i: (j, 0, 0), pipeline_mode=pl.Buffered(1)),
                  pl.BlockSpec((None, d, bw), lambda j, i: (j + half, 0, 0), pipeline_mode=pl.Buffered(1))],
        out_specs=[out] * 3, compiler_params=_params("parallel", "parallel"),
    )(h, sh, sc, w3, w3)


def mod_matmul(h, sh, sc, wt, bw, name):
    t, d = h.shape
    n = wt.shape[0]
    tm = _row_tile(t, 512, 16)

    def body(h_ref, sh_ref, sc_ref, w_ref, o_ref):
        o_ref[...] = _dot_nt(_modulate(h_ref[...], sh_ref[...], sc_ref[...]), w_ref[...]).astype(BF16)

    vec = pl.BlockSpec((1, d), lambda j, i: (0, 0))
    return pl.pallas_call(
        body, name=name, out_shape=_sds((t, n), BF16), grid=(n // bw, t // tm),
        in_specs=[pl.BlockSpec((tm, d), lambda j, i: (i, 0)), vec, vec, pl.BlockSpec((bw, d), lambda j, i: (j, 0))],
        out_specs=pl.BlockSpec((tm, bw), lambda j, i: (i, j)), compiler_params=_params("parallel", "parallel"),
    )(h, sh, sc, wt)


def out_ln(s, w, hin, gmod, ln_g, ln_b, coef, name):
    t, kdim = s.shape
    d = w.shape[1]
    tm = _row_tile(t, 256, 16)

    def body(s_ref, w_ref, hin_ref, gm_ref, g_ref, b_ref, f_ref, z_ref, h_ref):
        f = _dot(s_ref[...], w_ref[...])
        z = ALPHA * hin_ref[...] + (coef * gm_ref[...]) * f
        mu = jnp.mean(z, axis=-1, keepdims=True)
        zc = z - mu
        var = jnp.mean(zc * zc, axis=-1, keepdims=True)
        f_ref[...] = f.astype(BF16)
        z_ref[...] = z
        h_ref[...] = zc * lax.rsqrt(var + LN_EPS) * g_ref[...] + b_ref[...]

    vec = pl.BlockSpec((1, d), lambda i: (0, 0))
    row = pl.BlockSpec((tm, d), lambda i: (i, 0))
    return pl.pallas_call(
        body, name=name, out_shape=[_sds((t, d), BF16), _sds((t, d), F32), _sds((t, d), F32)],
        grid=(t // tm,),
        in_specs=[pl.BlockSpec((tm, kdim), lambda i: (i, 0)),
                  pl.BlockSpec((kdim, d), lambda i: (0, 0), pipeline_mode=pl.Buffered(1)), row, vec, vec, vec],
        out_specs=[row, row, row], compiler_params=_params("parallel"),
    )(s, w, hin, gmod, ln_g, ln_b)


def ln_bwd(dh, z, f, ln_g, gmod, coef, name, target=None):
    t, d = z.shape
    tm = _row_tile(t, 256, 16)
    head = target is not None

    def body(*refs):
        if head:
            dh_ref, tg_ref, z_ref, f_ref, g_ref, gm_ref, dz_ref, df_ref, dg_ref, db_ref, dgm_ref, loss_ref = refs
        else:
            dh_ref, z_ref, f_ref, g_ref, gm_ref, dz_ref, df_ref, dg_ref, db_ref, dgm_ref = refs
        i = pl.program_id(0)

        @pl.when(i == 0)
        def _():
            dg_ref[...] = jnp.zeros_like(dg_ref)
            db_ref[...] = jnp.zeros_like(db_ref)
            dgm_ref[...] = jnp.zeros_like(dgm_ref)
            if head:
                loss_ref[...] = jnp.zeros_like(loss_ref)

        dh = dh_ref[...]
        if head:
            err = dh - tg_ref[...]
            loss_ref[...] += 0.5 * jnp.sum(jnp.mean(err * err, axis=-1, keepdims=True))
            dh = err / d
        zv = z_ref[...]
        mu = jnp.mean(zv, axis=-1, keepdims=True)
        zc = zv - mu
        rstd = lax.rsqrt(jnp.mean(zc * zc, axis=-1, keepdims=True) + LN_EPS)
        xhat = zc * rstd
        dxh = dh * g_ref[...]
        dz = rstd * (dxh - jnp.mean(dxh, axis=-1, keepdims=True)
                     - xhat * jnp.mean(dxh * xhat, axis=-1, keepdims=True))
        dz_ref[...] = dz
        df_ref[...] = ((coef * gm_ref[...]) * dz).astype(BF16)
        dg_ref[...] += _colsum(dh * xhat)
        db_ref[...] += _colsum(dh)
        dgm_ref[...] += _colsum(coef * f_ref[...].astype(F32) * dz)

    vec = pl.BlockSpec((1, d), lambda i: (0, 0))
    row = pl.BlockSpec((tm, d), lambda i: (i, 0))
    ins = [dh] + ([target] if head else []) + [z, f, ln_g, gmod]
    in_specs = [row] + ([row] if head else []) + [row, row, vec, vec]
    out_shape = [_sds((t, d), F32), _sds((t, d), BF16)] + [_sds((1, d), F32)] * 3
    out_specs = [row, row, vec, vec, vec]
    if head:
        out_shape.append(_sds((1, LANE), F32))
        out_specs.append(pl.BlockSpec((1, LANE), lambda i: (0, 0)))
    return pl.pallas_call(
        body, name=name, out_shape=out_shape, grid=(t // tm,), in_specs=in_specs, out_specs=out_specs,
        compiler_params=_params("arbitrary"),
    )(*ins)


def ffn_bwd_act(df, w, a, b, name):
    t, d = df.shape
    fdim = w.shape[0]
    bw = fdim // (N_DEV // 2)
    tm = _row_tile(t, 512, 16)

    def body(df_ref, w_ref, a_ref, b_ref, o_ref):
        ds = _dot_nt(df_ref[...], w_ref[...])
        av = a_ref[...].astype(F32)
        sg = _sigmoid(av)
        o_ref[0] = (ds * b_ref[...].astype(F32) * (sg * (1.0 + av * (1.0 - sg)))).astype(BF16)
        o_ref[1] = (ds * (av * sg)).astype(BF16)

    act = pl.BlockSpec((tm, bw), lambda j, i: (i, j))
    return pl.pallas_call(
        body, name=name, out_shape=_sds((2, t, fdim), BF16), grid=(fdim // bw, t // tm),
        in_specs=[pl.BlockSpec((tm, d), lambda j, i: (i, 0)), pl.BlockSpec((bw, d), lambda j, i: (j, 0)), act, act],
        out_specs=pl.BlockSpec((2, tm, bw), lambda j, i: (0, i, j)),
        compiler_params=_params("parallel", "parallel"),
    )(df, w, a, b)


def matmul_tn(name, a, a_block, a_map, b, b_block, b_map, out_shape, o_block, o_map, n_out, mod=None,
              mod_b=False):
    tk = [s for s in a_block if s is not None][0]
    nk = a.shape[-2] // tk
    m, nn = [s for s in o_block if s is not None]

    def body(*refs):
        if mod is None:
            a_ref, b_ref, o_ref, acc = refs
        else:
            a_ref, sh_ref, sc_ref, b_ref, o_ref, acc = refs
        k = pl.program_id(1)

        @pl.when(k == 0)
        def _():
            acc[...] = jnp.zeros_like(acc)

        av, bv = a_ref[...], b_ref[...]
        if mod is not None and mod_b:
            bv = _modulate(bv, sh_ref[...], sc_ref[...])
        elif mod is not None:
            av = _modulate(av, sh_ref[...], sc_ref[...])
        acc[...] += _dot_tn(av, bv)

        @pl.when(k == nk - 1)
        def _():
            o_ref[...] = acc[...].astype(o_ref.dtype)

    ins = [a] + (list(mod) if mod is not None else []) + [b]
    in_specs = [pl.BlockSpec(a_block, a_map)]
    if mod is not None:
        vec = pl.BlockSpec((1, mod[0].shape[1]), lambda n, k: (0, 0))
        in_specs += [vec, vec]
    in_specs.append(pl.BlockSpec(b_block, b_map))
    return pl.pallas_call(
        body, name=name, out_shape=out_shape, grid=(n_out, nk), in_specs=in_specs,
        out_specs=pl.BlockSpec(o_block, o_map), scratch_shapes=[pltpu.VMEM((m, nn), F32)],
        compiler_params=_params("parallel", "arbitrary"),
    )(*ins)


def matmul_tn_whole(name, a, a_block, a_map, b, b_block, b_map, out_shape, o_block, o_map, grid, mod, mod_width,
                    mod_b, once):
    def body(a_ref, sh_ref, sc_ref, b_ref, o_ref):
        av, bv = a_ref[...], b_ref[...]
        if mod_b:
            bv = _modulate(bv, sh_ref[...], sc_ref[...])
        else:
            av = _modulate(av, sh_ref[...], sc_ref[...])
        o_ref[...] = _dot_tn(av, bv).astype(o_ref.dtype)

    def spec(block, index_map, single):
        return pl.BlockSpec(block, index_map, pipeline_mode=pl.Buffered(1)) if single else pl.BlockSpec(block, index_map)

    vec = pl.BlockSpec((1, mod_width), lambda n, m: (0, m))
    return pl.pallas_call(
        body, name=name, out_shape=out_shape, grid=grid,
        in_specs=[spec(a_block, a_map, once == "a"), vec, vec, spec(b_block, b_map, once == "b")],
        out_specs=pl.BlockSpec(o_block, o_map), compiler_params=_params("parallel", "arbitrary"),
    )(a, mod[0], mod[1], b)


def block_input_grad(dy, w3, name):
    t = dy.shape[0]
    nb, n, bw = w3.shape
    tm = _row_tile(t, 512, 16)

    def body(dy_ref, w_ref, o_ref):
        o_ref[...] = sum(_dot_nt(dy_ref[:, k * bw:(k + 1) * bw], w_ref[k]) for k in range(nb)).astype(BF16)

    return pl.pallas_call(
        body, name=name, out_shape=_sds((t, n), BF16), grid=(t // tm,),
        in_specs=[pl.BlockSpec((tm, nb * bw), lambda i: (i, 0)),
                  pl.BlockSpec((nb, n, bw), lambda i: (0, 0, 0), pipeline_mode=pl.Buffered(1))],
        out_specs=pl.BlockSpec((tm, n), lambda i: (i, 0)), compiler_params=_params("parallel"),
    )(dy, w3)


def row_input_grad(dy, w, parts, dz, hin, sc, name, after=None):
    t, kdim = dy.shape
    n = w.shape[1]
    kp = kdim // parts
    tm = _row_tile(t, 256, 16)
    row = pl.BlockSpec((tm, n), lambda i: (i, 0))
    vec = pl.BlockSpec((1, n), lambda i: (0, 0))

    def call(c, part):
        final = c == parts - 1

        order = [after] if c == 0 and after is not None else []

        def body(dy_ref, w_ref, *refs):
            du = _dot(dy_ref[...], w_ref[...])
            if c > 0:
                du = du + refs[0][...]
            if not final:
                refs[-1][...] = du
                return
            dz_ref, hin_ref, sc_ref, o_ref, dsc_ref, dsh_ref = refs[-6:]

            @pl.when(pl.program_id(0) == 0)
            def _():
                dsc_ref[...] = jnp.zeros_like(dsc_ref)
                dsh_ref[...] = jnp.zeros_like(dsh_ref)

            o_ref[...] = ALPHA * dz_ref[...] + du * (1.0 + sc_ref[...])
            dsc_ref[...] += _colsum(du * hin_ref[...])
            dsh_ref[...] += _colsum(du)

        ins = [dy, w] + ([part] if c > 0 else []) + order + ([dz, hin, sc] if final else [])
        in_specs = [pl.BlockSpec((tm, kp), lambda i: (i, c)),
                    pl.BlockSpec((kp, n), lambda i: (c, 0), pipeline_mode=pl.Buffered(1))]
        in_specs += ([row] if c > 0 else []) + [ANY] * len(order) + ([row, row, vec] if final else [])
        return pl.pallas_call(
            body, name=f"{name}_{c}", grid=(t // tm,), in_specs=in_specs,
            out_shape=[_sds((t, n), F32), _sds((1, n), F32), _sds((1, n), F32)] if final else _sds((t, n), F32),
            out_specs=[row, vec, vec] if final else row,
            compiler_params=_params("arbitrary" if final else "parallel"),
        )(*ins)

    part = None
    for c in range(parts):
        part = call(c, part)
    return part


REL_W = KW + QB


def bias_table(rel_bias):
    nh, n_rel = rel_bias.shape
    lo = KW - QB - REL_CLIP
    hi = KW - lo - n_rel
    assert n_rel == REL_CLIP + CHUNK and lo >= 0 and hi >= 0
    first, last = rel_bias[:, :1], rel_bias[:, -1:]
    row = jnp.concatenate([jnp.broadcast_to(first, (nh, lo)), rel_bias, jnp.broadcast_to(last, (nh, hi)),
                           jnp.broadcast_to(first, (nh, QB))], axis=1)
    table = jnp.tile(row, (1, QB))[:, :QB * (REL_W - 1)].reshape(nh, QB, REL_W - 1)[:, :, :KW]
    q = np.arange(QB)[:, None] // CHUNK
    k = np.arange(KW)[None, :] // CHUNK
    band = (k >= q) & (k <= q + A_PAST_CHUNKS)
    return jnp.where(band[None], table, NEG)


def bias_grad_skew(dbias):
    nh = dbias.shape[0]
    flat = jnp.pad(dbias, ((0, 0), (0, 0), (0, REL_W - 1 - KW))).reshape(nh, QB * (REL_W - 1))
    return jnp.pad(flat, ((0, 0), (0, QB))).reshape(nh, QB, REL_W)


def bias_clip_map(n_rel):
    m = np.arange(REL_W)
    dist = np.where(m < KW, m, m - REL_W) - (KW - QB)
    idx = np.clip(dist, -REL_CLIP, CHUNK - 1) + REL_CLIP
    return (idx[:, None] == np.arange(n_rel)[None, :]).astype(np.float32)


PAIR = 2


def _pair_specs(col, rows_of):
    return [pl.BlockSpec((QB, LANE), functools.partial(lambda r, h, i: (rows_of(r, i), col // LANE + h), r))
            for r in range(3)]


def _earlier(r, i):
    return jnp.maximum(i - 2 + r, 0)


def _head_lanes(hh, dh):
    lane = lax.broadcasted_iota(jnp.int32, (1, LANE), 1)
    return (lane < dh) if hh == 0 else (lane >= dh)


def _only(x, lanes):
    return jnp.where(lanes, x, jnp.zeros_like(x))


def _scores(q, ks, bias, i, scale):
    s = jnp.concatenate([_dot_nt(q, kk) for kk in ks], axis=1) * scale + bias
    col = lax.broadcasted_iota(jnp.int32, s.shape, 1)
    return jnp.where(col >= (2 - i) * QB, s, NEG)


def attn_fwd(p, cols, bias, dh, name):
    t = p.shape[0]
    nh = bias.shape[0]
    scale = dh ** -0.5

    def body(q_ref, k0, k1, k2, v0, v1, v2, b_ref, o_ref, lse_ref):
        i = pl.program_id(1)
        q = q_ref[...]
        outs = []
        for hh in range(PAIR):
            lanes = _head_lanes(hh, dh)
            s = _scores(q, [_only(kk[...], lanes) for kk in (k0, k1, k2)], b_ref[hh], i, scale)
            m = jnp.max(s, axis=-1, keepdims=True)
            e = jnp.exp(s - m)
            l = jnp.sum(e, axis=-1, keepdims=True)
            eb = e.astype(BF16)
            o = sum(_dot(eb[:, r * QB:(r + 1) * QB], vv[...]) for r, vv in enumerate((v0, v1, v2)))
            outs.append(o / l)
            lse_ref[hh] = m + jnp.log(l)
        o_ref[...] = jnp.where(_head_lanes(0, dh), outs[0], outs[1]).astype(BF16)

    st = pl.BlockSpec((PAIR, QB, 1), lambda h, i: (h, i, 0))
    return pl.pallas_call(
        body, name=name, out_shape=[_sds((t, nh * dh), BF16), _sds((nh, t, 1), F32)], grid=(nh // PAIR, t // QB),
        in_specs=[pl.BlockSpec((QB, LANE), lambda h, i: (i, cols["qa"] // LANE + h))]
        + _pair_specs(cols["ka"], _earlier) + _pair_specs(cols["va"], _earlier)
        + [pl.BlockSpec((PAIR, QB, KW), lambda h, i: (h, 0, 0))],
        out_specs=[pl.BlockSpec((QB, LANE), lambda h, i: (i, h)), st],
        compiler_params=_params("parallel", "parallel"),
    )(p, p, p, p, p, p, p, bias)


def attn_bwd(p, cols, bias, lse, dy, dh, name):
    t = p.shape[0]
    nh = bias.shape[0]
    nb = t // QB
    scale = dh ** -0.5

    def body(q_ref, k0, k1, k2, v0, v1, v2, b_ref, lse_ref, dy_ref, dq_ref, dk_ref, dv_ref, db_ref, dk_acc, dv_acc):
        i = pl.program_id(1)

        @pl.when(i == 0)
        def _():
            db_ref[...] = jnp.zeros_like(db_ref)
            dk_acc[...] = jnp.zeros_like(dk_acc)
            dv_acc[...] = jnp.zeros_like(dv_acc)

        q, dyv = q_ref[...], dy_ref[...]
        ks = [k0[...], k1[...], k2[...]]
        dqs, dks, dvs = [], [], []
        for hh in range(PAIR):
            lanes = _head_lanes(hh, dh)
            s = _scores(q, [_only(kk, lanes) for kk in ks], b_ref[hh], i, scale)
            prob = jnp.exp(s - lse_ref[hh])
            dprob = jnp.concatenate([_dot_nt(dyv, _only(vv[...], lanes)) for vv in (v0, v1, v2)], axis=1)
            delta = jnp.sum(prob * dprob, axis=-1, keepdims=True)
            ds = prob * (dprob - delta)
            dsb, pb = ds.astype(BF16), prob.astype(BF16)
            dqs.append(sum(_dot(dsb[:, r * QB:(r + 1) * QB], kk) for r, kk in enumerate(ks)))
            dks.append([_dot_tn(dsb[:, r * QB:(r + 1) * QB], q) for r in range(3)])
            dvs.append([_dot_tn(pb[:, r * QB:(r + 1) * QB], dyv) for r in range(3)])
            db_ref[hh] += ds
        first = _head_lanes(0, dh)
        dq_ref[...] = (jnp.where(first, dqs[0], dqs[1]) * scale).astype(BF16)
        for r in range(3):
            rows = pl.ds(pl.multiple_of(_earlier(r, i) * QB, QB), QB)
            dk_acc[rows, :] += jnp.where(first, dks[0][r], dks[1][r])
            dv_acc[rows, :] += jnp.where(first, dvs[0][r], dvs[1][r])

        @pl.when(i == nb - 1)
        def _():
            dk_ref[...] = (dk_acc[...] * scale).astype(BF16)
            dv_ref[...] = dv_acc[...].astype(BF16)

    st = pl.BlockSpec((PAIR, QB, 1), lambda h, i: (h, i, 0))
    tab = pl.BlockSpec((PAIR, QB, KW), lambda h, i: (h, 0, 0))
    own = pl.BlockSpec((QB, LANE), lambda h, i: (i, h))
    whole = pl.BlockSpec((t, LANE), lambda h, i: (0, h))
    return pl.pallas_call(
        body, name=name,
        out_shape=[_sds((t, nh * dh), BF16)] * 3 + [_sds((nh, QB, KW), F32)],
        grid=(nh // PAIR, nb),
        in_specs=[pl.BlockSpec((QB, LANE), lambda h, i: (i, cols["qa"] // LANE + h))]
        + _pair_specs(cols["ka"], _earlier) + _pair_specs(cols["va"], _earlier) + [tab, st, own],
        out_specs=[own, whole, whole, tab],
        scratch_shapes=[pltpu.VMEM((t, LANE), F32), pltpu.VMEM((t, LANE), F32)],
        compiler_params=_params("parallel", "arbitrary"),
    )(p, p, p, p, p, p, p, bias, lse, dy)


def _prefix_sums(x, strict):
    r = lax.broadcasted_iota(jnp.int32, (CHUNK, CHUNK), 0)
    c = lax.broadcasted_iota(jnp.int32, (CHUNK, CHUNK), 1)
    tri = jnp.where((c < r) if strict else (c <= r), 1.0, 0.0).astype(BF16)
    n = x.shape[1]
    hi = x.astype(BF16)
    rest = x - hi.astype(F32)
    mid = rest.astype(BF16)
    lo = (rest - mid.astype(F32)).astype(BF16)
    out = _dot(tri, jnp.concatenate([hi, mid, lo], axis=1))
    return out[:, :n] + out[:, n:2 * n] + out[:, 2 * n:]


def _gate(lr, wa, ba):
    y = _dot(lr, wa) + ba
    return (jnp.minimum(y, 0.0) - jnp.log(1.0 + jnp.exp(-jnp.abs(y)))) / GATE_TAU, y


def _decays(la):
    cum = _prefix_sums(la, strict=False)
    last = cum[CHUNK - 1:CHUNK, :]
    return jnp.exp(last - cum), jnp.exp(last)


def _gla_specs(cols, hk, hv, order):
    def at(start, width):
        return pl.BlockSpec((GB, width), lambda h, i: (order(i), start // width + h))
    return [at(cols["qb"], hk), at(cols["kb"], hk), at(cols["vb"], hv), at(cols["rb"], hv),
            pl.BlockSpec((GB, LANE), lambda h, i: (order(i), cols["lr"] // LANE))]


def gla_fwd(p, cols, wa, ba, gn, nh, hk, hv, name):
    t = p.shape[0]
    nc = t // CHUNK
    scale = hk ** -0.5
    per = GB // CHUNK

    def body(q_ref, k_ref, v_ref, r_ref, lr_ref, wa_ref, ba_ref, gn_ref, o_ref, y_ref, st_ref, state):
        @pl.when(pl.program_id(1) == 0)
        def _():
            state[...] = jnp.zeros_like(state)

        for c in range(per):
            rows = pl.ds(c * CHUNK, CHUNK)
            la, _ = _gate(lr_ref[rows, :], wa_ref[...], ba_ref[...])
            w, decay = _decays(la)
            kdec = (k_ref[rows, :].astype(F32) * w).astype(BF16)
            st = decay * state[...] + _dot_tn(v_ref[rows, :], kdec)
            state[...] = st
            st_ref[c] = st
            o = _dot_nt(q_ref[rows, :], st.astype(BF16)) * scale
            o_ref[rows, :] = o
            rinv = lax.rsqrt(jnp.mean(o * o, axis=-1, keepdims=True) + RMS_EPS)
            rv = r_ref[rows, :].astype(F32)
            y_ref[rows, :] = (o * rinv * gn_ref[...] * (rv * _sigmoid(rv))).astype(BF16)

    return pl.pallas_call(
        body, name=name,
        out_shape=[_sds((t, nh * hv), F32), _sds((t, nh * hv), BF16), _sds((nh, nc, hv, hk), F32)],
        grid=(nh, t // GB),
        in_specs=_gla_specs(cols, hk, hv, lambda i: i)
        + [pl.BlockSpec((LANE, hk), lambda h, i: (0, h)), pl.BlockSpec((1, hk), lambda h, i: (0, h)),
           pl.BlockSpec((1, hv), lambda h, i: (0, 0))],
        out_specs=[pl.BlockSpec((GB, hv), lambda h, i: (i, h)), pl.BlockSpec((GB, hv), lambda h, i: (i, h)),
                   pl.BlockSpec((None, per, hv, hk), lambda h, i: (h, i, 0, 0))],
        scratch_shapes=[pltpu.VMEM((hv, hk), F32)], compiler_params=_params("parallel", "arbitrary"),
    )(p, p, p, p, p, wa, ba, gn)


def gla_bwd(p, cols, wa, ba, gn, o, states, dy, nh, hk, hv, name):
    t = p.shape[0]
    nblk = t // GB
    scale = hk ** -0.5
    per = GB // CHUNK

    def rev(i):
        return nblk - 1 - i

    def body(q_ref, k_ref, v_ref, r_ref, lr_ref, wa_ref, ba_ref, gn_ref, o_ref, st_ref, sp_ref, dy_ref,
             dq_ref, dk_ref, dv_ref, dr_ref, dg_ref, dgn_ref, carry):
        h, i = pl.program_id(0), pl.program_id(1)

        @pl.when(i == 0)
        def _():
            carry[...] = jnp.zeros_like(carry)

        @pl.when((i == 0) & (h == 0))
        def _():
            dgn_ref[...] = jnp.zeros_like(dgn_ref)

        gnv = gn_ref[...]
        for c in reversed(range(per)):
            rows = pl.ds(c * CHUNK, CHUNK)
            rv = r_ref[rows, :].astype(F32)
            sg = _sigmoid(rv)
            dyv = dy_ref[rows, :].astype(F32)
            ov = o_ref[rows, :]
            rinv = lax.rsqrt(jnp.mean(ov * ov, axis=-1, keepdims=True) + RMS_EPS)
            dn = dyv * (rv * sg)
            dr_ref[rows, :] = (dyv * (ov * rinv * gnv) * (sg * (1.0 + rv * (1.0 - sg)))).astype(BF16)
            dgn_ref[...] += _colsum(dn * ov * rinv)
            dxh = dn * gnv
            do = rinv * dxh - ov * (rinv * rinv * rinv) * jnp.mean(dxh * ov, axis=-1, keepdims=True)
            dob = (do * scale).astype(BF16)
            qv, kv, vv = q_ref[rows, :], k_ref[rows, :], v_ref[rows, :]
            dq_ref[rows, :] = _dot(dob, st_ref[c].astype(BF16)).astype(BF16)
            dst = carry[...] + _dot_tn(dob, qv)
            if c > 0:
                prev = st_ref[c - 1]
            else:
                prev = jnp.where(i == nblk - 1, 0.0, sp_ref[0])
            ddecay = _colsum(dst * prev)
            la, y = _gate(lr_ref[rows, :], wa_ref[...], ba_ref[...])
            w, decay = _decays(la)
            kf = kv.astype(F32)
            kdec = (kf * w).astype(BF16)
            dstb = dst.astype(BF16)
            dkdec = _dot(vv, dstb)
            dv_ref[rows, :] = _dot_nt(kdec, dstb).astype(BF16)
            dk_ref[rows, :] = (dkdec * w).astype(BF16)
            e = dkdec * kf * w
            dla = _prefix_sums(e, strict=True) + ddecay * decay
            dg_ref[rows, :] = dla * (1.0 / GATE_TAU) * _sigmoid(-y)
            carry[...] = decay * dst

    per_head = lambda width: pl.BlockSpec((GB, width), lambda h, i: (rev(i), h))
    return pl.pallas_call(
        body, name=name,
        out_shape=[_sds((t, nh * hk), BF16), _sds((t, nh * hk), BF16), _sds((t, nh * hv), BF16),
                   _sds((t, nh * hv), BF16), _sds((t, nh * hk), F32), _sds((1, hv), F32)],
        grid=(nh, nblk),
        in_specs=_gla_specs(cols, hk, hv, rev)
        + [pl.BlockSpec((LANE, hk), lambda h, i: (0, h)), pl.BlockSpec((1, hk), lambda h, i: (0, h)),
           pl.BlockSpec((1, hv), lambda h, i: (0, 0)), per_head(hv),
           pl.BlockSpec((None, per, hv, hk), lambda h, i: (h, rev(i), 0, 0)),
           pl.BlockSpec((None, 1, hv, hk), lambda h, i: (h, jnp.maximum(rev(i) * per - 1, 0), 0, 0)),
           per_head(hv)],
        out_specs=[per_head(hk), per_head(hk), per_head(hv), per_head(hv), per_head(hk),
                   pl.BlockSpec((1, hv), lambda h, i: (0, 0))],
        scratch_shapes=[pltpu.VMEM((hv, hk), F32)], compiler_params=_params("arbitrary", "arbitrary"),
    )(p, p, p, p, p, wa, ba, gn, o, states, states, dy)


def gate_bwd(p, lr_col, dg, wa, name):
    t, kd = dg.shape
    tm = _row_tile(t, 512, 16)

    def body(lr_ref, dg_ref, wa_ref, dlr_ref, dwa_ref, dba_ref):
        @pl.when(pl.program_id(0) == 0)
        def _():
            dwa_ref[...] = jnp.zeros_like(dwa_ref)
            dba_ref[...] = jnp.zeros_like(dba_ref)

        g = dg_ref[...]
        gb = g.astype(BF16)
        dlr_ref[...] = _dot_nt(gb, wa_ref[...]).astype(BF16)
        dwa_ref[...] += _dot_tn(lr_ref[...], gb)
        dba_ref[...] += _colsum(g)

    return pl.pallas_call(
        body, name=name, out_shape=[_sds((t, LANE), BF16), _sds((LANE, kd), F32), _sds((1, kd), F32)],
        grid=(t // tm,),
        in_specs=[pl.BlockSpec((tm, LANE), lambda i: (i, lr_col // LANE)), pl.BlockSpec((tm, kd), lambda i: (i, 0)),
                  pl.BlockSpec((LANE, kd), lambda i: (0, 0))],
        out_specs=[pl.BlockSpec((tm, LANE), lambda i: (i, 0)), pl.BlockSpec((LANE, kd), lambda i: (0, 0)),
                   pl.BlockSpec((1, kd), lambda i: (0, 0))],
        compiler_params=_params("arbitrary"),
    )(p, dg, wa)


def proj_merge(ya, yb, wa3, wb3, p, ga_col, gb_col, name):
    t, kd = ya.shape
    nb, _, bw = wa3.shape
    tm = _row_tile(t, 1024, 16)

    def body(ya_ref, yb_ref, wa_ref, wb_ref, ga_ref, gb_ref, pa_ref, pb_ref, mg_ref):
        pa = _dot(ya_ref[...], wa_ref[...])
        pb = _dot(yb_ref[...], wb_ref[...])
        pa_ref[...] = pa.astype(BF16)
        pb_ref[...] = pb.astype(BF16)
        mg_ref[...] = (_sigmoid(ga_ref[...].astype(F32)) * pa + _sigmoid(gb_ref[...].astype(F32)) * pb).astype(BF16)

    act = pl.BlockSpec((tm, kd), lambda j, i: (i, 0))
    wsp = pl.BlockSpec((None, kd, bw), lambda j, i: (j, 0, 0))
    out = pl.BlockSpec((tm, bw), lambda j, i: (i, j))
    return pl.pallas_call(
        body, name=name, out_shape=[_sds((t, nb * bw), BF16)] * 3, grid=(nb, t // tm),
        in_specs=[act, act, wsp, wsp, pl.BlockSpec((tm, bw), lambda j, i: (i, ga_col // bw + j)),
                  pl.BlockSpec((tm, bw), lambda j, i: (i, gb_col // bw + j))],
        out_specs=[out] * 3, compiler_params=_params("parallel", "parallel"),
    )(ya, yb, wa3, wb3, p, p)


def merge_bwd(dm, w, p, ga_col, gb_col, pa, pb, name):
    t, d = dm.shape
    n = w.shape[0]
    tn = _row_tile(n, 512, LANE)
    tm = _row_tile(t, 512, 16)

    def body(dm_ref, w_ref, ga_ref, gb_ref, pa_ref, pb_ref, dpa_ref, dpb_ref, dga_ref, dgb_ref):
        dmg = _dot_nt(dm_ref[...], w_ref[...])
        sa = _sigmoid(ga_ref[...].astype(F32))
        sb = _sigmoid(gb_ref[...].astype(F32))
        dpa_ref[...] = (dmg * sa).astype(BF16)
        dpb_ref[...] = (dmg * sb).astype(BF16)
        dga_ref[...] = (dmg * pa_ref[...].astype(F32) * sa * (1.0 - sa)).astype(BF16)
        dgb_ref[...] = (dmg * pb_ref[...].astype(F32) * sb * (1.0 - sb)).astype(BF16)

    out = pl.BlockSpec((tm, tn), lambda j, i: (i, j))
    return pl.pallas_call(
        body, name=name, out_shape=[_sds((t, n), BF16)] * 4, grid=(n // tn, t // tm),
        in_specs=[pl.BlockSpec((tm, d), lambda j, i: (i, 0)), pl.BlockSpec((tn, d), lambda j, i: (j, 0)),
                  pl.BlockSpec((tm, tn), lambda j, i: (i, ga_col // tn + j)),
                  pl.BlockSpec((tm, tn), lambda j, i: (i, gb_col // tn + j)), out, out],
        out_specs=[out] * 4, compiler_params=_params("parallel", "parallel"),
    )(dm, w, p, p, pa, pb)


def rel_bias_grad(skew, clip_map, name):
    nh, _, jd = skew.shape
    n_rel = clip_map.shape[1]

    def body(s_ref, c_ref, o_ref):
        sums = jnp.concatenate([_colsum(s_ref[h]) for h in range(nh)], axis=0)
        o_ref[...] = jnp.dot(sums, c_ref[...], preferred_element_type=F32, precision=HI)

    return pl.pallas_call(
        body, name=name, out_shape=_sds((nh, n_rel), F32), in_specs=[VMEM_SPEC, VMEM_SPEC], out_specs=VMEM_SPEC,
        compiler_params=pltpu.CompilerParams(vmem_limit_bytes=VMEM_LIMIT),
    )(skew, clip_map)


MIX_BLOCK = 9 * LANE
MIX_PARTS = 3


def mix_layout(d, a_width, bk, bv):
    main = 3 * a_width + 2 * bk + 2 * bv
    cols = {"qa": 0, "ka": a_width, "va": 2 * a_width, "qb": 3 * a_width, "kb": 3 * a_width + bk,
            "vb": 3 * a_width + 2 * bk, "rb": 3 * a_width + 2 * bk + bv, "ga": main, "gb": main + d,
            "lr": main + 2 * d}
    total = main + 2 * d + LANE
    assert total % MIX_BLOCK == 0
    return cols, main, total


def _mix_pieces(per, main, rank, d):
    out = []
    for lo, hi in ((0, main), (main + rank, main + rank + 2 * d), (main, main + rank)):
        while lo < hi:
            cut = min(hi, (lo // per + 1) * per)
            out.append((lo, cut))
            lo = cut
    return out


def mix_weight_in(g3, main, rank):
    d = g3.shape[2]
    flat = g3.reshape(-1, d)
    return jnp.concatenate([flat[:main], flat[main + rank:], flat[main:main + rank],
                            jnp.zeros((LANE - rank, d), g3.dtype)], axis=0)


def mix_weight_grad_out(gt, main, rank, per):
    d = gt.shape[1]
    blocks = [[] for _ in range(N_DEV)]
    pos = 0
    for lo, hi in _mix_pieces(per, main, rank, d):
        blocks[lo // per].append((lo, gt[pos:pos + hi - lo]))
        pos += hi - lo
    return jnp.stack([jnp.concatenate([x for _, x in sorted(b, key=lambda e: e[0])], axis=0) for b in blocks])


def ffn_forward(h, sh, sc, g, w_in3, w_out_of, ln_g, ln_b, tag):
    a, b, s = ffn_in(h, sh, sc, w_in3, f"{tag}_in")
    w_out = w_out_of(s)
    f, z, hout = out_ln(s, w_out, h, g, ln_g, ln_b, 0.5, f"{tag}_out")
    return hout, (h, a, b, s, f, z), w_out


def ffn_backward_weights(dh, saved, sh, sc, g, w_in3, w_out, ln_g, tag, target=None):
    hin, a, b, s, f, z = saved
    t, d = hin.shape
    nb, _, bw = w_in3.shape
    half = nb // 2
    fdim = w_out.shape[0]
    res = ln_bwd(dh, z, f, ln_g, g, 0.5, f"{tag}_ln_bwd", target=target)
    dz, df, dln_g, dln_b, dg = res[:5]
    dab = ffn_bwd_act(df, w_out, a, b, f"{tag}_act_bwd")
    tk = _row_tile(t, 512, 16)
    dw_out = matmul_tn(f"{tag}_dwout", s, (tk, bw), lambda n, k: (k, n), df, (tk, d), lambda n, k: (k, 0),
                       _sds((fdim, d), BF16), (bw, d), lambda n, k: (n, 0), fdim // bw)
    mw = _row_tile(d, 512, LANE)
    dw_in = matmul_tn_whole(f"{tag}_dwin", hin, (t, mw), lambda n, m: (0, m), dab, (None, t, bw),
                            lambda n, m: (n // half, 0, n % half), _sds((nb, d, bw), BF16), (None, mw, bw),
                            lambda n, m: (n, m, 0), (nb, d // mw), (sh, sc), mw, False, "b")
    grads = dict(w_in=dw_in, w_out=dw_out.reshape(N_DEV, fdim // N_DEV, d), ln_g=dln_g, ln_b=dln_b, g=dg)
    return (dab, dz), grads, (res[5] if target is not None else None)


def ffn_backward_input(carry, saved, sc, w_in3, tag, after=None):
    dab, dz = carry
    hin = saved[0]
    t, d = hin.shape
    nb, _, bw = w_in3.shape
    half = nb // 2
    fdim = half * bw
    tm = _row_tile(t, 256, 16)

    def contract(dy_ref, w_ref):
        return sum(_dot_nt(dy_ref[:, j * bw:(j + 1) * bw], w_ref[j]) for j in range(half))

    order = [] if after is None else [after]

    def first(dy_ref, w_ref, *refs):
        refs[-1][...] = contract(dy_ref, w_ref)

    def second(dy_ref, w_ref, part_ref, dz_ref, hin_ref, sc_ref, o_ref, dsc_ref, dsh_ref):
        @pl.when(pl.program_id(0) == 0)
        def _():
            dsc_ref[...] = jnp.zeros_like(dsc_ref)
            dsh_ref[...] = jnp.zeros_like(dsh_ref)

        du = part_ref[...] + contract(dy_ref, w_ref)
        o_ref[...] = ALPHA * dz_ref[...] + du * (1.0 + sc_ref[...])
        dsc_ref[...] += _colsum(du * hin_ref[...])
        dsh_ref[...] += _colsum(du)

    def specs(which):
        return [pl.BlockSpec((None, tm, fdim), lambda i: (which, i, 0)),
                pl.BlockSpec((half, d, bw), lambda i: (which, 0, 0), pipeline_mode=pl.Buffered(1))]

    row = pl.BlockSpec((tm, d), lambda i: (i, 0))
    vec = pl.BlockSpec((1, d), lambda i: (0, 0))
    part = pl.pallas_call(
        first, name=f"{tag}_du_a", out_shape=_sds((t, d), F32), grid=(t // tm,),
        in_specs=specs(0) + [ANY] * len(order), out_specs=row, compiler_params=_params("parallel"),
    )(dab, w_in3, *order)
    return pl.pallas_call(
        second, name=f"{tag}_du_b", out_shape=[_sds((t, d), F32), _sds((1, d), F32), _sds((1, d), F32)],
        grid=(t // tm,), in_specs=specs(1) + [row, row, row, vec], out_specs=[row, vec, vec],
        compiler_params=_params("arbitrary"),
    )(dab, w_in3, part, dz, hin, sc)


def _after(v, token):
    return v if token is None else v + token[:1, :1]


def local_step(x, target, mod, weights_of, grads_ready, grads_sent, rel_bias, w_alpha2, b_alpha, gla_norm_g, lns,
               bias=None, weights_early=None):
    t, d = x.shape
    sh1, sc1, g1, sh2, sc2, g2, sh3, sc3, g3 = [mod[i:i + 1] for i in range(N_MOD)]
    ln1_g, ln1_b, ln2_g, ln2_b, ln3_g, ln3_b = lns
    n_heads_a, n_rel = rel_bias.shape
    rank, bk = w_alpha2.shape
    hv = gla_norm_g.shape[1]

    w1 = weights_of("ffn1_in", mod)
    h1, saved1, w1["out"] = ffn_forward(x, sh1, sc1, g1, w1["in"], lambda s: weights_of("ffn1_out", s)["out"],
                                        ln1_g, ln1_b, "ffn1")
    wm = weights_of("mix", h1)
    a_width = wm["proj_a"].shape[1]
    bv = wm["proj_b"].shape[1]
    nh_b = bv // hv
    hk = bk // nh_b
    cols, main, total = mix_layout(d, a_width, bk, bv)
    w_mix = mix_weight_in(wm["in_t"], main, rank)
    p = mod_matmul(h1, sh2, sc2, w_mix, MIX_BLOCK, "mix_in")
    bias = bias_table(rel_bias) if bias is None else bias
    dh = a_width // n_heads_a
    assert PAIR * dh == LANE
    ya, lse = attn_fwd(p, cols, bias, dh, "attn_fwd")
    token = None if weights_early is None else weights_early("ffn2", ya)
    b_alpha = _after(b_alpha, token)
    wa_pad = jnp.zeros((LANE, bk), BF16).at[:rank].set(w_alpha2.astype(BF16))
    o_b, yb, states = gla_fwd(p, cols, wa_pad, b_alpha, gla_norm_g, nh_b, hk, hv, "gla_fwd")
    pa, pb, merged = proj_merge(ya, yb, wm["proj_a"], wm["proj_b"], p, cols["ga"], cols["gb"], "proj_merge")
    m, z2, h2 = out_ln(merged, wm["out"], h1, g2, ln2_g, ln2_b, 1.0, "mix_out")
    w3 = weights_of("ffn2", h2)
    h3, saved3, _ = ffn_forward(h2, sh3, sc3, g3, w3["in"], lambda s: w3["out"], ln3_g, ln3_b, "ffn2")

    carry3, gr3, loss = ffn_backward_weights(h3, saved3, sh3, sc3, g3, w3["in"], w3["out"], ln3_g, "ffn2",
                                             target=target)
    token = grads_ready("ffn2", dict(ffn2_in=gr3["w_in"], ffn2_out=gr3["w_out"]))
    dh2, dsc3, dsh3 = ffn_backward_input(carry3, saved3, sc3, w3["in"], "ffn2", after=token)
    token = grads_sent("ffn2", dh2)
    dz2, dm, dln2_g, dln2_b, dg2 = ln_bwd(dh2, z2, m, _after(ln2_g, token), g2, 1.0, "mix_ln_bwd")
    dpa, dpb, dga, dgb = merge_bwd(dm, wm["out"], p, cols["ga"], cols["gb"], pa, pb, "merge_bwd")
    tk = _row_tile(t, 512, 16)
    dw_mix_out = matmul_tn("mix_dwout", merged, (tk, 512), lambda n, k: (k, n), dm, (tk, d), lambda n, k: (k, 0),
                           _sds((d, d), BF16), (512, d), lambda n, k: (n, 0), d // 512)
    pbw = wm["proj_a"].shape[2]
    dya = block_input_grad(dpa, wm["proj_a"], "proj_a_dy")
    dyb = block_input_grad(dpb, wm["proj_b"], "proj_b_dy")
    dw_pa = matmul_tn("proj_a_dw", ya, (tk, a_width), lambda n, k: (k, 0), dpa, (tk, pbw), lambda n, k: (k, n),
                      _sds((N_DEV, a_width, pbw), BF16), (None, a_width, pbw), lambda n, k: (n, 0, 0), N_DEV)
    dw_pb = matmul_tn("proj_b_dw", yb, (tk, bv), lambda n, k: (k, 0), dpb, (tk, pbw), lambda n, k: (k, n),
                      _sds((N_DEV, bv, pbw), BF16), (None, bv, pbw), lambda n, k: (n, 0, 0), N_DEV)
    dqb, dkb, dvb, drb, dgate, dgn = gla_bwd(p, cols, wa_pad, b_alpha, gla_norm_g, o_b, states, dyb,
                                             nh_b, hk, hv, "gla_bwd")
    dlr, dwa_pad, dba = gate_bwd(p, cols["lr"], dgate, wa_pad, "gate_bwd")
    dqa, dka, dva, dbias = attn_bwd(p, cols, bias, lse, dya, dh, "attn_bwd")
    d_rel = rel_bias_grad(bias_grad_skew(dbias), jnp.asarray(bias_clip_map(n_rel)), "rel_bias_grad")
    dp = jnp.concatenate([dqa, dka, dva, dqb, dkb, dvb, drb,
                          dga, dgb, dlr], axis=1)
    mw = _row_tile(d, 512, LANE)
    dw_mix_t = matmul_tn_whole("mix_dwin", dp, (t, MIX_BLOCK), lambda n, m: (0, n), h1, (t, mw), lambda n, m: (0, m),
                               _sds((total, d), BF16), (MIX_BLOCK, mw), lambda n, m: (n, m),
                               (total // MIX_BLOCK, d // mw), (sh2, sc2), mw, True, "a")
    dw_mix_in = mix_weight_grad_out(dw_mix_t, main, rank, wm["in_t"].shape[1])
    token = grads_ready("mix", dict(mix_in=dw_mix_in, proj_a=dw_pa, proj_b=dw_pb,
                                    mix_out=dw_mix_out.reshape(N_DEV, d // N_DEV, d)))
    dh1, dsc2, dsh2 = row_input_grad(dp, w_mix, MIX_PARTS, dz2, h1, sc2, "mix_du", after=token)
    token = grads_sent("mix", dh1)
    carry1, gr1, _ = ffn_backward_weights(dh1, saved1, sh1, sc1, g1, w1["in"], w1["out"], _after(ln1_g, token),
                                          "ffn1")
    token = grads_ready("ffn1", dict(ffn1_in=gr1["w_in"], ffn1_out=gr1["w_out"]))
    dx, dsc1, dsh1 = ffn_backward_input(carry1, saved1, sc1, w1["in"], "ffn1", after=token)

    dmod = [dsh1, dsc1, gr1["g"], dsh2, dsc2, dg2, dsh3, dsc3, gr3["g"]]
    small = dict(ln1_g=gr1["ln_g"], ln1_b=gr1["ln_b"], ln2_g=dln2_g, ln2_b=dln2_b, ln3_g=gr3["ln_g"],
                 ln3_b=gr3["ln_b"], b_alpha=dba, gla_norm_g=dgn, w_alpha2=dwa_pad[:rank], rel_bias=d_rel)
    return loss, dx, dmod, small


GROUPS = dict(ffn1=("ffn1_in", "ffn1_out"), mix=("mix_in", "proj_a", "proj_b", "mix_out"),
              ffn2=("ffn2_in", "ffn2_out"))
GATHERS = dict(ffn1_in=("ffn1_in",), ffn1_out=("ffn1_out",), mix=GROUPS["mix"], ffn2=GROUPS["ffn2"])
SMALL_REPLICATED = ("b_ada", "ln1_g", "ln1_b", "ln2_g", "ln2_b", "ln3_g", "ln3_b", "b_alpha", "gla_norm_g")
SMALL_SHARDED = ("rel_bias", "w_alpha2")
WEIGHT_ORDER = ("w_ada", "b_ada", "ffn1_w_in", "ffn1_w_out", "ln1_g", "ln1_b", "w_mix_in", "rel_bias", "w_alpha2",
                "b_alpha", "gla_norm_g", "w_proj_a", "w_proj_b", "w_mix_out", "ln2_g", "ln2_b", "ffn2_w_in",
                "ffn2_w_out", "ln3_g", "ln3_b")
BIG_NAME = dict(ffn1_in="ffn1_w_in", ffn1_out="ffn1_w_out", mix_in="w_mix_in", proj_a="w_proj_a",
                proj_b="w_proj_b", mix_out="w_mix_out", ffn2_in="ffn2_w_in", ffn2_out="ffn2_w_out")


def kernel(x, c, w_ada, b_ada, ffn1_w_in, ffn1_w_out, ln1_g, ln1_b, w_mix_in, rel_bias, w_alpha2, b_alpha, gla_norm_g, w_proj_a, w_proj_b, w_mix_out, ln2_g, ln2_b, ffn2_w_in, ffn2_w_out, ln3_g, ln3_b, loss_target, m_w_ada, m_b_ada, m_ffn1_w_in, m_ffn1_w_out, m_ln1_g, m_ln1_b, m_w_mix_in, m_rel_bias, m_w_alpha2, m_b_alpha, m_gla_norm_g, m_w_proj_a, m_w_proj_b, m_w_mix_out, m_ln2_g, m_ln2_b, m_ffn2_w_in, m_ffn2_w_out, m_ln3_g, m_ln3_b, v_w_ada, v_b_ada, v_ffn1_w_in, v_ffn1_w_out, v_ln1_g, v_ln1_b, v_w_mix_in, v_rel_bias, v_w_alpha2, v_b_alpha, v_gla_norm_g, v_w_proj_a, v_w_proj_b, v_w_mix_out, v_ln2_g, v_ln2_b, v_ffn2_w_in, v_ffn2_w_out, v_ln3_g, v_ln3_b):
    env = dict(locals())
    w = {n: env[n] for n in WEIGHT_ORDER}
    mom = {n: env["m_" + n] for n in WEIGHT_ORDER}
    var = {n: env["v_" + n] for n in WEIGHT_ORDER}
    me = _me()
    dev = _lin(me)
    core = jnp.reshape(me[2], (1,)).astype(jnp.int32)
    chip = jnp.reshape(2 * me[0] + me[1], (1,)).astype(jnp.int32)
    d = x.shape[-1]

    def shard(n):
        s = w[BIG_NAME[n]][0].astype(BF16)
        return s.T if n == "mix_in" else s

    dev_idx = jnp.reshape(dev, (1,)).astype(jnp.int32)
    started = {}

    def start(grp, after):
        shards = [shard(n) for n in GATHERS[grp]]
        lands = [place_own(dev_idx, s, f"place_own_{n}") for n, s in zip(GATHERS[grp], shards)]
        started[grp] = gather_start(shards, lands, after, f"gather_start_{grp}")
        return started[grp][-1]

    small_w = all_gather_small(jnp.concatenate([rel_bias[0], w_alpha2[0]], axis=1), "gather_small_w")
    n_rel_cols = rel_bias.shape[-1]
    rel_full = small_w[:, :, :n_rel_cols].transpose(1, 0, 2).reshape(small_w.shape[1], -1)
    wa2_full = small_w[:, :, n_rel_cols:].transpose(1, 0, 2).reshape(small_w.shape[1], -1)

    first, *rest = list(GATHERS)
    order = start(first, small_w[0, :1, :1])
    order, w, mom, var = lax.optimization_barrier((order, w, mom, var))

    ada_cols = w_ada.shape[-1]
    bias = bias_table(rel_full)
    c_all = all_gather_small(c, "gather_c", after=(order, bias))[:, 0, :]
    b_cols = lax.dynamic_slice_in_dim(b_ada, dev * ada_cols, ada_cols, axis=1)
    mod_cols = adaln_cols(c_all, w_ada[0], b_cols, "adaln_cols")
    mod_all = all_gather_small(mod_cols, "gather_mod")
    mod = lax.dynamic_index_in_dim(mod_all, dev, axis=1, keepdims=False).reshape(N_MOD, d)

    order = mod_all[0, :1, :1]
    for grp in rest:
        order = start(grp, order)
    mod = _after(mod, order)

    forwards = {}

    def weights_early(grp, after):
        _, zones = gather_wait(started[grp], after, f"gather_wait_{grp}")
        forwards[grp] = forward_start(zones, f"forward_start_{grp}")
        return forwards[grp][-1]

    def weights_of(grp, after):
        if grp in forwards:
            arrays = forward_wait(forwards[grp], after, f"forward_wait_{grp}")
        else:
            _, zones = gather_wait(started[grp], after, f"gather_wait_{grp}")
            arrays = gather_forward(zones, f"gather_forward_{grp}")
        full = dict(zip(GATHERS[grp], arrays))
        if grp == "mix":
            return dict(in_t=full["mix_in"], proj_a=full["proj_a"], proj_b=full["proj_b"],
                        out=full["mix_out"].reshape(-1, d))
        return {"in" if n.endswith("_in") else "out": v if n.endswith("_in") else v.reshape(-1, d)
                for n, v in full.items()}

    pairs, exchanges = {}, {}
    last = list(GROUPS)[0]

    def to_chips(grp, grads, got):
        sums = [pair_add(core, grads[n], g, f"grad_pair_add_{n}") for n, g in zip(GROUPS[grp], got)]
        exchanges[grp] = exchange_start(sums, chip_routes, f"grad_chip_start_{grp}")
        return exchanges[grp][-1]

    def grads_ready(grp, grads):
        if grp == last:
            return to_chips(grp, grads, pair_exchange([grads[n] for n in GROUPS[grp]], f"grad_pair_exchange_{grp}"))
        pairs[grp] = exchange_start([grads[n] for n in GROUPS[grp]], pair_routes, f"grad_pair_start_{grp}")
        return pairs[grp][-1]

    def grads_sent(grp, after):
        if grp == last:
            return None
        sent, got = exchange_wait(pairs[grp], pair_routes, after, f"grad_pair_wait_{grp}")
        return to_chips(grp, dict(zip(GROUPS[grp], sent)), got)

    lns = [ln1_g, ln1_b, ln2_g, ln2_b, ln3_g, ln3_b]
    loss, dx, dmod, small = local_step(x[0], loss_target[0], mod, weights_of, grads_ready, grads_sent, rel_full,
                                       wa2_full, b_alpha, gla_norm_g, lns, bias=bias, weights_early=weights_early)

    out = {}

    def finish(grp, after):
        sums, recv = exchange_wait(exchanges[grp], chip_routes, after, f"grad_chip_wait_{grp}")
        for n, hsum, r in zip(GROUPS[grp], sums, recv):
            full = BIG_NAME[n]
            if n == "mix_in":
                g = owned_sum(chip, hsum, r, f"owned_sum_{n}").T
                res_n = adamw_sum(g[None], w[full][0], mom[full][0], var[full][0], f"adamw_{n}")
            else:
                res_n = adamw_owned(chip, hsum, r, w[full][0], mom[full][0], var[full][0], f"adamw_{n}")
            out[full] = [o[None] for o in res_n]
            after = res_n[0]
        return after

    order = dx
    for grp in reversed(list(GROUPS)[1:]):
        order = finish(grp, order)

    pieces = (list(dmod) + [small[n].reshape(1, -1) for n in SMALL_REPLICATED[1:] + SMALL_SHARDED] + [loss])
    parts = all_gather_rows(pieces, order, "gather_small_grads")
    loss = jnp.sum(parts[:, 0, parts.shape[2] - loss.shape[1]])
    n_mod = N_MOD * d
    dmod_all = parts[:, 0, :n_mod]
    g_w_ada = adaln_wgrad(c_all, lax.dynamic_slice_in_dim(dmod_all, dev * ada_cols, ada_cols, axis=1), "adaln_wgrad")
    out["w_ada"] = [o[None] for o in adamw_sum(g_w_ada[None], w_ada[0], m_w_ada[0], v_w_ada[0], "adamw_w_ada")]

    sources, where, off = [parts], [], 0
    for n in SMALL_REPLICATED:
        where.append((0, off))
        off += w[n].size
    for n in SMALL_SHARDED:
        rows, cols_local = w[n].shape[1], w[n].shape[2]
        full_part = parts[:, 0, off:off + rows * cols_local * N_DEV].reshape(N_DEV, rows, cols_local * N_DEV)
        mine = lax.dynamic_slice_in_dim(full_part, dev * cols_local, cols_local, axis=2)
        where.append((len(sources), 0))
        sources.append(mine.reshape(N_DEV, 1, rows * cols_local))
        off += rows * cols_local * N_DEV
    names = SMALL_REPLICATED + SMALL_SHARDED
    res = adamw_rows(sources, where, *[[src[n].reshape(1, -1) for n in names] for src in (w, mom, var)],
                     "adamw_small")
    for n, res_n in zip(names, res):
        out[n] = [r.reshape(w[n].shape) for r in res_n]

    finish(last, res[0][0])

    flat = [loss, dx[None]]
    for k in range(4):
        flat += [out[n][k] for n in WEIGHT_ORDER]
    return tuple(flat)
```

```python
import functools

import numpy as np
import jax
import jax.numpy as jnp
from jax import lax
from jax.experimental import pallas as pl
from jax.experimental.pallas import tpu as pltpu

F32 = jnp.float32
BF16 = jnp.bfloat16
MESH = pl.DeviceIdType.MESH
N_DEV = 8
N_CHIP = 4

CHUNK = 64
A_PAST_CHUNKS = 8
REL_CLIP = 256
GATE_TAU = 16.0
N_MOD = 9
DEPTH = 1
ALPHA = (2.0 * DEPTH) ** 0.25
LN_EPS = 1e-5
RMS_EPS = 1e-6
ADAM_LR = 0.001
ADAM_B1 = 0.9
ADAM_B2 = 0.999
ADAM_EPS = 1e-08
ADAM_WD = 0.01
ADAM_STEP = 10

LANE = 128
VMEM_LIMIT = 56 * 2 ** 20
QB = 4 * CHUNK
KW = 3 * QB
GB = 8 * CHUNK
NEG = -1e30
HI = lax.Precision.HIGHEST

ANY = pl.BlockSpec(memory_space=pl.ANY)
VMEM_SPEC = pl.BlockSpec(memory_space=pltpu.VMEM)


def _params(*sem):
    return pltpu.CompilerParams(dimension_semantics=sem, vmem_limit_bytes=VMEM_LIMIT)


def _sds(shape, dtype):
    return jax.ShapeDtypeStruct(shape, dtype)


def _dot(a, b):
    return jnp.dot(a, b, preferred_element_type=F32)


def _dot_nt(a, b):
    return lax.dot_general(a, b, (((1,), (1,)), ((), ())), preferred_element_type=F32)


def _dot_tn(a, b):
    return lax.dot_general(a, b, (((0,), (0,)), ((), ())), preferred_element_type=F32)


def _sigmoid(x):
    return 0.5 * jnp.tanh(0.5 * x) + 0.5


def _colsum(x):
    return jnp.sum(x, axis=0, keepdims=True)


def _row_tile(rows, cap, mult):
    for t in range(min(rows, cap), 0, -1):
        if rows % t == 0 and t % mult == 0:
            return t
    return rows


def _me():
    return lax.axis_index("x"), lax.axis_index("y"), lax.axis_index("c")


def _flip(me, k):
    return tuple((1 - p) if (k >> s) & 1 else p for p, s in zip(me, (2, 1, 0)))


def _lin(p):
    return 4 * p[0] + 2 * p[1] + p[2]


def _gather_direct(x_ref, out_ref, send_sems, recv_sems, local_sem):
    me = _me()
    mine = pltpu.make_async_copy(x_ref, out_ref.at[_lin(me)], local_sem)
    mine.start()
    sends = []
    for k in range(1, N_DEV):
        cp = pltpu.make_async_remote_copy(
            src_ref=x_ref, dst_ref=out_ref.at[_lin(me)], send_sem=send_sems.at[k - 1],
            recv_sem=recv_sems.at[k - 1], device_id=_flip(me, k), device_id_type=MESH)
        cp.start()
        sends.append(cp)
    for k in range(1, N_DEV):
        peer = _flip(me, k)
        pltpu.make_async_remote_copy(
            src_ref=x_ref, dst_ref=out_ref.at[_lin(peer)], send_sem=send_sems.at[k - 1],
            recv_sem=recv_sems.at[k - 1], device_id=peer, device_id_type=MESH).wait_recv()
    for cp in sends:
        cp.wait_send()
    mine.wait()


GATHER_SEMS = [pltpu.SemaphoreType.DMA((N_DEV - 1,)), pltpu.SemaphoreType.DMA((N_DEV - 1,)), pltpu.SemaphoreType.DMA]


def all_gather_small(x, name, after=()):
    r, n = x.shape

    def body(x_ref, *refs):
        _gather_direct(x_ref, *refs[len(after):])

    return pl.pallas_call(
        body, name=name, out_shape=_sds((N_DEV, r, n), x.dtype),
        in_specs=[VMEM_SPEC] + [ANY] * len(after), out_specs=VMEM_SPEC, scratch_shapes=GATHER_SEMS,
    )(x, *after)


def all_gather_rows(pieces, after, name):
    sizes = [x.shape[1] for x in pieces]
    total = sum(sizes)
    assert all(n % LANE == 0 for n in sizes)

    def body(*refs):
        ins = refs[:len(pieces)]
        out_ref, row, send_sems, recv_sems, local_sem = refs[len(pieces) + 1:]
        off = 0
        for x_ref, n in zip(ins, sizes):
            row[:, off:off + n] = x_ref[...]
            off += n
        _gather_direct(row, out_ref, send_sems, recv_sems, local_sem)

    return pl.pallas_call(
        body, name=name, out_shape=_sds((N_DEV, 1, total), F32),
        in_specs=[VMEM_SPEC] * len(pieces) + [ANY], out_specs=VMEM_SPEC,
        scratch_shapes=[pltpu.VMEM((1, total), F32)] + GATHER_SEMS,
    )(*pieces, after)


HBM_SPEC = pl.BlockSpec(memory_space=pltpu.HBM)
SEM_SPEC = pl.BlockSpec(memory_space=pltpu.SEMAPHORE)
EFFECT = pltpu.SideEffectType.DATAFLOW_SIDE_EFFECTING
FIRST = N_CHIP


def _hbm(v):
    return pltpu.with_memory_space_constraint(v, pltpu.HBM)


def _other_chips(mx, my):
    return [(1 - mx, my), (mx, 1 - my), (1 - mx, 1 - my)]


def place_own(dev, shard, name):
    rows, cols = shard.shape
    tr, tc = _tile2(rows, cols, 16)

    def body(dev_ref, s_ref, land_ref, o_ref):
        o_ref[...] = s_ref[...]

    land = lax.empty((N_DEV, rows, cols), shard.dtype)
    return pl.pallas_call(
        body, name=name, out_shape=_sds(land.shape, land.dtype),
        grid_spec=pltpu.PrefetchScalarGridSpec(
            num_scalar_prefetch=1, grid=(rows // tr, cols // tc),
            in_specs=[pl.BlockSpec((tr, tc), lambda i, j, d: (i, j)), ANY],
            out_specs=pl.BlockSpec((None, tr, tc), lambda i, j, d: (d[0], i, j))),
        input_output_aliases={2: 0}, compiler_params=_params("parallel", "parallel"),
    )(dev, shard, land)


def gather_start(shards, lands, after, name):
    n = len(shards)

    def body(*refs):
        ins, zones = refs[:n], refs[n:2 * n]
        send_sems, recv_sems = refs[2 * n + 1], refs[2 * n + 2]
        token = refs[-1]
        me = _me()
        mx, my, mc = me
        for a in range(n):
            dst = zones[a].at[_lin(me)]
            targets = [(mx, my, 1 - mc)] + [(*chip, mc) for chip in _other_chips(mx, my)]
            for k, to in enumerate(targets):
                pltpu.make_async_remote_copy(
                    src_ref=ins[a], dst_ref=dst, send_sem=send_sems.at[a * FIRST + k],
                    recv_sem=recv_sems.at[a * FIRST + k], device_id=to, device_id_type=MESH).start()
        token[...] = jnp.zeros_like(token)

    sems = pltpu.SemaphoreType.DMA((n * FIRST,))
    out = pl.pallas_call(
        body, name=name,
        out_shape=(sems, sems, *[pltpu.HBM(s.shape, s.dtype) for s in shards],
                   *[pltpu.HBM(z.shape, z.dtype) for z in lands], _sds((8, LANE), F32)),
        in_specs=[HBM_SPEC] * (2 * n) + [ANY],
        out_specs=(SEM_SPEC, SEM_SPEC, *[HBM_SPEC] * (2 * n), VMEM_SPEC),
        input_output_aliases={a: 2 + a for a in range(2 * n)},
        compiler_params=pltpu.CompilerParams(has_side_effects=EFFECT),
    )(*[_hbm(s) for s in shards], *[_hbm(z) for z in lands], after)
    return out[0], out[1], out[2:2 + n], out[2 + n:2 + 2 * n], out[-1]


def gather_wait(started, after, name):
    send_sems, recv_sems, shards, lands, _ = started
    n = len(shards)

    def body(*refs):
        ins, zones = refs[:n], refs[n:2 * n]
        send_ref, recv_ref = refs[2 * n], refs[2 * n + 1]
        mx, my, mc = _me()
        for a in range(n):
            for k in range(FIRST):
                cp = pltpu.make_async_remote_copy(
                    src_ref=ins[a], dst_ref=zones[a].at[0], send_sem=send_ref.at[a * FIRST + k],
                    recv_sem=recv_ref.at[a * FIRST + k], device_id=(mx, my, 1 - mc), device_id_type=MESH)
                cp.wait_send()
                cp.wait_recv()

    out = pl.pallas_call(
        body, name=name,
        out_shape=(*[pltpu.HBM(s.shape, s.dtype) for s in shards], *[pltpu.HBM(z.shape, z.dtype) for z in lands]),
        in_specs=[HBM_SPEC] * (2 * n) + [SEM_SPEC, SEM_SPEC, ANY], out_specs=tuple([HBM_SPEC] * (2 * n)),
        input_output_aliases={a: a for a in range(2 * n)},
        compiler_params=pltpu.CompilerParams(has_side_effects=EFFECT),
    )(*shards, *lands, send_sems, recv_sems, after)
    return out[:n], out[n:]


def gather_forward(lands, name):
    n = len(lands)
    rel = N_CHIP - 1

    def body(*refs):
        zones, outs = refs[:n], refs[n:2 * n]
        send_sems, recv_sems = refs[2 * n:]
        mx, my, mc = _me()
        chips = _other_chips(mx, my)

        def copy(a, j, core):
            blk = _lin((*chips[j], core))
            return pltpu.make_async_remote_copy(
                src_ref=zones[a].at[blk], dst_ref=outs[a].at[blk], send_sem=send_sems.at[a * rel + j],
                recv_sem=recv_sems.at[a * rel + j], device_id=(mx, my, 1 - mc), device_id_type=MESH)

        sends = [copy(a, j, mc) for a in range(n) for j in range(rel)]
        for cp in sends:
            cp.start()
        for a in range(n):
            for j in range(rel):
                copy(a, j, 1 - mc).wait_recv()
        for cp in sends:
            cp.wait_send()

    return pl.pallas_call(
        body, name=name, out_shape=[_sds(z.shape, z.dtype) for z in lands],
        in_specs=[ANY] * n, out_specs=[ANY] * n, input_output_aliases={a: a for a in range(n)},
        scratch_shapes=[pltpu.SemaphoreType.DMA((n * rel,)), pltpu.SemaphoreType.DMA((n * rel,))],
    )(*lands)


def forward_start(lands, name):
    n = len(lands)
    rel = N_CHIP - 1

    def body(*refs):
        zones = refs[:n]
        send_sems, recv_sems = refs[n], refs[n + 1]
        token = refs[-1]
        mx, my, mc = _me()
        for a in range(n):
            for j, chip in enumerate(_other_chips(mx, my)):
                blk = zones[a].at[_lin((*chip, mc))]
                pltpu.make_async_remote_copy(
                    src_ref=blk, dst_ref=blk, send_sem=send_sems.at[a * rel + j], recv_sem=recv_sems.at[a * rel + j],
                    device_id=(mx, my, 1 - mc), device_id_type=MESH).start()
        token[...] = jnp.zeros_like(token)

    sems = pltpu.SemaphoreType.DMA((n * rel,))
    out = pl.pallas_call(
        body, name=name,
        out_shape=(sems, sems, *[pltpu.HBM(z.shape, z.dtype) for z in lands], _sds((8, LANE), F32)),
        in_specs=[HBM_SPEC] * n, out_specs=(SEM_SPEC, SEM_SPEC, *[HBM_SPEC] * n, VMEM_SPEC),
        input_output_aliases={a: 2 + a for a in range(n)},
        compiler_params=pltpu.CompilerParams(has_side_effects=EFFECT),
    )(*[_hbm(z) for z in lands])
    return out[0], out[1], out[2:2 + n], out[-1]


def forward_wait(started, after, name):
    send_sems, recv_sems, lands, _ = started
    n = len(lands)
    rel = N_CHIP - 1

    def body(*refs):
        zones = refs[:n]
        send_ref, recv_ref = refs[n], refs[n + 1]
        mx, my, mc = _me()
        for a in range(n):
            for j, chip in enumerate(_other_chips(mx, my)):
                cp = pltpu.make_async_remote_copy(
                    src_ref=zones[a].at[_lin((*chip, mc))], dst_ref=zones[a].at[_lin((*chip, 1 - mc))],
                    send_sem=send_ref.at[a * rel + j], recv_sem=recv_ref.at[a * rel + j],
                    device_id=(mx, my, 1 - mc), device_id_type=MESH)
                cp.wait_send()
                cp.wait_recv()

    out = pl.pallas_call(
        body, name=name, out_shape=tuple(pltpu.HBM(z.shape, z.dtype) for z in lands),
        in_specs=[HBM_SPEC] * n + [SEM_SPEC, SEM_SPEC, ANY], out_specs=tuple([HBM_SPEC] * n),
        input_output_aliases={a: a for a in range(n)},
        compiler_params=pltpu.CompilerParams(has_side_effects=EFFECT),
    )(*lands, send_sems, recv_sems, after)
    return list(out)


def pair_exchange(gs, name):
    n = len(gs)

    def body(*refs):
        ins, outs = refs[:n], refs[n:2 * n]
        send_sems, recv_sems = refs[2 * n:]
        mx, my, mc = _me()
        cps = []
        for a in range(n):
            for q in range(N_CHIP):
                cp = pltpu.make_async_remote_copy(
                    src_ref=ins[a].at[2 * q + (1 - mc)], dst_ref=outs[a].at[q],
                    send_sem=send_sems.at[a * N_CHIP + q], recv_sem=recv_sems.at[a * N_CHIP + q],
                    device_id=(mx, my, 1 - mc), device_id_type=MESH)
                cp.start()
                cps.append(cp)
        for cp in cps:
            cp.wait()

    return pl.pallas_call(
        body, name=name, out_shape=[_sds((N_CHIP,) + g.shape[1:], g.dtype) for g in gs],
        in_specs=[ANY] * n, out_specs=[ANY] * n,
        scratch_shapes=[pltpu.SemaphoreType.DMA((n * N_CHIP,)), pltpu.SemaphoreType.DMA((n * N_CHIP,))],
    )(*gs)


def _tile2(rows, cols, row_mult):
    tr = _row_tile(rows, 512, row_mult)
    if tr < rows or rows * cols <= 2 ** 20:
        return tr, cols
    return rows, _row_tile(cols, 512, LANE)


def pair_add(core, g, got, name):
    _, rows, cols = g.shape
    tr, tc = _tile2(rows, cols, 16)

    def body(core_ref, g_ref, got_ref, h_ref):
        h_ref[...] = (g_ref[...].astype(F32) + got_ref[...].astype(F32)).astype(h_ref.dtype)

    blk = pl.BlockSpec((None, tr, tc), lambda q, i, j, c: (q, i, j))
    return pl.pallas_call(
        body, name=name, out_shape=_sds((N_CHIP, rows, cols), g.dtype),
        grid_spec=pltpu.PrefetchScalarGridSpec(
            num_scalar_prefetch=1, grid=(N_CHIP, rows // tr, cols // tc),
            in_specs=[pl.BlockSpec((None, tr, tc), lambda q, i, j, c: (2 * q + c[0], i, j)), blk],
            out_specs=blk),
        compiler_params=_params("parallel", "parallel", "parallel"),
    )(core, g, got)


def chip_routes(mx, my, mc):
    return [(2 * px + py, k, (px, py, mc)) for k, (px, py) in enumerate(_other_chips(mx, my))]


def pair_routes(mx, my, mc):
    return [(2 * q + (1 - mc), q, (mx, my, 1 - mc)) for q in range(N_CHIP)]


def exchange_start(hs, routes, name):
    n = len(hs)
    rel = len(routes(0, 0, 0))
    lands = [lax.empty((rel,) + h.shape[1:], h.dtype) for h in hs]

    def body(*refs):
        ins, zones = refs[:n], refs[n:2 * n]
        send_sems, recv_sems = refs[2 * n], refs[2 * n + 1]
        token = refs[-1]
        for a in range(n):
            for k, (src, slot, to) in enumerate(routes(*_me())):
                pltpu.make_async_remote_copy(
                    src_ref=ins[a].at[src], dst_ref=zones[a].at[slot], send_sem=send_sems.at[a * rel + k],
                    recv_sem=recv_sems.at[a * rel + k], device_id=to, device_id_type=MESH).start()
        token[...] = jnp.zeros_like(token)

    sems = pltpu.SemaphoreType.DMA((n * rel,))
    out = pl.pallas_call(
        body, name=name,
        out_shape=(sems, sems, *[pltpu.HBM(h.shape, h.dtype) for h in hs],
                   *[pltpu.HBM(z.shape, z.dtype) for z in lands], _sds((8, LANE), F32)),
        in_specs=[HBM_SPEC] * (2 * n), out_specs=(SEM_SPEC, SEM_SPEC, *[HBM_SPEC] * (2 * n), VMEM_SPEC),
        input_output_aliases={a: 2 + a for a in range(2 * n)},
        compiler_params=pltpu.CompilerParams(has_side_effects=EFFECT),
    )(*[_hbm(h) for h in hs], *[_hbm(z) for z in lands])
    return out[0], out[1], out[2:2 + n], out[2 + n:2 + 2 * n], out[-1]


def exchange_wait(started, routes, after, name):
    send_sems, recv_sems, hs, lands, _ = started
    n = len(hs)
    rel = len(routes(0, 0, 0))

    def body(*refs):
        ins, zones = refs[:n], refs[n:2 * n]
        send_ref, recv_ref = refs[2 * n], refs[2 * n + 1]
        for a in range(n):
            for k, (src, slot, to) in enumerate(routes(*_me())):
                cp = pltpu.make_async_remote_copy(
                    src_ref=ins[a].at[src], dst_ref=zones[a].at[slot], send_sem=send_ref.at[a * rel + k],
                    recv_sem=recv_ref.at[a * rel + k], device_id=to, device_id_type=MESH)
                cp.wait_send()
                cp.wait_recv()

    out = pl.pallas_call(
        body, name=name,
        out_shape=(*[pltpu.HBM(h.shape, h.dtype) for h in hs], *[pltpu.HBM(z.shape, z.dtype) for z in lands]),
        in_specs=[HBM_SPEC] * (2 * n) + [SEM_SPEC, SEM_SPEC, ANY], out_specs=tuple([HBM_SPEC] * (2 * n)),
        input_output_aliases={a: a for a in range(2 * n)},
        compiler_params=pltpu.CompilerParams(has_side_effects=EFFECT),
    )(*hs, *lands, send_sems, recv_sems, after)
    return out[:n], out[n:]


def _adam(w, g, m, v):
    m = ADAM_B1 * m + (1.0 - ADAM_B1) * g
    v = ADAM_B2 * v + (1.0 - ADAM_B2) * (g * g)
    m_hat = m / (1.0 - ADAM_B1 ** ADAM_STEP)
    v_hat = v / (1.0 - ADAM_B2 ** ADAM_STEP)
    delta = -ADAM_LR * (m_hat / (jnp.sqrt(v_hat) + ADAM_EPS) + ADAM_WD * w)
    return delta, m, v


def adamw_owned(chip, h, got, w, m, v, name):
    rows, cols = w.shape
    tr = _row_tile(rows, 256, 16)

    def body(chip_ref, h_ref, got_ref, w_ref, m_ref, v_ref, g_out, d_out, m_out, v_out):
        g = h_ref[...].astype(F32)
        for k in range(N_CHIP - 1):
            g = g + got_ref[k].astype(F32)
        d, mn, vn = _adam(w_ref[...], g, m_ref[...], v_ref[...])
        g_out[...] = g
        d_out[...] = d
        m_out[...] = mn
        v_out[...] = vn

    blk = pl.BlockSpec((tr, cols), lambda i, c: (i, 0))
    return pl.pallas_call(
        body, name=name, out_shape=[_sds((rows, cols), F32)] * 4,
        grid_spec=pltpu.PrefetchScalarGridSpec(
            num_scalar_prefetch=1, grid=(rows // tr,),
            in_specs=[pl.BlockSpec((None, tr, cols), lambda i, c: (c[0], i, 0)),
                      pl.BlockSpec((N_CHIP - 1, tr, cols), lambda i, c: (0, i, 0)), blk, blk, blk],
            out_specs=[blk] * 4),
        compiler_params=_params("parallel"),
    )(chip, h, got, w, m, v)


def owned_sum(chip, h, got, name):
    _, rows, cols = h.shape
    tr, tc = _tile2(rows, cols, 16)

    def body(chip_ref, h_ref, got_ref, g_out):
        g = h_ref[...].astype(F32)
        for k in range(N_CHIP - 1):
            g = g + got_ref[k].astype(F32)
        g_out[...] = g

    return pl.pallas_call(
        body, name=name, out_shape=_sds((rows, cols), F32),
        grid_spec=pltpu.PrefetchScalarGridSpec(
            num_scalar_prefetch=1, grid=(rows // tr, cols // tc),
            in_specs=[pl.BlockSpec((None, tr, tc), lambda i, j, c: (c[0], i, j)),
                      pl.BlockSpec((N_CHIP - 1, tr, tc), lambda i, j, c: (0, i, j))],
            out_specs=pl.BlockSpec((tr, tc), lambda i, j, c: (i, j))),
        compiler_params=_params("parallel", "parallel"),
    )(chip, h, got)


def adamw_rows(sources, where, ws, ms, vs, name):
    n_src, n_par = len(sources), len(ws)

    def body(*refs):
        srcs = refs[:n_src]
        w_refs, m_refs, v_refs = (refs[n_src + j * n_par:n_src + (j + 1) * n_par] for j in range(3))
        outs = refs[n_src + 3 * n_par:]
        for k in range(n_par):
            src, off = srcs[where[k][0]], where[k][1]
            n = w_refs[k].shape[1]
            g = src[0, :, off:off + n]
            for dev in range(1, N_DEV):
                g = g + src[dev, :, off:off + n]
            d, mn, vn = _adam(w_refs[k][...], g, m_refs[k][...], v_refs[k][...])
            for o_ref, val in zip(outs[4 * k:4 * k + 4], (g, d, mn, vn)):
                o_ref[...] = val

    flat = pl.pallas_call(
        body, name=name, out_shape=[_sds(x.shape, F32) for x in ws for _ in range(4)],
        in_specs=[VMEM_SPEC] * (n_src + 3 * n_par), out_specs=[VMEM_SPEC] * (4 * n_par),
        compiler_params=pltpu.CompilerParams(vmem_limit_bytes=VMEM_LIMIT),
    )(*sources, *ws, *ms, *vs)
    return [flat[4 * k:4 * k + 4] for k in range(n_par)]


def adamw_sum(parts, w, m, v, name):
    n_parts, rows, cols = parts.shape
    tr = _row_tile(rows, 256, 8)

    def body(p_ref, w_ref, m_ref, v_ref, g_out, d_out, m_out, v_out):
        g = p_ref[0]
        for k in range(1, n_parts):
            g = g + p_ref[k]
        d, mn, vn = _adam(w_ref[...], g, m_ref[...], v_ref[...])
        g_out[...] = g
        d_out[...] = d
        m_out[...] = mn
        v_out[...] = vn

    blk = pl.BlockSpec((tr, cols), lambda i: (i, 0))
    return pl.pallas_call(
        body, name=name, out_shape=[_sds((rows, cols), F32)] * 4, grid=(rows // tr,),
        in_specs=[pl.BlockSpec((n_parts, tr, cols), lambda i: (0, i, 0)), blk, blk, blk],
        out_specs=[blk] * 4, compiler_params=_params("parallel"),
    )(parts, w, m, v)


def adaln_cols(c_all, w, b, name):
    d, n = w.shape
    tn = _row_tile(n, 768, LANE)

    def body(c_ref, w_ref, b_ref, o_ref):
        c = c_ref[...]
        o_ref[...] = jnp.dot(c * _sigmoid(c), w_ref[...], preferred_element_type=F32, precision=HI) + b_ref[...]

    return pl.pallas_call(
        body, name=name, out_shape=_sds((N_DEV, n), F32), grid=(n // tn,),
        in_specs=[pl.BlockSpec((N_DEV, d), lambda j: (0, 0)), pl.BlockSpec((d, tn), lambda j: (0, j)),
                  pl.BlockSpec((1, tn), lambda j: (0, j))],
        out_specs=pl.BlockSpec((N_DEV, tn), lambda j: (0, j)), compiler_params=_params("parallel"),
    )(c_all, w, b)


def adaln_wgrad(c_all, dmod_cols, name):
    d = c_all.shape[1]
    n = dmod_cols.shape[1]
    tn = _row_tile(n, 768, LANE)

    def body(c_ref, g_ref, o_ref):
        c = c_ref[...]
        o_ref[...] = lax.dot_general(c * _sigmoid(c), g_ref[...], (((0,), (0,)), ((), ())),
                                     preferred_element_type=F32, precision=HI)

    return pl.pallas_call(
        body, name=name, out_shape=_sds((d, n), F32), grid=(n // tn,),
        in_specs=[pl.BlockSpec((N_DEV, d), lambda j: (0, 0)), pl.BlockSpec((N_DEV, tn), lambda j: (0, j))],
        out_specs=pl.BlockSpec((d, tn), lambda j: (0, j)), compiler_params=_params("parallel"),
    )(c_all, dmod_cols)


def _modulate(h, sh, sc):
    return (h * (1.0 + sc) + sh).astype(BF16)


def ffn_in(h, sh, sc, w3, name):
    t, d = h.shape
    nb, _, bw = w3.shape
    half = nb // 2
    tm = _row_tile(t, 512, 16)

    def body(h_ref, sh_ref, sc_ref, wa_ref, wb_ref, a_ref, b_ref, s_ref):
        u = _modulate(h_ref[...], sh_ref[...], sc_ref[...])
        a = _dot(u, wa_ref[...])
        b = _dot(u, wb_ref[...])
        a_ref[...] = a.astype(BF16)
        b_ref[...] = b.astype(BF16)
        s_ref[...] = (a * _sigmoid(a) * b).astype(BF16)

    vec = pl.BlockSpec((1, d), lambda j, i: (0, 0))
    out = pl.BlockSpec((tm, bw), lambda j, i: (i, j))
    return pl.pallas_call(
        body, name=name, out_shape=[_sds((t, half * bw), BF16)] * 3, grid=(half, t // tm),
        in_specs=[pl.BlockSpec((tm, d), lambda j, i: (i, 0)), vec, vec,
                  pl.BlockSpec((None, d, bw), lambda j, i: (j, 0, 0), pipeline_mode=pl.Buffered(1)),
                  pl.BlockSpec((None, d, bw), lambda j, i: (j + half, 0, 0), pipeline_mode=pl.Buffered(1))],
        out_specs=[out] * 3, compiler_params=_params("parallel", "parallel"),
    )(h, sh, sc, w3, w3)


def mod_matmul(h, sh, sc, wt, bw, name):
    t, d = h.shape
    n = wt.shape[0]
    tm = _row_tile(t, 512, 16)

    def body(h_ref, sh_ref, sc_ref, w_ref, o_ref):
        o_ref[...] = _dot_nt(_modulate(h_ref[...], sh_ref[...], sc_ref[...]), w_ref[...]).astype(BF16)

    vec = pl.BlockSpec((1, d), lambda j, i: (0, 0))
    return pl.pallas_call(
        body, name=name, out_shape=_sds((t, n), BF16), grid=(n // bw, t // tm),
        in_specs=[pl.BlockSpec((tm, d), lambda j, i: (i, 0)), vec, vec, pl.BlockSpec((bw, d), lambda j, i: (j, 0))],
        out_specs=pl.BlockSpec((tm, bw), lambda j, i: (i, j)), compiler_params=_params("parallel", "parallel"),
    )(h, sh, sc, wt)


def out_ln(s, w, hin, gmod, ln_g, ln_b, coef, name):
    t, kdim = s.shape
    d = w.shape[1]
    tm = _row_tile(t, 256, 16)

    def body(s_ref, w_ref, hin_ref, gm_ref, g_ref, b_ref, f_ref, z_ref, h_ref):
        f = _dot(s_ref[...], w_ref[...])
        z = ALPHA * hin_ref[...] + (coef * gm_ref[...]) * f
        mu = jnp.mean(z, axis=-1, keepdims=True)
        zc = z - mu
        var = jnp.mean(zc * zc, axis=-1, keepdims=True)
        f_ref[...] = f.astype(BF16)
        z_ref[...] = z
        h_ref[...] = zc * lax.rsqrt(var + LN_EPS) * g_ref[...] + b_ref[...]

    vec = pl.BlockSpec((1, d), lambda i: (0, 0))
    row = pl.BlockSpec((tm, d), lambda i: (i, 0))
    return pl.pallas_call(
        body, name=name, out_shape=[_sds((t, d), BF16), _sds((t, d), F32), _sds((t, d), F32)],
        grid=(t // tm,),
        in_specs=[pl.BlockSpec((tm, kdim), lambda i: (i, 0)),
                  pl.BlockSpec((kdim, d), lambda i: (0, 0), pipeline_mode=pl.Buffered(1)), row, vec, vec, vec],
        out_specs=[row, row, row], compiler_params=_params("parallel"),
    )(s, w, hin, gmod, ln_g, ln_b)


def ln_bwd(dh, z, f, ln_g, gmod, coef, name, target=None):
    t, d = z.shape
    tm = _row_tile(t, 256, 16)
    head = target is not None

    def body(*refs):
        if head:
            dh_ref, tg_ref, z_ref, f_ref, g_ref, gm_ref, dz_ref, df_ref, dg_ref, db_ref, dgm_ref, loss_ref = refs
        else:
            dh_ref, z_ref, f_ref, g_ref, gm_ref, dz_ref, df_ref, dg_ref, db_ref, dgm_ref = refs
        i = pl.program_id(0)

        @pl.when(i == 0)
        def _():
            dg_ref[...] = jnp.zeros_like(dg_ref)
            db_ref[...] = jnp.zeros_like(db_ref)
            dgm_ref[...] = jnp.zeros_like(dgm_ref)
            if head:
                loss_ref[...] = jnp.zeros_like(loss_ref)

        dh = dh_ref[...]
        if head:
            err = dh - tg_ref[...]
            loss_ref[...] += 0.5 * jnp.sum(jnp.mean(err * err, axis=-1, keepdims=True))
            dh = err / d
        zv = z_ref[...]
        mu = jnp.mean(zv, axis=-1, keepdims=True)
        zc = zv - mu
        rstd = lax.rsqrt(jnp.mean(zc * zc, axis=-1, keepdims=True) + LN_EPS)
        xhat = zc * rstd
        dxh = dh * g_ref[...]
        dz = rstd * (dxh - jnp.mean(dxh, axis=-1, keepdims=True)
                     - xhat * jnp.mean(dxh * xhat, axis=-1, keepdims=True))
        dz_ref[...] = dz
        df_ref[...] = ((coef * gm_ref[...]) * dz).astype(BF16)
        dg_ref[...] += _colsum(dh * xhat)
        db_ref[...] += _colsum(dh)
        dgm_ref[...] += _colsum(coef * f_ref[...].astype(F32) * dz)

    vec = pl.BlockSpec((1, d), lambda i: (0, 0))
    row = pl.BlockSpec((tm, d), lambda i: (i, 0))
    ins = [dh] + ([target] if head else []) + [z, f, ln_g, gmod]
    in_specs = [row] + ([row] if head else []) + [row, row, vec, vec]
    out_shape = [_sds((t, d), F32), _sds((t, d), BF16)] + [_sds((1, d), F32)] * 3
    out_specs = [row, row, vec, vec, vec]
    if head:
        out_shape.append(_sds((1, LANE), F32))
        out_specs.append(pl.BlockSpec((1, LANE), lambda i: (0, 0)))
    return pl.pallas_call(
        body, name=name, out_shape=out_shape, grid=(t // tm,), in_specs=in_specs, out_specs=out_specs,
        compiler_params=_params("arbitrary"),
    )(*ins)


def ffn_bwd_act(df, w, a, b, name):
    t, d = df.shape
    fdim = w.shape[0]
    bw = fdim // (N_DEV // 2)
    tm = _row_tile(t, 512, 16)

    def body(df_ref, w_ref, a_ref, b_ref, o_ref):
        ds = _dot_nt(df_ref[...], w_ref[...])
        av = a_ref[...].astype(F32)
        sg = _sigmoid(av)
        o_ref[0] = (ds * b_ref[...].astype(F32) * (sg * (1.0 + av * (1.0 - sg)))).astype(BF16)
        o_ref[1] = (ds * (av * sg)).astype(BF16)

    act = pl.BlockSpec((tm, bw), lambda j, i: (i, j))
    return pl.pallas_call(
        body, name=name, out_shape=_sds((2, t, fdim), BF16), grid=(fdim // bw, t // tm),
        in_specs=[pl.BlockSpec((tm, d), lambda j, i: (i, 0)), pl.BlockSpec((bw, d), lambda j, i: (j, 0)), act, act],
        out_specs=pl.BlockSpec((2, tm, bw), lambda j, i: (0, i, j)),
        compiler_params=_params("parallel", "parallel"),
    )(df, w, a, b)


def matmul_tn(name, a, a_block, a_map, b, b_block, b_map, out_shape, o_block, o_map, n_out, mod=None,
              mod_b=False):
    tk = [s for s in a_block if s is not None][0]
    nk = a.shape[-2] // tk
    m, nn = [s for s in o_block if s is not None]

    def body(*refs):
        if mod is None:
            a_ref, b_ref, o_ref, acc = refs
        else:
            a_ref, sh_ref, sc_ref, b_ref, o_ref, acc = refs
        k = pl.program_id(1)

        @pl.when(k == 0)
        def _():
            acc[...] = jnp.zeros_like(acc)

        av, bv = a_ref[...], b_ref[...]
        if mod is not None and mod_b:
            bv = _modulate(bv, sh_ref[...], sc_ref[...])
        elif mod is not None:
            av = _modulate(av, sh_ref[...], sc_ref[...])
        acc[...] += _dot_tn(av, bv)

        @pl.when(k == nk - 1)
        def _():
            o_ref[...] = acc[...].astype(o_ref.dtype)

    ins = [a] + (list(mod) if mod is not None else []) + [b]
    in_specs = [pl.BlockSpec(a_block, a_map)]
    if mod is not None:
        vec = pl.BlockSpec((1, mod[0].shape[1]), lambda n, k: (0, 0))
        in_specs += [vec, vec]
    in_specs.append(pl.BlockSpec(b_block, b_map))
    return pl.pallas_call(
        body, name=name, out_shape=out_shape, grid=(n_out, nk), in_specs=in_specs,
        out_specs=pl.BlockSpec(o_block, o_map), scratch_shapes=[pltpu.VMEM((m, nn), F32)],
        compiler_params=_params("parallel", "arbitrary"),
    )(*ins)


def block_input_grad(dy, w3, name):
    t = dy.shape[0]
    nb, n, bw = w3.shape
    tm = _row_tile(t, 512, 16)

    def body(dy_ref, w_ref, o_ref):
        o_ref[...] = sum(_dot_nt(dy_ref[:, k * bw:(k + 1) * bw], w_ref[k]) for k in range(nb)).astype(BF16)

    return pl.pallas_call(
        body, name=name, out_shape=_sds((t, n), BF16), grid=(t // tm,),
        in_specs=[pl.BlockSpec((tm, nb * bw), lambda i: (i, 0)),
                  pl.BlockSpec((nb, n, bw), lambda i: (0, 0, 0), pipeline_mode=pl.Buffered(1))],
        out_specs=pl.BlockSpec((tm, n), lambda i: (i, 0)), compiler_params=_params("parallel"),
    )(dy, w3)


def row_input_grad(dy, w, parts, dz, hin, sc, name, after=None):
    t, kdim = dy.shape
    n = w.shape[1]
    kp = kdim // parts
    tm = _row_tile(t, 256, 16)
    row = pl.BlockSpec((tm, n), lambda i: (i, 0))
    vec = pl.BlockSpec((1, n), lambda i: (0, 0))

    def call(c, part):
        final = c == parts - 1

        order = [after] if c == 0 and after is not None else []

        def body(dy_ref, w_ref, *refs):
            du = _dot(dy_ref[...], w_ref[...])
            if c > 0:
                du = du + refs[0][...]
            if not final:
                refs[-1][...] = du
                return
            dz_ref, hin_ref, sc_ref, o_ref, dsc_ref, dsh_ref = refs[-6:]

            @pl.when(pl.program_id(0) == 0)
            def _():
                dsc_ref[...] = jnp.zeros_like(dsc_ref)
                dsh_ref[...] = jnp.zeros_like(dsh_ref)

            o_ref[...] = ALPHA * dz_ref[...] + du * (1.0 + sc_ref[...])
            dsc_ref[...] += _colsum(du * hin_ref[...])
            dsh_ref[...] += _colsum(du)

        ins = [dy, w] + ([part] if c > 0 else []) + order + ([dz, hin, sc] if final else [])
        in_specs = [pl.BlockSpec((tm, kp), lambda i: (i, c)),
                    pl.BlockSpec((kp, n), lambda i: (c, 0), pipeline_mode=pl.Buffered(1))]
        in_specs += ([row] if c > 0 else []) + [ANY] * len(order) + ([row, row, vec] if final else [])
        return pl.pallas_call(
            body, name=f"{name}_{c}", grid=(t // tm,), in_specs=in_specs,
            out_shape=[_sds((t, n), F32), _sds((1, n), F32), _sds((1, n), F32)] if final else _sds((t, n), F32),
            out_specs=[row, vec, vec] if final else row,
            compiler_params=_params("arbitrary" if final else "parallel"),
        )(*ins)

    part = None
    for c in range(parts):
        part = call(c, part)
    return part


REL_W = KW + QB


def bias_table(rel_bias):
    nh, n_rel = rel_bias.shape
    lo = KW - QB - REL_CLIP
    hi = KW - lo - n_rel
    assert n_rel == REL_CLIP + CHUNK and lo >= 0 and hi >= 0
    first, last = rel_bias[:, :1], rel_bias[:, -1:]
    row = jnp.concatenate([jnp.broadcast_to(first, (nh, lo)), rel_bias, jnp.broadcast_to(last, (nh, hi)),
                           jnp.broadcast_to(first, (nh, QB))], axis=1)
    table = jnp.tile(row, (1, QB))[:, :QB * (REL_W - 1)].reshape(nh, QB, REL_W - 1)[:, :, :KW]
    q = np.arange(QB)[:, None] // CHUNK
    k = np.arange(KW)[None, :] // CHUNK
    band = (k >= q) & (k <= q + A_PAST_CHUNKS)
    return jnp.where(band[None], table, NEG)


def bias_grad_skew(dbias):
    nh = dbias.shape[0]
    flat = jnp.pad(dbias, ((0, 0), (0, 0), (0, REL_W - 1 - KW))).reshape(nh, QB * (REL_W - 1))
    return jnp.pad(flat, ((0, 0), (0, QB))).reshape(nh, QB, REL_W)


def bias_clip_map(n_rel):
    m = np.arange(REL_W)
    dist = np.where(m < KW, m, m - REL_W) - (KW - QB)
    idx = np.clip(dist, -REL_CLIP, CHUNK - 1) + REL_CLIP
    return (idx[:, None] == np.arange(n_rel)[None, :]).astype(np.float32)


PAIR = 2


def _pair_specs(col, rows_of):
    return [pl.BlockSpec((QB, LANE), functools.partial(lambda r, h, i: (rows_of(r, i), col // LANE + h), r))
            for r in range(3)]


def _earlier(r, i):
    return jnp.maximum(i - 2 + r, 0)


def _head_lanes(hh, dh):
    lane = lax.broadcasted_iota(jnp.int32, (1, LANE), 1)
    return (lane < dh) if hh == 0 else (lane >= dh)


def _only(x, lanes):
    return jnp.where(lanes, x, jnp.zeros_like(x))


def _scores(q, ks, bias, i, scale):
    s = jnp.concatenate([_dot_nt(q, kk) for kk in ks], axis=1) * scale + bias
    col = lax.broadcasted_iota(jnp.int32, s.shape, 1)
    return jnp.where(col >= (2 - i) * QB, s, NEG)


def attn_fwd(p, cols, bias, dh, name):
    t = p.shape[0]
    nh = bias.shape[0]
    scale = dh ** -0.5

    def body(q_ref, k0, k1, k2, v0, v1, v2, b_ref, o_ref, lse_ref):
        i = pl.program_id(1)
        q = q_ref[...]
        outs = []
        for hh in range(PAIR):
            lanes = _head_lanes(hh, dh)
            s = _scores(q, [_only(kk[...], lanes) for kk in (k0, k1, k2)], b_ref[hh], i, scale)
            m = jnp.max(s, axis=-1, keepdims=True)
            e = jnp.exp(s - m)
            l = jnp.sum(e, axis=-1, keepdims=True)
            eb = e.astype(BF16)
            o = sum(_dot(eb[:, r * QB:(r + 1) * QB], vv[...]) for r, vv in enumerate((v0, v1, v2)))
            outs.append(o / l)
            lse_ref[hh] = m + jnp.log(l)
        o_ref[...] = jnp.where(_head_lanes(0, dh), outs[0], outs[1]).astype(BF16)

    st = pl.BlockSpec((PAIR, QB, 1), lambda h, i: (h, i, 0))
    return pl.pallas_call(
        body, name=name, out_shape=[_sds((t, nh * dh), BF16), _sds((nh, t, 1), F32)], grid=(nh // PAIR, t // QB),
        in_specs=[pl.BlockSpec((QB, LANE), lambda h, i: (i, cols["qa"] // LANE + h))]
        + _pair_specs(cols["ka"], _earlier) + _pair_specs(cols["va"], _earlier)
        + [pl.BlockSpec((PAIR, QB, KW), lambda h, i: (h, 0, 0))],
        out_specs=[pl.BlockSpec((QB, LANE), lambda h, i: (i, h)), st],
        compiler_params=_params("parallel", "parallel"),
    )(p, p, p, p, p, p, p, bias)


def attn_bwd(p, cols, bias, lse, dy, dh, name):
    t = p.shape[0]
    nh = bias.shape[0]
    nb = t // QB
    scale = dh ** -0.5

    def body(q_ref, k0, k1, k2, v0, v1, v2, b_ref, lse_ref, dy_ref, dq_ref, dk_ref, dv_ref, db_ref, dk_acc, dv_acc):
        i = pl.program_id(1)

        @pl.when(i == 0)
        def _():
            db_ref[...] = jnp.zeros_like(db_ref)
            dk_acc[...] = jnp.zeros_like(dk_acc)
            dv_acc[...] = jnp.zeros_like(dv_acc)

        q, dyv = q_ref[...], dy_ref[...]
        ks = [k0[...], k1[...], k2[...]]
        dqs, dks, dvs = [], [], []
        for hh in range(PAIR):
            lanes = _head_lanes(hh, dh)
            s = _scores(q, [_only(kk, lanes) for kk in ks], b_ref[hh], i, scale)
            prob = jnp.exp(s - lse_ref[hh])
            dprob = jnp.concatenate([_dot_nt(dyv, _only(vv[...], lanes)) for vv in (v0, v1, v2)], axis=1)
            delta = jnp.sum(prob * dprob, axis=-1, keepdims=True)
            ds = prob * (dprob - delta)
            dsb, pb = ds.astype(BF16), prob.astype(BF16)
            dqs.append(sum(_dot(dsb[:, r * QB:(r + 1) * QB], kk) for r, kk in enumerate(ks)))
            dks.append([_dot_tn(dsb[:, r * QB:(r + 1) * QB], q) for r in range(3)])
            dvs.append([_dot_tn(pb[:, r * QB:(r + 1) * QB], dyv) for r in range(3)])
            db_ref[hh] += ds
        first = _head_lanes(0, dh)
        dq_ref[...] = (jnp.where(first, dqs[0], dqs[1]) * scale).astype(BF16)
        for r in range(3):
            rows = pl.ds(pl.multiple_of(_earlier(r, i) * QB, QB), QB)
            dk_acc[rows, :] += jnp.where(first, dks[0][r], dks[1][r])
            dv_acc[rows, :] += jnp.where(first, dvs[0][r], dvs[1][r])

        @pl.when(i == nb - 1)
        def _():
            dk_ref[...] = (dk_acc[...] * scale).astype(BF16)
            dv_ref[...] = dv_acc[...].astype(BF16)

    st = pl.BlockSpec((PAIR, QB, 1), lambda h, i: (h, i, 0))
    tab = pl.BlockSpec((PAIR, QB, KW), lambda h, i: (h, 0, 0))
    own = pl.BlockSpec((QB, LANE), lambda h, i: (i, h))
    whole = pl.BlockSpec((t, LANE), lambda h, i: (0, h))
    return pl.pallas_call(
        body, name=name,
        out_shape=[_sds((t, nh * dh), BF16)] * 3 + [_sds((nh, QB, KW), F32)],
        grid=(nh // PAIR, nb),
        in_specs=[pl.BlockSpec((QB, LANE), lambda h, i: (i, cols["qa"] // LANE + h))]
        + _pair_specs(cols["ka"], _earlier) + _pair_specs(cols["va"], _earlier) + [tab, st, own],
        out_specs=[own, whole, whole, tab],
        scratch_shapes=[pltpu.VMEM((t, LANE), F32), pltpu.VMEM((t, LANE), F32)],
        compiler_params=_params("parallel", "arbitrary"),
    )(p, p, p, p, p, p, p, bias, lse, dy)


def _prefix_sums(x, strict):
    r = lax.broadcasted_iota(jnp.int32, (CHUNK, CHUNK), 0)
    c = lax.broadcasted_iota(jnp.int32, (CHUNK, CHUNK), 1)
    tri = jnp.where((c < r) if strict else (c <= r), 1.0, 0.0).astype(BF16)
    n = x.shape[1]
    hi = x.astype(BF16)
    rest = x - hi.astype(F32)
    mid = rest.astype(BF16)
    lo = (rest - mid.astype(F32)).astype(BF16)
    out = _dot(tri, jnp.concatenate([hi, mid, lo], axis=1))
    return out[:, :n] + out[:, n:2 * n] + out[:, 2 * n:]


def _gate(lr, wa, ba):
    y = _dot(lr, wa) + ba
    return (jnp.minimum(y, 0.0) - jnp.log(1.0 + jnp.exp(-jnp.abs(y)))) / GATE_TAU, y


def _decays(la):
    cum = _prefix_sums(la, strict=False)
    last = cum[CHUNK - 1:CHUNK, :]
    return jnp.exp(last - cum), jnp.exp(last)


def _gla_specs(cols, hk, hv, order):
    def at(start, width):
        return pl.BlockSpec((GB, width), lambda h, i: (order(i), start // width + h))
    return [at(cols["qb"], hk), at(cols["kb"], hk), at(cols["vb"], hv), at(cols["rb"], hv),
            pl.BlockSpec((GB, LANE), lambda h, i: (order(i), cols["lr"] // LANE))]


def gla_fwd(p, cols, wa, ba, gn, nh, hk, hv, name):
    t = p.shape[0]
    nc = t // CHUNK
    scale = hk ** -0.5
    per = GB // CHUNK

    def body(q_ref, k_ref, v_ref, r_ref, lr_ref, wa_ref, ba_ref, gn_ref, o_ref, y_ref, st_ref, state):
        @pl.when(pl.program_id(1) == 0)
        def _():
            state[...] = jnp.zeros_like(state)

        for c in range(per):
            rows = pl.ds(c * CHUNK, CHUNK)
            la, _ = _gate(lr_ref[rows, :], wa_ref[...], ba_ref[...])
            w, decay = _decays(la)
            kdec = (k_ref[rows, :].astype(F32) * w).astype(BF16)
            st = decay * state[...] + _dot_tn(v_ref[rows, :], kdec)
            state[...] = st
            st_ref[c] = st
            o = _dot_nt(q_ref[rows, :], st.astype(BF16)) * scale
            o_ref[rows, :] = o
            rinv = lax.rsqrt(jnp.mean(o * o, axis=-1, keepdims=True) + RMS_EPS)
            rv = r_ref[rows, :].astype(F32)
            y_ref[rows, :] = (o * rinv * gn_ref[...] * (rv * _sigmoid(rv))).astype(BF16)

    return pl.pallas_call(
        body, name=name,
        out_shape=[_sds((t, nh * hv), F32), _sds((t, nh * hv), BF16), _sds((nh, nc, hv, hk), F32)],
        grid=(nh, t // GB),
        in_specs=_gla_specs(cols, hk, hv, lambda i: i)
        + [pl.BlockSpec((LANE, hk), lambda h, i: (0, h)), pl.BlockSpec((1, hk), lambda h, i: (0, h)),
           pl.BlockSpec((1, hv), lambda h, i: (0, 0))],
        out_specs=[pl.BlockSpec((GB, hv), lambda h, i: (i, h)), pl.BlockSpec((GB, hv), lambda h, i: (i, h)),
                   pl.BlockSpec((None, per, hv, hk), lambda h, i: (h, i, 0, 0))],
        scratch_shapes=[pltpu.VMEM((hv, hk), F32)], compiler_params=_params("parallel", "arbitrary"),
    )(p, p, p, p, p, wa, ba, gn)


def gla_bwd(p, cols, wa, ba, gn, o, states, dy, nh, hk, hv, name):
    t = p.shape[0]
    nblk = t // GB
    scale = hk ** -0.5
    per = GB // CHUNK

    def rev(i):
        return nblk - 1 - i

    def body(q_ref, k_ref, v_ref, r_ref, lr_ref, wa_ref, ba_ref, gn_ref, o_ref, st_ref, sp_ref, dy_ref,
             dq_ref, dk_ref, dv_ref, dr_ref, dg_ref, dgn_ref, carry):
        h, i = pl.program_id(0), pl.program_id(1)

        @pl.when(i == 0)
        def _():
            carry[...] = jnp.zeros_like(carry)

        @pl.when((i == 0) & (h == 0))
        def _():
            dgn_ref[...] = jnp.zeros_like(dgn_ref)

        gnv = gn_ref[...]
        for c in reversed(range(per)):
            rows = pl.ds(c * CHUNK, CHUNK)
            rv = r_ref[rows, :].astype(F32)
            sg = _sigmoid(rv)
            dyv = dy_ref[rows, :].astype(F32)
            ov = o_ref[rows, :]
            rinv = lax.rsqrt(jnp.mean(ov * ov, axis=-1, keepdims=True) + RMS_EPS)
            dn = dyv * (rv * sg)
            dr_ref[rows, :] = (dyv * (ov * rinv * gnv) * (sg * (1.0 + rv * (1.0 - sg)))).astype(BF16)
            dgn_ref[...] += _colsum(dn * ov * rinv)
            dxh = dn * gnv
            do = rinv * dxh - ov * (rinv * rinv * rinv) * jnp.mean(dxh * ov, axis=-1, keepdims=True)
            dob = (do * scale).astype(BF16)
            qv, kv, vv = q_ref[rows, :], k_ref[rows, :], v_ref[rows, :]
            dq_ref[rows, :] = _dot(dob, st_ref[c].astype(BF16)).astype(BF16)
            dst = carry[...] + _dot_tn(dob, qv)
            if c > 0:
                prev = st_ref[c - 1]
            else:
                prev = jnp.where(i == nblk - 1, 0.0, sp_ref[0])
            ddecay = _colsum(dst * prev)
            la, y = _gate(lr_ref[rows, :], wa_ref[...], ba_ref[...])
            w, decay = _decays(la)
            kf = kv.astype(F32)
            kdec = (kf * w).astype(BF16)
            dstb = dst.astype(BF16)
            dkdec = _dot(vv, dstb)
            dv_ref[rows, :] = _dot_nt(kdec, dstb).astype(BF16)
            dk_ref[rows, :] = (dkdec * w).astype(BF16)
            e = dkdec * kf * w
            dla = _prefix_sums(e, strict=True) + ddecay * decay
            dg_ref[rows, :] = dla * (1.0 / GATE_TAU) * _sigmoid(-y)
            carry[...] = decay * dst

    per_head = lambda width: pl.BlockSpec((GB, width), lambda h, i: (rev(i), h))
    return pl.pallas_call(
        body, name=name,
        out_shape=[_sds((t, nh * hk), BF16), _sds((t, nh * hk), BF16), _sds((t, nh * hv), BF16),
                   _sds((t, nh * hv), BF16), _sds((t, nh * hk), F32), _sds((1, hv), F32)],
        grid=(nh, nblk),
        in_specs=_gla_specs(cols, hk, hv, rev)
        + [pl.BlockSpec((LANE, hk), lambda h, i: (0, h)), pl.BlockSpec((1, hk), lambda h, i: (0, h)),
           pl.BlockSpec((1, hv), lambda h, i: (0, 0)), per_head(hv),
           pl.BlockSpec((None, per, hv, hk), lambda h, i: (h, rev(i), 0, 0)),
           pl.BlockSpec((None, 1, hv, hk), lambda h, i: (h, jnp.maximum(rev(i) * per - 1, 0), 0, 0)),
           per_head(hv)],
        out_specs=[per_head(hk), per_head(hk), per_head(hv), per_head(hv), per_head(hk),
                   pl.BlockSpec((1, hv), lambda h, i: (0, 0))],
        scratch_shapes=[pltpu.VMEM((hv, hk), F32)], compiler_params=_params("arbitrary", "arbitrary"),
    )(p, p, p, p, p, wa, ba, gn, o, states, states, dy)


def gate_bwd(p, lr_col, dg, wa, name):
    t, kd = dg.shape
    tm = _row_tile(t, 512, 16)

    def body(lr_ref, dg_ref, wa_ref, dlr_ref, dwa_ref, dba_ref):
        @pl.when(pl.program_id(0) == 0)
        def _():
            dwa_ref[...] = jnp.zeros_like(dwa_ref)
            dba_ref[...] = jnp.zeros_like(dba_ref)

        g = dg_ref[...]
        gb = g.astype(BF16)
        dlr_ref[...] = _dot_nt(gb, wa_ref[...]).astype(BF16)
        dwa_ref[...] += _dot_tn(lr_ref[...], gb)
        dba_ref[...] += _colsum(g)

    return pl.pallas_call(
        body, name=name, out_shape=[_sds((t, LANE), BF16), _sds((LANE, kd), F32), _sds((1, kd), F32)],
        grid=(t // tm,),
        in_specs=[pl.BlockSpec((tm, LANE), lambda i: (i, lr_col // LANE)), pl.BlockSpec((tm, kd), lambda i: (i, 0)),
                  pl.BlockSpec((LANE, kd), lambda i: (0, 0))],
        out_specs=[pl.BlockSpec((tm, LANE), lambda i: (i, 0)), pl.BlockSpec((LANE, kd), lambda i: (0, 0)),
                   pl.BlockSpec((1, kd), lambda i: (0, 0))],
        compiler_params=_params("arbitrary"),
    )(p, dg, wa)


def proj_merge(ya, yb, wa3, wb3, p, ga_col, gb_col, name):
    t, kd = ya.shape
    nb, _, bw = wa3.shape
    tm = _row_tile(t, 1024, 16)

    def body(ya_ref, yb_ref, wa_ref, wb_ref, ga_ref, gb_ref, pa_ref, pb_ref, mg_ref):
        pa = _dot(ya_ref[...], wa_ref[...])
        pb = _dot(yb_ref[...], wb_ref[...])
        pa_ref[...] = pa.astype(BF16)
        pb_ref[...] = pb.astype(BF16)
        mg_ref[...] = (_sigmoid(ga_ref[...].astype(F32)) * pa + _sigmoid(gb_ref[...].astype(F32)) * pb).astype(BF16)

    act = pl.BlockSpec((tm, kd), lambda j, i: (i, 0))
    wsp = pl.BlockSpec((None, kd, bw), lambda j, i: (j, 0, 0))
    out = pl.BlockSpec((tm, bw), lambda j, i: (i, j))
    return pl.pallas_call(
        body, name=name, out_shape=[_sds((t, nb * bw), BF16)] * 3, grid=(nb, t // tm),
        in_specs=[act, act, wsp, wsp, pl.BlockSpec((tm, bw), lambda j, i: (i, ga_col // bw + j)),
                  pl.BlockSpec((tm, bw), lambda j, i: (i, gb_col // bw + j))],
        out_specs=[out] * 3, compiler_params=_params("parallel", "parallel"),
    )(ya, yb, wa3, wb3, p, p)


def merge_bwd(dm, w, p, ga_col, gb_col, pa, pb, name):
    t, d = dm.shape
    n = w.shape[0]
    tn = _row_tile(n, 512, LANE)
    tm = _row_tile(t, 512, 16)

    def body(dm_ref, w_ref, ga_ref, gb_ref, pa_ref, pb_ref, dpa_ref, dpb_ref, dga_ref, dgb_ref):
        dmg = _dot_nt(dm_ref[...], w_ref[...])
        sa = _sigmoid(ga_ref[...].astype(F32))
        sb = _sigmoid(gb_ref[...].astype(F32))
        dpa_ref[...] = (dmg * sa).astype(BF16)
        dpb_ref[...] = (dmg * sb).astype(BF16)
        dga_ref[...] = (dmg * pa_ref[...].astype(F32) * sa * (1.0 - sa)).astype(BF16)
        dgb_ref[...] = (dmg * pb_ref[...].astype(F32) * sb * (1.0 - sb)).astype(BF16)

    out = pl.BlockSpec((tm, tn), lambda j, i: (i, j))
    return pl.pallas_call(
        body, name=name, out_shape=[_sds((t, n), BF16)] * 4, grid=(n // tn, t // tm),
        in_specs=[pl.BlockSpec((tm, d), lambda j, i: (i, 0)), pl.BlockSpec((tn, d), lambda j, i: (j, 0)),
                  pl.BlockSpec((tm, tn), lambda j, i: (i, ga_col // tn + j)),
                  pl.BlockSpec((tm, tn), lambda j, i: (i, gb_col // tn + j)), out, out],
        out_specs=[out] * 4, compiler_params=_params("parallel", "parallel"),
    )(dm, w, p, p, pa, pb)


def rel_bias_grad(skew, clip_map, name):
    nh, _, jd = skew.shape
    n_rel = clip_map.shape[1]

    def body(s_ref, c_ref, o_ref):
        sums = jnp.concatenate([_colsum(s_ref[h]) for h in range(nh)], axis=0)
        o_ref[...] = jnp.dot(sums, c_ref[...], preferred_element_type=F32, precision=HI)

    return pl.pallas_call(
        body, name=name, out_shape=_sds((nh, n_rel), F32), in_specs=[VMEM_SPEC, VMEM_SPEC], out_specs=VMEM_SPEC,
        compiler_params=pltpu.CompilerParams(vmem_limit_bytes=VMEM_LIMIT),
    )(skew, clip_map)


MIX_BLOCK = 9 * LANE
MIX_PARTS = 3


def mix_layout(d, a_width, bk, bv):
    main = 3 * a_width + 2 * bk + 2 * bv
    cols = {"qa": 0, "ka": a_width, "va": 2 * a_width, "qb": 3 * a_width, "kb": 3 * a_width + bk,
            "vb": 3 * a_width + 2 * bk, "rb": 3 * a_width + 2 * bk + bv, "ga": main, "gb": main + d,
            "lr": main + 2 * d}
    total = main + 2 * d + LANE
    assert total % MIX_BLOCK == 0
    return cols, main, total


def _mix_pieces(per, main, rank, d):
    out = []
    for lo, hi in ((0, main), (main + rank, main + rank + 2 * d), (main, main + rank)):
        while lo < hi:
            cut = min(hi, (lo // per + 1) * per)
            out.append((lo, cut))
            lo = cut
    return out


def mix_weight_in(g3, main, rank):
    d = g3.shape[2]
    flat = g3.reshape(-1, d)
    return jnp.concatenate([flat[:main], flat[main + rank:], flat[main:main + rank],
                            jnp.zeros((LANE - rank, d), g3.dtype)], axis=0)


def mix_weight_grad_out(gt, main, rank, per):
    d = gt.shape[1]
    blocks = [[] for _ in range(N_DEV)]
    pos = 0
    for lo, hi in _mix_pieces(per, main, rank, d):
        blocks[lo // per].append((lo, gt[pos:pos + hi - lo]))
        pos += hi - lo
    return jnp.stack([jnp.concatenate([x for _, x in sorted(b, key=lambda e: e[0])], axis=0) for b in blocks])


def ffn_forward(h, sh, sc, g, w_in3, w_out_of, ln_g, ln_b, tag):
    a, b, s = ffn_in(h, sh, sc, w_in3, f"{tag}_in")
    w_out = w_out_of(s)
    f, z, hout = out_ln(s, w_out, h, g, ln_g, ln_b, 0.5, f"{tag}_out")
    return hout, (h, a, b, s, f, z), w_out


def ffn_backward_weights(dh, saved, sh, sc, g, w_in3, w_out, ln_g, tag, target=None):
    hin, a, b, s, f, z = saved
    t, d = hin.shape
    nb, _, bw = w_in3.shape
    half = nb // 2
    fdim = w_out.shape[0]
    res = ln_bwd(dh, z, f, ln_g, g, 0.5, f"{tag}_ln_bwd", target=target)
    dz, df, dln_g, dln_b, dg = res[:5]
    dab = ffn_bwd_act(df, w_out, a, b, f"{tag}_act_bwd")
    tk = _row_tile(t, 1024, 16)
    dw_out = matmul_tn(f"{tag}_dwout", s, (tk, bw), lambda n, k: (k, n), df, (tk, d), lambda n, k: (k, 0),
                       _sds((fdim, d), BF16), (bw, d), lambda n, k: (n, 0), fdim // bw)
    dw_in = matmul_tn(f"{tag}_dwin", hin, (tk, d), lambda n, k: (k, 0), dab, (None, tk, bw),
                      lambda n, k: (n // half, k, n % half), _sds((nb, d, bw), BF16), (None, d, bw),
                      lambda n, k: (n, 0, 0), nb, mod=(sh, sc))
    grads = dict(w_in=dw_in, w_out=dw_out.reshape(N_DEV, fdim // N_DEV, d), ln_g=dln_g, ln_b=dln_b, g=dg)
    return (dab, dz), grads, (res[5] if target is not None else None)


def ffn_backward_input(carry, saved, sc, w_in3, tag, after=None):
    dab, dz = carry
    hin = saved[0]
    t, d = hin.shape
    nb, _, bw = w_in3.shape
    half = nb // 2
    fdim = half * bw
    tm = _row_tile(t, 256, 16)

    def contract(dy_ref, w_ref):
        return sum(_dot_nt(dy_ref[:, j * bw:(j + 1) * bw], w_ref[j]) for j in range(half))

    order = [] if after is None else [after]

    def first(dy_ref, w_ref, *refs):
        refs[-1][...] = contract(dy_ref, w_ref)

    def second(dy_ref, w_ref, part_ref, dz_ref, hin_ref, sc_ref, o_ref, dsc_ref, dsh_ref):
        @pl.when(pl.program_id(0) == 0)
        def _():
            dsc_ref[...] = jnp.zeros_like(dsc_ref)
            dsh_ref[...] = jnp.zeros_like(dsh_ref)

        du = part_ref[...] + contract(dy_ref, w_ref)
        o_ref[...] = ALPHA * dz_ref[...] + du * (1.0 + sc_ref[...])
        dsc_ref[...] += _colsum(du * hin_ref[...])
        dsh_ref[...] += _colsum(du)

    def specs(which):
        return [pl.BlockSpec((None, tm, fdim), lambda i: (which, i, 0)),
                pl.BlockSpec((half, d, bw), lambda i: (which, 0, 0), pipeline_mode=pl.Buffered(1))]

    row = pl.BlockSpec((tm, d), lambda i: (i, 0))
    vec = pl.BlockSpec((1, d), lambda i: (0, 0))
    part = pl.pallas_call(
        first, name=f"{tag}_du_a", out_shape=_sds((t, d), F32), grid=(t // tm,),
        in_specs=specs(0) + [ANY] * len(order), out_specs=row, compiler_params=_params("parallel"),
    )(dab, w_in3, *order)
    return pl.pallas_call(
        second, name=f"{tag}_du_b", out_shape=[_sds((t, d), F32), _sds((1, d), F32), _sds((1, d), F32)],
        grid=(t // tm,), in_specs=specs(1) + [row, row, row, vec], out_specs=[row, vec, vec],
        compiler_params=_params("arbitrary"),
    )(dab, w_in3, part, dz, hin, sc)


def _after(v, token):
    return v if token is None else v + token[:1, :1]


def local_step(x, target, mod, weights_of, grads_ready, grads_sent, rel_bias, w_alpha2, b_alpha, gla_norm_g, lns,
               bias=None, weights_early=None):
    t, d = x.shape
    sh1, sc1, g1, sh2, sc2, g2, sh3, sc3, g3 = [mod[i:i + 1] for i in range(N_MOD)]
    ln1_g, ln1_b, ln2_g, ln2_b, ln3_g, ln3_b = lns
    n_heads_a, n_rel = rel_bias.shape
    rank, bk = w_alpha2.shape
    hv = gla_norm_g.shape[1]

    w1 = weights_of("ffn1_in", mod)
    h1, saved1, w1["out"] = ffn_forward(x, sh1, sc1, g1, w1["in"], lambda s: weights_of("ffn1_out", s)["out"],
                                        ln1_g, ln1_b, "ffn1")
    wm = weights_of("mix", h1)
    a_width = wm["proj_a"].shape[1]
    bv = wm["proj_b"].shape[1]
    nh_b = bv // hv
    hk = bk // nh_b
    cols, main, total = mix_layout(d, a_width, bk, bv)
    w_mix = mix_weight_in(wm["in_t"], main, rank)
    p = mod_matmul(h1, sh2, sc2, w_mix, MIX_BLOCK, "mix_in")
    bias = bias_table(rel_bias) if bias is None else bias
    dh = a_width // n_heads_a
    assert PAIR * dh == LANE
    ya, lse = attn_fwd(p, cols, bias, dh, "attn_fwd")
    token = None if weights_early is None else weights_early("ffn2", ya)
    b_alpha = _after(b_alpha, token)
    wa_pad = jnp.zeros((LANE, bk), BF16).at[:rank].set(w_alpha2.astype(BF16))
    o_b, yb, states = gla_fwd(p, cols, wa_pad, b_alpha, gla_norm_g, nh_b, hk, hv, "gla_fwd")
    pa, pb, merged = proj_merge(ya, yb, wm["proj_a"], wm["proj_b"], p, cols["ga"], cols["gb"], "proj_merge")
    m, z2, h2 = out_ln(merged, wm["out"], h1, g2, ln2_g, ln2_b, 1.0, "mix_out")
    w3 = weights_of("ffn2", h2)
    h3, saved3, _ = ffn_forward(h2, sh3, sc3, g3, w3["in"], lambda s: w3["out"], ln3_g, ln3_b, "ffn2")

    carry3, gr3, loss = ffn_backward_weights(h3, saved3, sh3, sc3, g3, w3["in"], w3["out"], ln3_g, "ffn2",
                                             target=target)
    token = grads_ready("ffn2", dict(ffn2_in=gr3["w_in"], ffn2_out=gr3["w_out"]))
    dh2, dsc3, dsh3 = ffn_backward_input(carry3, saved3, sc3, w3["in"], "ffn2", after=token)
    token = grads_sent("ffn2", dh2)
    dz2, dm, dln2_g, dln2_b, dg2 = ln_bwd(dh2, z2, m, _after(ln2_g, token), g2, 1.0, "mix_ln_bwd")
    dpa, dpb, dga, dgb = merge_bwd(dm, wm["out"], p, cols["ga"], cols["gb"], pa, pb, "merge_bwd")
    tk = _row_tile(t, 1024, 16)
    dw_mix_out = matmul_tn("mix_dwout", merged, (tk, 512), lambda n, k: (k, n), dm, (tk, d), lambda n, k: (k, 0),
                           _sds((d, d), BF16), (512, d), lambda n, k: (n, 0), d // 512)
    pbw = wm["proj_a"].shape[2]
    dya = block_input_grad(dpa, wm["proj_a"], "proj_a_dy")
    dyb = block_input_grad(dpb, wm["proj_b"], "proj_b_dy")
    dw_pa = matmul_tn("proj_a_dw", ya, (tk, a_width), lambda n, k: (k, 0), dpa, (tk, pbw), lambda n, k: (k, n),
                      _sds((N_DEV, a_width, pbw), BF16), (None, a_width, pbw), lambda n, k: (n, 0, 0), N_DEV)
    dw_pb = matmul_tn("proj_b_dw", yb, (tk, bv), lambda n, k: (k, 0), dpb, (tk, pbw), lambda n, k: (k, n),
                      _sds((N_DEV, bv, pbw), BF16), (None, bv, pbw), lambda n, k: (n, 0, 0), N_DEV)
    dqb, dkb, dvb, drb, dgate, dgn = gla_bwd(p, cols, wa_pad, b_alpha, gla_norm_g, o_b, states, dyb,
                                             nh_b, hk, hv, "gla_bwd")
    dlr, dwa_pad, dba = gate_bwd(p, cols["lr"], dgate, wa_pad, "gate_bwd")
    dqa, dka, dva, dbias = attn_bwd(p, cols, bias, lse, dya, dh, "attn_bwd")
    d_rel = rel_bias_grad(bias_grad_skew(dbias), jnp.asarray(bias_clip_map(n_rel)), "rel_bias_grad")
    dp = jnp.concatenate([dqa, dka, dva, dqb, dkb, dvb, drb,
                          dga, dgb, dlr], axis=1)
    dw_mix_t = matmul_tn("mix_dwin", dp, (tk, MIX_BLOCK), lambda n, k: (k, n), h1, (tk, d), lambda n, k: (k, 0),
                         _sds((total, d), BF16), (MIX_BLOCK, d), lambda n, k: (n, 0), total // MIX_BLOCK,
                         mod=(sh2, sc2), mod_b=True)
    dw_mix_in = mix_weight_grad_out(dw_mix_t, main, rank, wm["in_t"].shape[1])
    token = grads_ready("mix", dict(mix_in=dw_mix_in, proj_a=dw_pa, proj_b=dw_pb,
                                    mix_out=dw_mix_out.reshape(N_DEV, d // N_DEV, d)))
    dh1, dsc2, dsh2 = row_input_grad(dp, w_mix, MIX_PARTS, dz2, h1, sc2, "mix_du", after=token)
    token = grads_sent("mix", dh1)
    carry1, gr1, _ = ffn_backward_weights(dh1, saved1, sh1, sc1, g1, w1["in"], w1["out"], _after(ln1_g, token),
                                          "ffn1")
    token = grads_ready("ffn1", dict(ffn1_in=gr1["w_in"], ffn1_out=gr1["w_out"]))
    dx, dsc1, dsh1 = ffn_backward_input(carry1, saved1, sc1, w1["in"], "ffn1", after=token)

    dmod = [dsh1, dsc1, gr1["g"], dsh2, dsc2, dg2, dsh3, dsc3, gr3["g"]]
    small = dict(ln1_g=gr1["ln_g"], ln1_b=gr1["ln_b"], ln2_g=dln2_g, ln2_b=dln2_b, ln3_g=gr3["ln_g"],
                 ln3_b=gr3["ln_b"], b_alpha=dba, gla_norm_g=dgn, w_alpha2=dwa_pad[:rank], rel_bias=d_rel)
    return loss, dx, dmod, small


GROUPS = dict(ffn1=("ffn1_in", "ffn1_out"), mix=("mix_in", "proj_a", "proj_b", "mix_out"),
              ffn2=("ffn2_in", "ffn2_out"))
GATHERS = dict(ffn1_in=("ffn1_in",), ffn1_out=("ffn1_out",), mix=GROUPS["mix"], ffn2=GROUPS["ffn2"])
SMALL_REPLICATED = ("b_ada", "ln1_g", "ln1_b", "ln2_g", "ln2_b", "ln3_g", "ln3_b", "b_alpha", "gla_norm_g")
SMALL_SHARDED = ("rel_bias", "w_alpha2")
WEIGHT_ORDER = ("w_ada", "b_ada", "ffn1_w_in", "ffn1_w_out", "ln1_g", "ln1_b", "w_mix_in", "rel_bias", "w_alpha2",
                "b_alpha", "gla_norm_g", "w_proj_a", "w_proj_b", "w_mix_out", "ln2_g", "ln2_b", "ffn2_w_in",
                "ffn2_w_out", "ln3_g", "ln3_b")
BIG_NAME = dict(ffn1_in="ffn1_w_in", ffn1_out="ffn1_w_out", mix_in="w_mix_in", proj_a="w_proj_a",
                proj_b="w_proj_b", mix_out="w_mix_out", ffn2_in="ffn2_w_in", ffn2_out="ffn2_w_out")


def kernel(x, c, w_ada, b_ada, ffn1_w_in, ffn1_w_out, ln1_g, ln1_b, w_mix_in, rel_bias, w_alpha2, b_alpha, gla_norm_g, w_proj_a, w_proj_b, w_mix_out, ln2_g, ln2_b, ffn2_w_in, ffn2_w_out, ln3_g, ln3_b, loss_target, m_w_ada, m_b_ada, m_ffn1_w_in, m_ffn1_w_out, m_ln1_g, m_ln1_b, m_w_mix_in, m_rel_bias, m_w_alpha2, m_b_alpha, m_gla_norm_g, m_w_proj_a, m_w_proj_b, m_w_mix_out, m_ln2_g, m_ln2_b, m_ffn2_w_in, m_ffn2_w_out, m_ln3_g, m_ln3_b, v_w_ada, v_b_ada, v_ffn1_w_in, v_ffn1_w_out, v_ln1_g, v_ln1_b, v_w_mix_in, v_rel_bias, v_w_alpha2, v_b_alpha, v_gla_norm_g, v_w_proj_a, v_w_proj_b, v_w_mix_out, v_ln2_g, v_ln2_b, v_ffn2_w_in, v_ffn2_w_out, v_ln3_g, v_ln3_b):
    env = dict(locals())
    w = {n: env[n] for n in WEIGHT_ORDER}
    mom = {n: env["m_" + n] for n in WEIGHT_ORDER}
    var = {n: env["v_" + n] for n in WEIGHT_ORDER}
    me = _me()
    dev = _lin(me)
    core = jnp.reshape(me[2], (1,)).astype(jnp.int32)
    chip = jnp.reshape(2 * me[0] + me[1], (1,)).astype(jnp.int32)
    d = x.shape[-1]

    def shard(n):
        s = w[BIG_NAME[n]][0].astype(BF16)
        return s.T if n == "mix_in" else s

    dev_idx = jnp.reshape(dev, (1,)).astype(jnp.int32)
    started = {}

    def start(grp, after):
        shards = [shard(n) for n in GATHERS[grp]]
        lands = [place_own(dev_idx, s, f"place_own_{n}") for n, s in zip(GATHERS[grp], shards)]
        started[grp] = gather_start(shards, lands, after, f"gather_start_{grp}")
        return started[grp][-1]

    small_w = all_gather_small(jnp.concatenate([rel_bias[0], w_alpha2[0]], axis=1), "gather_small_w")
    n_rel_cols = rel_bias.shape[-1]
    rel_full = small_w[:, :, :n_rel_cols].transpose(1, 0, 2).reshape(small_w.shape[1], -1)
    wa2_full = small_w[:, :, n_rel_cols:].transpose(1, 0, 2).reshape(small_w.shape[1], -1)

    first, *rest = list(GATHERS)
    order = start(first, small_w[0, :1, :1])
    order, w, mom, var = lax.optimization_barrier((order, w, mom, var))

    ada_cols = w_ada.shape[-1]
    bias = bias_table(rel_full)
    c_all = all_gather_small(c, "gather_c", after=(order, bias))[:, 0, :]
    b_cols = lax.dynamic_slice_in_dim(b_ada, dev * ada_cols, ada_cols, axis=1)
    mod_cols = adaln_cols(c_all, w_ada[0], b_cols, "adaln_cols")
    mod_all = all_gather_small(mod_cols, "gather_mod")
    mod = lax.dynamic_index_in_dim(mod_all, dev, axis=1, keepdims=False).reshape(N_MOD, d)

    order = mod_all[0, :1, :1]
    for grp in rest:
        order = start(grp, order)
    mod = _after(mod, order)

    forwards = {}

    def weights_early(grp, after):
        _, zones = gather_wait(started[grp], after, f"gather_wait_{grp}")
        forwards[grp] = forward_start(zones, f"forward_start_{grp}")
        return forwards[grp][-1]

    def weights_of(grp, after):
        if grp in forwards:
            arrays = forward_wait(forwards[grp], after, f"forward_wait_{grp}")
        else:
            _, zones = gather_wait(started[grp], after, f"gather_wait_{grp}")
            arrays = gather_forward(zones, f"gather_forward_{grp}")
        full = dict(zip(GATHERS[grp], arrays))
        if grp == "mix":
            return dict(in_t=full["mix_in"], proj_a=full["proj_a"], proj_b=full["proj_b"],
                        out=full["mix_out"].reshape(-1, d))
        return {"in" if n.endswith("_in") else "out": v if n.endswith("_in") else v.reshape(-1, d)
                for n, v in full.items()}

    pairs, exchanges = {}, {}
    last = list(GROUPS)[0]

    def to_chips(grp, grads, got):
        sums = [pair_add(core, grads[n], g, f"grad_pair_add_{n}") for n, g in zip(GROUPS[grp], got)]
        exchanges[grp] = exchange_start(sums, chip_routes, f"grad_chip_start_{grp}")
        return exchanges[grp][-1]

    def grads_ready(grp, grads):
        if grp == last:
            return to_chips(grp, grads, pair_exchange([grads[n] for n in GROUPS[grp]], f"grad_pair_exchange_{grp}"))
        pairs[grp] = exchange_start([grads[n] for n in GROUPS[grp]], pair_routes, f"grad_pair_start_{grp}")
        return pairs[grp][-1]

    def grads_sent(grp, after):
        if grp == last:
            return None
        sent, got = exchange_wait(pairs[grp], pair_routes, after, f"grad_pair_wait_{grp}")
        return to_chips(grp, dict(zip(GROUPS[grp], sent)), got)

    lns = [ln1_g, ln1_b, ln2_g, ln2_b, ln3_g, ln3_b]
    loss, dx, dmod, small = local_step(x[0], loss_target[0], mod, weights_of, grads_ready, grads_sent, rel_full,
                                       wa2_full, b_alpha, gla_norm_g, lns, bias=bias, weights_early=weights_early)

    out = {}

    def finish(grp, after):
        sums, recv = exchange_wait(exchanges[grp], chip_routes, after, f"grad_chip_wait_{grp}")
        for n, hsum, r in zip(GROUPS[grp], sums, recv):
            full = BIG_NAME[n]
            if n == "mix_in":
                g = owned_sum(chip, hsum, r, f"owned_sum_{n}").T
                res_n = adamw_sum(g[None], w[full][0], mom[full][0], var[full][0], f"adamw_{n}")
            else:
                res_n = adamw_owned(chip, hsum, r, w[full][0], mom[full][0], var[full][0], f"adamw_{n}")
            out[full] = [o[None] for o in res_n]
            after = res_n[0]
        return after

    order = dx
    for grp in reversed(list(GROUPS)[1:]):
        order = finish(grp, order)

    pieces = (list(dmod) + [small[n].reshape(1, -1) for n in SMALL_REPLICATED[1:] + SMALL_SHARDED] + [loss])
    parts = all_gather_rows(pieces, order, "gather_small_grads")
    loss = jnp.sum(parts[:, 0, parts.shape[2] - loss.shape[1]])
    n_mod = N_MOD * d
    dmod_all = parts[:, 0, :n_mod]
    g_w_ada = adaln_wgrad(c_all, lax.dynamic_slice_in_dim(dmod_all, dev * ada_cols, ada_cols, axis=1), "adaln_wgrad")
    out["w_ada"] = [o[None] for o in adamw_sum(g_w_ada[None], w_ada[0], m_w_ada[0], v_w_ada[0], "adamw_w_ada")]

    sources, where, off = [parts], [], 0
    for n in SMALL_REPLICATED:
        where.append((0, off))
        off += w[n].size
    for n in SMALL_SHARDED:
        rows, cols_local = w[n].shape[1], w[n].shape[2]
        full_part = parts[:, 0, off:off + rows * cols_local * N_DEV].reshape(N_DEV, rows, cols_local * N_DEV)
        mine = lax.dynamic_slice_in_dim(full_part, dev * cols_local, cols_local, axis=2)
        where.append((len(sources), 0))
        sources.append(mine.reshape(N_DEV, 1, rows * cols_local))
        off += rows * cols_local * N_DEV
    names = SMALL_REPLICATED + SMALL_SHARDED
    res = adamw_rows(sources, where, *[[src[n].reshape(1, -1) for n in names] for src in (w, mom, var)],
                     "adamw_small")
    for n, res_n in zip(names, res):
        out[n] = [r.reshape(w[n].shape) for r in res_n]

    finish(last, res[0][0])

    flat = [loss, dx[None]]
    for k in range(4):
        flat += [out[n][k] for n in WEIGHT_ORDER]
    return tuple(flat)
```
